```python
import jax, jax.numpy as jnp
from jax import lax
import numpy as np

D_MODEL = 1024
BATCH = 8
SEQ = 2048
DEPTH = 1

N_MEM = 256
HEAD_DIM = 64
MIX_WIDTH = D_MODEL
NSA_HEADS = (MIX_WIDTH // 2) // HEAD_DIM
NSA_KV_GROUPS = 2
NSA_HPG = NSA_HEADS // NSA_KV_GROUPS
NSA_WIDTH = NSA_HEADS * HEAD_DIM
NSA_KV_WIDTH = NSA_KV_GROUPS * HEAD_DIM
CMP_BLOCK = 32
CMP_STRIDE = 16
CMP_HIDDEN = 2 * HEAD_DIM
SEL_BLOCK = 64
SEL_TOPK = 8
WINDOW = 512
Q_BLOCK = 128
RET_HEADS = (MIX_WIDTH - NSA_WIDTH) // HEAD_DIM
RET_WIDTH = RET_HEADS * HEAD_DIM
RET_CHUNK = 128
ROPE_BASE = 10000.0
PROJ_WIDTH = NSA_WIDTH + 6 * NSA_KV_WIDTH + 3 * NSA_HEADS + 4 * RET_WIDTH
PROJ_SPLIT_POINTS = (NSA_WIDTH, NSA_WIDTH + 6 * NSA_KV_WIDTH,
                     NSA_WIDTH + 6 * NSA_KV_WIDTH + 3 * NSA_HEADS)
MEM_HEADS = 4
MEM_WIDTH = MEM_HEADS * HEAD_DIM
N_GROUPS = 4
EXPERTS_PER_GROUP = 8
N_EXPERTS = N_GROUPS * EXPERTS_PER_GROUP
TOP_K_IN_GROUP = 2
EXPERT_FF = D_MODEL // 4
MOE_BLOCK = 128

EPS = 1e-6
NEG_INF = -1e30
FORCE_SCORE = 1e9

kernel_name = 'hymba_nsa_retention_hmoe_layer'


def rms_norm(x, g):
    xf = x.astype(jnp.float32)
    y = xf * lax.rsqrt(jnp.mean(xf * xf, axis=-1, keepdims=True) + EPS)
    return (y * g.astype(jnp.float32)).astype(x.dtype)


def masked_softmax(s, mask):
    s = jnp.where(mask, s.astype(jnp.float32), NEG_INF)
    m = jnp.max(s, axis=-1, keepdims=True)
    p = jnp.exp(s - m) * mask
    return p / jnp.maximum(jnp.sum(p, axis=-1, keepdims=True), 1e-30)


def compress_blocks(kv, pos, w1, w2):
    B, S, G, Dh = kv.shape
    n_c = (S - CMP_BLOCK) // CMP_STRIDE + 1
    idx = np.arange(n_c)[:, None] * CMP_STRIDE + np.arange(CMP_BLOCK)[None, :]
    blk = kv[:, idx] + pos[None, None, :, None, :]
    blk = jnp.moveaxis(blk, 3, 2).reshape(B, n_c, G, CMP_BLOCK * Dh)
    return jax.nn.silu(blk @ w1) @ w2


def sel_from_cmp_matrix(n_c, n_sel):
    c0 = np.arange(n_c) * CMP_STRIDE
    s0 = np.arange(n_sel) * SEL_BLOCK
    ov = np.minimum(c0[None, :] + CMP_BLOCK, s0[:, None] + SEL_BLOCK) - np.maximum(c0[None, :], s0[:, None])
    return (np.clip(ov, 0, None) / CMP_BLOCK).astype(np.float32)


def nsa_mixer(q, kc, vc, ks, vs, kw, vw, gates, q_norm, kc_norm, ks_norm, kw_norm,
              pos_k, pos_v, ck_w1, ck_w2, cv_w1, cv_w2):
    B, S = q.shape[:2]
    G, R, Dh = NSA_KV_GROUPS, NSA_HPG, HEAD_DIM
    q = rms_norm(q.reshape(B, S, G, R, Dh), q_norm) * (Dh ** -0.5)
    t_pos = jnp.arange(S)
    kcmp = rms_norm(compress_blocks(kc, pos_k, ck_w1, ck_w2), kc_norm)
    vcmp = compress_blocks(vc, pos_v, cv_w1, cv_w2)
    n_c = kcmp.shape[1]
    c_end = jnp.arange(n_c) * CMP_STRIDE + CMP_BLOCK - 1
    cmask = c_end[None, :] <= t_pos[:, None]
    s_c = jnp.einsum('bsgrd,bcgd->bsgrc', q, kcmp)
    p_c = masked_softmax(s_c, cmask[None, :, None, None, :])
    o_cmp = jnp.einsum('bsgrc,bcgd->bsgrd', p_c.astype(vcmp.dtype), vcmp)
    n_sel = S // SEL_BLOCK
    m_sc = jnp.asarray(sel_from_cmp_matrix(n_c, n_sel))
    imp = jnp.einsum('bsgrc,jc->bsgj', p_c, m_sc)
    blk_id = jnp.arange(n_sel)[None, :]
    cur = (t_pos // SEL_BLOCK)[:, None]
    forced = (blk_id == 0) | (blk_id == cur) | (blk_id == cur - 1)
    future = blk_id > cur
    imp = jnp.where(forced[None, :, None, :], FORCE_SCORE,
                    jnp.where(future[None, :, None, :], NEG_INF, imp))
    k_top = min(SEL_TOPK, n_sel)
    _, sel_idx = lax.top_k(imp, k_top)
    ks = rms_norm(ks, ks_norm)
    kw = rms_norm(kw, kw_norm)
    k_blocks = ks.reshape(B, n_sel, SEL_BLOCK, G, Dh).transpose(0, 3, 1, 2, 4)
    v_blocks = vs.reshape(B, n_sel, SEL_BLOCK, G, Dh).transpose(0, 3, 1, 2, 4)
    kw_pad = jnp.pad(kw, ((0, 0), (WINDOW, 0), (0, 0), (0, 0)))
    vw_pad = jnp.pad(vw, ((0, 0), (WINDOW, 0), (0, 0), (0, 0)))
    nq = S // Q_BLOCK
    q_blocks = q.reshape(B, nq, Q_BLOCK, G, R, Dh).transpose(1, 0, 2, 3, 4, 5)
    idx_blocks = sel_idx.reshape(B, nq, Q_BLOCK, G, k_top).transpose(1, 0, 3, 2, 4)
    gather = jax.vmap(jax.vmap(lambda kb, ib: kb[ib]))

    def block_attn(args):
        i, qb, ib = args
        t = i * Q_BLOCK + jnp.arange(Q_BLOCK)
        ksel = gather(k_blocks, ib).reshape(B, G, Q_BLOCK, k_top * SEL_BLOCK, Dh)
        vsel = gather(v_blocks, ib).reshape(B, G, Q_BLOCK, k_top * SEL_BLOCK, Dh)
        kpos = (ib[..., None] * SEL_BLOCK + jnp.arange(SEL_BLOCK)).reshape(B, G, Q_BLOCK, k_top * SEL_BLOCK)
        smask = (kpos <= t[None, None, :, None]).transpose(0, 2, 1, 3)[:, :, :, None, :]
        s_s = jnp.einsum('bqgrd,bgqkd->bqgrk', qb, ksel)
        p_s = masked_softmax(s_s, smask)
        o_s = jnp.einsum('bqgrk,bgqkd->bqgrd', p_s.astype(vsel.dtype), vsel)
        kwin = lax.dynamic_slice_in_dim(kw_pad, i * Q_BLOCK, Q_BLOCK + WINDOW, axis=1)
        vwin = lax.dynamic_slice_in_dim(vw_pad, i * Q_BLOCK, Q_BLOCK + WINDOW, axis=1)
        wpos = i * Q_BLOCK - WINDOW + jnp.arange(Q_BLOCK + WINDOW)
        diff = t[:, None] - wpos[None, :]
        wmask = (wpos[None, :] >= 0) & (diff >= 0) & (diff < WINDOW)
        s_w = jnp.einsum('bqgrd,bkgd->bqgrk', qb, kwin)
        p_w = masked_softmax(s_w, wmask[None, :, None, None, :])
        o_w = jnp.einsum('bqgrk,bkgd->bqgrd', p_w.astype(vwin.dtype), vwin)
        return o_s, o_w

    o_sel, o_win = lax.map(block_attn, (jnp.arange(nq), q_blocks, idx_blocks))
    o_sel = o_sel.transpose(1, 0, 2, 3, 4, 5).reshape(B, S, G, R, Dh)
    o_win = o_win.transpose(1, 0, 2, 3, 4, 5).reshape(B, S, G, R, Dh)
    g = jax.nn.sigmoid(gates.astype(jnp.float32)).reshape(B, S, G, R, 3).astype(o_cmp.dtype)
    out = g[..., 0:1] * o_cmp + g[..., 1:2] * o_sel + g[..., 2:3] * o_win
    return out.reshape(B, S, NSA_HEADS, Dh)


def rotary(x, pos):
    half = x.shape[-1] // 2
    inv_freq = ROPE_BASE ** (-jnp.arange(half, dtype=jnp.float32) / half)
    ang = pos.astype(jnp.float32)[:, None] * inv_freq[None, :]
    cos, sin = jnp.cos(ang)[None, :, None, :], jnp.sin(ang)[None, :, None, :]
    x1, x2 = x[..., :half], x[..., half:]
    return jnp.concatenate([x1 * cos - x2 * sin, x1 * sin + x2 * cos], axis=-1)


def retention_mixer(q, k, v, g, out_norm):
    B, S = q.shape[:2]
    H, Dh, C = RET_HEADS, HEAD_DIM, RET_CHUNK
    f32 = jnp.float32
    pos = jnp.arange(S)
    q = rotary(q.reshape(B, S, H, Dh).astype(f32), pos)
    k = rotary(k.reshape(B, S, H, Dh).astype(f32), pos) * (Dh ** -0.5)
    v = v.reshape(B, S, H, Dh).astype(f32)
    log_gamma = jnp.log1p(-jnp.power(2.0, -5.0 - jnp.arange(H, dtype=f32)))
    n_ch = S // C
    i = jnp.arange(C, dtype=f32)
    rel = i[:, None] - i[None, :]
    decay = jnp.where(rel >= 0, jnp.exp(jnp.maximum(rel, 0.0)[None] * log_gamma[:, None, None]), 0.0)
    qc = q.reshape(B, n_ch, C, H, Dh)
    kc = k.reshape(B, n_ch, C, H, Dh)
    vc = v.reshape(B, n_ch, C, H, Dh)
    inner = jnp.einsum('bnihd,bnjhd->bnhij', qc, kc) * decay[None, None]
    o_inner = jnp.einsum('bnhij,bnjhd->bnihd', inner, vc)
    xi = jnp.exp((i + 1.0)[:, None] * log_gamma[None, :])
    zeta = jnp.exp((C - 1.0 - i)[:, None] * log_gamma[None, :])
    gamma_c = jnp.exp(C * log_gamma)

    def step(state, xs):
        qn, kn, vn = xs
        cross = jnp.einsum('bihd,bhde->bihe', qn, state) * xi[None, :, :, None]
        state = gamma_c[None, :, None, None] * state + jnp.einsum('bjhd,bjhe->bhde', kn * zeta[None, :, :, None], vn)
        return state, cross

    state0 = jnp.zeros((B, H, Dh, Dh), f32)
    _, o_cross = lax.scan(step, state0, (qc.transpose(1, 0, 2, 3, 4), kc.transpose(1, 0, 2, 3, 4), vc.transpose(1, 0, 2, 3, 4)))
    y = (o_inner + o_cross.transpose(1, 0, 2, 3, 4)).reshape(B, S, H, Dh)
    mu = jnp.mean(y, axis=-1, keepdims=True)
    var = jnp.mean(jnp.square(y - mu), axis=-1, keepdims=True)
    y = (y - mu) * lax.rsqrt(var + EPS) * out_norm.astype(f32).reshape(H, Dh)
    return (jax.nn.silu(g.astype(f32)).reshape(B, S, H, Dh) * y).astype(g.dtype)


def memory_cross_attention(x, mem, norm_x, norm_mem, wq, wkv, q_norm, k_norm, wo):
    B, S, _ = x.shape
    M = mem.shape[1]
    h = rms_norm(x, norm_x)
    hm = rms_norm(mem, norm_mem)
    q = rms_norm((h @ wq).reshape(B, S, MEM_HEADS, HEAD_DIM), q_norm) * (HEAD_DIM ** -0.5)
    k, v = jnp.split(hm @ wkv, 2, axis=-1)
    k = rms_norm(k.reshape(B, M, MEM_HEADS, HEAD_DIM), k_norm)
    v = v.reshape(B, M, MEM_HEADS, HEAD_DIM)
    p = jax.nn.softmax(jnp.einsum('bshd,bmhd->bhsm', q, k).astype(jnp.float32), axis=-1)
    o = jnp.einsum('bhsm,bmhd->bshd', p.astype(v.dtype), v).reshape(B, S, MEM_WIDTH)
    return o @ wo


def hierarchical_moe(x, norm_g, w_rg, b_rg, w_re, b_re, w_gate, w_up, w_down):
    B, S, Dm = x.shape
    T = B * S
    K = TOP_K_IN_GROUP
    h = rms_norm(x, norm_g).reshape(T, Dm)
    g_logits = (h @ w_rg).astype(jnp.float32) + b_rg.astype(jnp.float32)
    g_prob = jax.nn.softmax(g_logits, axis=-1)
    grp = jnp.argmax(g_logits, axis=-1)
    g_w = jnp.take_along_axis(g_prob, grp[:, None], axis=-1)
    e_all = jnp.einsum('td,gde->tge', h, w_re).astype(jnp.float32) + b_re.astype(jnp.float32)
    e_logits = jnp.take_along_axis(e_all, grp[:, None, None], axis=1)[:, 0]
    top_val, top_local = lax.top_k(e_logits, K)
    e_w = jax.nn.softmax(top_val, axis=-1) * g_w
    expert_id = grp[:, None] * EXPERTS_PER_GROUP + top_local
    A = T * K
    flat_e = expert_id.reshape(A)
    flat_tok = jnp.repeat(jnp.arange(T, dtype=jnp.int32), K)
    flat_w = e_w.reshape(A)
    order = jnp.argsort(flat_e)
    se, stok, sw = flat_e[order], flat_tok[order], flat_w[order]
    counts = jnp.bincount(flat_e, length=N_EXPERTS)
    start = jnp.cumsum(counts) - counts
    padded = (counts + MOE_BLOCK - 1) // MOE_BLOCK * MOE_BLOCK
    pend = jnp.cumsum(padded)
    pstart = pend - padded
    dest = pstart[se] + jnp.arange(A) - start[se]
    P = A + N_EXPERTS * MOE_BLOCK
    n_blocks = P // MOE_BLOCK
    row_tok = jnp.full((P,), T, jnp.int32).at[dest].set(stok)
    row_w = jnp.zeros((P,), h.dtype).at[dest].set(sw.astype(h.dtype))
    blk_e = jnp.minimum(jnp.searchsorted(pend, jnp.arange(n_blocks) * MOE_BLOCK, side='right'), N_EXPERTS - 1)
    h_pad = jnp.concatenate([h, jnp.zeros((1, Dm), h.dtype)], axis=0)
    x_rows = h_pad[row_tok].reshape(n_blocks, MOE_BLOCK, Dm)

    def expert_block(args):
        xb, e = args
        return (jax.nn.silu(xb @ w_gate[e]) * (xb @ w_up[e])) @ w_down[e]

    y_rows = lax.map(expert_block, (x_rows, blk_e)).reshape(P, Dm)
    y = jax.ops.segment_sum(y_rows * row_w[:, None], row_tok, num_segments=T + 1)[:T]
    return y.reshape(B, S, Dm)


def setup_inputs(seed: int = 0) -> dict:
    key = jax.random.key(seed)
    keys = iter(jax.random.split(key, 48))
    f32 = jnp.float32
    L, Dh, D = DEPTH, HEAD_DIM, D_MODEL

    def nrm(shape, scale):
        return jax.random.normal(next(keys), shape, f32) * scale

    def gain(shape):
        return 1.0 + nrm(shape, 0.02)

    return {
        'x': nrm((BATCH, SEQ, D), 1.0),
        'mem': nrm((BATCH, N_MEM, D), 1.0),
        'mix_norm': gain((L, D)),
        'w_in': nrm((L, D, PROJ_WIDTH), D ** -0.5),
        'nsa_q_norm': gain((L, Dh)),
        'nsa_kcmp_norm': gain((L, Dh)),
        'nsa_ksel_norm': gain((L, Dh)),
        'nsa_kwin_norm': gain((L, Dh)),
        'cmp_pos_k': nrm((L, CMP_BLOCK, Dh), 0.1),
        'cmp_pos_v': nrm((L, CMP_BLOCK, Dh), 0.1),
        'cmp_k_w1': nrm((L, CMP_BLOCK * Dh, CMP_HIDDEN), (CMP_BLOCK * Dh) ** -0.5),
        'cmp_k_w2': nrm((L, CMP_HIDDEN, Dh), CMP_HIDDEN ** -0.5),
        'cmp_v_w1': nrm((L, CMP_BLOCK * Dh, CMP_HIDDEN), (CMP_BLOCK * Dh) ** -0.5),
        'cmp_v_w2': nrm((L, CMP_HIDDEN, Dh), CMP_HIDDEN ** -0.5),
        'nsa_out_norm': gain((L, NSA_WIDTH)),
        'ret_out_norm': gain((L, RET_WIDTH)),
        'w_out': nrm((L, MIX_WIDTH, D), MIX_WIDTH ** -0.5),
        'mem_x_norm': gain((L, D)),
        'mem_kv_norm': gain((L, D)),
        'mem_wq': nrm((L, D, MEM_WIDTH), D ** -0.5),
        'mem_wkv': nrm((L, D, 2 * MEM_WIDTH), D ** -0.5),
        'mem_q_norm': gain((L, Dh)),
        'mem_k_norm': gain((L, Dh)),
        'mem_wo': nrm((L, MEM_WIDTH, D), MEM_WIDTH ** -0.5),
        'ffn_norm': gain((L, D)),
        'router_group_w': nrm((L, D, N_GROUPS), D ** -0.5),
        'router_group_b': nrm((L, N_GROUPS), 0.01),
        'router_expert_w': nrm((L, N_GROUPS, D, EXPERTS_PER_GROUP), D ** -0.5),
        'router_expert_b': nrm((L, N_GROUPS, EXPERTS_PER_GROUP), 0.01),
        'exp_w_gate': nrm((L, N_EXPERTS, D, EXPERT_FF), D ** -0.5),
        'exp_w_up': nrm((L, N_EXPERTS, D, EXPERT_FF), D ** -0.5),
        'exp_w_down': nrm((L, N_EXPERTS, EXPERT_FF, D), EXPERT_FF ** -0.5),
    }


def reference(x, mem, mix_norm, w_in, nsa_q_norm, nsa_kcmp_norm, nsa_ksel_norm, nsa_kwin_norm,
              cmp_pos_k, cmp_pos_v, cmp_k_w1, cmp_k_w2, cmp_v_w1, cmp_v_w2,
              nsa_out_norm, ret_out_norm, w_out,
              mem_x_norm, mem_kv_norm, mem_wq, mem_wkv, mem_q_norm, mem_k_norm, mem_wo,
              ffn_norm, router_group_w, router_group_b, router_expert_w, router_expert_b,
              exp_w_gate, exp_w_up, exp_w_down):
    B, S, _ = x.shape
    G, Dh = NSA_KV_GROUPS, HEAD_DIM
    for l in range(DEPTH):
        h = rms_norm(x, mix_norm[l])
        proj = h @ w_in[l]
        q_a, kv_a, gate_a, ret_a = jnp.split(proj, list(PROJ_SPLIT_POINTS), axis=-1)
        kc, vc, ks, vs, kw, vw = jnp.split(kv_a.reshape(B, S, 6, G, Dh), 6, axis=2)
        o_a = nsa_mixer(q_a, kc[:, :, 0], vc[:, :, 0], ks[:, :, 0], vs[:, :, 0], kw[:, :, 0], vw[:, :, 0], gate_a,
                        nsa_q_norm[l], nsa_kcmp_norm[l], nsa_ksel_norm[l], nsa_kwin_norm[l],
                        cmp_pos_k[l], cmp_pos_v[l], cmp_k_w1[l], cmp_k_w2[l], cmp_v_w1[l], cmp_v_w2[l])
        o_a = rms_norm(o_a, nsa_out_norm[l].reshape(NSA_HEADS, Dh)).reshape(B, S, NSA_WIDTH)
        q_r, k_r, v_r, g_r = jnp.split(ret_a, 4, axis=-1)
        o_b = retention_mixer(q_r, k_r, v_r, g_r, ret_out_norm[l]).reshape(B, S, RET_WIDTH)
        x = x + jnp.concatenate([o_a, o_b], axis=-1) @ w_out[l]
        x = x + memory_cross_attention(x, mem, mem_x_norm[l], mem_kv_norm[l], mem_wq[l], mem_wkv[l],
                                       mem_q_norm[l], mem_k_norm[l], mem_wo[l])
        x = x + hierarchical_moe(x, ffn_norm[l], router_group_w[l], router_group_b[l],
                                 router_expert_w[l], router_expert_b[l],
                                 exp_w_gate[l], exp_w_up[l], exp_w_down[l])
    return x
```

```python
import functools

import numpy as np
import jax
import jax.numpy as jnp
from jax import lax
from jax.experimental import pallas as pl
from jax.experimental.pallas import tpu as pltpu

F32 = jnp.float32
BF16 = jnp.bfloat16

D_MODEL = 1024
HEAD_DIM = 64
LANES = 128
NSA_HEADS = 8
NSA_GROUPS = 2
NSA_WIDTH = NSA_HEADS * HEAD_DIM
CMP_BLOCK = 32
CMP_STRIDE = 16
CMP_HIDDEN = 2 * HEAD_DIM
SEL_BLOCK = 64
SEL_TOPK = 8
WINDOW = 512
RET_HEADS = 8
RET_WIDTH = RET_HEADS * HEAD_DIM
RET_CHUNK = 128
ROPE_BASE = 10000.0
MEM_HEADS = 4
MEM_WIDTH = MEM_HEADS * HEAD_DIM
N_GROUPS = 4
EXPERTS_PER_GROUP = 8
N_EXPERTS = N_GROUPS * EXPERTS_PER_GROUP
EXPERT_FF = D_MODEL // 4
EPS = 1e-6
NEG_INF = -1e30
FORCE_SCORE = 1e9
BELOW_ALL = -3e38

COL_QA = 0
COL_QR, COL_KR, COL_VR, COL_GR = 512, 1024, 1536, 2048
COL_KVC, COL_KSV, COL_KWV = 2560, 2816, 3072
COL_GATE = 3328
PROJ_PAD = 3456

PROJ_TM = 512
NSA_TQ = 256
SEL_KC = 512
WIN_KEYS = WINDOW + NSA_TQ
POST_TM = 256
RANK_TM = 512
MOE_RB = 256
MOVE_TM = 256
VMEM_LIMIT = 56 * 1024 * 1024


def _cparams(n_axes):
    return pltpu.CompilerParams(dimension_semantics=("arbitrary",) * n_axes,
                                vmem_limit_bytes=VMEM_LIMIT)


def _dot(a, b):
    return jnp.dot(a, b, preferred_element_type=F32)


def _dot_nt(a, b):
    return lax.dot_general(a, b, (((1,), (1,)), ((), ())), preferred_element_type=F32)


def _dot_tn(a, b):
    return lax.dot_general(a, b, (((0,), (0,)), ((), ())), preferred_element_type=F32)


def _rms_full(x, g):
    ms = jnp.mean(x * x, axis=-1, keepdims=True)
    return x * lax.rsqrt(ms + EPS) * g


def _seg_mean(x, avg):
    return _dot(x.astype(BF16), avg)


def _silu(x):
    return x * (1.0 / (1.0 + jnp.exp(-x)))


def _sigmoid(x):
    return 1.0 / (1.0 + jnp.exp(-x))


def _block_avg(width):
    i = np.arange(width)
    return ((i[:, None] // HEAD_DIM == i[None, :] // HEAD_DIM) / HEAD_DIM).astype(np.float32)


def _proj_body(x_ref, g_ref, w_ref, o_ref):
    h = _rms_full(x_ref[...], g_ref[...]).astype(BF16)
    step = PROJ_PAD // 3
    for j in range(3):
        o_ref[:, j * step:(j + 1) * step] = _dot(h, w_ref[:, j * step:(j + 1) * step]).astype(BF16)


def _proj(x2d, g, w):
    T = x2d.shape[0]
    return pl.pallas_call(
        _proj_body,
        grid=(T // PROJ_TM,),
        in_specs=[pl.BlockSpec((PROJ_TM, D_MODEL), lambda i: (i, 0)),
                  pl.BlockSpec((1, D_MODEL), lambda i: (0, 0)),
                  pl.BlockSpec((D_MODEL, PROJ_PAD), lambda i: (0, 0))],
        out_specs=pl.BlockSpec((PROJ_TM, PROJ_PAD), lambda i: (i, 0)),
        out_shape=jax.ShapeDtypeStruct((T, PROJ_PAD), BF16),
        compiler_params=_cparams(1),
        name="proj",
    )(x2d, g, w)


def _dup_groups(x):
    lane = lax.broadcasted_iota(jnp.int32, x.shape, 1)
    xs = pltpu.roll(x, HEAD_DIM, axis=1)
    lo = lane < HEAD_DIM
    return jnp.where(lo, x, xs), jnp.where(lo, xs, x)


def _nsa_prep_body(kvc_ref, ksv_ref, kwv_ref, pos_ref, w1_ref, w2_ref, gain_ref, avg_ref,
                   kcmp_ref, vcmp_ref, ks_ref, vs_ref, kw_ref, vw_ref, scr_k, scr_v):
    avg = avg_ref[...]
    n_c = scr_k.shape[0] // CMP_STRIDE
    scr_k[...] = kvc_ref[0, :, 0:LANES].astype(F32)
    scr_v[...] = kvc_ref[0, :, LANES:2 * LANES].astype(F32)
    for j, out_ref, scr in ((0, kcmp_ref, scr_k), (1, vcmp_ref, scr_v)):
        ycat = jnp.concatenate(
            [scr[pl.ds(l, n_c, stride=CMP_STRIDE), :] for l in range(CMP_STRIDE)], axis=1)
        first = _dot((ycat + pos_ref[j, 0]).astype(BF16), w1_ref[j, 0])
        second = _dot((ycat + pos_ref[j, 1]).astype(BF16), w1_ref[j, 1])
        hidden = first + pltpu.roll(second, n_c - 1, axis=0)
        cmp_tok = _dot(_silu(hidden).astype(BF16), w2_ref[j])
        if j == 0:
            ms = _seg_mean(cmp_tok * cmp_tok, avg)
            cmp_tok = cmp_tok * lax.rsqrt(ms + EPS) * gain_ref[0]
        d0, d1 = _dup_groups(cmp_tok)
        out_ref[0, 0] = d0.astype(BF16)
        out_ref[0, 1] = d1.astype(BF16)

    for src_ref, k_out, v_out, gi in ((ksv_ref, ks_ref, vs_ref, 1), (kwv_ref, kw_ref, vw_ref, 2)):
        k = src_ref[0, :, 0:LANES].astype(F32)
        ms = _seg_mean(k * k, avg)
        k = k * lax.rsqrt(ms + EPS) * gain_ref[gi]
        d0, d1 = _dup_groups(k)
        k_out[0, 0] = d0.astype(BF16)
        k_out[0, 1] = d1.astype(BF16)
        d0, d1 = _dup_groups(src_ref[0, :, LANES:2 * LANES].astype(F32))
        v_out[0, 0] = d0.astype(BF16)
        v_out[0, 1] = d1.astype(BF16)


def _nsa_prep(proj3, pos, w1, w2, gains):
    B, S, _ = proj3.shape
    n_c = S // CMP_STRIDE
    avg = jnp.asarray(_block_avg(LANES), BF16)
    col = lambda c: pl.BlockSpec((1, S, 2 * LANES), lambda b: (b, 0, c // (2 * LANES)))
    full = lambda a: pl.BlockSpec(a.shape, lambda b: (0,) * a.ndim)
    cmp_spec = pl.BlockSpec((1, NSA_GROUPS, n_c, LANES), lambda b: (b, 0, 0, 0))
    seq_spec = pl.BlockSpec((1, NSA_GROUPS, S, LANES), lambda b: (b, 0, 0, 0))
    cmp_shape = jax.ShapeDtypeStruct((B, NSA_GROUPS, n_c, LANES), BF16)
    seq_shape = jax.ShapeDtypeStruct((B, NSA_GROUPS, S, LANES), BF16)
    return pl.pallas_call(
        _nsa_prep_body,
        grid=(B,),
        in_specs=[col(COL_KVC), col(COL_KSV), col(COL_KWV), full(pos), full(w1), full(w2), full(gains), full(avg)],
        out_specs=[cmp_spec, cmp_spec, seq_spec, seq_spec, seq_spec, seq_spec],
        out_shape=[cmp_shape, cmp_shape, seq_shape, seq_shape, seq_shape, seq_shape],
        scratch_shapes=[pltpu.VMEM((S, LANES), F32), pltpu.VMEM((S, LANES), F32)],
        compiler_params=_cparams(1),
        name="nsa_prep",
    )(proj3, proj3, proj3, pos, w1, w2, gains, avg)


def _masked_softmax(s, mask):
    s = jnp.where(mask, s, NEG_INF)
    m = jnp.max(s, axis=-1, keepdims=True)
    p = jnp.where(mask, jnp.exp(s - m), 0.0)
    return p / jnp.maximum(jnp.sum(p, axis=-1, keepdims=True), 1e-30)


def _nsa_attn_body(q_ref, gate_ref, kcmp_ref, vcmp_ref, ks_ref, vs_ref, kw_ref, vw_ref,
                   qgain_ref, ogain_ref, avgq_ref, avgo_ref, msc_ref, esel_ref, egate_ref,
                   o_ref, m_scr, l_scr, acc_scr):
    tq = q_ref.shape[1]
    n_cmp = kcmp_ref.shape[2]
    rows = 4 * tq
    q0 = pl.program_id(1) * tq

    q = q_ref[0].astype(F32)
    ms = _seg_mean(q * q, avgq_ref[...])
    qn = q * lax.rsqrt(ms + EPS) * qgain_ref[...] * (HEAD_DIM ** -0.5)

    gate_raw = gate_ref[0]
    gates = [_sigmoid(_dot(gate_raw, egate_ref[j])) for j in range(3)]

    lane_q = lax.broadcasted_iota(jnp.int32, (tq, LANES), 1)
    lo_q = lane_q < HEAD_DIM
    t_q = q0 + lax.broadcasted_iota(jnp.int32, (tq, LANES), 0)
    cur = t_q // SEL_BLOCK
    forced = (lane_q == 0) | (lane_q == cur) | (lane_q == cur - 1)
    future = lane_q > cur
    lane_qf = lane_q.astype(F32)

    def t_rows(width):
        r = lax.broadcasted_iota(jnp.int32, (rows, width), 0)
        return q0 + (r & (tq - 1))

    def lanes(width):
        return lax.broadcasted_iota(jnp.int32, (rows, width), 1)

    def pair_heads(x):
        return [jnp.where(lo_q, x[(2 * p) * tq:(2 * p + 1) * tq], x[(2 * p + 1) * tq:(2 * p + 2) * tq])
                for p in range(2)]

    for g in range(NSA_GROUPS):
        slabs = [qn[:, (2 * g + p) * LANES:(2 * g + p + 1) * LANES] for p in range(2)]
        qs = jnp.concatenate(
            [jnp.where(lo_q, slabs[0], 0.0), jnp.where(lo_q, 0.0, slabs[0]),
             jnp.where(lo_q, slabs[1], 0.0), jnp.where(lo_q, 0.0, slabs[1])], axis=0).astype(BF16)

        s_c = _dot_nt(qs, kcmp_ref[0, g])
        cmask = (lanes(n_cmp) * CMP_STRIDE + (CMP_BLOCK - 1)) <= t_rows(n_cmp)
        p_c = _masked_softmax(s_c, cmask)
        o_cmp = _dot(p_c.astype(BF16), vcmp_ref[0, g])

        p_sum = p_c[0:tq] + p_c[tq:2 * tq] + p_c[2 * tq:3 * tq] + p_c[3 * tq:4 * tq]
        p_hi = p_sum.astype(BF16)
        p_lo = (p_sum - p_hi.astype(F32)).astype(BF16)
        imp = _dot(p_hi, msc_ref[...]) + _dot(p_lo, msc_ref[...])
        v = jnp.where(forced, FORCE_SCORE, jnp.where(future, NEG_INF, imp))
        v = jnp.where(lane_q < esel_ref.shape[0] * (SEL_KC // SEL_BLOCK), v, BELOW_ALL)
        sel = jnp.zeros((tq, LANES), F32)
        for _ in range(SEL_TOPK):
            mx = jnp.max(v, axis=-1, keepdims=True)
            first = jnp.min(jnp.where(v == mx, lane_qf, float(LANES)), axis=-1, keepdims=True)
            pick = lane_qf == first
            sel = jnp.where(pick, 1.0, sel)
            v = jnp.where(pick, BELOW_ALL, v)
        sel_b = sel.astype(BF16)

        m_scr[...] = jnp.full(m_scr.shape, NEG_INF, F32)
        l_scr[...] = jnp.zeros(l_scr.shape, F32)
        acc_scr[...] = jnp.zeros(acc_scr.shape, F32)
        n_chunks = lax.shift_right_logical(q0 + tq - 1, int(np.log2(SEL_KC))) + 1

        def sel_chunk(kc, carry):
            k0 = pl.multiple_of(kc * SEL_KC, SEL_KC)
            s = _dot_nt(qs, ks_ref[0, g, pl.ds(k0, SEL_KC), :])
            chosen = _dot(sel_b, esel_ref[kc])
            chosen = jnp.concatenate([chosen] * 4, axis=0)
            mask = (chosen > 0.5) & ((k0 + lanes(SEL_KC)) <= t_rows(SEL_KC))
            s = jnp.where(mask, s, NEG_INF)
            m_old = m_scr[...]
            m_new = jnp.maximum(m_old, jnp.max(s, axis=-1, keepdims=True))
            p = jnp.exp(s - m_new)
            alpha = jnp.exp(m_old - m_new)
            l_scr[...] = alpha * l_scr[...] + jnp.sum(p, axis=-1, keepdims=True)
            acc_scr[...] = alpha * acc_scr[...] + _dot(p.astype(BF16), vs_ref[0, g, pl.ds(k0, SEL_KC), :])
            m_scr[...] = m_new
            return carry

        lax.fori_loop(0, n_chunks, sel_chunk, 0)
        o_sel = acc_scr[...] / jnp.maximum(l_scr[...], 1e-30)

        w0 = pl.multiple_of(jnp.maximum(q0 - WINDOW, 0), tq)
        n_w = WINDOW + tq
        s_w = _dot_nt(qs, kw_ref[0, g, pl.ds(w0, n_w), :])
        diff = t_rows(n_w) - (w0 + lanes(n_w))
        p_w = _masked_softmax(s_w, (diff >= 0) & (diff < WINDOW))
        o_win = _dot(p_w.astype(BF16), vw_ref[0, g, pl.ds(w0, n_w), :])

        cmp_s, sel_s, win_s = pair_heads(o_cmp), pair_heads(o_sel), pair_heads(o_win)
        for p in range(2):
            cols = slice((2 * g + p) * LANES, (2 * g + p + 1) * LANES)
            mix = gates[0][:, cols] * cmp_s[p] + gates[1][:, cols] * sel_s[p] + gates[2][:, cols] * win_s[p]
            ms_o = _seg_mean(mix * mix, avgo_ref[...])
            o_ref[0, :, cols] = (mix * lax.rsqrt(ms_o + EPS) * ogain_ref[:, cols]).astype(BF16)


def _sel_from_cmp_t(n_cmp, n_sel):
    c0 = np.arange(n_cmp) * CMP_STRIDE
    s0 = np.arange(n_sel) * SEL_BLOCK
    ov = np.minimum(c0[None, :] + CMP_BLOCK, s0[:, None] + SEL_BLOCK) - np.maximum(c0[None, :], s0[:, None])
    m = (np.clip(ov, 0, None) / CMP_BLOCK).astype(np.float32)
    m[:, (np.arange(n_cmp) * CMP_STRIDE + CMP_BLOCK) > n_sel * SEL_BLOCK] = 0.0
    out = np.zeros((n_cmp, LANES), np.float32)
    out[:, :n_sel] = m.T
    return out


def _nsa_attn(proj3, kcmp, vcmp, ks, vs, kw, vw, q_gain, o_gain):
    B, S, _ = proj3.shape
    n_cmp = kcmp.shape[2]
    n_sel = S // SEL_BLOCK
    assert n_sel <= LANES and S % SEL_KC == 0 and S >= WINDOW + NSA_TQ
    tq = NSA_TQ
    avgq = jnp.asarray(_block_avg(NSA_WIDTH), BF16)
    avgo = jnp.asarray(_block_avg(LANES), BF16)
    msc = jnp.asarray(_sel_from_cmp_t(n_cmp, n_sel), BF16)
    blk = np.arange(LANES)[:, None]
    key = np.arange(S)[None, :]
    esel = (blk == key // SEL_BLOCK).astype(np.float32)
    esel = jnp.asarray(esel.reshape(LANES, S // SEL_KC, SEL_KC).transpose(1, 0, 2), BF16)
    src = np.arange(LANES)[:, None]
    dst = np.arange(NSA_WIDTH)[None, :]
    egate = jnp.asarray(np.stack([(src == (dst // HEAD_DIM) * 3 + j) for j in range(3)]).astype(np.float32), BF16)

    full = lambda a: pl.BlockSpec(a.shape, lambda b, i: (0,) * a.ndim)
    per_b = lambda a: pl.BlockSpec((1,) + a.shape[1:], lambda b, i: (b,) + (0,) * (a.ndim - 1))
    return pl.pallas_call(
        _nsa_attn_body,
        grid=(B, S // tq),
        in_specs=[pl.BlockSpec((1, tq, NSA_WIDTH), lambda b, i: (b, i, COL_QA // NSA_WIDTH)),
                  pl.BlockSpec((1, tq, LANES), lambda b, i: (b, i, COL_GATE // LANES)),
                  per_b(kcmp), per_b(vcmp), per_b(ks), per_b(vs), per_b(kw), per_b(vw),
                  full(q_gain), full(o_gain), full(avgq), full(avgo), full(msc), full(esel), full(egate)],
        out_specs=pl.BlockSpec((1, tq, NSA_WIDTH), lambda b, i: (b, i, 0)),
        out_shape=jax.ShapeDtypeStruct((B, S, NSA_WIDTH), BF16),
        scratch_shapes=[pltpu.VMEM((4 * tq, 1), F32), pltpu.VMEM((4 * tq, 1), F32),
                        pltpu.VMEM((4 * tq, LANES), F32)],
        compiler_params=_cparams(2),
        name="nsa_attn",
    )(proj3, proj3, kcmp, vcmp, ks, vs, kw, vw, q_gain, o_gain, avgq, avgo, msc, esel, egate)


def _retention_body(q_ref, k_ref, v_ref, g_ref, cos_ref, sin_ref, decay_ref, xi_ref, zeta_ref, gammac_ref,
                    gain_ref, avg_ref, o_ref, state_scr):
    S = q_ref.shape[1]
    C = RET_CHUNK
    lane = lax.broadcasted_iota(jnp.int32, (C, LANES), 1)
    lo = lane < HEAD_DIM
    first_half = (lane & (HEAD_DIM - 1)) < HEAD_DIM // 2
    r = lax.broadcasted_iota(jnp.int32, (LANES, LANES), 0)
    c = lax.broadcasted_iota(jnp.int32, (LANES, LANES), 1)
    same_head = (r < HEAD_DIM) == (c < HEAD_DIM)
    avg = avg_ref[...]
    state_scr[...] = jnp.zeros(state_scr.shape, F32)

    def rope(x, cos, sin):
        swapped = jnp.where(first_half, pltpu.roll(x, LANES - HEAD_DIM // 2, axis=1),
                            pltpu.roll(x, HEAD_DIM // 2, axis=1))
        return x * cos + swapped * sin

    def chunk(n, carry):
        r0 = pl.multiple_of(n * C, C)
        cos = cos_ref[pl.ds(r0, C), :]
        sin = sin_ref[pl.ds(r0, C), :]
        for p in range(RET_HEADS // 2):
            cols = slice(p * LANES, (p + 1) * LANES)
            q = rope(q_ref[0, pl.ds(r0, C), cols].astype(F32), cos, sin)
            k = rope(k_ref[0, pl.ds(r0, C), cols].astype(F32), cos, sin) * (HEAD_DIM ** -0.5)
            vb = v_ref[0, pl.ds(r0, C), cols]
            kb = k.astype(BF16)
            outs = []
            for half in range(2):
                qm = jnp.where(lo if half == 0 else ~lo, q, 0.0).astype(BF16)
                inner = _dot_nt(qm, kb) * decay_ref[2 * p + half]
                outs.append(_dot(inner.astype(BF16), vb))
            y = jnp.where(lo, outs[0], outs[1])
            state = state_scr[p]
            y = y + _dot(q.astype(BF16), state.astype(BF16)) * xi_ref[p]
            upd = _dot_tn((k * zeta_ref[p]).astype(BF16), vb)
            state_scr[p] = gammac_ref[p] * state + jnp.where(same_head, upd, 0.0)
            mu = _seg_mean(y, avg)
            d = y - mu
            var = _seg_mean(d * d, avg)
            gate = g_ref[0, pl.ds(r0, C), cols].astype(F32)
            o_ref[0, pl.ds(r0, C), cols] = (_silu(gate) * (d * lax.rsqrt(var + EPS) * gain_ref[:, cols])).astype(BF16)
        return carry

    lax.fori_loop(0, S // C, chunk, 0)


def _retention_tables(S):
    half = HEAD_DIM // 2
    inv_freq = ROPE_BASE ** (-jnp.arange(half, dtype=F32) / half)
    ang = jnp.arange(S, dtype=F32)[:, None] * inv_freq[None, :]
    cos, sin = jnp.cos(ang), jnp.sin(ang)
    cos_t = jnp.tile(cos, (1, 4))
    sin_t = jnp.tile(jnp.concatenate([-sin, sin], axis=1), (1, 2))
    C = RET_CHUNK
    H = RET_HEADS
    log_gamma = jnp.log1p(-jnp.power(2.0, -5.0 - jnp.arange(H, dtype=F32)))
    i = jnp.arange(C, dtype=F32)
    rel = i[:, None] - i[None, :]
    decay = jnp.where(rel >= 0, jnp.exp(jnp.maximum(rel, 0.0)[None] * log_gamma[:, None, None]), 0.0)
    xi = jnp.exp((i + 1.0)[:, None] * log_gamma[None, :])
    zeta = jnp.exp((C - 1.0 - i)[:, None] * log_gamma[None, :])
    gamma_c = jnp.exp(C * log_gamma)
    per_pair = lambda t: jnp.repeat(t.T.reshape(H // 2, 2, -1), HEAD_DIM, axis=1).transpose(0, 2, 1)
    gammac = jnp.repeat(gamma_c.reshape(H // 2, 2), HEAD_DIM, axis=1)[:, None, :]
    return cos_t, sin_t, decay, per_pair(xi), per_pair(zeta), gammac


def _retention(proj3, gain):
    B, S, _ = proj3.shape
    cos_t, sin_t, decay, xi, zeta, gammac = _retention_tables(S)
    avg = jnp.asarray(_block_avg(LANES), BF16)
    col = lambda c: pl.BlockSpec((1, S, RET_WIDTH), lambda b: (b, 0, c // RET_WIDTH))
    full = lambda a: pl.BlockSpec(a.shape, lambda b: (0,) * a.ndim)
    return pl.pallas_call(
        _retention_body,
        grid=(B,),
        in_specs=[col(COL_QR), col(COL_KR), col(COL_VR), col(COL_GR), full(cos_t), full(sin_t), full(decay),
                  full(xi), full(zeta), full(gammac), full(gain), full(avg)],
        out_specs=pl.BlockSpec((1, S, RET_WIDTH), lambda b: (b, 0, 0)),
        out_shape=jax.ShapeDtypeStruct((B, S, RET_WIDTH), BF16),
        scratch_shapes=[pltpu.VMEM((RET_HEADS // 2, LANES, LANES), F32)],
        compiler_params=_cparams(1),
        name="retention",
    )(proj3, proj3, proj3, proj3, cos_t, sin_t, decay, xi, zeta, gammac, gain, avg)


def _mem_prep_body(mem_ref, g_ref, wkv_ref, kgain_ref, avg_ref, k_ref, v_ref):
    hm = _rms_full(mem_ref[0], g_ref[...]).astype(BF16)
    kv = _dot(hm, wkv_ref[...])
    k = kv[:, :MEM_WIDTH]
    ms = _seg_mean(k * k, avg_ref[...])
    k_ref[0] = (k * lax.rsqrt(ms + EPS) * kgain_ref[...]).astype(BF16)
    v_ref[0] = kv[:, MEM_WIDTH:].astype(BF16)


def _mem_prep(mem, g, wkv, kgain):
    B, M, _ = mem.shape
    avg = jnp.asarray(_block_avg(MEM_WIDTH), BF16)
    full = lambda a: pl.BlockSpec(a.shape, lambda b: (0,) * a.ndim)
    out_spec = pl.BlockSpec((1, M, MEM_WIDTH), lambda b: (b, 0, 0))
    out_shape = jax.ShapeDtypeStruct((B, M, MEM_WIDTH), BF16)
    return pl.pallas_call(
        _mem_prep_body,
        grid=(B,),
        in_specs=[pl.BlockSpec((1, M, D_MODEL), lambda b: (b, 0, 0)), full(g), full(wkv), full(kgain), full(avg)],
        out_specs=[out_spec, out_spec],
        out_shape=[out_shape, out_shape],
        compiler_params=_cparams(1),
        name="mem_prep",
    )(mem, g, wkv, kgain, avg)


def _post_body(x_ref, oa_ref, ob_ref, wout_ref, mk_ref, mv_ref, gx_ref, wq_ref, qgain_ref, avg_ref, wo_ref,
               gf_ref, wr_ref, br_ref, x2_ref, h_ref, route_ref):
    tm = x_ref.shape[1]
    x1 = x_ref[0] + _dot(oa_ref[0], wout_ref[0:NSA_WIDTH, :]) + _dot(ob_ref[0], wout_ref[NSA_WIDTH:, :])

    h = _rms_full(x1, gx_ref[...]).astype(BF16)
    q = _dot(h, wq_ref[...])
    ms = _seg_mean(q * q, avg_ref[...])
    q = q * lax.rsqrt(ms + EPS) * qgain_ref[...] * (HEAD_DIM ** -0.5)
    lane = lax.broadcasted_iota(jnp.int32, (tm, LANES), 1)
    lo = lane < HEAD_DIM
    slabs = []
    for p in range(MEM_HEADS // 2):
        cols = slice(p * LANES, (p + 1) * LANES)
        qp = q[:, cols]
        kp = mk_ref[0, :, cols]
        vp = mv_ref[0, :, cols]
        outs = []
        for half in range(2):
            qm = jnp.where(lo if half == 0 else ~lo, qp, 0.0).astype(BF16)
            s = _dot_nt(qm, kp)
            e = jnp.exp(s - jnp.max(s, axis=-1, keepdims=True))
            pr = e / jnp.sum(e, axis=-1, keepdims=True)
            outs.append(_dot(pr.astype(BF16), vp))
        slabs.append(jnp.where(lo, outs[0], outs[1]))
    o = jnp.concatenate(slabs, axis=1).astype(BF16)
    x2 = x1 + _dot(o, wo_ref[...])
    x2_ref[0] = x2

    hf = _rms_full(x2, gf_ref[...])
    h_ref[0] = hf
    logits = _dot(hf.astype(BF16), wr_ref[...]) + br_ref[...]
    lane_f = lane.astype(F32)
    big = float(LANES)
    gl = jnp.where(lane < N_GROUPS, logits, BELOW_ALL)
    gmax = jnp.max(gl, axis=-1, keepdims=True)
    grp = jnp.min(jnp.where(gl == gmax, lane_f, big), axis=-1, keepdims=True)
    g_w = 1.0 / jnp.sum(jnp.where(lane < N_GROUPS, jnp.exp(gl - gmax), 0.0), axis=-1, keepdims=True)
    e_lo = N_GROUPS + grp * EXPERTS_PER_GROUP
    el = jnp.where((lane_f >= e_lo) & (lane_f < e_lo + EXPERTS_PER_GROUP), logits, BELOW_ALL)
    v0 = jnp.max(el, axis=-1, keepdims=True)
    i0 = jnp.min(jnp.where(el == v0, lane_f, big), axis=-1, keepdims=True)
    el = jnp.where(lane_f == i0, BELOW_ALL, el)
    v1 = jnp.max(el, axis=-1, keepdims=True)
    i1 = jnp.min(jnp.where(el == v1, lane_f, big), axis=-1, keepdims=True)
    e1 = jnp.exp(v1 - v0)
    w0 = g_w / (1.0 + e1)
    w1 = g_w * e1 / (1.0 + e1)
    route = jnp.where(lane == 0, i0 - N_GROUPS,
                      jnp.where(lane == 1, i1 - N_GROUPS,
                                jnp.where(lane == 2, w0, jnp.where(lane == 3, w1, 0.0))))
    route_ref[0] = route


def _post(x, oa, ob, wout, mk, mv, gx, wq, qgain, wo, gf, wr, br):
    B, S, _ = x.shape
    tm = POST_TM
    avg = jnp.asarray(_block_avg(MEM_WIDTH), BF16)
    full = lambda a: pl.BlockSpec(a.shape, lambda b, i: (0,) * a.ndim)
    per_b = lambda a: pl.BlockSpec((1,) + a.shape[1:], lambda b, i: (b,) + (0,) * (a.ndim - 1))
    tile = lambda w: pl.BlockSpec((1, tm, w), lambda b, i: (b, i, 0))
    return pl.pallas_call(
        _post_body,
        grid=(B, S // tm),
        in_specs=[tile(D_MODEL), tile(NSA_WIDTH), tile(RET_WIDTH), full(wout), per_b(mk), per_b(mv), full(gx),
                  full(wq), full(qgain), full(avg), full(wo), full(gf), full(wr), full(br)],
        out_specs=[tile(D_MODEL), tile(D_MODEL), tile(LANES)],
        out_shape=[jax.ShapeDtypeStruct((B, S, D_MODEL), F32), jax.ShapeDtypeStruct((B, S, D_MODEL), F32),
                   jax.ShapeDtypeStruct((B, S, LANES), F32)],
        compiler_params=_cparams(2),
        name="post_mixer",
    )(x, oa, ob, wout, mk, mv, gx, wq, qgain, avg, wo, gf, wr, br)


def _rank_body(route_ref, tri_ref, rank_ref, count_ref, carry):
    @pl.when(pl.program_id(0) == 0)
    def _():
        carry[...] = jnp.zeros(carry.shape, F32)

    route = route_ref[...]
    tm = route.shape[0]
    lane = lax.broadcasted_iota(jnp.int32, (tm, LANES), 1).astype(F32)
    oh0 = lane == route[:, 0:1]
    oh1 = lane == route[:, 1:2]
    both = jnp.where(oh0, 1.0, 0.0) + jnp.where(oh1, 1.0, 0.0)
    before = _dot(tri_ref[...], both.astype(BF16)) + carry[...]
    r0 = jnp.sum(jnp.where(oh0, before, 0.0), axis=-1, keepdims=True)
    r1 = jnp.sum(jnp.where(oh1, before, 0.0), axis=-1, keepdims=True)
    lane_i = lax.broadcasted_iota(jnp.int32, (tm, LANES), 1)
    rank_ref[...] = jnp.where(lane_i == 0, r0, jnp.where(lane_i == 1, r1, 0.0))
    total = carry[...] + jnp.sum(both, axis=0, keepdims=True)
    carry[...] = total
    count_ref[...] = jnp.broadcast_to(total, count_ref.shape)


def _rank(route2d):
    T = route2d.shape[0]
    tm = RANK_TM
    tri = jnp.asarray(np.tril(np.ones((tm, tm), np.float32), -1), BF16)
    return pl.pallas_call(
        _rank_body,
        grid=(T // tm,),
        in_specs=[pl.BlockSpec((tm, LANES), lambda i: (i, 0)), pl.BlockSpec((tm, tm), lambda i: (0, 0))],
        out_specs=[pl.BlockSpec((tm, LANES), lambda i: (i, 0)), pl.BlockSpec((8, LANES), lambda i: (0, 0))],
        out_shape=[jax.ShapeDtypeStruct((T, LANES), F32), jax.ShapeDtypeStruct((8, LANES), F32)],
        scratch_shapes=[pltpu.VMEM((1, LANES), F32)],
        compiler_params=_cparams(1),
        name="moe_rank",
    )(route2d, tri)


def _row_copy(src, dst, sem):
    return pltpu.make_async_copy(src, dst, sem)


def _scatter_body(dest_ref, h_ref, xs_in_ref, xs_ref, sem):
    del xs_in_ref
    tm = h_ref.shape[0]

    def issue(i, carry):
        for s in range(2):
            _row_copy(h_ref.at[pl.ds(i, 1)], xs_ref.at[pl.ds(dest_ref[2 * i + s], 1)], sem).start()
        return carry

    lax.fori_loop(0, tm, issue, 0)

    def drain(i, carry):
        _row_copy(h_ref.at[pl.ds(0, 1)], xs_ref.at[pl.ds(0, 1)], sem).wait()
        return carry

    lax.fori_loop(0, 2 * tm, drain, 0)


def _scatter_rows(dest_flat, h2d, n_rows):
    T = h2d.shape[0]
    tm = MOVE_TM
    zeros = jnp.zeros((n_rows, D_MODEL), F32)
    return pl.pallas_call(
        _scatter_body,
        grid=(T // tm,),
        in_specs=[pl.BlockSpec((2 * tm,), lambda i: (i,), memory_space=pltpu.SMEM),
                  pl.BlockSpec((tm, D_MODEL), lambda i: (i, 0)),
                  pl.BlockSpec(memory_space=pl.ANY)],
        out_specs=pl.BlockSpec(memory_space=pl.ANY),
        out_shape=jax.ShapeDtypeStruct((n_rows, D_MODEL), F32),
        scratch_shapes=[pltpu.SemaphoreType.DMA(())],
        input_output_aliases={2: 0},
        compiler_params=_cparams(1),
        name="moe_scatter",
    )(dest_flat, h2d, zeros)


def _expert_body(blk_e_ref, n_act_ref, xs_ref, wg_ref, wu_ref, wd_ref, ys_ref):
    i = pl.program_id(0)

    @pl.when(i < n_act_ref[0])
    def _():
        x = xs_ref[...].astype(BF16)
        a = _dot(x, wg_ref[0].astype(BF16))
        b = _dot(x, wu_ref[0].astype(BF16))
        ys_ref[...] = _dot((_silu(a) * b).astype(BF16), wd_ref[0].astype(BF16))

    @pl.when(i >= n_act_ref[0])
    def _():
        ys_ref[...] = jnp.zeros(ys_ref.shape, F32)


def _experts(blk_e, n_act, xs, wg, wu, wd):
    n_rows = xs.shape[0]
    rb = MOE_RB
    grid_spec = pltpu.PrefetchScalarGridSpec(
        num_scalar_prefetch=2,
        grid=(n_rows // rb,),
        in_specs=[pl.BlockSpec((rb, D_MODEL), lambda i, be, na: (i, 0)),
                  pl.BlockSpec((1, D_MODEL, EXPERT_FF), lambda i, be, na: (be[i], 0, 0)),
                  pl.BlockSpec((1, D_MODEL, EXPERT_FF), lambda i, be, na: (be[i], 0, 0)),
                  pl.BlockSpec((1, EXPERT_FF, D_MODEL), lambda i, be, na: (be[i], 0, 0))],
        out_specs=pl.BlockSpec((rb, D_MODEL), lambda i, be, na: (i, 0)),
    )
    return pl.pallas_call(
        _expert_body,
        grid_spec=grid_spec,
        out_shape=jax.ShapeDtypeStruct((n_rows, D_MODEL), F32),
        compiler_params=_cparams(1),
        name="moe_experts",
    )(blk_e, n_act, xs, wg, wu, wd)


def _combine_body(dest_ref, x_ref, route_ref, ys_ref, o_ref, buf, sem):
    tm = x_ref.shape[0]

    def issue(i, carry):
        for s in range(2):
            _row_copy(ys_ref.at[pl.ds(dest_ref[2 * i + s], 1)], buf.at[s, pl.ds(i, 1)], sem).start()
        return carry

    lax.fori_loop(0, tm, issue, 0)

    def drain(i, carry):
        _row_copy(ys_ref.at[pl.ds(0, 1)], buf.at[0, pl.ds(0, 1)], sem).wait()
        return carry

    lax.fori_loop(0, 2 * tm, drain, 0)
    route = route_ref[...]
    o_ref[...] = x_ref[...] + (route[:, 2:3] * buf[0] + route[:, 3:4] * buf[1])


def _combine(dest_flat, x2d, route2d, ys):
    T = x2d.shape[0]
    tm = MOVE_TM
    return pl.pallas_call(
        _combine_body,
        grid=(T // tm,),
        in_specs=[pl.BlockSpec((2 * tm,), lambda i: (i,), memory_space=pltpu.SMEM),
                  pl.BlockSpec((tm, D_MODEL), lambda i: (i, 0)),
                  pl.BlockSpec((tm, LANES), lambda i: (i, 0)),
                  pl.BlockSpec(memory_space=pl.ANY)],
        out_specs=pl.BlockSpec((tm, D_MODEL), lambda i: (i, 0)),
        out_shape=jax.ShapeDtypeStruct((T, D_MODEL), F32),
        scratch_shapes=[pltpu.VMEM((2, tm, D_MODEL), F32), pltpu.SemaphoreType.DMA(())],
        compiler_params=_cparams(1),
        name="moe_combine",
    )(dest_flat, x2d, route2d, ys)


def _permute_w_in(w):
    kv0 = NSA_WIDTH
    gate0 = kv0 + 6 * NSA_GROUPS * HEAD_DIM
    ret0 = gate0 + 3 * NSA_HEADS
    pad = jnp.zeros((w.shape[0], PROJ_PAD - w.shape[1]), w.dtype)
    return jnp.concatenate([w[:, :kv0], w[:, ret0:], w[:, kv0:gate0], w[:, gate0:ret0], pad], axis=1).astype(BF16)


def _compress_weights(pos, w1, w2):
    eye = jnp.eye(NSA_GROUPS, dtype=F32)
    w1r = w1.reshape(CMP_BLOCK, HEAD_DIM, CMP_HIDDEN)
    w1b = jnp.einsum('ldh,gk->lgdkh', w1r, eye).reshape(2, CMP_STRIDE * LANES, NSA_GROUPS * CMP_HIDDEN)
    w2b = jnp.einsum('hd,gk->ghkd', w2, eye).reshape(NSA_GROUPS * CMP_HIDDEN, LANES)
    posb = jnp.tile(pos, (1, NSA_GROUPS)).reshape(2, 1, CMP_STRIDE * LANES)
    return posb, w1b.astype(BF16), w2b.astype(BF16)


def _dup2(g):
    return jnp.tile(g.reshape(1, HEAD_DIM), (1, 2))


def kernel(x, mem, mix_norm, w_in, nsa_q_norm, nsa_kcmp_norm, nsa_ksel_norm, nsa_kwin_norm, cmp_pos_k, cmp_pos_v, cmp_k_w1, cmp_k_w2, cmp_v_w1, cmp_v_w2, nsa_out_norm, ret_out_norm, w_out, mem_x_norm, mem_kv_norm, mem_wq, mem_wkv, mem_q_norm, mem_k_norm, mem_wo, ffn_norm, router_group_w, router_group_b, router_expert_w, router_expert_b, exp_w_gate, exp_w_up, exp_w_down):
    B, S, D = x.shape
    T = B * S
    depth = mix_norm.shape[0]
    for l in range(depth):
        proj = _proj(x.reshape(T, D), mix_norm[l].reshape(1, D), _permute_w_in(w_in[l])).reshape(B, S, PROJ_PAD)
        pk, w1k, w2k = _compress_weights(cmp_pos_k[l], cmp_k_w1[l], cmp_k_w2[l])
        pv, w1v, w2v = _compress_weights(cmp_pos_v[l], cmp_v_w1[l], cmp_v_w2[l])
        gains = jnp.stack([_dup2(nsa_kcmp_norm[l]), _dup2(nsa_ksel_norm[l]), _dup2(nsa_kwin_norm[l])])
        kcmp, vcmp, ks, vs, kw, vw = _nsa_prep(proj, jnp.stack([pk, pv]), jnp.stack([w1k, w1v]),
                                               jnp.stack([w2k, w2v]), gains)
        o_a = _nsa_attn(proj, kcmp, vcmp, ks, vs, kw, vw,
                        jnp.tile(nsa_q_norm[l].reshape(1, HEAD_DIM), (1, NSA_HEADS)),
                        nsa_out_norm[l].reshape(1, NSA_WIDTH))
        o_b = _retention(proj, ret_out_norm[l].reshape(1, RET_WIDTH))
        mk, mv = _mem_prep(mem, mem_kv_norm[l].reshape(1, D), mem_wkv[l].astype(BF16),
                           jnp.tile(mem_k_norm[l].reshape(1, HEAD_DIM), (1, MEM_HEADS)))
        w_r = jnp.concatenate([router_group_w[l],
                               router_expert_w[l].transpose(1, 0, 2).reshape(D, N_EXPERTS),
                               jnp.zeros((D, LANES - N_GROUPS - N_EXPERTS), F32)], axis=1).astype(BF16)
        b_r = jnp.concatenate([router_group_b[l], router_expert_b[l].reshape(N_EXPERTS),
                               jnp.zeros((LANES - N_GROUPS - N_EXPERTS,), F32)]).reshape(1, LANES)
        x2, hf, route = _post(x, o_a, o_b, w_out[l].astype(BF16), mk, mv, mem_x_norm[l].reshape(1, D),
                              mem_wq[l].astype(BF16), jnp.tile(mem_q_norm[l].reshape(1, HEAD_DIM), (1, MEM_HEADS)),
                              mem_wo[l].astype(BF16), ffn_norm[l].reshape(1, D), w_r, b_r)
        route2d = route.reshape(T, LANES)
        rank, counts = _rank(route2d)
        counts = counts[0, :N_EXPERTS].astype(jnp.int32)
        padded = (counts + MOE_RB - 1) // MOE_RB * MOE_RB
        pend = jnp.cumsum(padded)
        pstart = pend - padded
        eid = route2d[:, 0:2].astype(jnp.int32)
        dest = (pstart[eid] + rank[:, 0:2].astype(jnp.int32)).reshape(2 * T)
        n_rows = 2 * T + N_EXPERTS * MOE_RB
        n_blocks = n_rows // MOE_RB
        blk_row0 = jnp.arange(n_blocks, dtype=jnp.int32) * MOE_RB
        blk_e = jnp.minimum(jnp.sum((pend[None, :] <= blk_row0[:, None]).astype(jnp.int32), axis=1), N_EXPERTS - 1)
        n_act = (pend[-1:] // MOE_RB).astype(jnp.int32)
        xs = _scatter_rows(dest, hf.reshape(T, D), n_rows)
        ys = _experts(blk_e, n_act, xs, exp_w_gate[l], exp_w_up[l], exp_w_down[l])
        x = _combine(dest, x2.reshape(T, D), route2d, ys).reshape(B, S, D)
    return x
```

```python
import functools

import numpy as np
import jax
import jax.numpy as jnp
from jax import lax
from jax.experimental import pallas as pl
from jax.experimental.pallas import tpu as pltpu

F32 = jnp.float32
BF16 = jnp.bfloat16

D_MODEL = 1024
HEAD_DIM = 64
LANES = 128
NSA_HEADS = 8
NSA_GROUPS = 2
NSA_WIDTH = NSA_HEADS * HEAD_DIM
CMP_BLOCK = 32
CMP_STRIDE = 16
CMP_HIDDEN = 2 * HEAD_DIM
SEL_BLOCK = 64
SEL_TOPK = 8
WINDOW = 512
RET_HEADS = 8
RET_WIDTH = RET_HEADS * HEAD_DIM
RET_CHUNK = 128
ROPE_BASE = 10000.0
MEM_HEADS = 4
MEM_WIDTH = MEM_HEADS * HEAD_DIM
N_GROUPS = 4
EXPERTS_PER_GROUP = 8
N_EXPERTS = N_GROUPS * EXPERTS_PER_GROUP
EXPERT_FF = D_MODEL // 4
EPS = 1e-6
NEG_INF = -1e30
FORCE_SCORE = 1e9
BELOW_ALL = -3e38

COL_QA = 0
COL_QR, COL_KR, COL_VR, COL_GR = 512, 1024, 1536, 2048
COL_KVC, COL_KSV, COL_KWV = 2560, 2816, 3072
COL_GATE = 3328
PROJ_PAD = 3456

PROJ_TM = 512
NSA_TQ = 256
SEL_KC = 512
WIN_KEYS = WINDOW + NSA_TQ
POST_TM = 256
RANK_TM = 512
MOE_RB = 256
MOVE_TM = 256
MOVE_UNROLL = 8
VMEM_LIMIT = 56 * 1024 * 1024


def _cparams(n_axes):
    return pltpu.CompilerParams(dimension_semantics=("arbitrary",) * n_axes,
                                vmem_limit_bytes=VMEM_LIMIT)


def _dot(a, b):
    return jnp.dot(a, b, preferred_element_type=F32)


def _dot_nt(a, b):
    return lax.dot_general(a, b, (((1,), (1,)), ((), ())), preferred_element_type=F32)


def _dot_tn(a, b):
    return lax.dot_general(a, b, (((0,), (0,)), ((), ())), preferred_element_type=F32)


def _rms_full(x, g):
    ms = jnp.mean(x * x, axis=-1, keepdims=True)
    return x * lax.rsqrt(ms + EPS) * g


def _seg_mean(x, avg):
    return _dot(x.astype(BF16), avg)


def _silu(x):
    return x * (1.0 / (1.0 + jnp.exp(-x)))


def _sigmoid(x):
    return 1.0 / (1.0 + jnp.exp(-x))


def _block_avg(width):
    i = np.arange(width)
    return ((i[:, None] // HEAD_DIM == i[None, :] // HEAD_DIM) / HEAD_DIM).astype(np.float32)


def _proj_body(x_ref, g_ref, w_ref, o_ref):
    h = _rms_full(x_ref[...], g_ref[...]).astype(BF16)
    step = PROJ_PAD // 3
    for j in range(3):
        o_ref[:, j * step:(j + 1) * step] = _dot(h, w_ref[:, j * step:(j + 1) * step]).astype(BF16)


def _proj(x2d, g, w):
    T = x2d.shape[0]
    return pl.pallas_call(
        _proj_body,
        grid=(T // PROJ_TM,),
        in_specs=[pl.BlockSpec((PROJ_TM, D_MODEL), lambda i: (i, 0)),
                  pl.BlockSpec((1, D_MODEL), lambda i: (0, 0)),
                  pl.BlockSpec((D_MODEL, PROJ_PAD), lambda i: (0, 0))],
        out_specs=pl.BlockSpec((PROJ_TM, PROJ_PAD), lambda i: (i, 0)),
        out_shape=jax.ShapeDtypeStruct((T, PROJ_PAD), BF16),
        compiler_params=_cparams(1),
        name="proj",
    )(x2d, g, w)


def _dup_groups(x):
    lane = lax.broadcasted_iota(jnp.int32, x.shape, 1)
    xs = pltpu.roll(x, HEAD_DIM, axis=1)
    lo = lane < HEAD_DIM
    return jnp.where(lo, x, xs), jnp.where(lo, xs, x)


def _ones_groups(x):
    lane = lax.broadcasted_iota(jnp.int32, x.shape, 1)
    lo = lane < HEAD_DIM
    return jnp.where(lo, x, 1.0), jnp.where(lo, pltpu.roll(x, HEAD_DIM, axis=1), 1.0)


def _nsa_prep_body(kvc_ref, ksv_ref, kwv_ref, pos_ref, w1_ref, w2_ref, gain_ref, avg_ref,
                   kcmp_ref, vcmp_ref, ks_ref, vs_ref, kw_ref, vw_ref, scr_k, scr_v):
    avg = avg_ref[...]
    n_c = scr_k.shape[0] // CMP_STRIDE
    scr_k[...] = kvc_ref[0, :, 0:LANES].astype(F32)
    scr_v[...] = kvc_ref[0, :, LANES:2 * LANES].astype(F32)
    for j, out_ref, scr in ((0, kcmp_ref, scr_k), (1, vcmp_ref, scr_v)):
        ycat = jnp.concatenate(
            [scr[pl.ds(l, n_c, stride=CMP_STRIDE), :] for l in range(CMP_STRIDE)], axis=1)
        first = _dot((ycat + pos_ref[j, 0]).astype(BF16), w1_ref[j, 0])
        second = _dot((ycat + pos_ref[j, 1]).astype(BF16), w1_ref[j, 1])
        hidden = first + pltpu.roll(second, n_c - 1, axis=0)
        cmp_tok = _dot(_silu(hidden).astype(BF16), w2_ref[j])
        if j == 0:
            ms = _seg_mean(cmp_tok * cmp_tok, avg)
            cmp_tok = cmp_tok * lax.rsqrt(ms + EPS) * gain_ref[0]
        d0, d1 = _dup_groups(cmp_tok) if j == 0 else _ones_groups(cmp_tok)
        out_ref[0, 0] = d0.astype(BF16)
        out_ref[0, 1] = d1.astype(BF16)

    for src_ref, k_out, v_out, gi in ((ksv_ref, ks_ref, vs_ref, 1), (kwv_ref, kw_ref, vw_ref, 2)):
        k = src_ref[0, :, 0:LANES].astype(F32)
        ms = _seg_mean(k * k, avg)
        k = k * lax.rsqrt(ms + EPS) * gain_ref[gi]
        d0, d1 = _dup_groups(k)
        k_out[0, 0] = d0.astype(BF16)
        k_out[0, 1] = d1.astype(BF16)
        d0, d1 = _ones_groups(src_ref[0, :, LANES:2 * LANES].astype(F32))
        v_out[0, 0] = d0.astype(BF16)
        v_out[0, 1] = d1.astype(BF16)


def _nsa_prep(proj3, pos, w1, w2, gains):
    B, S, _ = proj3.shape
    n_c = S // CMP_STRIDE
    avg = jnp.asarray(_block_avg(LANES), BF16)
    col = lambda c: pl.BlockSpec((1, S, 2 * LANES), lambda b: (b, 0, c // (2 * LANES)))
    full = lambda a: pl.BlockSpec(a.shape, lambda b: (0,) * a.ndim)
    cmp_spec = pl.BlockSpec((1, NSA_GROUPS, n_c, LANES), lambda b: (b, 0, 0, 0))
    seq_spec = pl.BlockSpec((1, NSA_GROUPS, S, LANES), lambda b: (b, 0, 0, 0))
    cmp_shape = jax.ShapeDtypeStruct((B, NSA_GROUPS, n_c, LANES), BF16)
    seq_shape = jax.ShapeDtypeStruct((B, NSA_GROUPS, S, LANES), BF16)
    return pl.pallas_call(
        _nsa_prep_body,
        grid=(B,),
        in_specs=[col(COL_KVC), col(COL_KSV), col(COL_KWV), full(pos), full(w1), full(w2), full(gains), full(avg)],
        out_specs=[cmp_spec, cmp_spec, seq_spec, seq_spec, seq_spec, seq_spec],
        out_shape=[cmp_shape, cmp_shape, seq_shape, seq_shape, seq_shape, seq_shape],
        scratch_shapes=[pltpu.VMEM((S, LANES), F32), pltpu.VMEM((S, LANES), F32)],
        compiler_params=_cparams(1),
        name="nsa_prep",
    )(proj3, proj3, proj3, pos, w1, w2, gains, avg)


def _nsa_attn_body(q_ref, gate_ref, kcmp_ref, vcmp_ref, ks_ref, vs_ref, kw_ref, vw_ref,
                   qgain_ref, ogain_ref, avgq_ref, avgo_ref, msct_ref, esel_ref, egate_ref, wbias_ref, dbias_ref,
                   o_ref, m_scr, acc_scr):
    tq = q_ref.shape[1]
    n_cmp = kcmp_ref.shape[2]
    n_sel = msct_ref.shape[0]
    kc_len = esel_ref.shape[2]
    rows = 4 * tq
    qi = pl.program_id(1)
    q0 = qi * tq

    q = q_ref[0].astype(F32)
    ms = _seg_mean(q * q, avgq_ref[...])
    qn = q * lax.rsqrt(ms + EPS) * qgain_ref[...] * (HEAD_DIM ** -0.5)

    gate_sig = _sigmoid(gate_ref[0].astype(F32)).astype(BF16)
    gates = [_dot(gate_sig, egate_ref[j]) for j in range(3)]

    lane_q = lax.broadcasted_iota(jnp.int32, (tq, LANES), 1)
    lo_q = lane_q < HEAD_DIM
    lo_r = lax.broadcasted_iota(jnp.int32, (rows, LANES), 1) < HEAD_DIM

    blk = lax.broadcasted_iota(jnp.int32, (n_sel, tq), 0)
    cur = lax.shift_right_logical(q0 + lax.broadcasted_iota(jnp.int32, (n_sel, tq), 1), int(np.log2(SEL_BLOCK)))
    forced = (blk == 0) | (blk == cur) | (blk == cur - 1)
    future = blk > cur
    blk_f = blk.astype(F32)

    def heads4(x):
        return jnp.concatenate([x] * 4, axis=0)

    def normalised_pairs(acc, guard):
        rolled = pltpu.roll(acc, HEAD_DIM, axis=1)
        den = jnp.where(lo_r, rolled, acc)
        if guard:
            den = jnp.maximum(den, 1e-30)
        out = []
        for p in range(2):
            ev = slice((2 * p) * tq, (2 * p + 1) * tq)
            od = slice((2 * p + 1) * tq, (2 * p + 2) * tq)
            out.append(jnp.where(lo_q, acc[ev] / den[ev], rolled[od] / den[od]))
        return out

    for g in range(NSA_GROUPS):
        slabs = [qn[:, (2 * g + p) * LANES:(2 * g + p + 1) * LANES] for p in range(2)]
        qs = jnp.concatenate(
            [jnp.where(lo_q, slabs[0], 0.0), jnp.where(lo_q, 0.0, slabs[0]),
             jnp.where(lo_q, slabs[1], 0.0), jnp.where(lo_q, 0.0, slabs[1])], axis=0).astype(BF16)

        s_c = _dot_nt(qs, kcmp_ref[0, g])
        r_c = lax.broadcasted_iota(jnp.int32, (rows, n_cmp), 0)
        c_c = lax.broadcasted_iota(jnp.int32, (rows, n_cmp), 1)
        cmask = (c_c * CMP_STRIDE + (CMP_BLOCK - 1)) <= q0 + (r_c & (tq - 1))
        s_c = jnp.where(cmask, s_c, NEG_INF)
        e_c = jnp.where(cmask, jnp.exp(s_c - jnp.max(s_c, axis=-1, keepdims=True)), 0.0)
        acc_c = _dot(e_c.astype(BF16), vcmp_ref[0, g])
        l_c = jnp.where(lo_r, pltpu.roll(acc_c, HEAD_DIM, axis=1), acc_c)
        p_c = e_c / jnp.maximum(l_c, 1e-30)

        p_sum = p_c[0:tq] + p_c[tq:2 * tq] + p_c[2 * tq:3 * tq] + p_c[3 * tq:4 * tq]
        p_hi = p_sum.astype(BF16)
        p_lo = (p_sum - p_hi.astype(F32)).astype(BF16)
        imp = _dot_nt(msct_ref[...], p_hi) + _dot_nt(msct_ref[...], p_lo)
        v = jnp.where(forced, FORCE_SCORE, jnp.where(future, NEG_INF, imp))
        sel = jnp.zeros((n_sel, tq), F32)
        for _ in range(SEL_TOPK):
            mx = jnp.max(v, axis=0, keepdims=True)
            first = jnp.min(jnp.where(v == mx, blk_f, float(LANES)), axis=0, keepdims=True)
            pick = blk_f == first
            sel = jnp.where(pick, 1.0, sel)
            v = jnp.where(pick, BELOW_ALL, v)
        sel_b = sel.astype(BF16)

        m_scr[...] = jnp.full(m_scr.shape, NEG_INF, F32)
        acc_scr[...] = jnp.zeros(acc_scr.shape, F32)

        def sel_chunk(kc, causal_bias):
            k0 = pl.multiple_of(kc * kc_len, kc_len)
            chosen = _dot_tn(sel_b, esel_ref[kc])
            bias = (chosen - 1.0) * (-NEG_INF)
            if causal_bias is not None:
                bias = bias + causal_bias
            s = _dot_nt(qs, ks_ref[0, g, pl.ds(k0, kc_len), :]) + heads4(bias)
            m_old = m_scr[...]
            m_new = jnp.maximum(m_old, jnp.max(s, axis=-1, keepdims=True))
            p = jnp.exp((s - m_new).astype(BF16))
            acc_scr[...] = jnp.exp(m_old - m_new) * acc_scr[...] + _dot(p, vs_ref[0, g, pl.ds(k0, kc_len), :])
            m_scr[...] = m_new

        n_before = lax.shift_right_logical(q0, int(np.log2(kc_len)))

        def before(kc, carry):
            sel_chunk(kc, None)
            return carry

        lax.fori_loop(0, n_before, before, 0)
        sel_chunk(n_before, dbias_ref[qi & (kc_len // tq - 1)])
        sel_s = normalised_pairs(acc_scr[...], False)

        w0 = pl.multiple_of(jnp.maximum(q0 - WINDOW, 0), tq)
        n_w = WINDOW + tq
        s_w = _dot_nt(qs, kw_ref[0, g, pl.ds(w0, n_w), :]) + heads4(wbias_ref[jnp.minimum(qi, WINDOW // tq)])
        p_w = jnp.exp((s_w - jnp.max(s_w, axis=-1, keepdims=True)).astype(BF16))
        win_s = normalised_pairs(_dot(p_w, vw_ref[0, g, pl.ds(w0, n_w), :]), False)

        cmp_s = normalised_pairs(acc_c, True)
        for p in range(2):
            cols = slice((2 * g + p) * LANES, (2 * g + p + 1) * LANES)
            mix = gates[0][:, cols] * cmp_s[p] + gates[1][:, cols] * sel_s[p] + gates[2][:, cols] * win_s[p]
            ms_o = _seg_mean(mix * mix, avgo_ref[...])
            o_ref[0, :, cols] = (mix * lax.rsqrt(ms_o + EPS) * ogain_ref[:, cols]).astype(BF16)


def _sel_from_cmp(n_cmp, n_sel):
    c0 = np.arange(n_cmp) * CMP_STRIDE
    s0 = np.arange(n_sel) * SEL_BLOCK
    ov = np.minimum(c0[None, :] + CMP_BLOCK, s0[:, None] + SEL_BLOCK) - np.maximum(c0[None, :], s0[:, None])
    m = (np.clip(ov, 0, None) / CMP_BLOCK).astype(np.float32)
    m[:, (np.arange(n_cmp) * CMP_STRIDE + CMP_BLOCK) > n_sel * SEL_BLOCK] = 0.0
    return m


def _nsa_attn(proj3, kcmp, vcmp, ks, vs, kw, vw, q_gain, o_gain):
    B, S, _ = proj3.shape
    n_cmp = kcmp.shape[2]
    n_sel = S // SEL_BLOCK
    tq = NSA_TQ
    assert n_sel % 8 == 0 and S % SEL_KC == 0 and SEL_KC % tq == 0 and WINDOW % tq == 0 and S >= WINDOW + tq
    avgq = jnp.asarray(_block_avg(NSA_WIDTH), BF16)
    avgo = jnp.asarray(_block_avg(LANES), BF16)
    msct = jnp.asarray(_sel_from_cmp(n_cmp, n_sel), BF16)
    esel = (np.arange(n_sel)[:, None] == np.arange(S)[None, :] // SEL_BLOCK).astype(np.float32)
    esel = jnp.asarray(esel.reshape(n_sel, S // SEL_KC, SEL_KC).transpose(1, 0, 2), BF16)
    src = np.arange(LANES)[:, None]
    dst = np.arange(NSA_WIDTH)[None, :]
    egate = jnp.asarray(np.stack([(src == (dst // HEAD_DIM) * 3 + j) for j in range(3)]).astype(np.float32), BF16)
    r = np.arange(tq)[:, None]
    n_w = WINDOW + tq
    wcases = []
    for i in range(WINDOW // tq + 1):
        diff = (i * tq - max(i * tq - WINDOW, 0)) + r - np.arange(n_w)[None, :]
        wcases.append(np.where((diff >= 0) & (diff < WINDOW), 0.0, NEG_INF))
    wbias = jnp.asarray(np.stack(wcases), F32)
    dbias = jnp.asarray(np.stack([np.where(np.arange(SEL_KC)[None, :] <= i * tq + r, 0.0, NEG_INF)
                                  for i in range(SEL_KC // tq)]), F32)

    full = lambda a: pl.BlockSpec(a.shape, lambda b, i: (0,) * a.ndim)
    per_b = lambda a: pl.BlockSpec((1,) + a.shape[1:], lambda b, i: (b,) + (0,) * (a.ndim - 1))
    return pl.pallas_call(
        _nsa_attn_body,
        grid=(B, S // tq),
        in_specs=[pl.BlockSpec((1, tq, NSA_WIDTH), lambda b, i: (b, i, COL_QA // NSA_WIDTH)),
                  pl.BlockSpec((1, tq, LANES), lambda b, i: (b, i, COL_GATE // LANES)),
                  per_b(kcmp), per_b(vcmp), per_b(ks), per_b(vs), per_b(kw), per_b(vw),
                  full(q_gain), full(o_gain), full(avgq), full(avgo), full(msct), full(esel), full(egate),
                  full(wbias), full(dbias)],
        out_specs=pl.BlockSpec((1, tq, NSA_WIDTH), lambda b, i: (b, i, 0)),
        out_shape=jax.ShapeDtypeStruct((B, S, NSA_WIDTH), BF16),
        scratch_shapes=[pltpu.VMEM((4 * tq, 1), F32), pltpu.VMEM((4 * tq, LANES), F32)],
        compiler_params=_cparams(2),
        name="nsa_attn",
    )(proj3, proj3, kcmp, vcmp, ks, vs, kw, vw, q_gain, o_gain, avgq, avgo, msct, esel, egate, wbias, dbias)


def _retention_body(q_ref, k_ref, v_ref, g_ref, cos_ref, sin_ref, decay_ref, xi_ref, zeta_ref, gammac_ref,
                    gain_ref, avg_ref, o_ref, state_scr):
    S = q_ref.shape[1]
    C = RET_CHUNK
    lane = lax.broadcasted_iota(jnp.int32, (C, LANES), 1)
    lo = lane < HEAD_DIM
    first_half = (lane & (HEAD_DIM - 1)) < HEAD_DIM // 2
    r = lax.broadcasted_iota(jnp.int32, (LANES, LANES), 0)
    c = lax.broadcasted_iota(jnp.int32, (LANES, LANES), 1)
    same_head = (r < HEAD_DIM) == (c < HEAD_DIM)
    avg = avg_ref[...]
    state_scr[...] = jnp.zeros(state_scr.shape, F32)

    def rope(x, cos, sin):
        swapped = jnp.where(first_half, pltpu.roll(x, LANES - HEAD_DIM // 2, axis=1),
                            pltpu.roll(x, HEAD_DIM // 2, axis=1))
        return x * cos + swapped * sin

    def chunk(n, carry):
        r0 = pl.multiple_of(n * C, C)
        cos = cos_ref[pl.ds(r0, C), :]
        sin = sin_ref[pl.ds(r0, C), :]
        for p in range(RET_HEADS // 2):
            cols = slice(p * LANES, (p + 1) * LANES)
            q = rope(q_ref[0, pl.ds(r0, C), cols].astype(F32), cos, sin)
            k = rope(k_ref[0, pl.ds(r0, C), cols].astype(F32), cos, sin) * (HEAD_DIM ** -0.5)
            vb = v_ref[0, pl.ds(r0, C), cols]
            kb = k.astype(BF16)
            outs = []
            for half in range(2):
                qm = jnp.where(lo if half == 0 else ~lo, q, 0.0).astype(BF16)
                inner = _dot_nt(qm, kb) * decay_ref[2 * p + half]
                outs.append(_dot(inner.astype(BF16), vb))
            y = jnp.where(lo, outs[0], outs[1])
            state = state_scr[p]
            y = y + _dot(q.astype(BF16), state.astype(BF16)) * xi_ref[p]
            upd = _dot_tn((k * zeta_ref[p]).astype(BF16), vb)
            state_scr[p] = gammac_ref[p] * state + jnp.where(same_head, upd, 0.0)
            mu = _seg_mean(y, avg)
            d = y - mu
            var = _seg_mean(d * d, avg)
            gate = g_ref[0, pl.ds(r0, C), cols].astype(F32)
            o_ref[0, pl.ds(r0, C), cols] = (_silu(gate) * (d * lax.rsqrt(var + EPS) * gain_ref[:, cols])).astype(BF16)
        return carry

    lax.fori_loop(0, S // C, chunk, 0)


def _retention_tables(S):
    half = HEAD_DIM // 2
    inv_freq = ROPE_BASE ** (-jnp.arange(half, dtype=F32) / half)
    ang = jnp.arange(S, dtype=F32)[:, None] * inv_freq[None, :]
    cos, sin = jnp.cos(ang), jnp.sin(ang)
    cos_t = jnp.tile(cos, (1, 4))
    sin_t = jnp.tile(jnp.concatenate([-sin, sin], axis=1), (1, 2))
    C = RET_CHUNK
    H = RET_HEADS
    log_gamma = jnp.log1p(-jnp.power(2.0, -5.0 - jnp.arange(H, dtype=F32)))
    i = jnp.arange(C, dtype=F32)
    rel = i[:, None] - i[None, :]
    decay = jnp.where(rel >= 0, jnp.exp(jnp.maximum(rel, 0.0)[None] * log_gamma[:, None, None]), 0.0)
    xi = jnp.exp((i + 1.0)[:, None] * log_gamma[None, :])
    zeta = jnp.exp((C - 1.0 - i)[:, None] * log_gamma[None, :])
    gamma_c = jnp.exp(C * log_gamma)
    per_pair = lambda t: jnp.repeat(t.T.reshape(H // 2, 2, -1), HEAD_DIM, axis=1).transpose(0, 2, 1)
    gammac = jnp.repeat(gamma_c.reshape(H // 2, 2), HEAD_DIM, axis=1)[:, None, :]
    return cos_t, sin_t, decay, per_pair(xi), per_pair(zeta), gammac


def _retention(proj3, gain):
    B, S, _ = proj3.shape
    cos_t, sin_t, decay, xi, zeta, gammac = _retention_tables(S)
    avg = jnp.asarray(_block_avg(LANES), BF16)
    col = lambda c: pl.BlockSpec((1, S, RET_WIDTH), lambda b: (b, 0, c // RET_WIDTH))
    full = lambda a: pl.BlockSpec(a.shape, lambda b: (0,) * a.ndim)
    return pl.pallas_call(
        _retention_body,
        grid=(B,),
        in_specs=[col(COL_QR), col(COL_KR), col(COL_VR), col(COL_GR), full(cos_t), full(sin_t), full(decay),
                  full(xi), full(zeta), full(gammac), full(gain), full(avg)],
        out_specs=pl.BlockSpec((1, S, RET_WIDTH), lambda b: (b, 0, 0)),
        out_shape=jax.ShapeDtypeStruct((B, S, RET_WIDTH), BF16),
        scratch_shapes=[pltpu.VMEM((RET_HEADS // 2, LANES, LANES), F32)],
        compiler_params=_cparams(1),
        name="retention",
    )(proj3, proj3, proj3, proj3, cos_t, sin_t, decay, xi, zeta, gammac, gain, avg)


def _mem_prep_body(mem_ref, g_ref, wkv_ref, kgain_ref, avg_ref, k_ref, v_ref):
    hm = _rms_full(mem_ref[0], g_ref[...]).astype(BF16)
    kv = _dot(hm, wkv_ref[...])
    k = kv[:, :MEM_WIDTH]
    ms = _seg_mean(k * k, avg_ref[...])
    k_ref[0] = (k * lax.rsqrt(ms + EPS) * kgain_ref[...]).astype(BF16)
    v_ref[0] = kv[:, MEM_WIDTH:].astype(BF16)


def _mem_prep(mem, g, wkv, kgain):
    B, M, _ = mem.shape
    avg = jnp.asarray(_block_avg(MEM_WIDTH), BF16)
    full = lambda a: pl.BlockSpec(a.shape, lambda b: (0,) * a.ndim)
    out_spec = pl.BlockSpec((1, M, MEM_WIDTH), lambda b: (b, 0, 0))
    out_shape = jax.ShapeDtypeStruct((B, M, MEM_WIDTH), BF16)
    return pl.pallas_call(
        _mem_prep_body,
        grid=(B,),
        in_specs=[pl.BlockSpec((1, M, D_MODEL), lambda b: (b, 0, 0)), full(g), full(wkv), full(kgain), full(avg)],
        out_specs=[out_spec, out_spec],
        out_shape=[out_shape, out_shape],
        compiler_params=_cparams(1),
        name="mem_prep",
    )(mem, g, wkv, kgain, avg)


def _post_body(x_ref, oa_ref, ob_ref, wout_ref, mk_ref, mv_ref, gx_ref, wq_ref, qgain_ref, avg_ref, wo_ref,
               gf_ref, wr_ref, br_ref, x2_ref, h_ref, route_ref):
    tm = x_ref.shape[1]
    x1 = x_ref[0] + _dot(oa_ref[0], wout_ref[0:NSA_WIDTH, :]) + _dot(ob_ref[0], wout_ref[NSA_WIDTH:, :])

    h = _rms_full(x1, gx_ref[...]).astype(BF16)
    q = _dot(h, wq_ref[...])
    ms = _seg_mean(q * q, avg_ref[...])
    q = q * lax.rsqrt(ms + EPS) * qgain_ref[...] * (HEAD_DIM ** -0.5)
    lane = lax.broadcasted_iota(jnp.int32, (tm, LANES), 1)
    lo = lane < HEAD_DIM
    slabs = []
    for p in range(MEM_HEADS // 2):
        cols = slice(p * LANES, (p + 1) * LANES)
        qp = q[:, cols]
        kp = mk_ref[0, :, cols]
        vp = mv_ref[0, :, cols]
        outs = []
        for half in range(2):
            qm = jnp.where(lo if half == 0 else ~lo, qp, 0.0).astype(BF16)
            s = _dot_nt(qm, kp)
            e = jnp.exp(s - jnp.max(s, axis=-1, keepdims=True))
            pr = e / jnp.sum(e, axis=-1, keepdims=True)
            outs.append(_dot(pr.astype(BF16), vp))
        slabs.append(jnp.where(lo, outs[0], outs[1]))
    o = jnp.concatenate(slabs, axis=1).astype(BF16)
    x2 = x1 + _dot(o, wo_ref[...])
    x2_ref[0] = x2

    hf = _rms_full(x2, gf_ref[...])
    h_ref[0] = hf
    logits = _dot(hf.astype(BF16), wr_ref[...]) + br_ref[...]
    lane_f = lane.astype(F32)
    big = float(LANES)
    gl = jnp.where(lane < N_GROUPS, logits, BELOW_ALL)
    gmax = jnp.max(gl, axis=-1, keepdims=True)
    grp = jnp.min(jnp.where(gl == gmax, lane_f, big), axis=-1, keepdims=True)
    g_w = 1.0 / jnp.sum(jnp.where(lane < N_GROUPS, jnp.exp(gl - gmax), 0.0), axis=-1, keepdims=True)
    e_lo = N_GROUPS + grp * EXPERTS_PER_GROUP
    el = jnp.where((lane_f >= e_lo) & (lane_f < e_lo + EXPERTS_PER_GROUP), logits, BELOW_ALL)
    v0 = jnp.max(el, axis=-1, keepdims=True)
    i0 = jnp.min(jnp.where(el == v0, lane_f, big), axis=-1, keepdims=True)
    el = jnp.where(lane_f == i0, BELOW_ALL, el)
    v1 = jnp.max(el, axis=-1, keepdims=True)
    i1 = jnp.min(jnp.where(el == v1, lane_f, big), axis=-1, keepdims=True)
    e1 = jnp.exp(v1 - v0)
    w0 = g_w / (1.0 + e1)
    w1 = g_w * e1 / (1.0 + e1)
    route = jnp.where(lane == 0, i0 - N_GROUPS,
                      jnp.where(lane == 1, i1 - N_GROUPS,
                                jnp.where(lane == 2, w0, jnp.where(lane == 3, w1, 0.0))))
    route_ref[0] = route


def _post(x, oa, ob, wout, mk, mv, gx, wq, qgain, wo, gf, wr, br):
    B, S, _ = x.shape
    tm = POST_TM
    avg = jnp.asarray(_block_avg(MEM_WIDTH), BF16)
    full = lambda a: pl.BlockSpec(a.shape, lambda b, i: (0,) * a.ndim)
    per_b = lambda a: pl.BlockSpec((1,) + a.shape[1:], lambda b, i: (b,) + (0,) * (a.ndim - 1))
    tile = lambda w: pl.BlockSpec((1, tm, w), lambda b, i: (b, i, 0))
    return pl.pallas_call(
        _post_body,
        grid=(B, S // tm),
        in_specs=[tile(D_MODEL), tile(NSA_WIDTH), tile(RET_WIDTH), full(wout), per_b(mk), per_b(mv), full(gx),
                  full(wq), full(qgain), full(avg), full(wo), full(gf), full(wr), full(br)],
        out_specs=[tile(D_MODEL), tile(D_MODEL), tile(LANES)],
        out_shape=[jax.ShapeDtypeStruct((B, S, D_MODEL), F32), jax.ShapeDtypeStruct((B, S, D_MODEL), F32),
                   jax.ShapeDtypeStruct((B, S, LANES), F32)],
        compiler_params=_cparams(2),
        name="post_mixer",
    )(x, oa, ob, wout, mk, mv, gx, wq, qgain, avg, wo, gf, wr, br)


def _rank_body(route_ref, tri_ref, rank_ref, count_ref, carry):
    @pl.when(pl.program_id(0) == 0)
    def _():
        carry[...] = jnp.zeros(carry.shape, F32)

    route = route_ref[...]
    tm = route.shape[0]
    lane = lax.broadcasted_iota(jnp.int32, (tm, LANES), 1).astype(F32)
    oh0 = lane == route[:, 0:1]
    oh1 = lane == route[:, 1:2]
    both = jnp.where(oh0, 1.0, 0.0) + jnp.where(oh1, 1.0, 0.0)
    before = _dot(tri_ref[...], both.astype(BF16)) + carry[...]
    r0 = jnp.sum(jnp.where(oh0, before, 0.0), axis=-1, keepdims=True)
    r1 = jnp.sum(jnp.where(oh1, before, 0.0), axis=-1, keepdims=True)
    lane_i = lax.broadcasted_iota(jnp.int32, (tm, LANES), 1)
    rank_ref[...] = jnp.where(lane_i == 0, r0, jnp.where(lane_i == 1, r1, 0.0))
    total = carry[...] + jnp.sum(both, axis=0, keepdims=True)
    carry[...] = total
    count_ref[...] = jnp.broadcast_to(total, count_ref.shape)


def _rank(route2d):
    T = route2d.shape[0]
    tm = RANK_TM
    tri = jnp.asarray(np.tril(np.ones((tm, tm), np.float32), -1), BF16)
    return pl.pallas_call(
        _rank_body,
        grid=(T // tm,),
        in_specs=[pl.BlockSpec((tm, LANES), lambda i: (i, 0)), pl.BlockSpec((tm, tm), lambda i: (0, 0))],
        out_specs=[pl.BlockSpec((tm, LANES), lambda i: (i, 0)), pl.BlockSpec((8, LANES), lambda i: (0, 0))],
        out_shape=[jax.ShapeDtypeStruct((T, LANES), F32), jax.ShapeDtypeStruct((8, LANES), F32)],
        scratch_shapes=[pltpu.VMEM((1, LANES), F32)],
        compiler_params=_cparams(1),
        name="moe_rank",
    )(route2d, tri)


def _row_copy(src, dst, sem):
    return pltpu.make_async_copy(src, dst, sem)


def _scatter_body(dest_ref, h_ref, xs_in_ref, xs_ref, sem):
    del xs_in_ref
    tm = h_ref.shape[0]

    def issue(j, carry):
        for u in range(MOVE_UNROLL):
            i = j * MOVE_UNROLL + u
            for s in range(2):
                _row_copy(h_ref.at[pl.ds(i, 1)], xs_ref.at[pl.ds(dest_ref[2 * i + s], 1)], sem).start()
        return carry

    lax.fori_loop(0, tm // MOVE_UNROLL, issue, 0)
    for _ in range(2):
        _row_copy(h_ref, xs_ref.at[pl.ds(0, tm)], sem).wait()


def _scatter_rows(dest_flat, h2d, n_rows):
    T = h2d.shape[0]
    tm = MOVE_TM
    zeros = jnp.zeros((n_rows, D_MODEL), F32)
    return pl.pallas_call(
        _scatter_body,
        grid=(T // tm,),
        in_specs=[pl.BlockSpec((2 * tm,), lambda i: (i,), memory_space=pltpu.SMEM),
                  pl.BlockSpec((tm, D_MODEL), lambda i: (i, 0)),
                  pl.BlockSpec(memory_space=pl.ANY)],
        out_specs=pl.BlockSpec(memory_space=pl.ANY),
        out_shape=jax.ShapeDtypeStruct((n_rows, D_MODEL), F32),
        scratch_shapes=[pltpu.SemaphoreType.DMA(())],
        input_output_aliases={2: 0},
        compiler_params=_cparams(1),
        name="moe_scatter",
    )(dest_flat, h2d, zeros)


def _expert_body(blk_e_ref, n_act_ref, xs_ref, wg_ref, wu_ref, wd_ref, ys_ref):
    i = pl.program_id(0)

    @pl.when(i < n_act_ref[0])
    def _():
        x = xs_ref[...].astype(BF16)
        a = _dot(x, wg_ref[0].astype(BF16))
        b = _dot(x, wu_ref[0].astype(BF16))
        ys_ref[...] = _dot((_silu(a) * b).astype(BF16), wd_ref[0].astype(BF16))

    @pl.when(i >= n_act_ref[0])
    def _():
        ys_ref[...] = jnp.zeros(ys_ref.shape, F32)


def _experts(blk_e, n_act, xs, wg, wu, wd):
    n_rows = xs.shape[0]
    rb = MOE_RB
    grid_spec = pltpu.PrefetchScalarGridSpec(
        num_scalar_prefetch=2,
        grid=(n_rows // rb,),
        in_specs=[pl.BlockSpec((rb, D_MODEL), lambda i, be, na: (i, 0)),
                  pl.BlockSpec((1, D_MODEL, EXPERT_FF), lambda i, be, na: (be[i], 0, 0)),
                  pl.BlockSpec((1, D_MODEL, EXPERT_FF), lambda i, be, na: (be[i], 0, 0)),
                  pl.BlockSpec((1, EXPERT_FF, D_MODEL), lambda i, be, na: (be[i], 0, 0))],
        out_specs=pl.BlockSpec((rb, D_MODEL), lambda i, be, na: (i, 0)),
    )
    return pl.pallas_call(
        _expert_body,
        grid_spec=grid_spec,
        out_shape=jax.ShapeDtypeStruct((n_rows, D_MODEL), F32),
        compiler_params=_cparams(1),
        name="moe_experts",
    )(blk_e, n_act, xs, wg, wu, wd)


def _combine_body(dest_ref, x_ref, route_ref, ys_ref, o_ref, buf, sem):
    tm = x_ref.shape[0]

    def issue(j, carry):
        for u in range(MOVE_UNROLL):
            i = j * MOVE_UNROLL + u
            for s in range(2):
                _row_copy(ys_ref.at[pl.ds(dest_ref[2 * i + s], 1)], buf.at[s, pl.ds(i, 1)], sem).start()
        return carry

    lax.fori_loop(0, tm // MOVE_UNROLL, issue, 0)
    for s in range(2):
        _row_copy(ys_ref.at[pl.ds(0, tm)], buf.at[s], sem).wait()
    route = route_ref[...]
    o_ref[...] = x_ref[...] + (route[:, 2:3] * buf[0] + route[:, 3:4] * buf[1])


def _combine(dest_flat, x2d, route2d, ys):
    T = x2d.shape[0]
    tm = MOVE_TM
    return pl.pallas_call(
        _combine_body,
        grid=(T // tm,),
        in_specs=[pl.BlockSpec((2 * tm,), lambda i: (i,), memory_space=pltpu.SMEM),
                  pl.BlockSpec((tm, D_MODEL), lambda i: (i, 0)),
                  pl.BlockSpec((tm, LANES), lambda i: (i, 0)),
                  pl.BlockSpec(memory_space=pl.ANY)],
        out_specs=pl.BlockSpec((tm, D_MODEL), lambda i: (i, 0)),
        out_shape=jax.ShapeDtypeStruct((T, D_MODEL), F32),
        scratch_shapes=[pltpu.VMEM((2, tm, D_MODEL), F32), pltpu.SemaphoreType.DMA(())],
        compiler_params=_cparams(1),
        name="moe_combine",
    )(dest_flat, x2d, route2d, ys)


def _permute_w_in(w):
    kv0 = NSA_WIDTH
    gate0 = kv0 + 6 * NSA_GROUPS * HEAD_DIM
    ret0 = gate0 + 3 * NSA_HEADS
    pad = jnp.zeros((w.shape[0], PROJ_PAD - w.shape[1]), w.dtype)
    return jnp.concatenate([w[:, :kv0], w[:, ret0:], w[:, kv0:gate0], w[:, gate0:ret0], pad], axis=1).astype(BF16)


def _compress_weights(pos, w1, w2):
    eye = jnp.eye(NSA_GROUPS, dtype=F32)
    w1r = w1.reshape(CMP_BLOCK, HEAD_DIM, CMP_HIDDEN)
    w1b = jnp.einsum('ldh,gk->lgdkh', w1r, eye).reshape(2, CMP_STRIDE * LANES, NSA_GROUPS * CMP_HIDDEN)
    w2b = jnp.einsum('hd,gk->ghkd', w2, eye).reshape(NSA_GROUPS * CMP_HIDDEN, LANES)
    posb = jnp.tile(pos, (1, NSA_GROUPS)).reshape(2, 1, CMP_STRIDE * LANES)
    return posb, w1b.astype(BF16), w2b.astype(BF16)


def _dup2(g):
    return jnp.tile(g.reshape(1, HEAD_DIM), (1, 2))


def kernel(x, mem, mix_norm, w_in, nsa_q_norm, nsa_kcmp_norm, nsa_ksel_norm, nsa_kwin_norm, cmp_pos_k, cmp_pos_v, cmp_k_w1, cmp_k_w2, cmp_v_w1, cmp_v_w2, nsa_out_norm, ret_out_norm, w_out, mem_x_norm, mem_kv_norm, mem_wq, mem_wkv, mem_q_norm, mem_k_norm, mem_wo, ffn_norm, router_group_w, router_group_b, router_expert_w, router_expert_b, exp_w_gate, exp_w_up, exp_w_down):
    B, S, D = x.shape
    T = B * S
    depth = mix_norm.shape[0]
    for l in range(depth):
        proj = _proj(x.reshape(T, D), mix_norm[l].reshape(1, D), _permute_w_in(w_in[l])).reshape(B, S, PROJ_PAD)
        pk, w1k, w2k = _compress_weights(cmp_pos_k[l], cmp_k_w1[l], cmp_k_w2[l])
        pv, w1v, w2v = _compress_weights(cmp_pos_v[l], cmp_v_w1[l], cmp_v_w2[l])
        gains = jnp.stack([_dup2(nsa_kcmp_norm[l]), _dup2(nsa_ksel_norm[l]), _dup2(nsa_kwin_norm[l])])
        kcmp, vcmp, ks, vs, kw, vw = _nsa_prep(proj, jnp.stack([pk, pv]), jnp.stack([w1k, w1v]),
                                               jnp.stack([w2k, w2v]), gains)
        o_a = _nsa_attn(proj, kcmp, vcmp, ks, vs, kw, vw,
                        jnp.tile(nsa_q_norm[l].reshape(1, HEAD_DIM), (1, NSA_HEADS)),
                        nsa_out_norm[l].reshape(1, NSA_WIDTH))
        o_b = _retention(proj, ret_out_norm[l].reshape(1, RET_WIDTH))
        mk, mv = _mem_prep(mem, mem_kv_norm[l].reshape(1, D), mem_wkv[l].astype(BF16),
                           jnp.tile(mem_k_norm[l].reshape(1, HEAD_DIM), (1, MEM_HEADS)))
        w_r = jnp.concatenate([router_group_w[l],
                               router_expert_w[l].transpose(1, 0, 2).reshape(D, N_EXPERTS),
                               jnp.zeros((D, LANES - N_GROUPS - N_EXPERTS), F32)], axis=1).astype(BF16)
        b_r = jnp.concatenate([router_group_b[l], router_expert_b[l].reshape(N_EXPERTS),
                               jnp.zeros((LANES - N_GROUPS - N_EXPERTS,), F32)]).reshape(1, LANES)
        x2, hf, route = _post(x, o_a, o_b, w_out[l].astype(BF16), mk, mv, mem_x_norm[l].reshape(1, D),
                              mem_wq[l].astype(BF16), jnp.tile(mem_q_norm[l].reshape(1, HEAD_DIM), (1, MEM_HEADS)),
                              mem_wo[l].astype(BF16), ffn_norm[l].reshape(1, D), w_r, b_r)
        route2d = route.reshape(T, LANES)
        rank, counts = _rank(route2d)
        counts = counts[0, :N_EXPERTS].astype(jnp.int32)
        padded = (counts + MOE_RB - 1) // MOE_RB * MOE_RB
        pend = jnp.cumsum(padded)
        pstart = pend - padded
        eid = route2d[:, 0:2].astype(jnp.int32)
        dest = (pstart[eid] + rank[:, 0:2].astype(jnp.int32)).reshape(2 * T)
        n_rows = 2 * T + N_EXPERTS * MOE_RB
        n_blocks = n_rows // MOE_RB
        blk_row0 = jnp.arange(n_blocks, dtype=jnp.int32) * MOE_RB
        blk_e = jnp.minimum(jnp.sum((pend[None, :] <= blk_row0[:, None]).astype(jnp.int32), axis=1), N_EXPERTS - 1)
        n_act = (pend[-1:] // MOE_RB).astype(jnp.int32)
        xs = _scatter_rows(dest, hf.reshape(T, D), n_rows)
        ys = _experts(blk_e, n_act, xs, exp_w_gate[l], exp_w_up[l], exp_w_down[l])
        x = _combine(dest, x2.reshape(T, D), route2d, ys).reshape(B, S, D)
    return x
```

```python
import functools

import numpy as np
import jax
import jax.numpy as jnp
from jax import lax
from jax.experimental import pallas as pl
from jax.experimental.pallas import tpu as pltpu

F32 = jnp.float32
BF16 = jnp.bfloat16

D_MODEL = 1024
HEAD_DIM = 64
LANES = 128
NSA_HEADS = 8
NSA_GROUPS = 2
NSA_WIDTH = NSA_HEADS * HEAD_DIM
CMP_BLOCK = 32
CMP_STRIDE = 16
CMP_HIDDEN = 2 * HEAD_DIM
SEL_BLOCK = 64
SEL_TOPK = 8
WINDOW = 512
RET_HEADS = 8
RET_WIDTH = RET_HEADS * HEAD_DIM
RET_CHUNK = 128
ROPE_BASE = 10000.0
MEM_HEADS = 4
MEM_WIDTH = MEM_HEADS * HEAD_DIM
N_GROUPS = 4
EXPERTS_PER_GROUP = 8
N_EXPERTS = N_GROUPS * EXPERTS_PER_GROUP
EXPERT_FF = D_MODEL // 4
EPS = 1e-6
NEG_INF = -1e30
FORCE_SCORE = 1e9
BELOW_ALL = -3e38
MAX_FIXED_SHIFT = 40.0

COL_QA = 0
COL_QR, COL_KR, COL_VR, COL_GR = 512, 1024, 1536, 2048
COL_KVC, COL_KSV, COL_KWV = 2560, 2816, 3072
COL_GATE = 3328
PROJ_PAD = 3456

PROJ_TM = 512
NSA_TQ = 256
SEL_KC = 512
WIN_KEYS = WINDOW + NSA_TQ
POST_TM = 512
POST_CHAINS = 2
RANK_TM = 512
MOE_RB = 256
MOVE_TM = 256
MOVE_UNROLL = 8
VMEM_LIMIT = 56 * 1024 * 1024


def _cparams(n_axes):
    return pltpu.CompilerParams(dimension_semantics=("arbitrary",) * n_axes,
                                vmem_limit_bytes=VMEM_LIMIT)


def _dot(a, b):
    return jnp.dot(a, b, preferred_element_type=F32)


def _dot_nt(a, b):
    return lax.dot_general(a, b, (((1,), (1,)), ((), ())), preferred_element_type=F32)


def _dot_tn(a, b):
    return lax.dot_general(a, b, (((0,), (0,)), ((), ())), preferred_element_type=F32)


def _rms_full(x, g):
    ms = jnp.mean(x * x, axis=-1, keepdims=True)
    return x * lax.rsqrt(ms + EPS) * g


def _seg_mean(x, avg):
    return _dot(x.astype(BF16), avg)


def _silu(x):
    return x * (1.0 / (1.0 + jnp.exp(-x)))


def _sigmoid(x):
    return 1.0 / (1.0 + jnp.exp(-x))


def _block_avg(width):
    i = np.arange(width)
    return ((i[:, None] // HEAD_DIM == i[None, :] // HEAD_DIM) / HEAD_DIM).astype(np.float32)


def _proj_body(x_ref, g_ref, w_ref, o_ref):
    h = _rms_full(x_ref[...], g_ref[...]).astype(BF16)
    step = PROJ_PAD // 3
    for j in range(3):
        o_ref[:, j * step:(j + 1) * step] = _dot(h, w_ref[:, j * step:(j + 1) * step]).astype(BF16)


def _proj(x2d, g, w):
    T = x2d.shape[0]
    return pl.pallas_call(
        _proj_body,
        grid=(T // PROJ_TM,),
        in_specs=[pl.BlockSpec((PROJ_TM, D_MODEL), lambda i: (i, 0)),
                  pl.BlockSpec((1, D_MODEL), lambda i: (0, 0)),
                  pl.BlockSpec((D_MODEL, PROJ_PAD), lambda i: (0, 0))],
        out_specs=pl.BlockSpec((PROJ_TM, PROJ_PAD), lambda i: (i, 0)),
        out_shape=jax.ShapeDtypeStruct((T, PROJ_PAD), BF16),
        compiler_params=_cparams(1),
        name="proj",
    )(x2d, g, w)


def _dup_groups(x):
    lane = lax.broadcasted_iota(jnp.int32, x.shape, 1)
    xs = pltpu.roll(x, HEAD_DIM, axis=1)
    lo = lane < HEAD_DIM
    return jnp.where(lo, x, xs), jnp.where(lo, xs, x)


def _ones_groups(x):
    lane = lax.broadcasted_iota(jnp.int32, x.shape, 1)
    lo = lane < HEAD_DIM
    return jnp.where(lo, x, 1.0), jnp.where(lo, pltpu.roll(x, HEAD_DIM, axis=1), 1.0)


def _nsa_prep_body(kvc_ref, ksv_ref, kwv_ref, pos_ref, w1_ref, w2_ref, gain_ref, avg_ref,
                   kcmp_ref, vcmp_ref, ks_ref, vs_ref, kw_ref, vw_ref, scr_k, scr_v):
    avg = avg_ref[...]
    n_c = scr_k.shape[0] // CMP_STRIDE
    scr_k[...] = kvc_ref[0, :, 0:LANES].astype(F32)
    scr_v[...] = kvc_ref[0, :, LANES:2 * LANES].astype(F32)
    for j, out_ref, scr in ((0, kcmp_ref, scr_k), (1, vcmp_ref, scr_v)):
        ycat = jnp.concatenate(
            [scr[pl.ds(l, n_c, stride=CMP_STRIDE), :] for l in range(CMP_STRIDE)], axis=1)
        first = _dot((ycat + pos_ref[j, 0]).astype(BF16), w1_ref[j, 0])
        second = _dot((ycat + pos_ref[j, 1]).astype(BF16), w1_ref[j, 1])
        hidden = first + pltpu.roll(second, n_c - 1, axis=0)
        cmp_tok = _dot(_silu(hidden).astype(BF16), w2_ref[j])
        if j == 0:
            ms = _seg_mean(cmp_tok * cmp_tok, avg)
            cmp_tok = cmp_tok * lax.rsqrt(ms + EPS) * gain_ref[0]
        d0, d1 = _dup_groups(cmp_tok) if j == 0 else _ones_groups(cmp_tok)
        out_ref[0, 0] = d0.astype(BF16)
        out_ref[0, 1] = d1.astype(BF16)

    for src_ref, k_out, v_out, gi in ((ksv_ref, ks_ref, vs_ref, 1), (kwv_ref, kw_ref, vw_ref, 2)):
        k = src_ref[0, :, 0:LANES].astype(F32)
        ms = _seg_mean(k * k, avg)
        k = k * lax.rsqrt(ms + EPS) * gain_ref[gi]
        d0, d1 = _dup_groups(k)
        k_out[0, 0] = d0.astype(BF16)
        k_out[0, 1] = d1.astype(BF16)
        d0, d1 = _ones_groups(src_ref[0, :, LANES:2 * LANES].astype(F32))
        v_out[0, 0] = d0.astype(BF16)
        v_out[0, 1] = d1.astype(BF16)


def _nsa_prep(proj3, pos, w1, w2, gains):
    B, S, _ = proj3.shape
    n_c = S // CMP_STRIDE
    avg = jnp.asarray(_block_avg(LANES), BF16)
    col = lambda c: pl.BlockSpec((1, S, 2 * LANES), lambda b: (b, 0, c // (2 * LANES)))
    full = lambda a: pl.BlockSpec(a.shape, lambda b: (0,) * a.ndim)
    cmp_spec = pl.BlockSpec((1, NSA_GROUPS, n_c, LANES), lambda b: (b, 0, 0, 0))
    seq_spec = pl.BlockSpec((1, NSA_GROUPS, S, LANES), lambda b: (b, 0, 0, 0))
    cmp_shape = jax.ShapeDtypeStruct((B, NSA_GROUPS, n_c, LANES), BF16)
    seq_shape = jax.ShapeDtypeStruct((B, NSA_GROUPS, S, LANES), BF16)
    return pl.pallas_call(
        _nsa_prep_body,
        grid=(B,),
        in_specs=[col(COL_KVC), col(COL_KSV), col(COL_KWV), full(pos), full(w1), full(w2), full(gains), full(avg)],
        out_specs=[cmp_spec, cmp_spec, seq_spec, seq_spec, seq_spec, seq_spec],
        out_shape=[cmp_shape, cmp_shape, seq_shape, seq_shape, seq_shape, seq_shape],
        scratch_shapes=[pltpu.VMEM((S, LANES), F32), pltpu.VMEM((S, LANES), F32)],
        compiler_params=_cparams(1),
        name="nsa_prep",
    )(proj3, proj3, proj3, pos, w1, w2, gains, avg)


def _nsa_attn_body(q_ref, gate_ref, kcmp_ref, vcmp_ref, ks_ref, vs_ref, kw_ref, vw_ref,
                   qgain_ref, kgain_ref, ogain_ref, avgq_ref, avgo_ref, msct_ref, esel_ref, egate_ref, wbias_ref,
                   dbias_ref,
                   o_ref, m_scr, acc_scr):
    tq = q_ref.shape[1]
    n_cmp = kcmp_ref.shape[2]
    n_sel = msct_ref.shape[0]
    kc_len = esel_ref.shape[2]
    rows = 4 * tq
    qi = pl.program_id(1)
    q0 = qi * tq

    q = q_ref[0].astype(F32)
    ms = _seg_mean(q * q, avgq_ref[...])
    qn = q * lax.rsqrt(ms + EPS) * qgain_ref[...] * (HEAD_DIM ** -0.5)

    gate_sig = _sigmoid(gate_ref[0].astype(F32)).astype(BF16)
    gates = [_dot(gate_sig, egate_ref[j]) for j in range(3)]

    lane_q = lax.broadcasted_iota(jnp.int32, (tq, LANES), 1)
    lo_q = lane_q < HEAD_DIM
    lo_r = lax.broadcasted_iota(jnp.int32, (rows, LANES), 1) < HEAD_DIM

    blk = lax.broadcasted_iota(jnp.int32, (n_sel, tq), 0)
    cur = lax.shift_right_logical(q0 + lax.broadcasted_iota(jnp.int32, (n_sel, tq), 1), int(np.log2(SEL_BLOCK)))
    forced = (blk == 0) | (blk == cur) | (blk == cur - 1)
    future = blk > cur
    blk_f = blk.astype(F32)

    def heads4(x):
        return jnp.concatenate([x] * 4, axis=0)

    def normalised_pairs(acc, guard):
        rolled = pltpu.roll(acc, HEAD_DIM, axis=1)
        den = jnp.where(lo_r, rolled, acc)
        if guard:
            den = jnp.maximum(den, 1e-30)
        out = []
        for p in range(2):
            ev = slice((2 * p) * tq, (2 * p + 1) * tq)
            od = slice((2 * p + 1) * tq, (2 * p + 2) * tq)
            out.append(jnp.where(lo_q, acc[ev] / den[ev], rolled[od] / den[od]))
        return out

    groups = range(NSA_GROUPS)
    qs = []
    for g in groups:
        slabs = [qn[:, (2 * g + p) * LANES:(2 * g + p + 1) * LANES] for p in range(2)]
        qs.append(jnp.concatenate(
            [jnp.where(lo_q, slabs[0], 0.0), jnp.where(lo_q, 0.0, slabs[0]),
             jnp.where(lo_q, slabs[1], 0.0), jnp.where(lo_q, 0.0, slabs[1])], axis=0).astype(BF16))

    r_c = lax.broadcasted_iota(jnp.int32, (rows, n_cmp), 0)
    c_c = lax.broadcasted_iota(jnp.int32, (rows, n_cmp), 1)
    cmask = (c_c * CMP_STRIDE + (CMP_BLOCK - 1)) <= q0 + (r_c & (tq - 1))
    s_c = [jnp.where(cmask, _dot_nt(qs[g], kcmp_ref[0, g]), NEG_INF) for g in groups]
    e_c = [jnp.where(cmask, jnp.exp(s_c[g] - jnp.max(s_c[g], axis=-1, keepdims=True)), 0.0) for g in groups]
    acc_c = [_dot(e_c[g].astype(BF16), vcmp_ref[0, g]) for g in groups]
    l_c = [jnp.where(lo_r, pltpu.roll(acc_c[g], HEAD_DIM, axis=1), acc_c[g]) for g in groups]
    p_c = [e_c[g] / jnp.maximum(l_c[g], 1e-30) for g in groups]

    p_sum = [p_c[g][0:tq] + p_c[g][tq:2 * tq] + p_c[g][2 * tq:3 * tq] + p_c[g][3 * tq:4 * tq] for g in groups]
    p_hi = [p_sum[g].astype(BF16) for g in groups]
    p_lo = [(p_sum[g] - p_hi[g].astype(F32)).astype(BF16) for g in groups]
    imp = [_dot_nt(msct_ref[...], p_hi[g]) + _dot_nt(msct_ref[...], p_lo[g]) for g in groups]
    v = [jnp.where(forced, FORCE_SCORE, jnp.where(future, NEG_INF, imp[g])) for g in groups]
    sel = [jnp.zeros((n_sel, tq), F32) for g in groups]
    for _ in range(SEL_TOPK):
        mx = [jnp.max(v[g], axis=0, keepdims=True) for g in groups]
        first = [jnp.min(jnp.where(v[g] == mx[g], blk_f, float(LANES)), axis=0, keepdims=True) for g in groups]
        pick = [blk_f == first[g] for g in groups]
        sel = [jnp.where(pick[g], 1.0, sel[g]) for g in groups]
        v = [jnp.where(pick[g], BELOW_ALL, v[g]) for g in groups]
    sel_b = [sel[g].astype(BF16) for g in groups]

    cmp_s = [normalised_pairs(acc_c[g], True) for g in groups]
    n_before = lax.shift_right_logical(q0, int(np.log2(kc_len)))
    causal = dbias_ref[qi & (kc_len // tq - 1)]
    w0 = pl.multiple_of(jnp.maximum(q0 - WINDOW, 0), tq)
    n_w = WINDOW + tq
    w_case = jnp.minimum(qi, WINDOW // tq)

    def sel_keys(ref, g, kc):
        return ref[0, g, pl.ds(pl.multiple_of(kc * kc_len, kc_len), kc_len), :]

    def sel_scores(g, kc, causal_bias, shift):
        chosen = _dot_tn(sel_b[g], esel_ref[kc])
        bias = (chosen - 1.0) * (-NEG_INF)
        if causal_bias is not None:
            bias = bias + causal_bias
        if shift is not None:
            bias = bias - shift
        return _dot_nt(qs[g], sel_keys(ks_ref, g, kc)) + heads4(bias)

    def win_scores(g, shift):
        bias = wbias_ref[w_case] if shift is None else wbias_ref[w_case] - shift
        return _dot_nt(qs[g], kw_ref[0, g, pl.ds(w0, n_w), :]) + heads4(bias)

    def win_values(g):
        return vw_ref[0, g, pl.ds(w0, n_w), :]

    def finish(acc_w):
        for g in groups:
            sel_s = normalised_pairs(acc_scr[g], False)
            win_s = normalised_pairs(acc_w[g], False)
            for p in range(2):
                cols = slice((2 * g + p) * LANES, (2 * g + p + 1) * LANES)
                mix = gates[0][:, cols] * cmp_s[g][p] + gates[1][:, cols] * sel_s[p] + gates[2][:, cols] * win_s[p]
                ms_o = _seg_mean(mix * mix, avgo_ref[...])
                o_ref[0, :, cols] = (mix * lax.rsqrt(ms_o + EPS) * ogain_ref[:, cols]).astype(BF16)

    def fixed_shift_path(shift):
        def probs(s):
            return jnp.exp(s).astype(BF16)

        for g in groups:
            acc_scr[g] = jnp.zeros(acc_scr.shape[1:], F32)

        def before(kc, carry):
            s = [sel_scores(g, kc, None, shift) for g in groups]
            p = [probs(s[g]) for g in groups]
            for g in groups:
                acc_scr[g] = acc_scr[g] + _dot(p[g], sel_keys(vs_ref, g, kc))
            return carry

        lax.fori_loop(0, n_before, before, 0)
        s_d0 = sel_scores(0, n_before, causal, shift)
        s_d1 = sel_scores(1, n_before, causal, shift)
        p_d0 = probs(s_d0)
        s_w0 = win_scores(0, shift)
        acc_scr[0] = acc_scr[0] + _dot(p_d0, sel_keys(vs_ref, 0, n_before))
        p_d1 = probs(s_d1)
        s_w1 = win_scores(1, shift)
        acc_scr[1] = acc_scr[1] + _dot(p_d1, sel_keys(vs_ref, 1, n_before))
        p_w0 = probs(s_w0)
        acc_w0 = _dot(p_w0, win_values(0))
        p_w1 = probs(s_w1)
        acc_w1 = _dot(p_w1, win_values(1))
        finish([acc_w0, acc_w1])

    def online_path():
        for g in groups:
            m_scr[g] = jnp.full(m_scr.shape[1:], NEG_INF, F32)
            acc_scr[g] = jnp.zeros(acc_scr.shape[1:], F32)

        def sel_softmax(g, s):
            m_old = m_scr[g]
            m_new = jnp.maximum(m_old, jnp.max(s, axis=-1, keepdims=True))
            m_scr[g] = m_new
            return jnp.exp(s - m_new).astype(BF16), jnp.exp(m_old - m_new)

        def sel_accumulate(g, kc, p, alpha):
            acc_scr[g] = alpha * acc_scr[g] + _dot(p, sel_keys(vs_ref, g, kc))

        def win_softmax(s):
            return jnp.exp(s - jnp.max(s, axis=-1, keepdims=True)).astype(BF16)

        def before(kc, carry):
            s = [sel_scores(g, kc, None, None) for g in groups]
            pa = [sel_softmax(g, s[g]) for g in groups]
            for g in groups:
                sel_accumulate(g, kc, *pa[g])
            return carry

        lax.fori_loop(0, n_before, before, 0)
        s_d0 = sel_scores(0, n_before, causal, None)
        s_d1 = sel_scores(1, n_before, causal, None)
        pa0 = sel_softmax(0, s_d0)
        s_w0 = win_scores(0, None)
        sel_accumulate(0, n_before, *pa0)
        pa1 = sel_softmax(1, s_d1)
        s_w1 = win_scores(1, None)
        sel_accumulate(1, n_before, *pa1)
        acc_w0 = _dot(win_softmax(s_w0), win_values(0))
        acc_w1 = _dot(win_softmax(s_w1), win_values(1))
        finish([acc_w0, acc_w1])

    bound = 1.01 * (HEAD_DIM ** 0.5) * jnp.max(jnp.abs(qgain_ref[...])) * jnp.max(jnp.abs(kgain_ref[...]))
    safe = bound <= MAX_FIXED_SHIFT
    pl.when(safe)(lambda: fixed_shift_path(bound))
    pl.when(jnp.logical_not(safe))(online_path)


def _sel_from_cmp(n_cmp, n_sel):
    c0 = np.arange(n_cmp) * CMP_STRIDE
    s0 = np.arange(n_sel) * SEL_BLOCK
    ov = np.minimum(c0[None, :] + CMP_BLOCK, s0[:, None] + SEL_BLOCK) - np.maximum(c0[None, :], s0[:, None])
    m = (np.clip(ov, 0, None) / CMP_BLOCK).astype(np.float32)
    m[:, (np.arange(n_cmp) * CMP_STRIDE + CMP_BLOCK) > n_sel * SEL_BLOCK] = 0.0
    return m


def _nsa_attn(proj3, kcmp, vcmp, ks, vs, kw, vw, q_gain, k_gains, o_gain):
    B, S, _ = proj3.shape
    n_cmp = kcmp.shape[2]
    n_sel = S // SEL_BLOCK
    tq = NSA_TQ
    assert n_sel % 8 == 0 and S % SEL_KC == 0 and SEL_KC % tq == 0 and WINDOW % tq == 0 and S >= WINDOW + tq
    avgq = jnp.asarray(_block_avg(NSA_WIDTH), BF16)
    avgo = jnp.asarray(_block_avg(LANES), BF16)
    msct = jnp.asarray(_sel_from_cmp(n_cmp, n_sel), BF16)
    esel = (np.arange(n_sel)[:, None] == np.arange(S)[None, :] // SEL_BLOCK).astype(np.float32)
    esel = jnp.asarray(esel.reshape(n_sel, S // SEL_KC, SEL_KC).transpose(1, 0, 2), BF16)
    src = np.arange(LANES)[:, None]
    dst = np.arange(NSA_WIDTH)[None, :]
    egate = jnp.asarray(np.stack([(src == (dst // HEAD_DIM) * 3 + j) for j in range(3)]).astype(np.float32), BF16)
    r = np.arange(tq)[:, None]
    n_w = WINDOW + tq
    wcases = []
    for i in range(WINDOW // tq + 1):
        diff = (i * tq - max(i * tq - WINDOW, 0)) + r - np.arange(n_w)[None, :]
        wcases.append(np.where((diff >= 0) & (diff < WINDOW), 0.0, NEG_INF))
    wbias = jnp.asarray(np.stack(wcases), F32)
    dbias = jnp.asarray(np.stack([np.where(np.arange(SEL_KC)[None, :] <= i * tq + r, 0.0, NEG_INF)
                                  for i in range(SEL_KC // tq)]), F32)

    full = lambda a: pl.BlockSpec(a.shape, lambda b, i: (0,) * a.ndim)
    per_b = lambda a: pl.BlockSpec((1,) + a.shape[1:], lambda b, i: (b,) + (0,) * (a.ndim - 1))
    return pl.pallas_call(
        _nsa_attn_body,
        grid=(B, S // tq),
        in_specs=[pl.BlockSpec((1, tq, NSA_WIDTH), lambda b, i: (b, i, COL_QA // NSA_WIDTH)),
                  pl.BlockSpec((1, tq, LANES), lambda b, i: (b, i, COL_GATE // LANES)),
                  per_b(kcmp), per_b(vcmp), per_b(ks), per_b(vs), per_b(kw), per_b(vw),
                  full(q_gain), full(k_gains), full(o_gain), full(avgq), full(avgo), full(msct), full(esel), full(egate),
                  full(wbias), full(dbias)],
        out_specs=pl.BlockSpec((1, tq, NSA_WIDTH), lambda b, i: (b, i, 0)),
        out_shape=jax.ShapeDtypeStruct((B, S, NSA_WIDTH), BF16),
        scratch_shapes=[pltpu.VMEM((NSA_GROUPS, 4 * tq, 1), F32), pltpu.VMEM((NSA_GROUPS, 4 * tq, LANES), F32)],
        compiler_params=_cparams(2),
        name="nsa_attn",
    )(proj3, proj3, kcmp, vcmp, ks, vs, kw, vw, q_gain, k_gains, o_gain, avgq, avgo, msct, esel, egate, wbias, dbias)


def _retention_body(q_ref, k_ref, v_ref, g_ref, cos_ref, sin_ref, decay_ref, xi_ref, zeta_ref, gammac_ref,
                    gain_ref, avg_ref, o_ref, state_scr):
    S = q_ref.shape[1]
    C = RET_CHUNK
    lane = lax.broadcasted_iota(jnp.int32, (C, LANES), 1)
    lo = lane < HEAD_DIM
    first_half = (lane & (HEAD_DIM - 1)) < HEAD_DIM // 2
    r = lax.broadcasted_iota(jnp.int32, (LANES, LANES), 0)
    c = lax.broadcasted_iota(jnp.int32, (LANES, LANES), 1)
    same_head = (r < HEAD_DIM) == (c < HEAD_DIM)
    avg = avg_ref[...]
    state_scr[...] = jnp.zeros(state_scr.shape, F32)

    def rope(x, cos, sin):
        swapped = jnp.where(first_half, pltpu.roll(x, LANES - HEAD_DIM // 2, axis=1),
                            pltpu.roll(x, HEAD_DIM // 2, axis=1))
        return x * cos + swapped * sin

    def chunk(n, carry):
        r0 = pl.multiple_of(n * C, C)
        cos = cos_ref[pl.ds(r0, C), :]
        sin = sin_ref[pl.ds(r0, C), :]
        pairs = range(RET_HEADS // 2)
        cols = [slice(p * LANES, (p + 1) * LANES) for p in pairs]
        q = [rope(q_ref[0, pl.ds(r0, C), cols[p]].astype(F32), cos, sin) for p in pairs]
        k = [rope(k_ref[0, pl.ds(r0, C), cols[p]].astype(F32), cos, sin) * (HEAD_DIM ** -0.5) for p in pairs]
        vb = [v_ref[0, pl.ds(r0, C), cols[p]] for p in pairs]
        kb = [k[p].astype(BF16) for p in pairs]
        inner = [_dot_nt(jnp.where(lo if half == 0 else ~lo, q[p], 0.0).astype(BF16), kb[p])
                 * decay_ref[2 * p + half] for p in pairs for half in range(2)]
        state = [state_scr[p] for p in pairs]
        cross = [_dot(q[p].astype(BF16), state[p].astype(BF16)) * xi_ref[p] for p in pairs]
        upd = [_dot_tn((k[p] * zeta_ref[p]).astype(BF16), vb[p]) for p in pairs]
        outs = [_dot(inner[i].astype(BF16), vb[i // 2]) for i in range(RET_HEADS)]
        for p in pairs:
            state_scr[p] = gammac_ref[p] * state[p] + jnp.where(same_head, upd[p], 0.0)
        y = jnp.concatenate([jnp.where(lo, outs[2 * p], outs[2 * p + 1]) + cross[p] for p in pairs], axis=0)
        mu = _seg_mean(y, avg)
        d = y - mu
        var = _seg_mean(d * d, avg)
        yn = d * lax.rsqrt(var + EPS)
        for p in pairs:
            gate = g_ref[0, pl.ds(r0, C), cols[p]].astype(F32)
            o_ref[0, pl.ds(r0, C), cols[p]] = (_silu(gate) * (yn[p * C:(p + 1) * C] * gain_ref[:, cols[p]])).astype(BF16)
        return carry

    lax.fori_loop(0, S // C, chunk, 0)


def _retention_tables(S):
    half = HEAD_DIM // 2
    inv_freq = ROPE_BASE ** (-jnp.arange(half, dtype=F32) / half)
    ang = jnp.arange(S, dtype=F32)[:, None] * inv_freq[None, :]
    cos, sin = jnp.cos(ang), jnp.sin(ang)
    cos_t = jnp.tile(cos, (1, 4))
    sin_t = jnp.tile(jnp.concatenate([-sin, sin], axis=1), (1, 2))
    C = RET_CHUNK
    H = RET_HEADS
    log_gamma = jnp.log1p(-jnp.power(2.0, -5.0 - jnp.arange(H, dtype=F32)))
    i = jnp.arange(C, dtype=F32)
    rel = i[:, None] - i[None, :]
    decay = jnp.where(rel >= 0, jnp.exp(jnp.maximum(rel, 0.0)[None] * log_gamma[:, None, None]), 0.0)
    xi = jnp.exp((i + 1.0)[:, None] * log_gamma[None, :])
    zeta = jnp.exp((C - 1.0 - i)[:, None] * log_gamma[None, :])
    gamma_c = jnp.exp(C * log_gamma)
    per_pair = lambda t: jnp.repeat(t.T.reshape(H // 2, 2, -1), HEAD_DIM, axis=1).transpose(0, 2, 1)
    gammac = jnp.repeat(gamma_c.reshape(H // 2, 2), HEAD_DIM, axis=1)[:, None, :]
    return cos_t, sin_t, decay, per_pair(xi), per_pair(zeta), gammac


def _retention(proj3, gain):
    B, S, _ = proj3.shape
    cos_t, sin_t, decay, xi, zeta, gammac = _retention_tables(S)
    avg = jnp.asarray(_block_avg(LANES), BF16)
    col = lambda c: pl.BlockSpec((1, S, RET_WIDTH), lambda b: (b, 0, c // RET_WIDTH))
    full = lambda a: pl.BlockSpec(a.shape, lambda b: (0,) * a.ndim)
    return pl.pallas_call(
        _retention_body,
        grid=(B,),
        in_specs=[col(COL_QR), col(COL_KR), col(COL_VR), col(COL_GR), full(cos_t), full(sin_t), full(decay),
                  full(xi), full(zeta), full(gammac), full(gain), full(avg)],
        out_specs=pl.BlockSpec((1, S, RET_WIDTH), lambda b: (b, 0, 0)),
        out_shape=jax.ShapeDtypeStruct((B, S, RET_WIDTH), BF16),
        scratch_shapes=[pltpu.VMEM((RET_HEADS // 2, LANES, LANES), F32)],
        compiler_params=_cparams(1),
        name="retention",
    )(proj3, proj3, proj3, proj3, cos_t, sin_t, decay, xi, zeta, gammac, gain, avg)


def _mem_prep_body(mem_ref, g_ref, wkv_ref, kgain_ref, avg_ref, k_ref, v_ref):
    hm = _rms_full(mem_ref[0], g_ref[...]).astype(BF16)
    kv = _dot(hm, wkv_ref[...])
    k = kv[:, :MEM_WIDTH]
    ms = _seg_mean(k * k, avg_ref[...])
    k_ref[0] = (k * lax.rsqrt(ms + EPS) * kgain_ref[...]).astype(BF16)
    v_ref[0] = kv[:, MEM_WIDTH:].astype(BF16)


def _mem_prep(mem, g, wkv, kgain):
    B, M, _ = mem.shape
    avg = jnp.asarray(_block_avg(MEM_WIDTH), BF16)
    full = lambda a: pl.BlockSpec(a.shape, lambda b: (0,) * a.ndim)
    out_spec = pl.BlockSpec((1, M, MEM_WIDTH), lambda b: (b, 0, 0))
    out_shape = jax.ShapeDtypeStruct((B, M, MEM_WIDTH), BF16)
    return pl.pallas_call(
        _mem_prep_body,
        grid=(B,),
        in_specs=[pl.BlockSpec((1, M, D_MODEL), lambda b: (b, 0, 0)), full(g), full(wkv), full(kgain), full(avg)],
        out_specs=[out_spec, out_spec],
        out_shape=[out_shape, out_shape],
        compiler_params=_cparams(1),
        name="mem_prep",
    )(mem, g, wkv, kgain, avg)


def _post_body(x_ref, oa_ref, ob_ref, wout_ref, mk_ref, mv_ref, gx_ref, wq_ref, qgain_ref, avg_ref, wo_ref,
               gf_ref, wr_ref, br_ref, x2_ref, h_ref, route_ref):
    tm = x_ref.shape[1] // POST_CHAINS
    chains = range(POST_CHAINS)
    rows = [slice(c * tm, (c + 1) * tm) for c in chains]
    x1 = [x_ref[0, rows[c]] + _dot(oa_ref[0, rows[c]], wout_ref[0:NSA_WIDTH, :])
          + _dot(ob_ref[0, rows[c]], wout_ref[NSA_WIDTH:, :]) for c in chains]

    h = [_rms_full(x1[c], gx_ref[...]).astype(BF16) for c in chains]
    q = [_dot(h[c], wq_ref[...]) for c in chains]
    ms = [_seg_mean(q[c] * q[c], avg_ref[...]) for c in chains]
    q = [q[c] * lax.rsqrt(ms[c] + EPS) * qgain_ref[...] * (HEAD_DIM ** -0.5) for c in chains]
    lane = lax.broadcasted_iota(jnp.int32, (tm, LANES), 1)
    lo = lane < HEAD_DIM
    heads = [(c, p, half) for c in chains for p in range(MEM_HEADS // 2) for half in range(2)]
    s = [_dot_nt(jnp.where(lo if half == 0 else ~lo, q[c][:, p * LANES:(p + 1) * LANES], 0.0).astype(BF16),
                 mk_ref[0, :, p * LANES:(p + 1) * LANES]) for c, p, half in heads]
    e = [jnp.exp(s[i] - jnp.max(s[i], axis=-1, keepdims=True)) for i in range(len(heads))]
    pr = [(e[i] / jnp.sum(e[i], axis=-1, keepdims=True)).astype(BF16) for i in range(len(heads))]
    outs = [_dot(pr[i], mv_ref[0, :, heads[i][1] * LANES:(heads[i][1] + 1) * LANES]) for i in range(len(heads))]
    per_chain = MEM_HEADS
    o = [jnp.concatenate([jnp.where(lo, outs[c * per_chain + 2 * p], outs[c * per_chain + 2 * p + 1])
                          for p in range(MEM_HEADS // 2)], axis=1).astype(BF16) for c in chains]
    x2 = [x1[c] + _dot(o[c], wo_ref[...]) for c in chains]
    for c in chains:
        x2_ref[0, rows[c]] = x2[c]

    hf = [_rms_full(x2[c], gf_ref[...]) for c in chains]
    for c in chains:
        h_ref[0, rows[c]] = hf[c]
    logits = [_dot(hf[c].astype(BF16), wr_ref[...]) + br_ref[...] for c in chains]
    lane_f = lane.astype(F32)
    big = float(LANES)
    for c in chains:
        gl = jnp.where(lane < N_GROUPS, logits[c], BELOW_ALL)
        gmax = jnp.max(gl, axis=-1, keepdims=True)
        grp = jnp.min(jnp.where(gl == gmax, lane_f, big), axis=-1, keepdims=True)
        g_w = 1.0 / jnp.sum(jnp.where(lane < N_GROUPS, jnp.exp(gl - gmax), 0.0), axis=-1, keepdims=True)
        e_lo = N_GROUPS + grp * EXPERTS_PER_GROUP
        el = jnp.where((lane_f >= e_lo) & (lane_f < e_lo + EXPERTS_PER_GROUP), logits[c], BELOW_ALL)
        v0 = jnp.max(el, axis=-1, keepdims=True)
        i0 = jnp.min(jnp.where(el == v0, lane_f, big), axis=-1, keepdims=True)
        el = jnp.where(lane_f == i0, BELOW_ALL, el)
        v1 = jnp.max(el, axis=-1, keepdims=True)
        i1 = jnp.min(jnp.where(el == v1, lane_f, big), axis=-1, keepdims=True)
        e1 = jnp.exp(v1 - v0)
        w0 = g_w / (1.0 + e1)
        w1 = g_w * e1 / (1.0 + e1)
        route_ref[0, rows[c]] = jnp.where(lane == 0, i0 - N_GROUPS,
                                          jnp.where(lane == 1, i1 - N_GROUPS,
                                                    jnp.where(lane == 2, w0, jnp.where(lane == 3, w1, 0.0))))


def _post(x, oa, ob, wout, mk, mv, gx, wq, qgain, wo, gf, wr, br):
    B, S, _ = x.shape
    tm = POST_TM
    avg = jnp.asarray(_block_avg(MEM_WIDTH), BF16)
    full = lambda a: pl.BlockSpec(a.shape, lambda b, i: (0,) * a.ndim)
    per_b = lambda a: pl.BlockSpec((1,) + a.shape[1:], lambda b, i: (b,) + (0,) * (a.ndim - 1))
    tile = lambda w: pl.BlockSpec((1, tm, w), lambda b, i: (b, i, 0))
    return pl.pallas_call(
        _post_body,
        grid=(B, S // tm),
        in_specs=[tile(D_MODEL), tile(NSA_WIDTH), tile(RET_WIDTH), full(wout), per_b(mk), per_b(mv), full(gx),
                  full(wq), full(qgain), full(avg), full(wo), full(gf), full(wr), full(br)],
        out_specs=[tile(D_MODEL), tile(D_MODEL), tile(LANES)],
        out_shape=[jax.ShapeDtypeStruct((B, S, D_MODEL), F32), jax.ShapeDtypeStruct((B, S, D_MODEL), F32),
                   jax.ShapeDtypeStruct((B, S, LANES), F32)],
        compiler_params=_cparams(2),
        name="post_mixer",
    )(x, oa, ob, wout, mk, mv, gx, wq, qgain, avg, wo, gf, wr, br)


def _rank_body(route_ref, tri_ref, rank_ref, count_ref, carry):
    @pl.when(pl.program_id(0) == 0)
    def _():
        carry[...] = jnp.zeros(carry.shape, F32)

    route = route_ref[...]
    tm = route.shape[0]
    lane = lax.broadcasted_iota(jnp.int32, (tm, LANES), 1).astype(F32)
    oh0 = lane == route[:, 0:1]
    oh1 = lane == route[:, 1:2]
    both = jnp.where(oh0, 1.0, 0.0) + jnp.where(oh1, 1.0, 0.0)
    before = _dot(tri_ref[...], both.astype(BF16)) + carry[...]
    r0 = jnp.sum(jnp.where(oh0, before, 0.0), axis=-1, keepdims=True)
    r1 = jnp.sum(jnp.where(oh1, before, 0.0), axis=-1, keepdims=True)
    lane_i = lax.broadcasted_iota(jnp.int32, (tm, LANES), 1)
    rank_ref[...] = jnp.where(lane_i == 0, r0, jnp.where(lane_i == 1, r1, 0.0))
    total = carry[...] + jnp.sum(both, axis=0, keepdims=True)
    carry[...] = total
    count_ref[...] = jnp.broadcast_to(total, count_ref.shape)


def _rank(route2d):
    T = route2d.shape[0]
    tm = RANK_TM
    tri = jnp.asarray(np.tril(np.ones((tm, tm), np.float32), -1), BF16)
    return pl.pallas_call(
        _rank_body,
        grid=(T // tm,),
        in_specs=[pl.BlockSpec((tm, LANES), lambda i: (i, 0)), pl.BlockSpec((tm, tm), lambda i: (0, 0))],
        out_specs=[pl.BlockSpec((tm, LANES), lambda i: (i, 0)), pl.BlockSpec((8, LANES), lambda i: (0, 0))],
        out_shape=[jax.ShapeDtypeStruct((T, LANES), F32), jax.ShapeDtypeStruct((8, LANES), F32)],
        scratch_shapes=[pltpu.VMEM((1, LANES), F32)],
        compiler_params=_cparams(1),
        name="moe_rank",
    )(route2d, tri)


def _row_copy(src, dst, sem):
    return pltpu.make_async_copy(src, dst, sem)


def _scatter_body(dest_ref, h_ref, xs_in_ref, xs_ref, sem):
    del xs_in_ref
    tm = h_ref.shape[0]

    def issue(j, carry):
        for u in range(MOVE_UNROLL):
            i = j * MOVE_UNROLL + u
            for s in range(2):
                _row_copy(h_ref.at[pl.ds(i, 1)], xs_ref.at[pl.ds(dest_ref[2 * i + s], 1)], sem).start(priority=s)
        return carry

    lax.fori_loop(0, tm // MOVE_UNROLL, issue, 0)
    for _ in range(2):
        _row_copy(h_ref, xs_ref.at[pl.ds(0, tm)], sem).wait()


def _scatter_rows(dest_flat, h2d, n_rows):
    T = h2d.shape[0]
    tm = MOVE_TM
    zeros = jnp.zeros((n_rows, D_MODEL), F32)
    return pl.pallas_call(
        _scatter_body,
        grid=(T // tm,),
        in_specs=[pl.BlockSpec((2 * tm,), lambda i: (i,), memory_space=pltpu.SMEM),
                  pl.BlockSpec((tm, D_MODEL), lambda i: (i, 0)),
                  pl.BlockSpec(memory_space=pl.ANY)],
        out_specs=pl.BlockSpec(memory_space=pl.ANY),
        out_shape=jax.ShapeDtypeStruct((n_rows, D_MODEL), F32),
        scratch_shapes=[pltpu.SemaphoreType.DMA(())],
        input_output_aliases={2: 0},
        compiler_params=_cparams(1),
        name="moe_scatter",
    )(dest_flat, h2d, zeros)


def _expert_body(blk_e_ref, n_act_ref, xs_ref, wg_ref, wu_ref, wd_ref, ys_ref):
    i = pl.program_id(0)

    @pl.when(i < n_act_ref[0])
    def _():
        x = xs_ref[...].astype(BF16)
        a = _dot(x, wg_ref[0].astype(BF16))
        b = _dot(x, wu_ref[0].astype(BF16))
        ys_ref[...] = _dot((_silu(a) * b).astype(BF16), wd_ref[0].astype(BF16))

    @pl.when(i >= n_act_ref[0])
    def _():
        ys_ref[...] = jnp.zeros(ys_ref.shape, F32)


def _experts(blk_e, n_act, xs, wg, wu, wd):
    n_rows = xs.shape[0]
    rb = MOE_RB
    grid_spec = pltpu.PrefetchScalarGridSpec(
        num_scalar_prefetch=2,
        grid=(n_rows // rb,),
        in_specs=[pl.BlockSpec((rb, D_MODEL), lambda i, be, na: (i, 0)),
                  pl.BlockSpec((1, D_MODEL, EXPERT_FF), lambda i, be, na: (be[i], 0, 0)),
                  pl.BlockSpec((1, D_MODEL, EXPERT_FF), lambda i, be, na: (be[i], 0, 0)),
                  pl.BlockSpec((1, EXPERT_FF, D_MODEL), lambda i, be, na: (be[i], 0, 0))],
        out_specs=pl.BlockSpec((rb, D_MODEL), lambda i, be, na: (i, 0)),
    )
    return pl.pallas_call(
        _expert_body,
        grid_spec=grid_spec,
        out_shape=jax.ShapeDtypeStruct((n_rows, D_MODEL), F32),
        compiler_params=_cparams(1),
        name="moe_experts",
    )(blk_e, n_act, xs, wg, wu, wd)


def _combine_body(dest_ref, x_ref, route_ref, ys_ref, o_ref, buf, sem):
    tm = x_ref.shape[0]

    def issue(j, carry):
        for u in range(MOVE_UNROLL):
            i = j * MOVE_UNROLL + u
            for s in range(2):
                _row_copy(ys_ref.at[pl.ds(dest_ref[2 * i + s], 1)], buf.at[s, pl.ds(i, 1)], sem).start(priority=s)
        return carry

    lax.fori_loop(0, tm // MOVE_UNROLL, issue, 0)
    for s in range(2):
        _row_copy(ys_ref.at[pl.ds(0, tm)], buf.at[s], sem).wait()
    route = route_ref[...]
    o_ref[...] = x_ref[...] + (route[:, 2:3] * buf[0] + route[:, 3:4] * buf[1])


def _combine(dest_flat, x2d, route2d, ys):
    T = x2d.shape[0]
    tm = MOVE_TM
    return pl.pallas_call(
        _combine_body,
        grid=(T // tm,),
        in_specs=[pl.BlockSpec((2 * tm,), lambda i: (i,), memory_space=pltpu.SMEM),
                  pl.BlockSpec((tm, D_MODEL), lambda i: (i, 0)),
                  pl.BlockSpec((tm, LANES), lambda i: (i, 0)),
                  pl.BlockSpec(memory_space=pl.ANY)],
        out_specs=pl.BlockSpec((tm, D_MODEL), lambda i: (i, 0)),
        out_shape=jax.ShapeDtypeStruct((T, D_MODEL), F32),
        scratch_shapes=[pltpu.VMEM((2, tm, D_MODEL), F32), pltpu.SemaphoreType.DMA(())],
        compiler_params=_cparams(1),
        name="moe_combine",
    )(dest_flat, x2d, route2d, ys)


def _permute_w_in(w):
    kv0 = NSA_WIDTH
    gate0 = kv0 + 6 * NSA_GROUPS * HEAD_DIM
    ret0 = gate0 + 3 * NSA_HEADS
    pad = jnp.zeros((w.shape[0], PROJ_PAD - w.shape[1]), w.dtype)
    return jnp.concatenate([w[:, :kv0], w[:, ret0:], w[:, kv0:gate0], w[:, gate0:ret0], pad], axis=1).astype(BF16)


def _compress_weights(pos, w1, w2):
    eye = jnp.eye(NSA_GROUPS, dtype=F32)
    w1r = w1.reshape(CMP_BLOCK, HEAD_DIM, CMP_HIDDEN)
    w1b = jnp.einsum('ldh,gk->lgdkh', w1r, eye).reshape(2, CMP_STRIDE * LANES, NSA_GROUPS * CMP_HIDDEN)
    w2b = jnp.einsum('hd,gk->ghkd', w2, eye).reshape(NSA_GROUPS * CMP_HIDDEN, LANES)
    posb = jnp.tile(pos, (1, NSA_GROUPS)).reshape(2, 1, CMP_STRIDE * LANES)
    return posb, w1b.astype(BF16), w2b.astype(BF16)


def _dup2(g):
    return jnp.tile(g.reshape(1, HEAD_DIM), (1, 2))


def kernel(x, mem, mix_norm, w_in, nsa_q_norm, nsa_kcmp_norm, nsa_ksel_norm, nsa_kwin_norm, cmp_pos_k, cmp_pos_v, cmp_k_w1, cmp_k_w2, cmp_v_w1, cmp_v_w2, nsa_out_norm, ret_out_norm, w_out, mem_x_norm, mem_kv_norm, mem_wq, mem_wkv, mem_q_norm, mem_k_norm, mem_wo, ffn_norm, router_group_w, router_group_b, router_expert_w, router_expert_b, exp_w_gate, exp_w_up, exp_w_down):
    B, S, D = x.shape
    T = B * S
    depth = mix_norm.shape[0]
    for l in range(depth):
        proj = _proj(x.reshape(T, D), mix_norm[l].reshape(1, D), _permute_w_in(w_in[l])).reshape(B, S, PROJ_PAD)
        pk, w1k, w2k = _compress_weights(cmp_pos_k[l], cmp_k_w1[l], cmp_k_w2[l])
        pv, w1v, w2v = _compress_weights(cmp_pos_v[l], cmp_v_w1[l], cmp_v_w2[l])
        gains = jnp.stack([_dup2(nsa_kcmp_norm[l]), _dup2(nsa_ksel_norm[l]), _dup2(nsa_kwin_norm[l])])
        kcmp, vcmp, ks, vs, kw, vw = _nsa_prep(proj, jnp.stack([pk, pv]), jnp.stack([w1k, w1v]),
                                               jnp.stack([w2k, w2v]), gains)
        o_a = _nsa_attn(proj, kcmp, vcmp, ks, vs, kw, vw,
                        jnp.tile(nsa_q_norm[l].reshape(1, HEAD_DIM), (1, NSA_HEADS)), gains[1:],
                        nsa_out_norm[l].reshape(1, NSA_WIDTH))
        o_b = _retention(proj, ret_out_norm[l].reshape(1, RET_WIDTH))
        mk, mv = _mem_prep(mem, mem_kv_norm[l].reshape(1, D), mem_wkv[l].astype(BF16),
                           jnp.tile(mem_k_norm[l].reshape(1, HEAD_DIM), (1, MEM_HEADS)))
        w_r = jnp.concatenate([router_group_w[l],
                               router_expert_w[l].transpose(1, 0, 2).reshape(D, N_EXPERTS),
                               jnp.zeros((D, LANES - N_GROUPS - N_EXPERTS), F32)], axis=1).astype(BF16)
        b_r = jnp.concatenate([router_group_b[l], router_expert_b[l].reshape(N_EXPERTS),
                               jnp.zeros((LANES - N_GROUPS - N_EXPERTS,), F32)]).reshape(1, LANES)
        x2, hf, route = _post(x, o_a, o_b, w_out[l].astype(BF16), mk, mv, mem_x_norm[l].reshape(1, D),
                              mem_wq[l].astype(BF16), jnp.tile(mem_q_norm[l].reshape(1, HEAD_DIM), (1, MEM_HEADS)),
                              mem_wo[l].astype(BF16), ffn_norm[l].reshape(1, D), w_r, b_r)
        route2d = route.reshape(T, LANES)
        rank, counts = _rank(route2d)
        counts = counts[0, :N_EXPERTS].astype(jnp.int32)
        padded = (counts + MOE_RB - 1) // MOE_RB * MOE_RB
        pend = jnp.cumsum(padded)
        pstart = pend - padded
        eid = route2d[:, 0:2].astype(jnp.int32)
        dest = (pstart[eid] + rank[:, 0:2].astype(jnp.int32)).reshape(2 * T)
        n_rows = 2 * T + N_EXPERTS * MOE_RB
        n_blocks = n_rows // MOE_RB
        blk_row0 = jnp.arange(n_blocks, dtype=jnp.int32) * MOE_RB
        blk_e = jnp.minimum(jnp.sum((pend[None, :] <= blk_row0[:, None]).astype(jnp.int32), axis=1), N_EXPERTS - 1)
        n_act = (pend[-1:] // MOE_RB).astype(jnp.int32)
        xs = _scatter_rows(dest, hf.reshape(T, D), n_rows)
        ys = _experts(blk_e, n_act, xs, exp_w_gate[l], exp_w_up[l], exp_w_down[l])
        x = _combine(dest, x2.reshape(T, D), route2d, ys).reshape(B, S, D)
    return x
```

```python
import functools

import numpy as np
import jax
import jax.numpy as jnp
from jax import lax
from jax.experimental import pallas as pl
from jax.experimental.pallas import tpu as pltpu

F32 = jnp.float32
BF16 = jnp.bfloat16

D_MODEL = 1024
HEAD_DIM = 64
LANES = 128
NSA_HEADS = 8
NSA_GROUPS = 2
NSA_WIDTH = NSA_HEADS * HEAD_DIM
CMP_BLOCK = 32
CMP_STRIDE = 16
CMP_HIDDEN = 2 * HEAD_DIM
SEL_BLOCK = 64
SEL_TOPK = 8
WINDOW = 512
RET_HEADS = 8
RET_WIDTH = RET_HEADS * HEAD_DIM
RET_CHUNK = 128
ROPE_BASE = 10000.0
MEM_HEADS = 4
MEM_WIDTH = MEM_HEADS * HEAD_DIM
N_GROUPS = 4
EXPERTS_PER_GROUP = 8
N_EXPERTS = N_GROUPS * EXPERTS_PER_GROUP
EXPERT_FF = D_MODEL // 4
EPS = 1e-6
NEG_INF = -1e30
FORCE_SCORE = 1e9
BELOW_ALL = -3e38
MAX_FIXED_SHIFT = 40.0

COL_QA = 0
COL_QR, COL_KR, COL_VR, COL_GR = 512, 1024, 1536, 2048
COL_KVC, COL_KSV, COL_KWV = 2560, 2816, 3072
COL_GATE = 3328
PROJ_PAD = 3456

PROJ_TM = 512
NSA_TQ = 256
SEL_KC = 512
WIN_KEYS = WINDOW + NSA_TQ
POST_TM = 512
POST_CHAINS = 2
MOE_TM = 512
MOE_RB = 256
RUN_ALIGN = 8
MOE_NLOC = 2 * MOE_TM + N_EXPERTS * RUN_ALIGN
VMEM_LIMIT = 56 * 1024 * 1024


def _cparams(n_axes):
    return pltpu.CompilerParams(dimension_semantics=("arbitrary",) * n_axes,
                                vmem_limit_bytes=VMEM_LIMIT)


def _dot(a, b):
    return jnp.dot(a, b, preferred_element_type=F32)


def _dot_nt(a, b):
    return lax.dot_general(a, b, (((1,), (1,)), ((), ())), preferred_element_type=F32)


def _dot_tn(a, b):
    return lax.dot_general(a, b, (((0,), (0,)), ((), ())), preferred_element_type=F32)


def _rms_full(x, g):
    ms = jnp.mean(x * x, axis=-1, keepdims=True)
    return x * lax.rsqrt(ms + EPS) * g


def _seg_mean(x, avg):
    return _dot(x.astype(BF16), avg)


def _silu(x):
    return x * (1.0 / (1.0 + jnp.exp(-x)))


def _sigmoid(x):
    return 1.0 / (1.0 + jnp.exp(-x))


def _block_avg(width):
    i = np.arange(width)
    return ((i[:, None] // HEAD_DIM == i[None, :] // HEAD_DIM) / HEAD_DIM).astype(np.float32)


def _proj_body(x_ref, g_ref, w_ref, o_ref):
    h = _rms_full(x_ref[...], g_ref[...]).astype(BF16)
    step = PROJ_PAD // 3
    for j in range(3):
        o_ref[:, j * step:(j + 1) * step] = _dot(h, w_ref[:, j * step:(j + 1) * step]).astype(BF16)


def _proj(x2d, g, w):
    T = x2d.shape[0]
    return pl.pallas_call(
        _proj_body,
        grid=(T // PROJ_TM,),
        in_specs=[pl.BlockSpec((PROJ_TM, D_MODEL), lambda i: (i, 0)),
                  pl.BlockSpec((1, D_MODEL), lambda i: (0, 0)),
                  pl.BlockSpec((D_MODEL, PROJ_PAD), lambda i: (0, 0))],
        out_specs=pl.BlockSpec((PROJ_TM, PROJ_PAD), lambda i: (i, 0)),
        out_shape=jax.ShapeDtypeStruct((T, PROJ_PAD), BF16),
        compiler_params=_cparams(1),
        name="proj",
    )(x2d, g, w)


def _dup_groups(x):
    lane = lax.broadcasted_iota(jnp.int32, x.shape, 1)
    xs = pltpu.roll(x, HEAD_DIM, axis=1)
    lo = lane < HEAD_DIM
    return jnp.where(lo, x, xs), jnp.where(lo, xs, x)


def _ones_groups(x):
    lane = lax.broadcasted_iota(jnp.int32, x.shape, 1)
    lo = lane < HEAD_DIM
    return jnp.where(lo, x, 1.0), jnp.where(lo, pltpu.roll(x, HEAD_DIM, axis=1), 1.0)


def _nsa_prep_body(kvc_ref, ksv_ref, kwv_ref, pos_ref, w1_ref, w2_ref, gain_ref, avg_ref,
                   kcmp_ref, vcmp_ref, ks_ref, vs_ref, kw_ref, vw_ref, scr_k, scr_v):
    avg = avg_ref[...]
    n_c = scr_k.shape[0] // CMP_STRIDE
    scr_k[...] = kvc_ref[0, :, 0:LANES].astype(F32)
    scr_v[...] = kvc_ref[0, :, LANES:2 * LANES].astype(F32)
    for j, out_ref, scr in ((0, kcmp_ref, scr_k), (1, vcmp_ref, scr_v)):
        ycat = jnp.concatenate(
            [scr[pl.ds(l, n_c, stride=CMP_STRIDE), :] for l in range(CMP_STRIDE)], axis=1)
        first = _dot((ycat + pos_ref[j, 0]).astype(BF16), w1_ref[j, 0])
        second = _dot((ycat + pos_ref[j, 1]).astype(BF16), w1_ref[j, 1])
        hidden = first + pltpu.roll(second, n_c - 1, axis=0)
        cmp_tok = _dot(_silu(hidden).astype(BF16), w2_ref[j])
        if j == 0:
            ms = _seg_mean(cmp_tok * cmp_tok, avg)
            cmp_tok = cmp_tok * lax.rsqrt(ms + EPS) * gain_ref[0]
        d0, d1 = _dup_groups(cmp_tok) if j == 0 else _ones_groups(cmp_tok)
        out_ref[0, 0] = d0.astype(BF16)
        out_ref[0, 1] = d1.astype(BF16)

    for src_ref, k_out, v_out, gi in ((ksv_ref, ks_ref, vs_ref, 1), (kwv_ref, kw_ref, vw_ref, 2)):
        k = src_ref[0, :, 0:LANES].astype(F32)
        ms = _seg_mean(k * k, avg)
        k = k * lax.rsqrt(ms + EPS) * gain_ref[gi]
        d0, d1 = _dup_groups(k)
        k_out[0, 0] = d0.astype(BF16)
        k_out[0, 1] = d1.astype(BF16)
        d0, d1 = _ones_groups(src_ref[0, :, LANES:2 * LANES].astype(F32))
        v_out[0, 0] = d0.astype(BF16)
        v_out[0, 1] = d1.astype(BF16)


def _nsa_prep(proj3, pos, w1, w2, gains):
    B, S, _ = proj3.shape
    n_c = S // CMP_STRIDE
    avg = jnp.asarray(_block_avg(LANES), BF16)
    col = lambda c: pl.BlockSpec((1, S, 2 * LANES), lambda b: (b, 0, c // (2 * LANES)))
    full = lambda a: pl.BlockSpec(a.shape, lambda b: (0,) * a.ndim)
    cmp_spec = pl.BlockSpec((1, NSA_GROUPS, n_c, LANES), lambda b: (b, 0, 0, 0))
    seq_spec = pl.BlockSpec((1, NSA_GROUPS, S, LANES), lambda b: (b, 0, 0, 0))
    cmp_shape = jax.ShapeDtypeStruct((B, NSA_GROUPS, n_c, LANES), BF16)
    seq_shape = jax.ShapeDtypeStruct((B, NSA_GROUPS, S, LANES), BF16)
    return pl.pallas_call(
        _nsa_prep_body,
        grid=(B,),
        in_specs=[col(COL_KVC), col(COL_KSV), col(COL_KWV), full(pos), full(w1), full(w2), full(gains), full(avg)],
        out_specs=[cmp_spec, cmp_spec, seq_spec, seq_spec, seq_spec, seq_spec],
        out_shape=[cmp_shape, cmp_shape, seq_shape, seq_shape, seq_shape, seq_shape],
        scratch_shapes=[pltpu.VMEM((S, LANES), F32), pltpu.VMEM((S, LANES), F32)],
        compiler_params=_cparams(1),
        name="nsa_prep",
    )(proj3, proj3, proj3, pos, w1, w2, gains, avg)


def _nsa_attn_body(q_ref, gate_ref, kcmp_ref, vcmp_ref, ks_ref, vs_ref, kw_ref, vw_ref,
                   qgain_ref, kgain_ref, ogain_ref, avgq_ref, avgo_ref, msct_ref, esel_ref, egate_ref, wbias_ref,
                   dbias_ref,
                   o_ref, m_scr, acc_scr):
    tq = q_ref.shape[1]
    n_cmp = kcmp_ref.shape[2]
    n_sel = msct_ref.shape[0]
    kc_len = esel_ref.shape[2]
    rows = 4 * tq
    qi = pl.program_id(1)
    q0 = qi * tq

    q = q_ref[0].astype(F32)
    ms = _seg_mean(q * q, avgq_ref[...])
    qn = q * lax.rsqrt(ms + EPS) * qgain_ref[...] * (HEAD_DIM ** -0.5)

    gate_sig = _sigmoid(gate_ref[0].astype(F32)).astype(BF16)
    gates = [_dot(gate_sig, egate_ref[j]) for j in range(3)]

    lane_q = lax.broadcasted_iota(jnp.int32, (tq, LANES), 1)
    lo_q = lane_q < HEAD_DIM
    lo_r = lax.broadcasted_iota(jnp.int32, (rows, LANES), 1) < HEAD_DIM

    blk = lax.broadcasted_iota(jnp.int32, (n_sel, tq), 0)
    cur = lax.shift_right_logical(q0 + lax.broadcasted_iota(jnp.int32, (n_sel, tq), 1), int(np.log2(SEL_BLOCK)))
    forced = (blk == 0) | (blk == cur) | (blk == cur - 1)
    future = blk > cur
    blk_f = blk.astype(F32)

    def heads4(x):
        return jnp.concatenate([x] * 4, axis=0)

    def normalised_pairs(acc, guard):
        rolled = pltpu.roll(acc, HEAD_DIM, axis=1)
        den = jnp.where(lo_r, rolled, acc)
        if guard:
            den = jnp.maximum(den, 1e-30)
        out = []
        for p in range(2):
            ev = slice((2 * p) * tq, (2 * p + 1) * tq)
            od = slice((2 * p + 1) * tq, (2 * p + 2) * tq)
            out.append(jnp.where(lo_q, acc[ev] / den[ev], rolled[od] / den[od]))
        return out

    groups = range(NSA_GROUPS)
    qs = []
    for g in groups:
        slabs = [qn[:, (2 * g + p) * LANES:(2 * g + p + 1) * LANES] for p in range(2)]
        qs.append(jnp.concatenate(
            [jnp.where(lo_q, slabs[0], 0.0), jnp.where(lo_q, 0.0, slabs[0]),
             jnp.where(lo_q, slabs[1], 0.0), jnp.where(lo_q, 0.0, slabs[1])], axis=0).astype(BF16))

    r_c = lax.broadcasted_iota(jnp.int32, (rows, n_cmp), 0)
    c_c = lax.broadcasted_iota(jnp.int32, (rows, n_cmp), 1)
    cmask = (c_c * CMP_STRIDE + (CMP_BLOCK - 1)) <= q0 + (r_c & (tq - 1))
    s_c = [jnp.where(cmask, _dot_nt(qs[g], kcmp_ref[0, g]), NEG_INF) for g in groups]
    e_c = [jnp.where(cmask, jnp.exp(s_c[g] - jnp.max(s_c[g], axis=-1, keepdims=True)), 0.0) for g in groups]
    acc_c = [_dot(e_c[g].astype(BF16), vcmp_ref[0, g]) for g in groups]
    l_c = [jnp.where(lo_r, pltpu.roll(acc_c[g], HEAD_DIM, axis=1), acc_c[g]) for g in groups]
    p_c = [e_c[g] / jnp.maximum(l_c[g], 1e-30) for g in groups]

    p_sum = [p_c[g][0:tq] + p_c[g][tq:2 * tq] + p_c[g][2 * tq:3 * tq] + p_c[g][3 * tq:4 * tq] for g in groups]
    p_hi = [p_sum[g].astype(BF16) for g in groups]
    p_lo = [(p_sum[g] - p_hi[g].astype(F32)).astype(BF16) for g in groups]
    imp = [_dot_nt(msct_ref[...], p_hi[g]) + _dot_nt(msct_ref[...], p_lo[g]) for g in groups]
    v = [jnp.where(forced, FORCE_SCORE, jnp.where(future, NEG_INF, imp[g])) for g in groups]
    sel = [jnp.zeros((n_sel, tq), F32) for g in groups]
    for _ in range(SEL_TOPK):
        mx = [jnp.max(v[g], axis=0, keepdims=True) for g in groups]
        first = [jnp.min(jnp.where(v[g] == mx[g], blk_f, float(LANES)), axis=0, keepdims=True) for g in groups]
        pick = [blk_f == first[g] for g in groups]
        sel = [jnp.where(pick[g], 1.0, sel[g]) for g in groups]
        v = [jnp.where(pick[g], BELOW_ALL, v[g]) for g in groups]
    sel_b = [sel[g].astype(BF16) for g in groups]

    cmp_s = [normalised_pairs(acc_c[g], True) for g in groups]
    n_before = lax.shift_right_logical(q0, int(np.log2(kc_len)))
    causal = dbias_ref[qi & (kc_len // tq - 1)]
    w0 = pl.multiple_of(jnp.maximum(q0 - WINDOW, 0), tq)
    n_w = WINDOW + tq
    w_case = jnp.minimum(qi, WINDOW // tq)

    def sel_keys(ref, g, kc):
        return ref[0, g, pl.ds(pl.multiple_of(kc * kc_len, kc_len), kc_len), :]

    def sel_scores(g, kc, causal_bias, shift):
        chosen = _dot_tn(sel_b[g], esel_ref[kc])
        bias = (chosen - 1.0) * (-NEG_INF)
        if causal_bias is not None:
            bias = bias + causal_bias
        if shift is not None:
            bias = bias - shift
        return _dot_nt(qs[g], sel_keys(ks_ref, g, kc)) + heads4(bias)

    def win_scores(g, shift):
        bias = wbias_ref[w_case] if shift is None else wbias_ref[w_case] - shift
        return _dot_nt(qs[g], kw_ref[0, g, pl.ds(w0, n_w), :]) + heads4(bias)

    def win_values(g):
        return vw_ref[0, g, pl.ds(w0, n_w), :]

    def finish(acc_w):
        for g in groups:
            sel_s = normalised_pairs(acc_scr[g], False)
            win_s = normalised_pairs(acc_w[g], False)
            for p in range(2):
                cols = slice((2 * g + p) * LANES, (2 * g + p + 1) * LANES)
                mix = gates[0][:, cols] * cmp_s[g][p] + gates[1][:, cols] * sel_s[p] + gates[2][:, cols] * win_s[p]
                ms_o = _seg_mean(mix * mix, avgo_ref[...])
                o_ref[0, :, cols] = (mix * lax.rsqrt(ms_o + EPS) * ogain_ref[:, cols]).astype(BF16)

    def fixed_shift_path(shift):
        def probs(s):
            return jnp.exp(s).astype(BF16)

        for g in groups:
            acc_scr[g] = jnp.zeros(acc_scr.shape[1:], F32)

        def before(kc, carry):
            s = [sel_scores(g, kc, None, shift) for g in groups]
            p = [probs(s[g]) for g in groups]
            for g in groups:
                acc_scr[g] = acc_scr[g] + _dot(p[g], sel_keys(vs_ref, g, kc))
            return carry

        lax.fori_loop(0, n_before, before, 0)
        s_d0 = sel_scores(0, n_before, causal, shift)
        s_d1 = sel_scores(1, n_before, causal, shift)
        p_d0 = probs(s_d0)
        s_w0 = win_scores(0, shift)
        acc_scr[0] = acc_scr[0] + _dot(p_d0, sel_keys(vs_ref, 0, n_before))
        p_d1 = probs(s_d1)
        s_w1 = win_scores(1, shift)
        acc_scr[1] = acc_scr[1] + _dot(p_d1, sel_keys(vs_ref, 1, n_before))
        p_w0 = probs(s_w0)
        acc_w0 = _dot(p_w0, win_values(0))
        p_w1 = probs(s_w1)
        acc_w1 = _dot(p_w1, win_values(1))
        finish([acc_w0, acc_w1])

    def online_path():
        for g in groups:
            m_scr[g] = jnp.full(m_scr.shape[1:], NEG_INF, F32)
            acc_scr[g] = jnp.zeros(acc_scr.shape[1:], F32)

        def sel_softmax(g, s):
            m_old = m_scr[g]
            m_new = jnp.maximum(m_old, jnp.max(s, axis=-1, keepdims=True))
            m_scr[g] = m_new
            return jnp.exp(s - m_new).astype(BF16), jnp.exp(m_old - m_new)

        def sel_accumulate(g, kc, p, alpha):
            acc_scr[g] = alpha * acc_scr[g] + _dot(p, sel_keys(vs_ref, g, kc))

        def win_softmax(s):
            return jnp.exp(s - jnp.max(s, axis=-1, keepdims=True)).astype(BF16)

        def before(kc, carry):
            s = [sel_scores(g, kc, None, None) for g in groups]
            pa = [sel_softmax(g, s[g]) for g in groups]
            for g in groups:
                sel_accumulate(g, kc, *pa[g])
            return carry

        lax.fori_loop(0, n_before, before, 0)
        s_d0 = sel_scores(0, n_before, causal, None)
        s_d1 = sel_scores(1, n_before, causal, None)
        pa0 = sel_softmax(0, s_d0)
        s_w0 = win_scores(0, None)
        sel_accumulate(0, n_before, *pa0)
        pa1 = sel_softmax(1, s_d1)
        s_w1 = win_scores(1, None)
        sel_accumulate(1, n_before, *pa1)
        acc_w0 = _dot(win_softmax(s_w0), win_values(0))
        acc_w1 = _dot(win_softmax(s_w1), win_values(1))
        finish([acc_w0, acc_w1])

    bound = 1.01 * (HEAD_DIM ** 0.5) * jnp.max(jnp.abs(qgain_ref[...])) * jnp.max(jnp.abs(kgain_ref[...]))
    safe = bound <= MAX_FIXED_SHIFT
    pl.when(safe)(lambda: fixed_shift_path(bound))
    pl.when(jnp.logical_not(safe))(online_path)


def _sel_from_cmp(n_cmp, n_sel):
    c0 = np.arange(n_cmp) * CMP_STRIDE
    s0 = np.arange(n_sel) * SEL_BLOCK
    ov = np.minimum(c0[None, :] + CMP_BLOCK, s0[:, None] + SEL_BLOCK) - np.maximum(c0[None, :], s0[:, None])
    m = (np.clip(ov, 0, None) / CMP_BLOCK).astype(np.float32)
    m[:, (np.arange(n_cmp) * CMP_STRIDE + CMP_BLOCK) > n_sel * SEL_BLOCK] = 0.0
    return m


def _nsa_attn(proj3, kcmp, vcmp, ks, vs, kw, vw, q_gain, k_gains, o_gain):
    B, S, _ = proj3.shape
    n_cmp = kcmp.shape[2]
    n_sel = S // SEL_BLOCK
    tq = NSA_TQ
    assert n_sel % 8 == 0 and S % SEL_KC == 0 and SEL_KC % tq == 0 and WINDOW % tq == 0 and S >= WINDOW + tq
    avgq = jnp.asarray(_block_avg(NSA_WIDTH), BF16)
    avgo = jnp.asarray(_block_avg(LANES), BF16)
    msct = jnp.asarray(_sel_from_cmp(n_cmp, n_sel), BF16)
    esel = (np.arange(n_sel)[:, None] == np.arange(S)[None, :] // SEL_BLOCK).astype(np.float32)
    esel = jnp.asarray(esel.reshape(n_sel, S // SEL_KC, SEL_KC).transpose(1, 0, 2), BF16)
    src = np.arange(LANES)[:, None]
    dst = np.arange(NSA_WIDTH)[None, :]
    egate = jnp.asarray(np.stack([(src == (dst // HEAD_DIM) * 3 + j) for j in range(3)]).astype(np.float32), BF16)
    r = np.arange(tq)[:, None]
    n_w = WINDOW + tq
    wcases = []
    for i in range(WINDOW // tq + 1):
        diff = (i * tq - max(i * tq - WINDOW, 0)) + r - np.arange(n_w)[None, :]
        wcases.append(np.where((diff >= 0) & (diff < WINDOW), 0.0, NEG_INF))
    wbias = jnp.asarray(np.stack(wcases), F32)
    dbias = jnp.asarray(np.stack([np.where(np.arange(SEL_KC)[None, :] <= i * tq + r, 0.0, NEG_INF)
                                  for i in range(SEL_KC // tq)]), F32)

    full = lambda a: pl.BlockSpec(a.shape, lambda b, i: (0,) * a.ndim)
    per_b = lambda a: pl.BlockSpec((1,) + a.shape[1:], lambda b, i: (b,) + (0,) * (a.ndim - 1))
    return pl.pallas_call(
        _nsa_attn_body,
        grid=(B, S // tq),
        in_specs=[pl.BlockSpec((1, tq, NSA_WIDTH), lambda b, i: (b, i, COL_QA // NSA_WIDTH)),
                  pl.BlockSpec((1, tq, LANES), lambda b, i: (b, i, COL_GATE // LANES)),
                  per_b(kcmp), per_b(vcmp), per_b(ks), per_b(vs), per_b(kw), per_b(vw),
                  full(q_gain), full(k_gains), full(o_gain), full(avgq), full(avgo), full(msct), full(esel), full(egate),
                  full(wbias), full(dbias)],
        out_specs=pl.BlockSpec((1, tq, NSA_WIDTH), lambda b, i: (b, i, 0)),
        out_shape=jax.ShapeDtypeStruct((B, S, NSA_WIDTH), BF16),
        scratch_shapes=[pltpu.VMEM((NSA_GROUPS, 4 * tq, 1), F32), pltpu.VMEM((NSA_GROUPS, 4 * tq, LANES), F32)],
        compiler_params=_cparams(2),
        name="nsa_attn",
    )(proj3, proj3, kcmp, vcmp, ks, vs, kw, vw, q_gain, k_gains, o_gain, avgq, avgo, msct, esel, egate, wbias, dbias)


def _retention_body(q_ref, k_ref, v_ref, g_ref, cos_ref, sin_ref, decay_ref, xi_ref, zeta_ref, gammac_ref,
                    gain_ref, avg_ref, o_ref, state_scr):
    S = q_ref.shape[1]
    C = RET_CHUNK
    lane = lax.broadcasted_iota(jnp.int32, (C, LANES), 1)
    lo = lane < HEAD_DIM
    first_half = (lane & (HEAD_DIM - 1)) < HEAD_DIM // 2
    r = lax.broadcasted_iota(jnp.int32, (LANES, LANES), 0)
    c = lax.broadcasted_iota(jnp.int32, (LANES, LANES), 1)
    same_head = (r < HEAD_DIM) == (c < HEAD_DIM)
    avg = avg_ref[...]
    state_scr[...] = jnp.zeros(state_scr.shape, F32)

    def rope(x, cos, sin):
        swapped = jnp.where(first_half, pltpu.roll(x, LANES - HEAD_DIM // 2, axis=1),
                            pltpu.roll(x, HEAD_DIM // 2, axis=1))
        return x * cos + swapped * sin

    def chunk(n, carry):
        r0 = pl.multiple_of(n * C, C)
        cos = cos_ref[pl.ds(r0, C), :]
        sin = sin_ref[pl.ds(r0, C), :]
        pairs = range(RET_HEADS // 2)
        cols = [slice(p * LANES, (p + 1) * LANES) for p in pairs]
        q = [rope(q_ref[0, pl.ds(r0, C), cols[p]].astype(F32), cos, sin) for p in pairs]
        k = [rope(k_ref[0, pl.ds(r0, C), cols[p]].astype(F32), cos, sin) * (HEAD_DIM ** -0.5) for p in pairs]
        vb = [v_ref[0, pl.ds(r0, C), cols[p]] for p in pairs]
        kb = [k[p].astype(BF16) for p in pairs]
        inner = [_dot_nt(jnp.where(lo if half == 0 else ~lo, q[p], 0.0).astype(BF16), kb[p])
                 * decay_ref[2 * p + half] for p in pairs for half in range(2)]
        state = [state_scr[p] for p in pairs]
        cross = [_dot(q[p].astype(BF16), state[p].astype(BF16)) * xi_ref[p] for p in pairs]
        upd = [_dot_tn((k[p] * zeta_ref[p]).astype(BF16), vb[p]) for p in pairs]
        outs = [_dot(inner[i].astype(BF16), vb[i // 2]) for i in range(RET_HEADS)]
        for p in pairs:
            state_scr[p] = gammac_ref[p] * state[p] + jnp.where(same_head, upd[p], 0.0)
        y = jnp.concatenate([jnp.where(lo, outs[2 * p], outs[2 * p + 1]) + cross[p] for p in pairs], axis=0)
        mu = _seg_mean(y, avg)
        d = y - mu
        var = _seg_mean(d * d, avg)
        yn = d * lax.rsqrt(var + EPS)
        for p in pairs:
            gate = g_ref[0, pl.ds(r0, C), cols[p]].astype(F32)
            o_ref[0, pl.ds(r0, C), cols[p]] = (_silu(gate) * (yn[p * C:(p + 1) * C] * gain_ref[:, cols[p]])).astype(BF16)
        return carry

    lax.fori_loop(0, S // C, chunk, 0)


def _retention_tables(S):
    half = HEAD_DIM // 2
    inv_freq = ROPE_BASE ** (-jnp.arange(half, dtype=F32) / half)
    ang = jnp.arange(S, dtype=F32)[:, None] * inv_freq[None, :]
    cos, sin = jnp.cos(ang), jnp.sin(ang)
    cos_t = jnp.tile(cos, (1, 4))
    sin_t = jnp.tile(jnp.concatenate([-sin, sin], axis=1), (1, 2))
    C = RET_CHUNK
    H = RET_HEADS
    log_gamma = jnp.log1p(-jnp.power(2.0, -5.0 - jnp.arange(H, dtype=F32)))
    i = jnp.arange(C, dtype=F32)
    rel = i[:, None] - i[None, :]
    decay = jnp.where(rel >= 0, jnp.exp(jnp.maximum(rel, 0.0)[None] * log_gamma[:, None, None]), 0.0)
    xi = jnp.exp((i + 1.0)[:, None] * log_gamma[None, :])
    zeta = jnp.exp((C - 1.0 - i)[:, None] * log_gamma[None, :])
    gamma_c = jnp.exp(C * log_gamma)
    per_pair = lambda t: jnp.repeat(t.T.reshape(H // 2, 2, -1), HEAD_DIM, axis=1).transpose(0, 2, 1)
    gammac = jnp.repeat(gamma_c.reshape(H // 2, 2), HEAD_DIM, axis=1)[:, None, :]
    return cos_t, sin_t, decay, per_pair(xi), per_pair(zeta), gammac


def _retention(proj3, gain):
    B, S, _ = proj3.shape
    cos_t, sin_t, decay, xi, zeta, gammac = _retention_tables(S)
    avg = jnp.asarray(_block_avg(LANES), BF16)
    col = lambda c: pl.BlockSpec((1, S, RET_WIDTH), lambda b: (b, 0, c // RET_WIDTH))
    full = lambda a: pl.BlockSpec(a.shape, lambda b: (0,) * a.ndim)
    return pl.pallas_call(
        _retention_body,
        grid=(B,),
        in_specs=[col(COL_QR), col(COL_KR), col(COL_VR), col(COL_GR), full(cos_t), full(sin_t), full(decay),
                  full(xi), full(zeta), full(gammac), full(gain), full(avg)],
        out_specs=pl.BlockSpec((1, S, RET_WIDTH), lambda b: (b, 0, 0)),
        out_shape=jax.ShapeDtypeStruct((B, S, RET_WIDTH), BF16),
        scratch_shapes=[pltpu.VMEM((RET_HEADS // 2, LANES, LANES), F32)],
        compiler_params=_cparams(1),
        name="retention",
    )(proj3, proj3, proj3, proj3, cos_t, sin_t, decay, xi, zeta, gammac, gain, avg)


def _mem_prep_body(mem_ref, g_ref, wkv_ref, kgain_ref, avg_ref, k_ref, v_ref):
    hm = _rms_full(mem_ref[0], g_ref[...]).astype(BF16)
    kv = _dot(hm, wkv_ref[...])
    k = kv[:, :MEM_WIDTH]
    ms = _seg_mean(k * k, avg_ref[...])
    k_ref[0] = (k * lax.rsqrt(ms + EPS) * kgain_ref[...]).astype(BF16)
    v_ref[0] = kv[:, MEM_WIDTH:].astype(BF16)


def _mem_prep(mem, g, wkv, kgain):
    B, M, _ = mem.shape
    avg = jnp.asarray(_block_avg(MEM_WIDTH), BF16)
    full = lambda a: pl.BlockSpec(a.shape, lambda b: (0,) * a.ndim)
    out_spec = pl.BlockSpec((1, M, MEM_WIDTH), lambda b: (b, 0, 0))
    out_shape = jax.ShapeDtypeStruct((B, M, MEM_WIDTH), BF16)
    return pl.pallas_call(
        _mem_prep_body,
        grid=(B,),
        in_specs=[pl.BlockSpec((1, M, D_MODEL), lambda b: (b, 0, 0)), full(g), full(wkv), full(kgain), full(avg)],
        out_specs=[out_spec, out_spec],
        out_shape=[out_shape, out_shape],
        compiler_params=_cparams(1),
        name="mem_prep",
    )(mem, g, wkv, kgain, avg)


def _post_body(x_ref, oa_ref, ob_ref, wout_ref, mk_ref, mv_ref, gx_ref, wq_ref, qgain_ref, avg_ref, wo_ref,
               gf_ref, wr_ref, br_ref, x2_ref, h_ref, route_ref):
    tm = x_ref.shape[1] // POST_CHAINS
    chains = range(POST_CHAINS)
    rows = [slice(c * tm, (c + 1) * tm) for c in chains]
    x1 = [x_ref[0, rows[c]] + _dot(oa_ref[0, rows[c]], wout_ref[0:NSA_WIDTH, :])
          + _dot(ob_ref[0, rows[c]], wout_ref[NSA_WIDTH:, :]) for c in chains]

    h = [_rms_full(x1[c], gx_ref[...]).astype(BF16) for c in chains]
    q = [_dot(h[c], wq_ref[...]) for c in chains]
    ms = [_seg_mean(q[c] * q[c], avg_ref[...]) for c in chains]
    q = [q[c] * lax.rsqrt(ms[c] + EPS) * qgain_ref[...] * (HEAD_DIM ** -0.5) for c in chains]
    lane = lax.broadcasted_iota(jnp.int32, (tm, LANES), 1)
    lo = lane < HEAD_DIM
    heads = [(c, p, half) for c in chains for p in range(MEM_HEADS // 2) for half in range(2)]
    s = [_dot_nt(jnp.where(lo if half == 0 else ~lo, q[c][:, p * LANES:(p + 1) * LANES], 0.0).astype(BF16),
                 mk_ref[0, :, p * LANES:(p + 1) * LANES]) for c, p, half in heads]
    e = [jnp.exp(s[i] - jnp.max(s[i], axis=-1, keepdims=True)) for i in range(len(heads))]
    pr = [(e[i] / jnp.sum(e[i], axis=-1, keepdims=True)).astype(BF16) for i in range(len(heads))]
    outs = [_dot(pr[i], mv_ref[0, :, heads[i][1] * LANES:(heads[i][1] + 1) * LANES]) for i in range(len(heads))]
    per_chain = MEM_HEADS
    o = [jnp.concatenate([jnp.where(lo, outs[c * per_chain + 2 * p], outs[c * per_chain + 2 * p + 1])
                          for p in range(MEM_HEADS // 2)], axis=1).astype(BF16) for c in chains]
    x2 = [x1[c] + _dot(o[c], wo_ref[...]) for c in chains]
    for c in chains:
        x2_ref[0, rows[c]] = x2[c]

    hf = [_rms_full(x2[c], gf_ref[...]) for c in chains]
    for c in chains:
        h_ref[0, rows[c]] = hf[c]
    logits = [_dot(hf[c].astype(BF16), wr_ref[...]) + br_ref[...] for c in chains]
    lane_f = lane.astype(F32)
    big = float(LANES)
    for c in chains:
        gl = jnp.where(lane < N_GROUPS, logits[c], BELOW_ALL)
        gmax = jnp.max(gl, axis=-1, keepdims=True)
        grp = jnp.min(jnp.where(gl == gmax, lane_f, big), axis=-1, keepdims=True)
        g_w = 1.0 / jnp.sum(jnp.where(lane < N_GROUPS, jnp.exp(gl - gmax), 0.0), axis=-1, keepdims=True)
        e_lo = N_GROUPS + grp * EXPERTS_PER_GROUP
        el = jnp.where((lane_f >= e_lo) & (lane_f < e_lo + EXPERTS_PER_GROUP), logits[c], BELOW_ALL)
        v0 = jnp.max(el, axis=-1, keepdims=True)
        i0 = jnp.min(jnp.where(el == v0, lane_f, big), axis=-1, keepdims=True)
        el = jnp.where(lane_f == i0, BELOW_ALL, el)
        v1 = jnp.max(el, axis=-1, keepdims=True)
        i1 = jnp.min(jnp.where(el == v1, lane_f, big), axis=-1, keepdims=True)
        e1 = jnp.exp(v1 - v0)
        w0 = g_w / (1.0 + e1)
        w1 = g_w * e1 / (1.0 + e1)
        route_ref[0, rows[c]] = jnp.where(lane == 0, i0 - N_GROUPS,
                                          jnp.where(lane == 1, i1 - N_GROUPS,
                                                    jnp.where(lane == 2, w0, jnp.where(lane == 3, w1, 0.0))))


def _post(x, oa, ob, wout, mk, mv, gx, wq, qgain, wo, gf, wr, br):
    B, S, _ = x.shape
    tm = POST_TM
    avg = jnp.asarray(_block_avg(MEM_WIDTH), BF16)
    full = lambda a: pl.BlockSpec(a.shape, lambda b, i: (0,) * a.ndim)
    per_b = lambda a: pl.BlockSpec((1,) + a.shape[1:], lambda b, i: (b,) + (0,) * (a.ndim - 1))
    tile = lambda w: pl.BlockSpec((1, tm, w), lambda b, i: (b, i, 0))
    return pl.pallas_call(
        _post_body,
        grid=(B, S // tm),
        in_specs=[tile(D_MODEL), tile(NSA_WIDTH), tile(RET_WIDTH), full(wout), per_b(mk), per_b(mv), full(gx),
                  full(wq), full(qgain), full(avg), full(wo), full(gf), full(wr), full(br)],
        out_specs=[tile(D_MODEL), tile(D_MODEL), tile(LANES)],
        out_shape=[jax.ShapeDtypeStruct((B, S, D_MODEL), F32), jax.ShapeDtypeStruct((B, S, D_MODEL), F32),
                   jax.ShapeDtypeStruct((B, S, LANES), F32)],
        compiler_params=_cparams(2),
        name="post_mixer",
    )(x, oa, ob, wout, mk, mv, gx, wq, qgain, avg, wo, gf, wr, br)


def _rank_body(route_ref, tri_ref, rank_ref, count_ref):
    route = route_ref[...]
    tm = route.shape[0]
    lane = lax.broadcasted_iota(jnp.int32, (tm, LANES), 1).astype(F32)
    oh0 = lane == route[:, 0:1]
    oh1 = lane == route[:, 1:2]
    both = jnp.where(oh0, 1.0, 0.0) + jnp.where(oh1, 1.0, 0.0)
    before = _dot(tri_ref[...], both.astype(BF16))
    r0 = jnp.sum(jnp.where(oh0, before, 0.0), axis=-1, keepdims=True)
    r1 = jnp.sum(jnp.where(oh1, before, 0.0), axis=-1, keepdims=True)
    lane_i = lax.broadcasted_iota(jnp.int32, (tm, LANES), 1)
    rank_ref[...] = jnp.where(lane_i == 0, r0, jnp.where(lane_i == 1, r1, 0.0))
    count_ref[0] = jnp.broadcast_to(jnp.sum(both, axis=0, keepdims=True), count_ref.shape[1:])


def _rank(route2d):
    T = route2d.shape[0]
    tm = MOE_TM
    tri = jnp.asarray(np.tril(np.ones((tm, tm), np.float32), -1), BF16)
    return pl.pallas_call(
        _rank_body,
        grid=(T // tm,),
        in_specs=[pl.BlockSpec((tm, LANES), lambda i: (i, 0)), pl.BlockSpec((tm, tm), lambda i: (0, 0))],
        out_specs=[pl.BlockSpec((tm, LANES), lambda i: (i, 0)), pl.BlockSpec((1, 8, LANES), lambda i: (i, 0, 0))],
        out_shape=[jax.ShapeDtypeStruct((T, LANES), F32), jax.ShapeDtypeStruct((T // tm, 8, LANES), F32)],
        compiler_params=_cparams(1),
        name="moe_rank",
    )(route2d, tri)


def _row_copy(src, dst, sem):
    return pltpu.make_async_copy(src, dst, sem)


def _run_pieces(n, max_piece, fn):
    b = RUN_ALIGN
    while b <= max_piece:
        pl.when((n & b) != 0)(functools.partial(fn, n & (-2 * b), b))
        b *= 2


def _move_groups(i, gmap_ref, n_loc, copy):
    n_groups = n_loc // RUN_ALIGN
    for j in range(n_groups):
        glob = pl.multiple_of(gmap_ref[i * n_groups + j], RUN_ALIGN)
        copy(pl.ds(j * RUN_ALIGN, RUN_ALIGN), pl.ds(glob, RUN_ALIGN)).start()


def _local_positions(route, rank, loff_row):
    lane = lax.broadcasted_iota(jnp.int32, route.shape, 1).astype(F32)
    pos = []
    for s in range(2):
        base = jnp.sum(jnp.where(lane == route[:, s:s + 1], loff_row, 0.0), axis=-1, keepdims=True)
        pos.append(base + rank[:, s:s + 1])
    return pos


def _scatter_body(gmap_ref, tstart_ref, tlen_ref, nact_ref,
                  h_ref, route_ref, rank_ref, lofff_ref, xs_ref, xloc, zbuf, sem):
    i = pl.program_id(0)
    tm = h_ref.shape[0]
    n_loc = xloc.shape[0]

    def tails(start):
        def per_expert(e, carry):
            n = tlen_ref[e]
            st = tstart_ref[e]

            def piece(off, size):
                c = _row_copy(zbuf.at[pl.ds(0, size)], xs_ref.at[pl.ds(pl.multiple_of(st + off, RUN_ALIGN), size)], sem)
                c.start() if start else c.wait()

            _run_pieces(n, MOE_RB // 2, piece)
            return carry

        lax.fori_loop(0, N_EXPERTS, per_expert, 0)

    def unused(start):
        rows = zbuf.shape[0]

        def per_unit(u, carry):
            c = _row_copy(zbuf, xs_ref.at[pl.ds(pl.multiple_of(u * rows, rows), rows)], sem)
            c.start() if start else c.wait()
            return carry

        lax.fori_loop(nact_ref[0] * (MOE_RB // rows), xs_ref.shape[0] // rows, per_unit, 0)

    @pl.when(i == 0)
    def _():
        zbuf[...] = jnp.zeros(zbuf.shape, F32)
        tails(True)
        unused(True)
        tails(False)
        unused(False)

    pos = _local_positions(route_ref[...], rank_ref[...], lofff_ref[0, 0:1, :])
    col = lax.broadcasted_iota(jnp.int32, (tm, n_loc), 1).astype(F32)
    perm_t = jnp.where((col == pos[0]) | (col == pos[1]), 1.0, 0.0).astype(BF16)
    xloc[...] = _dot_tn(perm_t, h_ref[...].astype(BF16))

    _move_groups(i, gmap_ref, n_loc, lambda loc, glob: _row_copy(xloc.at[loc], xs_ref.at[glob], sem))
    _row_copy(xloc, xs_ref.at[pl.ds(0, n_loc)], sem).wait()


def _scatter_rows(tables, h2d, route2d, rank, loff_f, n_rows):
    T = h2d.shape[0]
    tm = MOE_TM
    n_loc = MOE_NLOC
    tile = lambda w: pl.BlockSpec((tm, w), lambda i, *_: (i, 0))
    grid_spec = pltpu.PrefetchScalarGridSpec(
        num_scalar_prefetch=4,
        grid=(T // tm,),
        in_specs=[tile(D_MODEL), tile(LANES), tile(LANES), pl.BlockSpec((1, 8, LANES), lambda i, *_: (i, 0, 0))],
        out_specs=pl.BlockSpec(memory_space=pl.ANY),
        scratch_shapes=[pltpu.VMEM((n_loc, D_MODEL), F32), pltpu.VMEM((MOE_RB // 2, D_MODEL), F32),
                        pltpu.SemaphoreType.DMA(())],
    )
    return pl.pallas_call(
        _scatter_body,
        grid_spec=grid_spec,
        out_shape=jax.ShapeDtypeStruct((n_rows, D_MODEL), F32),
        compiler_params=_cparams(1),
        name="moe_scatter",
    )(*tables, h2d, route2d, rank, loff_f)


def _expert_body(blk_e_ref, n_act_ref, xs_ref, wg_ref, wu_ref, wd_ref, ys_ref):
    i = pl.program_id(0)

    @pl.when(i < n_act_ref[0])
    def _():
        x = xs_ref[...].astype(BF16)
        a = _dot(x, wg_ref[0].astype(BF16))
        b = _dot(x, wu_ref[0].astype(BF16))
        ys_ref[...] = _dot((_silu(a) * b).astype(BF16), wd_ref[0].astype(BF16))

    @pl.when(i >= n_act_ref[0])
    def _():
        ys_ref[...] = jnp.zeros(ys_ref.shape, F32)


def _experts(blk_e, n_act, xs, n_rows, wg, wu, wd):
    rb = MOE_RB
    grid_spec = pltpu.PrefetchScalarGridSpec(
        num_scalar_prefetch=2,
        grid=(n_rows // rb,),
        in_specs=[pl.BlockSpec((rb, D_MODEL), lambda i, be, na: (jnp.minimum(i, na[0] - 1), 0)),
                  pl.BlockSpec((1, D_MODEL, EXPERT_FF), lambda i, be, na: (be[i], 0, 0)),
                  pl.BlockSpec((1, D_MODEL, EXPERT_FF), lambda i, be, na: (be[i], 0, 0)),
                  pl.BlockSpec((1, EXPERT_FF, D_MODEL), lambda i, be, na: (be[i], 0, 0))],
        out_specs=pl.BlockSpec((rb, D_MODEL), lambda i, be, na: (i, 0)),
    )
    return pl.pallas_call(
        _expert_body,
        grid_spec=grid_spec,
        out_shape=jax.ShapeDtypeStruct((n_rows, D_MODEL), F32),
        compiler_params=_cparams(1),
        name="moe_experts",
    )(blk_e, n_act, xs, wg, wu, wd)


def _combine_body(gmap_ref, x_ref, route_ref, rank_ref, lofff_ref, ys_ref, o_ref, yloc, sem):
    i = pl.program_id(0)
    tm = x_ref.shape[0]
    n_loc = yloc.shape[0]

    _move_groups(i, gmap_ref, n_loc, lambda loc, glob: _row_copy(ys_ref.at[glob], yloc.at[loc], sem))
    _row_copy(ys_ref.at[pl.ds(0, n_loc)], yloc, sem).wait()

    route = route_ref[...]
    pos = _local_positions(route, rank_ref[...], lofff_ref[0, 0:1, :])
    col = lax.broadcasted_iota(jnp.int32, (tm, n_loc), 1).astype(F32)
    perm_w = (jnp.where(col == pos[0], route[:, 2:3], 0.0) + jnp.where(col == pos[1], route[:, 3:4], 0.0)).astype(BF16)
    o_ref[...] = x_ref[...] + _dot(perm_w, yloc[...].astype(BF16))


def _combine(tables, x2d, route2d, rank, loff_f, ys):
    T = x2d.shape[0]
    tm = MOE_TM
    n_loc = MOE_NLOC
    tile = lambda w: pl.BlockSpec((tm, w), lambda i, *_: (i, 0))
    grid_spec = pltpu.PrefetchScalarGridSpec(
        num_scalar_prefetch=1,
        grid=(T // tm,),
        in_specs=[tile(D_MODEL), tile(LANES), tile(LANES), pl.BlockSpec((1, 8, LANES), lambda i, *_: (i, 0, 0)),
                  pl.BlockSpec(memory_space=pl.ANY)],
        out_specs=tile(D_MODEL),
        scratch_shapes=[pltpu.VMEM((n_loc, D_MODEL), F32), pltpu.SemaphoreType.DMA(())],
    )
    return pl.pallas_call(
        _combine_body,
        grid_spec=grid_spec,
        out_shape=jax.ShapeDtypeStruct((T, D_MODEL), F32),
        compiler_params=_cparams(1),
        name="moe_combine",
    )(*tables, x2d, route2d, rank, loff_f, ys)


def _permute_w_in(w):
    kv0 = NSA_WIDTH
    gate0 = kv0 + 6 * NSA_GROUPS * HEAD_DIM
    ret0 = gate0 + 3 * NSA_HEADS
    pad = jnp.zeros((w.shape[0], PROJ_PAD - w.shape[1]), w.dtype)
    return jnp.concatenate([w[:, :kv0], w[:, ret0:], w[:, kv0:gate0], w[:, gate0:ret0], pad], axis=1).astype(BF16)


def _compress_weights(pos, w1, w2):
    eye = jnp.eye(NSA_GROUPS, dtype=F32)
    w1r = w1.reshape(CMP_BLOCK, HEAD_DIM, CMP_HIDDEN)
    w1b = jnp.einsum('ldh,gk->lgdkh', w1r, eye).reshape(2, CMP_STRIDE * LANES, NSA_GROUPS * CMP_HIDDEN)
    w2b = jnp.einsum('hd,gk->ghkd', w2, eye).reshape(NSA_GROUPS * CMP_HIDDEN, LANES)
    posb = jnp.tile(pos, (1, NSA_GROUPS)).reshape(2, 1, CMP_STRIDE * LANES)
    return posb, w1b.astype(BF16), w2b.astype(BF16)


def _dup2(g):
    return jnp.tile(g.reshape(1, HEAD_DIM), (1, 2))


def kernel(x, mem, mix_norm, w_in, nsa_q_norm, nsa_kcmp_norm, nsa_ksel_norm, nsa_kwin_norm, cmp_pos_k, cmp_pos_v, cmp_k_w1, cmp_k_w2, cmp_v_w1, cmp_v_w2, nsa_out_norm, ret_out_norm, w_out, mem_x_norm, mem_kv_norm, mem_wq, mem_wkv, mem_q_norm, mem_k_norm, mem_wo, ffn_norm, router_group_w, router_group_b, router_expert_w, router_expert_b, exp_w_gate, exp_w_up, exp_w_down):
    B, S, D = x.shape
    T = B * S
    depth = mix_norm.shape[0]
    for l in range(depth):
        proj = _proj(x.reshape(T, D), mix_norm[l].reshape(1, D), _permute_w_in(w_in[l])).reshape(B, S, PROJ_PAD)
        pk, w1k, w2k = _compress_weights(cmp_pos_k[l], cmp_k_w1[l], cmp_k_w2[l])
        pv, w1v, w2v = _compress_weights(cmp_pos_v[l], cmp_v_w1[l], cmp_v_w2[l])
        gains = jnp.stack([_dup2(nsa_kcmp_norm[l]), _dup2(nsa_ksel_norm[l]), _dup2(nsa_kwin_norm[l])])
        kcmp, vcmp, ks, vs, kw, vw = _nsa_prep(proj, jnp.stack([pk, pv]), jnp.stack([w1k, w1v]),
                                               jnp.stack([w2k, w2v]), gains)
        o_a = _nsa_attn(proj, kcmp, vcmp, ks, vs, kw, vw,
                        jnp.tile(nsa_q_norm[l].reshape(1, HEAD_DIM), (1, NSA_HEADS)), gains[1:],
                        nsa_out_norm[l].reshape(1, NSA_WIDTH))
        o_b = _retention(proj, ret_out_norm[l].reshape(1, RET_WIDTH))
        mk, mv = _mem_prep(mem, mem_kv_norm[l].reshape(1, D), mem_wkv[l].astype(BF16),
                           jnp.tile(mem_k_norm[l].reshape(1, HEAD_DIM), (1, MEM_HEADS)))
        w_r = jnp.concatenate([router_group_w[l],
                               router_expert_w[l].transpose(1, 0, 2).reshape(D, N_EXPERTS),
                               jnp.zeros((D, LANES - N_GROUPS - N_EXPERTS), F32)], axis=1).astype(BF16)
        b_r = jnp.concatenate([router_group_b[l], router_expert_b[l].reshape(N_EXPERTS),
                               jnp.zeros((LANES - N_GROUPS - N_EXPERTS,), F32)]).reshape(1, LANES)
        x2, hf, route = _post(x, o_a, o_b, w_out[l].astype(BF16), mk, mv, mem_x_norm[l].reshape(1, D),
                              mem_wq[l].astype(BF16), jnp.tile(mem_q_norm[l].reshape(1, HEAD_DIM), (1, MEM_HEADS)),
                              mem_wo[l].astype(BF16), ffn_norm[l].reshape(1, D), w_r, b_r)
        route2d = route.reshape(T, LANES)
        rank, counts = _rank(route2d)
        n_tiles = T // MOE_TM
        cnt = counts[:, 0, :N_EXPERTS].astype(jnp.int32)
        cnt = (cnt + RUN_ALIGN - 1) // RUN_ALIGN * RUN_ALIGN
        loff = jnp.cumsum(cnt, axis=1) - cnt
        total = jnp.sum(cnt, axis=0)
        padded = (total + MOE_RB - 1) // MOE_RB * MOE_RB
        pend = jnp.cumsum(padded)
        pstart = pend - padded
        goff = pstart[None, :] + jnp.cumsum(cnt, axis=0) - cnt
        n_rows = 2 * T + n_tiles * N_EXPERTS * RUN_ALIGN + N_EXPERTS * MOE_RB
        n_blocks = n_rows // MOE_RB
        blk_row0 = jnp.arange(n_blocks, dtype=jnp.int32) * MOE_RB
        blk_e = jnp.minimum(jnp.sum((pend[None, :] <= blk_row0[:, None]).astype(jnp.int32), axis=1), N_EXPERTS - 1)
        n_act = (pend[-1:] // MOE_RB).astype(jnp.int32)
        loff_f = jnp.broadcast_to(jnp.pad(loff.astype(F32), ((0, 0), (0, LANES - N_EXPERTS)))[:, None, :],
                                  (n_tiles, 8, LANES))
        grp_row = jnp.arange(MOE_NLOC // RUN_ALIGN, dtype=jnp.int32) * RUN_ALIGN
        inside = ((loff[:, None, :] <= grp_row[None, :, None])
                  & (grp_row[None, :, None] < (loff + cnt)[:, None, :])).astype(jnp.int32)
        shift = jnp.sum(inside * (goff - loff)[:, None, :], axis=2)
        used = jnp.sum(inside, axis=2) > 0
        gmap_scatter = jnp.where(used, shift + grp_row[None, :], n_rows + grp_row[None, :]).reshape(-1)
        gmap_gather = jnp.where(used, shift + grp_row[None, :], 0).reshape(-1)
        xs = _scatter_rows((gmap_scatter, pstart + total, padded - total, n_act), hf.reshape(T, D), route2d, rank, loff_f,
                           n_rows + MOE_NLOC)
        ys = _experts(blk_e, n_act, xs, n_rows, exp_w_gate[l], exp_w_up[l], exp_w_down[l])
        x = _combine((gmap_gather,), x2.reshape(T, D), route2d, rank, loff_f, ys).reshape(B, S, D)
    return x
```

```python
import functools

import numpy as np
import jax
import jax.numpy as jnp
from jax import lax
from jax.experimental import pallas as pl
from jax.experimental.pallas import tpu as pltpu

F32 = jnp.float32
BF16 = jnp.bfloat16

D_MODEL = 1024
HEAD_DIM = 64
LANES = 128
NSA_HEADS = 8
NSA_GROUPS = 2
NSA_WIDTH = NSA_HEADS * HEAD_DIM
CMP_BLOCK = 32
CMP_STRIDE = 16
CMP_HIDDEN = 2 * HEAD_DIM
SEL_BLOCK = 64
SEL_TOPK = 8
WINDOW = 512
RET_HEADS = 8
RET_WIDTH = RET_HEADS * HEAD_DIM
RET_CHUNK = 128
ROPE_BASE = 10000.0
MEM_HEADS = 4
MEM_WIDTH = MEM_HEADS * HEAD_DIM
N_GROUPS = 4
EXPERTS_PER_GROUP = 8
N_EXPERTS = N_GROUPS * EXPERTS_PER_GROUP
EXPERT_FF = D_MODEL // 4
EPS = 1e-6
NEG_INF = -1e30
FORCE_SCORE = 1e9
BELOW_ALL = -3e38
MAX_FIXED_SHIFT = 40.0

COL_QA = 0
COL_QR, COL_KR, COL_VR, COL_GR = 512, 1024, 1536, 2048
COL_KVC, COL_KSV, COL_KWV = 2560, 2816, 3072
COL_GATE = 3328
PROJ_PAD = 3456

PROJ_TM = 512
NSA_TQ = 256
SEL_KC = 512
WIN_KEYS = WINDOW + NSA_TQ
POST_TM = 512
POST_CHAINS = 2
MOE_TM = 512
MOE_RB = 256
RUN_ALIGN = 16
MOE_NLOC = 2 * MOE_TM + N_EXPERTS * RUN_ALIGN
VMEM_LIMIT = 56 * 1024 * 1024


def _cparams(n_axes):
    return pltpu.CompilerParams(dimension_semantics=("arbitrary",) * n_axes,
                                vmem_limit_bytes=VMEM_LIMIT)


def _dot(a, b):
    return jnp.dot(a, b, preferred_element_type=F32)


def _dot_nt(a, b):
    return lax.dot_general(a, b, (((1,), (1,)), ((), ())), preferred_element_type=F32)


def _dot_tn(a, b):
    return lax.dot_general(a, b, (((0,), (0,)), ((), ())), preferred_element_type=F32)


def _rms_full(x, g):
    ms = jnp.mean(x * x, axis=-1, keepdims=True)
    return x * lax.rsqrt(ms + EPS) * g


def _seg_mean(x, avg):
    return _dot(x.astype(BF16), avg)


def _silu(x):
    return x * (1.0 / (1.0 + jnp.exp(-x)))


def _sigmoid(x):
    return 1.0 / (1.0 + jnp.exp(-x))


def _block_avg(width):
    i = np.arange(width)
    return ((i[:, None] // HEAD_DIM == i[None, :] // HEAD_DIM) / HEAD_DIM).astype(np.float32)


def _proj_body(x_ref, g_ref, w_ref, o_ref):
    h = _rms_full(x_ref[...], g_ref[...]).astype(BF16)
    step = PROJ_PAD // 3
    for j in range(3):
        o_ref[:, j * step:(j + 1) * step] = _dot(h, w_ref[:, j * step:(j + 1) * step]).astype(BF16)


def _proj(x2d, g, w):
    T = x2d.shape[0]
    return pl.pallas_call(
        _proj_body,
        grid=(T // PROJ_TM,),
        in_specs=[pl.BlockSpec((PROJ_TM, D_MODEL), lambda i: (i, 0)),
                  pl.BlockSpec((1, D_MODEL), lambda i: (0, 0)),
                  pl.BlockSpec((D_MODEL, PROJ_PAD), lambda i: (0, 0))],
        out_specs=pl.BlockSpec((PROJ_TM, PROJ_PAD), lambda i: (i, 0)),
        out_shape=jax.ShapeDtypeStruct((T, PROJ_PAD), BF16),
        compiler_params=_cparams(1),
        name="proj",
    )(x2d, g, w)


def _dup_groups(x):
    lane = lax.broadcasted_iota(jnp.int32, x.shape, 1)
    xs = pltpu.roll(x, HEAD_DIM, axis=1)
    lo = lane < HEAD_DIM
    return jnp.where(lo, x, xs), jnp.where(lo, xs, x)


def _ones_groups(x):
    lane = lax.broadcasted_iota(jnp.int32, x.shape, 1)
    lo = lane < HEAD_DIM
    return jnp.where(lo, x, 1.0), jnp.where(lo, pltpu.roll(x, HEAD_DIM, axis=1), 1.0)


def _nsa_prep_body(kvc_ref, ksv_ref, kwv_ref, pos_ref, w1_ref, w2_ref, gain_ref, avg_ref,
                   kcmp_ref, vcmp_ref, ks_ref, vs_ref, kw_ref, vw_ref, scr_k, scr_v):
    avg = avg_ref[...]
    n_c = scr_k.shape[0] // CMP_STRIDE
    scr_k[...] = kvc_ref[0, :, 0:LANES].astype(F32)
    scr_v[...] = kvc_ref[0, :, LANES:2 * LANES].astype(F32)
    for j, out_ref, scr in ((0, kcmp_ref, scr_k), (1, vcmp_ref, scr_v)):
        ycat = jnp.concatenate(
            [scr[pl.ds(l, n_c, stride=CMP_STRIDE), :] for l in range(CMP_STRIDE)], axis=1)
        first = _dot((ycat + pos_ref[j, 0]).astype(BF16), w1_ref[j, 0])
        second = _dot((ycat + pos_ref[j, 1]).astype(BF16), w1_ref[j, 1])
        hidden = first + pltpu.roll(second, n_c - 1, axis=0)
        cmp_tok = _dot(_silu(hidden).astype(BF16), w2_ref[j])
        if j == 0:
            ms = _seg_mean(cmp_tok * cmp_tok, avg)
            cmp_tok = cmp_tok * lax.rsqrt(ms + EPS) * gain_ref[0]
        d0, d1 = _dup_groups(cmp_tok) if j == 0 else _ones_groups(cmp_tok)
        out_ref[0, 0] = d0.astype(BF16)
        out_ref[0, 1] = d1.astype(BF16)

    for src_ref, k_out, v_out, gi in ((ksv_ref, ks_ref, vs_ref, 1), (kwv_ref, kw_ref, vw_ref, 2)):
        k = src_ref[0, :, 0:LANES].astype(F32)
        ms = _seg_mean(k * k, avg)
        k = k * lax.rsqrt(ms + EPS) * gain_ref[gi]
        d0, d1 = _dup_groups(k)
        k_out[0, 0] = d0.astype(BF16)
        k_out[0, 1] = d1.astype(BF16)
        d0, d1 = _ones_groups(src_ref[0, :, LANES:2 * LANES].astype(F32))
        v_out[0, 0] = d0.astype(BF16)
        v_out[0, 1] = d1.astype(BF16)


def _nsa_prep(proj3, pos, w1, w2, gains):
    B, S, _ = proj3.shape
    n_c = S // CMP_STRIDE
    avg = jnp.asarray(_block_avg(LANES), BF16)
    col = lambda c: pl.BlockSpec((1, S, 2 * LANES), lambda b: (b, 0, c // (2 * LANES)))
    full = lambda a: pl.BlockSpec(a.shape, lambda b: (0,) * a.ndim)
    cmp_spec = pl.BlockSpec((1, NSA_GROUPS, n_c, LANES), lambda b: (b, 0, 0, 0))
    seq_spec = pl.BlockSpec((1, NSA_GROUPS, S, LANES), lambda b: (b, 0, 0, 0))
    cmp_shape = jax.ShapeDtypeStruct((B, NSA_GROUPS, n_c, LANES), BF16)
    seq_shape = jax.ShapeDtypeStruct((B, NSA_GROUPS, S, LANES), BF16)
    return pl.pallas_call(
        _nsa_prep_body,
        grid=(B,),
        in_specs=[col(COL_KVC), col(COL_KSV), col(COL_KWV), full(pos), full(w1), full(w2), full(gains), full(avg)],
        out_specs=[cmp_spec, cmp_spec, seq_spec, seq_spec, seq_spec, seq_spec],
        out_shape=[cmp_shape, cmp_shape, seq_shape, seq_shape, seq_shape, seq_shape],
        scratch_shapes=[pltpu.VMEM((S, LANES), F32), pltpu.VMEM((S, LANES), F32)],
        compiler_params=_cparams(1),
        name="nsa_prep",
    )(proj3, proj3, proj3, pos, w1, w2, gains, avg)


def _nsa_attn_body(q_ref, gate_ref, kcmp_ref, vcmp_ref, ks_ref, vs_ref, kw_ref, vw_ref,
                   qgain_ref, kgain_ref, ogain_ref, avgq_ref, avgo_ref, msct_ref, esel_ref, egate_ref, wbias_ref,
                   dbias_ref,
                   o_ref, m_scr, acc_scr):
    tq = q_ref.shape[1]
    n_cmp = kcmp_ref.shape[2]
    n_sel = msct_ref.shape[0]
    kc_len = esel_ref.shape[2]
    rows = 4 * tq
    qi = pl.program_id(1)
    q0 = qi * tq

    q = q_ref[0].astype(F32)
    ms = _seg_mean(q * q, avgq_ref[...])
    qn = q * lax.rsqrt(ms + EPS) * qgain_ref[...] * (HEAD_DIM ** -0.5)

    gate_sig = _sigmoid(gate_ref[0].astype(F32)).astype(BF16)
    gates = [_dot(gate_sig, egate_ref[j]) for j in range(3)]

    lane_q = lax.broadcasted_iota(jnp.int32, (tq, LANES), 1)
    lo_q = lane_q < HEAD_DIM
    lo_r = lax.broadcasted_iota(jnp.int32, (rows, LANES), 1) < HEAD_DIM

    blk = lax.broadcasted_iota(jnp.int32, (n_sel, tq), 0)
    cur = lax.shift_right_logical(q0 + lax.broadcasted_iota(jnp.int32, (n_sel, tq), 1), int(np.log2(SEL_BLOCK)))
    forced = (blk == 0) | (blk == cur) | (blk == cur - 1)
    future = blk > cur
    blk_f = blk.astype(F32)

    def heads4(x):
        return jnp.concatenate([x] * 4, axis=0)

    def normalised_pairs(acc, guard):
        rolled = pltpu.roll(acc, HEAD_DIM, axis=1)
        den = jnp.where(lo_r, rolled, acc)
        if guard:
            den = jnp.maximum(den, 1e-30)
        out = []
        for p in range(2):
            ev = slice((2 * p) * tq, (2 * p + 1) * tq)
            od = slice((2 * p + 1) * tq, (2 * p + 2) * tq)
            out.append(jnp.where(lo_q, acc[ev] / den[ev], rolled[od] / den[od]))
        return out

    groups = range(NSA_GROUPS)
    qs = []
    for g in groups:
        slabs = [qn[:, (2 * g + p) * LANES:(2 * g + p + 1) * LANES] for p in range(2)]
        qs.append(jnp.concatenate(
            [jnp.where(lo_q, slabs[0], 0.0), jnp.where(lo_q, 0.0, slabs[0]),
             jnp.where(lo_q, slabs[1], 0.0), jnp.where(lo_q, 0.0, slabs[1])], axis=0).astype(BF16))

    r_c = lax.broadcasted_iota(jnp.int32, (rows, n_cmp), 0)
    c_c = lax.broadcasted_iota(jnp.int32, (rows, n_cmp), 1)
    cmask = (c_c * CMP_STRIDE + (CMP_BLOCK - 1)) <= q0 + (r_c & (tq - 1))
    s_c = [jnp.where(cmask, _dot_nt(qs[g], kcmp_ref[0, g]), NEG_INF) for g in groups]
    e_c = [jnp.where(cmask, jnp.exp(s_c[g] - jnp.max(s_c[g], axis=-1, keepdims=True)), 0.0) for g in groups]
    acc_c = [_dot(e_c[g].astype(BF16), vcmp_ref[0, g]) for g in groups]
    l_c = [jnp.where(lo_r, pltpu.roll(acc_c[g], HEAD_DIM, axis=1), acc_c[g]) for g in groups]
    p_c = [e_c[g] / jnp.maximum(l_c[g], 1e-30) for g in groups]

    p_sum = [p_c[g][0:tq] + p_c[g][tq:2 * tq] + p_c[g][2 * tq:3 * tq] + p_c[g][3 * tq:4 * tq] for g in groups]
    p_hi = [p_sum[g].astype(BF16) for g in groups]
    p_lo = [(p_sum[g] - p_hi[g].astype(F32)).astype(BF16) for g in groups]
    imp = [_dot_nt(msct_ref[...], p_hi[g]) + _dot_nt(msct_ref[...], p_lo[g]) for g in groups]
    v = [jnp.where(forced, FORCE_SCORE, jnp.where(future, NEG_INF, imp[g])) for g in groups]
    sel = [jnp.zeros((n_sel, tq), F32) for g in groups]
    for _ in range(SEL_TOPK):
        mx = [jnp.max(v[g], axis=0, keepdims=True) for g in groups]
        first = [jnp.min(jnp.where(v[g] == mx[g], blk_f, float(LANES)), axis=0, keepdims=True) for g in groups]
        pick = [blk_f == first[g] for g in groups]
        sel = [jnp.where(pick[g], 1.0, sel[g]) for g in groups]
        v = [jnp.where(pick[g], BELOW_ALL, v[g]) for g in groups]
    sel_b = [sel[g].astype(BF16) for g in groups]

    cmp_s = [normalised_pairs(acc_c[g], True) for g in groups]
    n_before = lax.shift_right_logical(q0, int(np.log2(kc_len)))
    causal = dbias_ref[qi & (kc_len // tq - 1)]
    w0 = pl.multiple_of(jnp.maximum(q0 - WINDOW, 0), tq)
    n_w = WINDOW + tq
    w_case = jnp.minimum(qi, WINDOW // tq)

    def sel_keys(ref, g, kc):
        return ref[0, g, pl.ds(pl.multiple_of(kc * kc_len, kc_len), kc_len), :]

    def sel_scores(g, kc, causal_bias, shift):
        chosen = _dot_tn(sel_b[g], esel_ref[kc])
        bias = (chosen - 1.0) * (-NEG_INF)
        if causal_bias is not None:
            bias = bias + causal_bias
        if shift is not None:
            bias = bias - shift
        return _dot_nt(qs[g], sel_keys(ks_ref, g, kc)) + heads4(bias)

    def win_scores(g, shift):
        bias = wbias_ref[w_case] if shift is None else wbias_ref[w_case] - shift
        return _dot_nt(qs[g], kw_ref[0, g, pl.ds(w0, n_w), :]) + heads4(bias)

    def win_values(g):
        return vw_ref[0, g, pl.ds(w0, n_w), :]

    def finish(acc_w):
        for g in groups:
            sel_s = normalised_pairs(acc_scr[g], False)
            win_s = normalised_pairs(acc_w[g], False)
            for p in range(2):
                cols = slice((2 * g + p) * LANES, (2 * g + p + 1) * LANES)
                mix = gates[0][:, cols] * cmp_s[g][p] + gates[1][:, cols] * sel_s[p] + gates[2][:, cols] * win_s[p]
                ms_o = _seg_mean(mix * mix, avgo_ref[...])
                o_ref[0, :, cols] = (mix * lax.rsqrt(ms_o + EPS) * ogain_ref[:, cols]).astype(BF16)

    def fixed_shift_path(shift):
        def probs(s):
            return jnp.exp(s).astype(BF16)

        for g in groups:
            acc_scr[g] = jnp.zeros(acc_scr.shape[1:], F32)

        def before(kc, carry):
            s = [sel_scores(g, kc, None, shift) for g in groups]
            p = [probs(s[g]) for g in groups]
            for g in groups:
                acc_scr[g] = acc_scr[g] + _dot(p[g], sel_keys(vs_ref, g, kc))
            return carry

        lax.fori_loop(0, n_before, before, 0)
        s_d0 = sel_scores(0, n_before, causal, shift)
        s_d1 = sel_scores(1, n_before, causal, shift)
        p_d0 = probs(s_d0)
        s_w0 = win_scores(0, shift)
        acc_scr[0] = acc_scr[0] + _dot(p_d0, sel_keys(vs_ref, 0, n_before))
        p_d1 = probs(s_d1)
        s_w1 = win_scores(1, shift)
        acc_scr[1] = acc_scr[1] + _dot(p_d1, sel_keys(vs_ref, 1, n_before))
        p_w0 = probs(s_w0)
        acc_w0 = _dot(p_w0, win_values(0))
        p_w1 = probs(s_w1)
        acc_w1 = _dot(p_w1, win_values(1))
        finish([acc_w0, acc_w1])

    def online_path():
        for g in groups:
            m_scr[g] = jnp.full(m_scr.shape[1:], NEG_INF, F32)
            acc_scr[g] = jnp.zeros(acc_scr.shape[1:], F32)

        def sel_softmax(g, s):
            m_old = m_scr[g]
            m_new = jnp.maximum(m_old, jnp.max(s, axis=-1, keepdims=True))
            m_scr[g] = m_new
            return jnp.exp(s - m_new).astype(BF16), jnp.exp(m_old - m_new)

        def sel_accumulate(g, kc, p, alpha):
            acc_scr[g] = alpha * acc_scr[g] + _dot(p, sel_keys(vs_ref, g, kc))

        def win_softmax(s):
            return jnp.exp(s - jnp.max(s, axis=-1, keepdims=True)).astype(BF16)

        def before(kc, carry):
            s = [sel_scores(g, kc, None, None) for g in groups]
            pa = [sel_softmax(g, s[g]) for g in groups]
            for g in groups:
                sel_accumulate(g, kc, *pa[g])
            return carry

        lax.fori_loop(0, n_before, before, 0)
        s_d0 = sel_scores(0, n_before, causal, None)
        s_d1 = sel_scores(1, n_before, causal, None)
        pa0 = sel_softmax(0, s_d0)
        s_w0 = win_scores(0, None)
        sel_accumulate(0, n_before, *pa0)
        pa1 = sel_softmax(1, s_d1)
        s_w1 = win_scores(1, None)
        sel_accumulate(1, n_before, *pa1)
        acc_w0 = _dot(win_softmax(s_w0), win_values(0))
        acc_w1 = _dot(win_softmax(s_w1), win_values(1))
        finish([acc_w0, acc_w1])

    bound = 1.01 * (HEAD_DIM ** 0.5) * jnp.max(jnp.abs(qgain_ref[...])) * jnp.max(jnp.abs(kgain_ref[...]))
    safe = bound <= MAX_FIXED_SHIFT
    pl.when(safe)(lambda: fixed_shift_path(bound))
    pl.when(jnp.logical_not(safe))(online_path)


def _sel_from_cmp(n_cmp, n_sel):
    c0 = np.arange(n_cmp) * CMP_STRIDE
    s0 = np.arange(n_sel) * SEL_BLOCK
    ov = np.minimum(c0[None, :] + CMP_BLOCK, s0[:, None] + SEL_BLOCK) - np.maximum(c0[None, :], s0[:, None])
    m = (np.clip(ov, 0, None) / CMP_BLOCK).astype(np.float32)
    m[:, (np.arange(n_cmp) * CMP_STRIDE + CMP_BLOCK) > n_sel * SEL_BLOCK] = 0.0
    return m


def _nsa_attn(proj3, kcmp, vcmp, ks, vs, kw, vw, q_gain, k_gains, o_gain):
    B, S, _ = proj3.shape
    n_cmp = kcmp.shape[2]
    n_sel = S // SEL_BLOCK
    tq = NSA_TQ
    assert n_sel % 8 == 0 and S % SEL_KC == 0 and SEL_KC % tq == 0 and WINDOW % tq == 0 and S >= WINDOW + tq
    avgq = jnp.asarray(_block_avg(NSA_WIDTH), BF16)
    avgo = jnp.asarray(_block_avg(LANES), BF16)
    msct = jnp.asarray(_sel_from_cmp(n_cmp, n_sel), BF16)
    esel = (np.arange(n_sel)[:, None] == np.arange(S)[None, :] // SEL_BLOCK).astype(np.float32)
    esel = jnp.asarray(esel.reshape(n_sel, S // SEL_KC, SEL_KC).transpose(1, 0, 2), BF16)
    src = np.arange(LANES)[:, None]
    dst = np.arange(NSA_WIDTH)[None, :]
    egate = jnp.asarray(np.stack([(src == (dst // HEAD_DIM) * 3 + j) for j in range(3)]).astype(np.float32), BF16)
    r = np.arange(tq)[:, None]
    n_w = WINDOW + tq
    wcases = []
    for i in range(WINDOW // tq + 1):
        diff = (i * tq - max(i * tq - WINDOW, 0)) + r - np.arange(n_w)[None, :]
        wcases.append(np.where((diff >= 0) & (diff < WINDOW), 0.0, NEG_INF))
    wbias = jnp.asarray(np.stack(wcases), F32)
    dbias = jnp.asarray(np.stack([np.where(np.arange(SEL_KC)[None, :] <= i * tq + r, 0.0, NEG_INF)
                                  for i in range(SEL_KC // tq)]), F32)

    full = lambda a: pl.BlockSpec(a.shape, lambda b, i: (0,) * a.ndim)
    per_b = lambda a: pl.BlockSpec((1,) + a.shape[1:], lambda b, i: (b,) + (0,) * (a.ndim - 1))
    return pl.pallas_call(
        _nsa_attn_body,
        grid=(B, S // tq),
        in_specs=[pl.BlockSpec((1, tq, NSA_WIDTH), lambda b, i: (b, i, COL_QA // NSA_WIDTH)),
                  pl.BlockSpec((1, tq, LANES), lambda b, i: (b, i, COL_GATE // LANES)),
                  per_b(kcmp), per_b(vcmp), per_b(ks), per_b(vs), per_b(kw), per_b(vw),
                  full(q_gain), full(k_gains), full(o_gain), full(avgq), full(avgo), full(msct), full(esel), full(egate),
                  full(wbias), full(dbias)],
        out_specs=pl.BlockSpec((1, tq, NSA_WIDTH), lambda b, i: (b, i, 0)),
        out_shape=jax.ShapeDtypeStruct((B, S, NSA_WIDTH), BF16),
        scratch_shapes=[pltpu.VMEM((NSA_GROUPS, 4 * tq, 1), F32), pltpu.VMEM((NSA_GROUPS, 4 * tq, LANES), F32)],
        compiler_params=_cparams(2),
        name="nsa_attn",
    )(proj3, proj3, kcmp, vcmp, ks, vs, kw, vw, q_gain, k_gains, o_gain, avgq, avgo, msct, esel, egate, wbias, dbias)


def _retention_body(q_ref, k_ref, v_ref, g_ref, cos_ref, sin_ref, decay_ref, xi_ref, zeta_ref, gammac_ref,
                    gain_ref, avg_ref, o_ref, state_scr):
    S = q_ref.shape[1]
    C = RET_CHUNK
    lane = lax.broadcasted_iota(jnp.int32, (C, LANES), 1)
    lo = lane < HEAD_DIM
    first_half = (lane & (HEAD_DIM - 1)) < HEAD_DIM // 2
    r = lax.broadcasted_iota(jnp.int32, (LANES, LANES), 0)
    c = lax.broadcasted_iota(jnp.int32, (LANES, LANES), 1)
    same_head = (r < HEAD_DIM) == (c < HEAD_DIM)
    avg = avg_ref[...]
    state_scr[...] = jnp.zeros(state_scr.shape, F32)

    def rope(x, cos, sin):
        swapped = jnp.where(first_half, pltpu.roll(x, LANES - HEAD_DIM // 2, axis=1),
                            pltpu.roll(x, HEAD_DIM // 2, axis=1))
        return x * cos + swapped * sin

    def chunk(n, carry):
        r0 = pl.multiple_of(n * C, C)
        cos = cos_ref[pl.ds(r0, C), :]
        sin = sin_ref[pl.ds(r0, C), :]
        pairs = range(RET_HEADS // 2)
        cols = [slice(p * LANES, (p + 1) * LANES) for p in pairs]
        q = [rope(q_ref[0, pl.ds(r0, C), cols[p]].astype(F32), cos, sin) for p in pairs]
        k = [rope(k_ref[0, pl.ds(r0, C), cols[p]].astype(F32), cos, sin) * (HEAD_DIM ** -0.5) for p in pairs]
        vb = [v_ref[0, pl.ds(r0, C), cols[p]] for p in pairs]
        kb = [k[p].astype(BF16) for p in pairs]
        inner = [_dot_nt(jnp.where(lo if half == 0 else ~lo, q[p], 0.0).astype(BF16), kb[p])
                 * decay_ref[2 * p + half] for p in pairs for half in range(2)]
        state = [state_scr[p] for p in pairs]
        cross = [_dot(q[p].astype(BF16), state[p].astype(BF16)) * xi_ref[p] for p in pairs]
        upd = [_dot_tn((k[p] * zeta_ref[p]).astype(BF16), vb[p]) for p in pairs]
        outs = [_dot(inner[i].astype(BF16), vb[i // 2]) for i in range(RET_HEADS)]
        for p in pairs:
            state_scr[p] = gammac_ref[p] * state[p] + jnp.where(same_head, upd[p], 0.0)
        y = jnp.concatenate([jnp.where(lo, outs[2 * p], outs[2 * p + 1]) + cross[p] for p in pairs], axis=0)
        mu = _seg_mean(y, avg)
        d = y - mu
        var = _seg_mean(d * d, avg)
        yn = d * lax.rsqrt(var + EPS)
        for p in pairs:
            gate = g_ref[0, pl.ds(r0, C), cols[p]].astype(F32)
            o_ref[0, pl.ds(r0, C), cols[p]] = (_silu(gate) * (yn[p * C:(p + 1) * C] * gain_ref[:, cols[p]])).astype(BF16)
        return carry

    lax.fori_loop(0, S // C, chunk, 0)


def _retention_tables(S):
    half = HEAD_DIM // 2
    inv_freq = ROPE_BASE ** (-jnp.arange(half, dtype=F32) / half)
    ang = jnp.arange(S, dtype=F32)[:, None] * inv_freq[None, :]
    cos, sin = jnp.cos(ang), jnp.sin(ang)
    cos_t = jnp.tile(cos, (1, 4))
    sin_t = jnp.tile(jnp.concatenate([-sin, sin], axis=1), (1, 2))
    C = RET_CHUNK
    H = RET_HEADS
    log_gamma = jnp.log1p(-jnp.power(2.0, -5.0 - jnp.arange(H, dtype=F32)))
    i = jnp.arange(C, dtype=F32)
    rel = i[:, None] - i[None, :]
    decay = jnp.where(rel >= 0, jnp.exp(jnp.maximum(rel, 0.0)[None] * log_gamma[:, None, None]), 0.0)
    xi = jnp.exp((i + 1.0)[:, None] * log_gamma[None, :])
    zeta = jnp.exp((C - 1.0 - i)[:, None] * log_gamma[None, :])
    gamma_c = jnp.exp(C * log_gamma)
    per_pair = lambda t: jnp.repeat(t.T.reshape(H // 2, 2, -1), HEAD_DIM, axis=1).transpose(0, 2, 1)
    gammac = jnp.repeat(gamma_c.reshape(H // 2, 2), HEAD_DIM, axis=1)[:, None, :]
    return cos_t, sin_t, decay, per_pair(xi), per_pair(zeta), gammac


def _retention(proj3, gain):
    B, S, _ = proj3.shape
    cos_t, sin_t, decay, xi, zeta, gammac = _retention_tables(S)
    avg = jnp.asarray(_block_avg(LANES), BF16)
    col = lambda c: pl.BlockSpec((1, S, RET_WIDTH), lambda b: (b, 0, c // RET_WIDTH))
    full = lambda a: pl.BlockSpec(a.shape, lambda b: (0,) * a.ndim)
    return pl.pallas_call(
        _retention_body,
        grid=(B,),
        in_specs=[col(COL_QR), col(COL_KR), col(COL_VR), col(COL_GR), full(cos_t), full(sin_t), full(decay),
                  full(xi), full(zeta), full(gammac), full(gain), full(avg)],
        out_specs=pl.BlockSpec((1, S, RET_WIDTH), lambda b: (b, 0, 0)),
        out_shape=jax.ShapeDtypeStruct((B, S, RET_WIDTH), BF16),
        scratch_shapes=[pltpu.VMEM((RET_HEADS // 2, LANES, LANES), F32)],
        compiler_params=_cparams(1),
        name="retention",
    )(proj3, proj3, proj3, proj3, cos_t, sin_t, decay, xi, zeta, gammac, gain, avg)


def _mem_prep_body(mem_ref, g_ref, wkv_ref, kgain_ref, avg_ref, k_ref, v_ref):
    hm = _rms_full(mem_ref[0], g_ref[...]).astype(BF16)
    kv = _dot(hm, wkv_ref[...])
    k = kv[:, :MEM_WIDTH]
    ms = _seg_mean(k * k, avg_ref[...])
    k_ref[0] = (k * lax.rsqrt(ms + EPS) * kgain_ref[...]).astype(BF16)
    v_ref[0] = kv[:, MEM_WIDTH:].astype(BF16)


def _mem_prep(mem, g, wkv, kgain):
    B, M, _ = mem.shape
    avg = jnp.asarray(_block_avg(MEM_WIDTH), BF16)
    full = lambda a: pl.BlockSpec(a.shape, lambda b: (0,) * a.ndim)
    out_spec = pl.BlockSpec((1, M, MEM_WIDTH), lambda b: (b, 0, 0))
    out_shape = jax.ShapeDtypeStruct((B, M, MEM_WIDTH), BF16)
    return pl.pallas_call(
        _mem_prep_body,
        grid=(B,),
        in_specs=[pl.BlockSpec((1, M, D_MODEL), lambda b: (b, 0, 0)), full(g), full(wkv), full(kgain), full(avg)],
        out_specs=[out_spec, out_spec],
        out_shape=[out_shape, out_shape],
        compiler_params=_cparams(1),
        name="mem_prep",
    )(mem, g, wkv, kgain, avg)


def _post_body(x_ref, oa_ref, ob_ref, wout_ref, mk_ref, mv_ref, gx_ref, wq_ref, qgain_ref, avg_ref, wo_ref,
               gf_ref, wr_ref, br_ref, x2_ref, h_ref, route_ref):
    tm = x_ref.shape[1] // POST_CHAINS
    chains = range(POST_CHAINS)
    rows = [slice(c * tm, (c + 1) * tm) for c in chains]
    x1 = [x_ref[0, rows[c]] + _dot(oa_ref[0, rows[c]], wout_ref[0:NSA_WIDTH, :])
          + _dot(ob_ref[0, rows[c]], wout_ref[NSA_WIDTH:, :]) for c in chains]

    h = [_rms_full(x1[c], gx_ref[...]).astype(BF16) for c in chains]
    q = [_dot(h[c], wq_ref[...]) for c in chains]
    ms = [_seg_mean(q[c] * q[c], avg_ref[...]) for c in chains]
    q = [q[c] * lax.rsqrt(ms[c] + EPS) * qgain_ref[...] * (HEAD_DIM ** -0.5) for c in chains]
    lane = lax.broadcasted_iota(jnp.int32, (tm, LANES), 1)
    lo = lane < HEAD_DIM
    heads = [(c, p, half) for c in chains for p in range(MEM_HEADS // 2) for half in range(2)]
    s = [_dot_nt(jnp.where(lo if half == 0 else ~lo, q[c][:, p * LANES:(p + 1) * LANES], 0.0).astype(BF16),
                 mk_ref[0, :, p * LANES:(p + 1) * LANES]) for c, p, half in heads]
    e = [jnp.exp(s[i] - jnp.max(s[i], axis=-1, keepdims=True)) for i in range(len(heads))]
    pr = [(e[i] / jnp.sum(e[i], axis=-1, keepdims=True)).astype(BF16) for i in range(len(heads))]
    outs = [_dot(pr[i], mv_ref[0, :, heads[i][1] * LANES:(heads[i][1] + 1) * LANES]) for i in range(len(heads))]
    per_chain = MEM_HEADS
    o = [jnp.concatenate([jnp.where(lo, outs[c * per_chain + 2 * p], outs[c * per_chain + 2 * p + 1])
                          for p in range(MEM_HEADS // 2)], axis=1).astype(BF16) for c in chains]
    x2 = [x1[c] + _dot(o[c], wo_ref[...]) for c in chains]
    for c in chains:
        x2_ref[0, rows[c]] = x2[c]

    hf = [_rms_full(x2[c], gf_ref[...]).astype(BF16) for c in chains]
    for c in chains:
        h_ref[0, rows[c]] = hf[c]
    logits = [_dot(hf[c], wr_ref[...]) + br_ref[...] for c in chains]
    lane_f = lane.astype(F32)
    big = float(LANES)
    for c in chains:
        gl = jnp.where(lane < N_GROUPS, logits[c], BELOW_ALL)
        gmax = jnp.max(gl, axis=-1, keepdims=True)
        grp = jnp.min(jnp.where(gl == gmax, lane_f, big), axis=-1, keepdims=True)
        g_w = 1.0 / jnp.sum(jnp.where(lane < N_GROUPS, jnp.exp(gl - gmax), 0.0), axis=-1, keepdims=True)
        e_lo = N_GROUPS + grp * EXPERTS_PER_GROUP
        el = jnp.where((lane_f >= e_lo) & (lane_f < e_lo + EXPERTS_PER_GROUP), logits[c], BELOW_ALL)
        v0 = jnp.max(el, axis=-1, keepdims=True)
        i0 = jnp.min(jnp.where(el == v0, lane_f, big), axis=-1, keepdims=True)
        el = jnp.where(lane_f == i0, BELOW_ALL, el)
        v1 = jnp.max(el, axis=-1, keepdims=True)
        i1 = jnp.min(jnp.where(el == v1, lane_f, big), axis=-1, keepdims=True)
        e1 = jnp.exp(v1 - v0)
        w0 = g_w / (1.0 + e1)
        w1 = g_w * e1 / (1.0 + e1)
        route_ref[0, rows[c]] = jnp.where(lane == 0, i0 - N_GROUPS,
                                          jnp.where(lane == 1, i1 - N_GROUPS,
                                                    jnp.where(lane == 2, w0, jnp.where(lane == 3, w1, 0.0))))


def _post(x, oa, ob, wout, mk, mv, gx, wq, qgain, wo, gf, wr, br):
    B, S, _ = x.shape
    tm = POST_TM
    avg = jnp.asarray(_block_avg(MEM_WIDTH), BF16)
    full = lambda a: pl.BlockSpec(a.shape, lambda b, i: (0,) * a.ndim)
    per_b = lambda a: pl.BlockSpec((1,) + a.shape[1:], lambda b, i: (b,) + (0,) * (a.ndim - 1))
    tile = lambda w: pl.BlockSpec((1, tm, w), lambda b, i: (b, i, 0))
    return pl.pallas_call(
        _post_body,
        grid=(B, S // tm),
        in_specs=[tile(D_MODEL), tile(NSA_WIDTH), tile(RET_WIDTH), full(wout), per_b(mk), per_b(mv), full(gx),
                  full(wq), full(qgain), full(avg), full(wo), full(gf), full(wr), full(br)],
        out_specs=[tile(D_MODEL), tile(D_MODEL), tile(LANES)],
        out_shape=[jax.ShapeDtypeStruct((B, S, D_MODEL), F32), jax.ShapeDtypeStruct((B, S, D_MODEL), BF16),
                   jax.ShapeDtypeStruct((B, S, LANES), F32)],
        compiler_params=_cparams(2),
        name="post_mixer",
    )(x, oa, ob, wout, mk, mv, gx, wq, qgain, avg, wo, gf, wr, br)


def _rank_body(route_ref, tri_ref, rank_ref, count_ref):
    route = route_ref[...]
    tm = route.shape[0]
    lane = lax.broadcasted_iota(jnp.int32, (tm, LANES), 1).astype(F32)
    oh0 = lane == route[:, 0:1]
    oh1 = lane == route[:, 1:2]
    both = jnp.where(oh0, 1.0, 0.0) + jnp.where(oh1, 1.0, 0.0)
    before = _dot(tri_ref[...], both.astype(BF16))
    r0 = jnp.sum(jnp.where(oh0, before, 0.0), axis=-1, keepdims=True)
    r1 = jnp.sum(jnp.where(oh1, before, 0.0), axis=-1, keepdims=True)
    lane_i = lax.broadcasted_iota(jnp.int32, (tm, LANES), 1)
    rank_ref[...] = jnp.where(lane_i == 0, r0, jnp.where(lane_i == 1, r1, 0.0))
    count_ref[0] = jnp.broadcast_to(jnp.sum(both, axis=0, keepdims=True), count_ref.shape[1:])


def _rank(route2d):
    T = route2d.shape[0]
    tm = MOE_TM
    tri = jnp.asarray(np.tril(np.ones((tm, tm), np.float32), -1), BF16)
    return pl.pallas_call(
        _rank_body,
        grid=(T // tm,),
        in_specs=[pl.BlockSpec((tm, LANES), lambda i: (i, 0)), pl.BlockSpec((tm, tm), lambda i: (0, 0))],
        out_specs=[pl.BlockSpec((tm, LANES), lambda i: (i, 0)), pl.BlockSpec((1, 8, LANES), lambda i: (i, 0, 0))],
        out_shape=[jax.ShapeDtypeStruct((T, LANES), F32), jax.ShapeDtypeStruct((T // tm, 8, LANES), F32)],
        compiler_params=_cparams(1),
        name="moe_rank",
    )(route2d, tri)


def _row_copy(src, dst, sem):
    return pltpu.make_async_copy(src, dst, sem)


def _run_pieces(n, max_piece, fn):
    b = RUN_ALIGN
    while b <= max_piece:
        pl.when((n & b) != 0)(functools.partial(fn, n & (-2 * b), b))
        b *= 2


def _move_groups(i, gmap_ref, n_loc, copy):
    n_groups = n_loc // RUN_ALIGN
    for j in range(n_groups):
        glob = pl.multiple_of(gmap_ref[i * n_groups + j], RUN_ALIGN)
        copy(pl.ds(j * RUN_ALIGN, RUN_ALIGN), pl.ds(glob, RUN_ALIGN)).start()


def _local_positions(route, rank, loff_row):
    lane = lax.broadcasted_iota(jnp.int32, route.shape, 1).astype(F32)
    pos = []
    for s in range(2):
        base = jnp.sum(jnp.where(lane == route[:, s:s + 1], loff_row, 0.0), axis=-1, keepdims=True)
        pos.append(base + rank[:, s:s + 1])
    return pos


def _scatter_body(gmap_ref, tstart_ref, tlen_ref, nact_ref,
                  h_ref, route_ref, rank_ref, lofff_ref, xs_ref, xloc, zbuf, sems):
    i = pl.program_id(0)
    tm = h_ref.shape[0]
    n_loc = xloc.shape[1]
    slot = i & 1
    sem = sems.at[0]

    def tails(start):
        def per_expert(e, carry):
            n = tlen_ref[e]
            st = tstart_ref[e]

            def piece(off, size):
                c = _row_copy(zbuf.at[pl.ds(0, size)], xs_ref.at[pl.ds(pl.multiple_of(st + off, RUN_ALIGN), size)], sem)
                c.start() if start else c.wait()

            _run_pieces(n, MOE_RB // 2, piece)
            return carry

        lax.fori_loop(0, N_EXPERTS, per_expert, 0)

    def unused(start):
        rows = zbuf.shape[0]

        def per_unit(u, carry):
            c = _row_copy(zbuf, xs_ref.at[pl.ds(pl.multiple_of(u * rows, rows), rows)], sem)
            c.start() if start else c.wait()
            return carry

        lax.fori_loop(nact_ref[0] * (MOE_RB // rows), xs_ref.shape[0] // rows, per_unit, 0)

    @pl.when(i == 0)
    def _():
        zbuf[...] = jnp.zeros(zbuf.shape, zbuf.dtype)
        tails(True)
        unused(True)
        tails(False)
        unused(False)

    pos = _local_positions(route_ref[...], rank_ref[...], lofff_ref[0, 0:1, :])
    col = lax.broadcasted_iota(jnp.int32, (tm, n_loc), 1).astype(F32)
    perm_t = jnp.where((col == pos[0]) | (col == pos[1]), 1.0, 0.0).astype(BF16)
    xloc[slot] = _dot_tn(perm_t, h_ref[...]).astype(BF16)

    def wait_slot(s):
        _row_copy(xloc.at[s], xs_ref.at[pl.ds(0, n_loc)], sems.at[s]).wait()

    pl.when(i > 0)(lambda: wait_slot(1 - slot))
    _move_groups(i, gmap_ref, n_loc, lambda loc, glob: _row_copy(xloc.at[slot, loc], xs_ref.at[glob], sems.at[slot]))
    pl.when(i == pl.num_programs(0) - 1)(lambda: wait_slot(slot))


def _scatter_rows(tables, h2d, route2d, rank, loff_f, n_rows):
    T = h2d.shape[0]
    tm = MOE_TM
    n_loc = MOE_NLOC
    tile = lambda w: pl.BlockSpec((tm, w), lambda i, *_: (i, 0))
    grid_spec = pltpu.PrefetchScalarGridSpec(
        num_scalar_prefetch=4,
        grid=(T // tm,),
        in_specs=[tile(D_MODEL), tile(LANES), tile(LANES), pl.BlockSpec((1, 8, LANES), lambda i, *_: (i, 0, 0))],
        out_specs=pl.BlockSpec(memory_space=pl.ANY),
        scratch_shapes=[pltpu.VMEM((2, n_loc, D_MODEL), BF16), pltpu.VMEM((MOE_RB // 2, D_MODEL), BF16),
                        pltpu.SemaphoreType.DMA((2,))],
    )
    return pl.pallas_call(
        _scatter_body,
        grid_spec=grid_spec,
        out_shape=jax.ShapeDtypeStruct((n_rows, D_MODEL), BF16),
        compiler_params=_cparams(1),
        name="moe_scatter",
    )(*tables, h2d, route2d, rank, loff_f)


def _expert_body(blk_e_ref, n_act_ref, xs_ref, wg_ref, wu_ref, wd_ref, ys_ref):
    i = pl.program_id(0)

    @pl.when(i < n_act_ref[0])
    def _():
        x = xs_ref[...]
        a = _dot(x, wg_ref[0].astype(BF16))
        b = _dot(x, wu_ref[0].astype(BF16))
        ys_ref[...] = _dot((_silu(a) * b).astype(BF16), wd_ref[0].astype(BF16)).astype(BF16)

    @pl.when(i >= n_act_ref[0])
    def _():
        ys_ref[...] = jnp.zeros(ys_ref.shape, BF16)


def _experts(blk_e, n_act, xs, n_rows, wg, wu, wd):
    rb = MOE_RB
    grid_spec = pltpu.PrefetchScalarGridSpec(
        num_scalar_prefetch=2,
        grid=(n_rows // rb,),
        in_specs=[pl.BlockSpec((rb, D_MODEL), lambda i, be, na: (jnp.minimum(i, na[0] - 1), 0)),
                  pl.BlockSpec((1, D_MODEL, EXPERT_FF), lambda i, be, na: (be[i], 0, 0)),
                  pl.BlockSpec((1, D_MODEL, EXPERT_FF), lambda i, be, na: (be[i], 0, 0)),
                  pl.BlockSpec((1, EXPERT_FF, D_MODEL), lambda i, be, na: (be[i], 0, 0))],
        out_specs=pl.BlockSpec((rb, D_MODEL), lambda i, be, na: (i, 0)),
    )
    return pl.pallas_call(
        _expert_body,
        grid_spec=grid_spec,
        out_shape=jax.ShapeDtypeStruct((n_rows, D_MODEL), BF16),
        compiler_params=_cparams(1),
        name="moe_experts",
    )(blk_e, n_act, xs, wg, wu, wd)


def _combine_body(gmap_ref, x_ref, route_ref, rank_ref, lofff_ref, ys_ref, o_ref, yloc, sems):
    i = pl.program_id(0)
    tm = x_ref.shape[0]
    n_loc = yloc.shape[1]
    slot = i & 1

    def fetch(tile, s):
        _move_groups(tile, gmap_ref, n_loc, lambda loc, glob: _row_copy(ys_ref.at[glob], yloc.at[s, loc], sems.at[s]))

    pl.when(i == 0)(lambda: fetch(i, slot))
    _row_copy(ys_ref.at[pl.ds(0, n_loc)], yloc.at[slot], sems.at[slot]).wait()
    pl.when(i + 1 < pl.num_programs(0))(lambda: fetch(i + 1, 1 - slot))

    route = route_ref[...]
    pos = _local_positions(route, rank_ref[...], lofff_ref[0, 0:1, :])
    col = lax.broadcasted_iota(jnp.int32, (tm, n_loc), 1).astype(F32)
    perm_w = (jnp.where(col == pos[0], route[:, 2:3], 0.0) + jnp.where(col == pos[1], route[:, 3:4], 0.0)).astype(BF16)
    o_ref[...] = x_ref[...] + _dot(perm_w, yloc[slot])


def _combine(tables, x2d, route2d, rank, loff_f, ys):
    T = x2d.shape[0]
    tm = MOE_TM
    n_loc = MOE_NLOC
    tile = lambda w: pl.BlockSpec((tm, w), lambda i, *_: (i, 0))
    grid_spec = pltpu.PrefetchScalarGridSpec(
        num_scalar_prefetch=1,
        grid=(T // tm,),
        in_specs=[tile(D_MODEL), tile(LANES), tile(LANES), pl.BlockSpec((1, 8, LANES), lambda i, *_: (i, 0, 0)),
                  pl.BlockSpec(memory_space=pl.ANY)],
        out_specs=tile(D_MODEL),
        scratch_shapes=[pltpu.VMEM((2, n_loc, D_MODEL), BF16), pltpu.SemaphoreType.DMA((2,))],
    )
    return pl.pallas_call(
        _combine_body,
        grid_spec=grid_spec,
        out_shape=jax.ShapeDtypeStruct((T, D_MODEL), F32),
        compiler_params=_cparams(1),
        name="moe_combine",
    )(*tables, x2d, route2d, rank, loff_f, ys)


def _permute_w_in(w):
    kv0 = NSA_WIDTH
    gate0 = kv0 + 6 * NSA_GROUPS * HEAD_DIM
    ret0 = gate0 + 3 * NSA_HEADS
    pad = jnp.zeros((w.shape[0], PROJ_PAD - w.shape[1]), w.dtype)
    return jnp.concatenate([w[:, :kv0], w[:, ret0:], w[:, kv0:gate0], w[:, gate0:ret0], pad], axis=1).astype(BF16)


def _compress_weights(pos, w1, w2):
    eye = jnp.eye(NSA_GROUPS, dtype=F32)
    w1r = w1.reshape(CMP_BLOCK, HEAD_DIM, CMP_HIDDEN)
    w1b = jnp.einsum('ldh,gk->lgdkh', w1r, eye).reshape(2, CMP_STRIDE * LANES, NSA_GROUPS * CMP_HIDDEN)
    w2b = jnp.einsum('hd,gk->ghkd', w2, eye).reshape(NSA_GROUPS * CMP_HIDDEN, LANES)
    posb = jnp.tile(pos, (1, NSA_GROUPS)).reshape(2, 1, CMP_STRIDE * LANES)
    return posb, w1b.astype(BF16), w2b.astype(BF16)


def _dup2(g):
    return jnp.tile(g.reshape(1, HEAD_DIM), (1, 2))


def kernel(x, mem, mix_norm, w_in, nsa_q_norm, nsa_kcmp_norm, nsa_ksel_norm, nsa_kwin_norm, cmp_pos_k, cmp_pos_v, cmp_k_w1, cmp_k_w2, cmp_v_w1, cmp_v_w2, nsa_out_norm, ret_out_norm, w_out, mem_x_norm, mem_kv_norm, mem_wq, mem_wkv, mem_q_norm, mem_k_norm, mem_wo, ffn_norm, router_group_w, router_group_b, router_expert_w, router_expert_b, exp_w_gate, exp_w_up, exp_w_down):
    B, S, D = x.shape
    T = B * S
    depth = mix_norm.shape[0]
    for l in range(depth):
        proj = _proj(x.reshape(T, D), mix_norm[l].reshape(1, D), _permute_w_in(w_in[l])).reshape(B, S, PROJ_PAD)
        pk, w1k, w2k = _compress_weights(cmp_pos_k[l], cmp_k_w1[l], cmp_k_w2[l])
        pv, w1v, w2v = _compress_weights(cmp_pos_v[l], cmp_v_w1[l], cmp_v_w2[l])
        gains = jnp.stack([_dup2(nsa_kcmp_norm[l]), _dup2(nsa_ksel_norm[l]), _dup2(nsa_kwin_norm[l])])
        kcmp, vcmp, ks, vs, kw, vw = _nsa_prep(proj, jnp.stack([pk, pv]), jnp.stack([w1k, w1v]),
                                               jnp.stack([w2k, w2v]), gains)
        o_a = _nsa_attn(proj, kcmp, vcmp, ks, vs, kw, vw,
                        jnp.tile(nsa_q_norm[l].reshape(1, HEAD_DIM), (1, NSA_HEADS)), gains[1:],
                        nsa_out_norm[l].reshape(1, NSA_WIDTH))
        o_b = _retention(proj, ret_out_norm[l].reshape(1, RET_WIDTH))
        mk, mv = _mem_prep(mem, mem_kv_norm[l].reshape(1, D), mem_wkv[l].astype(BF16),
                           jnp.tile(mem_k_norm[l].reshape(1, HEAD_DIM), (1, MEM_HEADS)))
        w_r = jnp.concatenate([router_group_w[l],
                               router_expert_w[l].transpose(1, 0, 2).reshape(D, N_EXPERTS),
                               jnp.zeros((D, LANES - N_GROUPS - N_EXPERTS), F32)], axis=1).astype(BF16)
        b_r = jnp.concatenate([router_group_b[l], router_expert_b[l].reshape(N_EXPERTS),
                               jnp.zeros((LANES - N_GROUPS - N_EXPERTS,), F32)]).reshape(1, LANES)
        x2, hf, route = _post(x, o_a, o_b, w_out[l].astype(BF16), mk, mv, mem_x_norm[l].reshape(1, D),
                              mem_wq[l].astype(BF16), jnp.tile(mem_q_norm[l].reshape(1, HEAD_DIM), (1, MEM_HEADS)),
                              mem_wo[l].astype(BF16), ffn_norm[l].reshape(1, D), w_r, b_r)
        route2d = route.reshape(T, LANES)
        rank, counts = _rank(route2d)
        n_tiles = T // MOE_TM
        cnt = counts[:, 0, :N_EXPERTS].astype(jnp.int32)
        cnt = (cnt + RUN_ALIGN - 1) // RUN_ALIGN * RUN_ALIGN
        loff = jnp.cumsum(cnt, axis=1) - cnt
        total = jnp.sum(cnt, axis=0)
        padded = (total + MOE_RB - 1) // MOE_RB * MOE_RB
        pend = jnp.cumsum(padded)
        pstart = pend - padded
        goff = pstart[None, :] + jnp.cumsum(cnt, axis=0) - cnt
        n_rows = 2 * T + n_tiles * N_EXPERTS * RUN_ALIGN + N_EXPERTS * MOE_RB
        n_blocks = n_rows // MOE_RB
        blk_row0 = jnp.arange(n_blocks, dtype=jnp.int32) * MOE_RB
        blk_e = jnp.minimum(jnp.sum((pend[None, :] <= blk_row0[:, None]).astype(jnp.int32), axis=1), N_EXPERTS - 1)
        n_act = (pend[-1:] // MOE_RB).astype(jnp.int32)
        loff_f = jnp.broadcast_to(jnp.pad(loff.astype(F32), ((0, 0), (0, LANES - N_EXPERTS)))[:, None, :],
                                  (n_tiles, 8, LANES))
        grp_row = jnp.arange(MOE_NLOC // RUN_ALIGN, dtype=jnp.int32) * RUN_ALIGN
        inside = ((loff[:, None, :] <= grp_row[None, :, None])
                  & (grp_row[None, :, None] < (loff + cnt)[:, None, :])).astype(jnp.int32)
        shift = jnp.sum(inside * (goff - loff)[:, None, :], axis=2)
        used = jnp.sum(inside, axis=2) > 0
        gmap_scatter = jnp.where(used, shift + grp_row[None, :], n_rows + grp_row[None, :]).reshape(-1)
        gmap_gather = jnp.where(used, shift + grp_row[None, :], 0).reshape(-1)
        xs = _scatter_rows((gmap_scatter, pstart + total, padded - total, n_act), hf.reshape(T, D), route2d, rank, loff_f,
                           n_rows + MOE_NLOC)
        ys = _experts(blk_e, n_act, xs, n_rows, exp_w_gate[l], exp_w_up[l], exp_w_down[l])
        x = _combine((gmap_gather,), x2.reshape(T, D), route2d, rank, loff_f, ys).reshape(B, S, D)
    return x
```

```python
import functools

import numpy as np
import jax
import jax.numpy as jnp
from jax import lax
from jax.experimental import pallas as pl
from jax.experimental.pallas import tpu as pltpu

F32 = jnp.float32
BF16 = jnp.bfloat16

D_MODEL = 1024
HEAD_DIM = 64
LANES = 128
NSA_HEADS = 8
NSA_GROUPS = 2
NSA_WIDTH = NSA_HEADS * HEAD_DIM
CMP_BLOCK = 32
CMP_STRIDE = 16
CMP_HIDDEN = 2 * HEAD_DIM
SEL_BLOCK = 64
SEL_TOPK = 8
WINDOW = 512
RET_HEADS = 8
RET_WIDTH = RET_HEADS * HEAD_DIM
RET_CHUNK = 128
ROPE_BASE = 10000.0
MEM_HEADS = 4
MEM_WIDTH = MEM_HEADS * HEAD_DIM
N_GROUPS = 4
EXPERTS_PER_GROUP = 8
N_EXPERTS = N_GROUPS * EXPERTS_PER_GROUP
EXPERT_FF = D_MODEL // 4
EPS = 1e-6
NEG_INF = -1e30
FORCE_SCORE = 1e9
BELOW_ALL = -3e38
MAX_FIXED_SHIFT = 40.0

COL_QA = 0
COL_QR, COL_KR, COL_VR, COL_GR = 512, 1024, 1536, 2048
COL_KVC, COL_KSV, COL_KWV = 2560, 2816, 3072
COL_GATE = 3328
PROJ_PAD = 3456

PROJ_TM = 512
NSA_TQ = 256
SEL_KC = 512
WIN_KEYS = WINDOW + NSA_TQ
POST_CHAINS = 2
MOE_TM = 512
MOE_RB = 256
RUN_ALIGN = 16
MOE_NLOC = 2 * MOE_TM + N_EXPERTS * RUN_ALIGN
VMEM_LIMIT = 56 * 1024 * 1024


def _cparams(n_axes):
    return pltpu.CompilerParams(dimension_semantics=("arbitrary",) * n_axes,
                                vmem_limit_bytes=VMEM_LIMIT)


def _dot(a, b):
    return jnp.dot(a, b, preferred_element_type=F32)


def _dot_nt(a, b):
    return lax.dot_general(a, b, (((1,), (1,)), ((), ())), preferred_element_type=F32)


def _dot_tn(a, b):
    return lax.dot_general(a, b, (((0,), (0,)), ((), ())), preferred_element_type=F32)


def _rms_full(x, g):
    ms = jnp.mean(x * x, axis=-1, keepdims=True)
    return x * lax.rsqrt(ms + EPS) * g


def _seg_mean(x, avg):
    return _dot(x.astype(BF16), avg)


def _silu(x):
    return x * (1.0 / (1.0 + jnp.exp(-x)))


def _sigmoid(x):
    return 1.0 / (1.0 + jnp.exp(-x))


def _block_avg(width):
    i = np.arange(width)
    return ((i[:, None] // HEAD_DIM == i[None, :] // HEAD_DIM) / HEAD_DIM).astype(np.float32)


def _proj_body(x_ref, g_ref, w_ref, o_ref):
    h = _rms_full(x_ref[...], g_ref[...]).astype(BF16)
    step = PROJ_PAD // 3
    for j in range(3):
        o_ref[:, j * step:(j + 1) * step] = _dot(h, w_ref[:, j * step:(j + 1) * step]).astype(BF16)


def _proj(x2d, g, w):
    T = x2d.shape[0]
    return pl.pallas_call(
        _proj_body,
        grid=(T // PROJ_TM,),
        in_specs=[pl.BlockSpec((PROJ_TM, D_MODEL), lambda i: (i, 0)),
                  pl.BlockSpec((1, D_MODEL), lambda i: (0, 0)),
                  pl.BlockSpec((D_MODEL, PROJ_PAD), lambda i: (0, 0))],
        out_specs=pl.BlockSpec((PROJ_TM, PROJ_PAD), lambda i: (i, 0)),
        out_shape=jax.ShapeDtypeStruct((T, PROJ_PAD), BF16),
        compiler_params=_cparams(1),
        name="proj",
    )(x2d, g, w)


def _dup_groups(x):
    lane = lax.broadcasted_iota(jnp.int32, x.shape, 1)
    xs = pltpu.roll(x, HEAD_DIM, axis=1)
    lo = lane < HEAD_DIM
    return jnp.where(lo, x, xs), jnp.where(lo, xs, x)


def _ones_groups(x):
    lane = lax.broadcasted_iota(jnp.int32, x.shape, 1)
    lo = lane < HEAD_DIM
    return jnp.where(lo, x, 1.0), jnp.where(lo, pltpu.roll(x, HEAD_DIM, axis=1), 1.0)


def _nsa_prep_body(kvc_ref, ksv_ref, kwv_ref, pos_ref, w1_ref, w2_ref, gain_ref, avg_ref,
                   kcmp_ref, vcmp_ref, ks_ref, vs_ref, kw_ref, vw_ref, scr_k, scr_v):
    avg = avg_ref[...]
    n_c = scr_k.shape[0] // CMP_STRIDE
    scr_k[...] = kvc_ref[0, :, 0:LANES].astype(F32)
    scr_v[...] = kvc_ref[0, :, LANES:2 * LANES].astype(F32)
    for j, out_ref, scr in ((0, kcmp_ref, scr_k), (1, vcmp_ref, scr_v)):
        ycat = jnp.concatenate(
            [scr[pl.ds(l, n_c, stride=CMP_STRIDE), :] for l in range(CMP_STRIDE)], axis=1)
        first = _dot((ycat + pos_ref[j, 0]).astype(BF16), w1_ref[j, 0])
        second = _dot((ycat + pos_ref[j, 1]).astype(BF16), w1_ref[j, 1])
        hidden = first + pltpu.roll(second, n_c - 1, axis=0)
        cmp_tok = _dot(_silu(hidden).astype(BF16), w2_ref[j])
        if j == 0:
            ms = _seg_mean(cmp_tok * cmp_tok, avg)
            cmp_tok = cmp_tok * lax.rsqrt(ms + EPS) * gain_ref[0]
        d0, d1 = _dup_groups(cmp_tok) if j == 0 else _ones_groups(cmp_tok)
        out_ref[0, 0] = d0.astype(BF16)
        out_ref[0, 1] = d1.astype(BF16)

    for src_ref, k_out, v_out, gi in ((ksv_ref, ks_ref, vs_ref, 1), (kwv_ref, kw_ref, vw_ref, 2)):
        k = src_ref[0, :, 0:LANES].astype(F32)
        ms = _seg_mean(k * k, avg)
        k = k * lax.rsqrt(ms + EPS) * gain_ref[gi]
        d0, d1 = _dup_groups(k)
        k_out[0, 0] = d0.astype(BF16)
        k_out[0, 1] = d1.astype(BF16)
        d0, d1 = _ones_groups(src_ref[0, :, LANES:2 * LANES].astype(F32))
        v_out[0, 0] = d0.astype(BF16)
        v_out[0, 1] = d1.astype(BF16)


def _nsa_prep(proj3, pos, w1, w2, gains):
    B, S, _ = proj3.shape
    n_c = S // CMP_STRIDE
    avg = jnp.asarray(_block_avg(LANES), BF16)
    col = lambda c: pl.BlockSpec((1, S, 2 * LANES), lambda b: (b, 0, c // (2 * LANES)))
    full = lambda a: pl.BlockSpec(a.shape, lambda b: (0,) * a.ndim)
    cmp_spec = pl.BlockSpec((1, NSA_GROUPS, n_c, LANES), lambda b: (b, 0, 0, 0))
    seq_spec = pl.BlockSpec((1, NSA_GROUPS, S, LANES), lambda b: (b, 0, 0, 0))
    cmp_shape = jax.ShapeDtypeStruct((B, NSA_GROUPS, n_c, LANES), BF16)
    seq_shape = jax.ShapeDtypeStruct((B, NSA_GROUPS, S, LANES), BF16)
    return pl.pallas_call(
        _nsa_prep_body,
        grid=(B,),
        in_specs=[col(COL_KVC), col(COL_KSV), col(COL_KWV), full(pos), full(w1), full(w2), full(gains), full(avg)],
        out_specs=[cmp_spec, cmp_spec, seq_spec, seq_spec, seq_spec, seq_spec],
        out_shape=[cmp_shape, cmp_shape, seq_shape, seq_shape, seq_shape, seq_shape],
        scratch_shapes=[pltpu.VMEM((S, LANES), F32), pltpu.VMEM((S, LANES), F32)],
        compiler_params=_cparams(1),
        name="nsa_prep",
    )(proj3, proj3, proj3, pos, w1, w2, gains, avg)


def _nsa_attn_body(q_ref, gate_ref, kcmp_ref, vcmp_ref, ks_ref, vs_ref, kw_ref, vw_ref,
                   qgain_ref, kgain_ref, ogain_ref, avgq_ref, avgo_ref, msct_ref, esel_ref, egate_ref, wbias_ref,
                   dbias_ref,
                   o_ref, m_scr, acc_scr):
    tq = q_ref.shape[1]
    n_cmp = kcmp_ref.shape[2]
    n_sel = msct_ref.shape[0]
    kc_len = esel_ref.shape[2]
    rows = 4 * tq
    qi = pl.program_id(1)
    q0 = qi * tq

    q = q_ref[0].astype(F32)
    ms = _seg_mean(q * q, avgq_ref[...])
    qn = q * lax.rsqrt(ms + EPS) * qgain_ref[...] * (HEAD_DIM ** -0.5)

    gate_sig = _sigmoid(gate_ref[0].astype(F32)).astype(BF16)
    gates = [_dot(gate_sig, egate_ref[j]) for j in range(3)]

    lane_q = lax.broadcasted_iota(jnp.int32, (tq, LANES), 1)
    lo_q = lane_q < HEAD_DIM
    lo_r = lax.broadcasted_iota(jnp.int32, (rows, LANES), 1) < HEAD_DIM

    blk = lax.broadcasted_iota(jnp.int32, (n_sel, tq), 0)
    cur = lax.shift_right_logical(q0 + lax.broadcasted_iota(jnp.int32, (n_sel, tq), 1), int(np.log2(SEL_BLOCK)))
    forced = (blk == 0) | (blk == cur) | (blk == cur - 1)
    future = blk > cur
    blk_f = blk.astype(F32)

    def heads4(x):
        return jnp.concatenate([x] * 4, axis=0)

    def normalised_pairs(acc, guard):
        rolled = pltpu.roll(acc, HEAD_DIM, axis=1)
        den = jnp.where(lo_r, rolled, acc)
        if guard:
            den = jnp.maximum(den, 1e-30)
        out = []
        for p in range(2):
            ev = slice((2 * p) * tq, (2 * p + 1) * tq)
            od = slice((2 * p + 1) * tq, (2 * p + 2) * tq)
            out.append(jnp.where(lo_q, acc[ev] / den[ev], rolled[od] / den[od]))
        return out

    groups = range(NSA_GROUPS)
    qs = []
    for g in groups:
        slabs = [qn[:, (2 * g + p) * LANES:(2 * g + p + 1) * LANES] for p in range(2)]
        qs.append(jnp.concatenate(
            [jnp.where(lo_q, slabs[0], 0.0), jnp.where(lo_q, 0.0, slabs[0]),
             jnp.where(lo_q, slabs[1], 0.0), jnp.where(lo_q, 0.0, slabs[1])], axis=0).astype(BF16))

    r_c = lax.broadcasted_iota(jnp.int32, (rows, n_cmp), 0)
    c_c = lax.broadcasted_iota(jnp.int32, (rows, n_cmp), 1)
    cmask = (c_c * CMP_STRIDE + (CMP_BLOCK - 1)) <= q0 + (r_c & (tq - 1))
    s_c = [jnp.where(cmask, _dot_nt(qs[g], kcmp_ref[0, g]), NEG_INF) for g in groups]
    e_c = [jnp.where(cmask, jnp.exp(s_c[g] - jnp.max(s_c[g], axis=-1, keepdims=True)), 0.0) for g in groups]
    acc_c = [_dot(e_c[g].astype(BF16), vcmp_ref[0, g]) for g in groups]
    l_c = [jnp.where(lo_r, pltpu.roll(acc_c[g], HEAD_DIM, axis=1), acc_c[g]) for g in groups]
    p_c = [e_c[g] / jnp.maximum(l_c[g], 1e-30) for g in groups]

    p_sum = [p_c[g][0:tq] + p_c[g][tq:2 * tq] + p_c[g][2 * tq:3 * tq] + p_c[g][3 * tq:4 * tq] for g in groups]
    p_hi = [p_sum[g].astype(BF16) for g in groups]
    p_lo = [(p_sum[g] - p_hi[g].astype(F32)).astype(BF16) for g in groups]
    imp = [_dot_nt(msct_ref[...], p_hi[g]) + _dot_nt(msct_ref[...], p_lo[g]) for g in groups]
    v = [jnp.where(forced, FORCE_SCORE, jnp.where(future, NEG_INF, imp[g])) for g in groups]
    sel = [jnp.zeros((n_sel, tq), F32) for g in groups]
    for _ in range(SEL_TOPK):
        mx = [jnp.max(v[g], axis=0, keepdims=True) for g in groups]
        first = [jnp.min(jnp.where(v[g] == mx[g], blk_f, float(LANES)), axis=0, keepdims=True) for g in groups]
        pick = [blk_f == first[g] for g in groups]
        sel = [jnp.where(pick[g], 1.0, sel[g]) for g in groups]
        v = [jnp.where(pick[g], BELOW_ALL, v[g]) for g in groups]
    sel_b = [sel[g].astype(BF16) for g in groups]

    cmp_s = [normalised_pairs(acc_c[g], True) for g in groups]
    n_before = lax.shift_right_logical(q0, int(np.log2(kc_len)))
    causal = dbias_ref[qi & (kc_len // tq - 1)]
    w0 = pl.multiple_of(jnp.maximum(q0 - WINDOW, 0), tq)
    n_w = WINDOW + tq
    w_case = jnp.minimum(qi, WINDOW // tq)

    def sel_keys(ref, g, kc):
        return ref[0, g, pl.ds(pl.multiple_of(kc * kc_len, kc_len), kc_len), :]

    def sel_scores(g, kc, causal_bias, shift):
        chosen = _dot_tn(sel_b[g], esel_ref[kc])
        bias = (chosen - 1.0) * (-NEG_INF)
        if causal_bias is not None:
            bias = bias + causal_bias
        if shift is not None:
            bias = bias - shift
        return _dot_nt(qs[g], sel_keys(ks_ref, g, kc)) + heads4(bias)

    def win_scores(g, shift):
        bias = wbias_ref[w_case] if shift is None else wbias_ref[w_case] - shift
        return _dot_nt(qs[g], kw_ref[0, g, pl.ds(w0, n_w), :]) + heads4(bias)

    def win_values(g):
        return vw_ref[0, g, pl.ds(w0, n_w), :]

    def finish(acc_w):
        for g in groups:
            sel_s = normalised_pairs(acc_scr[g], False)
            win_s = normalised_pairs(acc_w[g], False)
            for p in range(2):
                cols = slice((2 * g + p) * LANES, (2 * g + p + 1) * LANES)
                mix = gates[0][:, cols] * cmp_s[g][p] + gates[1][:, cols] * sel_s[p] + gates[2][:, cols] * win_s[p]
                ms_o = _seg_mean(mix * mix, avgo_ref[...])
                o_ref[0, :, cols] = (mix * lax.rsqrt(ms_o + EPS) * ogain_ref[:, cols]).astype(BF16)

    def fixed_shift_path(shift):
        def probs(s):
            return jnp.exp(s).astype(BF16)

        for g in groups:
            acc_scr[g] = jnp.zeros(acc_scr.shape[1:], F32)

        def before(kc, carry):
            s = [sel_scores(g, kc, None, shift) for g in groups]
            p = [probs(s[g]) for g in groups]
            for g in groups:
                acc_scr[g] = acc_scr[g] + _dot(p[g], sel_keys(vs_ref, g, kc))
            return carry

        lax.fori_loop(0, n_before, before, 0)
        s_d0 = sel_scores(0, n_before, causal, shift)
        s_d1 = sel_scores(1, n_before, causal, shift)
        p_d0 = probs(s_d0)
        s_w0 = win_scores(0, shift)
        acc_scr[0] = acc_scr[0] + _dot(p_d0, sel_keys(vs_ref, 0, n_before))
        p_d1 = probs(s_d1)
        s_w1 = win_scores(1, shift)
        acc_scr[1] = acc_scr[1] + _dot(p_d1, sel_keys(vs_ref, 1, n_before))
        p_w0 = probs(s_w0)
        acc_w0 = _dot(p_w0, win_values(0))
        p_w1 = probs(s_w1)
        acc_w1 = _dot(p_w1, win_values(1))
        finish([acc_w0, acc_w1])

    def online_path():
        for g in groups:
            m_scr[g] = jnp.full(m_scr.shape[1:], NEG_INF, F32)
            acc_scr[g] = jnp.zeros(acc_scr.shape[1:], F32)

        def sel_softmax(g, s):
            m_old = m_scr[g]
            m_new = jnp.maximum(m_old, jnp.max(s, axis=-1, keepdims=True))
            m_scr[g] = m_new
            return jnp.exp(s - m_new).astype(BF16), jnp.exp(m_old - m_new)

        def sel_accumulate(g, kc, p, alpha):
            acc_scr[g] = alpha * acc_scr[g] + _dot(p, sel_keys(vs_ref, g, kc))

        def win_softmax(s):
            return jnp.exp(s - jnp.max(s, axis=-1, keepdims=True)).astype(BF16)

        def before(kc, carry):
            s = [sel_scores(g, kc, None, None) for g in groups]
            pa = [sel_softmax(g, s[g]) for g in groups]
            for g in groups:
                sel_accumulate(g, kc, *pa[g])
            return carry

        lax.fori_loop(0, n_before, before, 0)
        s_d0 = sel_scores(0, n_before, causal, None)
        s_d1 = sel_scores(1, n_before, causal, None)
        pa0 = sel_softmax(0, s_d0)
        s_w0 = win_scores(0, None)
        sel_accumulate(0, n_before, *pa0)
        pa1 = sel_softmax(1, s_d1)
        s_w1 = win_scores(1, None)
        sel_accumulate(1, n_before, *pa1)
        acc_w0 = _dot(win_softmax(s_w0), win_values(0))
        acc_w1 = _dot(win_softmax(s_w1), win_values(1))
        finish([acc_w0, acc_w1])

    bound = 1.01 * (HEAD_DIM ** 0.5) * jnp.max(jnp.abs(qgain_ref[...])) * jnp.max(jnp.abs(kgain_ref[...]))
    safe = bound <= MAX_FIXED_SHIFT
    pl.when(safe)(lambda: fixed_shift_path(bound))
    pl.when(jnp.logical_not(safe))(online_path)


def _sel_from_cmp(n_cmp, n_sel):
    c0 = np.arange(n_cmp) * CMP_STRIDE
    s0 = np.arange(n_sel) * SEL_BLOCK
    ov = np.minimum(c0[None, :] + CMP_BLOCK, s0[:, None] + SEL_BLOCK) - np.maximum(c0[None, :], s0[:, None])
    m = (np.clip(ov, 0, None) / CMP_BLOCK).astype(np.float32)
    m[:, (np.arange(n_cmp) * CMP_STRIDE + CMP_BLOCK) > n_sel * SEL_BLOCK] = 0.0
    return m


def _nsa_attn(proj3, kcmp, vcmp, ks, vs, kw, vw, q_gain, k_gains, o_gain):
    B, S, _ = proj3.shape
    n_cmp = kcmp.shape[2]
    n_sel = S // SEL_BLOCK
    tq = NSA_TQ
    assert n_sel % 8 == 0 and S % SEL_KC == 0 and SEL_KC % tq == 0 and WINDOW % tq == 0 and S >= WINDOW + tq
    avgq = jnp.asarray(_block_avg(NSA_WIDTH), BF16)
    avgo = jnp.asarray(_block_avg(LANES), BF16)
    msct = jnp.asarray(_sel_from_cmp(n_cmp, n_sel), BF16)
    esel = (np.arange(n_sel)[:, None] == np.arange(S)[None, :] // SEL_BLOCK).astype(np.float32)
    esel = jnp.asarray(esel.reshape(n_sel, S // SEL_KC, SEL_KC).transpose(1, 0, 2), BF16)
    src = np.arange(LANES)[:, None]
    dst = np.arange(NSA_WIDTH)[None, :]
    egate = jnp.asarray(np.stack([(src == (dst // HEAD_DIM) * 3 + j) for j in range(3)]).astype(np.float32), BF16)
    r = np.arange(tq)[:, None]
    n_w = WINDOW + tq
    wcases = []
    for i in range(WINDOW // tq + 1):
        diff = (i * tq - max(i * tq - WINDOW, 0)) + r - np.arange(n_w)[None, :]
        wcases.append(np.where((diff >= 0) & (diff < WINDOW), 0.0, NEG_INF))
    wbias = jnp.asarray(np.stack(wcases), F32)
    dbias = jnp.asarray(np.stack([np.where(np.arange(SEL_KC)[None, :] <= i * tq + r, 0.0, NEG_INF)
                                  for i in range(SEL_KC // tq)]), F32)

    full = lambda a: pl.BlockSpec(a.shape, lambda b, i: (0,) * a.ndim)
    per_b = lambda a: pl.BlockSpec((1,) + a.shape[1:], lambda b, i: (b,) + (0,) * (a.ndim - 1))
    return pl.pallas_call(
        _nsa_attn_body,
        grid=(B, S // tq),
        in_specs=[pl.BlockSpec((1, tq, NSA_WIDTH), lambda b, i: (b, i, COL_QA // NSA_WIDTH)),
                  pl.BlockSpec((1, tq, LANES), lambda b, i: (b, i, COL_GATE // LANES)),
                  per_b(kcmp), per_b(vcmp), per_b(ks), per_b(vs), per_b(kw), per_b(vw),
                  full(q_gain), full(k_gains), full(o_gain), full(avgq), full(avgo), full(msct), full(esel), full(egate),
                  full(wbias), full(dbias)],
        out_specs=pl.BlockSpec((1, tq, NSA_WIDTH), lambda b, i: (b, i, 0)),
        out_shape=jax.ShapeDtypeStruct((B, S, NSA_WIDTH), BF16),
        scratch_shapes=[pltpu.VMEM((NSA_GROUPS, 4 * tq, 1), F32), pltpu.VMEM((NSA_GROUPS, 4 * tq, LANES), F32)],
        compiler_params=_cparams(2),
        name="nsa_attn",
    )(proj3, proj3, kcmp, vcmp, ks, vs, kw, vw, q_gain, k_gains, o_gain, avgq, avgo, msct, esel, egate, wbias, dbias)


def _retention_body(q_ref, k_ref, v_ref, g_ref, cos_ref, sin_ref, decay_ref, xi_ref, zeta_ref, gammac_ref,
                    gain_ref, avg_ref, o_ref, state_scr):
    S = q_ref.shape[1]
    C = RET_CHUNK
    lane = lax.broadcasted_iota(jnp.int32, (C, LANES), 1)
    lo = lane < HEAD_DIM
    first_half = (lane & (HEAD_DIM - 1)) < HEAD_DIM // 2
    r = lax.broadcasted_iota(jnp.int32, (LANES, LANES), 0)
    c = lax.broadcasted_iota(jnp.int32, (LANES, LANES), 1)
    same_head = (r < HEAD_DIM) == (c < HEAD_DIM)
    avg = avg_ref[...]
    state_scr[...] = jnp.zeros(state_scr.shape, F32)

    def rope(x, cos, sin):
        swapped = jnp.where(first_half, pltpu.roll(x, LANES - HEAD_DIM // 2, axis=1),
                            pltpu.roll(x, HEAD_DIM // 2, axis=1))
        return x * cos + swapped * sin

    def chunk(n, carry):
        r0 = pl.multiple_of(n * C, C)
        cos = cos_ref[pl.ds(r0, C), :]
        sin = sin_ref[pl.ds(r0, C), :]
        pairs = range(RET_HEADS // 2)
        cols = [slice(p * LANES, (p + 1) * LANES) for p in pairs]
        q = [rope(q_ref[0, pl.ds(r0, C), cols[p]].astype(F32), cos, sin) for p in pairs]
        k = [rope(k_ref[0, pl.ds(r0, C), cols[p]].astype(F32), cos, sin) * (HEAD_DIM ** -0.5) for p in pairs]
        vb = [v_ref[0, pl.ds(r0, C), cols[p]] for p in pairs]
        kb = [k[p].astype(BF16) for p in pairs]
        inner = [_dot_nt(jnp.where(lo if half == 0 else ~lo, q[p], 0.0).astype(BF16), kb[p])
                 * decay_ref[2 * p + half] for p in pairs for half in range(2)]
        state = [state_scr[p] for p in pairs]
        cross = [_dot(q[p].astype(BF16), state[p].astype(BF16)) * xi_ref[p] for p in pairs]
        upd = [_dot_tn((k[p] * zeta_ref[p]).astype(BF16), vb[p]) for p in pairs]
        outs = [_dot(inner[i].astype(BF16), vb[i // 2]) for i in range(RET_HEADS)]
        for p in pairs:
            state_scr[p] = gammac_ref[p] * state[p] + jnp.where(same_head, upd[p], 0.0)
        y = jnp.concatenate([jnp.where(lo, outs[2 * p], outs[2 * p + 1]) + cross[p] for p in pairs], axis=0)
        mu = _seg_mean(y, avg)
        d = y - mu
        var = _seg_mean(d * d, avg)
        yn = d * lax.rsqrt(var + EPS)
        for p in pairs:
            gate = g_ref[0, pl.ds(r0, C), cols[p]].astype(F32)
            o_ref[0, pl.ds(r0, C), cols[p]] = (_silu(gate) * (yn[p * C:(p + 1) * C] * gain_ref[:, cols[p]])).astype(BF16)
        return carry

    lax.fori_loop(0, S // C, chunk, 0)


def _retention_tables(S):
    half = HEAD_DIM // 2
    inv_freq = ROPE_BASE ** (-jnp.arange(half, dtype=F32) / half)
    ang = jnp.arange(S, dtype=F32)[:, None] * inv_freq[None, :]
    cos, sin = jnp.cos(ang), jnp.sin(ang)
    cos_t = jnp.tile(cos, (1, 4))
    sin_t = jnp.tile(jnp.concatenate([-sin, sin], axis=1), (1, 2))
    C = RET_CHUNK
    H = RET_HEADS
    log_gamma = jnp.log1p(-jnp.power(2.0, -5.0 - jnp.arange(H, dtype=F32)))
    i = jnp.arange(C, dtype=F32)
    rel = i[:, None] - i[None, :]
    decay = jnp.where(rel >= 0, jnp.exp(jnp.maximum(rel, 0.0)[None] * log_gamma[:, None, None]), 0.0)
    xi = jnp.exp((i + 1.0)[:, None] * log_gamma[None, :])
    zeta = jnp.exp((C - 1.0 - i)[:, None] * log_gamma[None, :])
    gamma_c = jnp.exp(C * log_gamma)
    per_pair = lambda t: jnp.repeat(t.T.reshape(H // 2, 2, -1), HEAD_DIM, axis=1).transpose(0, 2, 1)
    gammac = jnp.repeat(gamma_c.reshape(H // 2, 2), HEAD_DIM, axis=1)[:, None, :]
    return cos_t, sin_t, decay, per_pair(xi), per_pair(zeta), gammac


def _retention(proj3, gain):
    B, S, _ = proj3.shape
    cos_t, sin_t, decay, xi, zeta, gammac = _retention_tables(S)
    avg = jnp.asarray(_block_avg(LANES), BF16)
    col = lambda c: pl.BlockSpec((1, S, RET_WIDTH), lambda b: (b, 0, c // RET_WIDTH))
    full = lambda a: pl.BlockSpec(a.shape, lambda b: (0,) * a.ndim)
    return pl.pallas_call(
        _retention_body,
        grid=(B,),
        in_specs=[col(COL_QR), col(COL_KR), col(COL_VR), col(COL_GR), full(cos_t), full(sin_t), full(decay),
                  full(xi), full(zeta), full(gammac), full(gain), full(avg)],
        out_specs=pl.BlockSpec((1, S, RET_WIDTH), lambda b: (b, 0, 0)),
        out_shape=jax.ShapeDtypeStruct((B, S, RET_WIDTH), BF16),
        scratch_shapes=[pltpu.VMEM((RET_HEADS // 2, LANES, LANES), F32)],
        compiler_params=_cparams(1),
        name="retention",
    )(proj3, proj3, proj3, proj3, cos_t, sin_t, decay, xi, zeta, gammac, gain, avg)


def _mem_prep_body(mem_ref, g_ref, wkv_ref, kgain_ref, avg_ref, k_ref, v_ref):
    hm = _rms_full(mem_ref[0], g_ref[...]).astype(BF16)
    kv = _dot(hm, wkv_ref[...])
    k = kv[:, :MEM_WIDTH]
    ms = _seg_mean(k * k, avg_ref[...])
    k_ref[0] = (k * lax.rsqrt(ms + EPS) * kgain_ref[...]).astype(BF16)
    v_ref[0] = kv[:, MEM_WIDTH:].astype(BF16)


def _mem_prep(mem, g, wkv, kgain):
    B, M, _ = mem.shape
    avg = jnp.asarray(_block_avg(MEM_WIDTH), BF16)
    full = lambda a: pl.BlockSpec(a.shape, lambda b: (0,) * a.ndim)
    out_spec = pl.BlockSpec((1, M, MEM_WIDTH), lambda b: (b, 0, 0))
    out_shape = jax.ShapeDtypeStruct((B, M, MEM_WIDTH), BF16)
    return pl.pallas_call(
        _mem_prep_body,
        grid=(B,),
        in_specs=[pl.BlockSpec((1, M, D_MODEL), lambda b: (b, 0, 0)), full(g), full(wkv), full(kgain), full(avg)],
        out_specs=[out_spec, out_spec],
        out_shape=[out_shape, out_shape],
        compiler_params=_cparams(1),
        name="mem_prep",
    )(mem, g, wkv, kgain, avg)


def _post_body(x_ref, oa_ref, ob_ref, wout_ref, mk_ref, mv_ref, gx_ref, wq_ref, qgain_ref, avg_ref, wo_ref,
               gf_ref, wr_ref, br_ref, tri_ref, x2_ref, h_ref, route_ref, count_ref):
    tm = x_ref.shape[1] // POST_CHAINS
    chains = range(POST_CHAINS)
    rows = [slice(c * tm, (c + 1) * tm) for c in chains]
    x1 = [x_ref[0, rows[c]] + _dot(oa_ref[0, rows[c]], wout_ref[0:NSA_WIDTH, :])
          + _dot(ob_ref[0, rows[c]], wout_ref[NSA_WIDTH:, :]) for c in chains]

    h = [_rms_full(x1[c], gx_ref[...]).astype(BF16) for c in chains]
    q = [_dot(h[c], wq_ref[...]) for c in chains]
    ms = [_seg_mean(q[c] * q[c], avg_ref[...]) for c in chains]
    q = [q[c] * lax.rsqrt(ms[c] + EPS) * qgain_ref[...] * (HEAD_DIM ** -0.5) for c in chains]
    lane = lax.broadcasted_iota(jnp.int32, (tm, LANES), 1)
    lo = lane < HEAD_DIM
    heads = [(c, p, half) for c in chains for p in range(MEM_HEADS // 2) for half in range(2)]
    s = [_dot_nt(jnp.where(lo if half == 0 else ~lo, q[c][:, p * LANES:(p + 1) * LANES], 0.0).astype(BF16),
                 mk_ref[0, :, p * LANES:(p + 1) * LANES]) for c, p, half in heads]
    e = [jnp.exp(s[i] - jnp.max(s[i], axis=-1, keepdims=True)) for i in range(len(heads))]
    pr = [(e[i] / jnp.sum(e[i], axis=-1, keepdims=True)).astype(BF16) for i in range(len(heads))]
    outs = [_dot(pr[i], mv_ref[0, :, heads[i][1] * LANES:(heads[i][1] + 1) * LANES]) for i in range(len(heads))]
    per_chain = MEM_HEADS
    o = [jnp.concatenate([jnp.where(lo, outs[c * per_chain + 2 * p], outs[c * per_chain + 2 * p + 1])
                          for p in range(MEM_HEADS // 2)], axis=1).astype(BF16) for c in chains]
    x2 = [x1[c] + _dot(o[c], wo_ref[...]) for c in chains]
    for c in chains:
        x2_ref[0, rows[c]] = x2[c]

    hf = [_rms_full(x2[c], gf_ref[...]).astype(BF16) for c in chains]
    for c in chains:
        h_ref[0, rows[c]] = hf[c]
    logits = [_dot(hf[c], wr_ref[...]) + br_ref[...] for c in chains]
    lane_f = lane.astype(F32)
    big = float(LANES)
    picks = []
    for c in chains:
        gl = jnp.where(lane < N_GROUPS, logits[c], BELOW_ALL)
        gmax = jnp.max(gl, axis=-1, keepdims=True)
        grp = jnp.min(jnp.where(gl == gmax, lane_f, big), axis=-1, keepdims=True)
        g_w = 1.0 / jnp.sum(jnp.where(lane < N_GROUPS, jnp.exp(gl - gmax), 0.0), axis=-1, keepdims=True)
        e_lo = N_GROUPS + grp * EXPERTS_PER_GROUP
        el = jnp.where((lane_f >= e_lo) & (lane_f < e_lo + EXPERTS_PER_GROUP), logits[c], BELOW_ALL)
        v0 = jnp.max(el, axis=-1, keepdims=True)
        i0 = jnp.min(jnp.where(el == v0, lane_f, big), axis=-1, keepdims=True)
        el = jnp.where(lane_f == i0, BELOW_ALL, el)
        v1 = jnp.max(el, axis=-1, keepdims=True)
        i1 = jnp.min(jnp.where(el == v1, lane_f, big), axis=-1, keepdims=True)
        e1 = jnp.exp(v1 - v0)
        picks.append((i0 - N_GROUPS, i1 - N_GROUPS, g_w / (1.0 + e1), g_w * e1 / (1.0 + e1)))

    hot = [[lane_f == picks[c][s] for s in range(2)] for c in chains]
    both = jnp.concatenate([jnp.where(hot[c][0], 1.0, 0.0) + jnp.where(hot[c][1], 1.0, 0.0) for c in chains], axis=0)
    before = _dot(tri_ref[...], both.astype(BF16))
    count_ref[0] = jnp.broadcast_to(jnp.sum(both, axis=0, keepdims=True), count_ref.shape[1:])
    for c in chains:
        e0, e1, w0, w1 = picks[c]
        r0 = jnp.sum(jnp.where(hot[c][0], before[rows[c]], 0.0), axis=-1, keepdims=True)
        r1 = jnp.sum(jnp.where(hot[c][1], before[rows[c]], 0.0), axis=-1, keepdims=True)
        cols = (e0, e1, w0, w1, r0, r1)
        route = jnp.zeros((tm, LANES), F32)
        for k in range(len(cols)):
            route = jnp.where(lane == k, cols[k], route)
        route_ref[0, rows[c]] = route


def _post(x, oa, ob, wout, mk, mv, gx, wq, qgain, wo, gf, wr, br):
    B, S, _ = x.shape
    tm = MOE_TM
    n_s = S // tm
    avg = jnp.asarray(_block_avg(MEM_WIDTH), BF16)
    tri = jnp.asarray(np.tril(np.ones((tm, tm), np.float32), -1), BF16)
    full = lambda a: pl.BlockSpec(a.shape, lambda b, i: (0,) * a.ndim)
    per_b = lambda a: pl.BlockSpec((1,) + a.shape[1:], lambda b, i: (b,) + (0,) * (a.ndim - 1))
    tile = lambda w: pl.BlockSpec((1, tm, w), lambda b, i: (b, i, 0))
    return pl.pallas_call(
        _post_body,
        grid=(B, n_s),
        in_specs=[tile(D_MODEL), tile(NSA_WIDTH), tile(RET_WIDTH), full(wout), per_b(mk), per_b(mv), full(gx),
                  full(wq), full(qgain), full(avg), full(wo), full(gf), full(wr), full(br), full(tri)],
        out_specs=[tile(D_MODEL), tile(D_MODEL), tile(LANES),
                   pl.BlockSpec((1, 8, LANES), lambda b, i: (b * n_s + i, 0, 0))],
        out_shape=[jax.ShapeDtypeStruct((B, S, D_MODEL), F32), jax.ShapeDtypeStruct((B, S, D_MODEL), BF16),
                   jax.ShapeDtypeStruct((B, S, LANES), F32), jax.ShapeDtypeStruct((B * n_s, 8, LANES), F32)],
        compiler_params=_cparams(2),
        name="post_mixer",
    )(x, oa, ob, wout, mk, mv, gx, wq, qgain, avg, wo, gf, wr, br, tri)


def _row_copy(src, dst, sem):
    return pltpu.make_async_copy(src, dst, sem)


def _run_pieces(n, max_piece, fn):
    b = RUN_ALIGN
    while b <= max_piece:
        pl.when((n & b) != 0)(functools.partial(fn, n & (-2 * b), b))
        b *= 2


def _move_groups(i, gmap_ref, n_loc, copy):
    n_groups = n_loc // RUN_ALIGN
    for j in range(n_groups):
        glob = pl.multiple_of(gmap_ref[i * n_groups + j], RUN_ALIGN)
        copy(pl.ds(j * RUN_ALIGN, RUN_ALIGN), pl.ds(glob, RUN_ALIGN)).start()


def _local_positions(route, loff_row):
    lane = lax.broadcasted_iota(jnp.int32, route.shape, 1).astype(F32)
    pos = []
    for s in range(2):
        base = jnp.sum(jnp.where(lane == route[:, s:s + 1], loff_row, 0.0), axis=-1, keepdims=True)
        pos.append(base + route[:, 4 + s:5 + s])
    return pos


def _scatter_body(gmap_ref, tstart_ref, tlen_ref, nact_ref,
                  h_ref, route_ref, lofff_ref, xs_ref, xloc, zbuf, sems):
    i = pl.program_id(0)
    tm = h_ref.shape[0]
    n_loc = xloc.shape[1]
    slot = i & 1
    sem = sems.at[0]

    def tails(start):
        def per_expert(e, carry):
            n = tlen_ref[e]
            st = tstart_ref[e]

            def piece(off, size):
                c = _row_copy(zbuf.at[pl.ds(0, size)], xs_ref.at[pl.ds(pl.multiple_of(st + off, RUN_ALIGN), size)], sem)
                c.start() if start else c.wait()

            _run_pieces(n, MOE_RB // 2, piece)
            return carry

        lax.fori_loop(0, N_EXPERTS, per_expert, 0)

    def unused(start):
        rows = zbuf.shape[0]

        def per_unit(u, carry):
            c = _row_copy(zbuf, xs_ref.at[pl.ds(pl.multiple_of(u * rows, rows), rows)], sem)
            c.start() if start else c.wait()
            return carry

        lax.fori_loop(nact_ref[0] * (MOE_RB // rows), xs_ref.shape[0] // rows, per_unit, 0)

    @pl.when(i == 0)
    def _():
        zbuf[...] = jnp.zeros(zbuf.shape, zbuf.dtype)
        tails(True)
        unused(True)
        tails(False)
        unused(False)

    pos = _local_positions(route_ref[...], lofff_ref[0, 0:1, :])
    col = lax.broadcasted_iota(jnp.int32, (tm, n_loc), 1).astype(F32)
    perm_t = jnp.where((col == pos[0]) | (col == pos[1]), 1.0, 0.0).astype(BF16)
    xloc[slot] = _dot_tn(perm_t, h_ref[...]).astype(BF16)

    def wait_slot(s):
        _row_copy(xloc.at[s], xs_ref.at[pl.ds(0, n_loc)], sems.at[s]).wait()

    pl.when(i > 0)(lambda: wait_slot(1 - slot))
    _move_groups(i, gmap_ref, n_loc, lambda loc, glob: _row_copy(xloc.at[slot, loc], xs_ref.at[glob], sems.at[slot]))
    pl.when(i == pl.num_programs(0) - 1)(lambda: wait_slot(slot))


def _scatter_rows(tables, h2d, route2d, loff_f, n_rows):
    T = h2d.shape[0]
    tm = MOE_TM
    n_loc = MOE_NLOC
    tile = lambda w: pl.BlockSpec((tm, w), lambda i, *_: (i, 0))
    grid_spec = pltpu.PrefetchScalarGridSpec(
        num_scalar_prefetch=4,
        grid=(T // tm,),
        in_specs=[tile(D_MODEL), tile(LANES), pl.BlockSpec((1, 8, LANES), lambda i, *_: (i, 0, 0))],
        out_specs=pl.BlockSpec(memory_space=pl.ANY),
        scratch_shapes=[pltpu.VMEM((2, n_loc, D_MODEL), BF16), pltpu.VMEM((MOE_RB // 2, D_MODEL), BF16),
                        pltpu.SemaphoreType.DMA((2,))],
    )
    return pl.pallas_call(
        _scatter_body,
        grid_spec=grid_spec,
        out_shape=jax.ShapeDtypeStruct((n_rows, D_MODEL), BF16),
        compiler_params=_cparams(1),
        name="moe_scatter",
    )(*tables, h2d, route2d, loff_f)


def _expert_body(blk0_ref, nblk_ref, n_act_ref, xs_ref, wg_ref, wu_ref, wd_ref, ys_ref,
                 wg_b, wu_b, wd_b, xbuf, ybuf, sem_in, sem_out):
    e = pl.program_id(0)
    rb = xbuf.shape[1]
    n_act = n_act_ref[0]
    b0 = blk0_ref[e]

    def rows(g):
        return pl.ds(pl.multiple_of(g * rb, rb), rb)

    def x_copy(g, slot):
        return _row_copy(xs_ref.at[rows(g)], xbuf.at[slot], sem_in.at[slot])

    def y_copy(g, slot):
        return _row_copy(ybuf.at[slot], ys_ref.at[rows(g)], sem_out.at[slot])

    pl.when(e == 0)(lambda: x_copy(0, 0).start())
    wg_b[...] = wg_ref[0].astype(BF16)
    wu_b[...] = wu_ref[0].astype(BF16)
    wd_b[...] = wd_ref[0].astype(BF16)

    def block(j, carry):
        g = b0 + j
        slot = g & 1
        x_copy(g, slot).wait()
        pl.when(g + 1 < n_act)(lambda: x_copy(g + 1, 1 - slot).start())
        x = xbuf[slot]
        a = _dot(x, wg_b[...])
        b = _dot(x, wu_b[...])
        y = _dot((_silu(a) * b).astype(BF16), wd_b[...]).astype(BF16)
        pl.when(g >= 2)(lambda: y_copy(g - 2, slot).wait())
        ybuf[slot] = y
        y_copy(g, slot).start()
        return carry

    lax.fori_loop(0, nblk_ref[e], block, 0)

    @pl.when(e == pl.num_programs(0) - 1)
    def _():
        pl.when(n_act >= 2)(lambda: y_copy(n_act - 2, n_act & 1).wait())
        y_copy(n_act - 1, (n_act - 1) & 1).wait()
        ybuf[0] = jnp.zeros(ybuf.shape[1:], ybuf.dtype)
        n_blocks = ys_ref.shape[0] // rb

        def fill(start):
            def per_block(g, carry):
                c = y_copy(g, 0)
                c.start() if start else c.wait()
                return carry

            lax.fori_loop(n_act, n_blocks, per_block, 0)

        fill(True)
        fill(False)


def _experts(blk0, nblk, n_act, xs, n_rows, wg, wu, wd):
    rb = MOE_RB
    weight = lambda a: pl.BlockSpec((1,) + a.shape[1:], lambda e, *_: (e, 0, 0))
    grid_spec = pltpu.PrefetchScalarGridSpec(
        num_scalar_prefetch=3,
        grid=(N_EXPERTS,),
        in_specs=[pl.BlockSpec(memory_space=pl.ANY), weight(wg), weight(wu), weight(wd)],
        out_specs=pl.BlockSpec(memory_space=pl.ANY),
        scratch_shapes=[pltpu.VMEM((D_MODEL, EXPERT_FF), BF16), pltpu.VMEM((D_MODEL, EXPERT_FF), BF16),
                        pltpu.VMEM((EXPERT_FF, D_MODEL), BF16), pltpu.VMEM((2, rb, D_MODEL), BF16),
                        pltpu.VMEM((2, rb, D_MODEL), BF16), pltpu.SemaphoreType.DMA((2,)),
                        pltpu.SemaphoreType.DMA((2,))],
    )
    return pl.pallas_call(
        _expert_body,
        grid_spec=grid_spec,
        out_shape=jax.ShapeDtypeStruct((n_rows, D_MODEL), BF16),
        compiler_params=_cparams(1),
        name="moe_experts",
    )(blk0, nblk, n_act, xs, wg, wu, wd)


def _combine_body(gmap_ref, x_ref, route_ref, lofff_ref, ys_ref, o_ref, yloc, sems):
    i = pl.program_id(0)
    tm = x_ref.shape[0]
    n_loc = yloc.shape[1]
    slot = i & 1

    def fetch(tile, s):
        _move_groups(tile, gmap_ref, n_loc, lambda loc, glob: _row_copy(ys_ref.at[glob], yloc.at[s, loc], sems.at[s]))

    pl.when(i == 0)(lambda: fetch(i, slot))
    _row_copy(ys_ref.at[pl.ds(0, n_loc)], yloc.at[slot], sems.at[slot]).wait()
    pl.when(i + 1 < pl.num_programs(0))(lambda: fetch(i + 1, 1 - slot))

    route = route_ref[...]
    pos = _local_positions(route, lofff_ref[0, 0:1, :])
    col = lax.broadcasted_iota(jnp.int32, (tm, n_loc), 1).astype(F32)
    perm_w = (jnp.where(col == pos[0], route[:, 2:3], 0.0) + jnp.where(col == pos[1], route[:, 3:4], 0.0)).astype(BF16)
    o_ref[...] = x_ref[...] + _dot(perm_w, yloc[slot])


def _combine(tables, x2d, route2d, loff_f, ys):
    T = x2d.shape[0]
    tm = MOE_TM
    n_loc = MOE_NLOC
    tile = lambda w: pl.BlockSpec((tm, w), lambda i, *_: (i, 0))
    grid_spec = pltpu.PrefetchScalarGridSpec(
        num_scalar_prefetch=1,
        grid=(T // tm,),
        in_specs=[tile(D_MODEL), tile(LANES), pl.BlockSpec((1, 8, LANES), lambda i, *_: (i, 0, 0)),
                  pl.BlockSpec(memory_space=pl.ANY)],
        out_specs=tile(D_MODEL),
        scratch_shapes=[pltpu.VMEM((2, n_loc, D_MODEL), BF16), pltpu.SemaphoreType.DMA((2,))],
    )
    return pl.pallas_call(
        _combine_body,
        grid_spec=grid_spec,
        out_shape=jax.ShapeDtypeStruct((T, D_MODEL), F32),
        compiler_params=_cparams(1),
        name="moe_combine",
    )(*tables, x2d, route2d, loff_f, ys)


def _permute_w_in(w):
    kv0 = NSA_WIDTH
    gate0 = kv0 + 6 * NSA_GROUPS * HEAD_DIM
    ret0 = gate0 + 3 * NSA_HEADS
    pad = jnp.zeros((w.shape[0], PROJ_PAD - w.shape[1]), w.dtype)
    return jnp.concatenate([w[:, :kv0], w[:, ret0:], w[:, kv0:gate0], w[:, gate0:ret0], pad], axis=1).astype(BF16)


def _compress_weights(pos, w1, w2):
    eye = jnp.eye(NSA_GROUPS, dtype=F32)
    w1r = w1.reshape(CMP_BLOCK, HEAD_DIM, CMP_HIDDEN)
    w1b = jnp.einsum('ldh,gk->lgdkh', w1r, eye).reshape(2, CMP_STRIDE * LANES, NSA_GROUPS * CMP_HIDDEN)
    w2b = jnp.einsum('hd,gk->ghkd', w2, eye).reshape(NSA_GROUPS * CMP_HIDDEN, LANES)
    posb = jnp.tile(pos, (1, NSA_GROUPS)).reshape(2, 1, CMP_STRIDE * LANES)
    return posb, w1b.astype(BF16), w2b.astype(BF16)


def _dup2(g):
    return jnp.tile(g.reshape(1, HEAD_DIM), (1, 2))


def kernel(x, mem, mix_norm, w_in, nsa_q_norm, nsa_kcmp_norm, nsa_ksel_norm, nsa_kwin_norm, cmp_pos_k, cmp_pos_v, cmp_k_w1, cmp_k_w2, cmp_v_w1, cmp_v_w2, nsa_out_norm, ret_out_norm, w_out, mem_x_norm, mem_kv_norm, mem_wq, mem_wkv, mem_q_norm, mem_k_norm, mem_wo, ffn_norm, router_group_w, router_group_b, router_expert_w, router_expert_b, exp_w_gate, exp_w_up, exp_w_down):
    B, S, D = x.shape
    T = B * S
    depth = mix_norm.shape[0]
    for l in range(depth):
        proj = _proj(x.reshape(T, D), mix_norm[l].reshape(1, D), _permute_w_in(w_in[l])).reshape(B, S, PROJ_PAD)
        pk, w1k, w2k = _compress_weights(cmp_pos_k[l], cmp_k_w1[l], cmp_k_w2[l])
        pv, w1v, w2v = _compress_weights(cmp_pos_v[l], cmp_v_w1[l], cmp_v_w2[l])
        gains = jnp.stack([_dup2(nsa_kcmp_norm[l]), _dup2(nsa_ksel_norm[l]), _dup2(nsa_kwin_norm[l])])
        kcmp, vcmp, ks, vs, kw, vw = _nsa_prep(proj, jnp.stack([pk, pv]), jnp.stack([w1k, w1v]),
                                               jnp.stack([w2k, w2v]), gains)
        o_a = _nsa_attn(proj, kcmp, vcmp, ks, vs, kw, vw,
                        jnp.tile(nsa_q_norm[l].reshape(1, HEAD_DIM), (1, NSA_HEADS)), gains[1:],
                        nsa_out_norm[l].reshape(1, NSA_WIDTH))
        o_b = _retention(proj, ret_out_norm[l].reshape(1, RET_WIDTH))
        mk, mv = _mem_prep(mem, mem_kv_norm[l].reshape(1, D), mem_wkv[l].astype(BF16),
                           jnp.tile(mem_k_norm[l].reshape(1, HEAD_DIM), (1, MEM_HEADS)))
        w_r = jnp.concatenate([router_group_w[l],
                               router_expert_w[l].transpose(1, 0, 2).reshape(D, N_EXPERTS),
                               jnp.zeros((D, LANES - N_GROUPS - N_EXPERTS), F32)], axis=1).astype(BF16)
        b_r = jnp.concatenate([router_group_b[l], router_expert_b[l].reshape(N_EXPERTS),
                               jnp.zeros((LANES - N_GROUPS - N_EXPERTS,), F32)]).reshape(1, LANES)
        x2, hf, route, counts = _post(
            x, o_a, o_b, w_out[l].astype(BF16), mk, mv, mem_x_norm[l].reshape(1, D), mem_wq[l].astype(BF16),
            jnp.tile(mem_q_norm[l].reshape(1, HEAD_DIM), (1, MEM_HEADS)), mem_wo[l].astype(BF16),
            ffn_norm[l].reshape(1, D), w_r, b_r)
        route2d = route.reshape(T, LANES)
        n_tiles = T // MOE_TM
        cnt = counts[:, 0, :N_EXPERTS].astype(jnp.int32)
        cnt = (cnt + RUN_ALIGN - 1) // RUN_ALIGN * RUN_ALIGN
        loff = jnp.cumsum(cnt, axis=1) - cnt
        total = jnp.sum(cnt, axis=0)
        padded = (total + MOE_RB - 1) // MOE_RB * MOE_RB
        pend = jnp.cumsum(padded)
        pstart = pend - padded
        goff = pstart[None, :] + jnp.cumsum(cnt, axis=0) - cnt
        n_rows = 2 * T + n_tiles * N_EXPERTS * RUN_ALIGN + N_EXPERTS * MOE_RB
        n_act = (pend[-1:] // MOE_RB).astype(jnp.int32)
        loff_f = jnp.broadcast_to(jnp.pad(loff.astype(F32), ((0, 0), (0, LANES - N_EXPERTS)))[:, None, :],
                                  (n_tiles, 8, LANES))
        grp_row = jnp.arange(MOE_NLOC // RUN_ALIGN, dtype=jnp.int32) * RUN_ALIGN
        inside = ((loff[:, None, :] <= grp_row[None, :, None])
                  & (grp_row[None, :, None] < (loff + cnt)[:, None, :])).astype(jnp.int32)
        shift = jnp.sum(inside * (goff - loff)[:, None, :], axis=2)
        used = jnp.sum(inside, axis=2) > 0
        gmap_scatter = jnp.where(used, shift + grp_row[None, :], n_rows + grp_row[None, :]).reshape(-1)
        gmap_gather = jnp.where(used, shift + grp_row[None, :], 0).reshape(-1)
        xs = _scatter_rows((gmap_scatter, pstart + total, padded - total, n_act), hf.reshape(T, D), route2d, loff_f,
                           n_rows + MOE_NLOC)
        ys = _experts(pstart // MOE_RB, padded // MOE_RB, n_act, xs, n_rows,
                      exp_w_gate[l], exp_w_up[l], exp_w_down[l])
        x = _combine((gmap_gather,), x2.reshape(T, D), route2d, loff_f, ys).reshape(B, S, D)
    return x
```

```python
import functools

import numpy as np
import jax
import jax.numpy as jnp
from jax import lax
from jax.experimental import pallas as pl
from jax.experimental.pallas import tpu as pltpu

F32 = jnp.float32
BF16 = jnp.bfloat16

D_MODEL = 1024
HEAD_DIM = 64
LANES = 128
NSA_HEADS = 8
NSA_GROUPS = 2
NSA_WIDTH = NSA_HEADS * HEAD_DIM
CMP_BLOCK = 32
CMP_STRIDE = 16
CMP_HIDDEN = 2 * HEAD_DIM
SEL_BLOCK = 64
SEL_TOPK = 8
WINDOW = 512
RET_HEADS = 8
RET_WIDTH = RET_HEADS * HEAD_DIM
RET_CHUNK = 128
ROPE_BASE = 10000.0
MEM_HEADS = 4
MEM_WIDTH = MEM_HEADS * HEAD_DIM
N_GROUPS = 4
EXPERTS_PER_GROUP = 8
N_EXPERTS = N_GROUPS * EXPERTS_PER_GROUP
EXPERT_FF = D_MODEL // 4
EPS = 1e-6
NEG_INF = -1e30
FORCE_SCORE = 1e9
BELOW_ALL = -3e38
MAX_FIXED_SHIFT = 40.0

COL_QA = 0
COL_QR, COL_KR, COL_VR, COL_GR = 512, 1024, 1536, 2048
COL_KVC, COL_KSV, COL_KWV = 2560, 2816, 3072
COL_GATE = 3328
PROJ_PAD = 3456

PROJ_TM = 512
NSA_TQ = 256
SEL_KC = 512
WIN_KEYS = WINDOW + NSA_TQ
POST_CHAINS = 2
MOE_TM = 512
MOE_RB = 256
MOE_SLOTS = 4
RUN_ALIGN = 16
MOE_NLOC = 2 * MOE_TM + N_EXPERTS * RUN_ALIGN
VMEM_LIMIT = 56 * 1024 * 1024


def _cparams(n_axes):
    return pltpu.CompilerParams(dimension_semantics=("arbitrary",) * n_axes,
                                vmem_limit_bytes=VMEM_LIMIT)


def _dot(a, b):
    return jnp.dot(a, b, preferred_element_type=F32)


def _dot_nt(a, b):
    return lax.dot_general(a, b, (((1,), (1,)), ((), ())), preferred_element_type=F32)


def _dot_tn(a, b):
    return lax.dot_general(a, b, (((0,), (0,)), ((), ())), preferred_element_type=F32)


def _rms_full(x, g):
    ms = jnp.mean(x * x, axis=-1, keepdims=True)
    return x * lax.rsqrt(ms + EPS) * g


def _seg_mean(x, avg):
    return _dot(x.astype(BF16), avg)


def _silu(x):
    return x * (1.0 / (1.0 + jnp.exp(-x)))


def _sigmoid(x):
    return 1.0 / (1.0 + jnp.exp(-x))


def _block_avg(width):
    i = np.arange(width)
    return ((i[:, None] // HEAD_DIM == i[None, :] // HEAD_DIM) / HEAD_DIM).astype(np.float32)


def _proj_body(x_ref, g_ref, w_ref, o_ref):
    h = _rms_full(x_ref[...], g_ref[...]).astype(BF16)
    step = PROJ_PAD // 3
    for j in range(3):
        o_ref[:, j * step:(j + 1) * step] = _dot(h, w_ref[:, j * step:(j + 1) * step]).astype(BF16)


def _proj(x2d, g, w):
    T = x2d.shape[0]
    return pl.pallas_call(
        _proj_body,
        grid=(T // PROJ_TM,),
        in_specs=[pl.BlockSpec((PROJ_TM, D_MODEL), lambda i: (i, 0)),
                  pl.BlockSpec((1, D_MODEL), lambda i: (0, 0)),
                  pl.BlockSpec((D_MODEL, PROJ_PAD), lambda i: (0, 0))],
        out_specs=pl.BlockSpec((PROJ_TM, PROJ_PAD), lambda i: (i, 0)),
        out_shape=jax.ShapeDtypeStruct((T, PROJ_PAD), BF16),
        compiler_params=_cparams(1),
        name="proj",
    )(x2d, g, w)


def _dup_groups(x):
    lane = lax.broadcasted_iota(jnp.int32, x.shape, 1)
    xs = pltpu.roll(x, HEAD_DIM, axis=1)
    lo = lane < HEAD_DIM
    return jnp.where(lo, x, xs), jnp.where(lo, xs, x)


def _ones_groups(x):
    lane = lax.broadcasted_iota(jnp.int32, x.shape, 1)
    lo = lane < HEAD_DIM
    return jnp.where(lo, x, 1.0), jnp.where(lo, pltpu.roll(x, HEAD_DIM, axis=1), 1.0)


def _nsa_prep_body(kvc_ref, ksv_ref, kwv_ref, pos_ref, w1_ref, w2_ref, gain_ref, avg_ref,
                   kcmp_ref, vcmp_ref, ks_ref, vs_ref, kw_ref, vw_ref, scr_k, scr_v):
    avg = avg_ref[...]
    n_c = scr_k.shape[0] // CMP_STRIDE
    scr_k[...] = kvc_ref[0, :, 0:LANES].astype(F32)
    scr_v[...] = kvc_ref[0, :, LANES:2 * LANES].astype(F32)
    for j, out_ref, scr in ((0, kcmp_ref, scr_k), (1, vcmp_ref, scr_v)):
        ycat = jnp.concatenate(
            [scr[pl.ds(l, n_c, stride=CMP_STRIDE), :] for l in range(CMP_STRIDE)], axis=1)
        first = _dot((ycat + pos_ref[j, 0]).astype(BF16), w1_ref[j, 0])
        second = _dot((ycat + pos_ref[j, 1]).astype(BF16), w1_ref[j, 1])
        hidden = first + pltpu.roll(second, n_c - 1, axis=0)
        cmp_tok = _dot(_silu(hidden).astype(BF16), w2_ref[j])
        if j == 0:
            ms = _seg_mean(cmp_tok * cmp_tok, avg)
            cmp_tok = cmp_tok * lax.rsqrt(ms + EPS) * gain_ref[0]
        d0, d1 = _dup_groups(cmp_tok) if j == 0 else _ones_groups(cmp_tok)
        out_ref[0, 0] = d0.astype(BF16)
        out_ref[0, 1] = d1.astype(BF16)

    for src_ref, k_out, v_out, gi in ((ksv_ref, ks_ref, vs_ref, 1), (kwv_ref, kw_ref, vw_ref, 2)):
        k = src_ref[0, :, 0:LANES].astype(F32)
        ms = _seg_mean(k * k, avg)
        k = k * lax.rsqrt(ms + EPS) * gain_ref[gi]
        d0, d1 = _dup_groups(k)
        k_out[0, 0] = d0.astype(BF16)
        k_out[0, 1] = d1.astype(BF16)
        d0, d1 = _ones_groups(src_ref[0, :, LANES:2 * LANES].astype(F32))
        v_out[0, 0] = d0.astype(BF16)
        v_out[0, 1] = d1.astype(BF16)


def _nsa_prep(proj3, pos, w1, w2, gains):
    B, S, _ = proj3.shape
    n_c = S // CMP_STRIDE
    avg = jnp.asarray(_block_avg(LANES), BF16)
    col = lambda c: pl.BlockSpec((1, S, 2 * LANES), lambda b: (b, 0, c // (2 * LANES)))
    full = lambda a: pl.BlockSpec(a.shape, lambda b: (0,) * a.ndim)
    cmp_spec = pl.BlockSpec((1, NSA_GROUPS, n_c, LANES), lambda b: (b, 0, 0, 0))
    seq_spec = pl.BlockSpec((1, NSA_GROUPS, S, LANES), lambda b: (b, 0, 0, 0))
    cmp_shape = jax.ShapeDtypeStruct((B, NSA_GROUPS, n_c, LANES), BF16)
    seq_shape = jax.ShapeDtypeStruct((B, NSA_GROUPS, S, LANES), BF16)
    return pl.pallas_call(
        _nsa_prep_body,
        grid=(B,),
        in_specs=[col(COL_KVC), col(COL_KSV), col(COL_KWV), full(pos), full(w1), full(w2), full(gains), full(avg)],
        out_specs=[cmp_spec, cmp_spec, seq_spec, seq_spec, seq_spec, seq_spec],
        out_shape=[cmp_shape, cmp_shape, seq_shape, seq_shape, seq_shape, seq_shape],
        scratch_shapes=[pltpu.VMEM((S, LANES), F32), pltpu.VMEM((S, LANES), F32)],
        compiler_params=_cparams(1),
        name="nsa_prep",
    )(proj3, proj3, proj3, pos, w1, w2, gains, avg)


def _nsa_attn_body(q_ref, gate_ref, kcmp_ref, vcmp_ref, ks_ref, vs_ref, kw_ref, vw_ref,
                   qgain_ref, kgain_ref, ogain_ref, avgq_ref, avgo_ref, msct_ref, esel_ref, egate_ref, wbias_ref,
                   dbias_ref,
                   o_ref, m_scr, acc_scr):
    tq = q_ref.shape[1]
    n_cmp = kcmp_ref.shape[2]
    n_sel = msct_ref.shape[0]
    kc_len = esel_ref.shape[2]
    rows = 4 * tq
    qi = pl.program_id(1)
    q0 = qi * tq

    q = q_ref[0].astype(F32)
    ms = _seg_mean(q * q, avgq_ref[...])
    qn = q * lax.rsqrt(ms + EPS) * qgain_ref[...] * (HEAD_DIM ** -0.5)

    gate_sig = _sigmoid(gate_ref[0].astype(F32)).astype(BF16)
    gates = [_dot(gate_sig, egate_ref[j]) for j in range(3)]

    lane_q = lax.broadcasted_iota(jnp.int32, (tq, LANES), 1)
    lo_q = lane_q < HEAD_DIM
    lo_r = lax.broadcasted_iota(jnp.int32, (rows, LANES), 1) < HEAD_DIM

    blk = lax.broadcasted_iota(jnp.int32, (n_sel, tq), 0)
    cur = lax.shift_right_logical(q0 + lax.broadcasted_iota(jnp.int32, (n_sel, tq), 1), int(np.log2(SEL_BLOCK)))
    forced = (blk == 0) | (blk == cur) | (blk == cur - 1)
    future = blk > cur
    blk_f = blk.astype(F32)

    def heads4(x):
        return jnp.concatenate([x] * 4, axis=0)

    def normalised_pairs(acc, guard):
        rolled = pltpu.roll(acc, HEAD_DIM, axis=1)
        den = jnp.where(lo_r, rolled, acc)
        if guard:
            den = jnp.maximum(den, 1e-30)
        out = []
        for p in range(2):
            ev = slice((2 * p) * tq, (2 * p + 1) * tq)
            od = slice((2 * p + 1) * tq, (2 * p + 2) * tq)
            out.append(jnp.where(lo_q, acc[ev] / den[ev], rolled[od] / den[od]))
        return out

    groups = range(NSA_GROUPS)
    qs = []
    for g in groups:
        slabs = [qn[:, (2 * g + p) * LANES:(2 * g + p + 1) * LANES] for p in range(2)]
        qs.append(jnp.concatenate(
            [jnp.where(lo_q, slabs[0], 0.0), jnp.where(lo_q, 0.0, slabs[0]),
             jnp.where(lo_q, slabs[1], 0.0), jnp.where(lo_q, 0.0, slabs[1])], axis=0).astype(BF16))

    r_c = lax.broadcasted_iota(jnp.int32, (rows, n_cmp), 0)
    c_c = lax.broadcasted_iota(jnp.int32, (rows, n_cmp), 1)
    cmask = (c_c * CMP_STRIDE + (CMP_BLOCK - 1)) <= q0 + (r_c & (tq - 1))
    s_c = [jnp.where(cmask, _dot_nt(qs[g], kcmp_ref[0, g]), NEG_INF) for g in groups]
    e_c = [jnp.where(cmask, jnp.exp(s_c[g] - jnp.max(s_c[g], axis=-1, keepdims=True)), 0.0) for g in groups]
    acc_c = [_dot(e_c[g].astype(BF16), vcmp_ref[0, g]) for g in groups]
    l_c = [jnp.where(lo_r, pltpu.roll(acc_c[g], HEAD_DIM, axis=1), acc_c[g]) for g in groups]
    p_c = [e_c[g] / jnp.maximum(l_c[g], 1e-30) for g in groups]

    p_sum = [p_c[g][0:tq] + p_c[g][tq:2 * tq] + p_c[g][2 * tq:3 * tq] + p_c[g][3 * tq:4 * tq] for g in groups]
    p_hi = [p_sum[g].astype(BF16) for g in groups]
    p_lo = [(p_sum[g] - p_hi[g].astype(F32)).astype(BF16) for g in groups]
    imp = [_dot_nt(msct_ref[...], p_hi[g]) + _dot_nt(msct_ref[...], p_lo[g]) for g in groups]
    v = [jnp.where(forced, FORCE_SCORE, jnp.where(future, NEG_INF, imp[g])) for g in groups]
    sel = [jnp.zeros((n_sel, tq), F32) for g in groups]
    for _ in range(SEL_TOPK):
        mx = [jnp.max(v[g], axis=0, keepdims=True) for g in groups]
        first = [jnp.min(jnp.where(v[g] == mx[g], blk_f, float(LANES)), axis=0, keepdims=True) for g in groups]
        pick = [blk_f == first[g] for g in groups]
        sel = [jnp.where(pick[g], 1.0, sel[g]) for g in groups]
        v = [jnp.where(pick[g], BELOW_ALL, v[g]) for g in groups]
    sel_b = [sel[g].astype(BF16) for g in groups]

    cmp_s = [normalised_pairs(acc_c[g], True) for g in groups]
    n_before = lax.shift_right_logical(q0, int(np.log2(kc_len)))
    causal = dbias_ref[qi & (kc_len // tq - 1)]
    w0 = pl.multiple_of(jnp.maximum(q0 - WINDOW, 0), tq)
    n_w = WINDOW + tq
    w_case = jnp.minimum(qi, WINDOW // tq)

    def sel_keys(ref, g, kc):
        return ref[0, g, pl.ds(pl.multiple_of(kc * kc_len, kc_len), kc_len), :]

    def sel_scores(g, kc, causal_bias, shift):
        chosen = _dot_tn(sel_b[g], esel_ref[kc])
        bias = (chosen - 1.0) * (-NEG_INF)
        if causal_bias is not None:
            bias = bias + causal_bias
        if shift is not None:
            bias = bias - shift
        return _dot_nt(qs[g], sel_keys(ks_ref, g, kc)) + heads4(bias)

    def win_scores(g, shift):
        bias = wbias_ref[w_case] if shift is None else wbias_ref[w_case] - shift
        return _dot_nt(qs[g], kw_ref[0, g, pl.ds(w0, n_w), :]) + heads4(bias)

    def win_values(g):
        return vw_ref[0, g, pl.ds(w0, n_w), :]

    def finish(acc_w):
        for g in groups:
            sel_s = normalised_pairs(acc_scr[g], False)
            win_s = normalised_pairs(acc_w[g], False)
            for p in range(2):
                cols = slice((2 * g + p) * LANES, (2 * g + p + 1) * LANES)
                mix = gates[0][:, cols] * cmp_s[g][p] + gates[1][:, cols] * sel_s[p] + gates[2][:, cols] * win_s[p]
                ms_o = _seg_mean(mix * mix, avgo_ref[...])
                o_ref[0, :, cols] = (mix * lax.rsqrt(ms_o + EPS) * ogain_ref[:, cols]).astype(BF16)

    def fixed_shift_path(shift):
        def probs(s):
            return jnp.exp(s).astype(BF16)

        for g in groups:
            acc_scr[g] = jnp.zeros(acc_scr.shape[1:], F32)

        def before(kc, carry):
            s = [sel_scores(g, kc, None, shift) for g in groups]
            p = [probs(s[g]) for g in groups]
            for g in groups:
                acc_scr[g] = acc_scr[g] + _dot(p[g], sel_keys(vs_ref, g, kc))
            return carry

        lax.fori_loop(0, n_before, before, 0)
        s_d0 = sel_scores(0, n_before, causal, shift)
        s_d1 = sel_scores(1, n_before, causal, shift)
        p_d0 = probs(s_d0)
        s_w0 = win_scores(0, shift)
        acc_scr[0] = acc_scr[0] + _dot(p_d0, sel_keys(vs_ref, 0, n_before))
        p_d1 = probs(s_d1)
        s_w1 = win_scores(1, shift)
        acc_scr[1] = acc_scr[1] + _dot(p_d1, sel_keys(vs_ref, 1, n_before))
        p_w0 = probs(s_w0)
        acc_w0 = _dot(p_w0, win_values(0))
        p_w1 = probs(s_w1)
        acc_w1 = _dot(p_w1, win_values(1))
        finish([acc_w0, acc_w1])

    def online_path():
        for g in groups:
            m_scr[g] = jnp.full(m_scr.shape[1:], NEG_INF, F32)
            acc_scr[g] = jnp.zeros(acc_scr.shape[1:], F32)

        def sel_softmax(g, s):
            m_old = m_scr[g]
            m_new = jnp.maximum(m_old, jnp.max(s, axis=-1, keepdims=True))
            m_scr[g] = m_new
            return jnp.exp(s - m_new).astype(BF16), jnp.exp(m_old - m_new)

        def sel_accumulate(g, kc, p, alpha):
            acc_scr[g] = alpha * acc_scr[g] + _dot(p, sel_keys(vs_ref, g, kc))

        def win_softmax(s):
            return jnp.exp(s - jnp.max(s, axis=-1, keepdims=True)).astype(BF16)

        def before(kc, carry):
            s = [sel_scores(g, kc, None, None) for g in groups]
            pa = [sel_softmax(g, s[g]) for g in groups]
            for g in groups:
                sel_accumulate(g, kc, *pa[g])
            return carry

        lax.fori_loop(0, n_before, before, 0)
        s_d0 = sel_scores(0, n_before, causal, None)
        s_d1 = sel_scores(1, n_before, causal, None)
        pa0 = sel_softmax(0, s_d0)
        s_w0 = win_scores(0, None)
        sel_accumulate(0, n_before, *pa0)
        pa1 = sel_softmax(1, s_d1)
        s_w1 = win_scores(1, None)
        sel_accumulate(1, n_before, *pa1)
        acc_w0 = _dot(win_softmax(s_w0), win_values(0))
        acc_w1 = _dot(win_softmax(s_w1), win_values(1))
        finish([acc_w0, acc_w1])

    bound = 1.01 * (HEAD_DIM ** 0.5) * jnp.max(jnp.abs(qgain_ref[...])) * jnp.max(jnp.abs(kgain_ref[...]))
    safe = bound <= MAX_FIXED_SHIFT
    pl.when(safe)(lambda: fixed_shift_path(bound))
    pl.when(jnp.logical_not(safe))(online_path)


def _sel_from_cmp(n_cmp, n_sel):
    c0 = np.arange(n_cmp) * CMP_STRIDE
    s0 = np.arange(n_sel) * SEL_BLOCK
    ov = np.minimum(c0[None, :] + CMP_BLOCK, s0[:, None] + SEL_BLOCK) - np.maximum(c0[None, :], s0[:, None])
    m = (np.clip(ov, 0, None) / CMP_BLOCK).astype(np.float32)
    m[:, (np.arange(n_cmp) * CMP_STRIDE + CMP_BLOCK) > n_sel * SEL_BLOCK] = 0.0
    return m


def _nsa_attn(proj3, kcmp, vcmp, ks, vs, kw, vw, q_gain, k_gains, o_gain):
    B, S, _ = proj3.shape
    n_cmp = kcmp.shape[2]
    n_sel = S // SEL_BLOCK
    tq = NSA_TQ
    assert n_sel % 8 == 0 and S % SEL_KC == 0 and SEL_KC % tq == 0 and WINDOW % tq == 0 and S >= WINDOW + tq
    avgq = jnp.asarray(_block_avg(NSA_WIDTH), BF16)
    avgo = jnp.asarray(_block_avg(LANES), BF16)
    msct = jnp.asarray(_sel_from_cmp(n_cmp, n_sel), BF16)
    esel = (np.arange(n_sel)[:, None] == np.arange(S)[None, :] // SEL_BLOCK).astype(np.float32)
    esel = jnp.asarray(esel.reshape(n_sel, S // SEL_KC, SEL_KC).transpose(1, 0, 2), BF16)
    src = np.arange(LANES)[:, None]
    dst = np.arange(NSA_WIDTH)[None, :]
    egate = jnp.asarray(np.stack([(src == (dst // HEAD_DIM) * 3 + j) for j in range(3)]).astype(np.float32), BF16)
    r = np.arange(tq)[:, None]
    n_w = WINDOW + tq
    wcases = []
    for i in range(WINDOW // tq + 1):
        diff = (i * tq - max(i * tq - WINDOW, 0)) + r - np.arange(n_w)[None, :]
        wcases.append(np.where((diff >= 0) & (diff < WINDOW), 0.0, NEG_INF))
    wbias = jnp.asarray(np.stack(wcases), F32)
    dbias = jnp.asarray(np.stack([np.where(np.arange(SEL_KC)[None, :] <= i * tq + r, 0.0, NEG_INF)
                                  for i in range(SEL_KC // tq)]), F32)

    full = lambda a: pl.BlockSpec(a.shape, lambda b, i: (0,) * a.ndim)
    per_b = lambda a: pl.BlockSpec((1,) + a.shape[1:], lambda b, i: (b,) + (0,) * (a.ndim - 1))
    return pl.pallas_call(
        _nsa_attn_body,
        grid=(B, S // tq),
        in_specs=[pl.BlockSpec((1, tq, NSA_WIDTH), lambda b, i: (b, i, COL_QA // NSA_WIDTH)),
                  pl.BlockSpec((1, tq, LANES), lambda b, i: (b, i, COL_GATE // LANES)),
                  per_b(kcmp), per_b(vcmp), per_b(ks), per_b(vs), per_b(kw), per_b(vw),
                  full(q_gain), full(k_gains), full(o_gain), full(avgq), full(avgo), full(msct), full(esel), full(egate),
                  full(wbias), full(dbias)],
        out_specs=pl.BlockSpec((1, tq, NSA_WIDTH), lambda b, i: (b, i, 0)),
        out_shape=jax.ShapeDtypeStruct((B, S, NSA_WIDTH), BF16),
        scratch_shapes=[pltpu.VMEM((NSA_GROUPS, 4 * tq, 1), F32), pltpu.VMEM((NSA_GROUPS, 4 * tq, LANES), F32)],
        compiler_params=_cparams(2),
        name="nsa_attn",
    )(proj3, proj3, kcmp, vcmp, ks, vs, kw, vw, q_gain, k_gains, o_gain, avgq, avgo, msct, esel, egate, wbias, dbias)


def _retention_body(q_ref, k_ref, v_ref, g_ref, cos_ref, sin_ref, decay_ref, xi_ref, zeta_ref, gammac_ref,
                    gain_ref, avg_ref, o_ref, state_scr):
    S = q_ref.shape[1]
    C = RET_CHUNK
    lane = lax.broadcasted_iota(jnp.int32, (C, LANES), 1)
    lo = lane < HEAD_DIM
    first_half = (lane & (HEAD_DIM - 1)) < HEAD_DIM // 2
    r = lax.broadcasted_iota(jnp.int32, (LANES, LANES), 0)
    c = lax.broadcasted_iota(jnp.int32, (LANES, LANES), 1)
    same_head = (r < HEAD_DIM) == (c < HEAD_DIM)
    avg = avg_ref[...]
    state_scr[...] = jnp.zeros(state_scr.shape, F32)

    def rope(x, cos, sin):
        swapped = jnp.where(first_half, pltpu.roll(x, LANES - HEAD_DIM // 2, axis=1),
                            pltpu.roll(x, HEAD_DIM // 2, axis=1))
        return x * cos + swapped * sin

    def chunk(n, carry):
        r0 = pl.multiple_of(n * C, C)
        cos = cos_ref[pl.ds(r0, C), :]
        sin = sin_ref[pl.ds(r0, C), :]
        pairs = range(RET_HEADS // 2)
        cols = [slice(p * LANES, (p + 1) * LANES) for p in pairs]
        q = [rope(q_ref[0, pl.ds(r0, C), cols[p]].astype(F32), cos, sin) for p in pairs]
        k = [rope(k_ref[0, pl.ds(r0, C), cols[p]].astype(F32), cos, sin) * (HEAD_DIM ** -0.5) for p in pairs]
        vb = [v_ref[0, pl.ds(r0, C), cols[p]] for p in pairs]
        kb = [k[p].astype(BF16) for p in pairs]
        inner = [_dot_nt(jnp.where(lo if half == 0 else ~lo, q[p], 0.0).astype(BF16), kb[p])
                 * decay_ref[2 * p + half] for p in pairs for half in range(2)]
        state = [state_scr[p] for p in pairs]
        cross = [_dot(q[p].astype(BF16), state[p].astype(BF16)) * xi_ref[p] for p in pairs]
        upd = [_dot_tn((k[p] * zeta_ref[p]).astype(BF16), vb[p]) for p in pairs]
        outs = [_dot(inner[i].astype(BF16), vb[i // 2]) for i in range(RET_HEADS)]
        for p in pairs:
            state_scr[p] = gammac_ref[p] * state[p] + jnp.where(same_head, upd[p], 0.0)
        y = jnp.concatenate([jnp.where(lo, outs[2 * p], outs[2 * p + 1]) + cross[p] for p in pairs], axis=0)
        mu = _seg_mean(y, avg)
        d = y - mu
        var = _seg_mean(d * d, avg)
        yn = d * lax.rsqrt(var + EPS)
        for p in pairs:
            gate = g_ref[0, pl.ds(r0, C), cols[p]].astype(F32)
            o_ref[0, pl.ds(r0, C), cols[p]] = (_silu(gate) * (yn[p * C:(p + 1) * C] * gain_ref[:, cols[p]])).astype(BF16)
        return carry

    lax.fori_loop(0, S // C, chunk, 0)


def _retention_tables(S):
    half = HEAD_DIM // 2
    inv_freq = ROPE_BASE ** (-jnp.arange(half, dtype=F32) / half)
    ang = jnp.arange(S, dtype=F32)[:, None] * inv_freq[None, :]
    cos, sin = jnp.cos(ang), jnp.sin(ang)
    cos_t = jnp.tile(cos, (1, 4))
    sin_t = jnp.tile(jnp.concatenate([-sin, sin], axis=1), (1, 2))
    C = RET_CHUNK
    H = RET_HEADS
    log_gamma = jnp.log1p(-jnp.power(2.0, -5.0 - jnp.arange(H, dtype=F32)))
    i = jnp.arange(C, dtype=F32)
    rel = i[:, None] - i[None, :]
    decay = jnp.where(rel >= 0, jnp.exp(jnp.maximum(rel, 0.0)[None] * log_gamma[:, None, None]), 0.0)
    xi = jnp.exp((i + 1.0)[:, None] * log_gamma[None, :])
    zeta = jnp.exp((C - 1.0 - i)[:, None] * log_gamma[None, :])
    gamma_c = jnp.exp(C * log_gamma)
    per_pair = lambda t: jnp.repeat(t.T.reshape(H // 2, 2, -1), HEAD_DIM, axis=1).transpose(0, 2, 1)
    gammac = jnp.repeat(gamma_c.reshape(H // 2, 2), HEAD_DIM, axis=1)[:, None, :]
    return cos_t, sin_t, decay, per_pair(xi), per_pair(zeta), gammac


def _retention(proj3, gain):
    B, S, _ = proj3.shape
    cos_t, sin_t, decay, xi, zeta, gammac = _retention_tables(S)
    avg = jnp.asarray(_block_avg(LANES), BF16)
    col = lambda c: pl.BlockSpec((1, S, RET_WIDTH), lambda b: (b, 0, c // RET_WIDTH))
    full = lambda a: pl.BlockSpec(a.shape, lambda b: (0,) * a.ndim)
    return pl.pallas_call(
        _retention_body,
        grid=(B,),
        in_specs=[col(COL_QR), col(COL_KR), col(COL_VR), col(COL_GR), full(cos_t), full(sin_t), full(decay),
                  full(xi), full(zeta), full(gammac), full(gain), full(avg)],
        out_specs=pl.BlockSpec((1, S, RET_WIDTH), lambda b: (b, 0, 0)),
        out_shape=jax.ShapeDtypeStruct((B, S, RET_WIDTH), BF16),
        scratch_shapes=[pltpu.VMEM((RET_HEADS // 2, LANES, LANES), F32)],
        compiler_params=_cparams(1),
        name="retention",
    )(proj3, proj3, proj3, proj3, cos_t, sin_t, decay, xi, zeta, gammac, gain, avg)


def _mem_prep_body(mem_ref, g_ref, wkv_ref, kgain_ref, avg_ref, k_ref, v_ref):
    hm = _rms_full(mem_ref[0], g_ref[...]).astype(BF16)
    kv = _dot(hm, wkv_ref[...])
    k = kv[:, :MEM_WIDTH]
    ms = _seg_mean(k * k, avg_ref[...])
    k_ref[0] = (k * lax.rsqrt(ms + EPS) * kgain_ref[...]).astype(BF16)
    v_ref[0] = kv[:, MEM_WIDTH:].astype(BF16)


def _mem_prep(mem, g, wkv, kgain):
    B, M, _ = mem.shape
    avg = jnp.asarray(_block_avg(MEM_WIDTH), BF16)
    full = lambda a: pl.BlockSpec(a.shape, lambda b: (0,) * a.ndim)
    out_spec = pl.BlockSpec((1, M, MEM_WIDTH), lambda b: (b, 0, 0))
    out_shape = jax.ShapeDtypeStruct((B, M, MEM_WIDTH), BF16)
    return pl.pallas_call(
        _mem_prep_body,
        grid=(B,),
        in_specs=[pl.BlockSpec((1, M, D_MODEL), lambda b: (b, 0, 0)), full(g), full(wkv), full(kgain), full(avg)],
        out_specs=[out_spec, out_spec],
        out_shape=[out_shape, out_shape],
        compiler_params=_cparams(1),
        name="mem_prep",
    )(mem, g, wkv, kgain, avg)


def _post_body(x_ref, oa_ref, ob_ref, wout_ref, mk_ref, mv_ref, gx_ref, wq_ref, qgain_ref, avg_ref, wo_ref,
               gf_ref, wr_ref, br_ref, tri_ref, x2_ref, h_ref, route_ref, count_ref):
    tm = x_ref.shape[1] // POST_CHAINS
    chains = range(POST_CHAINS)
    rows = [slice(c * tm, (c + 1) * tm) for c in chains]
    x1 = [x_ref[0, rows[c]] + _dot(oa_ref[0, rows[c]], wout_ref[0:NSA_WIDTH, :])
          + _dot(ob_ref[0, rows[c]], wout_ref[NSA_WIDTH:, :]) for c in chains]

    h = [_rms_full(x1[c], gx_ref[...]).astype(BF16) for c in chains]
    q = [_dot(h[c], wq_ref[...]) for c in chains]
    ms = [_seg_mean(q[c] * q[c], avg_ref[...]) for c in chains]
    q = [q[c] * lax.rsqrt(ms[c] + EPS) * qgain_ref[...] * (HEAD_DIM ** -0.5) for c in chains]
    lane = lax.broadcasted_iota(jnp.int32, (tm, LANES), 1)
    lo = lane < HEAD_DIM
    heads = [(c, p, half) for c in chains for p in range(MEM_HEADS // 2) for half in range(2)]
    s = [_dot_nt(jnp.where(lo if half == 0 else ~lo, q[c][:, p * LANES:(p + 1) * LANES], 0.0).astype(BF16),
                 mk_ref[0, :, p * LANES:(p + 1) * LANES]) for c, p, half in heads]
    e = [jnp.exp(s[i] - jnp.max(s[i], axis=-1, keepdims=True)) for i in range(len(heads))]
    pr = [(e[i] / jnp.sum(e[i], axis=-1, keepdims=True)).astype(BF16) for i in range(len(heads))]
    outs = [_dot(pr[i], mv_ref[0, :, heads[i][1] * LANES:(heads[i][1] + 1) * LANES]) for i in range(len(heads))]
    per_chain = MEM_HEADS
    o = [jnp.concatenate([jnp.where(lo, outs[c * per_chain + 2 * p], outs[c * per_chain + 2 * p + 1])
                          for p in range(MEM_HEADS // 2)], axis=1).astype(BF16) for c in chains]
    x2 = [x1[c] + _dot(o[c], wo_ref[...]) for c in chains]
    for c in chains:
        x2_ref[0, rows[c]] = x2[c]

    hf = [_rms_full(x2[c], gf_ref[...]).astype(BF16) for c in chains]
    for c in chains:
        h_ref[0, rows[c]] = hf[c]
    logits = [_dot(hf[c], wr_ref[...]) + br_ref[...] for c in chains]
    lane_f = lane.astype(F32)
    big = float(LANES)
    picks = []
    for c in chains:
        gl = jnp.where(lane < N_GROUPS, logits[c], BELOW_ALL)
        gmax = jnp.max(gl, axis=-1, keepdims=True)
        grp = jnp.min(jnp.where(gl == gmax, lane_f, big), axis=-1, keepdims=True)
        g_w = 1.0 / jnp.sum(jnp.where(lane < N_GROUPS, jnp.exp(gl - gmax), 0.0), axis=-1, keepdims=True)
        e_lo = N_GROUPS + grp * EXPERTS_PER_GROUP
        el = jnp.where((lane_f >= e_lo) & (lane_f < e_lo + EXPERTS_PER_GROUP), logits[c], BELOW_ALL)
        v0 = jnp.max(el, axis=-1, keepdims=True)
        i0 = jnp.min(jnp.where(el == v0, lane_f, big), axis=-1, keepdims=True)
        el = jnp.where(lane_f == i0, BELOW_ALL, el)
        v1 = jnp.max(el, axis=-1, keepdims=True)
        i1 = jnp.min(jnp.where(el == v1, lane_f, big), axis=-1, keepdims=True)
        e1 = jnp.exp(v1 - v0)
        picks.append((i0 - N_GROUPS, i1 - N_GROUPS, g_w / (1.0 + e1), g_w * e1 / (1.0 + e1)))

    hot = [[lane_f == picks[c][s] for s in range(2)] for c in chains]
    both = jnp.concatenate([jnp.where(hot[c][0], 1.0, 0.0) + jnp.where(hot[c][1], 1.0, 0.0) for c in chains], axis=0)
    before = _dot(tri_ref[...], both.astype(BF16))
    count_ref[0] = jnp.broadcast_to(jnp.sum(both, axis=0, keepdims=True), count_ref.shape[1:])
    for c in chains:
        e0, e1, w0, w1 = picks[c]
        r0 = jnp.sum(jnp.where(hot[c][0], before[rows[c]], 0.0), axis=-1, keepdims=True)
        r1 = jnp.sum(jnp.where(hot[c][1], before[rows[c]], 0.0), axis=-1, keepdims=True)
        cols = (e0, e1, w0, w1, r0, r1)
        route = jnp.zeros((tm, LANES), F32)
        for k in range(len(cols)):
            route = jnp.where(lane == k, cols[k], route)
        route_ref[0, rows[c]] = route


def _post(x, oa, ob, wout, mk, mv, gx, wq, qgain, wo, gf, wr, br):
    B, S, _ = x.shape
    tm = MOE_TM
    n_s = S // tm
    avg = jnp.asarray(_block_avg(MEM_WIDTH), BF16)
    tri = jnp.asarray(np.tril(np.ones((tm, tm), np.float32), -1), BF16)
    full = lambda a: pl.BlockSpec(a.shape, lambda b, i: (0,) * a.ndim)
    per_b = lambda a: pl.BlockSpec((1,) + a.shape[1:], lambda b, i: (b,) + (0,) * (a.ndim - 1))
    tile = lambda w: pl.BlockSpec((1, tm, w), lambda b, i: (b, i, 0))
    return pl.pallas_call(
        _post_body,
        grid=(B, n_s),
        in_specs=[tile(D_MODEL), tile(NSA_WIDTH), tile(RET_WIDTH), full(wout), per_b(mk), per_b(mv), full(gx),
                  full(wq), full(qgain), full(avg), full(wo), full(gf), full(wr), full(br), full(tri)],
        out_specs=[tile(D_MODEL), tile(D_MODEL), tile(LANES),
                   pl.BlockSpec((1, 8, LANES), lambda b, i: (b * n_s + i, 0, 0))],
        out_shape=[jax.ShapeDtypeStruct((B, S, D_MODEL), F32), jax.ShapeDtypeStruct((B, S, D_MODEL), BF16),
                   jax.ShapeDtypeStruct((B, S, LANES), F32), jax.ShapeDtypeStruct((B * n_s, 8, LANES), F32)],
        compiler_params=_cparams(2),
        name="post_mixer",
    )(x, oa, ob, wout, mk, mv, gx, wq, qgain, avg, wo, gf, wr, br, tri)


def _row_copy(src, dst, sem):
    return pltpu.make_async_copy(src, dst, sem)


def _run_pieces(n, max_piece, fn):
    b = RUN_ALIGN
    while b <= max_piece:
        pl.when((n & b) != 0)(functools.partial(fn, n & (-2 * b), b))
        b *= 2


def _move_groups(i, gmap_ref, n_loc, copy):
    n_groups = n_loc // RUN_ALIGN
    for j in range(n_groups):
        glob = pl.multiple_of(gmap_ref[i * n_groups + j], RUN_ALIGN)
        copy(pl.ds(j * RUN_ALIGN, RUN_ALIGN), pl.ds(glob, RUN_ALIGN)).start()


def _local_positions(route, loff_row):
    lane = lax.broadcasted_iota(jnp.int32, route.shape, 1).astype(F32)
    pos = []
    for s in range(2):
        base = jnp.sum(jnp.where(lane == route[:, s:s + 1], loff_row, 0.0), axis=-1, keepdims=True)
        pos.append(base + route[:, 4 + s:5 + s])
    return pos


def _scatter_body(gmap_ref, tstart_ref, tlen_ref, nact_ref,
                  h_ref, route_ref, lofff_ref, xs_ref, xloc, zbuf, sems):
    i = pl.program_id(0)
    tm = h_ref.shape[0]
    n_loc = xloc.shape[1]
    slot = i & 1
    sem = sems.at[0]

    def tails(start):
        def per_expert(e, carry):
            n = tlen_ref[e]
            st = tstart_ref[e]

            def piece(off, size):
                c = _row_copy(zbuf.at[pl.ds(0, size)], xs_ref.at[pl.ds(pl.multiple_of(st + off, RUN_ALIGN), size)], sem)
                c.start() if start else c.wait()

            _run_pieces(n, MOE_RB // 2, piece)
            return carry

        lax.fori_loop(0, N_EXPERTS, per_expert, 0)

    def unused(start):
        rows = zbuf.shape[0]

        def per_unit(u, carry):
            c = _row_copy(zbuf, xs_ref.at[pl.ds(pl.multiple_of(u * rows, rows), rows)], sem)
            c.start() if start else c.wait()
            return carry

        lax.fori_loop(nact_ref[0] * (MOE_RB // rows), xs_ref.shape[0] // rows, per_unit, 0)

    @pl.when(i == 0)
    def _():
        zbuf[...] = jnp.zeros(zbuf.shape, zbuf.dtype)
        tails(True)
        unused(True)
        tails(False)
        unused(False)

    pos = _local_positions(route_ref[...], lofff_ref[0, 0:1, :])
    col = lax.broadcasted_iota(jnp.int32, (tm, n_loc), 1).astype(F32)
    perm_t = jnp.where((col == pos[0]) | (col == pos[1]), 1.0, 0.0).astype(BF16)
    xloc[slot] = _dot_tn(perm_t, h_ref[...]).astype(BF16)

    def wait_slot(s):
        _row_copy(xloc.at[s], xs_ref.at[pl.ds(0, n_loc)], sems.at[s]).wait()

    pl.when(i > 0)(lambda: wait_slot(1 - slot))
    _move_groups(i, gmap_ref, n_loc, lambda loc, glob: _row_copy(xloc.at[slot, loc], xs_ref.at[glob], sems.at[slot]))
    pl.when(i == pl.num_programs(0) - 1)(lambda: wait_slot(slot))


def _scatter_rows(tables, h2d, route2d, loff_f, n_rows):
    T = h2d.shape[0]
    tm = MOE_TM
    n_loc = MOE_NLOC
    tile = lambda w: pl.BlockSpec((tm, w), lambda i, *_: (i, 0))
    grid_spec = pltpu.PrefetchScalarGridSpec(
        num_scalar_prefetch=4,
        grid=(T // tm,),
        in_specs=[tile(D_MODEL), tile(LANES), pl.BlockSpec((1, 8, LANES), lambda i, *_: (i, 0, 0))],
        out_specs=pl.BlockSpec(memory_space=pl.ANY),
        scratch_shapes=[pltpu.VMEM((2, n_loc, D_MODEL), BF16), pltpu.VMEM((MOE_RB // 2, D_MODEL), BF16),
                        pltpu.SemaphoreType.DMA((2,))],
    )
    return pl.pallas_call(
        _scatter_body,
        grid_spec=grid_spec,
        out_shape=jax.ShapeDtypeStruct((n_rows, D_MODEL), BF16),
        compiler_params=_cparams(1),
        name="moe_scatter",
    )(*tables, h2d, route2d, loff_f)


def _expert_body(blk0_ref, nblk_ref, n_act_ref, xs_ref, wg_ref, wu_ref, wd_ref, ys_ref,
                 wg_b, wu_b, wd_b, xbuf, ybuf, sem_in, sem_out):
    e = pl.program_id(0)
    n_slots, rb = xbuf.shape[0], xbuf.shape[1]
    ahead = n_slots - 1
    n_act = n_act_ref[0]
    b0 = blk0_ref[e]

    def rows(g):
        return pl.ds(pl.multiple_of(g * rb, rb), rb)

    def x_copy(g, slot):
        return _row_copy(xs_ref.at[rows(g)], xbuf.at[slot], sem_in.at[slot])

    def y_copy(g, slot):
        return _row_copy(ybuf.at[slot], ys_ref.at[rows(g)], sem_out.at[slot])

    @pl.when(e == 0)
    def _():
        for k in range(ahead):
            pl.when(k < n_act)(lambda k=k: x_copy(k, k).start())

    wg_b[...] = wg_ref[0].astype(BF16)
    wu_b[...] = wu_ref[0].astype(BF16)
    wd_b[...] = wd_ref[0].astype(BF16)

    def block(j, carry):
        g = b0 + j
        slot = g & (n_slots - 1)
        x_copy(g, slot).wait()
        pl.when(g + ahead < n_act)(lambda: x_copy(g + ahead, (g + ahead) & (n_slots - 1)).start())
        x = xbuf[slot]
        a = _dot(x, wg_b[...])
        b = _dot(x, wu_b[...])
        y = _dot((_silu(a) * b).astype(BF16), wd_b[...]).astype(BF16)
        pl.when(g >= n_slots)(lambda: y_copy(g - n_slots, slot).wait())
        ybuf[slot] = y
        y_copy(g, slot).start()
        return carry

    lax.fori_loop(0, nblk_ref[e], block, 0)

    @pl.when(e == pl.num_programs(0) - 1)
    def _():
        for k in range(1, n_slots + 1):
            pl.when(n_act >= k)(lambda k=k: y_copy(n_act - k, (n_act - k) & (n_slots - 1)).wait())
        ybuf[0] = jnp.zeros(ybuf.shape[1:], ybuf.dtype)
        n_blocks = ys_ref.shape[0] // rb

        def fill(start):
            def per_block(g, carry):
                c = y_copy(g, 0)
                c.start() if start else c.wait()
                return carry

            lax.fori_loop(n_act, n_blocks, per_block, 0)

        fill(True)
        fill(False)


def _experts(blk0, nblk, n_act, xs, n_rows, wg, wu, wd):
    rb = MOE_RB
    weight = lambda a: pl.BlockSpec((1,) + a.shape[1:], lambda e, *_: (e, 0, 0))
    grid_spec = pltpu.PrefetchScalarGridSpec(
        num_scalar_prefetch=3,
        grid=(N_EXPERTS,),
        in_specs=[pl.BlockSpec(memory_space=pl.ANY), weight(wg), weight(wu), weight(wd)],
        out_specs=pl.BlockSpec(memory_space=pl.ANY),
        scratch_shapes=[pltpu.VMEM((D_MODEL, EXPERT_FF), BF16), pltpu.VMEM((D_MODEL, EXPERT_FF), BF16),
                        pltpu.VMEM((EXPERT_FF, D_MODEL), BF16), pltpu.VMEM((MOE_SLOTS, rb, D_MODEL), BF16),
                        pltpu.VMEM((MOE_SLOTS, rb, D_MODEL), BF16), pltpu.SemaphoreType.DMA((MOE_SLOTS,)),
                        pltpu.SemaphoreType.DMA((MOE_SLOTS,))],
    )
    return pl.pallas_call(
        _expert_body,
        grid_spec=grid_spec,
        out_shape=jax.ShapeDtypeStruct((n_rows, D_MODEL), BF16),
        compiler_params=_cparams(1),
        name="moe_experts",
    )(blk0, nblk, n_act, xs, wg, wu, wd)


def _combine_body(gmap_ref, x_ref, route_ref, lofff_ref, ys_ref, o_ref, yloc, sems):
    i = pl.program_id(0)
    tm = x_ref.shape[0]
    n_loc = yloc.shape[1]
    slot = i & 1

    def fetch(tile, s):
        _move_groups(tile, gmap_ref, n_loc, lambda loc, glob: _row_copy(ys_ref.at[glob], yloc.at[s, loc], sems.at[s]))

    pl.when(i == 0)(lambda: fetch(i, slot))
    _row_copy(ys_ref.at[pl.ds(0, n_loc)], yloc.at[slot], sems.at[slot]).wait()
    pl.when(i + 1 < pl.num_programs(0))(lambda: fetch(i + 1, 1 - slot))

    route = route_ref[...]
    pos = _local_positions(route, lofff_ref[0, 0:1, :])
    col = lax.broadcasted_iota(jnp.int32, (tm, n_loc), 1).astype(F32)
    perm_w = (jnp.where(col == pos[0], route[:, 2:3], 0.0) + jnp.where(col == pos[1], route[:, 3:4], 0.0)).astype(BF16)
    o_ref[...] = x_ref[...] + _dot(perm_w, yloc[slot])


def _combine(tables, x2d, route2d, loff_f, ys):
    T = x2d.shape[0]
    tm = MOE_TM
    n_loc = MOE_NLOC
    tile = lambda w: pl.BlockSpec((tm, w), lambda i, *_: (i, 0))
    grid_spec = pltpu.PrefetchScalarGridSpec(
        num_scalar_prefetch=1,
        grid=(T // tm,),
        in_specs=[tile(D_MODEL), tile(LANES), pl.BlockSpec((1, 8, LANES), lambda i, *_: (i, 0, 0)),
                  pl.BlockSpec(memory_space=pl.ANY)],
        out_specs=tile(D_MODEL),
        scratch_shapes=[pltpu.VMEM((2, n_loc, D_MODEL), BF16), pltpu.SemaphoreType.DMA((2,))],
    )
    return pl.pallas_call(
        _combine_body,
        grid_spec=grid_spec,
        out_shape=jax.ShapeDtypeStruct((T, D_MODEL), F32),
        compiler_params=_cparams(1),
        name="moe_combine",
    )(*tables, x2d, route2d, loff_f, ys)


def _permute_w_in(w):
    kv0 = NSA_WIDTH
    gate0 = kv0 + 6 * NSA_GROUPS * HEAD_DIM
    ret0 = gate0 + 3 * NSA_HEADS
    pad = jnp.zeros((w.shape[0], PROJ_PAD - w.shape[1]), w.dtype)
    return jnp.concatenate([w[:, :kv0], w[:, ret0:], w[:, kv0:gate0], w[:, gate0:ret0], pad], axis=1).astype(BF16)


def _compress_weights(pos, w1, w2):
    eye = jnp.eye(NSA_GROUPS, dtype=F32)
    w1r = w1.reshape(CMP_BLOCK, HEAD_DIM, CMP_HIDDEN)
    w1b = jnp.einsum('ldh,gk->lgdkh', w1r, eye).reshape(2, CMP_STRIDE * LANES, NSA_GROUPS * CMP_HIDDEN)
    w2b = jnp.einsum('hd,gk->ghkd', w2, eye).reshape(NSA_GROUPS * CMP_HIDDEN, LANES)
    posb = jnp.tile(pos, (1, NSA_GROUPS)).reshape(2, 1, CMP_STRIDE * LANES)
    return posb, w1b.astype(BF16), w2b.astype(BF16)


def _dup2(g):
    return jnp.tile(g.reshape(1, HEAD_DIM), (1, 2))


def kernel(x, mem, mix_norm, w_in, nsa_q_norm, nsa_kcmp_norm, nsa_ksel_norm, nsa_kwin_norm, cmp_pos_k, cmp_pos_v, cmp_k_w1, cmp_k_w2, cmp_v_w1, cmp_v_w2, nsa_out_norm, ret_out_norm, w_out, mem_x_norm, mem_kv_norm, mem_wq, mem_wkv, mem_q_norm, mem_k_norm, mem_wo, ffn_norm, router_group_w, router_group_b, router_expert_w, router_expert_b, exp_w_gate, exp_w_up, exp_w_down):
    B, S, D = x.shape
    T = B * S
    depth = mix_norm.shape[0]
    for l in range(depth):
        proj = _proj(x.reshape(T, D), mix_norm[l].reshape(1, D), _permute_w_in(w_in[l])).reshape(B, S, PROJ_PAD)
        pk, w1k, w2k = _compress_weights(cmp_pos_k[l], cmp_k_w1[l], cmp_k_w2[l])
        pv, w1v, w2v = _compress_weights(cmp_pos_v[l], cmp_v_w1[l], cmp_v_w2[l])
        gains = jnp.stack([_dup2(nsa_kcmp_norm[l]), _dup2(nsa_ksel_norm[l]), _dup2(nsa_kwin_norm[l])])
        kcmp, vcmp, ks, vs, kw, vw = _nsa_prep(proj, jnp.stack([pk, pv]), jnp.stack([w1k, w1v]),
                                               jnp.stack([w2k, w2v]), gains)
        o_a = _nsa_attn(proj, kcmp, vcmp, ks, vs, kw, vw,
                        jnp.tile(nsa_q_norm[l].reshape(1, HEAD_DIM), (1, NSA_HEADS)), gains[1:],
                        nsa_out_norm[l].reshape(1, NSA_WIDTH))
        o_b = _retention(proj, ret_out_norm[l].reshape(1, RET_WIDTH))
        mk, mv = _mem_prep(mem, mem_kv_norm[l].reshape(1, D), mem_wkv[l].astype(BF16),
                           jnp.tile(mem_k_norm[l].reshape(1, HEAD_DIM), (1, MEM_HEADS)))
        w_r = jnp.concatenate([router_group_w[l],
                               router_expert_w[l].transpose(1, 0, 2).reshape(D, N_EXPERTS),
                               jnp.zeros((D, LANES - N_GROUPS - N_EXPERTS), F32)], axis=1).astype(BF16)
        b_r = jnp.concatenate([router_group_b[l], router_expert_b[l].reshape(N_EXPERTS),
                               jnp.zeros((LANES - N_GROUPS - N_EXPERTS,), F32)]).reshape(1, LANES)
        x2, hf, route, counts = _post(
            x, o_a, o_b, w_out[l].astype(BF16), mk, mv, mem_x_norm[l].reshape(1, D), mem_wq[l].astype(BF16),
            jnp.tile(mem_q_norm[l].reshape(1, HEAD_DIM), (1, MEM_HEADS)), mem_wo[l].astype(BF16),
            ffn_norm[l].reshape(1, D), w_r, b_r)
        route2d = route.reshape(T, LANES)
        n_tiles = T // MOE_TM
        cnt = counts[:, 0, :N_EXPERTS].astype(jnp.int32)
        cnt = (cnt + RUN_ALIGN - 1) // RUN_ALIGN * RUN_ALIGN
        loff = jnp.cumsum(cnt, axis=1) - cnt
        total = jnp.sum(cnt, axis=0)
        padded = (total + MOE_RB - 1) // MOE_RB * MOE_RB
        pend = jnp.cumsum(padded)
        pstart = pend - padded
        goff = pstart[None, :] + jnp.cumsum(cnt, axis=0) - cnt
        n_rows = 2 * T + n_tiles * N_EXPERTS * RUN_ALIGN + N_EXPERTS * MOE_RB
        n_act = (pend[-1:] // MOE_RB).astype(jnp.int32)
        loff_f = jnp.broadcast_to(jnp.pad(loff.astype(F32), ((0, 0), (0, LANES - N_EXPERTS)))[:, None, :],
                                  (n_tiles, 8, LANES))
        grp_row = jnp.arange(MOE_NLOC // RUN_ALIGN, dtype=jnp.int32) * RUN_ALIGN
        inside = ((loff[:, None, :] <= grp_row[None, :, None])
                  & (grp_row[None, :, None] < (loff + cnt)[:, None, :])).astype(jnp.int32)
        shift = jnp.sum(inside * (goff - loff)[:, None, :], axis=2)
        used = jnp.sum(inside, axis=2) > 0
        gmap_scatter = jnp.where(used, shift + grp_row[None, :], n_rows + grp_row[None, :]).reshape(-1)
        gmap_gather = jnp.where(used, shift + grp_row[None, :], 0).reshape(-1)
        xs = _scatter_rows((gmap_scatter, pstart + total, padded - total, n_act), hf.reshape(T, D), route2d, loff_f,
                           n_rows + MOE_NLOC)
        ys = _experts(pstart // MOE_RB, padded // MOE_RB, n_act, xs, n_rows,
                      exp_w_gate[l], exp_w_up[l], exp_w_down[l])
        x = _combine((gmap_gather,), x2.reshape(T, D), route2d, loff_f, ys).reshape(B, S, D)
    return x
```

```python
import functools

import numpy as np
import jax
import jax.numpy as jnp
from jax import lax
from jax.experimental import pallas as pl
from jax.experimental.pallas import tpu as pltpu

F32 = jnp.float32
BF16 = jnp.bfloat16

D_MODEL = 1024
HEAD_DIM = 64
LANES = 128
NSA_HEADS = 8
NSA_GROUPS = 2
NSA_WIDTH = NSA_HEADS * HEAD_DIM
CMP_BLOCK = 32
CMP_STRIDE = 16
CMP_HIDDEN = 2 * HEAD_DIM
SEL_BLOCK = 64
SEL_TOPK = 8
WINDOW = 512
RET_HEADS = 8
RET_WIDTH = RET_HEADS * HEAD_DIM
RET_CHUNK = 128
ROPE_BASE = 10000.0
MEM_HEADS = 4
MEM_WIDTH = MEM_HEADS * HEAD_DIM
N_GROUPS = 4
EXPERTS_PER_GROUP = 8
N_EXPERTS = N_GROUPS * EXPERTS_PER_GROUP
EXPERT_FF = D_MODEL // 4
EPS = 1e-6
NEG_INF = -1e30
FORCE_SCORE = 1e9
BELOW_ALL = -3e38
MAX_FIXED_SHIFT = 40.0

COL_QA = 0
COL_QR, COL_KR, COL_VR, COL_GR = 512, 1024, 1536, 2048
COL_KVC, COL_KSV, COL_KWV = 2560, 2816, 3072
COL_GATE = 3328
PROJ_PAD = 3456

PROJ_TM = 512
NSA_TQ = 256
SEL_KC = 512
WIN_KEYS = WINDOW + NSA_TQ
POST_CHAINS = 2
MOE_TM = 512
MOE_RB = 256
MOE_SLOTS = 4
RUN_ALIGN = 16
MOE_NLOC = 2 * MOE_TM + N_EXPERTS * RUN_ALIGN
VMEM_LIMIT = 56 * 1024 * 1024


def _cparams(n_axes):
    return pltpu.CompilerParams(dimension_semantics=("arbitrary",) * n_axes,
                                vmem_limit_bytes=VMEM_LIMIT)


def _dot(a, b):
    return jnp.dot(a, b, preferred_element_type=F32)


def _dot_nt(a, b):
    return lax.dot_general(a, b, (((1,), (1,)), ((), ())), preferred_element_type=F32)


def _dot_tn(a, b):
    return lax.dot_general(a, b, (((0,), (0,)), ((), ())), preferred_element_type=F32)


def _rms_full(x, g):
    ms = jnp.mean(x * x, axis=-1, keepdims=True)
    return x * lax.rsqrt(ms + EPS) * g


def _seg_mean(x, avg):
    return _dot(x.astype(BF16), avg)


def _silu(x):
    return x * (1.0 / (1.0 + jnp.exp(-x)))


def _sigmoid(x):
    return 1.0 / (1.0 + jnp.exp(-x))


def _block_avg(width):
    i = np.arange(width)
    return ((i[:, None] // HEAD_DIM == i[None, :] // HEAD_DIM) / HEAD_DIM).astype(np.float32)


def _proj_body(x_ref, g_ref, w_ref, o_ref):
    h = _rms_full(x_ref[...], g_ref[...]).astype(BF16)
    step = PROJ_PAD // 3
    for j in range(3):
        o_ref[:, j * step:(j + 1) * step] = _dot(h, w_ref[:, j * step:(j + 1) * step]).astype(BF16)


def _proj(x2d, g, w):
    T = x2d.shape[0]
    return pl.pallas_call(
        _proj_body,
        grid=(T // PROJ_TM,),
        in_specs=[pl.BlockSpec((PROJ_TM, D_MODEL), lambda i: (i, 0)),
                  pl.BlockSpec((1, D_MODEL), lambda i: (0, 0)),
                  pl.BlockSpec((D_MODEL, PROJ_PAD), lambda i: (0, 0))],
        out_specs=pl.BlockSpec((PROJ_TM, PROJ_PAD), lambda i: (i, 0)),
        out_shape=jax.ShapeDtypeStruct((T, PROJ_PAD), BF16),
        compiler_params=_cparams(1),
        name="proj",
    )(x2d, g, w)


def _dup_groups(x):
    lane = lax.broadcasted_iota(jnp.int32, x.shape, 1)
    xs = pltpu.roll(x, HEAD_DIM, axis=1)
    lo = lane < HEAD_DIM
    return jnp.where(lo, x, xs), jnp.where(lo, xs, x)


def _ones_groups(x):
    lane = lax.broadcasted_iota(jnp.int32, x.shape, 1)
    lo = lane < HEAD_DIM
    return jnp.where(lo, x, 1.0), jnp.where(lo, pltpu.roll(x, HEAD_DIM, axis=1), 1.0)


def _nsa_prep_body(kvc_ref, ksv_ref, kwv_ref, pos_ref, w1_ref, w2_ref, gain_ref, avg_ref,
                   kcmp_ref, vcmp_ref, ks_ref, vs_ref, kw_ref, vw_ref, scr_k, scr_v):
    avg = avg_ref[...]
    n_c = scr_k.shape[0] // CMP_STRIDE
    scr_k[...] = kvc_ref[0, :, 0:LANES].astype(F32)
    scr_v[...] = kvc_ref[0, :, LANES:2 * LANES].astype(F32)
    for j, out_ref, scr in ((0, kcmp_ref, scr_k), (1, vcmp_ref, scr_v)):
        ycat = jnp.concatenate(
            [scr[pl.ds(l, n_c, stride=CMP_STRIDE), :] for l in range(CMP_STRIDE)], axis=1)
        first = _dot((ycat + pos_ref[j, 0]).astype(BF16), w1_ref[j, 0])
        second = _dot((ycat + pos_ref[j, 1]).astype(BF16), w1_ref[j, 1])
        hidden = first + pltpu.roll(second, n_c - 1, axis=0)
        cmp_tok = _dot(_silu(hidden).astype(BF16), w2_ref[j])
        if j == 0:
            ms = _seg_mean(cmp_tok * cmp_tok, avg)
            cmp_tok = cmp_tok * lax.rsqrt(ms + EPS) * gain_ref[0]
        d0, d1 = _dup_groups(cmp_tok) if j == 0 else _ones_groups(cmp_tok)
        out_ref[0, 0] = d0.astype(BF16)
        out_ref[0, 1] = d1.astype(BF16)

    for src_ref, k_out, v_out, gi in ((ksv_ref, ks_ref, vs_ref, 1), (kwv_ref, kw_ref, vw_ref, 2)):
        k = src_ref[0, :, 0:LANES].astype(F32)
        ms = _seg_mean(k * k, avg)
        k = k * lax.rsqrt(ms + EPS) * gain_ref[gi]
        d0, d1 = _dup_groups(k)
        k_out[0, 0] = d0.astype(BF16)
        k_out[0, 1] = d1.astype(BF16)
        d0, d1 = _ones_groups(src_ref[0, :, LANES:2 * LANES].astype(F32))
        v_out[0, 0] = d0.astype(BF16)
        v_out[0, 1] = d1.astype(BF16)


def _nsa_prep(proj3, pos, w1, w2, gains):
    B, S, _ = proj3.shape
    n_c = S // CMP_STRIDE
    avg = jnp.asarray(_block_avg(LANES), BF16)
    col = lambda c: pl.BlockSpec((1, S, 2 * LANES), lambda b: (b, 0, c // (2 * LANES)))
    full = lambda a: pl.BlockSpec(a.shape, lambda b: (0,) * a.ndim)
    cmp_spec = pl.BlockSpec((1, NSA_GROUPS, n_c, LANES), lambda b: (b, 0, 0, 0))
    seq_spec = pl.BlockSpec((1, NSA_GROUPS, S, LANES), lambda b: (b, 0, 0, 0))
    cmp_shape = jax.ShapeDtypeStruct((B, NSA_GROUPS, n_c, LANES), BF16)
    seq_shape = jax.ShapeDtypeStruct((B, NSA_GROUPS, S, LANES), BF16)
    return pl.pallas_call(
        _nsa_prep_body,
        grid=(B,),
        in_specs=[col(COL_KVC), col(COL_KSV), col(COL_KWV), full(pos), full(w1), full(w2), full(gains), full(avg)],
        out_specs=[cmp_spec, cmp_spec, seq_spec, seq_spec, seq_spec, seq_spec],
        out_shape=[cmp_shape, cmp_shape, seq_shape, seq_shape, seq_shape, seq_shape],
        scratch_shapes=[pltpu.VMEM((S, LANES), F32), pltpu.VMEM((S, LANES), F32)],
        compiler_params=_cparams(1),
        name="nsa_prep",
    )(proj3, proj3, proj3, pos, w1, w2, gains, avg)


def _nsa_attn_body(q_ref, gate_ref, kcmp_ref, vcmp_ref, ks_ref, vs_ref, kw_ref, vw_ref,
                   qgain_ref, kgain_ref, ogain_ref, avgq_ref, avgo_ref, msct_ref, esel_ref, egate_ref, wbias_ref,
                   dbias_ref,
                   o_ref, m_scr, acc_scr):
    tq = q_ref.shape[1]
    n_cmp = kcmp_ref.shape[2]
    n_sel = msct_ref.shape[0]
    kc_len = esel_ref.shape[2]
    rows = 4 * tq
    qi = pl.program_id(1)
    q0 = qi * tq

    q = q_ref[0].astype(F32)
    ms = _seg_mean(q * q, avgq_ref[...])
    qn = q * lax.rsqrt(ms + EPS) * qgain_ref[...] * (HEAD_DIM ** -0.5)

    gate_sig = _sigmoid(gate_ref[0].astype(F32)).astype(BF16)
    gates = [_dot(gate_sig, egate_ref[j]) for j in range(3)]

    lane_q = lax.broadcasted_iota(jnp.int32, (tq, LANES), 1)
    lo_q = lane_q < HEAD_DIM
    lo_r = lax.broadcasted_iota(jnp.int32, (rows, LANES), 1) < HEAD_DIM

    blk = lax.broadcasted_iota(jnp.int32, (n_sel, tq), 0)
    cur = lax.shift_right_logical(q0 + lax.broadcasted_iota(jnp.int32, (n_sel, tq), 1), int(np.log2(SEL_BLOCK)))
    forced = (blk == 0) | (blk == cur) | (blk == cur - 1)
    future = blk > cur
    blk_f = blk.astype(F32)

    def heads4(x):
        return jnp.concatenate([x] * 4, axis=0)

    def normalised_pairs(acc, guard):
        rolled = pltpu.roll(acc, HEAD_DIM, axis=1)
        den = jnp.where(lo_r, rolled, acc)
        if guard:
            den = jnp.maximum(den, 1e-30)
        out = []
        for p in range(2):
            ev = slice((2 * p) * tq, (2 * p + 1) * tq)
            od = slice((2 * p + 1) * tq, (2 * p + 2) * tq)
            out.append(jnp.where(lo_q, acc[ev] / den[ev], rolled[od] / den[od]))
        return out

    groups = range(NSA_GROUPS)
    qs = []
    for g in groups:
        slabs = [qn[:, (2 * g + p) * LANES:(2 * g + p + 1) * LANES] for p in range(2)]
        qs.append(jnp.concatenate(
            [jnp.where(lo_q, slabs[0], 0.0), jnp.where(lo_q, 0.0, slabs[0]),
             jnp.where(lo_q, slabs[1], 0.0), jnp.where(lo_q, 0.0, slabs[1])], axis=0).astype(BF16))

    r_c = lax.broadcasted_iota(jnp.int32, (rows, n_cmp), 0)
    c_c = lax.broadcasted_iota(jnp.int32, (rows, n_cmp), 1)
    cmask = (c_c * CMP_STRIDE + (CMP_BLOCK - 1)) <= q0 + (r_c & (tq - 1))
    s_c = [jnp.where(cmask, _dot_nt(qs[g], kcmp_ref[0, g]), NEG_INF) for g in groups]
    e_c = [jnp.where(cmask, jnp.exp(s_c[g] - jnp.max(s_c[g], axis=-1, keepdims=True)), 0.0) for g in groups]
    acc_c = [_dot(e_c[g].astype(BF16), vcmp_ref[0, g]) for g in groups]
    l_c = [jnp.where(lo_r, pltpu.roll(acc_c[g], HEAD_DIM, axis=1), acc_c[g]) for g in groups]
    p_c = [e_c[g] / jnp.maximum(l_c[g], 1e-30) for g in groups]

    p_sum = [p_c[g][0:tq] + p_c[g][tq:2 * tq] + p_c[g][2 * tq:3 * tq] + p_c[g][3 * tq:4 * tq] for g in groups]
    p_hi = [p_sum[g].astype(BF16) for g in groups]
    p_lo = [(p_sum[g] - p_hi[g].astype(F32)).astype(BF16) for g in groups]
    imp = [_dot_nt(msct_ref[...], p_hi[g]) + _dot_nt(msct_ref[...], p_lo[g]) for g in groups]
    v = [jnp.where(forced, FORCE_SCORE, jnp.where(future, NEG_INF, imp[g])) for g in groups]
    sel = [jnp.zeros((n_sel, tq), F32) for g in groups]
    for _ in range(SEL_TOPK):
        mx = [jnp.max(v[g], axis=0, keepdims=True) for g in groups]
        first = [jnp.min(jnp.where(v[g] == mx[g], blk_f, float(LANES)), axis=0, keepdims=True) for g in groups]
        pick = [blk_f == first[g] for g in groups]
        sel = [jnp.where(pick[g], 1.0, sel[g]) for g in groups]
        v = [jnp.where(pick[g], BELOW_ALL, v[g]) for g in groups]
    sel_b = [sel[g].astype(BF16) for g in groups]

    cmp_s = [normalised_pairs(acc_c[g], True) for g in groups]
    n_before = lax.shift_right_logical(q0, int(np.log2(kc_len)))
    causal = dbias_ref[qi & (kc_len // tq - 1)]
    w0 = pl.multiple_of(jnp.maximum(q0 - WINDOW, 0), tq)
    n_w = WINDOW + tq
    w_case = jnp.minimum(qi, WINDOW // tq)

    def sel_keys(ref, g, kc):
        return ref[0, g, pl.ds(pl.multiple_of(kc * kc_len, kc_len), kc_len), :]

    def sel_scores(g, kc, causal_bias, shift):
        chosen = _dot_tn(sel_b[g], esel_ref[kc])
        bias = (chosen - 1.0) * (-NEG_INF)
        if causal_bias is not None:
            bias = bias + causal_bias
        if shift is not None:
            bias = bias - shift
        return _dot_nt(qs[g], sel_keys(ks_ref, g, kc)) + heads4(bias)

    def win_scores(g, shift):
        bias = wbias_ref[w_case] if shift is None else wbias_ref[w_case] - shift
        return _dot_nt(qs[g], kw_ref[0, g, pl.ds(w0, n_w), :]) + heads4(bias)

    def win_values(g):
        return vw_ref[0, g, pl.ds(w0, n_w), :]

    def finish(acc_w):
        for g in groups:
            sel_s = normalised_pairs(acc_scr[g], False)
            win_s = normalised_pairs(acc_w[g], False)
            for p in range(2):
                cols = slice((2 * g + p) * LANES, (2 * g + p + 1) * LANES)
                mix = gates[0][:, cols] * cmp_s[g][p] + gates[1][:, cols] * sel_s[p] + gates[2][:, cols] * win_s[p]
                ms_o = _seg_mean(mix * mix, avgo_ref[...])
                o_ref[0, :, cols] = (mix * lax.rsqrt(ms_o + EPS) * ogain_ref[:, cols]).astype(BF16)

    def fixed_shift_path(shift):
        def probs(s):
            return jnp.exp(s).astype(BF16)

        for g in groups:
            acc_scr[g] = jnp.zeros(acc_scr.shape[1:], F32)

        def before(kc, carry):
            s = [sel_scores(g, kc, None, shift) for g in groups]
            p = [probs(s[g]) for g in groups]
            for g in groups:
                acc_scr[g] = acc_scr[g] + _dot(p[g], sel_keys(vs_ref, g, kc))
            return carry

        lax.fori_loop(0, n_before, before, 0)
        s_d0 = sel_scores(0, n_before, causal, shift)
        s_d1 = sel_scores(1, n_before, causal, shift)
        p_d0 = probs(s_d0)
        s_w0 = win_scores(0, shift)
        acc_scr[0] = acc_scr[0] + _dot(p_d0, sel_keys(vs_ref, 0, n_before))
        p_d1 = probs(s_d1)
        s_w1 = win_scores(1, shift)
        acc_scr[1] = acc_scr[1] + _dot(p_d1, sel_keys(vs_ref, 1, n_before))
        p_w0 = probs(s_w0)
        acc_w0 = _dot(p_w0, win_values(0))
        p_w1 = probs(s_w1)
        acc_w1 = _dot(p_w1, win_values(1))
        finish([acc_w0, acc_w1])

    def online_path():
        for g in groups:
            m_scr[g] = jnp.full(m_scr.shape[1:], NEG_INF, F32)
            acc_scr[g] = jnp.zeros(acc_scr.shape[1:], F32)

        def sel_softmax(g, s):
            m_old = m_scr[g]
            m_new = jnp.maximum(m_old, jnp.max(s, axis=-1, keepdims=True))
            m_scr[g] = m_new
            return jnp.exp(s - m_new).astype(BF16), jnp.exp(m_old - m_new)

        def sel_accumulate(g, kc, p, alpha):
            acc_scr[g] = alpha * acc_scr[g] + _dot(p, sel_keys(vs_ref, g, kc))

        def win_softmax(s):
            return jnp.exp(s - jnp.max(s, axis=-1, keepdims=True)).astype(BF16)

        def before(kc, carry):
            s = [sel_scores(g, kc, None, None) for g in groups]
            pa = [sel_softmax(g, s[g]) for g in groups]
            for g in groups:
                sel_accumulate(g, kc, *pa[g])
            return carry

        lax.fori_loop(0, n_before, before, 0)
        s_d0 = sel_scores(0, n_before, causal, None)
        s_d1 = sel_scores(1, n_before, causal, None)
        pa0 = sel_softmax(0, s_d0)
        s_w0 = win_scores(0, None)
        sel_accumulate(0, n_before, *pa0)
        pa1 = sel_softmax(1, s_d1)
        s_w1 = win_scores(1, None)
        sel_accumulate(1, n_before, *pa1)
        acc_w0 = _dot(win_softmax(s_w0), win_values(0))
        acc_w1 = _dot(win_softmax(s_w1), win_values(1))
        finish([acc_w0, acc_w1])

    bound = 1.01 * (HEAD_DIM ** 0.5) * jnp.max(jnp.abs(qgain_ref[...])) * jnp.max(jnp.abs(kgain_ref[...]))
    safe = bound <= MAX_FIXED_SHIFT
    pl.when(safe)(lambda: fixed_shift_path(bound))
    pl.when(jnp.logical_not(safe))(online_path)


def _sel_from_cmp(n_cmp, n_sel):
    c0 = np.arange(n_cmp) * CMP_STRIDE
    s0 = np.arange(n_sel) * SEL_BLOCK
    ov = np.minimum(c0[None, :] + CMP_BLOCK, s0[:, None] + SEL_BLOCK) - np.maximum(c0[None, :], s0[:, None])
    m = (np.clip(ov, 0, None) / CMP_BLOCK).astype(np.float32)
    m[:, (np.arange(n_cmp) * CMP_STRIDE + CMP_BLOCK) > n_sel * SEL_BLOCK] = 0.0
    return m


def _nsa_attn(proj3, kcmp, vcmp, ks, vs, kw, vw, q_gain, k_gains, o_gain):
    B, S, _ = proj3.shape
    n_cmp = kcmp.shape[2]
    n_sel = S // SEL_BLOCK
    tq = NSA_TQ
    assert n_sel % 8 == 0 and S % SEL_KC == 0 and SEL_KC % tq == 0 and WINDOW % tq == 0 and S >= WINDOW + tq
    avgq = jnp.asarray(_block_avg(NSA_WIDTH), BF16)
    avgo = jnp.asarray(_block_avg(LANES), BF16)
    msct = jnp.asarray(_sel_from_cmp(n_cmp, n_sel), BF16)
    esel = (np.arange(n_sel)[:, None] == np.arange(S)[None, :] // SEL_BLOCK).astype(np.float32)
    esel = jnp.asarray(esel.reshape(n_sel, S // SEL_KC, SEL_KC).transpose(1, 0, 2), BF16)
    src = np.arange(LANES)[:, None]
    dst = np.arange(NSA_WIDTH)[None, :]
    egate = jnp.asarray(np.stack([(src == (dst // HEAD_DIM) * 3 + j) for j in range(3)]).astype(np.float32), BF16)
    r = np.arange(tq)[:, None]
    n_w = WINDOW + tq
    wcases = []
    for i in range(WINDOW // tq + 1):
        diff = (i * tq - max(i * tq - WINDOW, 0)) + r - np.arange(n_w)[None, :]
        wcases.append(np.where((diff >= 0) & (diff < WINDOW), 0.0, NEG_INF))
    wbias = jnp.asarray(np.stack(wcases), F32)
    dbias = jnp.asarray(np.stack([np.where(np.arange(SEL_KC)[None, :] <= i * tq + r, 0.0, NEG_INF)
                                  for i in range(SEL_KC // tq)]), F32)

    full = lambda a: pl.BlockSpec(a.shape, lambda b, i: (0,) * a.ndim)
    per_b = lambda a: pl.BlockSpec((1,) + a.shape[1:], lambda b, i: (b,) + (0,) * (a.ndim - 1))
    return pl.pallas_call(
        _nsa_attn_body,
        grid=(B, S // tq),
        in_specs=[pl.BlockSpec((1, tq, NSA_WIDTH), lambda b, i: (b, i, COL_QA // NSA_WIDTH)),
                  pl.BlockSpec((1, tq, LANES), lambda b, i: (b, i, COL_GATE // LANES)),
                  per_b(kcmp), per_b(vcmp), per_b(ks), per_b(vs), per_b(kw), per_b(vw),
                  full(q_gain), full(k_gains), full(o_gain), full(avgq), full(avgo), full(msct), full(esel), full(egate),
                  full(wbias), full(dbias)],
        out_specs=pl.BlockSpec((1, tq, NSA_WIDTH), lambda b, i: (b, i, 0)),
        out_shape=jax.ShapeDtypeStruct((B, S, NSA_WIDTH), BF16),
        scratch_shapes=[pltpu.VMEM((NSA_GROUPS, 4 * tq, 1), F32), pltpu.VMEM((NSA_GROUPS, 4 * tq, LANES), F32)],
        compiler_params=_cparams(2),
        name="nsa_attn",
    )(proj3, proj3, kcmp, vcmp, ks, vs, kw, vw, q_gain, k_gains, o_gain, avgq, avgo, msct, esel, egate, wbias, dbias)


def _retention_body(q_ref, k_ref, v_ref, g_ref, cos_ref, sin_ref, decay_ref, xi_ref, zeta_ref, gammac_ref,
                    gain_ref, avg_ref, o_ref, state_scr):
    S = q_ref.shape[1]
    C = RET_CHUNK
    lane = lax.broadcasted_iota(jnp.int32, (C, LANES), 1)
    lo = lane < HEAD_DIM
    first_half = (lane & (HEAD_DIM - 1)) < HEAD_DIM // 2
    r = lax.broadcasted_iota(jnp.int32, (LANES, LANES), 0)
    c = lax.broadcasted_iota(jnp.int32, (LANES, LANES), 1)
    same_head = (r < HEAD_DIM) == (c < HEAD_DIM)
    avg = avg_ref[...]
    state_scr[...] = jnp.zeros(state_scr.shape, F32)

    def rope(x, cos, sin):
        swapped = jnp.where(first_half, pltpu.roll(x, LANES - HEAD_DIM // 2, axis=1),
                            pltpu.roll(x, HEAD_DIM // 2, axis=1))
        return x * cos + swapped * sin

    def chunk(n, carry):
        r0 = pl.multiple_of(n * C, C)
        cos = cos_ref[pl.ds(r0, C), :]
        sin = sin_ref[pl.ds(r0, C), :]
        pairs = range(RET_HEADS // 2)
        cols = [slice(p * LANES, (p + 1) * LANES) for p in pairs]
        q = [rope(q_ref[0, pl.ds(r0, C), cols[p]].astype(F32), cos, sin) for p in pairs]
        k = [rope(k_ref[0, pl.ds(r0, C), cols[p]].astype(F32), cos, sin) * (HEAD_DIM ** -0.5) for p in pairs]
        vb = [v_ref[0, pl.ds(r0, C), cols[p]] for p in pairs]
        kb = [k[p].astype(BF16) for p in pairs]
        inner = [_dot_nt(jnp.where(lo if half == 0 else ~lo, q[p], 0.0).astype(BF16), kb[p])
                 * decay_ref[2 * p + half] for p in pairs for half in range(2)]
        state = [state_scr[p] for p in pairs]
        cross = [_dot(q[p].astype(BF16), state[p].astype(BF16)) * xi_ref[p] for p in pairs]
        upd = [_dot_tn((k[p] * zeta_ref[p]).astype(BF16), vb[p]) for p in pairs]
        outs = [_dot(inner[i].astype(BF16), vb[i // 2]) for i in range(RET_HEADS)]
        for p in pairs:
            state_scr[p] = gammac_ref[p] * state[p] + jnp.where(same_head, upd[p], 0.0)
        y = jnp.concatenate([jnp.where(lo, outs[2 * p], outs[2 * p + 1]) + cross[p] for p in pairs], axis=0)
        mu = _seg_mean(y, avg)
        d = y - mu
        var = _seg_mean(d * d, avg)
        yn = d * lax.rsqrt(var + EPS)
        for p in pairs:
            gate = g_ref[0, pl.ds(r0, C), cols[p]].astype(F32)
            o_ref[0, pl.ds(r0, C), cols[p]] = (_silu(gate) * (yn[p * C:(p + 1) * C] * gain_ref[:, cols[p]])).astype(BF16)
        return carry

    lax.fori_loop(0, S // C, chunk, 0)


def _retention_tables(S):
    half = HEAD_DIM // 2
    inv_freq = ROPE_BASE ** (-jnp.arange(half, dtype=F32) / half)
    ang = jnp.arange(S, dtype=F32)[:, None] * inv_freq[None, :]
    cos, sin = jnp.cos(ang), jnp.sin(ang)
    cos_t = jnp.tile(cos, (1, 4))
    sin_t = jnp.tile(jnp.concatenate([-sin, sin], axis=1), (1, 2))
    C = RET_CHUNK
    H = RET_HEADS
    log_gamma = jnp.log1p(-jnp.power(2.0, -5.0 - jnp.arange(H, dtype=F32)))
    i = jnp.arange(C, dtype=F32)
    rel = i[:, None] - i[None, :]
    decay = jnp.where(rel >= 0, jnp.exp(jnp.maximum(rel, 0.0)[None] * log_gamma[:, None, None]), 0.0)
    xi = jnp.exp((i + 1.0)[:, None] * log_gamma[None, :])
    zeta = jnp.exp((C - 1.0 - i)[:, None] * log_gamma[None, :])
    gamma_c = jnp.exp(C * log_gamma)
    per_pair = lambda t: jnp.repeat(t.T.reshape(H // 2, 2, -1), HEAD_DIM, axis=1).transpose(0, 2, 1)
    gammac = jnp.repeat(gamma_c.reshape(H // 2, 2), HEAD_DIM, axis=1)[:, None, :]
    return cos_t, sin_t, decay, per_pair(xi), per_pair(zeta), gammac


def _retention(proj3, gain):
    B, S, _ = proj3.shape
    cos_t, sin_t, decay, xi, zeta, gammac = _retention_tables(S)
    avg = jnp.asarray(_block_avg(LANES), BF16)
    col = lambda c: pl.BlockSpec((1, S, RET_WIDTH), lambda b: (b, 0, c // RET_WIDTH))
    full = lambda a: pl.BlockSpec(a.shape, lambda b: (0,) * a.ndim)
    return pl.pallas_call(
        _retention_body,
        grid=(B,),
        in_specs=[col(COL_QR), col(COL_KR), col(COL_VR), col(COL_GR), full(cos_t), full(sin_t), full(decay),
                  full(xi), full(zeta), full(gammac), full(gain), full(avg)],
        out_specs=pl.BlockSpec((1, S, RET_WIDTH), lambda b: (b, 0, 0)),
        out_shape=jax.ShapeDtypeStruct((B, S, RET_WIDTH), BF16),
        scratch_shapes=[pltpu.VMEM((RET_HEADS // 2, LANES, LANES), F32)],
        compiler_params=_cparams(1),
        name="retention",
    )(proj3, proj3, proj3, proj3, cos_t, sin_t, decay, xi, zeta, gammac, gain, avg)


def _mem_prep_body(mem_ref, g_ref, wkv_ref, kgain_ref, avg_ref, k_ref, v_ref):
    hm = _rms_full(mem_ref[0], g_ref[...]).astype(BF16)
    kv = _dot(hm, wkv_ref[...])
    k = kv[:, :MEM_WIDTH]
    ms = _seg_mean(k * k, avg_ref[...])
    k_ref[0] = (k * lax.rsqrt(ms + EPS) * kgain_ref[...]).astype(BF16)
    v_ref[0] = kv[:, MEM_WIDTH:].astype(BF16)


def _mem_prep(mem, g, wkv, kgain):
    B, M, _ = mem.shape
    avg = jnp.asarray(_block_avg(MEM_WIDTH), BF16)
    full = lambda a: pl.BlockSpec(a.shape, lambda b: (0,) * a.ndim)
    out_spec = pl.BlockSpec((1, M, MEM_WIDTH), lambda b: (b, 0, 0))
    out_shape = jax.ShapeDtypeStruct((B, M, MEM_WIDTH), BF16)
    return pl.pallas_call(
        _mem_prep_body,
        grid=(B,),
        in_specs=[pl.BlockSpec((1, M, D_MODEL), lambda b: (b, 0, 0)), full(g), full(wkv), full(kgain), full(avg)],
        out_specs=[out_spec, out_spec],
        out_shape=[out_shape, out_shape],
        compiler_params=_cparams(1),
        name="mem_prep",
    )(mem, g, wkv, kgain, avg)


def _post_body(x_ref, oa_ref, ob_ref, wout_ref, mk_ref, mv_ref, gx_ref, wq_ref, qgain_ref, avg_ref, wo_ref,
               gf_ref, wr_ref, br_ref, tri_ref, x2_ref, h_ref, route_ref, count_ref):
    tm = x_ref.shape[1] // POST_CHAINS
    chains = range(POST_CHAINS)
    rows = [slice(c * tm, (c + 1) * tm) for c in chains]
    x1 = [x_ref[0, rows[c]] + _dot(oa_ref[0, rows[c]], wout_ref[0:NSA_WIDTH, :])
          + _dot(ob_ref[0, rows[c]], wout_ref[NSA_WIDTH:, :]) for c in chains]

    h = [_rms_full(x1[c], gx_ref[...]).astype(BF16) for c in chains]
    q = [_dot(h[c], wq_ref[...]) for c in chains]
    ms = [_seg_mean(q[c] * q[c], avg_ref[...]) for c in chains]
    q = [q[c] * lax.rsqrt(ms[c] + EPS) * qgain_ref[...] * (HEAD_DIM ** -0.5) for c in chains]
    lane = lax.broadcasted_iota(jnp.int32, (tm, LANES), 1)
    lo = lane < HEAD_DIM
    heads = [(c, p, half) for c in chains for p in range(MEM_HEADS // 2) for half in range(2)]
    s = [_dot_nt(jnp.where(lo if half == 0 else ~lo, q[c][:, p * LANES:(p + 1) * LANES], 0.0).astype(BF16),
                 mk_ref[0, :, p * LANES:(p + 1) * LANES]) for c, p, half in heads]
    e = [jnp.exp(s[i] - jnp.max(s[i], axis=-1, keepdims=True)) for i in range(len(heads))]
    pr = [(e[i] / jnp.sum(e[i], axis=-1, keepdims=True)).astype(BF16) for i in range(len(heads))]
    outs = [_dot(pr[i], mv_ref[0, :, heads[i][1] * LANES:(heads[i][1] + 1) * LANES]) for i in range(len(heads))]
    per_chain = MEM_HEADS
    o = [jnp.concatenate([jnp.where(lo, outs[c * per_chain + 2 * p], outs[c * per_chain + 2 * p + 1])
                          for p in range(MEM_HEADS // 2)], axis=1).astype(BF16) for c in chains]
    x2 = [x1[c] + _dot(o[c], wo_ref[...]) for c in chains]
    for c in chains:
        x2_ref[0, rows[c]] = x2[c]

    hf = [_rms_full(x2[c], gf_ref[...]).astype(BF16) for c in chains]
    for c in chains:
        h_ref[0, rows[c]] = hf[c]
    logits = [_dot(hf[c], wr_ref[...]) + br_ref[...] for c in chains]
    lane_f = lane.astype(F32)
    big = float(LANES)
    picks = []
    for c in chains:
        gl = jnp.where(lane < N_GROUPS, logits[c], BELOW_ALL)
        gmax = jnp.max(gl, axis=-1, keepdims=True)
        grp = jnp.min(jnp.where(gl == gmax, lane_f, big), axis=-1, keepdims=True)
        g_w = 1.0 / jnp.sum(jnp.where(lane < N_GROUPS, jnp.exp(gl - gmax), 0.0), axis=-1, keepdims=True)
        e_lo = N_GROUPS + grp * EXPERTS_PER_GROUP
        el = jnp.where((lane_f >= e_lo) & (lane_f < e_lo + EXPERTS_PER_GROUP), logits[c], BELOW_ALL)
        v0 = jnp.max(el, axis=-1, keepdims=True)
        i0 = jnp.min(jnp.where(el == v0, lane_f, big), axis=-1, keepdims=True)
        el = jnp.where(lane_f == i0, BELOW_ALL, el)
        v1 = jnp.max(el, axis=-1, keepdims=True)
        i1 = jnp.min(jnp.where(el == v1, lane_f, big), axis=-1, keepdims=True)
        e1 = jnp.exp(v1 - v0)
        picks.append((i0 - N_GROUPS, i1 - N_GROUPS, g_w / (1.0 + e1), g_w * e1 / (1.0 + e1)))

    hot = [[lane_f == picks[c][s] for s in range(2)] for c in chains]
    both = jnp.concatenate([jnp.where(hot[c][0], 1.0, 0.0) + jnp.where(hot[c][1], 1.0, 0.0) for c in chains], axis=0)
    before = _dot(tri_ref[...], both.astype(BF16))
    count_ref[0] = jnp.broadcast_to(jnp.sum(both, axis=0, keepdims=True), count_ref.shape[1:])
    for c in chains:
        e0, e1, w0, w1 = picks[c]
        r0 = jnp.sum(jnp.where(hot[c][0], before[rows[c]], 0.0), axis=-1, keepdims=True)
        r1 = jnp.sum(jnp.where(hot[c][1], before[rows[c]], 0.0), axis=-1, keepdims=True)
        cols = (e0, e1, w0, w1, r0, r1)
        route = jnp.zeros((tm, LANES), F32)
        for k in range(len(cols)):
            route = jnp.where(lane == k, cols[k], route)
        route_ref[0, rows[c]] = route


def _post(x, oa, ob, wout, mk, mv, gx, wq, qgain, wo, gf, wr, br):
    B, S, _ = x.shape
    tm = MOE_TM
    n_s = S // tm
    avg = jnp.asarray(_block_avg(MEM_WIDTH), BF16)
    tri = jnp.asarray(np.tril(np.ones((tm, tm), np.float32), -1), BF16)
    full = lambda a: pl.BlockSpec(a.shape, lambda b, i: (0,) * a.ndim)
    per_b = lambda a: pl.BlockSpec((1,) + a.shape[1:], lambda b, i: (b,) + (0,) * (a.ndim - 1))
    tile = lambda w: pl.BlockSpec((1, tm, w), lambda b, i: (b, i, 0))
    return pl.pallas_call(
        _post_body,
        grid=(B, n_s),
        in_specs=[tile(D_MODEL), tile(NSA_WIDTH), tile(RET_WIDTH), full(wout), per_b(mk), per_b(mv), full(gx),
                  full(wq), full(qgain), full(avg), full(wo), full(gf), full(wr), full(br), full(tri)],
        out_specs=[tile(D_MODEL), tile(D_MODEL), tile(LANES),
                   pl.BlockSpec((1, 8, LANES), lambda b, i: (b * n_s + i, 0, 0))],
        out_shape=[jax.ShapeDtypeStruct((B, S, D_MODEL), F32), jax.ShapeDtypeStruct((B, S, D_MODEL), BF16),
                   jax.ShapeDtypeStruct((B, S, LANES), F32), jax.ShapeDtypeStruct((B * n_s, 8, LANES), F32)],
        compiler_params=_cparams(2),
        name="post_mixer",
    )(x, oa, ob, wout, mk, mv, gx, wq, qgain, avg, wo, gf, wr, br, tri)


def _row_copy(src, dst, sem):
    return pltpu.make_async_copy(src, dst, sem)


def _run_pieces(n, max_piece, fn):
    b = RUN_ALIGN
    while b <= max_piece:
        pl.when((n & b) != 0)(functools.partial(fn, n & (-2 * b), b))
        b *= 2


def _move_groups(i, gmap_ref, n_loc, copy):
    n_groups = n_loc // RUN_ALIGN
    for j in range(n_groups):
        glob = pl.multiple_of(gmap_ref[i * n_groups + j], RUN_ALIGN)
        copy(pl.ds(j * RUN_ALIGN, RUN_ALIGN), pl.ds(glob, RUN_ALIGN)).start()


def _local_positions(route, loff_row):
    lane = lax.broadcasted_iota(jnp.int32, route.shape, 1).astype(F32)
    pos = []
    for s in range(2):
        base = jnp.sum(jnp.where(lane == route[:, s:s + 1], loff_row, 0.0), axis=-1, keepdims=True)
        pos.append(base + route[:, 4 + s:5 + s])
    return pos


def _scatter_body(gmap_ref, tstart_ref, tlen_ref, nact_ref,
                  h_ref, route_ref, lofff_ref, route_nx_ref, lofff_nx_ref, xs_ref, xloc, perm_scr, zbuf, sems):
    i = pl.program_id(0)
    tm = h_ref.shape[0]
    n_loc = xloc.shape[1]
    slot = i & 1
    sem = sems.at[0]

    def tails(start):
        def per_expert(e, carry):
            n = tlen_ref[e]
            st = tstart_ref[e]

            def piece(off, size):
                c = _row_copy(zbuf.at[pl.ds(0, size)], xs_ref.at[pl.ds(pl.multiple_of(st + off, RUN_ALIGN), size)], sem)
                c.start() if start else c.wait()

            _run_pieces(n, MOE_RB // 2, piece)
            return carry

        lax.fori_loop(0, N_EXPERTS, per_expert, 0)

    def unused(start):
        rows = zbuf.shape[0]

        def per_unit(u, carry):
            c = _row_copy(zbuf, xs_ref.at[pl.ds(pl.multiple_of(u * rows, rows), rows)], sem)
            c.start() if start else c.wait()
            return carry

        lax.fori_loop(nact_ref[0] * (MOE_RB // rows), xs_ref.shape[0] // rows, per_unit, 0)

    def sort_perm(route, loff_row):
        pos = _local_positions(route, loff_row)
        col = lax.broadcasted_iota(jnp.int32, (tm, n_loc), 1).astype(F32)
        return jnp.where((col == pos[0]) | (col == pos[1]), 1.0, 0.0).astype(BF16)

    def wait_slot(s):
        _row_copy(xloc.at[s], xs_ref.at[pl.ds(0, n_loc)], sems.at[s]).wait()

    @pl.when(i == 0)
    def _():
        zbuf[...] = jnp.zeros(zbuf.shape, zbuf.dtype)
        tails(True)
        unused(True)
        tails(False)
        unused(False)
        xloc[1] = jnp.zeros(xloc.shape[1:], xloc.dtype)
        perm_scr[0] = sort_perm(route_ref[...], lofff_ref[0, 0:1, :])

    pl.when(i > 0)(lambda: wait_slot(slot))

    xloc[slot] = _dot_tn(perm_scr[slot], h_ref[...]).astype(BF16)
    perm_scr[1 - slot] = sort_perm(route_nx_ref[...], lofff_nx_ref[0, 0:1, :])
    _move_groups(i, gmap_ref, n_loc,
                 lambda loc, glob: _row_copy(xloc.at[1 - slot, loc], xs_ref.at[glob], sems.at[1 - slot]))
    pl.when(i == pl.num_programs(0) - 1)(lambda: wait_slot(1 - slot))


def _scatter_rows(tables, h2d, route2d, loff_f, n_rows):
    T = h2d.shape[0]
    tm = MOE_TM
    n_loc = MOE_NLOC
    last = T // tm - 1
    cur = lambda w: pl.BlockSpec((tm, w), lambda i, *_: (jnp.minimum(i, last), 0))
    nxt = lambda w: pl.BlockSpec((tm, w), lambda i, *_: (jnp.minimum(i + 1, last), 0))
    grid_spec = pltpu.PrefetchScalarGridSpec(
        num_scalar_prefetch=4,
        grid=(T // tm + 1,),
        in_specs=[cur(D_MODEL), cur(LANES), pl.BlockSpec((1, 8, LANES), lambda i, *_: (jnp.minimum(i, last), 0, 0)),
                  nxt(LANES), pl.BlockSpec((1, 8, LANES), lambda i, *_: (jnp.minimum(i + 1, last), 0, 0))],
        out_specs=pl.BlockSpec(memory_space=pl.ANY),
        scratch_shapes=[pltpu.VMEM((2, n_loc, D_MODEL), BF16), pltpu.VMEM((2, tm, n_loc), BF16),
                        pltpu.VMEM((MOE_RB // 2, D_MODEL), BF16), pltpu.SemaphoreType.DMA((2,))],
    )
    return pl.pallas_call(
        _scatter_body,
        grid_spec=grid_spec,
        out_shape=jax.ShapeDtypeStruct((n_rows, D_MODEL), BF16),
        compiler_params=_cparams(1),
        name="moe_scatter",
    )(*tables, h2d, route2d, loff_f, route2d, loff_f)


def _expert_body(blk0_ref, nblk_ref, n_act_ref, xs_ref, wg_ref, wu_ref, wd_ref, ys_ref,
                 wg_b, wu_b, wd_b, xbuf, ybuf, sem_in, sem_out):
    e = pl.program_id(0)
    n_slots, rb = xbuf.shape[0], xbuf.shape[1]
    ahead = n_slots - 1
    n_act = n_act_ref[0]
    b0 = blk0_ref[e]

    def rows(g):
        return pl.ds(pl.multiple_of(g * rb, rb), rb)

    def x_copy(g, slot):
        return _row_copy(xs_ref.at[rows(g)], xbuf.at[slot], sem_in.at[slot])

    def y_copy(g, slot):
        return _row_copy(ybuf.at[slot], ys_ref.at[rows(g)], sem_out.at[slot])

    @pl.when(e == 0)
    def _():
        for k in range(ahead):
            pl.when(k < n_act)(lambda k=k: x_copy(k, k).start())

    wg_b[...] = wg_ref[0].astype(BF16)
    wu_b[...] = wu_ref[0].astype(BF16)
    wd_b[...] = wd_ref[0].astype(BF16)

    def block(j, carry):
        g = b0 + j
        slot = g & (n_slots - 1)
        x_copy(g, slot).wait()
        pl.when(g + ahead < n_act)(lambda: x_copy(g + ahead, (g + ahead) & (n_slots - 1)).start())
        x = xbuf[slot]
        a = _dot(x, wg_b[...])
        b = _dot(x, wu_b[...])
        y = _dot((_silu(a) * b).astype(BF16), wd_b[...]).astype(BF16)
        pl.when(g >= n_slots)(lambda: y_copy(g - n_slots, slot).wait())
        ybuf[slot] = y
        y_copy(g, slot).start()
        return carry

    lax.fori_loop(0, nblk_ref[e], block, 0)

    @pl.when(e == pl.num_programs(0) - 1)
    def _():
        for k in range(1, n_slots + 1):
            pl.when(n_act >= k)(lambda k=k: y_copy(n_act - k, (n_act - k) & (n_slots - 1)).wait())
        ybuf[0] = jnp.zeros(ybuf.shape[1:], ybuf.dtype)
        n_blocks = ys_ref.shape[0] // rb

        def fill(start):
            def per_block(g, carry):
                c = y_copy(g, 0)
                c.start() if start else c.wait()
                return carry

            lax.fori_loop(n_act, n_blocks, per_block, 0)

        fill(True)
        fill(False)


def _experts(blk0, nblk, n_act, xs, n_rows, wg, wu, wd):
    rb = MOE_RB
    weight = lambda a: pl.BlockSpec((1,) + a.shape[1:], lambda e, *_: (e, 0, 0))
    grid_spec = pltpu.PrefetchScalarGridSpec(
        num_scalar_prefetch=3,
        grid=(N_EXPERTS,),
        in_specs=[pl.BlockSpec(memory_space=pl.ANY), weight(wg), weight(wu), weight(wd)],
        out_specs=pl.BlockSpec(memory_space=pl.ANY),
        scratch_shapes=[pltpu.VMEM((D_MODEL, EXPERT_FF), BF16), pltpu.VMEM((D_MODEL, EXPERT_FF), BF16),
                        pltpu.VMEM((EXPERT_FF, D_MODEL), BF16), pltpu.VMEM((MOE_SLOTS, rb, D_MODEL), BF16),
                        pltpu.VMEM((MOE_SLOTS, rb, D_MODEL), BF16), pltpu.SemaphoreType.DMA((MOE_SLOTS,)),
                        pltpu.SemaphoreType.DMA((MOE_SLOTS,))],
    )
    return pl.pallas_call(
        _expert_body,
        grid_spec=grid_spec,
        out_shape=jax.ShapeDtypeStruct((n_rows, D_MODEL), BF16),
        compiler_params=_cparams(1),
        name="moe_experts",
    )(blk0, nblk, n_act, xs, wg, wu, wd)


def _combine_body(gmap_ref, x_ref, route_ref, lofff_ref, route_nx_ref, lofff_nx_ref, ys_ref, o_ref,
                  yloc, perm_scr, sems):
    i = pl.program_id(0)
    last = pl.num_programs(0) - 1
    tm = x_ref.shape[0]
    n_loc = yloc.shape[1]
    slot = i & 1

    def fetch(tile, s):
        _move_groups(tile, gmap_ref, n_loc, lambda loc, glob: _row_copy(ys_ref.at[glob], yloc.at[s, loc], sems.at[s]))

    def wait_slot(s):
        _row_copy(ys_ref.at[pl.ds(0, n_loc)], yloc.at[s], sems.at[s]).wait()

    def unsort_perm(route, loff_row):
        pos = _local_positions(route, loff_row)
        col = lax.broadcasted_iota(jnp.int32, (tm, n_loc), 1).astype(F32)
        return (jnp.where(col == pos[0], route[:, 2:3], 0.0) + jnp.where(col == pos[1], route[:, 3:4], 0.0)).astype(BF16)

    @pl.when(i == 0)
    def _():
        fetch(0, 0)
        perm_scr[0] = unsort_perm(route_ref[...], lofff_ref[0, 0:1, :])

    wait_slot(slot)
    fetch(jnp.minimum(i + 1, last), 1 - slot)
    o_ref[...] = x_ref[...] + _dot(perm_scr[slot], yloc[slot])
    perm_scr[1 - slot] = unsort_perm(route_nx_ref[...], lofff_nx_ref[0, 0:1, :])
    pl.when(i == last)(lambda: wait_slot(1 - slot))


def _combine(tables, x2d, route2d, loff_f, ys):
    T = x2d.shape[0]
    tm = MOE_TM
    n_loc = MOE_NLOC
    last = T // tm - 1
    tile = lambda w: pl.BlockSpec((tm, w), lambda i, *_: (i, 0))
    nxt = lambda w: pl.BlockSpec((tm, w), lambda i, *_: (jnp.minimum(i + 1, last), 0))
    grid_spec = pltpu.PrefetchScalarGridSpec(
        num_scalar_prefetch=1,
        grid=(T // tm,),
        in_specs=[tile(D_MODEL), tile(LANES), pl.BlockSpec((1, 8, LANES), lambda i, *_: (i, 0, 0)),
                  nxt(LANES), pl.BlockSpec((1, 8, LANES), lambda i, *_: (jnp.minimum(i + 1, last), 0, 0)),
                  pl.BlockSpec(memory_space=pl.ANY)],
        out_specs=tile(D_MODEL),
        scratch_shapes=[pltpu.VMEM((2, n_loc, D_MODEL), BF16), pltpu.VMEM((2, tm, n_loc), BF16),
                        pltpu.SemaphoreType.DMA((2,))],
    )
    return pl.pallas_call(
        _combine_body,
        grid_spec=grid_spec,
        out_shape=jax.ShapeDtypeStruct((T, D_MODEL), F32),
        compiler_params=_cparams(1),
        name="moe_combine",
    )(*tables, x2d, route2d, loff_f, route2d, loff_f, ys)


def _permute_w_in(w):
    kv0 = NSA_WIDTH
    gate0 = kv0 + 6 * NSA_GROUPS * HEAD_DIM
    ret0 = gate0 + 3 * NSA_HEADS
    pad = jnp.zeros((w.shape[0], PROJ_PAD - w.shape[1]), w.dtype)
    return jnp.concatenate([w[:, :kv0], w[:, ret0:], w[:, kv0:gate0], w[:, gate0:ret0], pad], axis=1).astype(BF16)


def _compress_weights(pos, w1, w2):
    eye = jnp.eye(NSA_GROUPS, dtype=F32)
    w1r = w1.reshape(CMP_BLOCK, HEAD_DIM, CMP_HIDDEN)
    w1b = jnp.einsum('ldh,gk->lgdkh', w1r, eye).reshape(2, CMP_STRIDE * LANES, NSA_GROUPS * CMP_HIDDEN)
    w2b = jnp.einsum('hd,gk->ghkd', w2, eye).reshape(NSA_GROUPS * CMP_HIDDEN, LANES)
    posb = jnp.tile(pos, (1, NSA_GROUPS)).reshape(2, 1, CMP_STRIDE * LANES)
    return posb, w1b.astype(BF16), w2b.astype(BF16)


def _dup2(g):
    return jnp.tile(g.reshape(1, HEAD_DIM), (1, 2))


def kernel(x, mem, mix_norm, w_in, nsa_q_norm, nsa_kcmp_norm, nsa_ksel_norm, nsa_kwin_norm, cmp_pos_k, cmp_pos_v, cmp_k_w1, cmp_k_w2, cmp_v_w1, cmp_v_w2, nsa_out_norm, ret_out_norm, w_out, mem_x_norm, mem_kv_norm, mem_wq, mem_wkv, mem_q_norm, mem_k_norm, mem_wo, ffn_norm, router_group_w, router_group_b, router_expert_w, router_expert_b, exp_w_gate, exp_w_up, exp_w_down):
    B, S, D = x.shape
    T = B * S
    depth = mix_norm.shape[0]
    for l in range(depth):
        proj = _proj(x.reshape(T, D), mix_norm[l].reshape(1, D), _permute_w_in(w_in[l])).reshape(B, S, PROJ_PAD)
        pk, w1k, w2k = _compress_weights(cmp_pos_k[l], cmp_k_w1[l], cmp_k_w2[l])
        pv, w1v, w2v = _compress_weights(cmp_pos_v[l], cmp_v_w1[l], cmp_v_w2[l])
        gains = jnp.stack([_dup2(nsa_kcmp_norm[l]), _dup2(nsa_ksel_norm[l]), _dup2(nsa_kwin_norm[l])])
        kcmp, vcmp, ks, vs, kw, vw = _nsa_prep(proj, jnp.stack([pk, pv]), jnp.stack([w1k, w1v]),
                                               jnp.stack([w2k, w2v]), gains)
        o_a = _nsa_attn(proj, kcmp, vcmp, ks, vs, kw, vw,
                        jnp.tile(nsa_q_norm[l].reshape(1, HEAD_DIM), (1, NSA_HEADS)), gains[1:],
                        nsa_out_norm[l].reshape(1, NSA_WIDTH))
        o_b = _retention(proj, ret_out_norm[l].reshape(1, RET_WIDTH))
        mk, mv = _mem_prep(mem, mem_kv_norm[l].reshape(1, D), mem_wkv[l].astype(BF16),
                           jnp.tile(mem_k_norm[l].reshape(1, HEAD_DIM), (1, MEM_HEADS)))
        w_r = jnp.concatenate([router_group_w[l],
                               router_expert_w[l].transpose(1, 0, 2).reshape(D, N_EXPERTS),
                               jnp.zeros((D, LANES - N_GROUPS - N_EXPERTS), F32)], axis=1).astype(BF16)
        b_r = jnp.concatenate([router_group_b[l], router_expert_b[l].reshape(N_EXPERTS),
                               jnp.zeros((LANES - N_GROUPS - N_EXPERTS,), F32)]).reshape(1, LANES)
        x2, hf, route, counts = _post(
            x, o_a, o_b, w_out[l].astype(BF16), mk, mv, mem_x_norm[l].reshape(1, D), mem_wq[l].astype(BF16),
            jnp.tile(mem_q_norm[l].reshape(1, HEAD_DIM), (1, MEM_HEADS)), mem_wo[l].astype(BF16),
            ffn_norm[l].reshape(1, D), w_r, b_r)
        route2d = route.reshape(T, LANES)
        n_tiles = T // MOE_TM
        cnt = counts[:, 0, :N_EXPERTS].astype(jnp.int32)
        cnt = (cnt + RUN_ALIGN - 1) // RUN_ALIGN * RUN_ALIGN
        loff = jnp.cumsum(cnt, axis=1) - cnt
        total = jnp.sum(cnt, axis=0)
        padded = (total + MOE_RB - 1) // MOE_RB * MOE_RB
        pend = jnp.cumsum(padded)
        pstart = pend - padded
        goff = pstart[None, :] + jnp.cumsum(cnt, axis=0) - cnt
        n_rows = 2 * T + n_tiles * N_EXPERTS * RUN_ALIGN + N_EXPERTS * MOE_RB
        n_act = (pend[-1:] // MOE_RB).astype(jnp.int32)
        loff_f = jnp.broadcast_to(jnp.pad(loff.astype(F32), ((0, 0), (0, LANES - N_EXPERTS)))[:, None, :],
                                  (n_tiles, 8, LANES))
        grp_row = jnp.arange(MOE_NLOC // RUN_ALIGN, dtype=jnp.int32) * RUN_ALIGN
        inside = ((loff[:, None, :] <= grp_row[None, :, None])
                  & (grp_row[None, :, None] < (loff + cnt)[:, None, :])).astype(jnp.int32)
        shift = jnp.sum(inside * (goff - loff)[:, None, :], axis=2)
        used = jnp.sum(inside, axis=2) > 0
        spare = n_rows + grp_row[None, :]
        gmap_scatter = jnp.concatenate([spare, jnp.where(used, shift + grp_row[None, :], spare)], axis=0).reshape(-1)
        gmap_gather = jnp.where(used, shift + grp_row[None, :], 0).reshape(-1)
        xs = _scatter_rows((gmap_scatter, pstart + total, padded - total, n_act), hf.reshape(T, D), route2d, loff_f,
                           n_rows + MOE_NLOC)
        ys = _experts(pstart // MOE_RB, padded // MOE_RB, n_act, xs, n_rows,
                      exp_w_gate[l], exp_w_up[l], exp_w_down[l])
        x = _combine((gmap_gather,), x2.reshape(T, D), route2d, loff_f, ys).reshape(B, S, D)
    return x
```

```python
import functools

import numpy as np
import jax
import jax.numpy as jnp
from jax import lax
from jax.experimental import pallas as pl
from jax.experimental.pallas import tpu as pltpu

F32 = jnp.float32
BF16 = jnp.bfloat16

D_MODEL = 1024
HEAD_DIM = 64
LANES = 128
NSA_HEADS = 8
NSA_GROUPS = 2
NSA_WIDTH = NSA_HEADS * HEAD_DIM
CMP_BLOCK = 32
CMP_STRIDE = 16
CMP_HIDDEN = 2 * HEAD_DIM
SEL_BLOCK = 64
SEL_TOPK = 8
WINDOW = 512
RET_HEADS = 8
RET_WIDTH = RET_HEADS * HEAD_DIM
RET_CHUNK = 128
ROPE_BASE = 10000.0
MEM_HEADS = 4
MEM_WIDTH = MEM_HEADS * HEAD_DIM
N_GROUPS = 4
EXPERTS_PER_GROUP = 8
N_EXPERTS = N_GROUPS * EXPERTS_PER_GROUP
EXPERT_FF = D_MODEL // 4
EPS = 1e-6
NEG_INF = -1e30
FORCE_SCORE = 1e9
BELOW_ALL = -3e38
MAX_FIXED_SHIFT = 40.0

COL_QA = 0
COL_QR, COL_KR, COL_VR, COL_GR = 512, 1024, 1536, 2048
COL_KVC, COL_KSV, COL_KWV = 2560, 2816, 3072
COL_GATE = 3328
PROJ_PAD = 3456
PROJ_SPLITS = (NSA_WIDTH, NSA_WIDTH + 6 * NSA_GROUPS * HEAD_DIM, NSA_WIDTH + 6 * NSA_GROUPS * HEAD_DIM + 3 * NSA_HEADS)

PROJ_TM = 512
NSA_TQ = 256
SEL_KC = 512
WIN_KEYS = WINDOW + NSA_TQ
POST_CHAINS = 2
MOE_TM = 512
MOE_RB = 256
MOE_SLOTS = 8
RUN_ALIGN = 16
MOE_NLOC = 2 * MOE_TM + N_EXPERTS * RUN_ALIGN
VMEM_LIMIT = 56 * 1024 * 1024


def _cparams(n_axes):
    return pltpu.CompilerParams(dimension_semantics=("arbitrary",) * n_axes,
                                vmem_limit_bytes=VMEM_LIMIT)


def _dot(a, b):
    return jnp.dot(a, b, preferred_element_type=F32)


def _dot_nt(a, b):
    return lax.dot_general(a, b, (((1,), (1,)), ((), ())), preferred_element_type=F32)


def _dot_tn(a, b):
    return lax.dot_general(a, b, (((0,), (0,)), ((), ())), preferred_element_type=F32)


def _rms_full(x, g):
    ms = jnp.mean(x * x, axis=-1, keepdims=True)
    return x * lax.rsqrt(ms + EPS) * g


def _seg_mean(x, avg):
    return _dot(x.astype(BF16), avg)


def _silu(x):
    return x * (1.0 / (1.0 + jnp.exp(-x)))


def _sigmoid(x):
    return 1.0 / (1.0 + jnp.exp(-x))


def _block_avg(width):
    i = np.arange(width)
    return ((i[:, None] // HEAD_DIM == i[None, :] // HEAD_DIM) / HEAD_DIM).astype(np.float32)


def _proj_body(x_ref, g_ref, wt_ref, o_ref, w_scr):
    @pl.when(pl.program_id(0) == 0)
    def _():
        kv0, gate0, ret0 = PROJ_SPLITS
        w_scr[COL_QA:COL_QR] = wt_ref[0:kv0].astype(BF16)
        w_scr[COL_QR:COL_KVC] = wt_ref[ret0:].astype(BF16)
        w_scr[COL_KVC:COL_GATE] = wt_ref[kv0:gate0].astype(BF16)
        pad = jnp.zeros((PROJ_PAD - COL_GATE - (ret0 - gate0), wt_ref.shape[1]), F32)
        w_scr[COL_GATE:] = jnp.concatenate([wt_ref[gate0:ret0], pad], axis=0).astype(BF16)

    h = _rms_full(x_ref[...], g_ref[...]).astype(BF16)
    step = PROJ_PAD // 3
    for j in range(3):
        o_ref[:, j * step:(j + 1) * step] = _dot_nt(h, w_scr[j * step:(j + 1) * step]).astype(BF16)


def _proj(x2d, g, w_t):
    T = x2d.shape[0]
    assert w_t.shape[0] == PROJ_SPLITS[2] + 4 * RET_WIDTH
    return pl.pallas_call(
        _proj_body,
        grid=(T // PROJ_TM,),
        in_specs=[pl.BlockSpec((PROJ_TM, D_MODEL), lambda i: (i, 0)),
                  pl.BlockSpec((1, D_MODEL), lambda i: (0, 0)),
                  pl.BlockSpec(w_t.shape, lambda i: (0, 0))],
        out_specs=pl.BlockSpec((PROJ_TM, PROJ_PAD), lambda i: (i, 0)),
        out_shape=jax.ShapeDtypeStruct((T, PROJ_PAD), BF16),
        scratch_shapes=[pltpu.VMEM((PROJ_PAD, D_MODEL), BF16)],
        compiler_params=_cparams(1),
        name="proj",
    )(x2d, g, w_t)


def _dup_groups(x):
    lane = lax.broadcasted_iota(jnp.int32, x.shape, 1)
    xs = pltpu.roll(x, HEAD_DIM, axis=1)
    lo = lane < HEAD_DIM
    return jnp.where(lo, x, xs), jnp.where(lo, xs, x)


def _ones_groups(x):
    lane = lax.broadcasted_iota(jnp.int32, x.shape, 1)
    lo = lane < HEAD_DIM
    return jnp.where(lo, x, 1.0), jnp.where(lo, pltpu.roll(x, HEAD_DIM, axis=1), 1.0)


def _nsa_prep_body(kvc_ref, ksv_ref, kwv_ref, pos_ref, w1_ref, w2_ref, gain_ref, avg_ref,
                   kcmp_ref, vcmp_ref, ks_ref, vs_ref, kw_ref, vw_ref, scr_k, scr_v):
    avg = avg_ref[...]
    n_c = scr_k.shape[0] // CMP_STRIDE
    scr_k[...] = kvc_ref[0, :, 0:LANES].astype(F32)
    scr_v[...] = kvc_ref[0, :, LANES:2 * LANES].astype(F32)
    for j, out_ref, scr in ((0, kcmp_ref, scr_k), (1, vcmp_ref, scr_v)):
        ycat = jnp.concatenate(
            [scr[pl.ds(l, n_c, stride=CMP_STRIDE), :] for l in range(CMP_STRIDE)], axis=1)
        first = _dot((ycat + pos_ref[j, 0]).astype(BF16), w1_ref[j, 0])
        second = _dot((ycat + pos_ref[j, 1]).astype(BF16), w1_ref[j, 1])
        hidden = first + pltpu.roll(second, n_c - 1, axis=0)
        cmp_tok = _dot(_silu(hidden).astype(BF16), w2_ref[j])
        if j == 0:
            ms = _seg_mean(cmp_tok * cmp_tok, avg)
            cmp_tok = cmp_tok * lax.rsqrt(ms + EPS) * gain_ref[0]
        d0, d1 = _dup_groups(cmp_tok) if j == 0 else _ones_groups(cmp_tok)
        out_ref[0, 0] = d0.astype(BF16)
        out_ref[0, 1] = d1.astype(BF16)

    for src_ref, k_out, v_out, gi in ((ksv_ref, ks_ref, vs_ref, 1), (kwv_ref, kw_ref, vw_ref, 2)):
        k = src_ref[0, :, 0:LANES].astype(F32)
        ms = _seg_mean(k * k, avg)
        k = k * lax.rsqrt(ms + EPS) * gain_ref[gi]
        d0, d1 = _dup_groups(k)
        k_out[0, 0] = d0.astype(BF16)
        k_out[0, 1] = d1.astype(BF16)
        d0, d1 = _ones_groups(src_ref[0, :, LANES:2 * LANES].astype(F32))
        v_out[0, 0] = d0.astype(BF16)
        v_out[0, 1] = d1.astype(BF16)


def _nsa_prep(proj3, pos, w1, w2, gains):
    B, S, _ = proj3.shape
    n_c = S // CMP_STRIDE
    avg = jnp.asarray(_block_avg(LANES), BF16)
    col = lambda c: pl.BlockSpec((1, S, 2 * LANES), lambda b: (b, 0, c // (2 * LANES)))
    full = lambda a: pl.BlockSpec(a.shape, lambda b: (0,) * a.ndim)
    cmp_spec = pl.BlockSpec((1, NSA_GROUPS, n_c, LANES), lambda b: (b, 0, 0, 0))
    seq_spec = pl.BlockSpec((1, NSA_GROUPS, S, LANES), lambda b: (b, 0, 0, 0))
    cmp_shape = jax.ShapeDtypeStruct((B, NSA_GROUPS, n_c, LANES), BF16)
    seq_shape = jax.ShapeDtypeStruct((B, NSA_GROUPS, S, LANES), BF16)
    return pl.pallas_call(
        _nsa_prep_body,
        grid=(B,),
        in_specs=[col(COL_KVC), col(COL_KSV), col(COL_KWV), full(pos), full(w1), full(w2), full(gains), full(avg)],
        out_specs=[cmp_spec, cmp_spec, seq_spec, seq_spec, seq_spec, seq_spec],
        out_shape=[cmp_shape, cmp_shape, seq_shape, seq_shape, seq_shape, seq_shape],
        scratch_shapes=[pltpu.VMEM((S, LANES), F32), pltpu.VMEM((S, LANES), F32)],
        compiler_params=_cparams(1),
        name="nsa_prep",
    )(proj3, proj3, proj3, pos, w1, w2, gains, avg)


def _nsa_attn_body(q_ref, gate_ref, kcmp_ref, vcmp_ref, ks_ref, vs_ref, kw_ref, vw_ref,
                   qgain_ref, kgain_ref, ogain_ref, avgq_ref, avgo_ref, msct_ref, esel_ref, egate_ref, wbias_ref,
                   dbias_ref,
                   o_ref, m_scr, acc_scr):
    tq = q_ref.shape[1]
    n_cmp = kcmp_ref.shape[2]
    n_sel = msct_ref.shape[0]
    kc_len = esel_ref.shape[2]
    rows = 4 * tq
    qi = pl.program_id(1)
    q0 = qi * tq

    q = q_ref[0].astype(F32)
    ms = _seg_mean(q * q, avgq_ref[...])
    qn = q * lax.rsqrt(ms + EPS) * qgain_ref[...] * (HEAD_DIM ** -0.5)

    gate_sig = _sigmoid(gate_ref[0].astype(F32)).astype(BF16)
    gates = [_dot(gate_sig, egate_ref[j]) for j in range(3)]

    lane_q = lax.broadcasted_iota(jnp.int32, (tq, LANES), 1)
    lo_q = lane_q < HEAD_DIM
    lo_r = lax.broadcasted_iota(jnp.int32, (rows, LANES), 1) < HEAD_DIM

    blk = lax.broadcasted_iota(jnp.int32, (n_sel, tq), 0)
    cur = lax.shift_right_logical(q0 + lax.broadcasted_iota(jnp.int32, (n_sel, tq), 1), int(np.log2(SEL_BLOCK)))
    forced = (blk == 0) | (blk == cur) | (blk == cur - 1)
    future = blk > cur
    blk_f = blk.astype(F32)

    def heads4(x):
        return jnp.concatenate([x] * 4, axis=0)

    def normalised_pairs(acc, guard):
        rolled = pltpu.roll(acc, HEAD_DIM, axis=1)
        den = jnp.where(lo_r, rolled, acc)
        if guard:
            den = jnp.maximum(den, 1e-30)
        out = []
        for p in range(2):
            ev = slice((2 * p) * tq, (2 * p + 1) * tq)
            od = slice((2 * p + 1) * tq, (2 * p + 2) * tq)
            out.append(jnp.where(lo_q, acc[ev] / den[ev], rolled[od] / den[od]))
        return out

    groups = range(NSA_GROUPS)
    qs = []
    for g in groups:
        slabs = [qn[:, (2 * g + p) * LANES:(2 * g + p + 1) * LANES] for p in range(2)]
        qs.append(jnp.concatenate(
            [jnp.where(lo_q, slabs[0], 0.0), jnp.where(lo_q, 0.0, slabs[0]),
             jnp.where(lo_q, slabs[1], 0.0), jnp.where(lo_q, 0.0, slabs[1])], axis=0).astype(BF16))

    r_c = lax.broadcasted_iota(jnp.int32, (rows, n_cmp), 0)
    c_c = lax.broadcasted_iota(jnp.int32, (rows, n_cmp), 1)
    cmask = (c_c * CMP_STRIDE + (CMP_BLOCK - 1)) <= q0 + (r_c & (tq - 1))
    s_c = [jnp.where(cmask, _dot_nt(qs[g], kcmp_ref[0, g]), NEG_INF) for g in groups]
    e_c = [jnp.where(cmask, jnp.exp(s_c[g] - jnp.max(s_c[g], axis=-1, keepdims=True)), 0.0) for g in groups]
    acc_c = [_dot(e_c[g].astype(BF16), vcmp_ref[0, g]) for g in groups]
    l_c = [jnp.where(lo_r, pltpu.roll(acc_c[g], HEAD_DIM, axis=1), acc_c[g]) for g in groups]
    p_c = [e_c[g] / jnp.maximum(l_c[g], 1e-30) for g in groups]

    p_sum = [p_c[g][0:tq] + p_c[g][tq:2 * tq] + p_c[g][2 * tq:3 * tq] + p_c[g][3 * tq:4 * tq] for g in groups]
    p_hi = [p_sum[g].astype(BF16) for g in groups]
    p_lo = [(p_sum[g] - p_hi[g].astype(F32)).astype(BF16) for g in groups]
    imp = [_dot_nt(msct_ref[...], p_hi[g]) + _dot_nt(msct_ref[...], p_lo[g]) for g in groups]
    v = [jnp.where(forced, FORCE_SCORE, jnp.where(future, NEG_INF, imp[g])) for g in groups]
    sel = [jnp.zeros((n_sel, tq), F32) for g in groups]
    for _ in range(SEL_TOPK):
        mx = [jnp.max(v[g], axis=0, keepdims=True) for g in groups]
        first = [jnp.min(jnp.where(v[g] == mx[g], blk_f, float(LANES)), axis=0, keepdims=True) for g in groups]
        pick = [blk_f == first[g] for g in groups]
        sel = [jnp.where(pick[g], 1.0, sel[g]) for g in groups]
        v = [jnp.where(pick[g], BELOW_ALL, v[g]) for g in groups]
    sel_b = [sel[g].astype(BF16) for g in groups]

    cmp_s = [normalised_pairs(acc_c[g], True) for g in groups]
    n_before = lax.shift_right_logical(q0, int(np.log2(kc_len)))
    causal = dbias_ref[qi & (kc_len // tq - 1)]
    w0 = pl.multiple_of(jnp.maximum(q0 - WINDOW, 0), tq)
    n_w = WINDOW + tq
    w_case = jnp.minimum(qi, WINDOW // tq)

    def sel_keys(ref, g, kc):
        return ref[0, g, pl.ds(pl.multiple_of(kc * kc_len, kc_len), kc_len), :]

    def sel_scores(g, kc, causal_bias, shift):
        chosen = _dot_tn(sel_b[g], esel_ref[kc])
        bias = (chosen - 1.0) * (-NEG_INF)
        if causal_bias is not None:
            bias = bias + causal_bias
        if shift is not None:
            bias = bias - shift
        return _dot_nt(qs[g], sel_keys(ks_ref, g, kc)) + heads4(bias)

    def win_scores(g, shift):
        bias = wbias_ref[w_case] if shift is None else wbias_ref[w_case] - shift
        return _dot_nt(qs[g], kw_ref[0, g, pl.ds(w0, n_w), :]) + heads4(bias)

    def win_values(g):
        return vw_ref[0, g, pl.ds(w0, n_w), :]

    def finish(acc_w):
        for g in groups:
            sel_s = normalised_pairs(acc_scr[g], False)
            win_s = normalised_pairs(acc_w[g], False)
            for p in range(2):
                cols = slice((2 * g + p) * LANES, (2 * g + p + 1) * LANES)
                mix = gates[0][:, cols] * cmp_s[g][p] + gates[1][:, cols] * sel_s[p] + gates[2][:, cols] * win_s[p]
                ms_o = _seg_mean(mix * mix, avgo_ref[...])
                o_ref[0, :, cols] = (mix * lax.rsqrt(ms_o + EPS) * ogain_ref[:, cols]).astype(BF16)

    def fixed_shift_path(shift):
        def probs(s):
            return jnp.exp(s).astype(BF16)

        for g in groups:
            acc_scr[g] = jnp.zeros(acc_scr.shape[1:], F32)

        def before(kc, carry):
            s = [sel_scores(g, kc, None, shift) for g in groups]
            p = [probs(s[g]) for g in groups]
            for g in groups:
                acc_scr[g] = acc_scr[g] + _dot(p[g], sel_keys(vs_ref, g, kc))
            return carry

        lax.fori_loop(0, n_before, before, 0)
        s_d0 = sel_scores(0, n_before, causal, shift)
        s_d1 = sel_scores(1, n_before, causal, shift)
        p_d0 = probs(s_d0)
        s_w0 = win_scores(0, shift)
        acc_scr[0] = acc_scr[0] + _dot(p_d0, sel_keys(vs_ref, 0, n_before))
        p_d1 = probs(s_d1)
        s_w1 = win_scores(1, shift)
        acc_scr[1] = acc_scr[1] + _dot(p_d1, sel_keys(vs_ref, 1, n_before))
        p_w0 = probs(s_w0)
        acc_w0 = _dot(p_w0, win_values(0))
        p_w1 = probs(s_w1)
        acc_w1 = _dot(p_w1, win_values(1))
        finish([acc_w0, acc_w1])

    def online_path():
        for g in groups:
            m_scr[g] = jnp.full(m_scr.shape[1:], NEG_INF, F32)
            acc_scr[g] = jnp.zeros(acc_scr.shape[1:], F32)

        def sel_softmax(g, s):
            m_old = m_scr[g]
            m_new = jnp.maximum(m_old, jnp.max(s, axis=-1, keepdims=True))
            m_scr[g] = m_new
            return jnp.exp(s - m_new).astype(BF16), jnp.exp(m_old - m_new)

        def sel_accumulate(g, kc, p, alpha):
            acc_scr[g] = alpha * acc_scr[g] + _dot(p, sel_keys(vs_ref, g, kc))

        def win_softmax(s):
            return jnp.exp(s - jnp.max(s, axis=-1, keepdims=True)).astype(BF16)

        def before(kc, carry):
            s = [sel_scores(g, kc, None, None) for g in groups]
            pa = [sel_softmax(g, s[g]) for g in groups]
            for g in groups:
                sel_accumulate(g, kc, *pa[g])
            return carry

        lax.fori_loop(0, n_before, before, 0)
        s_d0 = sel_scores(0, n_before, causal, None)
        s_d1 = sel_scores(1, n_before, causal, None)
        pa0 = sel_softmax(0, s_d0)
        s_w0 = win_scores(0, None)
        sel_accumulate(0, n_before, *pa0)
        pa1 = sel_softmax(1, s_d1)
        s_w1 = win_scores(1, None)
        sel_accumulate(1, n_before, *pa1)
        acc_w0 = _dot(win_softmax(s_w0), win_values(0))
        acc_w1 = _dot(win_softmax(s_w1), win_values(1))
        finish([acc_w0, acc_w1])

    bound = 1.01 * (HEAD_DIM ** 0.5) * jnp.max(jnp.abs(qgain_ref[...])) * jnp.max(jnp.abs(kgain_ref[...]))
    safe = bound <= MAX_FIXED_SHIFT
    pl.when(safe)(lambda: fixed_shift_path(bound))
    pl.when(jnp.logical_not(safe))(online_path)


def _sel_from_cmp(n_cmp, n_sel):
    c0 = np.arange(n_cmp) * CMP_STRIDE
    s0 = np.arange(n_sel) * SEL_BLOCK
    ov = np.minimum(c0[None, :] + CMP_BLOCK, s0[:, None] + SEL_BLOCK) - np.maximum(c0[None, :], s0[:, None])
    m = (np.clip(ov, 0, None) / CMP_BLOCK).astype(np.float32)
    m[:, (np.arange(n_cmp) * CMP_STRIDE + CMP_BLOCK) > n_sel * SEL_BLOCK] = 0.0
    return m


def _nsa_attn(proj3, kcmp, vcmp, ks, vs, kw, vw, q_gain, k_gains, o_gain):
    B, S, _ = proj3.shape
    n_cmp = kcmp.shape[2]
    n_sel = S // SEL_BLOCK
    tq = NSA_TQ
    assert n_sel % 8 == 0 and S % SEL_KC == 0 and SEL_KC % tq == 0 and WINDOW % tq == 0 and S >= WINDOW + tq
    avgq = jnp.asarray(_block_avg(NSA_WIDTH), BF16)
    avgo = jnp.asarray(_block_avg(LANES), BF16)
    msct = jnp.asarray(_sel_from_cmp(n_cmp, n_sel), BF16)
    esel = (np.arange(n_sel)[:, None] == np.arange(S)[None, :] // SEL_BLOCK).astype(np.float32)
    esel = jnp.asarray(esel.reshape(n_sel, S // SEL_KC, SEL_KC).transpose(1, 0, 2), BF16)
    src = np.arange(LANES)[:, None]
    dst = np.arange(NSA_WIDTH)[None, :]
    egate = jnp.asarray(np.stack([(src == (dst // HEAD_DIM) * 3 + j) for j in range(3)]).astype(np.float32), BF16)
    r = np.arange(tq)[:, None]
    n_w = WINDOW + tq
    wcases = []
    for i in range(WINDOW // tq + 1):
        diff = (i * tq - max(i * tq - WINDOW, 0)) + r - np.arange(n_w)[None, :]
        wcases.append(np.where((diff >= 0) & (diff < WINDOW), 0.0, NEG_INF))
    wbias = jnp.asarray(np.stack(wcases), F32)
    dbias = jnp.asarray(np.stack([np.where(np.arange(SEL_KC)[None, :] <= i * tq + r, 0.0, NEG_INF)
                                  for i in range(SEL_KC // tq)]), F32)

    full = lambda a: pl.BlockSpec(a.shape, lambda b, i: (0,) * a.ndim)
    per_b = lambda a: pl.BlockSpec((1,) + a.shape[1:], lambda b, i: (b,) + (0,) * (a.ndim - 1))
    return pl.pallas_call(
        _nsa_attn_body,
        grid=(B, S // tq),
        in_specs=[pl.BlockSpec((1, tq, NSA_WIDTH), lambda b, i: (b, i, COL_QA // NSA_WIDTH)),
                  pl.BlockSpec((1, tq, LANES), lambda b, i: (b, i, COL_GATE // LANES)),
                  per_b(kcmp), per_b(vcmp), per_b(ks), per_b(vs), per_b(kw), per_b(vw),
                  full(q_gain), full(k_gains), full(o_gain), full(avgq), full(avgo), full(msct), full(esel), full(egate),
                  full(wbias), full(dbias)],
        out_specs=pl.BlockSpec((1, tq, NSA_WIDTH), lambda b, i: (b, i, 0)),
        out_shape=jax.ShapeDtypeStruct((B, S, NSA_WIDTH), BF16),
        scratch_shapes=[pltpu.VMEM((NSA_GROUPS, 4 * tq, 1), F32), pltpu.VMEM((NSA_GROUPS, 4 * tq, LANES), F32)],
        compiler_params=_cparams(2),
        name="nsa_attn",
    )(proj3, proj3, kcmp, vcmp, ks, vs, kw, vw, q_gain, k_gains, o_gain, avgq, avgo, msct, esel, egate, wbias, dbias)


def _retention_body(q_ref, k_ref, v_ref, g_ref, cos_ref, sin_ref, decay_ref, xi_ref, zeta_ref, gammac_ref,
                    gain_ref, avg_ref, o_ref, state_scr):
    S = q_ref.shape[1]
    C = RET_CHUNK
    lane = lax.broadcasted_iota(jnp.int32, (C, LANES), 1)
    lo = lane < HEAD_DIM
    first_half = (lane & (HEAD_DIM - 1)) < HEAD_DIM // 2
    r = lax.broadcasted_iota(jnp.int32, (LANES, LANES), 0)
    c = lax.broadcasted_iota(jnp.int32, (LANES, LANES), 1)
    same_head = (r < HEAD_DIM) == (c < HEAD_DIM)
    avg = avg_ref[...]
    state_scr[...] = jnp.zeros(state_scr.shape, F32)

    def rope(x, cos, sin):
        swapped = jnp.where(first_half, pltpu.roll(x, LANES - HEAD_DIM // 2, axis=1),
                            pltpu.roll(x, HEAD_DIM // 2, axis=1))
        return x * cos + swapped * sin

    def chunk(n, carry):
        r0 = pl.multiple_of(n * C, C)
        cos = cos_ref[pl.ds(r0, C), :]
        sin = sin_ref[pl.ds(r0, C), :]
        pairs = range(RET_HEADS // 2)
        cols = [slice(p * LANES, (p + 1) * LANES) for p in pairs]
        q = [rope(q_ref[0, pl.ds(r0, C), cols[p]].astype(F32), cos, sin) for p in pairs]
        k = [rope(k_ref[0, pl.ds(r0, C), cols[p]].astype(F32), cos, sin) * (HEAD_DIM ** -0.5) for p in pairs]
        vb = [v_ref[0, pl.ds(r0, C), cols[p]] for p in pairs]
        kb = [k[p].astype(BF16) for p in pairs]
        inner = [_dot_nt(jnp.where(lo if half == 0 else ~lo, q[p], 0.0).astype(BF16), kb[p])
                 * decay_ref[2 * p + half] for p in pairs for half in range(2)]
        state = [state_scr[p] for p in pairs]
        cross = [_dot(q[p].astype(BF16), state[p].astype(BF16)) * xi_ref[p] for p in pairs]
        upd = [_dot_tn((k[p] * zeta_ref[p]).astype(BF16), vb[p]) for p in pairs]
        outs = [_dot(inner[i].astype(BF16), vb[i // 2]) for i in range(RET_HEADS)]
        for p in pairs:
            state_scr[p] = gammac_ref[p] * state[p] + jnp.where(same_head, upd[p], 0.0)
        y = jnp.concatenate([jnp.where(lo, outs[2 * p], outs[2 * p + 1]) + cross[p] for p in pairs], axis=0)
        mu = _seg_mean(y, avg)
        d = y - mu
        var = _seg_mean(d * d, avg)
        yn = d * lax.rsqrt(var + EPS)
        for p in pairs:
            gate = g_ref[0, pl.ds(r0, C), cols[p]].astype(F32)
            o_ref[0, pl.ds(r0, C), cols[p]] = (_silu(gate) * (yn[p * C:(p + 1) * C] * gain_ref[:, cols[p]])).astype(BF16)
        return carry

    lax.fori_loop(0, S // C, chunk, 0)


def _retention_tables(S):
    half = HEAD_DIM // 2
    inv_freq = ROPE_BASE ** (-jnp.arange(half, dtype=F32) / half)
    ang = jnp.arange(S, dtype=F32)[:, None] * inv_freq[None, :]
    cos, sin = jnp.cos(ang), jnp.sin(ang)
    cos_t = jnp.tile(cos, (1, 4))
    sin_t = jnp.tile(jnp.concatenate([-sin, sin], axis=1), (1, 2))
    C = RET_CHUNK
    H = RET_HEADS
    log_gamma = jnp.log1p(-jnp.power(2.0, -5.0 - jnp.arange(H, dtype=F32)))
    i = jnp.arange(C, dtype=F32)
    rel = i[:, None] - i[None, :]
    decay = jnp.where(rel >= 0, jnp.exp(jnp.maximum(rel, 0.0)[None] * log_gamma[:, None, None]), 0.0)
    xi = jnp.exp((i + 1.0)[:, None] * log_gamma[None, :])
    zeta = jnp.exp((C - 1.0 - i)[:, None] * log_gamma[None, :])
    gamma_c = jnp.exp(C * log_gamma)
    per_pair = lambda t: jnp.repeat(t.T.reshape(H // 2, 2, -1), HEAD_DIM, axis=1).transpose(0, 2, 1)
    gammac = jnp.repeat(gamma_c.reshape(H // 2, 2), HEAD_DIM, axis=1)[:, None, :]
    return cos_t, sin_t, decay, per_pair(xi), per_pair(zeta), gammac


def _retention(proj3, gain):
    B, S, _ = proj3.shape
    cos_t, sin_t, decay, xi, zeta, gammac = _retention_tables(S)
    avg = jnp.asarray(_block_avg(LANES), BF16)
    col = lambda c: pl.BlockSpec((1, S, RET_WIDTH), lambda b: (b, 0, c // RET_WIDTH))
    full = lambda a: pl.BlockSpec(a.shape, lambda b: (0,) * a.ndim)
    return pl.pallas_call(
        _retention_body,
        grid=(B,),
        in_specs=[col(COL_QR), col(COL_KR), col(COL_VR), col(COL_GR), full(cos_t), full(sin_t), full(decay),
                  full(xi), full(zeta), full(gammac), full(gain), full(avg)],
        out_specs=pl.BlockSpec((1, S, RET_WIDTH), lambda b: (b, 0, 0)),
        out_shape=jax.ShapeDtypeStruct((B, S, RET_WIDTH), BF16),
        scratch_shapes=[pltpu.VMEM((RET_HEADS // 2, LANES, LANES), F32)],
        compiler_params=_cparams(1),
        name="retention",
    )(proj3, proj3, proj3, proj3, cos_t, sin_t, decay, xi, zeta, gammac, gain, avg)


def _mem_prep_body(mem_ref, g_ref, wkv_ref, kgain_ref, avg_ref, k_ref, v_ref):
    hm = _rms_full(mem_ref[0], g_ref[...]).astype(BF16)
    kv = _dot(hm, wkv_ref[...])
    k = kv[:, :MEM_WIDTH]
    ms = _seg_mean(k * k, avg_ref[...])
    k_ref[0] = (k * lax.rsqrt(ms + EPS) * kgain_ref[...]).astype(BF16)
    v_ref[0] = kv[:, MEM_WIDTH:].astype(BF16)


def _mem_prep(mem, g, wkv, kgain):
    B, M, _ = mem.shape
    avg = jnp.asarray(_block_avg(MEM_WIDTH), BF16)
    full = lambda a: pl.BlockSpec(a.shape, lambda b: (0,) * a.ndim)
    out_spec = pl.BlockSpec((1, M, MEM_WIDTH), lambda b: (b, 0, 0))
    out_shape = jax.ShapeDtypeStruct((B, M, MEM_WIDTH), BF16)
    return pl.pallas_call(
        _mem_prep_body,
        grid=(B,),
        in_specs=[pl.BlockSpec((1, M, D_MODEL), lambda b: (b, 0, 0)), full(g), full(wkv), full(kgain), full(avg)],
        out_specs=[out_spec, out_spec],
        out_shape=[out_shape, out_shape],
        compiler_params=_cparams(1),
        name="mem_prep",
    )(mem, g, wkv, kgain, avg)


def _post_body(x_ref, oa_ref, ob_ref, wout_ref, mk_ref, mv_ref, gx_ref, wq_ref, qgain_ref, avg_ref, wo_ref,
               gf_ref, wr_ref, br_ref, tri_ref, x2_ref, h_ref, route_ref, count_ref):
    tm = x_ref.shape[1] // POST_CHAINS
    chains = range(POST_CHAINS)
    rows = [slice(c * tm, (c + 1) * tm) for c in chains]
    x1 = [x_ref[0, rows[c]] + _dot(oa_ref[0, rows[c]], wout_ref[0:NSA_WIDTH, :])
          + _dot(ob_ref[0, rows[c]], wout_ref[NSA_WIDTH:, :]) for c in chains]

    h = [_rms_full(x1[c], gx_ref[...]).astype(BF16) for c in chains]
    q = [_dot(h[c], wq_ref[...]) for c in chains]
    ms = [_seg_mean(q[c] * q[c], avg_ref[...]) for c in chains]
    q = [q[c] * lax.rsqrt(ms[c] + EPS) * qgain_ref[...] * (HEAD_DIM ** -0.5) for c in chains]
    lane = lax.broadcasted_iota(jnp.int32, (tm, LANES), 1)
    lo = lane < HEAD_DIM
    heads = [(c, p, half) for c in chains for p in range(MEM_HEADS // 2) for half in range(2)]
    s = [_dot_nt(jnp.where(lo if half == 0 else ~lo, q[c][:, p * LANES:(p + 1) * LANES], 0.0).astype(BF16),
                 mk_ref[0, :, p * LANES:(p + 1) * LANES]) for c, p, half in heads]
    e = [jnp.exp(s[i] - jnp.max(s[i], axis=-1, keepdims=True)) for i in range(len(heads))]
    pr = [(e[i] / jnp.sum(e[i], axis=-1, keepdims=True)).astype(BF16) for i in range(len(heads))]
    outs = [_dot(pr[i], mv_ref[0, :, heads[i][1] * LANES:(heads[i][1] + 1) * LANES]) for i in range(len(heads))]
    per_chain = MEM_HEADS
    o = [jnp.concatenate([jnp.where(lo, outs[c * per_chain + 2 * p], outs[c * per_chain + 2 * p + 1])
                          for p in range(MEM_HEADS // 2)], axis=1).astype(BF16) for c in chains]
    x2 = [x1[c] + _dot(o[c], wo_ref[...]) for c in chains]
    for c in chains:
        x2_ref[0, rows[c]] = x2[c]

    hf = [_rms_full(x2[c], gf_ref[...]).astype(BF16) for c in chains]
    for c in chains:
        h_ref[0, rows[c]] = hf[c]
    logits = [_dot(hf[c], wr_ref[...]) + br_ref[...] for c in chains]
    lane_f = lane.astype(F32)
    big = float(LANES)
    picks = []
    for c in chains:
        gl = jnp.where(lane < N_GROUPS, logits[c], BELOW_ALL)
        gmax = jnp.max(gl, axis=-1, keepdims=True)
        grp = jnp.min(jnp.where(gl == gmax, lane_f, big), axis=-1, keepdims=True)
        g_w = 1.0 / jnp.sum(jnp.where(lane < N_GROUPS, jnp.exp(gl - gmax), 0.0), axis=-1, keepdims=True)
        e_lo = N_GROUPS + grp * EXPERTS_PER_GROUP
        el = jnp.where((lane_f >= e_lo) & (lane_f < e_lo + EXPERTS_PER_GROUP), logits[c], BELOW_ALL)
        v0 = jnp.max(el, axis=-1, keepdims=True)
        i0 = jnp.min(jnp.where(el == v0, lane_f, big), axis=-1, keepdims=True)
        el = jnp.where(lane_f == i0, BELOW_ALL, el)
        v1 = jnp.max(el, axis=-1, keepdims=True)
        i1 = jnp.min(jnp.where(el == v1, lane_f, big), axis=-1, keepdims=True)
        e1 = jnp.exp(v1 - v0)
        picks.append((i0 - N_GROUPS, i1 - N_GROUPS, g_w / (1.0 + e1), g_w * e1 / (1.0 + e1)))

    hot = [[lane_f == picks[c][s] for s in range(2)] for c in chains]
    both = jnp.concatenate([jnp.where(hot[c][0], 1.0, 0.0) + jnp.where(hot[c][1], 1.0, 0.0) for c in chains], axis=0)
    before = _dot(tri_ref[...], both.astype(BF16))
    count_ref[0] = jnp.broadcast_to(jnp.sum(both, axis=0, keepdims=True), count_ref.shape[1:])
    for c in chains:
        e0, e1, w0, w1 = picks[c]
        r0 = jnp.sum(jnp.where(hot[c][0], before[rows[c]], 0.0), axis=-1, keepdims=True)
        r1 = jnp.sum(jnp.where(hot[c][1], before[rows[c]], 0.0), axis=-1, keepdims=True)
        cols = (e0, e1, w0, w1, r0, r1)
        route = jnp.zeros((tm, LANES), F32)
        for k in range(len(cols)):
            route = jnp.where(lane == k, cols[k], route)
        route_ref[0, rows[c]] = route


def _post(x, oa, ob, wout, mk, mv, gx, wq, qgain, wo, gf, wr, br):
    B, S, _ = x.shape
    tm = MOE_TM
    n_s = S // tm
    avg = jnp.asarray(_block_avg(MEM_WIDTH), BF16)
    tri = jnp.asarray(np.tril(np.ones((tm, tm), np.float32), -1), BF16)
    full = lambda a: pl.BlockSpec(a.shape, lambda b, i: (0,) * a.ndim)
    per_b = lambda a: pl.BlockSpec((1,) + a.shape[1:], lambda b, i: (b,) + (0,) * (a.ndim - 1))
    tile = lambda w: pl.BlockSpec((1, tm, w), lambda b, i: (b, i, 0))
    return pl.pallas_call(
        _post_body,
        grid=(B, n_s),
        in_specs=[tile(D_MODEL), tile(NSA_WIDTH), tile(RET_WIDTH), full(wout), per_b(mk), per_b(mv), full(gx),
                  full(wq), full(qgain), full(avg), full(wo), full(gf), full(wr), full(br), full(tri)],
        out_specs=[tile(D_MODEL), tile(D_MODEL), tile(LANES),
                   pl.BlockSpec((1, 8, LANES), lambda b, i: (b * n_s + i, 0, 0))],
        out_shape=[jax.ShapeDtypeStruct((B, S, D_MODEL), F32), jax.ShapeDtypeStruct((B, S, D_MODEL), BF16),
                   jax.ShapeDtypeStruct((B, S, LANES), F32), jax.ShapeDtypeStruct((B * n_s, 8, LANES), F32)],
        compiler_params=_cparams(2),
        name="post_mixer",
    )(x, oa, ob, wout, mk, mv, gx, wq, qgain, avg, wo, gf, wr, br, tri)


def _row_copy(src, dst, sem):
    return pltpu.make_async_copy(src, dst, sem)


def _run_pieces(n, max_piece, fn):
    b = RUN_ALIGN
    while b <= max_piece:
        pl.when((n & b) != 0)(functools.partial(fn, n & (-2 * b), b))
        b *= 2


def _move_groups(i, gmap_ref, n_loc, copy):
    n_groups = n_loc // RUN_ALIGN
    for j in range(n_groups):
        glob = pl.multiple_of(gmap_ref[i * n_groups + j], RUN_ALIGN)
        copy(pl.ds(j * RUN_ALIGN, RUN_ALIGN), pl.ds(glob, RUN_ALIGN)).start()


def _local_positions(route, loff_row):
    lane = lax.broadcasted_iota(jnp.int32, route.shape, 1).astype(F32)
    pos = []
    for s in range(2):
        base = jnp.sum(jnp.where(lane == route[:, s:s + 1], loff_row, 0.0), axis=-1, keepdims=True)
        pos.append(base + route[:, 4 + s:5 + s])
    return pos


def _scatter_body(gmap_ref, tstart_ref, tlen_ref, nact_ref,
                  h_ref, route_ref, lofff_ref, xs_ref, xloc, zbuf, sems):
    i = pl.program_id(0)
    tm = h_ref.shape[0]
    n_loc = xloc.shape[1]
    slot = i & 1
    sem = sems.at[0]

    def tails(start):
        def per_expert(e, carry):
            n = tlen_ref[e]
            st = tstart_ref[e]

            def piece(off, size):
                c = _row_copy(zbuf.at[pl.ds(0, size)], xs_ref.at[pl.ds(pl.multiple_of(st + off, RUN_ALIGN), size)], sem)
                c.start() if start else c.wait()

            _run_pieces(n, MOE_RB // 2, piece)
            return carry

        lax.fori_loop(0, N_EXPERTS, per_expert, 0)

    def unused(start):
        rows = zbuf.shape[0]

        def per_unit(u, carry):
            c = _row_copy(zbuf, xs_ref.at[pl.ds(pl.multiple_of(u * rows, rows), rows)], sem)
            c.start() if start else c.wait()
            return carry

        lax.fori_loop(nact_ref[0] * (MOE_RB // rows), xs_ref.shape[0] // rows, per_unit, 0)

    @pl.when(i == 0)
    def _():
        zbuf[...] = jnp.zeros(zbuf.shape, zbuf.dtype)
        tails(True)
        unused(True)
        tails(False)
        unused(False)

    pos = _local_positions(route_ref[...], lofff_ref[0, 0:1, :])
    col = lax.broadcasted_iota(jnp.int32, (tm, n_loc), 1).astype(F32)
    perm_t = jnp.where((col == pos[0]) | (col == pos[1]), 1.0, 0.0).astype(BF16)
    xloc[slot] = _dot_tn(perm_t, h_ref[...]).astype(BF16)

    def wait_slot(s):
        _row_copy(xloc.at[s], xs_ref.at[pl.ds(0, n_loc)], sems.at[s]).wait()

    pl.when(i > 0)(lambda: wait_slot(1 - slot))
    _move_groups(i, gmap_ref, n_loc, lambda loc, glob: _row_copy(xloc.at[slot, loc], xs_ref.at[glob], sems.at[slot]))
    pl.when(i == pl.num_programs(0) - 1)(lambda: wait_slot(slot))


def _scatter_rows(tables, h2d, route2d, loff_f, n_rows):
    T = h2d.shape[0]
    tm = MOE_TM
    n_loc = MOE_NLOC
    tile = lambda w: pl.BlockSpec((tm, w), lambda i, *_: (i, 0))
    grid_spec = pltpu.PrefetchScalarGridSpec(
        num_scalar_prefetch=4,
        grid=(T // tm,),
        in_specs=[tile(D_MODEL), tile(LANES), pl.BlockSpec((1, 8, LANES), lambda i, *_: (i, 0, 0))],
        out_specs=pl.BlockSpec(memory_space=pl.ANY),
        scratch_shapes=[pltpu.VMEM((2, n_loc, D_MODEL), BF16), pltpu.VMEM((MOE_RB // 2, D_MODEL), BF16),
                        pltpu.SemaphoreType.DMA((2,))],
    )
    return pl.pallas_call(
        _scatter_body,
        grid_spec=grid_spec,
        out_shape=jax.ShapeDtypeStruct((n_rows, D_MODEL), BF16),
        compiler_params=_cparams(1),
        name="moe_scatter",
    )(*tables, h2d, route2d, loff_f)


def _expert_body(blk0_ref, nblk_ref, n_act_ref, xs_ref, wg_ref, wu_ref, wd_ref, ys_ref,
                 wg_b, wu_b, wd_b, xbuf, ybuf, sem_in, sem_out):
    e = pl.program_id(0)
    n_slots, rb = xbuf.shape[0], xbuf.shape[1]
    ahead = n_slots - 2
    n_act = n_act_ref[0]
    b0 = blk0_ref[e]

    def rows(g):
        return pl.ds(pl.multiple_of(g * rb, rb), rb)

    def x_copy(g, slot):
        return _row_copy(xs_ref.at[rows(g)], xbuf.at[slot], sem_in.at[slot])

    def y_copy(g, slot):
        return _row_copy(ybuf.at[slot], ys_ref.at[rows(g)], sem_out.at[slot])

    @pl.when(e == 0)
    def _():
        for k in range(ahead):
            pl.when(k < n_act)(lambda k=k: x_copy(k, k).start())

    wg_b[...] = wg_ref[0].astype(BF16)
    wu_b[...] = wu_ref[0].astype(BF16)
    wd_b[...] = wd_ref[0].astype(BF16)

    def blocks(g0, count):
        gs = [g0 + c for c in range(count)]
        slots = [g & (n_slots - 1) for g in gs]
        for c in range(count):
            x_copy(gs[c], slots[c]).wait()
        for c in range(count):
            nxt = gs[c] + ahead
            pl.when(nxt < n_act)(lambda nxt=nxt: x_copy(nxt, nxt & (n_slots - 1)).start())
        x = [xbuf[slots[c]] for c in range(count)]
        a = [_dot(x[c], wg_b[...]) for c in range(count)]
        b = [_dot(x[c], wu_b[...]) for c in range(count)]
        h = [(_silu(a[c]) * b[c]).astype(BF16) for c in range(count)]
        y = [_dot(h[c], wd_b[...]).astype(BF16) for c in range(count)]
        for c in range(count):
            pl.when(gs[c] >= n_slots)(lambda c=c: y_copy(gs[c] - n_slots, slots[c]).wait())
        for c in range(count):
            ybuf[slots[c]] = y[c]
            y_copy(gs[c], slots[c]).start()

    nb = nblk_ref[e]

    def pair(j, carry):
        blocks(b0 + 2 * j, 2)
        return carry

    lax.fori_loop(0, lax.shift_right_logical(nb, 1), pair, 0)
    pl.when((nb & 1) == 1)(lambda: blocks(b0 + nb - 1, 1))

    @pl.when(e == pl.num_programs(0) - 1)
    def _():
        for k in range(1, n_slots + 1):
            pl.when(n_act >= k)(lambda k=k: y_copy(n_act - k, (n_act - k) & (n_slots - 1)).wait())
        ybuf[0] = jnp.zeros(ybuf.shape[1:], ybuf.dtype)
        n_blocks = ys_ref.shape[0] // rb

        def fill(start):
            def per_block(g, carry):
                c = y_copy(g, 0)
                c.start() if start else c.wait()
                return carry

            lax.fori_loop(n_act, n_blocks, per_block, 0)

        fill(True)
        fill(False)


def _experts(blk0, nblk, n_act, xs, n_rows, wg, wu, wd):
    rb = MOE_RB
    weight = lambda a: pl.BlockSpec((1,) + a.shape[1:], lambda e, *_: (e, 0, 0))
    grid_spec = pltpu.PrefetchScalarGridSpec(
        num_scalar_prefetch=3,
        grid=(N_EXPERTS,),
        in_specs=[pl.BlockSpec(memory_space=pl.ANY), weight(wg), weight(wu), weight(wd)],
        out_specs=pl.BlockSpec(memory_space=pl.ANY),
        scratch_shapes=[pltpu.VMEM((D_MODEL, EXPERT_FF), BF16), pltpu.VMEM((D_MODEL, EXPERT_FF), BF16),
                        pltpu.VMEM((EXPERT_FF, D_MODEL), BF16), pltpu.VMEM((MOE_SLOTS, rb, D_MODEL), BF16),
                        pltpu.VMEM((MOE_SLOTS, rb, D_MODEL), BF16), pltpu.SemaphoreType.DMA((MOE_SLOTS,)),
                        pltpu.SemaphoreType.DMA((MOE_SLOTS,))],
    )
    return pl.pallas_call(
        _expert_body,
        grid_spec=grid_spec,
        out_shape=jax.ShapeDtypeStruct((n_rows, D_MODEL), BF16),
        compiler_params=_cparams(1),
        name="moe_experts",
    )(blk0, nblk, n_act, xs, wg, wu, wd)


def _combine_body(gmap_ref, x_ref, route_ref, lofff_ref, ys_ref, o_ref, yloc, sems):
    i = pl.program_id(0)
    tm = x_ref.shape[0]
    n_loc = yloc.shape[1]
    slot = i & 1

    def fetch(tile, s):
        _move_groups(tile, gmap_ref, n_loc, lambda loc, glob: _row_copy(ys_ref.at[glob], yloc.at[s, loc], sems.at[s]))

    pl.when(i == 0)(lambda: fetch(i, slot))
    _row_copy(ys_ref.at[pl.ds(0, n_loc)], yloc.at[slot], sems.at[slot]).wait()
    pl.when(i + 1 < pl.num_programs(0))(lambda: fetch(i + 1, 1 - slot))

    route = route_ref[...]
    pos = _local_positions(route, lofff_ref[0, 0:1, :])
    col = lax.broadcasted_iota(jnp.int32, (tm, n_loc), 1).astype(F32)
    perm_w = (jnp.where(col == pos[0], route[:, 2:3], 0.0) + jnp.where(col == pos[1], route[:, 3:4], 0.0)).astype(BF16)
    o_ref[...] = x_ref[...] + _dot(perm_w, yloc[slot])


def _combine(tables, x2d, route2d, loff_f, ys):
    T = x2d.shape[0]
    tm = MOE_TM
    n_loc = MOE_NLOC
    tile = lambda w: pl.BlockSpec((tm, w), lambda i, *_: (i, 0))
    grid_spec = pltpu.PrefetchScalarGridSpec(
        num_scalar_prefetch=1,
        grid=(T // tm,),
        in_specs=[tile(D_MODEL), tile(LANES), pl.BlockSpec((1, 8, LANES), lambda i, *_: (i, 0, 0)),
                  pl.BlockSpec(memory_space=pl.ANY)],
        out_specs=tile(D_MODEL),
        scratch_shapes=[pltpu.VMEM((2, n_loc, D_MODEL), BF16), pltpu.SemaphoreType.DMA((2,))],
    )
    return pl.pallas_call(
        _combine_body,
        grid_spec=grid_spec,
        out_shape=jax.ShapeDtypeStruct((T, D_MODEL), F32),
        compiler_params=_cparams(1),
        name="moe_combine",
    )(*tables, x2d, route2d, loff_f, ys)


def _compress_weights(pos, w1, w2):
    eye = jnp.eye(NSA_GROUPS, dtype=F32)
    w1r = w1.reshape(CMP_BLOCK, HEAD_DIM, CMP_HIDDEN)
    w1b = jnp.einsum('ldh,gk->lgdkh', w1r, eye).reshape(2, CMP_STRIDE * LANES, NSA_GROUPS * CMP_HIDDEN)
    w2b = jnp.einsum('hd,gk->ghkd', w2, eye).reshape(NSA_GROUPS * CMP_HIDDEN, LANES)
    posb = jnp.tile(pos, (1, NSA_GROUPS)).reshape(2, 1, CMP_STRIDE * LANES)
    return posb, w1b.astype(BF16), w2b.astype(BF16)


def _dup2(g):
    return jnp.tile(g.reshape(1, HEAD_DIM), (1, 2))


def kernel(x, mem, mix_norm, w_in, nsa_q_norm, nsa_kcmp_norm, nsa_ksel_norm, nsa_kwin_norm, cmp_pos_k, cmp_pos_v, cmp_k_w1, cmp_k_w2, cmp_v_w1, cmp_v_w2, nsa_out_norm, ret_out_norm, w_out, mem_x_norm, mem_kv_norm, mem_wq, mem_wkv, mem_q_norm, mem_k_norm, mem_wo, ffn_norm, router_group_w, router_group_b, router_expert_w, router_expert_b, exp_w_gate, exp_w_up, exp_w_down):
    B, S, D = x.shape
    T = B * S
    depth = mix_norm.shape[0]
    for l in range(depth):
        proj = _proj(x.reshape(T, D), mix_norm[l].reshape(1, D), w_in[l].T).reshape(B, S, PROJ_PAD)
        pk, w1k, w2k = _compress_weights(cmp_pos_k[l], cmp_k_w1[l], cmp_k_w2[l])
        pv, w1v, w2v = _compress_weights(cmp_pos_v[l], cmp_v_w1[l], cmp_v_w2[l])
        gains = jnp.stack([_dup2(nsa_kcmp_norm[l]), _dup2(nsa_ksel_norm[l]), _dup2(nsa_kwin_norm[l])])
        kcmp, vcmp, ks, vs, kw, vw = _nsa_prep(proj, jnp.stack([pk, pv]), jnp.stack([w1k, w1v]),
                                               jnp.stack([w2k, w2v]), gains)
        o_a = _nsa_attn(proj, kcmp, vcmp, ks, vs, kw, vw,
                        jnp.tile(nsa_q_norm[l].reshape(1, HEAD_DIM), (1, NSA_HEADS)), gains[1:],
                        nsa_out_norm[l].reshape(1, NSA_WIDTH))
        o_b = _retention(proj, ret_out_norm[l].reshape(1, RET_WIDTH))
        mk, mv = _mem_prep(mem, mem_kv_norm[l].reshape(1, D), mem_wkv[l].astype(BF16),
                           jnp.tile(mem_k_norm[l].reshape(1, HEAD_DIM), (1, MEM_HEADS)))
        w_r = jnp.concatenate([router_group_w[l],
                               router_expert_w[l].transpose(1, 0, 2).reshape(D, N_EXPERTS),
                               jnp.zeros((D, LANES - N_GROUPS - N_EXPERTS), F32)], axis=1).astype(BF16)
        b_r = jnp.concatenate([router_group_b[l], router_expert_b[l].reshape(N_EXPERTS),
                               jnp.zeros((LANES - N_GROUPS - N_EXPERTS,), F32)]).reshape(1, LANES)
        x2, hf, route, counts = _post(
            x, o_a, o_b, w_out[l].astype(BF16), mk, mv, mem_x_norm[l].reshape(1, D), mem_wq[l].astype(BF16),
            jnp.tile(mem_q_norm[l].reshape(1, HEAD_DIM), (1, MEM_HEADS)), mem_wo[l].astype(BF16),
            ffn_norm[l].reshape(1, D), w_r, b_r)
        route2d = route.reshape(T, LANES)
        n_tiles = T // MOE_TM
        cnt = counts[:, 0, :N_EXPERTS].astype(jnp.int32)
        cnt = (cnt + RUN_ALIGN - 1) // RUN_ALIGN * RUN_ALIGN
        loff = jnp.cumsum(cnt, axis=1) - cnt
        total = jnp.sum(cnt, axis=0)
        padded = (total + MOE_RB - 1) // MOE_RB * MOE_RB
        pend = jnp.cumsum(padded)
        pstart = pend - padded
        goff = pstart[None, :] + jnp.cumsum(cnt, axis=0) - cnt
        n_rows = 2 * T + n_tiles * N_EXPERTS * RUN_ALIGN + N_EXPERTS * MOE_RB
        n_act = (pend[-1:] // MOE_RB).astype(jnp.int32)
        loff_f = jnp.broadcast_to(jnp.pad(loff.astype(F32), ((0, 0), (0, LANES - N_EXPERTS)))[:, None, :],
                                  (n_tiles, 8, LANES))
        grp_row = jnp.arange(MOE_NLOC // RUN_ALIGN, dtype=jnp.int32) * RUN_ALIGN
        inside = ((loff[:, None, :] <= grp_row[None, :, None])
                  & (grp_row[None, :, None] < (loff + cnt)[:, None, :])).astype(jnp.int32)
        shift = jnp.sum(inside * (goff - loff)[:, None, :], axis=2)
        used = jnp.sum(inside, axis=2) > 0
        gmap_scatter = jnp.where(used, shift + grp_row[None, :], n_rows + grp_row[None, :]).reshape(-1)
        gmap_gather = jnp.where(used, shift + grp_row[None, :], 0).reshape(-1)
        xs = _scatter_rows((gmap_scatter, pstart + total, padded - total, n_act), hf.reshape(T, D), route2d, loff_f,
                           n_rows + MOE_NLOC)
        ys = _experts(pstart // MOE_RB, padded // MOE_RB, n_act, xs, n_rows,
                      exp_w_gate[l], exp_w_up[l], exp_w_down[l])
        x = _combine((gmap_gather,), x2.reshape(T, D), route2d, loff_f, ys).reshape(B, S, D)
    return x
```

```python
import functools

import numpy as np
import jax
import jax.numpy as jnp
from jax import lax
from jax.experimental import pallas as pl
from jax.experimental.pallas import tpu as pltpu

F32 = jnp.float32
BF16 = jnp.bfloat16

D_MODEL = 1024
HEAD_DIM = 64
LANES = 128
NSA_HEADS = 8
NSA_GROUPS = 2
NSA_WIDTH = NSA_HEADS * HEAD_DIM
CMP_BLOCK = 32
CMP_STRIDE = 16
CMP_HIDDEN = 2 * HEAD_DIM
SEL_BLOCK = 64
SEL_TOPK = 8
WINDOW = 512
RET_HEADS = 8
RET_WIDTH = RET_HEADS * HEAD_DIM
RET_CHUNK = 128
ROPE_BASE = 10000.0
MEM_HEADS = 4
MEM_WIDTH = MEM_HEADS * HEAD_DIM
N_GROUPS = 4
EXPERTS_PER_GROUP = 8
N_EXPERTS = N_GROUPS * EXPERTS_PER_GROUP
EXPERT_FF = D_MODEL // 4
EPS = 1e-6
NEG_INF = -1e30
FORCE_SCORE = 1e9
BELOW_ALL = -3e38
MAX_FIXED_SHIFT = 40.0

COL_QA = 0
COL_QR, COL_KR, COL_VR, COL_GR = 512, 1024, 1536, 2048
COL_KVC, COL_KSV, COL_KWV = 2560, 2816, 3072
COL_GATE = 3328
PROJ_PAD = 3456
PROJ_SPLITS = (NSA_WIDTH, NSA_WIDTH + 6 * NSA_GROUPS * HEAD_DIM, NSA_WIDTH + 6 * NSA_GROUPS * HEAD_DIM + 3 * NSA_HEADS)

PROJ_TM = 512
NSA_TQ = 256
SEL_KC = 512
WIN_KEYS = WINDOW + NSA_TQ
POST_CHAINS = 2
MOE_TM = 512
MOE_RB = 256
MOE_SLOTS = 8
RUN_ALIGN = 16
MOE_NLOC = 2 * MOE_TM + N_EXPERTS * RUN_ALIGN
VMEM_LIMIT = 56 * 1024 * 1024


def _cparams(n_axes):
    return pltpu.CompilerParams(dimension_semantics=("arbitrary",) * n_axes,
                                vmem_limit_bytes=VMEM_LIMIT)


def _dot(a, b):
    return jnp.dot(a, b, preferred_element_type=F32)


def _dot_nt(a, b):
    return lax.dot_general(a, b, (((1,), (1,)), ((), ())), preferred_element_type=F32)


def _dot_tn(a, b):
    return lax.dot_general(a, b, (((0,), (0,)), ((), ())), preferred_element_type=F32)


def _rms_full(x, g):
    ms = jnp.mean(x * x, axis=-1, keepdims=True)
    return x * lax.rsqrt(ms + EPS) * g


def _seg_mean(x, avg):
    return _dot(x.astype(BF16), avg)


def _silu(x):
    return x * (1.0 / (1.0 + jnp.exp(-x)))


def _sigmoid(x):
    return 1.0 / (1.0 + jnp.exp(-x))


def _block_avg(width):
    i = np.arange(width)
    return ((i[:, None] // HEAD_DIM == i[None, :] // HEAD_DIM) / HEAD_DIM).astype(np.float32)


def _proj_body(x_ref, g_ref, wt_ref, o_ref, w_scr):
    @pl.when(pl.program_id(0) == 0)
    def _():
        kv0, gate0, ret0 = PROJ_SPLITS
        w_scr[COL_QA:COL_QR] = wt_ref[0:kv0].astype(BF16)
        w_scr[COL_QR:COL_KVC] = wt_ref[ret0:].astype(BF16)
        w_scr[COL_KVC:COL_GATE] = wt_ref[kv0:gate0].astype(BF16)
        pad = jnp.zeros((PROJ_PAD - COL_GATE - (ret0 - gate0), wt_ref.shape[1]), F32)
        w_scr[COL_GATE:] = jnp.concatenate([wt_ref[gate0:ret0], pad], axis=0).astype(BF16)

    h = _rms_full(x_ref[...], g_ref[...]).astype(BF16)
    step = PROJ_PAD // 3
    for j in range(3):
        o_ref[:, j * step:(j + 1) * step] = _dot_nt(h, w_scr[j * step:(j + 1) * step]).astype(BF16)


def _proj(x2d, g, w_t):
    T = x2d.shape[0]
    assert w_t.shape[0] == PROJ_SPLITS[2] + 4 * RET_WIDTH
    return pl.pallas_call(
        _proj_body,
        grid=(T // PROJ_TM,),
        in_specs=[pl.BlockSpec((PROJ_TM, D_MODEL), lambda i: (i, 0)),
                  pl.BlockSpec((1, D_MODEL), lambda i: (0, 0)),
                  pl.BlockSpec(w_t.shape, lambda i: (0, 0))],
        out_specs=pl.BlockSpec((PROJ_TM, PROJ_PAD), lambda i: (i, 0)),
        out_shape=jax.ShapeDtypeStruct((T, PROJ_PAD), BF16),
        scratch_shapes=[pltpu.VMEM((PROJ_PAD, D_MODEL), BF16)],
        compiler_params=_cparams(1),
        name="proj",
    )(x2d, g, w_t)


def _dup_groups(x):
    lane = lax.broadcasted_iota(jnp.int32, x.shape, 1)
    xs = pltpu.roll(x, HEAD_DIM, axis=1)
    lo = lane < HEAD_DIM
    return jnp.where(lo, x, xs), jnp.where(lo, xs, x)


def _ones_groups(x):
    lane = lax.broadcasted_iota(jnp.int32, x.shape, 1)
    lo = lane < HEAD_DIM
    return jnp.where(lo, x, 1.0), jnp.where(lo, pltpu.roll(x, HEAD_DIM, axis=1), 1.0)


def _nsa_prep_body(kvc_ref, ksv_ref, kwv_ref, pos_ref, w1k_ref, w1v_ref, w2_ref, gain_ref, avg_ref,
                   kcmp_ref, vcmp_ref, ks_ref, vs_ref, kw_ref, vw_ref, scr_k, scr_v, w1_ref):
    @pl.when(pl.program_id(0) == 0)
    def _():
        zero = jnp.zeros((HEAD_DIM, CMP_HIDDEN), BF16)
        for j, src in ((0, w1k_ref), (1, w1v_ref)):
            for l in range(CMP_BLOCK):
                piece = src[l * HEAD_DIM:(l + 1) * HEAD_DIM, :].astype(BF16)
                r0 = (l % CMP_STRIDE) * LANES
                w1_ref[j, l // CMP_STRIDE, r0:r0 + HEAD_DIM, :] = jnp.concatenate([piece, zero], axis=1)
                w1_ref[j, l // CMP_STRIDE, r0 + HEAD_DIM:r0 + LANES, :] = jnp.concatenate([zero, piece], axis=1)

    avg = avg_ref[...]
    n_c = scr_k.shape[0] // CMP_STRIDE
    scr_k[...] = kvc_ref[0, :, 0:LANES].astype(F32)
    scr_v[...] = kvc_ref[0, :, LANES:2 * LANES].astype(F32)
    for j, out_ref, scr in ((0, kcmp_ref, scr_k), (1, vcmp_ref, scr_v)):
        ycat = jnp.concatenate(
            [scr[pl.ds(l, n_c, stride=CMP_STRIDE), :] for l in range(CMP_STRIDE)], axis=1)
        first = _dot((ycat + pos_ref[j, 0]).astype(BF16), w1_ref[j, 0])
        second = _dot((ycat + pos_ref[j, 1]).astype(BF16), w1_ref[j, 1])
        hidden = first + pltpu.roll(second, n_c - 1, axis=0)
        cmp_tok = _dot(_silu(hidden).astype(BF16), w2_ref[j])
        if j == 0:
            ms = _seg_mean(cmp_tok * cmp_tok, avg)
            cmp_tok = cmp_tok * lax.rsqrt(ms + EPS) * gain_ref[0]
        d0, d1 = _dup_groups(cmp_tok) if j == 0 else _ones_groups(cmp_tok)
        out_ref[0, 0] = d0.astype(BF16)
        out_ref[0, 1] = d1.astype(BF16)

    for src_ref, k_out, v_out, gi in ((ksv_ref, ks_ref, vs_ref, 1), (kwv_ref, kw_ref, vw_ref, 2)):
        k = src_ref[0, :, 0:LANES].astype(F32)
        ms = _seg_mean(k * k, avg)
        k = k * lax.rsqrt(ms + EPS) * gain_ref[gi]
        d0, d1 = _dup_groups(k)
        k_out[0, 0] = d0.astype(BF16)
        k_out[0, 1] = d1.astype(BF16)
        d0, d1 = _ones_groups(src_ref[0, :, LANES:2 * LANES].astype(F32))
        v_out[0, 0] = d0.astype(BF16)
        v_out[0, 1] = d1.astype(BF16)


def _nsa_prep(proj3, pos, w1k, w1v, w2, gains):
    B, S, _ = proj3.shape
    n_c = S // CMP_STRIDE
    avg = jnp.asarray(_block_avg(LANES), BF16)
    col = lambda c: pl.BlockSpec((1, S, 2 * LANES), lambda b: (b, 0, c // (2 * LANES)))
    full = lambda a: pl.BlockSpec(a.shape, lambda b: (0,) * a.ndim)
    cmp_spec = pl.BlockSpec((1, NSA_GROUPS, n_c, LANES), lambda b: (b, 0, 0, 0))
    seq_spec = pl.BlockSpec((1, NSA_GROUPS, S, LANES), lambda b: (b, 0, 0, 0))
    cmp_shape = jax.ShapeDtypeStruct((B, NSA_GROUPS, n_c, LANES), BF16)
    seq_shape = jax.ShapeDtypeStruct((B, NSA_GROUPS, S, LANES), BF16)
    return pl.pallas_call(
        _nsa_prep_body,
        grid=(B,),
        in_specs=[col(COL_KVC), col(COL_KSV), col(COL_KWV), full(pos), full(w1k), full(w1v), full(w2), full(gains),
                  full(avg)],
        out_specs=[cmp_spec, cmp_spec, seq_spec, seq_spec, seq_spec, seq_spec],
        out_shape=[cmp_shape, cmp_shape, seq_shape, seq_shape, seq_shape, seq_shape],
        scratch_shapes=[pltpu.VMEM((S, LANES), F32), pltpu.VMEM((S, LANES), F32),
                        pltpu.VMEM((2, 2, CMP_STRIDE * LANES, NSA_GROUPS * CMP_HIDDEN), BF16)],
        compiler_params=_cparams(1),
        name="nsa_prep",
    )(proj3, proj3, proj3, pos, w1k, w1v, w2, gains, avg)


def _nsa_attn_body(q_ref, gate_ref, kcmp_ref, vcmp_ref, ks_ref, vs_ref, kw_ref, vw_ref,
                   qgain_ref, kgain_ref, ogain_ref, avgq_ref, avgo_ref, msct_ref, esel_ref, egate_ref, wbias_ref,
                   dbias_ref, cbias_ref,
                   o_ref, m_scr, acc_scr):
    tq = q_ref.shape[1]
    n_cmp = kcmp_ref.shape[2]
    n_sel = msct_ref.shape[0]
    kc_len = esel_ref.shape[2]
    rows = 4 * tq
    qi = pl.program_id(1)
    q0 = qi * tq

    q = q_ref[0].astype(F32)
    ms = _seg_mean(q * q, avgq_ref[...])
    qn = q * lax.rsqrt(ms + EPS) * qgain_ref[...] * (HEAD_DIM ** -0.5)

    gate_sig = _sigmoid(gate_ref[0].astype(F32)).astype(BF16)
    gates = [_dot(gate_sig, egate_ref[j]) for j in range(3)]

    lane_q = lax.broadcasted_iota(jnp.int32, (tq, LANES), 1)
    lo_q = lane_q < HEAD_DIM
    lo_r = lax.broadcasted_iota(jnp.int32, (rows, LANES), 1) < HEAD_DIM

    blk = lax.broadcasted_iota(jnp.int32, (n_sel, tq), 0)
    cur = lax.shift_right_logical(q0 + lax.broadcasted_iota(jnp.int32, (n_sel, tq), 1), int(np.log2(SEL_BLOCK)))
    forced = (blk == 0) | (blk == cur) | (blk == cur - 1)
    future = blk > cur
    blk_f = blk.astype(F32)

    def heads4(x):
        return jnp.concatenate([x] * 4, axis=0)

    def normalised_pairs(acc, guard):
        rolled = pltpu.roll(acc, HEAD_DIM, axis=1)
        den = jnp.where(lo_r, rolled, acc)
        if guard:
            den = jnp.maximum(den, 1e-30)
        out = []
        for p in range(2):
            ev = slice((2 * p) * tq, (2 * p + 1) * tq)
            od = slice((2 * p + 1) * tq, (2 * p + 2) * tq)
            out.append(jnp.where(lo_q, acc[ev] / den[ev], rolled[od] / den[od]))
        return out

    groups = range(NSA_GROUPS)
    qs = []
    for g in groups:
        slabs = [qn[:, (2 * g + p) * LANES:(2 * g + p + 1) * LANES] for p in range(2)]
        qs.append(jnp.concatenate(
            [jnp.where(lo_q, slabs[0], 0.0), jnp.where(lo_q, 0.0, slabs[0]),
             jnp.where(lo_q, slabs[1], 0.0), jnp.where(lo_q, 0.0, slabs[1])], axis=0).astype(BF16))

    def compressed_and_select(shift):
        s_c = [_dot_nt(qs[g], kcmp_ref[0, g]) for g in groups]
        if shift is None:
            r_c = lax.broadcasted_iota(jnp.int32, (rows, n_cmp), 0)
            c_c = lax.broadcasted_iota(jnp.int32, (rows, n_cmp), 1)
            cmask = (c_c * CMP_STRIDE + (CMP_BLOCK - 1)) <= q0 + (r_c & (tq - 1))
            s_c = [jnp.where(cmask, s_c[g], NEG_INF) for g in groups]
            e_c = [jnp.where(cmask, jnp.exp(s_c[g] - jnp.max(s_c[g], axis=-1, keepdims=True)), 0.0) for g in groups]
        else:
            bias = heads4(cbias_ref[qi] - shift)
            e_c = [jnp.exp(s_c[g] + bias) for g in groups]
        e_b = [e_c[g].astype(BF16) for g in groups]
        acc_c = [_dot(e_b[g], vcmp_ref[0, g]) for g in groups]
        ones = jnp.ones((8, n_cmp), BF16)
        imp = []
        for g in groups:
            num = [_dot_nt(msct_ref[...], e_b[g][h * tq:(h + 1) * tq]) for h in range(4)]
            den = [_dot_nt(ones, e_b[g][h * tq:(h + 1) * tq])[0:1] for h in range(4)]
            parts = [num[h] / jnp.maximum(den[h], 1e-30) for h in range(4)]
            imp.append((parts[0] + parts[1]) + (parts[2] + parts[3]))
        v = [jnp.where(forced, FORCE_SCORE, jnp.where(future, NEG_INF, imp[g])) for g in groups]
        sel = [jnp.zeros((n_sel, tq), F32) for g in groups]
        for _ in range(SEL_TOPK):
            mx = [jnp.max(v[g], axis=0, keepdims=True) for g in groups]
            first = [jnp.min(jnp.where(v[g] == mx[g], blk_f, float(LANES)), axis=0, keepdims=True) for g in groups]
            pick = [blk_f == first[g] for g in groups]
            sel = [jnp.where(pick[g], 1.0, sel[g]) for g in groups]
            v = [jnp.where(pick[g], BELOW_ALL, v[g]) for g in groups]
        return [normalised_pairs(acc_c[g], True) for g in groups], [sel[g].astype(BF16) for g in groups]

    n_before = lax.shift_right_logical(q0, int(np.log2(kc_len)))
    causal = dbias_ref[qi & (kc_len // tq - 1)]
    w0 = pl.multiple_of(jnp.maximum(q0 - WINDOW, 0), tq)
    n_w = WINDOW + tq
    w_case = jnp.minimum(qi, WINDOW // tq)

    def sel_keys(ref, g, kc):
        return ref[0, g, pl.ds(pl.multiple_of(kc * kc_len, kc_len), kc_len), :]

    def sel_scores(sel_b, g, kc, causal_bias, shift):
        chosen = _dot_tn(sel_b[g], esel_ref[kc])
        bias = (chosen - 1.0) * (-NEG_INF)
        if causal_bias is not None:
            bias = bias + causal_bias
        if shift is not None:
            bias = bias - shift
        return _dot_nt(qs[g], sel_keys(ks_ref, g, kc)) + heads4(bias)

    def win_scores(g, shift):
        bias = wbias_ref[w_case] if shift is None else wbias_ref[w_case] - shift
        return _dot_nt(qs[g], kw_ref[0, g, pl.ds(w0, n_w), :]) + heads4(bias)

    def win_values(g):
        return vw_ref[0, g, pl.ds(w0, n_w), :]

    def finish(cmp_s, acc_w):
        for g in groups:
            sel_s = normalised_pairs(acc_scr[g], False)
            win_s = normalised_pairs(acc_w[g], False)
            for p in range(2):
                cols = slice((2 * g + p) * LANES, (2 * g + p + 1) * LANES)
                mix = gates[0][:, cols] * cmp_s[g][p] + gates[1][:, cols] * sel_s[p] + gates[2][:, cols] * win_s[p]
                ms_o = _seg_mean(mix * mix, avgo_ref[...])
                o_ref[0, :, cols] = (mix * lax.rsqrt(ms_o + EPS) * ogain_ref[:, cols]).astype(BF16)

    def fixed_shift_path(shift):
        cmp_s, sel_b = compressed_and_select(shift)

        def probs(s):
            return jnp.exp(s).astype(BF16)

        for g in groups:
            acc_scr[g] = jnp.zeros(acc_scr.shape[1:], F32)

        def before(kc, carry):
            s = [sel_scores(sel_b, g, kc, None, shift) for g in groups]
            p = [probs(s[g]) for g in groups]
            for g in groups:
                acc_scr[g] = acc_scr[g] + _dot(p[g], sel_keys(vs_ref, g, kc))
            return carry

        lax.fori_loop(0, n_before, before, 0)
        s_d0 = sel_scores(sel_b, 0, n_before, causal, shift)
        s_d1 = sel_scores(sel_b, 1, n_before, causal, shift)
        p_d0 = probs(s_d0)
        s_w0 = win_scores(0, shift)
        acc_scr[0] = acc_scr[0] + _dot(p_d0, sel_keys(vs_ref, 0, n_before))
        p_d1 = probs(s_d1)
        s_w1 = win_scores(1, shift)
        acc_scr[1] = acc_scr[1] + _dot(p_d1, sel_keys(vs_ref, 1, n_before))
        p_w0 = probs(s_w0)
        acc_w0 = _dot(p_w0, win_values(0))
        p_w1 = probs(s_w1)
        acc_w1 = _dot(p_w1, win_values(1))
        finish(cmp_s, [acc_w0, acc_w1])

    def online_path():
        cmp_s, sel_b = compressed_and_select(None)
        for g in groups:
            m_scr[g] = jnp.full(m_scr.shape[1:], NEG_INF, F32)
            acc_scr[g] = jnp.zeros(acc_scr.shape[1:], F32)

        def sel_softmax(g, s):
            m_old = m_scr[g]
            m_new = jnp.maximum(m_old, jnp.max(s, axis=-1, keepdims=True))
            m_scr[g] = m_new
            return jnp.exp(s - m_new).astype(BF16), jnp.exp(m_old - m_new)

        def sel_accumulate(g, kc, p, alpha):
            acc_scr[g] = alpha * acc_scr[g] + _dot(p, sel_keys(vs_ref, g, kc))

        def win_softmax(s):
            return jnp.exp(s - jnp.max(s, axis=-1, keepdims=True)).astype(BF16)

        def before(kc, carry):
            s = [sel_scores(sel_b, g, kc, None, None) for g in groups]
            pa = [sel_softmax(g, s[g]) for g in groups]
            for g in groups:
                sel_accumulate(g, kc, *pa[g])
            return carry

        lax.fori_loop(0, n_before, before, 0)
        s_d0 = sel_scores(sel_b, 0, n_before, causal, None)
        s_d1 = sel_scores(sel_b, 1, n_before, causal, None)
        pa0 = sel_softmax(0, s_d0)
        s_w0 = win_scores(0, None)
        sel_accumulate(0, n_before, *pa0)
        pa1 = sel_softmax(1, s_d1)
        s_w1 = win_scores(1, None)
        sel_accumulate(1, n_before, *pa1)
        acc_w0 = _dot(win_softmax(s_w0), win_values(0))
        acc_w1 = _dot(win_softmax(s_w1), win_values(1))
        finish(cmp_s, [acc_w0, acc_w1])

    bound = 1.01 * (HEAD_DIM ** 0.5) * jnp.max(jnp.abs(qgain_ref[...])) * jnp.max(jnp.abs(kgain_ref[...]))
    safe = bound <= MAX_FIXED_SHIFT
    pl.when(safe)(lambda: fixed_shift_path(bound))
    pl.when(jnp.logical_not(safe))(online_path)


def _sel_from_cmp(n_cmp, n_sel):
    c0 = np.arange(n_cmp) * CMP_STRIDE
    s0 = np.arange(n_sel) * SEL_BLOCK
    ov = np.minimum(c0[None, :] + CMP_BLOCK, s0[:, None] + SEL_BLOCK) - np.maximum(c0[None, :], s0[:, None])
    m = (np.clip(ov, 0, None) / CMP_BLOCK).astype(np.float32)
    m[:, (np.arange(n_cmp) * CMP_STRIDE + CMP_BLOCK) > n_sel * SEL_BLOCK] = 0.0
    return m


def _nsa_attn(proj3, kcmp, vcmp, ks, vs, kw, vw, q_gain, k_gains, o_gain):
    B, S, _ = proj3.shape
    n_cmp = kcmp.shape[2]
    n_sel = S // SEL_BLOCK
    tq = NSA_TQ
    assert n_sel % 8 == 0 and S % SEL_KC == 0 and SEL_KC % tq == 0 and WINDOW % tq == 0 and S >= WINDOW + tq
    avgq = jnp.asarray(_block_avg(NSA_WIDTH), BF16)
    avgo = jnp.asarray(_block_avg(LANES), BF16)
    msct = jnp.asarray(_sel_from_cmp(n_cmp, n_sel), BF16)
    esel = (np.arange(n_sel)[:, None] == np.arange(S)[None, :] // SEL_BLOCK).astype(np.float32)
    esel = jnp.asarray(esel.reshape(n_sel, S // SEL_KC, SEL_KC).transpose(1, 0, 2), BF16)
    src = np.arange(LANES)[:, None]
    dst = np.arange(NSA_WIDTH)[None, :]
    egate = jnp.asarray(np.stack([(src == (dst // HEAD_DIM) * 3 + j) for j in range(3)]).astype(np.float32), BF16)
    r = np.arange(tq)[:, None]
    n_w = WINDOW + tq
    wcases = []
    for i in range(WINDOW // tq + 1):
        diff = (i * tq - max(i * tq - WINDOW, 0)) + r - np.arange(n_w)[None, :]
        wcases.append(np.where((diff >= 0) & (diff < WINDOW), 0.0, NEG_INF))
    wbias = jnp.asarray(np.stack(wcases), F32)
    dbias = jnp.asarray(np.stack([np.where(np.arange(SEL_KC)[None, :] <= i * tq + r, 0.0, NEG_INF)
                                  for i in range(SEL_KC // tq)]), F32)
    c_end = np.arange(n_cmp)[None, :] * CMP_STRIDE + (CMP_BLOCK - 1)
    cbias = jnp.asarray(np.stack([np.where(c_end <= i * tq + r, 0.0, NEG_INF) for i in range(S // tq)]), F32)

    full = lambda a: pl.BlockSpec(a.shape, lambda b, i: (0,) * a.ndim)
    per_b = lambda a: pl.BlockSpec((1,) + a.shape[1:], lambda b, i: (b,) + (0,) * (a.ndim - 1))
    return pl.pallas_call(
        _nsa_attn_body,
        grid=(B, S // tq),
        in_specs=[pl.BlockSpec((1, tq, NSA_WIDTH), lambda b, i: (b, i, COL_QA // NSA_WIDTH)),
                  pl.BlockSpec((1, tq, LANES), lambda b, i: (b, i, COL_GATE // LANES)),
                  per_b(kcmp), per_b(vcmp), per_b(ks), per_b(vs), per_b(kw), per_b(vw),
                  full(q_gain), full(k_gains), full(o_gain), full(avgq), full(avgo), full(msct), full(esel), full(egate),
                  full(wbias), full(dbias), full(cbias)],
        out_specs=pl.BlockSpec((1, tq, NSA_WIDTH), lambda b, i: (b, i, 0)),
        out_shape=jax.ShapeDtypeStruct((B, S, NSA_WIDTH), BF16),
        scratch_shapes=[pltpu.VMEM((NSA_GROUPS, 4 * tq, 1), F32), pltpu.VMEM((NSA_GROUPS, 4 * tq, LANES), F32)],
        compiler_params=_cparams(2),
        name="nsa_attn",
    )(proj3, proj3, kcmp, vcmp, ks, vs, kw, vw, q_gain, k_gains, o_gain, avgq, avgo, msct, esel, egate, wbias, dbias,
      cbias)


def _retention_body(q_ref, k_ref, v_ref, g_ref, cos_ref, sin_ref, decay_ref, xi_ref, zeta_ref, gammac_ref,
                    gain_ref, avg_ref, o_ref, state_scr):
    S = q_ref.shape[1]
    C = RET_CHUNK
    lane = lax.broadcasted_iota(jnp.int32, (C, LANES), 1)
    lo = lane < HEAD_DIM
    first_half = (lane & (HEAD_DIM - 1)) < HEAD_DIM // 2
    r = lax.broadcasted_iota(jnp.int32, (LANES, LANES), 0)
    c = lax.broadcasted_iota(jnp.int32, (LANES, LANES), 1)
    same_head = (r < HEAD_DIM) == (c < HEAD_DIM)
    avg = avg_ref[...]
    state_scr[...] = jnp.zeros(state_scr.shape, F32)

    def rope(x, cos, sin):
        swapped = jnp.where(first_half, pltpu.roll(x, LANES - HEAD_DIM // 2, axis=1),
                            pltpu.roll(x, HEAD_DIM // 2, axis=1))
        return x * cos + swapped * sin

    def chunk(n, carry):
        r0 = pl.multiple_of(n * C, C)
        cos = cos_ref[pl.ds(r0, C), :]
        sin = sin_ref[pl.ds(r0, C), :]
        pairs = range(RET_HEADS // 2)
        cols = [slice(p * LANES, (p + 1) * LANES) for p in pairs]
        q = [rope(q_ref[0, pl.ds(r0, C), cols[p]].astype(F32), cos, sin) for p in pairs]
        k = [rope(k_ref[0, pl.ds(r0, C), cols[p]].astype(F32), cos, sin) * (HEAD_DIM ** -0.5) for p in pairs]
        vb = [v_ref[0, pl.ds(r0, C), cols[p]] for p in pairs]
        kb = [k[p].astype(BF16) for p in pairs]
        inner = [_dot_nt(jnp.where(lo if half == 0 else ~lo, q[p], 0.0).astype(BF16), kb[p])
                 * decay_ref[2 * p + half] for p in pairs for half in range(2)]
        state = [state_scr[p] for p in pairs]
        cross = [_dot(q[p].astype(BF16), state[p].astype(BF16)) * xi_ref[p] for p in pairs]
        upd = [_dot_tn((k[p] * zeta_ref[p]).astype(BF16), vb[p]) for p in pairs]
        outs = [_dot(inner[i].astype(BF16), vb[i // 2]) for i in range(RET_HEADS)]
        for p in pairs:
            state_scr[p] = gammac_ref[p] * state[p] + jnp.where(same_head, upd[p], 0.0)
        y = jnp.concatenate([jnp.where(lo, outs[2 * p], outs[2 * p + 1]) + cross[p] for p in pairs], axis=0)
        mu = _seg_mean(y, avg)
        d = y - mu
        var = _seg_mean(d * d, avg)
        yn = d * lax.rsqrt(var + EPS)
        for p in pairs:
            gate = g_ref[0, pl.ds(r0, C), cols[p]].astype(F32)
            o_ref[0, pl.ds(r0, C), cols[p]] = (_silu(gate) * (yn[p * C:(p + 1) * C] * gain_ref[:, cols[p]])).astype(BF16)
        return carry

    lax.fori_loop(0, S // C, chunk, 0)


def _retention_tables(S):
    half = HEAD_DIM // 2
    inv_freq = ROPE_BASE ** (-jnp.arange(half, dtype=F32) / half)
    ang = jnp.arange(S, dtype=F32)[:, None] * inv_freq[None, :]
    cos, sin = jnp.cos(ang), jnp.sin(ang)
    cos_t = jnp.tile(cos, (1, 4))
    sin_t = jnp.tile(jnp.concatenate([-sin, sin], axis=1), (1, 2))
    C = RET_CHUNK
    H = RET_HEADS
    log_gamma = jnp.log1p(-jnp.power(2.0, -5.0 - jnp.arange(H, dtype=F32)))
    i = jnp.arange(C, dtype=F32)
    rel = i[:, None] - i[None, :]
    decay = jnp.where(rel >= 0, jnp.exp(jnp.maximum(rel, 0.0)[None] * log_gamma[:, None, None]), 0.0)
    xi = jnp.exp((i + 1.0)[:, None] * log_gamma[None, :])
    zeta = jnp.exp((C - 1.0 - i)[:, None] * log_gamma[None, :])
    gamma_c = jnp.exp(C * log_gamma)
    per_pair = lambda t: jnp.repeat(t.T.reshape(H // 2, 2, -1), HEAD_DIM, axis=1).transpose(0, 2, 1)
    gammac = jnp.repeat(gamma_c.reshape(H // 2, 2), HEAD_DIM, axis=1)[:, None, :]
    return cos_t, sin_t, decay, per_pair(xi), per_pair(zeta), gammac


def _retention(proj3, gain):
    B, S, _ = proj3.shape
    cos_t, sin_t, decay, xi, zeta, gammac = _retention_tables(S)
    avg = jnp.asarray(_block_avg(LANES), BF16)
    col = lambda c: pl.BlockSpec((1, S, RET_WIDTH), lambda b: (b, 0, c // RET_WIDTH))
    full = lambda a: pl.BlockSpec(a.shape, lambda b: (0,) * a.ndim)
    return pl.pallas_call(
        _retention_body,
        grid=(B,),
        in_specs=[col(COL_QR), col(COL_KR), col(COL_VR), col(COL_GR), full(cos_t), full(sin_t), full(decay),
                  full(xi), full(zeta), full(gammac), full(gain), full(avg)],
        out_specs=pl.BlockSpec((1, S, RET_WIDTH), lambda b: (b, 0, 0)),
        out_shape=jax.ShapeDtypeStruct((B, S, RET_WIDTH), BF16),
        scratch_shapes=[pltpu.VMEM((RET_HEADS // 2, LANES, LANES), F32)],
        compiler_params=_cparams(1),
        name="retention",
    )(proj3, proj3, proj3, proj3, cos_t, sin_t, decay, xi, zeta, gammac, gain, avg)


def _mem_prep_body(mem_ref, g_ref, wkv_ref, kgain_ref, avg_ref, k_ref, v_ref):
    hm = _rms_full(mem_ref[0], g_ref[...]).astype(BF16)
    kv = _dot(hm, wkv_ref[...])
    k = kv[:, :MEM_WIDTH]
    ms = _seg_mean(k * k, avg_ref[...])
    k_ref[0] = (k * lax.rsqrt(ms + EPS) * kgain_ref[...]).astype(BF16)
    v_ref[0] = kv[:, MEM_WIDTH:].astype(BF16)


def _mem_prep(mem, g, wkv, kgain):
    B, M, _ = mem.shape
    avg = jnp.asarray(_block_avg(MEM_WIDTH), BF16)
    full = lambda a: pl.BlockSpec(a.shape, lambda b: (0,) * a.ndim)
    out_spec = pl.BlockSpec((1, M, MEM_WIDTH), lambda b: (b, 0, 0))
    out_shape = jax.ShapeDtypeStruct((B, M, MEM_WIDTH), BF16)
    return pl.pallas_call(
        _mem_prep_body,
        grid=(B,),
        in_specs=[pl.BlockSpec((1, M, D_MODEL), lambda b: (b, 0, 0)), full(g), full(wkv), full(kgain), full(avg)],
        out_specs=[out_spec, out_spec],
        out_shape=[out_shape, out_shape],
        compiler_params=_cparams(1),
        name="mem_prep",
    )(mem, g, wkv, kgain, avg)


def _post_body(x_ref, oa_ref, ob_ref, wout_ref, mk_ref, mv_ref, gx_ref, wq_ref, qgain_ref, avg_ref, wo_ref,
               gf_ref, wr_ref, br_ref, tri_ref, x2_ref, h_ref, route_ref, count_ref):
    tm = x_ref.shape[1] // POST_CHAINS
    chains = range(POST_CHAINS)
    rows = [slice(c * tm, (c + 1) * tm) for c in chains]
    x1 = [x_ref[0, rows[c]] + _dot(oa_ref[0, rows[c]], wout_ref[0:NSA_WIDTH, :])
          + _dot(ob_ref[0, rows[c]], wout_ref[NSA_WIDTH:, :]) for c in chains]

    h = [_rms_full(x1[c], gx_ref[...]).astype(BF16) for c in chains]
    q = [_dot(h[c], wq_ref[...]) for c in chains]
    ms = [_seg_mean(q[c] * q[c], avg_ref[...]) for c in chains]
    q = [q[c] * lax.rsqrt(ms[c] + EPS) * qgain_ref[...] * (HEAD_DIM ** -0.5) for c in chains]
    lane = lax.broadcasted_iota(jnp.int32, (tm, LANES), 1)
    lo = lane < HEAD_DIM
    heads = [(c, p, half) for c in chains for p in range(MEM_HEADS // 2) for half in range(2)]
    s = [_dot_nt(jnp.where(lo if half == 0 else ~lo, q[c][:, p * LANES:(p + 1) * LANES], 0.0).astype(BF16),
                 mk_ref[0, :, p * LANES:(p + 1) * LANES]) for c, p, half in heads]
    e = [jnp.exp(s[i] - jnp.max(s[i], axis=-1, keepdims=True)) for i in range(len(heads))]
    pr = [(e[i] / jnp.sum(e[i], axis=-1, keepdims=True)).astype(BF16) for i in range(len(heads))]
    outs = [_dot(pr[i], mv_ref[0, :, heads[i][1] * LANES:(heads[i][1] + 1) * LANES]) for i in range(len(heads))]
    per_chain = MEM_HEADS
    o = [jnp.concatenate([jnp.where(lo, outs[c * per_chain + 2 * p], outs[c * per_chain + 2 * p + 1])
                          for p in range(MEM_HEADS // 2)], axis=1).astype(BF16) for c in chains]
    x2 = [x1[c] + _dot(o[c], wo_ref[...]) for c in chains]
    for c in chains:
        x2_ref[0, rows[c]] = x2[c]

    hf = [_rms_full(x2[c], gf_ref[...]).astype(BF16) for c in chains]
    for c in chains:
        h_ref[0, rows[c]] = hf[c]
    logits = [_dot(hf[c], wr_ref[...]) + br_ref[...] for c in chains]
    lane_f = lane.astype(F32)
    big = float(LANES)
    picks = []
    for c in chains:
        gl = jnp.where(lane < N_GROUPS, logits[c], BELOW_ALL)
        gmax = jnp.max(gl, axis=-1, keepdims=True)
        grp = jnp.min(jnp.where(gl == gmax, lane_f, big), axis=-1, keepdims=True)
        g_w = 1.0 / jnp.sum(jnp.where(lane < N_GROUPS, jnp.exp(gl - gmax), 0.0), axis=-1, keepdims=True)
        e_lo = N_GROUPS + grp * EXPERTS_PER_GROUP
        el = jnp.where((lane_f >= e_lo) & (lane_f < e_lo + EXPERTS_PER_GROUP), logits[c], BELOW_ALL)
        v0 = jnp.max(el, axis=-1, keepdims=True)
        i0 = jnp.min(jnp.where(el == v0, lane_f, big), axis=-1, keepdims=True)
        el = jnp.where(lane_f == i0, BELOW_ALL, el)
        v1 = jnp.max(el, axis=-1, keepdims=True)
        i1 = jnp.min(jnp.where(el == v1, lane_f, big), axis=-1, keepdims=True)
        e1 = jnp.exp(v1 - v0)
        picks.append((i0 - N_GROUPS, i1 - N_GROUPS, g_w / (1.0 + e1), g_w * e1 / (1.0 + e1)))

    hot = [[lane_f == picks[c][s] for s in range(2)] for c in chains]
    both = jnp.concatenate([jnp.where(hot[c][0], 1.0, 0.0) + jnp.where(hot[c][1], 1.0, 0.0) for c in chains], axis=0)
    before = _dot(tri_ref[...], both.astype(BF16))
    count_ref[0] = jnp.broadcast_to(jnp.sum(both, axis=0, keepdims=True), count_ref.shape[1:])
    for c in chains:
        e0, e1, w0, w1 = picks[c]
        r0 = jnp.sum(jnp.where(hot[c][0], before[rows[c]], 0.0), axis=-1, keepdims=True)
        r1 = jnp.sum(jnp.where(hot[c][1], before[rows[c]], 0.0), axis=-1, keepdims=True)
        cols = (e0, e1, w0, w1, r0, r1)
        route = jnp.zeros((tm, LANES), F32)
        for k in range(len(cols)):
            route = jnp.where(lane == k, cols[k], route)
        route_ref[0, rows[c]] = route


def _post(x, oa, ob, wout, mk, mv, gx, wq, qgain, wo, gf, wr, br):
    B, S, _ = x.shape
    tm = MOE_TM
    n_s = S // tm
    avg = jnp.asarray(_block_avg(MEM_WIDTH), BF16)
    tri = jnp.asarray(np.tril(np.ones((tm, tm), np.float32), -1), BF16)
    full = lambda a: pl.BlockSpec(a.shape, lambda b, i: (0,) * a.ndim)
    per_b = lambda a: pl.BlockSpec((1,) + a.shape[1:], lambda b, i: (b,) + (0,) * (a.ndim - 1))
    tile = lambda w: pl.BlockSpec((1, tm, w), lambda b, i: (b, i, 0))
    return pl.pallas_call(
        _post_body,
        grid=(B, n_s),
        in_specs=[tile(D_MODEL), tile(NSA_WIDTH), tile(RET_WIDTH), full(wout), per_b(mk), per_b(mv), full(gx),
                  full(wq), full(qgain), full(avg), full(wo), full(gf), full(wr), full(br), full(tri)],
        out_specs=[tile(D_MODEL), tile(D_MODEL), tile(LANES),
                   pl.BlockSpec((1, 8, LANES), lambda b, i: (b * n_s + i, 0, 0))],
        out_shape=[jax.ShapeDtypeStruct((B, S, D_MODEL), F32), jax.ShapeDtypeStruct((B, S, D_MODEL), BF16),
                   jax.ShapeDtypeStruct((B, S, LANES), F32), jax.ShapeDtypeStruct((B * n_s, 8, LANES), F32)],
        compiler_params=_cparams(2),
        name="post_mixer",
    )(x, oa, ob, wout, mk, mv, gx, wq, qgain, avg, wo, gf, wr, br, tri)


def _row_copy(src, dst, sem):
    return pltpu.make_async_copy(src, dst, sem)


def _run_pieces(n, max_piece, fn):
    b = RUN_ALIGN
    while b <= max_piece:
        pl.when((n & b) != 0)(functools.partial(fn, n & (-2 * b), b))
        b *= 2


def _move_groups(i, gmap_ref, n_loc, copy):
    n_groups = n_loc // RUN_ALIGN
    for j in range(n_groups):
        glob = pl.multiple_of(gmap_ref[i * n_groups + j], RUN_ALIGN)
        copy(pl.ds(j * RUN_ALIGN, RUN_ALIGN), pl.ds(glob, RUN_ALIGN)).start()


def _local_positions(route, loff_row):
    lane = lax.broadcasted_iota(jnp.int32, route.shape, 1).astype(F32)
    pos = []
    for s in range(2):
        base = jnp.sum(jnp.where(lane == route[:, s:s + 1], loff_row, 0.0), axis=-1, keepdims=True)
        pos.append(base + route[:, 4 + s:5 + s])
    return pos


def _scatter_body(gmap_ref, tstart_ref, tlen_ref, nact_ref,
                  h_ref, route_ref, lofff_ref, xs_ref, xloc, zbuf, sems):
    i = pl.program_id(0)
    tm = h_ref.shape[0]
    n_loc = xloc.shape[1]
    slot = i & 1
    sem = sems.at[0]

    def tails(start):
        def per_expert(e, carry):
            n = tlen_ref[e]
            st = tstart_ref[e]

            def piece(off, size):
                c = _row_copy(zbuf.at[pl.ds(0, size)], xs_ref.at[pl.ds(pl.multiple_of(st + off, RUN_ALIGN), size)], sem)
                c.start() if start else c.wait()

            _run_pieces(n, MOE_RB // 2, piece)
            return carry

        lax.fori_loop(0, N_EXPERTS, per_expert, 0)

    def unused(start):
        rows = zbuf.shape[0]

        def per_unit(u, carry):
            c = _row_copy(zbuf, xs_ref.at[pl.ds(pl.multiple_of(u * rows, rows), rows)], sem)
            c.start() if start else c.wait()
            return carry

        lax.fori_loop(nact_ref[0] * (MOE_RB // rows), xs_ref.shape[0] // rows, per_unit, 0)

    @pl.when(i == 0)
    def _():
        zbuf[...] = jnp.zeros(zbuf.shape, zbuf.dtype)
        tails(True)
        unused(True)
        tails(False)
        unused(False)

    pos = _local_positions(route_ref[...], lofff_ref[0, 0:1, :])
    col = lax.broadcasted_iota(jnp.int32, (tm, n_loc), 1).astype(F32)
    perm_t = jnp.where((col == pos[0]) | (col == pos[1]), 1.0, 0.0).astype(BF16)
    xloc[slot] = _dot_tn(perm_t, h_ref[...]).astype(BF16)

    def wait_slot(s):
        _row_copy(xloc.at[s], xs_ref.at[pl.ds(0, n_loc)], sems.at[s]).wait()

    pl.when(i > 0)(lambda: wait_slot(1 - slot))
    _move_groups(i, gmap_ref, n_loc, lambda loc, glob: _row_copy(xloc.at[slot, loc], xs_ref.at[glob], sems.at[slot]))
    pl.when(i == pl.num_programs(0) - 1)(lambda: wait_slot(slot))


def _scatter_rows(tables, h2d, route2d, loff_f, n_rows):
    T = h2d.shape[0]
    tm = MOE_TM
    n_loc = MOE_NLOC
    tile = lambda w: pl.BlockSpec((tm, w), lambda i, *_: (i, 0))
    grid_spec = pltpu.PrefetchScalarGridSpec(
        num_scalar_prefetch=4,
        grid=(T // tm,),
        in_specs=[tile(D_MODEL), tile(LANES), pl.BlockSpec((1, 8, LANES), lambda i, *_: (i, 0, 0))],
        out_specs=pl.BlockSpec(memory_space=pl.ANY),
        scratch_shapes=[pltpu.VMEM((2, n_loc, D_MODEL), BF16), pltpu.VMEM((MOE_RB // 2, D_MODEL), BF16),
                        pltpu.SemaphoreType.DMA((2,))],
    )
    return pl.pallas_call(
        _scatter_body,
        grid_spec=grid_spec,
        out_shape=jax.ShapeDtypeStruct((n_rows, D_MODEL), BF16),
        compiler_params=_cparams(1),
        name="moe_scatter",
    )(*tables, h2d, route2d, loff_f)


def _expert_body(blk0_ref, nblk_ref, n_act_ref, xs_ref, wg_ref, wu_ref, wd_ref, ys_ref,
                 wg_b, wu_b, wd_b, xbuf, ybuf, sem_in, sem_out):
    e = pl.program_id(0)
    n_slots, rb = xbuf.shape[0], xbuf.shape[1]
    ahead = n_slots - 2
    n_act = n_act_ref[0]
    b0 = blk0_ref[e]

    def rows(g):
        return pl.ds(pl.multiple_of(g * rb, rb), rb)

    def x_copy(g, slot):
        return _row_copy(xs_ref.at[rows(g)], xbuf.at[slot], sem_in.at[slot])

    def y_copy(g, slot):
        return _row_copy(ybuf.at[slot], ys_ref.at[rows(g)], sem_out.at[slot])

    @pl.when(e == 0)
    def _():
        for k in range(ahead):
            pl.when(k < n_act)(lambda k=k: x_copy(k, k).start())

    wg_b[...] = wg_ref[0].astype(BF16)
    wu_b[...] = wu_ref[0].astype(BF16)
    wd_b[...] = wd_ref[0].astype(BF16)

    def blocks(g0, count):
        gs = [g0 + c for c in range(count)]
        slots = [g & (n_slots - 1) for g in gs]
        for c in range(count):
            x_copy(gs[c], slots[c]).wait()
        for c in range(count):
            nxt = gs[c] + ahead
            pl.when(nxt < n_act)(lambda nxt=nxt: x_copy(nxt, nxt & (n_slots - 1)).start())
        x = [xbuf[slots[c]] for c in range(count)]
        a = [_dot(x[c], wg_b[...]) for c in range(count)]
        b = [_dot(x[c], wu_b[...]) for c in range(count)]
        h = [(_silu(a[c]) * b[c]).astype(BF16) for c in range(count)]
        y = [_dot(h[c], wd_b[...]).astype(BF16) for c in range(count)]
        for c in range(count):
            pl.when(gs[c] >= n_slots)(lambda c=c: y_copy(gs[c] - n_slots, slots[c]).wait())
        for c in range(count):
            ybuf[slots[c]] = y[c]
            y_copy(gs[c], slots[c]).start()

    nb = nblk_ref[e]

    def pair(j, carry):
        blocks(b0 + 2 * j, 2)
        return carry

    lax.fori_loop(0, lax.shift_right_logical(nb, 1), pair, 0)
    pl.when((nb & 1) == 1)(lambda: blocks(b0 + nb - 1, 1))

    @pl.when(e == pl.num_programs(0) - 1)
    def _():
        for k in range(1, n_slots + 1):
            pl.when(n_act >= k)(lambda k=k: y_copy(n_act - k, (n_act - k) & (n_slots - 1)).wait())
        ybuf[0] = jnp.zeros(ybuf.shape[1:], ybuf.dtype)
        n_blocks = ys_ref.shape[0] // rb

        def fill(start):
            def per_block(g, carry):
                c = y_copy(g, 0)
                c.start() if start else c.wait()
                return carry

            lax.fori_loop(n_act, n_blocks, per_block, 0)

        fill(True)
        fill(False)


def _experts(blk0, nblk, n_act, xs, n_rows, wg, wu, wd):
    rb = MOE_RB
    weight = lambda a: pl.BlockSpec((1,) + a.shape[1:], lambda e, *_: (e, 0, 0))
    grid_spec = pltpu.PrefetchScalarGridSpec(
        num_scalar_prefetch=3,
        grid=(N_EXPERTS,),
        in_specs=[pl.BlockSpec(memory_space=pl.ANY), weight(wg), weight(wu), weight(wd)],
        out_specs=pl.BlockSpec(memory_space=pl.ANY),
        scratch_shapes=[pltpu.VMEM((D_MODEL, EXPERT_FF), BF16), pltpu.VMEM((D_MODEL, EXPERT_FF), BF16),
                        pltpu.VMEM((EXPERT_FF, D_MODEL), BF16), pltpu.VMEM((MOE_SLOTS, rb, D_MODEL), BF16),
                        pltpu.VMEM((MOE_SLOTS, rb, D_MODEL), BF16), pltpu.SemaphoreType.DMA((MOE_SLOTS,)),
                        pltpu.SemaphoreType.DMA((MOE_SLOTS,))],
    )
    return pl.pallas_call(
        _expert_body,
        grid_spec=grid_spec,
        out_shape=jax.ShapeDtypeStruct((n_rows, D_MODEL), BF16),
        compiler_params=_cparams(1),
        name="moe_experts",
    )(blk0, nblk, n_act, xs, wg, wu, wd)


def _combine_body(gmap_ref, x_ref, route_ref, lofff_ref, ys_ref, o_ref, yloc, sems):
    i = pl.program_id(0)
    tm = x_ref.shape[0]
    n_loc = yloc.shape[1]
    slot = i & 1

    def fetch(tile, s):
        _move_groups(tile, gmap_ref, n_loc, lambda loc, glob: _row_copy(ys_ref.at[glob], yloc.at[s, loc], sems.at[s]))

    pl.when(i == 0)(lambda: fetch(i, slot))
    _row_copy(ys_ref.at[pl.ds(0, n_loc)], yloc.at[slot], sems.at[slot]).wait()
    pl.when(i + 1 < pl.num_programs(0))(lambda: fetch(i + 1, 1 - slot))

    route = route_ref[...]
    pos = _local_positions(route, lofff_ref[0, 0:1, :])
    col = lax.broadcasted_iota(jnp.int32, (tm, n_loc), 1).astype(F32)
    perm_w = (jnp.where(col == pos[0], route[:, 2:3], 0.0) + jnp.where(col == pos[1], route[:, 3:4], 0.0)).astype(BF16)
    o_ref[...] = x_ref[...] + _dot(perm_w, yloc[slot])


def _combine(tables, x2d, route2d, loff_f, ys):
    T = x2d.shape[0]
    tm = MOE_TM
    n_loc = MOE_NLOC
    tile = lambda w: pl.BlockSpec((tm, w), lambda i, *_: (i, 0))
    grid_spec = pltpu.PrefetchScalarGridSpec(
        num_scalar_prefetch=1,
        grid=(T // tm,),
        in_specs=[tile(D_MODEL), tile(LANES), pl.BlockSpec((1, 8, LANES), lambda i, *_: (i, 0, 0)),
                  pl.BlockSpec(memory_space=pl.ANY)],
        out_specs=tile(D_MODEL),
        scratch_shapes=[pltpu.VMEM((2, n_loc, D_MODEL), BF16), pltpu.SemaphoreType.DMA((2,))],
    )
    return pl.pallas_call(
        _combine_body,
        grid_spec=grid_spec,
        out_shape=jax.ShapeDtypeStruct((T, D_MODEL), F32),
        compiler_params=_cparams(1),
        name="moe_combine",
    )(*tables, x2d, route2d, loff_f, ys)


def _compress_weights(pos, w2):
    eye = jnp.eye(NSA_GROUPS, dtype=F32)
    w2b = jnp.einsum('hd,gk->ghkd', w2, eye).reshape(NSA_GROUPS * CMP_HIDDEN, LANES)
    posb = jnp.tile(pos, (1, NSA_GROUPS)).reshape(2, 1, CMP_STRIDE * LANES)
    return posb, w2b.astype(BF16)


def _dup2(g):
    return jnp.tile(g.reshape(1, HEAD_DIM), (1, 2))


def kernel(x, mem, mix_norm, w_in, nsa_q_norm, nsa_kcmp_norm, nsa_ksel_norm, nsa_kwin_norm, cmp_pos_k, cmp_pos_v, cmp_k_w1, cmp_k_w2, cmp_v_w1, cmp_v_w2, nsa_out_norm, ret_out_norm, w_out, mem_x_norm, mem_kv_norm, mem_wq, mem_wkv, mem_q_norm, mem_k_norm, mem_wo, ffn_norm, router_group_w, router_group_b, router_expert_w, router_expert_b, exp_w_gate, exp_w_up, exp_w_down):
    B, S, D = x.shape
    T = B * S
    depth = mix_norm.shape[0]
    for l in range(depth):
        proj = _proj(x.reshape(T, D), mix_norm[l].reshape(1, D), w_in[l].T).reshape(B, S, PROJ_PAD)
        pk, w2k = _compress_weights(cmp_pos_k[l], cmp_k_w2[l])
        pv, w2v = _compress_weights(cmp_pos_v[l], cmp_v_w2[l])
        gains = jnp.stack([_dup2(nsa_kcmp_norm[l]), _dup2(nsa_ksel_norm[l]), _dup2(nsa_kwin_norm[l])])
        kcmp, vcmp, ks, vs, kw, vw = _nsa_prep(proj, jnp.stack([pk, pv]), cmp_k_w1[l], cmp_v_w1[l],
                                               jnp.stack([w2k, w2v]), gains)
        o_a = _nsa_attn(proj, kcmp, vcmp, ks, vs, kw, vw,
                        jnp.tile(nsa_q_norm[l].reshape(1, HEAD_DIM), (1, NSA_HEADS)), gains,
                        nsa_out_norm[l].reshape(1, NSA_WIDTH))
        o_b = _retention(proj, ret_out_norm[l].reshape(1, RET_WIDTH))
        mk, mv = _mem_prep(mem, mem_kv_norm[l].reshape(1, D), mem_wkv[l].astype(BF16),
                           jnp.tile(mem_k_norm[l].reshape(1, HEAD_DIM), (1, MEM_HEADS)))
        w_r = jnp.concatenate([router_group_w[l],
                               router_expert_w[l].transpose(1, 0, 2).reshape(D, N_EXPERTS),
                               jnp.zeros((D, LANES - N_GROUPS - N_EXPERTS), F32)], axis=1).astype(BF16)
        b_r = jnp.concatenate([router_group_b[l], router_expert_b[l].reshape(N_EXPERTS),
                               jnp.zeros((LANES - N_GROUPS - N_EXPERTS,), F32)]).reshape(1, LANES)
        x2, hf, route, counts = _post(
            x, o_a, o_b, w_out[l].astype(BF16), mk, mv, mem_x_norm[l].reshape(1, D), mem_wq[l].astype(BF16),
            jnp.tile(mem_q_norm[l].reshape(1, HEAD_DIM), (1, MEM_HEADS)), mem_wo[l].astype(BF16),
            ffn_norm[l].reshape(1, D), w_r, b_r)
        route2d = route.reshape(T, LANES)
        n_tiles = T // MOE_TM
        cnt = counts[:, 0, :N_EXPERTS].astype(jnp.int32)
        cnt = (cnt + RUN_ALIGN - 1) // RUN_ALIGN * RUN_ALIGN
        loff = jnp.cumsum(cnt, axis=1) - cnt
        total = jnp.sum(cnt, axis=0)
        padded = (total + MOE_RB - 1) // MOE_RB * MOE_RB
        pend = jnp.cumsum(padded)
        pstart = pend - padded
        goff = pstart[None, :] + jnp.cumsum(cnt, axis=0) - cnt
        n_rows = 2 * T + n_tiles * N_EXPERTS * RUN_ALIGN + N_EXPERTS * MOE_RB
        n_act = (pend[-1:] // MOE_RB).astype(jnp.int32)
        loff_f = jnp.broadcast_to(jnp.pad(loff.astype(F32), ((0, 0), (0, LANES - N_EXPERTS)))[:, None, :],
                                  (n_tiles, 8, LANES))
        grp_row = jnp.arange(MOE_NLOC // RUN_ALIGN, dtype=jnp.int32) * RUN_ALIGN
        inside = ((loff[:, None, :] <= grp_row[None, :, None])
                  & (grp_row[None, :, None] < (loff + cnt)[:, None, :])).astype(jnp.int32)
        shift = jnp.sum(inside * (goff - loff)[:, None, :], axis=2)
        used = jnp.sum(inside, axis=2) > 0
        gmap_scatter = jnp.where(used, shift + grp_row[None, :], n_rows + grp_row[None, :]).reshape(-1)
        gmap_gather = jnp.where(used, shift + grp_row[None, :], 0).reshape(-1)
        xs = _scatter_rows((gmap_scatter, pstart + total, padded - total, n_act), hf.reshape(T, D), route2d, loff_f,
                           n_rows + MOE_NLOC)
        ys = _experts(pstart // MOE_RB, padded // MOE_RB, n_act, xs, n_rows,
                      exp_w_gate[l], exp_w_up[l], exp_w_down[l])
        x = _combine((gmap_gather,), x2.reshape(T, D), route2d, loff_f, ys).reshape(B, S, D)
    return x
```

```python
import functools

import numpy as np
import jax
import jax.numpy as jnp
from jax import lax
from jax.experimental import pallas as pl
from jax.experimental.pallas import tpu as pltpu

F32 = jnp.float32
BF16 = jnp.bfloat16

D_MODEL = 1024
HEAD_DIM = 64
LANES = 128
NSA_HEADS = 8
NSA_GROUPS = 2
NSA_WIDTH = NSA_HEADS * HEAD_DIM
CMP_BLOCK = 32
CMP_STRIDE = 16
CMP_HIDDEN = 2 * HEAD_DIM
SEL_BLOCK = 64
SEL_TOPK = 8
WINDOW = 512
RET_HEADS = 8
RET_WIDTH = RET_HEADS * HEAD_DIM
RET_CHUNK = 128
ROPE_BASE = 10000.0
MEM_HEADS = 4
MEM_WIDTH = MEM_HEADS * HEAD_DIM
N_GROUPS = 4
EXPERTS_PER_GROUP = 8
N_EXPERTS = N_GROUPS * EXPERTS_PER_GROUP
EXPERT_FF = D_MODEL // 4
EPS = 1e-6
NEG_INF = -1e30
FORCE_SCORE = 1e9
BELOW_ALL = -3e38
MAX_FIXED_SHIFT = 40.0

COL_QA = 0
COL_QR, COL_KR, COL_VR, COL_GR = 512, 1024, 1536, 2048
COL_KVC, COL_KSV, COL_KWV = 2560, 2816, 3072
COL_GATE = 3328
PROJ_PAD = 3456
PROJ_SPLITS = (NSA_WIDTH, NSA_WIDTH + 6 * NSA_GROUPS * HEAD_DIM, NSA_WIDTH + 6 * NSA_GROUPS * HEAD_DIM + 3 * NSA_HEADS)

PROJ_TM = 512
NSA_TQ = 256
SEL_KC = 512
POST_CHAINS = 2
MOE_TM = 512
MOE_RB = 256
MOE_SLOTS = 8
RUN_ALIGN = 16
MOE_NLOC = 2 * MOE_TM + N_EXPERTS * RUN_ALIGN
VMEM_LIMIT = 56 * 1024 * 1024


def _cparams(n_axes):
    return pltpu.CompilerParams(dimension_semantics=("arbitrary",) * n_axes,
                                vmem_limit_bytes=VMEM_LIMIT)


def _dot(a, b):
    return jnp.dot(a, b, preferred_element_type=F32)


def _dot_nt(a, b):
    return lax.dot_general(a, b, (((1,), (1,)), ((), ())), preferred_element_type=F32)


def _dot_tn(a, b):
    return lax.dot_general(a, b, (((0,), (0,)), ((), ())), preferred_element_type=F32)


def _rms_full(x, g):
    ms = jnp.mean(x * x, axis=-1, keepdims=True)
    return x * lax.rsqrt(ms + EPS) * g


def _seg_mean(x, avg):
    return _dot(x.astype(BF16), avg)


def _silu(x):
    return x * (1.0 / (1.0 + jnp.exp(-x)))


def _sigmoid(x):
    return 1.0 / (1.0 + jnp.exp(-x))


def _block_avg(width):
    i = np.arange(width)
    return ((i[:, None] // HEAD_DIM == i[None, :] // HEAD_DIM) / HEAD_DIM).astype(np.float32)


def _proj_body(x_ref, g_ref, wt_ref, o_ref, w_scr):
    @pl.when(pl.program_id(0) == 0)
    def _():
        kv0, gate0, ret0 = PROJ_SPLITS
        w_scr[COL_QA:COL_QR] = wt_ref[0:kv0].astype(BF16)
        w_scr[COL_QR:COL_KVC] = wt_ref[ret0:].astype(BF16)
        w_scr[COL_KVC:COL_GATE] = wt_ref[kv0:gate0].astype(BF16)
        pad = jnp.zeros((PROJ_PAD - COL_GATE - (ret0 - gate0), wt_ref.shape[1]), F32)
        w_scr[COL_GATE:] = jnp.concatenate([wt_ref[gate0:ret0], pad], axis=0).astype(BF16)

    h = _rms_full(x_ref[...], g_ref[...]).astype(BF16)
    step = PROJ_PAD // 3
    for j in range(3):
        o_ref[:, j * step:(j + 1) * step] = _dot_nt(h, w_scr[j * step:(j + 1) * step]).astype(BF16)


def _proj(x2d, g, w_t):
    T = x2d.shape[0]
    assert w_t.shape[0] == PROJ_SPLITS[2] + 4 * RET_WIDTH
    return pl.pallas_call(
        _proj_body,
        grid=(T // PROJ_TM,),
        in_specs=[pl.BlockSpec((PROJ_TM, D_MODEL), lambda i: (i, 0)),
                  pl.BlockSpec((1, D_MODEL), lambda i: (0, 0)),
                  pl.BlockSpec(w_t.shape, lambda i: (0, 0))],
        out_specs=pl.BlockSpec((PROJ_TM, PROJ_PAD), lambda i: (i, 0)),
        out_shape=jax.ShapeDtypeStruct((T, PROJ_PAD), BF16),
        scratch_shapes=[pltpu.VMEM((PROJ_PAD, D_MODEL), BF16)],
        compiler_params=_cparams(1),
        name="proj",
    )(x2d, g, w_t)


def _dup_groups(x):
    lane = lax.broadcasted_iota(jnp.int32, x.shape, 1)
    xs = pltpu.roll(x, HEAD_DIM, axis=1)
    lo = lane < HEAD_DIM
    return jnp.where(lo, x, xs), jnp.where(lo, xs, x)


def _ones_groups(x):
    lane = lax.broadcasted_iota(jnp.int32, x.shape, 1)
    lo = lane < HEAD_DIM
    return jnp.where(lo, x, 1.0), jnp.where(lo, pltpu.roll(x, HEAD_DIM, axis=1), 1.0)


def _nsa_prep_body(kvc_ref, ksv_ref, kwv_ref, pos_ref, w1k_ref, w1v_ref, w2_ref, gain_ref, avg_ref,
                   kcmp_ref, vcmp_ref, ks_ref, vs_ref, kw_ref, vw_ref, scr_k, scr_v, w1_ref):
    @pl.when(pl.program_id(0) == 0)
    def _():
        zero = jnp.zeros((HEAD_DIM, CMP_HIDDEN), BF16)
        for j, src in ((0, w1k_ref), (1, w1v_ref)):
            for l in range(CMP_BLOCK):
                piece = src[l * HEAD_DIM:(l + 1) * HEAD_DIM, :].astype(BF16)
                r0 = (l % CMP_STRIDE) * LANES
                w1_ref[j, l // CMP_STRIDE, r0:r0 + HEAD_DIM, :] = jnp.concatenate([piece, zero], axis=1)
                w1_ref[j, l // CMP_STRIDE, r0 + HEAD_DIM:r0 + LANES, :] = jnp.concatenate([zero, piece], axis=1)

    avg = avg_ref[...]
    n_c = scr_k.shape[0] // CMP_STRIDE
    scr_k[...] = kvc_ref[0, :, 0:LANES].astype(F32)
    scr_v[...] = kvc_ref[0, :, LANES:2 * LANES].astype(F32)
    for j, out_ref, scr in ((0, kcmp_ref, scr_k), (1, vcmp_ref, scr_v)):
        ycat = jnp.concatenate(
            [scr[pl.ds(l, n_c, stride=CMP_STRIDE), :] for l in range(CMP_STRIDE)], axis=1)
        first = _dot((ycat + pos_ref[j, 0]).astype(BF16), w1_ref[j, 0])
        second = _dot((ycat + pos_ref[j, 1]).astype(BF16), w1_ref[j, 1])
        hidden = first + pltpu.roll(second, n_c - 1, axis=0)
        cmp_tok = _dot(_silu(hidden).astype(BF16), w2_ref[j])
        if j == 0:
            ms = _seg_mean(cmp_tok * cmp_tok, avg)
            cmp_tok = cmp_tok * lax.rsqrt(ms + EPS) * gain_ref[0]
        d0, d1 = _dup_groups(cmp_tok) if j == 0 else _ones_groups(cmp_tok)
        out_ref[0, 0] = d0.astype(BF16)
        out_ref[0, 1] = d1.astype(BF16)

    for src_ref, k_out, v_out, gi in ((ksv_ref, ks_ref, vs_ref, 1), (kwv_ref, kw_ref, vw_ref, 2)):
        k = src_ref[0, :, 0:LANES].astype(F32)
        ms = _seg_mean(k * k, avg)
        k = k * lax.rsqrt(ms + EPS) * gain_ref[gi]
        d0, d1 = _dup_groups(k)
        k_out[0, 0] = d0.astype(BF16)
        k_out[0, 1] = d1.astype(BF16)
        d0, d1 = _ones_groups(src_ref[0, :, LANES:2 * LANES].astype(F32))
        v_out[0, 0] = d0.astype(BF16)
        v_out[0, 1] = d1.astype(BF16)


def _nsa_prep(proj3, pos, w1k, w1v, w2, gains):
    B, S, _ = proj3.shape
    n_c = S // CMP_STRIDE
    avg = jnp.asarray(_block_avg(LANES), BF16)
    col = lambda c: pl.BlockSpec((1, S, 2 * LANES), lambda b: (b, 0, c // (2 * LANES)))
    full = lambda a: pl.BlockSpec(a.shape, lambda b: (0,) * a.ndim)
    cmp_spec = pl.BlockSpec((1, NSA_GROUPS, n_c, LANES), lambda b: (b, 0, 0, 0))
    seq_spec = pl.BlockSpec((1, NSA_GROUPS, S, LANES), lambda b: (b, 0, 0, 0))
    cmp_shape = jax.ShapeDtypeStruct((B, NSA_GROUPS, n_c, LANES), BF16)
    seq_shape = jax.ShapeDtypeStruct((B, NSA_GROUPS, S, LANES), BF16)
    return pl.pallas_call(
        _nsa_prep_body,
        grid=(B,),
        in_specs=[col(COL_KVC), col(COL_KSV), col(COL_KWV), full(pos), full(w1k), full(w1v), full(w2), full(gains),
                  full(avg)],
        out_specs=[cmp_spec, cmp_spec, seq_spec, seq_spec, seq_spec, seq_spec],
        out_shape=[cmp_shape, cmp_shape, seq_shape, seq_shape, seq_shape, seq_shape],
        scratch_shapes=[pltpu.VMEM((S, LANES), F32), pltpu.VMEM((S, LANES), F32),
                        pltpu.VMEM((2, 2, CMP_STRIDE * LANES, NSA_GROUPS * CMP_HIDDEN), BF16)],
        compiler_params=_cparams(1),
        name="nsa_prep",
    )(proj3, proj3, proj3, pos, w1k, w1v, w2, gains, avg)


def _nsa_attn_body(q_ref, gate_ref, kcmp_ref, vcmp_ref, ks_ref, vs_ref, kw_ref, vw_ref,
                   qgain_ref, kgain_ref, ogain_ref, avgq_ref, avgo_ref, msct_ref, esel_ref, egate_ref, wbias_ref,
                   dbias_ref, cbias_ref,
                   o_ref, m_scr, acc_scr):
    tq = q_ref.shape[1]
    n_cmp = kcmp_ref.shape[2]
    n_sel = msct_ref.shape[0]
    kc_len = esel_ref.shape[2]
    rows = 4 * tq
    qi = pl.program_id(1)
    q0 = qi * tq

    q = q_ref[0].astype(F32)
    ms = _seg_mean(q * q, avgq_ref[...])
    qn = q * lax.rsqrt(ms + EPS) * qgain_ref[...] * (HEAD_DIM ** -0.5)

    gate_sig = _sigmoid(gate_ref[0].astype(F32)).astype(BF16)
    gates = [_dot(gate_sig, egate_ref[j]) for j in range(3)]

    lane_q = lax.broadcasted_iota(jnp.int32, (tq, LANES), 1)
    lo_q = lane_q < HEAD_DIM
    lo_r = lax.broadcasted_iota(jnp.int32, (rows, LANES), 1) < HEAD_DIM

    blk = lax.broadcasted_iota(jnp.int32, (n_sel, tq), 0)
    cur = lax.shift_right_logical(q0 + lax.broadcasted_iota(jnp.int32, (n_sel, tq), 1), int(np.log2(SEL_BLOCK)))
    forced = (blk == 0) | (blk == cur) | (blk == cur - 1)
    future = blk > cur
    blk_f = blk.astype(F32)

    def heads4(x):
        return jnp.concatenate([x] * 4, axis=0)

    def normalised_pairs(acc, guard):
        rolled = pltpu.roll(acc, HEAD_DIM, axis=1)
        den = jnp.where(lo_r, rolled, acc)
        if guard:
            den = jnp.maximum(den, 1e-30)
        out = []
        for p in range(2):
            ev = slice((2 * p) * tq, (2 * p + 1) * tq)
            od = slice((2 * p + 1) * tq, (2 * p + 2) * tq)
            out.append(jnp.where(lo_q, acc[ev] / den[ev], rolled[od] / den[od]))
        return out

    groups = range(NSA_GROUPS)
    qs = []
    for g in groups:
        slabs = [qn[:, (2 * g + p) * LANES:(2 * g + p + 1) * LANES] for p in range(2)]
        qs.append(jnp.concatenate(
            [jnp.where(lo_q, slabs[0], 0.0), jnp.where(lo_q, 0.0, slabs[0]),
             jnp.where(lo_q, slabs[1], 0.0), jnp.where(lo_q, 0.0, slabs[1])], axis=0).astype(BF16))

    def compressed_and_select(shift):
        s_c = [_dot_nt(qs[g], kcmp_ref[0, g]) for g in groups]
        if shift is None:
            r_c = lax.broadcasted_iota(jnp.int32, (rows, n_cmp), 0)
            c_c = lax.broadcasted_iota(jnp.int32, (rows, n_cmp), 1)
            cmask = (c_c * CMP_STRIDE + (CMP_BLOCK - 1)) <= q0 + (r_c & (tq - 1))
            s_c = [jnp.where(cmask, s_c[g], NEG_INF) for g in groups]
            e_c = [jnp.where(cmask, jnp.exp(s_c[g] - jnp.max(s_c[g], axis=-1, keepdims=True)), 0.0) for g in groups]
        else:
            bias = heads4(cbias_ref[qi] - shift)
            e_c = [jnp.exp(s_c[g] + bias) for g in groups]
        e_b = [e_c[g].astype(BF16) for g in groups]
        acc_c = [_dot(e_b[g], vcmp_ref[0, g]) for g in groups]
        ones = jnp.ones((8, n_cmp), BF16)
        imp = []
        for g in groups:
            num = [_dot_nt(msct_ref[...], e_b[g][h * tq:(h + 1) * tq]) for h in range(4)]
            den = [_dot_nt(ones, e_b[g][h * tq:(h + 1) * tq])[0:1] for h in range(4)]
            parts = [num[h] / jnp.maximum(den[h], 1e-30) for h in range(4)]
            imp.append((parts[0] + parts[1]) + (parts[2] + parts[3]))
        v = [jnp.where(forced, FORCE_SCORE, jnp.where(future, NEG_INF, imp[g])) for g in groups]
        sel = [jnp.zeros((n_sel, tq), F32) for g in groups]
        for _ in range(SEL_TOPK):
            mx = [jnp.max(v[g], axis=0, keepdims=True) for g in groups]
            first = [jnp.min(jnp.where(v[g] == mx[g], blk_f, float(LANES)), axis=0, keepdims=True) for g in groups]
            pick = [blk_f == first[g] for g in groups]
            sel = [jnp.where(pick[g], 1.0, sel[g]) for g in groups]
            v = [jnp.where(pick[g], BELOW_ALL, v[g]) for g in groups]
        return [normalised_pairs(acc_c[g], True) for g in groups], [sel[g].astype(BF16) for g in groups]

    n_before = lax.shift_right_logical(q0, int(np.log2(kc_len)))
    causal = dbias_ref[qi & (kc_len // tq - 1)]
    w0 = pl.multiple_of(jnp.maximum(q0 - WINDOW, 0), tq)
    n_w = WINDOW + tq
    w_case = jnp.minimum(qi, WINDOW // tq)

    def sel_keys(ref, g, kc):
        return ref[0, g, pl.ds(pl.multiple_of(kc * kc_len, kc_len), kc_len), :]

    def sel_scores(sel_b, g, kc, causal_bias, shift):
        chosen = _dot_tn(sel_b[g], esel_ref[kc])
        bias = (chosen - 1.0) * (-NEG_INF)
        if causal_bias is not None:
            bias = bias + causal_bias
        if shift is not None:
            bias = bias - shift
        return _dot_nt(qs[g], sel_keys(ks_ref, g, kc)) + heads4(bias)

    def win_scores(g, shift):
        bias = wbias_ref[w_case] if shift is None else wbias_ref[w_case] - shift
        return _dot_nt(qs[g], kw_ref[0, g, pl.ds(w0, n_w), :]) + heads4(bias)

    def win_values(g):
        return vw_ref[0, g, pl.ds(w0, n_w), :]

    def finish(cmp_s, acc_w):
        for g in groups:
            sel_s = normalised_pairs(acc_scr[g], False)
            win_s = normalised_pairs(acc_w[g], False)
            for p in range(2):
                cols = slice((2 * g + p) * LANES, (2 * g + p + 1) * LANES)
                mix = gates[0][:, cols] * cmp_s[g][p] + gates[1][:, cols] * sel_s[p] + gates[2][:, cols] * win_s[p]
                ms_o = _seg_mean(mix * mix, avgo_ref[...])
                o_ref[0, :, cols] = (mix * lax.rsqrt(ms_o + EPS) * ogain_ref[:, cols]).astype(BF16)

    def fixed_shift_path(shift):
        cmp_s, sel_b = compressed_and_select(shift)

        def probs(s):
            return jnp.exp(s).astype(BF16)

        for g in groups:
            acc_scr[g] = jnp.zeros(acc_scr.shape[1:], F32)

        def before(kc, carry):
            s = [sel_scores(sel_b, g, kc, None, shift) for g in groups]
            p = [probs(s[g]) for g in groups]
            for g in groups:
                acc_scr[g] = acc_scr[g] + _dot(p[g], sel_keys(vs_ref, g, kc))
            return carry

        lax.fori_loop(0, n_before, before, 0)
        s_d0 = sel_scores(sel_b, 0, n_before, causal, shift)
        s_d1 = sel_scores(sel_b, 1, n_before, causal, shift)
        p_d0 = probs(s_d0)
        s_w0 = win_scores(0, shift)
        acc_scr[0] = acc_scr[0] + _dot(p_d0, sel_keys(vs_ref, 0, n_before))
        p_d1 = probs(s_d1)
        s_w1 = win_scores(1, shift)
        acc_scr[1] = acc_scr[1] + _dot(p_d1, sel_keys(vs_ref, 1, n_before))
        p_w0 = probs(s_w0)
        acc_w0 = _dot(p_w0, win_values(0))
        p_w1 = probs(s_w1)
        acc_w1 = _dot(p_w1, win_values(1))
        finish(cmp_s, [acc_w0, acc_w1])

    def online_path():
        cmp_s, sel_b = compressed_and_select(None)
        for g in groups:
            m_scr[g] = jnp.full(m_scr.shape[1:], NEG_INF, F32)
            acc_scr[g] = jnp.zeros(acc_scr.shape[1:], F32)

        def sel_softmax(g, s):
            m_old = m_scr[g]
            m_new = jnp.maximum(m_old, jnp.max(s, axis=-1, keepdims=True))
            m_scr[g] = m_new
            return jnp.exp(s - m_new).astype(BF16), jnp.exp(m_old - m_new)

        def sel_accumulate(g, kc, p, alpha):
            acc_scr[g] = alpha * acc_scr[g] + _dot(p, sel_keys(vs_ref, g, kc))

        def win_softmax(s):
            return jnp.exp(s - jnp.max(s, axis=-1, keepdims=True)).astype(BF16)

        def before(kc, carry):
            s = [sel_scores(sel_b, g, kc, None, None) for g in groups]
            pa = [sel_softmax(g, s[g]) for g in groups]
            for g in groups:
                sel_accumulate(g, kc, *pa[g])
            return carry

        lax.fori_loop(0, n_before, before, 0)
        s_d0 = sel_scores(sel_b, 0, n_before, causal, None)
        s_d1 = sel_scores(sel_b, 1, n_before, causal, None)
        pa0 = sel_softmax(0, s_d0)
        s_w0 = win_scores(0, None)
        sel_accumulate(0, n_before, *pa0)
        pa1 = sel_softmax(1, s_d1)
        s_w1 = win_scores(1, None)
        sel_accumulate(1, n_before, *pa1)
        acc_w0 = _dot(win_softmax(s_w0), win_values(0))
        acc_w1 = _dot(win_softmax(s_w1), win_values(1))
        finish(cmp_s, [acc_w0, acc_w1])

    bound = 1.01 * (HEAD_DIM ** 0.5) * jnp.max(jnp.abs(qgain_ref[...])) * jnp.max(jnp.abs(kgain_ref[...]))
    safe = bound <= MAX_FIXED_SHIFT
    pl.when(safe)(lambda: fixed_shift_path(bound))
    pl.when(jnp.logical_not(safe))(online_path)


def _sel_from_cmp(n_cmp, n_sel):
    c0 = np.arange(n_cmp) * CMP_STRIDE
    s0 = np.arange(n_sel) * SEL_BLOCK
    ov = np.minimum(c0[None, :] + CMP_BLOCK, s0[:, None] + SEL_BLOCK) - np.maximum(c0[None, :], s0[:, None])
    m = (np.clip(ov, 0, None) / CMP_BLOCK).astype(np.float32)
    m[:, (np.arange(n_cmp) * CMP_STRIDE + CMP_BLOCK) > n_sel * SEL_BLOCK] = 0.0
    return m


def _nsa_attn(proj3, kcmp, vcmp, ks, vs, kw, vw, q_gain, k_gains, o_gain):
    B, S, _ = proj3.shape
    n_cmp = kcmp.shape[2]
    n_sel = S // SEL_BLOCK
    tq = NSA_TQ
    assert n_sel % 8 == 0 and S % SEL_KC == 0 and SEL_KC % tq == 0 and WINDOW % tq == 0 and S >= WINDOW + tq
    avgq = jnp.asarray(_block_avg(NSA_WIDTH), BF16)
    avgo = jnp.asarray(_block_avg(LANES), BF16)
    msct = jnp.asarray(_sel_from_cmp(n_cmp, n_sel), BF16)
    esel = (np.arange(n_sel)[:, None] == np.arange(S)[None, :] // SEL_BLOCK).astype(np.float32)
    esel = jnp.asarray(esel.reshape(n_sel, S // SEL_KC, SEL_KC).transpose(1, 0, 2), BF16)
    src = np.arange(LANES)[:, None]
    dst = np.arange(NSA_WIDTH)[None, :]
    egate = jnp.asarray(np.stack([(src == (dst // HEAD_DIM) * 3 + j) for j in range(3)]).astype(np.float32), BF16)
    r = np.arange(tq)[:, None]
    n_w = WINDOW + tq
    wcases = []
    for i in range(WINDOW // tq + 1):
        diff = (i * tq - max(i * tq - WINDOW, 0)) + r - np.arange(n_w)[None, :]
        wcases.append(np.where((diff >= 0) & (diff < WINDOW), 0.0, NEG_INF))
    wbias = jnp.asarray(np.stack(wcases), F32)
    dbias = jnp.asarray(np.stack([np.where(np.arange(SEL_KC)[None, :] <= i * tq + r, 0.0, NEG_INF)
                                  for i in range(SEL_KC // tq)]), F32)
    c_end = np.arange(n_cmp)[None, :] * CMP_STRIDE + (CMP_BLOCK - 1)
    cbias = jnp.asarray(np.stack([np.where(c_end <= i * tq + r, 0.0, NEG_INF) for i in range(S // tq)]), F32)

    full = lambda a: pl.BlockSpec(a.shape, lambda b, i: (0,) * a.ndim)
    per_b = lambda a: pl.BlockSpec((1,) + a.shape[1:], lambda b, i: (b,) + (0,) * (a.ndim - 1))
    return pl.pallas_call(
        _nsa_attn_body,
        grid=(B, S // tq),
        in_specs=[pl.BlockSpec((1, tq, NSA_WIDTH), lambda b, i: (b, i, COL_QA // NSA_WIDTH)),
                  pl.BlockSpec((1, tq, LANES), lambda b, i: (b, i, COL_GATE // LANES)),
                  per_b(kcmp), per_b(vcmp), per_b(ks), per_b(vs), per_b(kw), per_b(vw),
                  full(q_gain), full(k_gains), full(o_gain), full(avgq), full(avgo), full(msct), full(esel), full(egate),
                  full(wbias), full(dbias), full(cbias)],
        out_specs=pl.BlockSpec((1, tq, NSA_WIDTH), lambda b, i: (b, i, 0)),
        out_shape=jax.ShapeDtypeStruct((B, S, NSA_WIDTH), BF16),
        scratch_shapes=[pltpu.VMEM((NSA_GROUPS, 4 * tq, 1), F32), pltpu.VMEM((NSA_GROUPS, 4 * tq, LANES), F32)],
        compiler_params=_cparams(2),
        name="nsa_attn",
    )(proj3, proj3, kcmp, vcmp, ks, vs, kw, vw, q_gain, k_gains, o_gain, avgq, avgo, msct, esel, egate, wbias, dbias,
      cbias)


def _retention_body(q_ref, k_ref, v_ref, g_ref, cos_ref, sin_ref, decay_ref, xi_ref, zeta_ref, gammac_ref,
                    gain_ref, avg_ref, o_ref, state_scr):
    S = q_ref.shape[1]
    C = RET_CHUNK
    lane = lax.broadcasted_iota(jnp.int32, (C, LANES), 1)
    lo = lane < HEAD_DIM
    first_half = (lane & (HEAD_DIM - 1)) < HEAD_DIM // 2
    r = lax.broadcasted_iota(jnp.int32, (LANES, LANES), 0)
    c = lax.broadcasted_iota(jnp.int32, (LANES, LANES), 1)
    same_head = (r < HEAD_DIM) == (c < HEAD_DIM)
    avg = avg_ref[...]
    state_scr[...] = jnp.zeros(state_scr.shape, F32)

    def rope(x, cos, sin):
        swapped = jnp.where(first_half, pltpu.roll(x, LANES - HEAD_DIM // 2, axis=1),
                            pltpu.roll(x, HEAD_DIM // 2, axis=1))
        return x * cos + swapped * sin

    def chunk(n, carry):
        r0 = pl.multiple_of(n * C, C)
        cos = cos_ref[pl.ds(r0, C), :]
        sin = sin_ref[pl.ds(r0, C), :]
        pairs = range(RET_HEADS // 2)
        cols = [slice(p * LANES, (p + 1) * LANES) for p in pairs]
        q = [rope(q_ref[0, pl.ds(r0, C), cols[p]].astype(F32), cos, sin) for p in pairs]
        k = [rope(k_ref[0, pl.ds(r0, C), cols[p]].astype(F32), cos, sin) * (HEAD_DIM ** -0.5) for p in pairs]
        vb = [v_ref[0, pl.ds(r0, C), cols[p]] for p in pairs]
        kb = [k[p].astype(BF16) for p in pairs]
        inner = [_dot_nt(jnp.where(lo if half == 0 else ~lo, q[p], 0.0).astype(BF16), kb[p])
                 * decay_ref[2 * p + half] for p in pairs for half in range(2)]
        state = [state_scr[p] for p in pairs]
        cross = [_dot(q[p].astype(BF16), state[p].astype(BF16)) * xi_ref[p] for p in pairs]
        upd = [_dot_tn((k[p] * zeta_ref[p]).astype(BF16), vb[p]) for p in pairs]
        outs = [_dot(inner[i].astype(BF16), vb[i // 2]) for i in range(RET_HEADS)]
        for p in pairs:
            state_scr[p] = gammac_ref[p] * state[p] + jnp.where(same_head, upd[p], 0.0)
        y = jnp.concatenate([jnp.where(lo, outs[2 * p], outs[2 * p + 1]) + cross[p] for p in pairs], axis=0)
        mu = _seg_mean(y, avg)
        d = y - mu
        var = _seg_mean(d * d, avg)
        yn = d * lax.rsqrt(var + EPS)
        for p in pairs:
            gate = g_ref[0, pl.ds(r0, C), cols[p]].astype(F32)
            o_ref[0, pl.ds(r0, C), cols[p]] = (_silu(gate) * (yn[p * C:(p + 1) * C] * gain_ref[:, cols[p]])).astype(BF16)
        return carry

    lax.fori_loop(0, S // C, chunk, 0)


def _retention_tables(S):
    half = HEAD_DIM // 2
    inv_freq = ROPE_BASE ** (-jnp.arange(half, dtype=F32) / half)
    ang = jnp.arange(S, dtype=F32)[:, None] * inv_freq[None, :]
    cos, sin = jnp.cos(ang), jnp.sin(ang)
    cos_t = jnp.tile(cos, (1, 4))
    sin_t = jnp.tile(jnp.concatenate([-sin, sin], axis=1), (1, 2))
    C = RET_CHUNK
    H = RET_HEADS
    log_gamma = jnp.log1p(-jnp.power(2.0, -5.0 - jnp.arange(H, dtype=F32)))
    i = jnp.arange(C, dtype=F32)
    rel = i[:, None] - i[None, :]
    decay = jnp.where(rel >= 0, jnp.exp(jnp.maximum(rel, 0.0)[None] * log_gamma[:, None, None]), 0.0)
    xi = jnp.exp((i + 1.0)[:, None] * log_gamma[None, :])
    zeta = jnp.exp((C - 1.0 - i)[:, None] * log_gamma[None, :])
    gamma_c = jnp.exp(C * log_gamma)
    per_pair = lambda t: jnp.repeat(t.T.reshape(H // 2, 2, -1), HEAD_DIM, axis=1).transpose(0, 2, 1)
    gammac = jnp.repeat(gamma_c.reshape(H // 2, 2), HEAD_DIM, axis=1)[:, None, :]
    return cos_t, sin_t, decay, per_pair(xi), per_pair(zeta), gammac


def _retention(proj3, gain):
    B, S, _ = proj3.shape
    cos_t, sin_t, decay, xi, zeta, gammac = _retention_tables(S)
    avg = jnp.asarray(_block_avg(LANES), BF16)
    col = lambda c: pl.BlockSpec((1, S, RET_WIDTH), lambda b: (b, 0, c // RET_WIDTH))
    full = lambda a: pl.BlockSpec(a.shape, lambda b: (0,) * a.ndim)
    return pl.pallas_call(
        _retention_body,
        grid=(B,),
        in_specs=[col(COL_QR), col(COL_KR), col(COL_VR), col(COL_GR), full(cos_t), full(sin_t), full(decay),
                  full(xi), full(zeta), full(gammac), full(gain), full(avg)],
        out_specs=pl.BlockSpec((1, S, RET_WIDTH), lambda b: (b, 0, 0)),
        out_shape=jax.ShapeDtypeStruct((B, S, RET_WIDTH), BF16),
        scratch_shapes=[pltpu.VMEM((RET_HEADS // 2, LANES, LANES), F32)],
        compiler_params=_cparams(1),
        name="retention",
    )(proj3, proj3, proj3, proj3, cos_t, sin_t, decay, xi, zeta, gammac, gain, avg)


def _mem_prep_body(mem_ref, g_ref, wkv_ref, kgain_ref, avg_ref, k_ref, v_ref):
    hm = _rms_full(mem_ref[0], g_ref[...]).astype(BF16)
    kv = _dot(hm, wkv_ref[...])
    k = kv[:, :MEM_WIDTH]
    ms = _seg_mean(k * k, avg_ref[...])
    k_ref[0] = (k * lax.rsqrt(ms + EPS) * kgain_ref[...]).astype(BF16)
    v_ref[0] = kv[:, MEM_WIDTH:].astype(BF16)


def _mem_prep(mem, g, wkv, kgain):
    B, M, _ = mem.shape
    avg = jnp.asarray(_block_avg(MEM_WIDTH), BF16)
    full = lambda a: pl.BlockSpec(a.shape, lambda b: (0,) * a.ndim)
    out_spec = pl.BlockSpec((1, M, MEM_WIDTH), lambda b: (b, 0, 0))
    out_shape = jax.ShapeDtypeStruct((B, M, MEM_WIDTH), BF16)
    return pl.pallas_call(
        _mem_prep_body,
        grid=(B,),
        in_specs=[pl.BlockSpec((1, M, D_MODEL), lambda b: (b, 0, 0)), full(g), full(wkv), full(kgain), full(avg)],
        out_specs=[out_spec, out_spec],
        out_shape=[out_shape, out_shape],
        compiler_params=_cparams(1),
        name="mem_prep",
    )(mem, g, wkv, kgain, avg)


def _post_body(x_ref, oa_ref, ob_ref, wout_ref, mk_ref, mv_ref, gx_ref, wq_ref, qgain_ref, avg_ref, wo_ref,
               gf_ref, wr_ref, br_ref, tri_ref, x2_ref, h_ref, route_ref, count_ref):
    tm = x_ref.shape[1] // POST_CHAINS
    chains = range(POST_CHAINS)
    rows = [slice(c * tm, (c + 1) * tm) for c in chains]
    x1 = [x_ref[0, rows[c]] + _dot(oa_ref[0, rows[c]], wout_ref[0:NSA_WIDTH, :])
          + _dot(ob_ref[0, rows[c]], wout_ref[NSA_WIDTH:, :]) for c in chains]

    h = [_rms_full(x1[c], gx_ref[...]).astype(BF16) for c in chains]
    q = [_dot(h[c], wq_ref[...]) for c in chains]
    ms = [_seg_mean(q[c] * q[c], avg_ref[...]) for c in chains]
    q = [q[c] * lax.rsqrt(ms[c] + EPS) * qgain_ref[...] * (HEAD_DIM ** -0.5) for c in chains]
    lane = lax.broadcasted_iota(jnp.int32, (tm, LANES), 1)
    lo = lane < HEAD_DIM
    heads = [(c, p, half) for c in chains for p in range(MEM_HEADS // 2) for half in range(2)]
    s = [_dot_nt(jnp.where(lo if half == 0 else ~lo, q[c][:, p * LANES:(p + 1) * LANES], 0.0).astype(BF16),
                 mk_ref[0, :, p * LANES:(p + 1) * LANES]) for c, p, half in heads]
    e = [jnp.exp(s[i] - jnp.max(s[i], axis=-1, keepdims=True)) for i in range(len(heads))]
    pr = [(e[i] / jnp.sum(e[i], axis=-1, keepdims=True)).astype(BF16) for i in range(len(heads))]
    outs = [_dot(pr[i], mv_ref[0, :, heads[i][1] * LANES:(heads[i][1] + 1) * LANES]) for i in range(len(heads))]
    per_chain = MEM_HEADS
    o = [jnp.concatenate([jnp.where(lo, outs[c * per_chain + 2 * p], outs[c * per_chain + 2 * p + 1])
                          for p in range(MEM_HEADS // 2)], axis=1).astype(BF16) for c in chains]
    x2 = [x1[c] + _dot(o[c], wo_ref[...]) for c in chains]
    for c in chains:
        x2_ref[0, rows[c]] = x2[c]

    hf = [_rms_full(x2[c], gf_ref[...]).astype(BF16) for c in chains]
    for c in chains:
        h_ref[0, rows[c]] = hf[c]
    logits = [_dot(hf[c], wr_ref[...]) + br_ref[...] for c in chains]
    lane_f = lane.astype(F32)
    big = float(LANES)
    picks = []
    for c in chains:
        gl = jnp.where(lane < N_GROUPS, logits[c], BELOW_ALL)
        gmax = jnp.max(gl, axis=-1, keepdims=True)
        grp = jnp.min(jnp.where(gl == gmax, lane_f, big), axis=-1, keepdims=True)
        g_w = 1.0 / jnp.sum(jnp.where(lane < N_GROUPS, jnp.exp(gl - gmax), 0.0), axis=-1, keepdims=True)
        e_lo = N_GROUPS + grp * EXPERTS_PER_GROUP
        el = jnp.where((lane_f >= e_lo) & (lane_f < e_lo + EXPERTS_PER_GROUP), logits[c], BELOW_ALL)
        v0 = jnp.max(el, axis=-1, keepdims=True)
        i0 = jnp.min(jnp.where(el == v0, lane_f, big), axis=-1, keepdims=True)
        el = jnp.where(lane_f == i0, BELOW_ALL, el)
        v1 = jnp.max(el, axis=-1, keepdims=True)
        i1 = jnp.min(jnp.where(el == v1, lane_f, big), axis=-1, keepdims=True)
        e1 = jnp.exp(v1 - v0)
        picks.append((i0 - N_GROUPS, i1 - N_GROUPS, g_w / (1.0 + e1), g_w * e1 / (1.0 + e1)))

    hot = [[lane_f == picks[c][s] for s in range(2)] for c in chains]
    both = jnp.concatenate([jnp.where(hot[c][0], 1.0, 0.0) + jnp.where(hot[c][1], 1.0, 0.0) for c in chains], axis=0)
    before = _dot(tri_ref[...], both.astype(BF16))
    count_ref[0] = jnp.broadcast_to(jnp.sum(both, axis=0, keepdims=True), count_ref.shape[1:])
    for c in chains:
        e0, e1, w0, w1 = picks[c]
        r0 = jnp.sum(jnp.where(hot[c][0], before[rows[c]], 0.0), axis=-1, keepdims=True)
        r1 = jnp.sum(jnp.where(hot[c][1], before[rows[c]], 0.0), axis=-1, keepdims=True)
        cols = (e0, e1, w0, w1, r0, r1)
        route = jnp.zeros((tm, LANES), F32)
        for k in range(len(cols)):
            route = jnp.where(lane == k, cols[k], route)
        route_ref[0, rows[c]] = route


def _post(x, oa, ob, wout, mk, mv, gx, wq, qgain, wo, gf, wr, br):
    B, S, _ = x.shape
    tm = MOE_TM
    n_s = S // tm
    avg = jnp.asarray(_block_avg(MEM_WIDTH), BF16)
    tri = jnp.asarray(np.tril(np.ones((tm, tm), np.float32), -1), BF16)
    full = lambda a: pl.BlockSpec(a.shape, lambda b, i: (0,) * a.ndim)
    per_b = lambda a: pl.BlockSpec((1,) + a.shape[1:], lambda b, i: (b,) + (0,) * (a.ndim - 1))
    tile = lambda w: pl.BlockSpec((1, tm, w), lambda b, i: (b, i, 0))
    return pl.pallas_call(
        _post_body,
        grid=(B, n_s),
        in_specs=[tile(D_MODEL), tile(NSA_WIDTH), tile(RET_WIDTH), full(wout), per_b(mk), per_b(mv), full(gx),
                  full(wq), full(qgain), full(avg), full(wo), full(gf), full(wr), full(br), full(tri)],
        out_specs=[tile(D_MODEL), tile(D_MODEL), tile(LANES),
                   pl.BlockSpec((1, 8, LANES), lambda b, i: (b * n_s + i, 0, 0))],
        out_shape=[jax.ShapeDtypeStruct((B, S, D_MODEL), F32), jax.ShapeDtypeStruct((B, S, D_MODEL), BF16),
                   jax.ShapeDtypeStruct((B, S, LANES), F32), jax.ShapeDtypeStruct((B * n_s, 8, LANES), F32)],
        compiler_params=_cparams(2),
        name="post_mixer",
    )(x, oa, ob, wout, mk, mv, gx, wq, qgain, avg, wo, gf, wr, br, tri)


def _row_copy(src, dst, sem):
    return pltpu.make_async_copy(src, dst, sem)


def _run_pieces(n, max_piece, fn):
    b = RUN_ALIGN
    while b <= max_piece:
        pl.when((n & b) != 0)(functools.partial(fn, n & (-2 * b), b))
        b *= 2


def _move_groups(i, gmap_ref, n_loc, copy):
    n_groups = n_loc // RUN_ALIGN
    for j in range(n_groups):
        glob = pl.multiple_of(gmap_ref[i * n_groups + j], RUN_ALIGN)
        copy(pl.ds(j * RUN_ALIGN, RUN_ALIGN), pl.ds(glob, RUN_ALIGN)).start()


def _local_positions(route, loff_row):
    lane = lax.broadcasted_iota(jnp.int32, route.shape, 1).astype(F32)
    pos = []
    for s in range(2):
        base = jnp.sum(jnp.where(lane == route[:, s:s + 1], loff_row, 0.0), axis=-1, keepdims=True)
        pos.append(base + route[:, 4 + s:5 + s])
    return pos


def _scatter_body(gmap_ref, tstart_ref, tlen_ref, nact_ref,
                  h_ref, route_ref, lofff_ref, xs_ref, xloc, zbuf, sems):
    i = pl.program_id(0)
    tm = h_ref.shape[0]
    n_loc = xloc.shape[1]
    slot = i & 1
    sem = sems.at[0]

    def tails(start):
        def per_expert(e, carry):
            n = tlen_ref[e]
            st = tstart_ref[e]

            def piece(off, size):
                c = _row_copy(zbuf.at[pl.ds(0, size)], xs_ref.at[pl.ds(pl.multiple_of(st + off, RUN_ALIGN), size)], sem)
                c.start() if start else c.wait()

            _run_pieces(n, MOE_RB // 2, piece)
            return carry

        lax.fori_loop(0, N_EXPERTS, per_expert, 0)

    def unused(start):
        rows = zbuf.shape[0]

        def per_unit(u, carry):
            c = _row_copy(zbuf, xs_ref.at[pl.ds(pl.multiple_of(u * rows, rows), rows)], sem)
            c.start() if start else c.wait()
            return carry

        lax.fori_loop(nact_ref[0] * (MOE_RB // rows), xs_ref.shape[0] // rows, per_unit, 0)

    @pl.when(i == 0)
    def _():
        zbuf[...] = jnp.zeros(zbuf.shape, zbuf.dtype)
        tails(True)
        unused(True)
        tails(False)
        unused(False)

    pos = _local_positions(route_ref[...], lofff_ref[0, 0:1, :])
    col = lax.broadcasted_iota(jnp.int32, (tm, n_loc), 1).astype(F32)
    perm_t = jnp.where((col == pos[0]) | (col == pos[1]), 1.0, 0.0).astype(BF16)
    xloc[slot] = _dot_tn(perm_t, h_ref[...]).astype(BF16)

    def wait_slot(s):
        _row_copy(xloc.at[s], xs_ref.at[pl.ds(0, n_loc)], sems.at[s]).wait()

    pl.when(i > 0)(lambda: wait_slot(1 - slot))
    _move_groups(i, gmap_ref, n_loc, lambda loc, glob: _row_copy(xloc.at[slot, loc], xs_ref.at[glob], sems.at[slot]))
    pl.when(i == pl.num_programs(0) - 1)(lambda: wait_slot(slot))


def _scatter_rows(tables, h2d, route2d, loff_f, n_rows):
    T = h2d.shape[0]
    tm = MOE_TM
    n_loc = MOE_NLOC
    tile = lambda w: pl.BlockSpec((tm, w), lambda i, *_: (i, 0))
    grid_spec = pltpu.PrefetchScalarGridSpec(
        num_scalar_prefetch=4,
        grid=(T // tm,),
        in_specs=[tile(D_MODEL), tile(LANES), pl.BlockSpec((1, 8, LANES), lambda i, *_: (i, 0, 0))],
        out_specs=pl.BlockSpec(memory_space=pl.ANY),
        scratch_shapes=[pltpu.VMEM((2, n_loc, D_MODEL), BF16), pltpu.VMEM((MOE_RB // 2, D_MODEL), BF16),
                        pltpu.SemaphoreType.DMA((2,))],
    )
    return pl.pallas_call(
        _scatter_body,
        grid_spec=grid_spec,
        out_shape=jax.ShapeDtypeStruct((n_rows, D_MODEL), BF16),
        compiler_params=_cparams(1),
        name="moe_scatter",
    )(*tables, h2d, route2d, loff_f)


def _expert_body(blk0_ref, nblk_ref, n_act_ref, xs_ref, wg_ref, wu_ref, wd_ref, ys_ref,
                 wg_b, wu_b, wd_b, xbuf, ybuf, sem_in, sem_out):
    e = pl.program_id(0)
    n_slots, rb = xbuf.shape[0], xbuf.shape[1]
    ahead = n_slots - 2
    n_act = n_act_ref[0]
    b0 = blk0_ref[e]

    def rows(g):
        return pl.ds(pl.multiple_of(g * rb, rb), rb)

    def x_copy(g, slot):
        return _row_copy(xs_ref.at[rows(g)], xbuf.at[slot], sem_in.at[slot])

    def y_copy(g, slot):
        return _row_copy(ybuf.at[slot], ys_ref.at[rows(g)], sem_out.at[slot])

    @pl.when(e == 0)
    def _():
        for k in range(ahead):
            pl.when(k < n_act)(lambda k=k: x_copy(k, k).start())

    wg_b[...] = wg_ref[0].astype(BF16)
    wu_b[...] = wu_ref[0].astype(BF16)
    wd_b[...] = wd_ref[0].astype(BF16)

    def blocks(g0, count):
        gs = [g0 + c for c in range(count)]
        slots = [g & (n_slots - 1) for g in gs]
        for c in range(count):
            x_copy(gs[c], slots[c]).wait()
        for c in range(count):
            nxt = gs[c] + ahead
            pl.when(nxt < n_act)(lambda nxt=nxt: x_copy(nxt, nxt & (n_slots - 1)).start())
        x = [xbuf[slots[c]] for c in range(count)]
        a = [_dot(x[c], wg_b[...]) for c in range(count)]
        b = [_dot(x[c], wu_b[...]) for c in range(count)]
        h = [(_silu(a[c]) * b[c]).astype(BF16) for c in range(count)]
        y = [_dot(h[c], wd_b[...]).astype(BF16) for c in range(count)]
        for c in range(count):
            pl.when(gs[c] >= n_slots)(lambda c=c: y_copy(gs[c] - n_slots, slots[c]).wait())
        for c in range(count):
            ybuf[slots[c]] = y[c]
            y_copy(gs[c], slots[c]).start()

    nb = nblk_ref[e]

    def pair(j, carry):
        blocks(b0 + 2 * j, 2)
        return carry

    lax.fori_loop(0, lax.shift_right_logical(nb, 1), pair, 0)
    pl.when((nb & 1) == 1)(lambda: blocks(b0 + nb - 1, 1))

    @pl.when(e == pl.num_programs(0) - 1)
    def _():
        for k in range(1, n_slots + 1):
            pl.when(n_act >= k)(lambda k=k: y_copy(n_act - k, (n_act - k) & (n_slots - 1)).wait())
        ybuf[0] = jnp.zeros(ybuf.shape[1:], ybuf.dtype)
        n_blocks = ys_ref.shape[0] // rb

        def fill(start):
            def per_block(g, carry):
                c = y_copy(g, 0)
                c.start() if start else c.wait()
                return carry

            lax.fori_loop(n_act, n_blocks, per_block, 0)

        fill(True)
        fill(False)


def _experts(blk0, nblk, n_act, xs, n_rows, wg, wu, wd):
    rb = MOE_RB
    weight = lambda a: pl.BlockSpec((1,) + a.shape[1:], lambda e, *_: (e, 0, 0))
    grid_spec = pltpu.PrefetchScalarGridSpec(
        num_scalar_prefetch=3,
        grid=(N_EXPERTS,),
        in_specs=[pl.BlockSpec(memory_space=pl.ANY), weight(wg), weight(wu), weight(wd)],
        out_specs=pl.BlockSpec(memory_space=pl.ANY),
        scratch_shapes=[pltpu.VMEM((D_MODEL, EXPERT_FF), BF16), pltpu.VMEM((D_MODEL, EXPERT_FF), BF16),
                        pltpu.VMEM((EXPERT_FF, D_MODEL), BF16), pltpu.VMEM((MOE_SLOTS, rb, D_MODEL), BF16),
                        pltpu.VMEM((MOE_SLOTS, rb, D_MODEL), BF16), pltpu.SemaphoreType.DMA((MOE_SLOTS,)),
                        pltpu.SemaphoreType.DMA((MOE_SLOTS,))],
    )
    return pl.pallas_call(
        _expert_body,
        grid_spec=grid_spec,
        out_shape=jax.ShapeDtypeStruct((n_rows, D_MODEL), BF16),
        compiler_params=_cparams(1),
        name="moe_experts",
    )(blk0, nblk, n_act, xs, wg, wu, wd)


def _combine_body(gmap_ref, x_ref, route_ref, lofff_ref, ys_ref, o_ref, yloc, sems):
    i = pl.program_id(0)
    tm = x_ref.shape[0]
    n_loc = yloc.shape[1]
    slot = i & 1

    def fetch(tile, s):
        _move_groups(tile, gmap_ref, n_loc, lambda loc, glob: _row_copy(ys_ref.at[glob], yloc.at[s, loc], sems.at[s]))

    pl.when(i == 0)(lambda: fetch(i, slot))
    _row_copy(ys_ref.at[pl.ds(0, n_loc)], yloc.at[slot], sems.at[slot]).wait()
    pl.when(i + 1 < pl.num_programs(0))(lambda: fetch(i + 1, 1 - slot))

    route = route_ref[...]
    pos = _local_positions(route, lofff_ref[0, 0:1, :])
    col = lax.broadcasted_iota(jnp.int32, (tm, n_loc), 1).astype(F32)
    perm_w = jnp.where(col == pos[0], route[:, 2:3], jnp.where(col == pos[1], route[:, 3:4], 0.0)).astype(BF16)
    o_ref[...] = x_ref[...] + _dot(perm_w, yloc[slot])


def _combine(tables, x2d, route2d, loff_f, ys):
    T = x2d.shape[0]
    tm = MOE_TM
    n_loc = MOE_NLOC
    tile = lambda w: pl.BlockSpec((tm, w), lambda i, *_: (i, 0))
    grid_spec = pltpu.PrefetchScalarGridSpec(
        num_scalar_prefetch=1,
        grid=(T // tm,),
        in_specs=[tile(D_MODEL), tile(LANES), pl.BlockSpec((1, 8, LANES), lambda i, *_: (i, 0, 0)),
                  pl.BlockSpec(memory_space=pl.ANY)],
        out_specs=tile(D_MODEL),
        scratch_shapes=[pltpu.VMEM((2, n_loc, D_MODEL), BF16), pltpu.SemaphoreType.DMA((2,))],
    )
    return pl.pallas_call(
        _combine_body,
        grid_spec=grid_spec,
        out_shape=jax.ShapeDtypeStruct((T, D_MODEL), F32),
        compiler_params=_cparams(1),
        name="moe_combine",
    )(*tables, x2d, route2d, loff_f, ys)


def _compress_weights(pos, w2):
    eye = jnp.eye(NSA_GROUPS, dtype=F32)
    w2b = jnp.einsum('hd,gk->ghkd', w2, eye).reshape(NSA_GROUPS * CMP_HIDDEN, LANES)
    posb = jnp.tile(pos, (1, NSA_GROUPS)).reshape(2, 1, CMP_STRIDE * LANES)
    return posb, w2b.astype(BF16)


def _dup2(g):
    return jnp.tile(g.reshape(1, HEAD_DIM), (1, 2))


def kernel(x, mem, mix_norm, w_in, nsa_q_norm, nsa_kcmp_norm, nsa_ksel_norm, nsa_kwin_norm, cmp_pos_k, cmp_pos_v, cmp_k_w1, cmp_k_w2, cmp_v_w1, cmp_v_w2, nsa_out_norm, ret_out_norm, w_out, mem_x_norm, mem_kv_norm, mem_wq, mem_wkv, mem_q_norm, mem_k_norm, mem_wo, ffn_norm, router_group_w, router_group_b, router_expert_w, router_expert_b, exp_w_gate, exp_w_up, exp_w_down):
    B, S, D = x.shape
    T = B * S
    depth = mix_norm.shape[0]
    for l in range(depth):
        proj = _proj(x.reshape(T, D), mix_norm[l].reshape(1, D), w_in[l].T).reshape(B, S, PROJ_PAD)
        pk, w2k = _compress_weights(cmp_pos_k[l], cmp_k_w2[l])
        pv, w2v = _compress_weights(cmp_pos_v[l], cmp_v_w2[l])
        gains = jnp.stack([_dup2(nsa_kcmp_norm[l]), _dup2(nsa_ksel_norm[l]), _dup2(nsa_kwin_norm[l])])
        kcmp, vcmp, ks, vs, kw, vw = _nsa_prep(proj, jnp.stack([pk, pv]), cmp_k_w1[l], cmp_v_w1[l],
                                               jnp.stack([w2k, w2v]), gains)
        o_a = _nsa_attn(proj, kcmp, vcmp, ks, vs, kw, vw,
                        jnp.tile(nsa_q_norm[l].reshape(1, HEAD_DIM), (1, NSA_HEADS)), gains,
                        nsa_out_norm[l].reshape(1, NSA_WIDTH))
        o_b = _retention(proj, ret_out_norm[l].reshape(1, RET_WIDTH))
        mk, mv = _mem_prep(mem, mem_kv_norm[l].reshape(1, D), mem_wkv[l].astype(BF16),
                           jnp.tile(mem_k_norm[l].reshape(1, HEAD_DIM), (1, MEM_HEADS)))
        w_r = jnp.concatenate([router_group_w[l],
                               router_expert_w[l].transpose(1, 0, 2).reshape(D, N_EXPERTS),
                               jnp.zeros((D, LANES - N_GROUPS - N_EXPERTS), F32)], axis=1).astype(BF16)
        b_r = jnp.concatenate([router_group_b[l], router_expert_b[l].reshape(N_EXPERTS),
                               jnp.zeros((LANES - N_GROUPS - N_EXPERTS,), F32)]).reshape(1, LANES)
        x2, hf, route, counts = _post(
            x, o_a, o_b, w_out[l].astype(BF16), mk, mv, mem_x_norm[l].reshape(1, D), mem_wq[l].astype(BF16),
            jnp.tile(mem_q_norm[l].reshape(1, HEAD_DIM), (1, MEM_HEADS)), mem_wo[l].astype(BF16),
            ffn_norm[l].reshape(1, D), w_r, b_r)
        route2d = route.reshape(T, LANES)
        n_tiles = T // MOE_TM
        cnt = counts[:, 0, :N_EXPERTS].astype(jnp.int32)
        cnt = (cnt + RUN_ALIGN - 1) // RUN_ALIGN * RUN_ALIGN
        loff = jnp.cumsum(cnt, axis=1) - cnt
        total = jnp.sum(cnt, axis=0)
        padded = (total + MOE_RB - 1) // MOE_RB * MOE_RB
        pend = jnp.cumsum(padded)
        pstart = pend - padded
        goff = pstart[None, :] + jnp.cumsum(cnt, axis=0) - cnt
        n_rows = 2 * T + n_tiles * N_EXPERTS * RUN_ALIGN + N_EXPERTS * MOE_RB
        n_act = (pend[-1:] // MOE_RB).astype(jnp.int32)
        loff_f = jnp.broadcast_to(jnp.pad(loff.astype(F32), ((0, 0), (0, LANES - N_EXPERTS)))[:, None, :],
                                  (n_tiles, 8, LANES))
        grp_row = jnp.arange(MOE_NLOC // RUN_ALIGN, dtype=jnp.int32) * RUN_ALIGN
        inside = ((loff[:, None, :] <= grp_row[None, :, None])
                  & (grp_row[None, :, None] < (loff + cnt)[:, None, :])).astype(jnp.int32)
        shift = jnp.sum(inside * (goff - loff)[:, None, :], axis=2)
        used = jnp.sum(inside, axis=2) > 0
        gmap_scatter = jnp.where(used, shift + grp_row[None, :], n_rows + grp_row[None, :]).reshape(-1)
        gmap_gather = jnp.where(used, shift + grp_row[None, :], 0).reshape(-1)
        xs = _scatter_rows((gmap_scatter, pstart + total, padded - total, n_act), hf.reshape(T, D), route2d, loff_f,
                           n_rows + MOE_NLOC)
        ys = _experts(pstart // MOE_RB, padded // MOE_RB, n_act, xs, n_rows,
                      exp_w_gate[l], exp_w_up[l], exp_w_down[l])
        x = _combine((gmap_gather,), x2.reshape(T, D), route2d, loff_f, ys).reshape(B, S, D)
    return x
```

```python
import functools

import numpy as np
import jax
import jax.numpy as jnp
from jax import lax
from jax.experimental import pallas as pl
from jax.experimental.pallas import tpu as pltpu

F32 = jnp.float32
BF16 = jnp.bfloat16

D_MODEL = 1024
HEAD_DIM = 64
LANES = 128
NSA_HEADS = 8
NSA_GROUPS = 2
NSA_WIDTH = NSA_HEADS * HEAD_DIM
CMP_BLOCK = 32
CMP_STRIDE = 16
CMP_HIDDEN = 2 * HEAD_DIM
SEL_BLOCK = 64
SEL_TOPK = 8
WINDOW = 512
RET_HEADS = 8
RET_WIDTH = RET_HEADS * HEAD_DIM
RET_CHUNK = 128
ROPE_BASE = 10000.0
MEM_HEADS = 4
MEM_WIDTH = MEM_HEADS * HEAD_DIM
N_GROUPS = 4
EXPERTS_PER_GROUP = 8
N_EXPERTS = N_GROUPS * EXPERTS_PER_GROUP
EXPERT_FF = D_MODEL // 4
EPS = 1e-6
NEG_INF = -1e30
FORCE_SCORE = 1e9
BELOW_ALL = -3e38
MAX_FIXED_SHIFT = 40.0

COL_QA = 0
COL_QR, COL_KR, COL_VR, COL_GR = 512, 1024, 1536, 2048
COL_KVC, COL_KSV, COL_KWV = 2560, 2816, 3072
COL_GATE = 3328
PROJ_PAD = 3456
PROJ_SPLITS = (NSA_WIDTH, NSA_WIDTH + 6 * NSA_GROUPS * HEAD_DIM, NSA_WIDTH + 6 * NSA_GROUPS * HEAD_DIM + 3 * NSA_HEADS)

PROJ_TM = 512
NSA_TQ = 256
SEL_KC = 512
POST_CHAINS = 2
MOE_TM = 512
MOE_RB = 256
MOE_SLOTS = 8
RUN_ALIGN = 16
MOE_NLOC = 2 * MOE_TM + N_EXPERTS * RUN_ALIGN
VMEM_LIMIT = 56 * 1024 * 1024


def _cparams(n_axes):
    return pltpu.CompilerParams(dimension_semantics=("arbitrary",) * n_axes,
                                vmem_limit_bytes=VMEM_LIMIT)


def _dot(a, b):
    return jnp.dot(a, b, preferred_element_type=F32)


def _dot_nt(a, b):
    return lax.dot_general(a, b, (((1,), (1,)), ((), ())), preferred_element_type=F32)


def _dot_tn(a, b):
    return lax.dot_general(a, b, (((0,), (0,)), ((), ())), preferred_element_type=F32)


def _rms_full(x, g):
    ms = jnp.mean(x * x, axis=-1, keepdims=True)
    return x * lax.rsqrt(ms + EPS) * g


def _seg_mean(x, avg):
    return _dot(x.astype(BF16), avg)


def _silu(x):
    return x * (1.0 / (1.0 + jnp.exp(-x)))


def _sigmoid(x):
    return 1.0 / (1.0 + jnp.exp(-x))


def _block_avg(width):
    i = np.arange(width)
    return ((i[:, None] // HEAD_DIM == i[None, :] // HEAD_DIM) / HEAD_DIM).astype(np.float32)


def _proj_body(x_ref, g_ref, wt_ref, o_ref, w_scr):
    @pl.when(pl.program_id(0) == 0)
    def _():
        kv0, gate0, ret0 = PROJ_SPLITS
        w_scr[COL_QA:COL_QR] = wt_ref[0:kv0].astype(BF16)
        w_scr[COL_QR:COL_KVC] = wt_ref[ret0:].astype(BF16)
        w_scr[COL_KVC:COL_GATE] = wt_ref[kv0:gate0].astype(BF16)
        pad = jnp.zeros((PROJ_PAD - COL_GATE - (ret0 - gate0), wt_ref.shape[1]), F32)
        w_scr[COL_GATE:] = jnp.concatenate([wt_ref[gate0:ret0], pad], axis=0).astype(BF16)

    h = _rms_full(x_ref[...], g_ref[...]).astype(BF16)
    step = PROJ_PAD // 3
    for j in range(3):
        o_ref[:, j * step:(j + 1) * step] = _dot_nt(h, w_scr[j * step:(j + 1) * step]).astype(BF16)


def _proj(x2d, g, w_t):
    T = x2d.shape[0]
    assert w_t.shape[0] == PROJ_SPLITS[2] + 4 * RET_WIDTH
    return pl.pallas_call(
        _proj_body,
        grid=(T // PROJ_TM,),
        in_specs=[pl.BlockSpec((PROJ_TM, D_MODEL), lambda i: (i, 0)),
                  pl.BlockSpec((1, D_MODEL), lambda i: (0, 0)),
                  pl.BlockSpec(w_t.shape, lambda i: (0, 0))],
        out_specs=pl.BlockSpec((PROJ_TM, PROJ_PAD), lambda i: (i, 0)),
        out_shape=jax.ShapeDtypeStruct((T, PROJ_PAD), BF16),
        scratch_shapes=[pltpu.VMEM((PROJ_PAD, D_MODEL), BF16)],
        compiler_params=_cparams(1),
        name="proj",
    )(x2d, g, w_t)


def _dup_groups(x):
    lane = lax.broadcasted_iota(jnp.int32, x.shape, 1)
    xs = pltpu.roll(x, HEAD_DIM, axis=1)
    lo = lane < HEAD_DIM
    return jnp.where(lo, x, xs), jnp.where(lo, xs, x)


def _ones_groups(x):
    lane = lax.broadcasted_iota(jnp.int32, x.shape, 1)
    lo = lane < HEAD_DIM
    return jnp.where(lo, x, 1.0), jnp.where(lo, pltpu.roll(x, HEAD_DIM, axis=1), 1.0)


def _nsa_prep_body(kvc_ref, ksv_ref, kwv_ref, pos_ref, w1k_ref, w1v_ref, w2_ref, gain_ref, avg_ref,
                   kcmp_ref, vcmp_ref, ks_ref, vs_ref, kw_ref, vw_ref, scr_k, scr_v, w1_ref):
    @pl.when(pl.program_id(0) == 0)
    def _():
        zero = jnp.zeros((HEAD_DIM, CMP_HIDDEN), BF16)
        for j, src in ((0, w1k_ref), (1, w1v_ref)):
            for l in range(CMP_BLOCK):
                piece = src[l * HEAD_DIM:(l + 1) * HEAD_DIM, :].astype(BF16)
                r0 = (l % CMP_STRIDE) * LANES
                w1_ref[j, l // CMP_STRIDE, r0:r0 + HEAD_DIM, :] = jnp.concatenate([piece, zero], axis=1)
                w1_ref[j, l // CMP_STRIDE, r0 + HEAD_DIM:r0 + LANES, :] = jnp.concatenate([zero, piece], axis=1)

    avg = avg_ref[...]
    n_c = scr_k.shape[0] // CMP_STRIDE
    scr_k[...] = kvc_ref[0, :, 0:LANES].astype(F32)
    scr_v[...] = kvc_ref[0, :, LANES:2 * LANES].astype(F32)
    for j, out_ref, scr in ((0, kcmp_ref, scr_k), (1, vcmp_ref, scr_v)):
        ycat = jnp.concatenate(
            [scr[pl.ds(l, n_c, stride=CMP_STRIDE), :] for l in range(CMP_STRIDE)], axis=1)
        first = _dot((ycat + pos_ref[j, 0]).astype(BF16), w1_ref[j, 0])
        second = _dot((ycat + pos_ref[j, 1]).astype(BF16), w1_ref[j, 1])
        hidden = first + pltpu.roll(second, n_c - 1, axis=0)
        cmp_tok = _dot(_silu(hidden).astype(BF16), w2_ref[j])
        if j == 0:
            ms = _seg_mean(cmp_tok * cmp_tok, avg)
            cmp_tok = cmp_tok * lax.rsqrt(ms + EPS) * gain_ref[0]
        d0, d1 = _dup_groups(cmp_tok) if j == 0 else _ones_groups(cmp_tok)
        out_ref[0, 0] = d0.astype(BF16)
        out_ref[0, 1] = d1.astype(BF16)

    for src_ref, k_out, v_out, gi in ((ksv_ref, ks_ref, vs_ref, 1), (kwv_ref, kw_ref, vw_ref, 2)):
        k = src_ref[0, :, 0:LANES].astype(F32)
        ms = _seg_mean(k * k, avg)
        k = k * lax.rsqrt(ms + EPS) * gain_ref[gi]
        d0, d1 = _dup_groups(k)
        k_out[0, 0] = d0.astype(BF16)
        k_out[0, 1] = d1.astype(BF16)
        d0, d1 = _ones_groups(src_ref[0, :, LANES:2 * LANES].astype(F32))
        v_out[0, 0] = d0.astype(BF16)
        v_out[0, 1] = d1.astype(BF16)


def _nsa_prep(proj3, pos, w1k, w1v, w2, gains):
    B, S, _ = proj3.shape
    n_c = S // CMP_STRIDE
    avg = jnp.asarray(_block_avg(LANES), BF16)
    col = lambda c: pl.BlockSpec((1, S, 2 * LANES), lambda b: (b, 0, c // (2 * LANES)))
    full = lambda a: pl.BlockSpec(a.shape, lambda b: (0,) * a.ndim)
    cmp_spec = pl.BlockSpec((1, NSA_GROUPS, n_c, LANES), lambda b: (b, 0, 0, 0))
    seq_spec = pl.BlockSpec((1, NSA_GROUPS, S, LANES), lambda b: (b, 0, 0, 0))
    cmp_shape = jax.ShapeDtypeStruct((B, NSA_GROUPS, n_c, LANES), BF16)
    seq_shape = jax.ShapeDtypeStruct((B, NSA_GROUPS, S, LANES), BF16)
    return pl.pallas_call(
        _nsa_prep_body,
        grid=(B,),
        in_specs=[col(COL_KVC), col(COL_KSV), col(COL_KWV), full(pos), full(w1k), full(w1v), full(w2), full(gains),
                  full(avg)],
        out_specs=[cmp_spec, cmp_spec, seq_spec, seq_spec, seq_spec, seq_spec],
        out_shape=[cmp_shape, cmp_shape, seq_shape, seq_shape, seq_shape, seq_shape],
        scratch_shapes=[pltpu.VMEM((S, LANES), F32), pltpu.VMEM((S, LANES), F32),
                        pltpu.VMEM((2, 2, CMP_STRIDE * LANES, NSA_GROUPS * CMP_HIDDEN), BF16)],
        compiler_params=_cparams(1),
        name="nsa_prep",
    )(proj3, proj3, proj3, pos, w1k, w1v, w2, gains, avg)


def _nsa_attn_body(q_ref, gate_ref, kcmp_ref, vcmp_ref, ks_ref, vs_ref, kw_ref, vw_ref,
                   qgain_ref, kgain_ref, ogain_ref, avgq_ref, avgo_ref, msct_ref, esel_ref, egate_ref, wbias_ref,
                   dbias_ref, cbias_ref,
                   o_ref, m_scr, acc_scr):
    tq = q_ref.shape[1]
    n_cmp = kcmp_ref.shape[2]
    n_sel = msct_ref.shape[0]
    kc_len = esel_ref.shape[2]
    rows = 4 * tq
    qi = pl.program_id(1)
    q0 = qi * tq

    q = q_ref[0].astype(F32)
    ms = _seg_mean(q * q, avgq_ref[...])
    qn = q * lax.rsqrt(ms + EPS) * qgain_ref[...] * (HEAD_DIM ** -0.5)

    gate_sig = _sigmoid(gate_ref[0].astype(F32)).astype(BF16)
    gates = [_dot(gate_sig, egate_ref[j]) for j in range(3)]

    lane_q = lax.broadcasted_iota(jnp.int32, (tq, LANES), 1)
    lo_q = lane_q < HEAD_DIM
    lo_r = lax.broadcasted_iota(jnp.int32, (rows, LANES), 1) < HEAD_DIM

    blk = lax.broadcasted_iota(jnp.int32, (n_sel, tq), 0)
    cur = lax.shift_right_logical(q0 + lax.broadcasted_iota(jnp.int32, (n_sel, tq), 1), int(np.log2(SEL_BLOCK)))
    forced = (blk == 0) | (blk == cur) | (blk == cur - 1)
    future = blk > cur
    blk_f = blk.astype(F32)

    def heads4(x):
        return jnp.concatenate([x] * 4, axis=0)

    def normalised_pairs(acc, guard):
        rolled = pltpu.roll(acc, HEAD_DIM, axis=1)
        den = jnp.where(lo_r, rolled, acc)
        if guard:
            den = jnp.maximum(den, 1e-30)
        out = []
        for p in range(2):
            ev = slice((2 * p) * tq, (2 * p + 1) * tq)
            od = slice((2 * p + 1) * tq, (2 * p + 2) * tq)
            out.append(jnp.where(lo_q, acc[ev] / den[ev], rolled[od] / den[od]))
        return out

    groups = range(NSA_GROUPS)
    qs = []
    for g in groups:
        slabs = [qn[:, (2 * g + p) * LANES:(2 * g + p + 1) * LANES] for p in range(2)]
        qs.append(jnp.concatenate(
            [jnp.where(lo_q, slabs[0], 0.0), jnp.where(lo_q, 0.0, slabs[0]),
             jnp.where(lo_q, slabs[1], 0.0), jnp.where(lo_q, 0.0, slabs[1])], axis=0).astype(BF16))

    def compressed_and_select(shift):
        s_c = [_dot_nt(qs[g], kcmp_ref[0, g]) for g in groups]
        if shift is None:
            r_c = lax.broadcasted_iota(jnp.int32, (rows, n_cmp), 0)
            c_c = lax.broadcasted_iota(jnp.int32, (rows, n_cmp), 1)
            cmask = (c_c * CMP_STRIDE + (CMP_BLOCK - 1)) <= q0 + (r_c & (tq - 1))
            s_c = [jnp.where(cmask, s_c[g], NEG_INF) for g in groups]
            e_c = [jnp.where(cmask, jnp.exp(s_c[g] - jnp.max(s_c[g], axis=-1, keepdims=True)), 0.0) for g in groups]
        else:
            bias = heads4(cbias_ref[qi] - shift)
            e_c = [jnp.exp(s_c[g] + bias) for g in groups]
        e_b = [e_c[g].astype(BF16) for g in groups]
        acc_c = [_dot(e_b[g], vcmp_ref[0, g]) for g in groups]
        ones = jnp.ones((8, n_cmp), BF16)
        imp = []
        for g in groups:
            num = [_dot_nt(msct_ref[...], e_b[g][h * tq:(h + 1) * tq]) for h in range(4)]
            den = [_dot_nt(ones, e_b[g][h * tq:(h + 1) * tq])[0:1] for h in range(4)]
            parts = [num[h] / jnp.maximum(den[h], 1e-30) for h in range(4)]
            imp.append((parts[0] + parts[1]) + (parts[2] + parts[3]))
        v = [jnp.where(forced, FORCE_SCORE, jnp.where(future, NEG_INF, imp[g])) for g in groups]
        sel = [jnp.zeros((n_sel, tq), F32) for g in groups]
        for _ in range(SEL_TOPK):
            mx = [jnp.max(v[g], axis=0, keepdims=True) for g in groups]
            first = [jnp.min(jnp.where(v[g] == mx[g], blk_f, float(LANES)), axis=0, keepdims=True) for g in groups]
            pick = [blk_f == first[g] for g in groups]
            sel = [jnp.where(pick[g], 1.0, sel[g]) for g in groups]
            v = [jnp.where(pick[g], BELOW_ALL, v[g]) for g in groups]
        return [normalised_pairs(acc_c[g], True) for g in groups], [sel[g].astype(BF16) for g in groups]

    n_before = lax.shift_right_logical(q0, int(np.log2(kc_len)))
    causal = dbias_ref[qi & (kc_len // tq - 1)]
    w0 = pl.multiple_of(jnp.maximum(q0 - WINDOW, 0), tq)
    n_w = WINDOW + tq
    w_case = jnp.minimum(qi, WINDOW // tq)

    def sel_keys(ref, g, kc):
        return ref[0, g, pl.ds(pl.multiple_of(kc * kc_len, kc_len), kc_len), :]

    def sel_scores(sel_b, g, kc, causal_bias, shift):
        chosen = _dot_tn(sel_b[g], esel_ref[kc])
        bias = (chosen - 1.0) * (-NEG_INF)
        if causal_bias is not None:
            bias = bias + causal_bias
        if shift is not None:
            bias = bias - shift
        return _dot_nt(qs[g], sel_keys(ks_ref, g, kc)) + heads4(bias)

    def win_scores(g, shift):
        bias = wbias_ref[w_case] if shift is None else wbias_ref[w_case] - shift
        return _dot_nt(qs[g], kw_ref[0, g, pl.ds(w0, n_w), :]) + heads4(bias)

    def win_values(g):
        return vw_ref[0, g, pl.ds(w0, n_w), :]

    def finish(cmp_s, acc_w):
        for g in groups:
            sel_s = normalised_pairs(acc_scr[g], False)
            win_s = normalised_pairs(acc_w[g], False)
            for p in range(2):
                cols = slice((2 * g + p) * LANES, (2 * g + p + 1) * LANES)
                mix = gates[0][:, cols] * cmp_s[g][p] + gates[1][:, cols] * sel_s[p] + gates[2][:, cols] * win_s[p]
                ms_o = _seg_mean(mix * mix, avgo_ref[...])
                o_ref[0, :, cols] = (mix * lax.rsqrt(ms_o + EPS) * ogain_ref[:, cols]).astype(BF16)

    def fixed_shift_path(shift):
        cmp_s, sel_b = compressed_and_select(shift)

        def probs(s):
            return jnp.exp(s).astype(BF16)

        for g in groups:
            acc_scr[g] = jnp.zeros(acc_scr.shape[1:], F32)

        def before(kc, carry):
            s = [sel_scores(sel_b, g, kc, None, shift) for g in groups]
            p = [probs(s[g]) for g in groups]
            for g in groups:
                acc_scr[g] = acc_scr[g] + _dot(p[g], sel_keys(vs_ref, g, kc))
            return carry

        lax.fori_loop(0, n_before, before, 0)
        s_d0 = sel_scores(sel_b, 0, n_before, causal, shift)
        s_d1 = sel_scores(sel_b, 1, n_before, causal, shift)
        p_d0 = probs(s_d0)
        s_w0 = win_scores(0, shift)
        acc_scr[0] = acc_scr[0] + _dot(p_d0, sel_keys(vs_ref, 0, n_before))
        p_d1 = probs(s_d1)
        s_w1 = win_scores(1, shift)
        acc_scr[1] = acc_scr[1] + _dot(p_d1, sel_keys(vs_ref, 1, n_before))
        p_w0 = probs(s_w0)
        acc_w0 = _dot(p_w0, win_values(0))
        p_w1 = probs(s_w1)
        acc_w1 = _dot(p_w1, win_values(1))
        finish(cmp_s, [acc_w0, acc_w1])

    def online_path():
        cmp_s, sel_b = compressed_and_select(None)
        for g in groups:
            m_scr[g] = jnp.full(m_scr.shape[1:], NEG_INF, F32)
            acc_scr[g] = jnp.zeros(acc_scr.shape[1:], F32)

        def sel_softmax(g, s):
            m_old = m_scr[g]
            m_new = jnp.maximum(m_old, jnp.max(s, axis=-1, keepdims=True))
            m_scr[g] = m_new
            return jnp.exp(s - m_new).astype(BF16), jnp.exp(m_old - m_new)

        def sel_accumulate(g, kc, p, alpha):
            acc_scr[g] = alpha * acc_scr[g] + _dot(p, sel_keys(vs_ref, g, kc))

        def win_softmax(s):
            return jnp.exp(s - jnp.max(s, axis=-1, keepdims=True)).astype(BF16)

        def before(kc, carry):
            s = [sel_scores(sel_b, g, kc, None, None) for g in groups]
            pa = [sel_softmax(g, s[g]) for g in groups]
            for g in groups:
                sel_accumulate(g, kc, *pa[g])
            return carry

        lax.fori_loop(0, n_before, before, 0)
        s_d0 = sel_scores(sel_b, 0, n_before, causal, None)
        s_d1 = sel_scores(sel_b, 1, n_before, causal, None)
        pa0 = sel_softmax(0, s_d0)
        s_w0 = win_scores(0, None)
        sel_accumulate(0, n_before, *pa0)
        pa1 = sel_softmax(1, s_d1)
        s_w1 = win_scores(1, None)
        sel_accumulate(1, n_before, *pa1)
        acc_w0 = _dot(win_softmax(s_w0), win_values(0))
        acc_w1 = _dot(win_softmax(s_w1), win_values(1))
        finish(cmp_s, [acc_w0, acc_w1])

    bound = 1.01 * (HEAD_DIM ** 0.5) * jnp.max(jnp.abs(qgain_ref[...])) * jnp.max(jnp.abs(kgain_ref[...]))
    safe = bound <= MAX_FIXED_SHIFT
    pl.when(safe)(lambda: fixed_shift_path(bound))
    pl.when(jnp.logical_not(safe))(online_path)


def _sel_from_cmp(n_cmp, n_sel):
    c0 = np.arange(n_cmp) * CMP_STRIDE
    s0 = np.arange(n_sel) * SEL_BLOCK
    ov = np.minimum(c0[None, :] + CMP_BLOCK, s0[:, None] + SEL_BLOCK) - np.maximum(c0[None, :], s0[:, None])
    m = (np.clip(ov, 0, None) / CMP_BLOCK).astype(np.float32)
    m[:, (np.arange(n_cmp) * CMP_STRIDE + CMP_BLOCK) > n_sel * SEL_BLOCK] = 0.0
    return m


def _nsa_attn(proj3, kcmp, vcmp, ks, vs, kw, vw, q_gain, k_gains, o_gain):
    B, S, _ = proj3.shape
    n_cmp = kcmp.shape[2]
    n_sel = S // SEL_BLOCK
    tq = NSA_TQ
    assert n_sel % 8 == 0 and S % SEL_KC == 0 and SEL_KC % tq == 0 and WINDOW % tq == 0 and S >= WINDOW + tq
    avgq = jnp.asarray(_block_avg(NSA_WIDTH), BF16)
    avgo = jnp.asarray(_block_avg(LANES), BF16)
    msct = jnp.asarray(_sel_from_cmp(n_cmp, n_sel), BF16)
    esel = (np.arange(n_sel)[:, None] == np.arange(S)[None, :] // SEL_BLOCK).astype(np.float32)
    esel = jnp.asarray(esel.reshape(n_sel, S // SEL_KC, SEL_KC).transpose(1, 0, 2), BF16)
    src = np.arange(LANES)[:, None]
    dst = np.arange(NSA_WIDTH)[None, :]
    egate = jnp.asarray(np.stack([(src == (dst // HEAD_DIM) * 3 + j) for j in range(3)]).astype(np.float32), BF16)
    r = np.arange(tq)[:, None]
    n_w = WINDOW + tq
    wcases = []
    for i in range(WINDOW // tq + 1):
        diff = (i * tq - max(i * tq - WINDOW, 0)) + r - np.arange(n_w)[None, :]
        wcases.append(np.where((diff >= 0) & (diff < WINDOW), 0.0, NEG_INF))
    wbias = jnp.asarray(np.stack(wcases), F32)
    dbias = jnp.asarray(np.stack([np.where(np.arange(SEL_KC)[None, :] <= i * tq + r, 0.0, NEG_INF)
                                  for i in range(SEL_KC // tq)]), F32)
    c_end = np.arange(n_cmp)[None, :] * CMP_STRIDE + (CMP_BLOCK - 1)
    cbias = jnp.asarray(np.stack([np.where(c_end <= i * tq + r, 0.0, NEG_INF) for i in range(S // tq)]), F32)

    full = lambda a: pl.BlockSpec(a.shape, lambda b, i: (0,) * a.ndim)
    per_b = lambda a: pl.BlockSpec((1,) + a.shape[1:], lambda b, i: (b,) + (0,) * (a.ndim - 1))
    return pl.pallas_call(
        _nsa_attn_body,
        grid=(B, S // tq),
        in_specs=[pl.BlockSpec((1, tq, NSA_WIDTH), lambda b, i: (b, i, COL_QA // NSA_WIDTH)),
                  pl.BlockSpec((1, tq, LANES), lambda b, i: (b, i, COL_GATE // LANES)),
                  per_b(kcmp), per_b(vcmp), per_b(ks), per_b(vs), per_b(kw), per_b(vw),
                  full(q_gain), full(k_gains), full(o_gain), full(avgq), full(avgo), full(msct), full(esel), full(egate),
                  full(wbias), full(dbias), full(cbias)],
        out_specs=pl.BlockSpec((1, tq, NSA_WIDTH), lambda b, i: (b, i, 0)),
        out_shape=jax.ShapeDtypeStruct((B, S, NSA_WIDTH), BF16),
        scratch_shapes=[pltpu.VMEM((NSA_GROUPS, 4 * tq, 1), F32), pltpu.VMEM((NSA_GROUPS, 4 * tq, LANES), F32)],
        compiler_params=_cparams(2),
        name="nsa_attn",
    )(proj3, proj3, kcmp, vcmp, ks, vs, kw, vw, q_gain, k_gains, o_gain, avgq, avgo, msct, esel, egate, wbias, dbias,
      cbias)


def _retention_body(q_ref, k_ref, v_ref, g_ref, cos_ref, sin_ref, decay_ref, xi_ref, zeta_ref, gammac_ref,
                    gain_ref, avg_ref, o_ref, state_scr):
    S = q_ref.shape[1]
    C = RET_CHUNK
    lane = lax.broadcasted_iota(jnp.int32, (C, LANES), 1)
    lo = lane < HEAD_DIM
    first_half = (lane & (HEAD_DIM - 1)) < HEAD_DIM // 2
    r = lax.broadcasted_iota(jnp.int32, (LANES, LANES), 0)
    c = lax.broadcasted_iota(jnp.int32, (LANES, LANES), 1)
    same_head = (r < HEAD_DIM) == (c < HEAD_DIM)
    avg = avg_ref[...]
    state_scr[...] = jnp.zeros(state_scr.shape, F32)

    def rope(x, cos, sin):
        swapped = jnp.where(first_half, pltpu.roll(x, LANES - HEAD_DIM // 2, axis=1),
                            pltpu.roll(x, HEAD_DIM // 2, axis=1))
        return x * cos + swapped * sin

    def chunk(n, carry):
        r0 = pl.multiple_of(n * C, C)
        cos = cos_ref[pl.ds(r0, C), :]
        sin = sin_ref[pl.ds(r0, C), :]
        pairs = range(RET_HEADS // 2)
        cols = [slice(p * LANES, (p + 1) * LANES) for p in pairs]
        q = [rope(q_ref[0, pl.ds(r0, C), cols[p]].astype(F32), cos, sin) for p in pairs]
        k = [rope(k_ref[0, pl.ds(r0, C), cols[p]].astype(F32), cos, sin) * (HEAD_DIM ** -0.5) for p in pairs]
        vb = [v_ref[0, pl.ds(r0, C), cols[p]] for p in pairs]
        kb = [k[p].astype(BF16) for p in pairs]
        inner = [_dot_nt(jnp.where(lo if half == 0 else ~lo, q[p], 0.0).astype(BF16), kb[p])
                 * decay_ref[2 * p + half] for p in pairs for half in range(2)]
        state = [state_scr[p] for p in pairs]
        cross = [_dot(q[p].astype(BF16), state[p].astype(BF16)) * xi_ref[p] for p in pairs]
        upd = [_dot_tn((k[p] * zeta_ref[p]).astype(BF16), vb[p]) for p in pairs]
        outs = [_dot(inner[i].astype(BF16), vb[i // 2]) for i in range(RET_HEADS)]
        for p in pairs:
            state_scr[p] = gammac_ref[p] * state[p] + jnp.where(same_head, upd[p], 0.0)
        y = jnp.concatenate([jnp.where(lo, outs[2 * p], outs[2 * p + 1]) + cross[p] for p in pairs], axis=0)
        mu = _seg_mean(y, avg)
        d = y - mu
        var = _seg_mean(d * d, avg)
        yn = d * lax.rsqrt(var + EPS)
        for p in pairs:
            gate = g_ref[0, pl.ds(r0, C), cols[p]].astype(F32)
            o_ref[0, pl.ds(r0, C), cols[p]] = (_silu(gate) * (yn[p * C:(p + 1) * C] * gain_ref[:, cols[p]])).astype(BF16)
        return carry

    lax.fori_loop(0, S // C, chunk, 0)


def _retention_tables(S):
    half = HEAD_DIM // 2
    inv_freq = ROPE_BASE ** (-jnp.arange(half, dtype=F32) / half)
    ang = jnp.arange(S, dtype=F32)[:, None] * inv_freq[None, :]
    cos, sin = jnp.cos(ang), jnp.sin(ang)
    cos_t = jnp.tile(cos, (1, 4))
    sin_t = jnp.tile(jnp.concatenate([-sin, sin], axis=1), (1, 2))
    C = RET_CHUNK
    H = RET_HEADS
    log_gamma = jnp.log1p(-jnp.power(2.0, -5.0 - jnp.arange(H, dtype=F32)))
    i = jnp.arange(C, dtype=F32)
    rel = i[:, None] - i[None, :]
    decay = jnp.where(rel >= 0, jnp.exp(jnp.maximum(rel, 0.0)[None] * log_gamma[:, None, None]), 0.0)
    xi = jnp.exp((i + 1.0)[:, None] * log_gamma[None, :])
    zeta = jnp.exp((C - 1.0 - i)[:, None] * log_gamma[None, :])
    gamma_c = jnp.exp(C * log_gamma)
    per_pair = lambda t: jnp.repeat(t.T.reshape(H // 2, 2, -1), HEAD_DIM, axis=1).transpose(0, 2, 1)
    gammac = jnp.repeat(gamma_c.reshape(H // 2, 2), HEAD_DIM, axis=1)[:, None, :]
    return cos_t, sin_t, decay, per_pair(xi), per_pair(zeta), gammac


def _retention(proj3, gain):
    B, S, _ = proj3.shape
    cos_t, sin_t, decay, xi, zeta, gammac = _retention_tables(S)
    avg = jnp.asarray(_block_avg(LANES), BF16)
    col = lambda c: pl.BlockSpec((1, S, RET_WIDTH), lambda b: (b, 0, c // RET_WIDTH))
    full = lambda a: pl.BlockSpec(a.shape, lambda b: (0,) * a.ndim)
    return pl.pallas_call(
        _retention_body,
        grid=(B,),
        in_specs=[col(COL_QR), col(COL_KR), col(COL_VR), col(COL_GR), full(cos_t), full(sin_t), full(decay),
                  full(xi), full(zeta), full(gammac), full(gain), full(avg)],
        out_specs=pl.BlockSpec((1, S, RET_WIDTH), lambda b: (b, 0, 0)),
        out_shape=jax.ShapeDtypeStruct((B, S, RET_WIDTH), BF16),
        scratch_shapes=[pltpu.VMEM((RET_HEADS // 2, LANES, LANES), F32)],
        compiler_params=_cparams(1),
        name="retention",
    )(proj3, proj3, proj3, proj3, cos_t, sin_t, decay, xi, zeta, gammac, gain, avg)


def _mem_prep_body(mem_ref, g_ref, wkv_ref, kgain_ref, avg_ref, k_ref, v_ref):
    hm = _rms_full(mem_ref[0], g_ref[...]).astype(BF16)
    kv = _dot(hm, wkv_ref[...])
    k = kv[:, :MEM_WIDTH]
    ms = _seg_mean(k * k, avg_ref[...])
    k_ref[0] = (k * lax.rsqrt(ms + EPS) * kgain_ref[...]).astype(BF16)
    v_ref[0] = kv[:, MEM_WIDTH:].astype(BF16)


def _mem_prep(mem, g, wkv, kgain):
    B, M, _ = mem.shape
    avg = jnp.asarray(_block_avg(MEM_WIDTH), BF16)
    full = lambda a: pl.BlockSpec(a.shape, lambda b: (0,) * a.ndim)
    out_spec = pl.BlockSpec((1, M, MEM_WIDTH), lambda b: (b, 0, 0))
    out_shape = jax.ShapeDtypeStruct((B, M, MEM_WIDTH), BF16)
    return pl.pallas_call(
        _mem_prep_body,
        grid=(B,),
        in_specs=[pl.BlockSpec((1, M, D_MODEL), lambda b: (b, 0, 0)), full(g), full(wkv), full(kgain), full(avg)],
        out_specs=[out_spec, out_spec],
        out_shape=[out_shape, out_shape],
        compiler_params=_cparams(1),
        name="mem_prep",
    )(mem, g, wkv, kgain, avg)


def _post_body(x_ref, oa_ref, ob_ref, wout_ref, mk_ref, mv_ref, gx_ref, wq_ref, qgain_ref, avg_ref, wo_ref,
               gf_ref, wr_ref, br_ref, tri_ref, x2_ref, h_ref, route_ref, count_ref):
    tm = x_ref.shape[1] // POST_CHAINS
    chains = range(POST_CHAINS)
    rows = [slice(c * tm, (c + 1) * tm) for c in chains]
    x1 = [x_ref[0, rows[c]] + _dot(oa_ref[0, rows[c]], wout_ref[0:NSA_WIDTH, :])
          + _dot(ob_ref[0, rows[c]], wout_ref[NSA_WIDTH:, :]) for c in chains]

    h = [_rms_full(x1[c], gx_ref[...]).astype(BF16) for c in chains]
    q = [_dot(h[c], wq_ref[...]) for c in chains]
    ms = [_seg_mean(q[c] * q[c], avg_ref[...]) for c in chains]
    q = [q[c] * lax.rsqrt(ms[c] + EPS) * qgain_ref[...] * (HEAD_DIM ** -0.5) for c in chains]
    lane = lax.broadcasted_iota(jnp.int32, (tm, LANES), 1)
    lo = lane < HEAD_DIM
    heads = [(c, p, half) for c in chains for p in range(MEM_HEADS // 2) for half in range(2)]
    s = [_dot_nt(jnp.where(lo if half == 0 else ~lo, q[c][:, p * LANES:(p + 1) * LANES], 0.0).astype(BF16),
                 mk_ref[0, :, p * LANES:(p + 1) * LANES]) for c, p, half in heads]
    e = [jnp.exp(s[i] - jnp.max(s[i], axis=-1, keepdims=True)) for i in range(len(heads))]
    pr = [(e[i] / jnp.sum(e[i], axis=-1, keepdims=True)).astype(BF16) for i in range(len(heads))]
    outs = [_dot(pr[i], mv_ref[0, :, heads[i][1] * LANES:(heads[i][1] + 1) * LANES]) for i in range(len(heads))]
    per_chain = MEM_HEADS
    o = [jnp.concatenate([jnp.where(lo, outs[c * per_chain + 2 * p], outs[c * per_chain + 2 * p + 1])
                          for p in range(MEM_HEADS // 2)], axis=1).astype(BF16) for c in chains]
    x2 = [x1[c] + _dot(o[c], wo_ref[...]) for c in chains]
    for c in chains:
        x2_ref[0, rows[c]] = x2[c]

    hf = [_rms_full(x2[c], gf_ref[...]).astype(BF16) for c in chains]
    for c in chains:
        h_ref[0, rows[c]] = hf[c]
    logits = [_dot(hf[c], wr_ref[...]) + br_ref[...] for c in chains]
    lane_f = lane.astype(F32)
    big = float(LANES)
    picks = []
    for c in chains:
        gl = jnp.where(lane < N_GROUPS, logits[c], BELOW_ALL)
        gmax = jnp.max(gl, axis=-1, keepdims=True)
        grp = jnp.min(jnp.where(gl == gmax, lane_f, big), axis=-1, keepdims=True)
        g_w = 1.0 / jnp.sum(jnp.where(lane < N_GROUPS, jnp.exp(gl - gmax), 0.0), axis=-1, keepdims=True)
        e_lo = N_GROUPS + grp * EXPERTS_PER_GROUP
        el = jnp.where((lane_f >= e_lo) & (lane_f < e_lo + EXPERTS_PER_GROUP), logits[c], BELOW_ALL)
        v0 = jnp.max(el, axis=-1, keepdims=True)
        i0 = jnp.min(jnp.where(el == v0, lane_f, big), axis=-1, keepdims=True)
        el = jnp.where(lane_f == i0, BELOW_ALL, el)
        v1 = jnp.max(el, axis=-1, keepdims=True)
        i1 = jnp.min(jnp.where(el == v1, lane_f, big), axis=-1, keepdims=True)
        e1 = jnp.exp(v1 - v0)
        picks.append((i0 - N_GROUPS, i1 - N_GROUPS, g_w / (1.0 + e1), g_w * e1 / (1.0 + e1)))

    hot = [[lane_f == picks[c][s] for s in range(2)] for c in chains]
    both = jnp.concatenate([jnp.where(hot[c][0], 1.0, 0.0) + jnp.where(hot[c][1], 1.0, 0.0) for c in chains], axis=0)
    before = _dot(tri_ref[...], both.astype(BF16))
    count_ref[0] = jnp.broadcast_to(jnp.sum(both, axis=0, keepdims=True), count_ref.shape[1:])
    for c in chains:
        e0, e1, w0, w1 = picks[c]
        r0 = jnp.sum(jnp.where(hot[c][0], before[rows[c]], 0.0), axis=-1, keepdims=True)
        r1 = jnp.sum(jnp.where(hot[c][1], before[rows[c]], 0.0), axis=-1, keepdims=True)
        cols = (e0, e1, w0, w1, r0, r1)
        route = jnp.zeros((tm, LANES), F32)
        for k in range(len(cols)):
            route = jnp.where(lane == k, cols[k], route)
        route_ref[0, rows[c]] = route


def _post(x, oa, ob, wout, mk, mv, gx, wq, qgain, wo, gf, wr, br):
    B, S, _ = x.shape
    tm = MOE_TM
    n_s = S // tm
    avg = jnp.asarray(_block_avg(MEM_WIDTH), BF16)
    tri = jnp.asarray(np.tril(np.ones((tm, tm), np.float32), -1), BF16)
    full = lambda a: pl.BlockSpec(a.shape, lambda b, i: (0,) * a.ndim)
    per_b = lambda a: pl.BlockSpec((1,) + a.shape[1:], lambda b, i: (b,) + (0,) * (a.ndim - 1))
    tile = lambda w: pl.BlockSpec((1, tm, w), lambda b, i: (b, i, 0))
    return pl.pallas_call(
        _post_body,
        grid=(B, n_s),
        in_specs=[tile(D_MODEL), tile(NSA_WIDTH), tile(RET_WIDTH), full(wout), per_b(mk), per_b(mv), full(gx),
                  full(wq), full(qgain), full(avg), full(wo), full(gf), full(wr), full(br), full(tri)],
        out_specs=[tile(D_MODEL), tile(D_MODEL), tile(LANES),
                   pl.BlockSpec((1, 8, LANES), lambda b, i: (b * n_s + i, 0, 0))],
        out_shape=[jax.ShapeDtypeStruct((B, S, D_MODEL), F32), jax.ShapeDtypeStruct((B, S, D_MODEL), BF16),
                   jax.ShapeDtypeStruct((B, S, LANES), F32), jax.ShapeDtypeStruct((B * n_s, 8, LANES), F32)],
        compiler_params=_cparams(2),
        name="post_mixer",
    )(x, oa, ob, wout, mk, mv, gx, wq, qgain, avg, wo, gf, wr, br, tri)


def _row_copy(src, dst, sem):
    return pltpu.make_async_copy(src, dst, sem)


def _run_pieces(n, max_piece, fn):
    b = RUN_ALIGN
    while b <= max_piece:
        pl.when((n & b) != 0)(functools.partial(fn, n & (-2 * b), b))
        b *= 2


def _move_groups(i, gmap_ref, n_loc, copy):
    n_groups = n_loc // RUN_ALIGN
    for j in range(n_groups):
        glob = pl.multiple_of(gmap_ref[i * n_groups + j], RUN_ALIGN)
        copy(pl.ds(j * RUN_ALIGN, RUN_ALIGN), pl.ds(glob, RUN_ALIGN)).start()


def _local_positions(route, loff_row):
    lane = lax.broadcasted_iota(jnp.int32, route.shape, 1).astype(F32)
    pos = []
    for s in range(2):
        base = jnp.sum(jnp.where(lane == route[:, s:s + 1], loff_row, 0.0), axis=-1, keepdims=True)
        pos.append(base + route[:, 4 + s:5 + s])
    return pos


def _scatter_body(gmap_ref, tstart_ref, tlen_ref, nact_ref,
                  h_ref, route_ref, lofff_ref, xs_ref, xloc, zbuf, sems):
    i = pl.program_id(0)
    tm = h_ref.shape[0]
    n_loc = xloc.shape[1]
    slot = i & 1
    sem = sems.at[0]

    def tails(start):
        def per_expert(e, carry):
            n = tlen_ref[e]
            st = tstart_ref[e]

            def piece(off, size):
                c = _row_copy(zbuf.at[pl.ds(0, size)], xs_ref.at[pl.ds(pl.multiple_of(st + off, RUN_ALIGN), size)], sem)
                c.start() if start else c.wait()

            _run_pieces(n, MOE_RB // 2, piece)
            return carry

        lax.fori_loop(0, N_EXPERTS, per_expert, 0)

    def unused(start):
        rows = zbuf.shape[0]

        def per_unit(u, carry):
            c = _row_copy(zbuf, xs_ref.at[pl.ds(pl.multiple_of(u * rows, rows), rows)], sem)
            c.start() if start else c.wait()
            return carry

        lax.fori_loop(nact_ref[0] * (MOE_RB // rows), xs_ref.shape[0] // rows, per_unit, 0)

    @pl.when(i == 0)
    def _():
        zbuf[...] = jnp.zeros(zbuf.shape, zbuf.dtype)
        tails(True)
        unused(True)
        tails(False)
        unused(False)

    pos = _local_positions(route_ref[...], lofff_ref[0, 0:1, :])
    col = lax.broadcasted_iota(jnp.int32, (tm, n_loc), 1).astype(F32)
    perm_t = jnp.where((col == pos[0]) | (col == pos[1]), 1.0, 0.0).astype(BF16)
    xloc[slot] = _dot_tn(perm_t, h_ref[...]).astype(BF16)

    def wait_slot(s):
        _row_copy(xloc.at[s], xs_ref.at[pl.ds(0, n_loc)], sems.at[s]).wait()

    pl.when(i > 0)(lambda: wait_slot(1 - slot))
    _move_groups(i, gmap_ref, n_loc, lambda loc, glob: _row_copy(xloc.at[slot, loc], xs_ref.at[glob], sems.at[slot]))
    pl.when(i == pl.num_programs(0) - 1)(lambda: wait_slot(slot))


def _scatter_rows(tables, h2d, route2d, loff_f, n_rows):
    T = h2d.shape[0]
    tm = MOE_TM
    n_loc = MOE_NLOC
    tile = lambda w: pl.BlockSpec((tm, w), lambda i, *_: (i, 0))
    grid_spec = pltpu.PrefetchScalarGridSpec(
        num_scalar_prefetch=4,
        grid=(T // tm,),
        in_specs=[tile(D_MODEL), tile(LANES), pl.BlockSpec((1, 8, LANES), lambda i, *_: (i, 0, 0))],
        out_specs=pl.BlockSpec(memory_space=pl.ANY),
        scratch_shapes=[pltpu.VMEM((2, n_loc, D_MODEL), BF16), pltpu.VMEM((MOE_RB // 2, D_MODEL), BF16),
                        pltpu.SemaphoreType.DMA((2,))],
    )
    return pl.pallas_call(
        _scatter_body,
        grid_spec=grid_spec,
        out_shape=jax.ShapeDtypeStruct((n_rows, D_MODEL), BF16),
        compiler_params=_cparams(1),
        name="moe_scatter",
    )(*tables, h2d, route2d, loff_f)


def _expert_body(blk0_ref, nblk_ref, n_act_ref, xs_ref, wg_ref, wu_ref, wd_ref, ys_ref,
                 wg_b, wu_b, wd_b, xbuf, ybuf, sem_in, sem_out):
    e = pl.program_id(0)
    n_slots, rb = xbuf.shape[0], xbuf.shape[1]
    ahead = n_slots - 4
    n_act = n_act_ref[0]
    b0 = blk0_ref[e]

    def rows(g):
        return pl.ds(pl.multiple_of(g * rb, rb), rb)

    def x_copy(g, slot):
        return _row_copy(xs_ref.at[rows(g)], xbuf.at[slot], sem_in.at[slot])

    def y_copy(g, slot):
        return _row_copy(ybuf.at[slot], ys_ref.at[rows(g)], sem_out.at[slot])

    @pl.when(e == 0)
    def _():
        for k in range(ahead):
            pl.when(k < n_act)(lambda k=k: x_copy(k, k).start())

    wg_b[...] = wg_ref[0].astype(BF16)
    wu_b[...] = wu_ref[0].astype(BF16)
    wd_b[...] = wd_ref[0].astype(BF16)

    def blocks(g0, count):
        gs = [g0 + c for c in range(count)]
        slots = [g & (n_slots - 1) for g in gs]
        for c in range(count):
            x_copy(gs[c], slots[c]).wait()
        for c in range(count):
            nxt = gs[c] + ahead
            pl.when(nxt < n_act)(lambda nxt=nxt: x_copy(nxt, nxt & (n_slots - 1)).start())
        x = [xbuf[slots[c]] for c in range(count)]
        a = [_dot(x[c], wg_b[...]) for c in range(count)]
        b = [_dot(x[c], wu_b[...]) for c in range(count)]
        h = [(_silu(a[c]) * b[c]).astype(BF16) for c in range(count)]
        y = [_dot(h[c], wd_b[...]).astype(BF16) for c in range(count)]
        for c in range(count):
            pl.when(gs[c] >= n_slots)(lambda c=c: y_copy(gs[c] - n_slots, slots[c]).wait())
        for c in range(count):
            ybuf[slots[c]] = y[c]
            y_copy(gs[c], slots[c]).start()

    nb = nblk_ref[e]

    n_quad = lax.shift_right_logical(nb, 2)
    rem = nb & 3

    def quad(j, carry):
        blocks(b0 + 4 * j, 4)
        return carry

    lax.fori_loop(0, n_quad, quad, 0)
    for r in range(1, 4):
        pl.when(rem == r)(lambda r=r: blocks(b0 + nb - r, r))

    @pl.when(e == pl.num_programs(0) - 1)
    def _():
        for k in range(1, n_slots + 1):
            pl.when(n_act >= k)(lambda k=k: y_copy(n_act - k, (n_act - k) & (n_slots - 1)).wait())
        ybuf[0] = jnp.zeros(ybuf.shape[1:], ybuf.dtype)
        n_blocks = ys_ref.shape[0] // rb

        def fill(start):
            def per_block(g, carry):
                c = y_copy(g, 0)
                c.start() if start else c.wait()
                return carry

            lax.fori_loop(n_act, n_blocks, per_block, 0)

        fill(True)
        fill(False)


def _experts(blk0, nblk, n_act, xs, n_rows, wg, wu, wd):
    rb = MOE_RB
    weight = lambda a: pl.BlockSpec((1,) + a.shape[1:], lambda e, *_: (e, 0, 0))
    grid_spec = pltpu.PrefetchScalarGridSpec(
        num_scalar_prefetch=3,
        grid=(N_EXPERTS,),
        in_specs=[pl.BlockSpec(memory_space=pl.ANY), weight(wg), weight(wu), weight(wd)],
        out_specs=pl.BlockSpec(memory_space=pl.ANY),
        scratch_shapes=[pltpu.VMEM((D_MODEL, EXPERT_FF), BF16), pltpu.VMEM((D_MODEL, EXPERT_FF), BF16),
                        pltpu.VMEM((EXPERT_FF, D_MODEL), BF16), pltpu.VMEM((MOE_SLOTS, rb, D_MODEL), BF16),
                        pltpu.VMEM((MOE_SLOTS, rb, D_MODEL), BF16), pltpu.SemaphoreType.DMA((MOE_SLOTS,)),
                        pltpu.SemaphoreType.DMA((MOE_SLOTS,))],
    )
    return pl.pallas_call(
        _expert_body,
        grid_spec=grid_spec,
        out_shape=jax.ShapeDtypeStruct((n_rows, D_MODEL), BF16),
        compiler_params=_cparams(1),
        name="moe_experts",
    )(blk0, nblk, n_act, xs, wg, wu, wd)


def _combine_body(gmap_ref, x_ref, route_ref, lofff_ref, ys_ref, o_ref, yloc, sems):
    i = pl.program_id(0)
    tm = x_ref.shape[0]
    n_loc = yloc.shape[1]
    slot = i & 1

    def fetch(tile, s):
        _move_groups(tile, gmap_ref, n_loc, lambda loc, glob: _row_copy(ys_ref.at[glob], yloc.at[s, loc], sems.at[s]))

    pl.when(i == 0)(lambda: fetch(i, slot))
    _row_copy(ys_ref.at[pl.ds(0, n_loc)], yloc.at[slot], sems.at[slot]).wait()
    pl.when(i + 1 < pl.num_programs(0))(lambda: fetch(i + 1, 1 - slot))

    route = route_ref[...]
    pos = _local_positions(route, lofff_ref[0, 0:1, :])
    col = lax.broadcasted_iota(jnp.int32, (tm, n_loc), 1).astype(F32)
    perm_w = jnp.where(col == pos[0], route[:, 2:3], jnp.where(col == pos[1], route[:, 3:4], 0.0)).astype(BF16)
    o_ref[...] = x_ref[...] + _dot(perm_w, yloc[slot])


def _combine(tables, x2d, route2d, loff_f, ys):
    T = x2d.shape[0]
    tm = MOE_TM
    n_loc = MOE_NLOC
    tile = lambda w: pl.BlockSpec((tm, w), lambda i, *_: (i, 0))
    grid_spec = pltpu.PrefetchScalarGridSpec(
        num_scalar_prefetch=1,
        grid=(T // tm,),
        in_specs=[tile(D_MODEL), tile(LANES), pl.BlockSpec((1, 8, LANES), lambda i, *_: (i, 0, 0)),
                  pl.BlockSpec(memory_space=pl.ANY)],
        out_specs=tile(D_MODEL),
        scratch_shapes=[pltpu.VMEM((2, n_loc, D_MODEL), BF16), pltpu.SemaphoreType.DMA((2,))],
    )
    return pl.pallas_call(
        _combine_body,
        grid_spec=grid_spec,
        out_shape=jax.ShapeDtypeStruct((T, D_MODEL), F32),
        compiler_params=_cparams(1),
        name="moe_combine",
    )(*tables, x2d, route2d, loff_f, ys)


def _compress_weights(pos, w2):
    eye = jnp.eye(NSA_GROUPS, dtype=F32)
    w2b = jnp.einsum('hd,gk->ghkd', w2, eye).reshape(NSA_GROUPS * CMP_HIDDEN, LANES)
    posb = jnp.tile(pos, (1, NSA_GROUPS)).reshape(2, 1, CMP_STRIDE * LANES)
    return posb, w2b.astype(BF16)


def _dup2(g):
    return jnp.tile(g.reshape(1, HEAD_DIM), (1, 2))


def kernel(x, mem, mix_norm, w_in, nsa_q_norm, nsa_kcmp_norm, nsa_ksel_norm, nsa_kwin_norm, cmp_pos_k, cmp_pos_v, cmp_k_w1, cmp_k_w2, cmp_v_w1, cmp_v_w2, nsa_out_norm, ret_out_norm, w_out, mem_x_norm, mem_kv_norm, mem_wq, mem_wkv, mem_q_norm, mem_k_norm, mem_wo, ffn_norm, router_group_w, router_group_b, router_expert_w, router_expert_b, exp_w_gate, exp_w_up, exp_w_down):
    B, S, D = x.shape
    T = B * S
    depth = mix_norm.shape[0]
    for l in range(depth):
        proj = _proj(x.reshape(T, D), mix_norm[l].reshape(1, D), w_in[l].T).reshape(B, S, PROJ_PAD)
        pk, w2k = _compress_weights(cmp_pos_k[l], cmp_k_w2[l])
        pv, w2v = _compress_weights(cmp_pos_v[l], cmp_v_w2[l])
        gains = jnp.stack([_dup2(nsa_kcmp_norm[l]), _dup2(nsa_ksel_norm[l]), _dup2(nsa_kwin_norm[l])])
        kcmp, vcmp, ks, vs, kw, vw = _nsa_prep(proj, jnp.stack([pk, pv]), cmp_k_w1[l], cmp_v_w1[l],
                                               jnp.stack([w2k, w2v]), gains)
        o_a = _nsa_attn(proj, kcmp, vcmp, ks, vs, kw, vw,
                        jnp.tile(nsa_q_norm[l].reshape(1, HEAD_DIM), (1, NSA_HEADS)), gains,
                        nsa_out_norm[l].reshape(1, NSA_WIDTH))
        o_b = _retention(proj, ret_out_norm[l].reshape(1, RET_WIDTH))
        mk, mv = _mem_prep(mem, mem_kv_norm[l].reshape(1, D), mem_wkv[l].astype(BF16),
                           jnp.tile(mem_k_norm[l].reshape(1, HEAD_DIM), (1, MEM_HEADS)))
        w_r = jnp.concatenate([router_group_w[l],
                               router_expert_w[l].transpose(1, 0, 2).reshape(D, N_EXPERTS),
                               jnp.zeros((D, LANES - N_GROUPS - N_EXPERTS), F32)], axis=1).astype(BF16)
        b_r = jnp.concatenate([router_group_b[l], router_expert_b[l].reshape(N_EXPERTS),
                               jnp.zeros((LANES - N_GROUPS - N_EXPERTS,), F32)]).reshape(1, LANES)
        x2, hf, route, counts = _post(
            x, o_a, o_b, w_out[l].astype(BF16), mk, mv, mem_x_norm[l].reshape(1, D), mem_wq[l].astype(BF16),
            jnp.tile(mem_q_norm[l].reshape(1, HEAD_DIM), (1, MEM_HEADS)), mem_wo[l].astype(BF16),
            ffn_norm[l].reshape(1, D), w_r, b_r)
        route2d = route.reshape(T, LANES)
        n_tiles = T // MOE_TM
        cnt = counts[:, 0, :N_EXPERTS].astype(jnp.int32)
        cnt = (cnt + RUN_ALIGN - 1) // RUN_ALIGN * RUN_ALIGN
        loff = jnp.cumsum(cnt, axis=1) - cnt
        total = jnp.sum(cnt, axis=0)
        padded = (total + MOE_RB - 1) // MOE_RB * MOE_RB
        pend = jnp.cumsum(padded)
        pstart = pend - padded
        goff = pstart[None, :] + jnp.cumsum(cnt, axis=0) - cnt
        n_rows = 2 * T + n_tiles * N_EXPERTS * RUN_ALIGN + N_EXPERTS * MOE_RB
        n_act = (pend[-1:] // MOE_RB).astype(jnp.int32)
        loff_f = jnp.broadcast_to(jnp.pad(loff.astype(F32), ((0, 0), (0, LANES - N_EXPERTS)))[:, None, :],
                                  (n_tiles, 8, LANES))
        grp_row = jnp.arange(MOE_NLOC // RUN_ALIGN, dtype=jnp.int32) * RUN_ALIGN
        inside = ((loff[:, None, :] <= grp_row[None, :, None])
                  & (grp_row[None, :, None] < (loff + cnt)[:, None, :])).astype(jnp.int32)
        shift = jnp.sum(inside * (goff - loff)[:, None, :], axis=2)
        used = jnp.sum(inside, axis=2) > 0
        gmap_scatter = jnp.where(used, shift + grp_row[None, :], n_rows + grp_row[None, :]).reshape(-1)
        gmap_gather = jnp.where(used, shift + grp_row[None, :], 0).reshape(-1)
        xs = _scatter_rows((gmap_scatter, pstart + total, padded - total, n_act), hf.reshape(T, D), route2d, loff_f,
                           n_rows + MOE_NLOC)
        ys = _experts(pstart // MOE_RB, padded // MOE_RB, n_act, xs, n_rows,
                      exp_w_gate[l], exp_w_up[l], exp_w_down[l])
        x = _combine((gmap_gather,), x2.reshape(T, D), route2d, loff_f, ys).reshape(B, S, D)
    return x
```

```python
import functools

import numpy as np
import jax
import jax.numpy as jnp
from jax import lax
from jax.experimental import pallas as pl
from jax.experimental.pallas import tpu as pltpu

F32 = jnp.float32
BF16 = jnp.bfloat16

D_MODEL = 1024
HEAD_DIM = 64
LANES = 128
NSA_HEADS = 8
NSA_GROUPS = 2
NSA_WIDTH = NSA_HEADS * HEAD_DIM
CMP_BLOCK = 32
CMP_STRIDE = 16
CMP_HIDDEN = 2 * HEAD_DIM
SEL_BLOCK = 64
SEL_TOPK = 8
WINDOW = 512
RET_HEADS = 8
RET_WIDTH = RET_HEADS * HEAD_DIM
RET_CHUNK = 128
ROPE_BASE = 10000.0
MEM_HEADS = 4
MEM_WIDTH = MEM_HEADS * HEAD_DIM
N_GROUPS = 4
EXPERTS_PER_GROUP = 8
N_EXPERTS = N_GROUPS * EXPERTS_PER_GROUP
EXPERT_FF = D_MODEL // 4
EPS = 1e-6
NEG_INF = -1e30
FORCE_SCORE = 1e9
BELOW_ALL = -3e38
MAX_FIXED_SHIFT = 40.0

COL_QA = 0
COL_QR, COL_KR, COL_VR, COL_GR = 512, 1024, 1536, 2048
COL_KVC, COL_KSV, COL_KWV = 2560, 2816, 3072
COL_GATE = 3328
PROJ_PAD = 3456
PROJ_SPLITS = (NSA_WIDTH, NSA_WIDTH + 6 * NSA_GROUPS * HEAD_DIM, NSA_WIDTH + 6 * NSA_GROUPS * HEAD_DIM + 3 * NSA_HEADS)

PROJ_TM = 512
RET_UNROLL = 4
NSA_TQ = 256
SEL_KC = 512
POST_CHAINS = 2
MOE_TM = 512
MOE_RB = 256
MOE_SLOTS = 8
RUN_ALIGN = 16
MOE_NLOC = 2 * MOE_TM + N_EXPERTS * RUN_ALIGN
VMEM_LIMIT = 56 * 1024 * 1024


def _cparams(n_axes):
    return pltpu.CompilerParams(dimension_semantics=("arbitrary",) * n_axes,
                                vmem_limit_bytes=VMEM_LIMIT)


def _dot(a, b):
    return jnp.dot(a, b, preferred_element_type=F32)


def _dot_nt(a, b):
    return lax.dot_general(a, b, (((1,), (1,)), ((), ())), preferred_element_type=F32)


def _dot_tn(a, b):
    return lax.dot_general(a, b, (((0,), (0,)), ((), ())), preferred_element_type=F32)


def _rms_full(x, g):
    ms = jnp.mean(x * x, axis=-1, keepdims=True)
    return x * lax.rsqrt(ms + EPS) * g


def _seg_mean(x, avg):
    return _dot(x.astype(BF16), avg)


def _silu(x):
    return x * (1.0 / (1.0 + jnp.exp(-x)))


def _sigmoid(x):
    return 1.0 / (1.0 + jnp.exp(-x))


def _block_avg(width):
    i = np.arange(width)
    return ((i[:, None] // HEAD_DIM == i[None, :] // HEAD_DIM) / HEAD_DIM).astype(np.float32)


def _proj_body(x_ref, g_ref, wt_ref, o_ref, w_scr):
    @pl.when(pl.program_id(0) == 0)
    def _():
        kv0, gate0, ret0 = PROJ_SPLITS
        w_scr[COL_QA:COL_QR] = wt_ref[0:kv0].astype(BF16)
        w_scr[COL_QR:COL_KVC] = wt_ref[ret0:].astype(BF16)
        w_scr[COL_KVC:COL_GATE] = wt_ref[kv0:gate0].astype(BF16)
        pad = jnp.zeros((PROJ_PAD - COL_GATE - (ret0 - gate0), wt_ref.shape[1]), F32)
        w_scr[COL_GATE:] = jnp.concatenate([wt_ref[gate0:ret0], pad], axis=0).astype(BF16)

    h = _rms_full(x_ref[...], g_ref[...]).astype(BF16)
    step = PROJ_PAD // 3
    for j in range(3):
        o_ref[:, j * step:(j + 1) * step] = _dot_nt(h, w_scr[j * step:(j + 1) * step]).astype(BF16)


def _proj(x2d, g, w_t):
    T = x2d.shape[0]
    assert w_t.shape[0] == PROJ_SPLITS[2] + 4 * RET_WIDTH
    return pl.pallas_call(
        _proj_body,
        grid=(T // PROJ_TM,),
        in_specs=[pl.BlockSpec((PROJ_TM, D_MODEL), lambda i: (i, 0)),
                  pl.BlockSpec((1, D_MODEL), lambda i: (0, 0)),
                  pl.BlockSpec(w_t.shape, lambda i: (0, 0))],
        out_specs=pl.BlockSpec((PROJ_TM, PROJ_PAD), lambda i: (i, 0)),
        out_shape=jax.ShapeDtypeStruct((T, PROJ_PAD), BF16),
        scratch_shapes=[pltpu.VMEM((PROJ_PAD, D_MODEL), BF16)],
        compiler_params=_cparams(1),
        name="proj",
    )(x2d, g, w_t)


def _dup_groups(x):
    lane = lax.broadcasted_iota(jnp.int32, x.shape, 1)
    xs = pltpu.roll(x, HEAD_DIM, axis=1)
    lo = lane < HEAD_DIM
    return jnp.where(lo, x, xs), jnp.where(lo, xs, x)


def _ones_groups(x):
    lane = lax.broadcasted_iota(jnp.int32, x.shape, 1)
    lo = lane < HEAD_DIM
    return jnp.where(lo, x, 1.0), jnp.where(lo, pltpu.roll(x, HEAD_DIM, axis=1), 1.0)


def _nsa_prep_body(kvc_ref, ksv_ref, kwv_ref, pos_ref, w1k_ref, w1v_ref, w2_ref, gain_ref, avg_ref,
                   kcmp_ref, vcmp_ref, ks_ref, vs_ref, kw_ref, vw_ref, scr_k, scr_v, w1_ref):
    @pl.when(pl.program_id(0) == 0)
    def _():
        zero = jnp.zeros((HEAD_DIM, CMP_HIDDEN), BF16)
        for j, src in ((0, w1k_ref), (1, w1v_ref)):
            for l in range(CMP_BLOCK):
                piece = src[l * HEAD_DIM:(l + 1) * HEAD_DIM, :].astype(BF16)
                r0 = (l % CMP_STRIDE) * LANES
                w1_ref[j, l // CMP_STRIDE, r0:r0 + HEAD_DIM, :] = jnp.concatenate([piece, zero], axis=1)
                w1_ref[j, l // CMP_STRIDE, r0 + HEAD_DIM:r0 + LANES, :] = jnp.concatenate([zero, piece], axis=1)

    avg = avg_ref[...]
    n_c = scr_k.shape[0] // CMP_STRIDE
    scr_k[...] = kvc_ref[0, :, 0:LANES].astype(F32)
    scr_v[...] = kvc_ref[0, :, LANES:2 * LANES].astype(F32)
    for j, out_ref, scr in ((0, kcmp_ref, scr_k), (1, vcmp_ref, scr_v)):
        ycat = jnp.concatenate(
            [scr[pl.ds(l, n_c, stride=CMP_STRIDE), :] for l in range(CMP_STRIDE)], axis=1)
        first = _dot((ycat + pos_ref[j, 0]).astype(BF16), w1_ref[j, 0])
        second = _dot((ycat + pos_ref[j, 1]).astype(BF16), w1_ref[j, 1])
        hidden = first + pltpu.roll(second, n_c - 1, axis=0)
        cmp_tok = _dot(_silu(hidden).astype(BF16), w2_ref[j])
        if j == 0:
            ms = _seg_mean(cmp_tok * cmp_tok, avg)
            cmp_tok = cmp_tok * lax.rsqrt(ms + EPS) * gain_ref[0]
        d0, d1 = _dup_groups(cmp_tok) if j == 0 else _ones_groups(cmp_tok)
        out_ref[0, 0] = d0.astype(BF16)
        out_ref[0, 1] = d1.astype(BF16)

    for src_ref, k_out, v_out, gi in ((ksv_ref, ks_ref, vs_ref, 1), (kwv_ref, kw_ref, vw_ref, 2)):
        k = src_ref[0, :, 0:LANES].astype(F32)
        ms = _seg_mean(k * k, avg)
        k = k * lax.rsqrt(ms + EPS) * gain_ref[gi]
        d0, d1 = _dup_groups(k)
        k_out[0, 0] = d0.astype(BF16)
        k_out[0, 1] = d1.astype(BF16)
        d0, d1 = _ones_groups(src_ref[0, :, LANES:2 * LANES].astype(F32))
        v_out[0, 0] = d0.astype(BF16)
        v_out[0, 1] = d1.astype(BF16)


def _nsa_prep(proj3, pos, w1k, w1v, w2, gains):
    B, S, _ = proj3.shape
    n_c = S // CMP_STRIDE
    avg = jnp.asarray(_block_avg(LANES), BF16)
    col = lambda c: pl.BlockSpec((1, S, 2 * LANES), lambda b: (b, 0, c // (2 * LANES)))
    full = lambda a: pl.BlockSpec(a.shape, lambda b: (0,) * a.ndim)
    cmp_spec = pl.BlockSpec((1, NSA_GROUPS, n_c, LANES), lambda b: (b, 0, 0, 0))
    seq_spec = pl.BlockSpec((1, NSA_GROUPS, S, LANES), lambda b: (b, 0, 0, 0))
    cmp_shape = jax.ShapeDtypeStruct((B, NSA_GROUPS, n_c, LANES), BF16)
    seq_shape = jax.ShapeDtypeStruct((B, NSA_GROUPS, S, LANES), BF16)
    return pl.pallas_call(
        _nsa_prep_body,
        grid=(B,),
        in_specs=[col(COL_KVC), col(COL_KSV), col(COL_KWV), full(pos), full(w1k), full(w1v), full(w2), full(gains),
                  full(avg)],
        out_specs=[cmp_spec, cmp_spec, seq_spec, seq_spec, seq_spec, seq_spec],
        out_shape=[cmp_shape, cmp_shape, seq_shape, seq_shape, seq_shape, seq_shape],
        scratch_shapes=[pltpu.VMEM((S, LANES), F32), pltpu.VMEM((S, LANES), F32),
                        pltpu.VMEM((2, 2, CMP_STRIDE * LANES, NSA_GROUPS * CMP_HIDDEN), BF16)],
        compiler_params=_cparams(1),
        name="nsa_prep",
    )(proj3, proj3, proj3, pos, w1k, w1v, w2, gains, avg)


def _nsa_attn_body(q_ref, gate_ref, kcmp_ref, vcmp_ref, ks_ref, vs_ref, kw_ref, vw_ref,
                   qgain_ref, kgain_ref, ogain_ref, avgq_ref, avgo_ref, msct_ref, esel_ref, egate_ref, wbias_ref,
                   dbias_ref, cbias_ref,
                   o_ref, m_scr, acc_scr):
    tq = q_ref.shape[1]
    n_cmp = kcmp_ref.shape[2]
    n_sel = msct_ref.shape[0]
    kc_len = esel_ref.shape[2]
    rows = 4 * tq
    qi = pl.program_id(1)
    q0 = qi * tq

    q = q_ref[0].astype(F32)
    ms = _seg_mean(q * q, avgq_ref[...])
    qn = q * lax.rsqrt(ms + EPS) * qgain_ref[...] * (HEAD_DIM ** -0.5)

    gate_sig = _sigmoid(gate_ref[0].astype(F32)).astype(BF16)
    gates = [_dot(gate_sig, egate_ref[j]) for j in range(3)]

    lane_q = lax.broadcasted_iota(jnp.int32, (tq, LANES), 1)
    lo_q = lane_q < HEAD_DIM
    lo_r = lax.broadcasted_iota(jnp.int32, (rows, LANES), 1) < HEAD_DIM

    blk = lax.broadcasted_iota(jnp.int32, (n_sel, tq), 0)
    cur = lax.shift_right_logical(q0 + lax.broadcasted_iota(jnp.int32, (n_sel, tq), 1), int(np.log2(SEL_BLOCK)))
    forced = (blk == 0) | (blk == cur) | (blk == cur - 1)
    future = blk > cur
    blk_f = blk.astype(F32)

    def heads4(x):
        return jnp.concatenate([x] * 4, axis=0)

    def normalised_pairs(acc, guard):
        rolled = pltpu.roll(acc, HEAD_DIM, axis=1)
        den = jnp.where(lo_r, rolled, acc)
        if guard:
            den = jnp.maximum(den, 1e-30)
        out = []
        for p in range(2):
            ev = slice((2 * p) * tq, (2 * p + 1) * tq)
            od = slice((2 * p + 1) * tq, (2 * p + 2) * tq)
            out.append(jnp.where(lo_q, acc[ev] / den[ev], rolled[od] / den[od]))
        return out

    groups = range(NSA_GROUPS)
    qs = []
    for g in groups:
        slabs = [qn[:, (2 * g + p) * LANES:(2 * g + p + 1) * LANES] for p in range(2)]
        qs.append(jnp.concatenate(
            [jnp.where(lo_q, slabs[0], 0.0), jnp.where(lo_q, 0.0, slabs[0]),
             jnp.where(lo_q, slabs[1], 0.0), jnp.where(lo_q, 0.0, slabs[1])], axis=0).astype(BF16))

    def compressed_and_select(shift):
        s_c = [_dot_nt(qs[g], kcmp_ref[0, g]) for g in groups]
        if shift is None:
            r_c = lax.broadcasted_iota(jnp.int32, (rows, n_cmp), 0)
            c_c = lax.broadcasted_iota(jnp.int32, (rows, n_cmp), 1)
            cmask = (c_c * CMP_STRIDE + (CMP_BLOCK - 1)) <= q0 + (r_c & (tq - 1))
            s_c = [jnp.where(cmask, s_c[g], NEG_INF) for g in groups]
            e_c = [jnp.where(cmask, jnp.exp(s_c[g] - jnp.max(s_c[g], axis=-1, keepdims=True)), 0.0) for g in groups]
        else:
            bias = heads4(cbias_ref[qi] - shift)
            e_c = [jnp.exp(s_c[g] + bias) for g in groups]
        e_b = [e_c[g].astype(BF16) for g in groups]
        acc_c = [_dot(e_b[g], vcmp_ref[0, g]) for g in groups]
        ones = jnp.ones((8, n_cmp), BF16)
        imp = []
        for g in groups:
            num = [_dot_nt(msct_ref[...], e_b[g][h * tq:(h + 1) * tq]) for h in range(4)]
            den = [_dot_nt(ones, e_b[g][h * tq:(h + 1) * tq])[0:1] for h in range(4)]
            parts = [num[h] / jnp.maximum(den[h], 1e-30) for h in range(4)]
            imp.append((parts[0] + parts[1]) + (parts[2] + parts[3]))
        v = [jnp.where(forced, FORCE_SCORE, jnp.where(future, NEG_INF, imp[g])) for g in groups]
        sel = [jnp.zeros((n_sel, tq), F32) for g in groups]
        for _ in range(SEL_TOPK):
            mx = [jnp.max(v[g], axis=0, keepdims=True) for g in groups]
            first = [jnp.min(jnp.where(v[g] == mx[g], blk_f, float(LANES)), axis=0, keepdims=True) for g in groups]
            pick = [blk_f == first[g] for g in groups]
            sel = [jnp.where(pick[g], 1.0, sel[g]) for g in groups]
            v = [jnp.where(pick[g], BELOW_ALL, v[g]) for g in groups]
        return [normalised_pairs(acc_c[g], True) for g in groups], [sel[g].astype(BF16) for g in groups]

    n_before = lax.shift_right_logical(q0, int(np.log2(kc_len)))
    causal = dbias_ref[qi & (kc_len // tq - 1)]
    w0 = pl.multiple_of(jnp.maximum(q0 - WINDOW, 0), tq)
    n_w = WINDOW + tq
    w_case = jnp.minimum(qi, WINDOW // tq)

    def sel_keys(ref, g, kc):
        return ref[0, g, pl.ds(pl.multiple_of(kc * kc_len, kc_len), kc_len), :]

    def sel_scores(sel_b, g, kc, causal_bias, shift):
        chosen = _dot_tn(sel_b[g], esel_ref[kc])
        bias = (chosen - 1.0) * (-NEG_INF)
        if causal_bias is not None:
            bias = bias + causal_bias
        if shift is not None:
            bias = bias - shift
        return _dot_nt(qs[g], sel_keys(ks_ref, g, kc)) + heads4(bias)

    def win_scores(g, shift):
        bias = wbias_ref[w_case] if shift is None else wbias_ref[w_case] - shift
        return _dot_nt(qs[g], kw_ref[0, g, pl.ds(w0, n_w), :]) + heads4(bias)

    def win_values(g):
        return vw_ref[0, g, pl.ds(w0, n_w), :]

    def finish(cmp_s, acc_w):
        for g in groups:
            sel_s = normalised_pairs(acc_scr[g], False)
            win_s = normalised_pairs(acc_w[g], False)
            for p in range(2):
                cols = slice((2 * g + p) * LANES, (2 * g + p + 1) * LANES)
                mix = gates[0][:, cols] * cmp_s[g][p] + gates[1][:, cols] * sel_s[p] + gates[2][:, cols] * win_s[p]
                ms_o = _seg_mean(mix * mix, avgo_ref[...])
                o_ref[0, :, cols] = (mix * lax.rsqrt(ms_o + EPS) * ogain_ref[:, cols]).astype(BF16)

    def fixed_shift_path(shift):
        cmp_s, sel_b = compressed_and_select(shift)

        def probs(s):
            return jnp.exp(s).astype(BF16)

        for g in groups:
            acc_scr[g] = jnp.zeros(acc_scr.shape[1:], F32)

        def before(kc, carry):
            s = [sel_scores(sel_b, g, kc, None, shift) for g in groups]
            p = [probs(s[g]) for g in groups]
            for g in groups:
                acc_scr[g] = acc_scr[g] + _dot(p[g], sel_keys(vs_ref, g, kc))
            return carry

        lax.fori_loop(0, n_before, before, 0)
        s_d0 = sel_scores(sel_b, 0, n_before, causal, shift)
        s_d1 = sel_scores(sel_b, 1, n_before, causal, shift)
        p_d0 = probs(s_d0)
        s_w0 = win_scores(0, shift)
        acc_scr[0] = acc_scr[0] + _dot(p_d0, sel_keys(vs_ref, 0, n_before))
        p_d1 = probs(s_d1)
        s_w1 = win_scores(1, shift)
        acc_scr[1] = acc_scr[1] + _dot(p_d1, sel_keys(vs_ref, 1, n_before))
        p_w0 = probs(s_w0)
        acc_w0 = _dot(p_w0, win_values(0))
        p_w1 = probs(s_w1)
        acc_w1 = _dot(p_w1, win_values(1))
        finish(cmp_s, [acc_w0, acc_w1])

    def online_path():
        cmp_s, sel_b = compressed_and_select(None)
        for g in groups:
            m_scr[g] = jnp.full(m_scr.shape[1:], NEG_INF, F32)
            acc_scr[g] = jnp.zeros(acc_scr.shape[1:], F32)

        def sel_softmax(g, s):
            m_old = m_scr[g]
            m_new = jnp.maximum(m_old, jnp.max(s, axis=-1, keepdims=True))
            m_scr[g] = m_new
            return jnp.exp(s - m_new).astype(BF16), jnp.exp(m_old - m_new)

        def sel_accumulate(g, kc, p, alpha):
            acc_scr[g] = alpha * acc_scr[g] + _dot(p, sel_keys(vs_ref, g, kc))

        def win_softmax(s):
            return jnp.exp(s - jnp.max(s, axis=-1, keepdims=True)).astype(BF16)

        def before(kc, carry):
            s = [sel_scores(sel_b, g, kc, None, None) for g in groups]
            pa = [sel_softmax(g, s[g]) for g in groups]
            for g in groups:
                sel_accumulate(g, kc, *pa[g])
            return carry

        lax.fori_loop(0, n_before, before, 0)
        s_d0 = sel_scores(sel_b, 0, n_before, causal, None)
        s_d1 = sel_scores(sel_b, 1, n_before, causal, None)
        pa0 = sel_softmax(0, s_d0)
        s_w0 = win_scores(0, None)
        sel_accumulate(0, n_before, *pa0)
        pa1 = sel_softmax(1, s_d1)
        s_w1 = win_scores(1, None)
        sel_accumulate(1, n_before, *pa1)
        acc_w0 = _dot(win_softmax(s_w0), win_values(0))
        acc_w1 = _dot(win_softmax(s_w1), win_values(1))
        finish(cmp_s, [acc_w0, acc_w1])

    bound = 1.01 * (HEAD_DIM ** 0.5) * jnp.max(jnp.abs(qgain_ref[...])) * jnp.max(jnp.abs(kgain_ref[...]))
    safe = bound <= MAX_FIXED_SHIFT
    pl.when(safe)(lambda: fixed_shift_path(bound))
    pl.when(jnp.logical_not(safe))(online_path)


def _sel_from_cmp(n_cmp, n_sel):
    c0 = np.arange(n_cmp) * CMP_STRIDE
    s0 = np.arange(n_sel) * SEL_BLOCK
    ov = np.minimum(c0[None, :] + CMP_BLOCK, s0[:, None] + SEL_BLOCK) - np.maximum(c0[None, :], s0[:, None])
    m = (np.clip(ov, 0, None) / CMP_BLOCK).astype(np.float32)
    m[:, (np.arange(n_cmp) * CMP_STRIDE + CMP_BLOCK) > n_sel * SEL_BLOCK] = 0.0
    return m


def _nsa_attn(proj3, kcmp, vcmp, ks, vs, kw, vw, q_gain, k_gains, o_gain):
    B, S, _ = proj3.shape
    n_cmp = kcmp.shape[2]
    n_sel = S // SEL_BLOCK
    tq = NSA_TQ
    assert n_sel % 8 == 0 and S % SEL_KC == 0 and SEL_KC % tq == 0 and WINDOW % tq == 0 and S >= WINDOW + tq
    avgq = jnp.asarray(_block_avg(NSA_WIDTH), BF16)
    avgo = jnp.asarray(_block_avg(LANES), BF16)
    msct = jnp.asarray(_sel_from_cmp(n_cmp, n_sel), BF16)
    esel = (np.arange(n_sel)[:, None] == np.arange(S)[None, :] // SEL_BLOCK).astype(np.float32)
    esel = jnp.asarray(esel.reshape(n_sel, S // SEL_KC, SEL_KC).transpose(1, 0, 2), BF16)
    src = np.arange(LANES)[:, None]
    dst = np.arange(NSA_WIDTH)[None, :]
    egate = jnp.asarray(np.stack([(src == (dst // HEAD_DIM) * 3 + j) for j in range(3)]).astype(np.float32), BF16)
    r = np.arange(tq)[:, None]
    n_w = WINDOW + tq
    wcases = []
    for i in range(WINDOW // tq + 1):
        diff = (i * tq - max(i * tq - WINDOW, 0)) + r - np.arange(n_w)[None, :]
        wcases.append(np.where((diff >= 0) & (diff < WINDOW), 0.0, NEG_INF))
    wbias = jnp.asarray(np.stack(wcases), F32)
    dbias = jnp.asarray(np.stack([np.where(np.arange(SEL_KC)[None, :] <= i * tq + r, 0.0, NEG_INF)
                                  for i in range(SEL_KC // tq)]), F32)
    c_end = np.arange(n_cmp)[None, :] * CMP_STRIDE + (CMP_BLOCK - 1)
    cbias = jnp.asarray(np.stack([np.where(c_end <= i * tq + r, 0.0, NEG_INF) for i in range(S // tq)]), F32)

    full = lambda a: pl.BlockSpec(a.shape, lambda b, i: (0,) * a.ndim)
    per_b = lambda a: pl.BlockSpec((1,) + a.shape[1:], lambda b, i: (b,) + (0,) * (a.ndim - 1))
    return pl.pallas_call(
        _nsa_attn_body,
        grid=(B, S // tq),
        in_specs=[pl.BlockSpec((1, tq, NSA_WIDTH), lambda b, i: (b, i, COL_QA // NSA_WIDTH)),
                  pl.BlockSpec((1, tq, LANES), lambda b, i: (b, i, COL_GATE // LANES)),
                  per_b(kcmp), per_b(vcmp), per_b(ks), per_b(vs), per_b(kw), per_b(vw),
                  full(q_gain), full(k_gains), full(o_gain), full(avgq), full(avgo), full(msct), full(esel), full(egate),
                  full(wbias), full(dbias), full(cbias)],
        out_specs=pl.BlockSpec((1, tq, NSA_WIDTH), lambda b, i: (b, i, 0)),
        out_shape=jax.ShapeDtypeStruct((B, S, NSA_WIDTH), BF16),
        scratch_shapes=[pltpu.VMEM((NSA_GROUPS, 4 * tq, 1), F32), pltpu.VMEM((NSA_GROUPS, 4 * tq, LANES), F32)],
        compiler_params=_cparams(2),
        name="nsa_attn",
    )(proj3, proj3, kcmp, vcmp, ks, vs, kw, vw, q_gain, k_gains, o_gain, avgq, avgo, msct, esel, egate, wbias, dbias,
      cbias)


def _retention_body(q_ref, k_ref, v_ref, g_ref, cos_ref, sin_ref, decay_ref, xi_ref, zeta_ref, gammac_ref,
                    gain_ref, avg_ref, o_ref, state_scr):
    S = q_ref.shape[1]
    C = RET_CHUNK
    lane = lax.broadcasted_iota(jnp.int32, (C, LANES), 1)
    lo = lane < HEAD_DIM
    first_half = (lane & (HEAD_DIM - 1)) < HEAD_DIM // 2
    r = lax.broadcasted_iota(jnp.int32, (LANES, LANES), 0)
    c = lax.broadcasted_iota(jnp.int32, (LANES, LANES), 1)
    same_head = (r < HEAD_DIM) == (c < HEAD_DIM)
    avg = avg_ref[...]
    state_scr[...] = jnp.zeros(state_scr.shape, F32)

    def rope(x, cos, sin):
        swapped = jnp.where(first_half, pltpu.roll(x, LANES - HEAD_DIM // 2, axis=1),
                            pltpu.roll(x, HEAD_DIM // 2, axis=1))
        return x * cos + swapped * sin

    n_pairs = RET_HEADS // 2
    cols = [slice(p * LANES, (p + 1) * LANES) for p in range(n_pairs)]
    units = [(u, p) for u in range(RET_UNROLL) for p in range(n_pairs)]

    def chunks(n, carry):
        r0 = [pl.multiple_of((n * RET_UNROLL + u) * C, C) for u in range(RET_UNROLL)]
        cos = [cos_ref[pl.ds(r0[u], C), :] for u in range(RET_UNROLL)]
        sin = [sin_ref[pl.ds(r0[u], C), :] for u in range(RET_UNROLL)]
        q = [rope(q_ref[0, pl.ds(r0[u], C), cols[p]].astype(F32), cos[u], sin[u]) for u, p in units]
        k = [rope(k_ref[0, pl.ds(r0[u], C), cols[p]].astype(F32), cos[u], sin[u]) * (HEAD_DIM ** -0.5)
             for u, p in units]
        vb = [v_ref[0, pl.ds(r0[u], C), cols[p]] for u, p in units]
        kb = [k[i].astype(BF16) for i in range(len(units))]
        inner = [_dot_nt(jnp.where(lo if half == 0 else ~lo, q[i], 0.0).astype(BF16), kb[i])
                 * decay_ref[2 * units[i][1] + half] for i in range(len(units)) for half in range(2)]
        upd = [_dot_tn((k[i] * zeta_ref[units[i][1]]).astype(BF16), vb[i]) for i in range(len(units))]
        state = [state_scr[p] for p in range(n_pairs)]
        for u in range(RET_UNROLL):
            for p in range(n_pairs):
                prev = state[u * n_pairs + p]
                state.append(gammac_ref[p] * prev + jnp.where(same_head, upd[u * n_pairs + p], 0.0))
        cross = [_dot(q[i].astype(BF16), state[i].astype(BF16)) * xi_ref[units[i][1]] for i in range(len(units))]
        outs = [_dot(inner[j].astype(BF16), vb[j // 2]) for j in range(2 * len(units))]
        for p in range(n_pairs):
            state_scr[p] = state[RET_UNROLL * n_pairs + p]
        y = jnp.concatenate([jnp.where(lo, outs[2 * i], outs[2 * i + 1]) + cross[i] for i in range(len(units))],
                            axis=0)
        mu = _seg_mean(y, avg)
        d = y - mu
        var = _seg_mean(d * d, avg)
        yn = d * lax.rsqrt(var + EPS)
        for i, (u, p) in enumerate(units):
            gate = g_ref[0, pl.ds(r0[u], C), cols[p]].astype(F32)
            o_ref[0, pl.ds(r0[u], C), cols[p]] = (_silu(gate) * (yn[i * C:(i + 1) * C] * gain_ref[:, cols[p]])).astype(BF16)
        return carry

    lax.fori_loop(0, S // (C * RET_UNROLL), chunks, 0)


def _retention_tables(S):
    half = HEAD_DIM // 2
    inv_freq = ROPE_BASE ** (-jnp.arange(half, dtype=F32) / half)
    ang = jnp.arange(S, dtype=F32)[:, None] * inv_freq[None, :]
    cos, sin = jnp.cos(ang), jnp.sin(ang)
    cos_t = jnp.tile(cos, (1, 4))
    sin_t = jnp.tile(jnp.concatenate([-sin, sin], axis=1), (1, 2))
    C = RET_CHUNK
    H = RET_HEADS
    log_gamma = jnp.log1p(-jnp.power(2.0, -5.0 - jnp.arange(H, dtype=F32)))
    i = jnp.arange(C, dtype=F32)
    rel = i[:, None] - i[None, :]
    decay = jnp.where(rel >= 0, jnp.exp(jnp.maximum(rel, 0.0)[None] * log_gamma[:, None, None]), 0.0)
    xi = jnp.exp((i + 1.0)[:, None] * log_gamma[None, :])
    zeta = jnp.exp((C - 1.0 - i)[:, None] * log_gamma[None, :])
    gamma_c = jnp.exp(C * log_gamma)
    per_pair = lambda t: jnp.repeat(t.T.reshape(H // 2, 2, -1), HEAD_DIM, axis=1).transpose(0, 2, 1)
    gammac = jnp.repeat(gamma_c.reshape(H // 2, 2), HEAD_DIM, axis=1)[:, None, :]
    return cos_t, sin_t, decay, per_pair(xi), per_pair(zeta), gammac


def _retention(proj3, gain):
    B, S, _ = proj3.shape
    cos_t, sin_t, decay, xi, zeta, gammac = _retention_tables(S)
    avg = jnp.asarray(_block_avg(LANES), BF16)
    col = lambda c: pl.BlockSpec((1, S, RET_WIDTH), lambda b: (b, 0, c // RET_WIDTH))
    full = lambda a: pl.BlockSpec(a.shape, lambda b: (0,) * a.ndim)
    return pl.pallas_call(
        _retention_body,
        grid=(B,),
        in_specs=[col(COL_QR), col(COL_KR), col(COL_VR), col(COL_GR), full(cos_t), full(sin_t), full(decay),
                  full(xi), full(zeta), full(gammac), full(gain), full(avg)],
        out_specs=pl.BlockSpec((1, S, RET_WIDTH), lambda b: (b, 0, 0)),
        out_shape=jax.ShapeDtypeStruct((B, S, RET_WIDTH), BF16),
        scratch_shapes=[pltpu.VMEM((RET_HEADS // 2, LANES, LANES), F32)],
        compiler_params=_cparams(1),
        name="retention",
    )(proj3, proj3, proj3, proj3, cos_t, sin_t, decay, xi, zeta, gammac, gain, avg)


def _mem_prep_body(mem_ref, g_ref, wkv_ref, kgain_ref, avg_ref, k_ref, v_ref):
    hm = _rms_full(mem_ref[0], g_ref[...]).astype(BF16)
    kv = _dot(hm, wkv_ref[...])
    k = kv[:, :MEM_WIDTH]
    ms = _seg_mean(k * k, avg_ref[...])
    k_ref[0] = (k * lax.rsqrt(ms + EPS) * kgain_ref[...]).astype(BF16)
    v_ref[0] = kv[:, MEM_WIDTH:].astype(BF16)


def _mem_prep(mem, g, wkv, kgain):
    B, M, _ = mem.shape
    avg = jnp.asarray(_block_avg(MEM_WIDTH), BF16)
    full = lambda a: pl.BlockSpec(a.shape, lambda b: (0,) * a.ndim)
    out_spec = pl.BlockSpec((1, M, MEM_WIDTH), lambda b: (b, 0, 0))
    out_shape = jax.ShapeDtypeStruct((B, M, MEM_WIDTH), BF16)
    return pl.pallas_call(
        _mem_prep_body,
        grid=(B,),
        in_specs=[pl.BlockSpec((1, M, D_MODEL), lambda b: (b, 0, 0)), full(g), full(wkv), full(kgain), full(avg)],
        out_specs=[out_spec, out_spec],
        out_shape=[out_shape, out_shape],
        compiler_params=_cparams(1),
        name="mem_prep",
    )(mem, g, wkv, kgain, avg)


def _post_body(x_ref, oa_ref, ob_ref, wout_ref, mk_ref, mv_ref, gx_ref, wq_ref, qgain_ref, avg_ref, wo_ref,
               gf_ref, wr_ref, br_ref, tri_ref, x2_ref, h_ref, route_ref, count_ref):
    tm = x_ref.shape[1] // POST_CHAINS
    chains = range(POST_CHAINS)
    rows = [slice(c * tm, (c + 1) * tm) for c in chains]
    x1 = [x_ref[0, rows[c]] + _dot(oa_ref[0, rows[c]], wout_ref[0:NSA_WIDTH, :])
          + _dot(ob_ref[0, rows[c]], wout_ref[NSA_WIDTH:, :]) for c in chains]

    h = [_rms_full(x1[c], gx_ref[...]).astype(BF16) for c in chains]
    q = [_dot(h[c], wq_ref[...]) for c in chains]
    ms = [_seg_mean(q[c] * q[c], avg_ref[...]) for c in chains]
    q = [q[c] * lax.rsqrt(ms[c] + EPS) * qgain_ref[...] * (HEAD_DIM ** -0.5) for c in chains]
    lane = lax.broadcasted_iota(jnp.int32, (tm, LANES), 1)
    lo = lane < HEAD_DIM
    heads = [(c, p, half) for c in chains for p in range(MEM_HEADS // 2) for half in range(2)]
    s = [_dot_nt(jnp.where(lo if half == 0 else ~lo, q[c][:, p * LANES:(p + 1) * LANES], 0.0).astype(BF16),
                 mk_ref[0, :, p * LANES:(p + 1) * LANES]) for c, p, half in heads]
    e = [jnp.exp(s[i] - jnp.max(s[i], axis=-1, keepdims=True)) for i in range(len(heads))]
    pr = [(e[i] / jnp.sum(e[i], axis=-1, keepdims=True)).astype(BF16) for i in range(len(heads))]
    outs = [_dot(pr[i], mv_ref[0, :, heads[i][1] * LANES:(heads[i][1] + 1) * LANES]) for i in range(len(heads))]
    per_chain = MEM_HEADS
    o = [jnp.concatenate([jnp.where(lo, outs[c * per_chain + 2 * p], outs[c * per_chain + 2 * p + 1])
                          for p in range(MEM_HEADS // 2)], axis=1).astype(BF16) for c in chains]
    x2 = [x1[c] + _dot(o[c], wo_ref[...]) for c in chains]
    for c in chains:
        x2_ref[0, rows[c]] = x2[c]

    hf = [_rms_full(x2[c], gf_ref[...]).astype(BF16) for c in chains]
    for c in chains:
        h_ref[0, rows[c]] = hf[c]
    logits = [_dot(hf[c], wr_ref[...]) + br_ref[...] for c in chains]
    lane_f = lane.astype(F32)
    big = float(LANES)
    picks = []
    for c in chains:
        gl = jnp.where(lane < N_GROUPS, logits[c], BELOW_ALL)
        gmax = jnp.max(gl, axis=-1, keepdims=True)
        grp = jnp.min(jnp.where(gl == gmax, lane_f, big), axis=-1, keepdims=True)
        g_w = 1.0 / jnp.sum(jnp.where(lane < N_GROUPS, jnp.exp(gl - gmax), 0.0), axis=-1, keepdims=True)
        e_lo = N_GROUPS + grp * EXPERTS_PER_GROUP
        el = jnp.where((lane_f >= e_lo) & (lane_f < e_lo + EXPERTS_PER_GROUP), logits[c], BELOW_ALL)
        v0 = jnp.max(el, axis=-1, keepdims=True)
        i0 = jnp.min(jnp.where(el == v0, lane_f, big), axis=-1, keepdims=True)
        el = jnp.where(lane_f == i0, BELOW_ALL, el)
        v1 = jnp.max(el, axis=-1, keepdims=True)
        i1 = jnp.min(jnp.where(el == v1, lane_f, big), axis=-1, keepdims=True)
        e1 = jnp.exp(v1 - v0)
        picks.append((i0 - N_GROUPS, i1 - N_GROUPS, g_w / (1.0 + e1), g_w * e1 / (1.0 + e1)))

    hot = [[lane_f == picks[c][s] for s in range(2)] for c in chains]
    both = jnp.concatenate([jnp.where(hot[c][0], 1.0, 0.0) + jnp.where(hot[c][1], 1.0, 0.0) for c in chains], axis=0)
    before = _dot(tri_ref[...], both.astype(BF16))
    count_ref[0] = jnp.broadcast_to(jnp.sum(both, axis=0, keepdims=True), count_ref.shape[1:])
    for c in chains:
        e0, e1, w0, w1 = picks[c]
        r0 = jnp.sum(jnp.where(hot[c][0], before[rows[c]], 0.0), axis=-1, keepdims=True)
        r1 = jnp.sum(jnp.where(hot[c][1], before[rows[c]], 0.0), axis=-1, keepdims=True)
        cols = (e0, e1, w0, w1, r0, r1)
        route = jnp.zeros((tm, LANES), F32)
        for k in range(len(cols)):
            route = jnp.where(lane == k, cols[k], route)
        route_ref[0, rows[c]] = route


def _post(x, oa, ob, wout, mk, mv, gx, wq, qgain, wo, gf, wr, br):
    B, S, _ = x.shape
    tm = MOE_TM
    n_s = S // tm
    avg = jnp.asarray(_block_avg(MEM_WIDTH), BF16)
    tri = jnp.asarray(np.tril(np.ones((tm, tm), np.float32), -1), BF16)
    full = lambda a: pl.BlockSpec(a.shape, lambda b, i: (0,) * a.ndim)
    per_b = lambda a: pl.BlockSpec((1,) + a.shape[1:], lambda b, i: (b,) + (0,) * (a.ndim - 1))
    tile = lambda w: pl.BlockSpec((1, tm, w), lambda b, i: (b, i, 0))
    return pl.pallas_call(
        _post_body,
        grid=(B, n_s),
        in_specs=[tile(D_MODEL), tile(NSA_WIDTH), tile(RET_WIDTH), full(wout), per_b(mk), per_b(mv), full(gx),
                  full(wq), full(qgain), full(avg), full(wo), full(gf), full(wr), full(br), full(tri)],
        out_specs=[tile(D_MODEL), tile(D_MODEL), tile(LANES),
                   pl.BlockSpec((1, 8, LANES), lambda b, i: (b * n_s + i, 0, 0))],
        out_shape=[jax.ShapeDtypeStruct((B, S, D_MODEL), F32), jax.ShapeDtypeStruct((B, S, D_MODEL), BF16),
                   jax.ShapeDtypeStruct((B, S, LANES), F32), jax.ShapeDtypeStruct((B * n_s, 8, LANES), F32)],
        compiler_params=_cparams(2),
        name="post_mixer",
    )(x, oa, ob, wout, mk, mv, gx, wq, qgain, avg, wo, gf, wr, br, tri)


def _row_copy(src, dst, sem):
    return pltpu.make_async_copy(src, dst, sem)


def _run_pieces(n, max_piece, fn):
    b = RUN_ALIGN
    while b <= max_piece:
        pl.when((n & b) != 0)(functools.partial(fn, n & (-2 * b), b))
        b *= 2


def _move_groups(i, gmap_ref, n_loc, copy):
    n_groups = n_loc // RUN_ALIGN
    for j in range(n_groups):
        glob = pl.multiple_of(gmap_ref[i * n_groups + j], RUN_ALIGN)
        copy(pl.ds(j * RUN_ALIGN, RUN_ALIGN), pl.ds(glob, RUN_ALIGN)).start()


def _local_positions(route, loff_row):
    lane = lax.broadcasted_iota(jnp.int32, route.shape, 1).astype(F32)
    pos = []
    for s in range(2):
        base = jnp.sum(jnp.where(lane == route[:, s:s + 1], loff_row, 0.0), axis=-1, keepdims=True)
        pos.append(base + route[:, 4 + s:5 + s])
    return pos


def _scatter_body(gmap_ref, tstart_ref, tlen_ref, nact_ref,
                  h_ref, route_ref, lofff_ref, xs_ref, xloc, zbuf, sems):
    i = pl.program_id(0)
    tm = h_ref.shape[0]
    n_loc = xloc.shape[1]
    slot = i & 1
    sem = sems.at[0]

    def tails(start):
        def per_expert(e, carry):
            n = tlen_ref[e]
            st = tstart_ref[e]

            def piece(off, size):
                c = _row_copy(zbuf.at[pl.ds(0, size)], xs_ref.at[pl.ds(pl.multiple_of(st + off, RUN_ALIGN), size)], sem)
                c.start() if start else c.wait()

            _run_pieces(n, MOE_RB // 2, piece)
            return carry

        lax.fori_loop(0, N_EXPERTS, per_expert, 0)

    def unused(start):
        rows = zbuf.shape[0]

        def per_unit(u, carry):
            c = _row_copy(zbuf, xs_ref.at[pl.ds(pl.multiple_of(u * rows, rows), rows)], sem)
            c.start() if start else c.wait()
            return carry

        lax.fori_loop(nact_ref[0] * (MOE_RB // rows), xs_ref.shape[0] // rows, per_unit, 0)

    @pl.when(i == 0)
    def _():
        zbuf[...] = jnp.zeros(zbuf.shape, zbuf.dtype)
        tails(True)
        unused(True)
        tails(False)
        unused(False)

    pos = _local_positions(route_ref[...], lofff_ref[0, 0:1, :])
    col = lax.broadcasted_iota(jnp.int32, (tm, n_loc), 1).astype(F32)
    perm_t = jnp.where((col == pos[0]) | (col == pos[1]), 1.0, 0.0).astype(BF16)
    xloc[slot] = _dot_tn(perm_t, h_ref[...]).astype(BF16)

    def wait_slot(s):
        _row_copy(xloc.at[s], xs_ref.at[pl.ds(0, n_loc)], sems.at[s]).wait()

    pl.when(i > 0)(lambda: wait_slot(1 - slot))
    _move_groups(i, gmap_ref, n_loc, lambda loc, glob: _row_copy(xloc.at[slot, loc], xs_ref.at[glob], sems.at[slot]))
    pl.when(i == pl.num_programs(0) - 1)(lambda: wait_slot(slot))


def _scatter_rows(tables, h2d, route2d, loff_f, n_rows):
    T = h2d.shape[0]
    tm = MOE_TM
    n_loc = MOE_NLOC
    tile = lambda w: pl.BlockSpec((tm, w), lambda i, *_: (i, 0))
    grid_spec = pltpu.PrefetchScalarGridSpec(
        num_scalar_prefetch=4,
        grid=(T // tm,),
        in_specs=[tile(D_MODEL), tile(LANES), pl.BlockSpec((1, 8, LANES), lambda i, *_: (i, 0, 0))],
        out_specs=pl.BlockSpec(memory_space=pl.ANY),
        scratch_shapes=[pltpu.VMEM((2, n_loc, D_MODEL), BF16), pltpu.VMEM((MOE_RB // 2, D_MODEL), BF16),
                        pltpu.SemaphoreType.DMA((2,))],
    )
    return pl.pallas_call(
        _scatter_body,
        grid_spec=grid_spec,
        out_shape=jax.ShapeDtypeStruct((n_rows, D_MODEL), BF16),
        compiler_params=_cparams(1),
        name="moe_scatter",
    )(*tables, h2d, route2d, loff_f)


def _expert_body(blk0_ref, nblk_ref, n_act_ref, xs_ref, wg_ref, wu_ref, wd_ref, ys_ref,
                 wg_b, wu_b, wd_b, xbuf, ybuf, sem_in, sem_out):
    e = pl.program_id(0)
    n_slots, rb = xbuf.shape[0], xbuf.shape[1]
    ahead = n_slots - 2
    n_act = n_act_ref[0]
    b0 = blk0_ref[e]

    def rows(g):
        return pl.ds(pl.multiple_of(g * rb, rb), rb)

    def x_copy(g, slot):
        return _row_copy(xs_ref.at[rows(g)], xbuf.at[slot], sem_in.at[slot])

    def y_copy(g, slot):
        return _row_copy(ybuf.at[slot], ys_ref.at[rows(g)], sem_out.at[slot])

    @pl.when(e == 0)
    def _():
        for k in range(ahead):
            pl.when(k < n_act)(lambda k=k: x_copy(k, k).start())

    wg_b[...] = wg_ref[0].astype(BF16)
    wu_b[...] = wu_ref[0].astype(BF16)
    wd_b[...] = wd_ref[0].astype(BF16)

    def blocks(g0, count):
        gs = [g0 + c for c in range(count)]
        slots = [g & (n_slots - 1) for g in gs]
        for c in range(count):
            x_copy(gs[c], slots[c]).wait()
        for c in range(count):
            nxt = gs[c] + ahead
            pl.when(nxt < n_act)(lambda nxt=nxt: x_copy(nxt, nxt & (n_slots - 1)).start())
        x = [xbuf[slots[c]] for c in range(count)]
        a = [_dot(x[c], wg_b[...]) for c in range(count)]
        b = [_dot(x[c], wu_b[...]) for c in range(count)]
        h = [(_silu(a[c]) * b[c]).astype(BF16) for c in range(count)]
        y = [_dot(h[c], wd_b[...]).astype(BF16) for c in range(count)]
        for c in range(count):
            pl.when(gs[c] >= n_slots)(lambda c=c: y_copy(gs[c] - n_slots, slots[c]).wait())
        for c in range(count):
            ybuf[slots[c]] = y[c]
            y_copy(gs[c], slots[c]).start()

    nb = nblk_ref[e]

    def pair(j, carry):
        blocks(b0 + 2 * j, 2)
        return carry

    lax.fori_loop(0, lax.shift_right_logical(nb, 1), pair, 0)
    pl.when((nb & 1) == 1)(lambda: blocks(b0 + nb - 1, 1))

    @pl.when(e == pl.num_programs(0) - 1)
    def _():
        for k in range(1, n_slots + 1):
            pl.when(n_act >= k)(lambda k=k: y_copy(n_act - k, (n_act - k) & (n_slots - 1)).wait())
        ybuf[0] = jnp.zeros(ybuf.shape[1:], ybuf.dtype)
        n_blocks = ys_ref.shape[0] // rb

        def fill(start):
            def per_block(g, carry):
                c = y_copy(g, 0)
                c.start() if start else c.wait()
                return carry

            lax.fori_loop(n_act, n_blocks, per_block, 0)

        fill(True)
        fill(False)


def _experts(blk0, nblk, n_act, xs, n_rows, wg, wu, wd):
    rb = MOE_RB
    weight = lambda a: pl.BlockSpec((1,) + a.shape[1:], lambda e, *_: (e, 0, 0))
    grid_spec = pltpu.PrefetchScalarGridSpec(
        num_scalar_prefetch=3,
        grid=(N_EXPERTS,),
        in_specs=[pl.BlockSpec(memory_space=pl.ANY), weight(wg), weight(wu), weight(wd)],
        out_specs=pl.BlockSpec(memory_space=pl.ANY),
        scratch_shapes=[pltpu.VMEM((D_MODEL, EXPERT_FF), BF16), pltpu.VMEM((D_MODEL, EXPERT_FF), BF16),
                        pltpu.VMEM((EXPERT_FF, D_MODEL), BF16), pltpu.VMEM((MOE_SLOTS, rb, D_MODEL), BF16),
                        pltpu.VMEM((MOE_SLOTS, rb, D_MODEL), BF16), pltpu.SemaphoreType.DMA((MOE_SLOTS,)),
                        pltpu.SemaphoreType.DMA((MOE_SLOTS,))],
    )
    return pl.pallas_call(
        _expert_body,
        grid_spec=grid_spec,
        out_shape=jax.ShapeDtypeStruct((n_rows, D_MODEL), BF16),
        compiler_params=_cparams(1),
        name="moe_experts",
    )(blk0, nblk, n_act, xs, wg, wu, wd)


def _combine_body(gmap_ref, x_ref, route_ref, lofff_ref, ys_ref, o_ref, yloc, sems):
    i = pl.program_id(0)
    tm = x_ref.shape[0]
    n_loc = yloc.shape[1]
    slot = i & 1

    def fetch(tile, s):
        _move_groups(tile, gmap_ref, n_loc, lambda loc, glob: _row_copy(ys_ref.at[glob], yloc.at[s, loc], sems.at[s]))

    pl.when(i == 0)(lambda: fetch(i, slot))
    _row_copy(ys_ref.at[pl.ds(0, n_loc)], yloc.at[slot], sems.at[slot]).wait()
    pl.when(i + 1 < pl.num_programs(0))(lambda: fetch(i + 1, 1 - slot))

    route = route_ref[...]
    pos = _local_positions(route, lofff_ref[0, 0:1, :])
    col = lax.broadcasted_iota(jnp.int32, (tm, n_loc), 1).astype(F32)
    perm_w = jnp.where(col == pos[0], route[:, 2:3], jnp.where(col == pos[1], route[:, 3:4], 0.0)).astype(BF16)
    o_ref[...] = x_ref[...] + _dot(perm_w, yloc[slot])


def _combine(tables, x2d, route2d, loff_f, ys):
    T = x2d.shape[0]
    tm = MOE_TM
    n_loc = MOE_NLOC
    tile = lambda w: pl.BlockSpec((tm, w), lambda i, *_: (i, 0))
    grid_spec = pltpu.PrefetchScalarGridSpec(
        num_scalar_prefetch=1,
        grid=(T // tm,),
        in_specs=[tile(D_MODEL), tile(LANES), pl.BlockSpec((1, 8, LANES), lambda i, *_: (i, 0, 0)),
                  pl.BlockSpec(memory_space=pl.ANY)],
        out_specs=tile(D_MODEL),
        scratch_shapes=[pltpu.VMEM((2, n_loc, D_MODEL), BF16), pltpu.SemaphoreType.DMA((2,))],
    )
    return pl.pallas_call(
        _combine_body,
        grid_spec=grid_spec,
        out_shape=jax.ShapeDtypeStruct((T, D_MODEL), F32),
        compiler_params=_cparams(1),
        name="moe_combine",
    )(*tables, x2d, route2d, loff_f, ys)


def _compress_weights(pos, w2):
    eye = jnp.eye(NSA_GROUPS, dtype=F32)
    w2b = jnp.einsum('hd,gk->ghkd', w2, eye).reshape(NSA_GROUPS * CMP_HIDDEN, LANES)
    posb = jnp.tile(pos, (1, NSA_GROUPS)).reshape(2, 1, CMP_STRIDE * LANES)
    return posb, w2b.astype(BF16)


def _dup2(g):
    return jnp.tile(g.reshape(1, HEAD_DIM), (1, 2))


def kernel(x, mem, mix_norm, w_in, nsa_q_norm, nsa_kcmp_norm, nsa_ksel_norm, nsa_kwin_norm, cmp_pos_k, cmp_pos_v, cmp_k_w1, cmp_k_w2, cmp_v_w1, cmp_v_w2, nsa_out_norm, ret_out_norm, w_out, mem_x_norm, mem_kv_norm, mem_wq, mem_wkv, mem_q_norm, mem_k_norm, mem_wo, ffn_norm, router_group_w, router_group_b, router_expert_w, router_expert_b, exp_w_gate, exp_w_up, exp_w_down):
    B, S, D = x.shape
    T = B * S
    depth = mix_norm.shape[0]
    for l in range(depth):
        proj = _proj(x.reshape(T, D), mix_norm[l].reshape(1, D), w_in[l].T).reshape(B, S, PROJ_PAD)
        pk, w2k = _compress_weights(cmp_pos_k[l], cmp_k_w2[l])
        pv, w2v = _compress_weights(cmp_pos_v[l], cmp_v_w2[l])
        gains = jnp.stack([_dup2(nsa_kcmp_norm[l]), _dup2(nsa_ksel_norm[l]), _dup2(nsa_kwin_norm[l])])
        kcmp, vcmp, ks, vs, kw, vw = _nsa_prep(proj, jnp.stack([pk, pv]), cmp_k_w1[l], cmp_v_w1[l],
                                               jnp.stack([w2k, w2v]), gains)
        o_a = _nsa_attn(proj, kcmp, vcmp, ks, vs, kw, vw,
                        jnp.tile(nsa_q_norm[l].reshape(1, HEAD_DIM), (1, NSA_HEADS)), gains,
                        nsa_out_norm[l].reshape(1, NSA_WIDTH))
        o_b = _retention(proj, ret_out_norm[l].reshape(1, RET_WIDTH))
        mk, mv = _mem_prep(mem, mem_kv_norm[l].reshape(1, D), mem_wkv[l].astype(BF16),
                           jnp.tile(mem_k_norm[l].reshape(1, HEAD_DIM), (1, MEM_HEADS)))
        w_r = jnp.concatenate([router_group_w[l],
                               router_expert_w[l].transpose(1, 0, 2).reshape(D, N_EXPERTS),
                               jnp.zeros((D, LANES - N_GROUPS - N_EXPERTS), F32)], axis=1).astype(BF16)
        b_r = jnp.concatenate([router_group_b[l], router_expert_b[l].reshape(N_EXPERTS),
                               jnp.zeros((LANES - N_GROUPS - N_EXPERTS,), F32)]).reshape(1, LANES)
        x2, hf, route, counts = _post(
            x, o_a, o_b, w_out[l].astype(BF16), mk, mv, mem_x_norm[l].reshape(1, D), mem_wq[l].astype(BF16),
            jnp.tile(mem_q_norm[l].reshape(1, HEAD_DIM), (1, MEM_HEADS)), mem_wo[l].astype(BF16),
            ffn_norm[l].reshape(1, D), w_r, b_r)
        route2d = route.reshape(T, LANES)
        n_tiles = T // MOE_TM
        cnt = counts[:, 0, :N_EXPERTS].astype(jnp.int32)
        cnt = (cnt + RUN_ALIGN - 1) // RUN_ALIGN * RUN_ALIGN
        loff = jnp.cumsum(cnt, axis=1) - cnt
        total = jnp.sum(cnt, axis=0)
        padded = (total + MOE_RB - 1) // MOE_RB * MOE_RB
        pend = jnp.cumsum(padded)
        pstart = pend - padded
        goff = pstart[None, :] + jnp.cumsum(cnt, axis=0) - cnt
        n_rows = 2 * T + n_tiles * N_EXPERTS * RUN_ALIGN + N_EXPERTS * MOE_RB
        n_act = (pend[-1:] // MOE_RB).astype(jnp.int32)
        loff_f = jnp.broadcast_to(jnp.pad(loff.astype(F32), ((0, 0), (0, LANES - N_EXPERTS)))[:, None, :],
                                  (n_tiles, 8, LANES))
        grp_row = jnp.arange(MOE_NLOC // RUN_ALIGN, dtype=jnp.int32) * RUN_ALIGN
        inside = ((loff[:, None, :] <= grp_row[None, :, None])
                  & (grp_row[None, :, None] < (loff + cnt)[:, None, :])).astype(jnp.int32)
        shift = jnp.sum(inside * (goff - loff)[:, None, :], axis=2)
        used = jnp.sum(inside, axis=2) > 0
        gmap_scatter = jnp.where(used, shift + grp_row[None, :], n_rows + grp_row[None, :]).reshape(-1)
        gmap_gather = jnp.where(used, shift + grp_row[None, :], 0).reshape(-1)
        xs = _scatter_rows((gmap_scatter, pstart + total, padded - total, n_act), hf.reshape(T, D), route2d, loff_f,
                           n_rows + MOE_NLOC)
        ys = _experts(pstart // MOE_RB, padded // MOE_RB, n_act, xs, n_rows,
                      exp_w_gate[l], exp_w_up[l], exp_w_down[l])
        x = _combine((gmap_gather,), x2.reshape(T, D), route2d, loff_f, ys).reshape(B, S, D)
    return x
```

```python
import functools

import numpy as np
import jax
import jax.numpy as jnp
from jax import lax
from jax.experimental import pallas as pl
from jax.experimental.pallas import tpu as pltpu

F32 = jnp.float32
BF16 = jnp.bfloat16

D_MODEL = 1024
HEAD_DIM = 64
LANES = 128
NSA_HEADS = 8
NSA_GROUPS = 2
NSA_WIDTH = NSA_HEADS * HEAD_DIM
CMP_BLOCK = 32
CMP_STRIDE = 16
CMP_HIDDEN = 2 * HEAD_DIM
SEL_BLOCK = 64
SEL_TOPK = 8
WINDOW = 512
RET_HEADS = 8
RET_WIDTH = RET_HEADS * HEAD_DIM
RET_CHUNK = 128
ROPE_BASE = 10000.0
MEM_HEADS = 4
MEM_WIDTH = MEM_HEADS * HEAD_DIM
N_GROUPS = 4
EXPERTS_PER_GROUP = 8
N_EXPERTS = N_GROUPS * EXPERTS_PER_GROUP
EXPERT_FF = D_MODEL // 4
EPS = 1e-6
NEG_INF = -1e30
FORCE_SCORE = 1e9
BELOW_ALL = -3e38
MAX_FIXED_SHIFT = 40.0

COL_QA = 0
COL_QR, COL_KR, COL_VR, COL_GR = 512, 1024, 1536, 2048
COL_KVC, COL_KSV, COL_KWV = 2560, 2816, 3072
COL_GATE = 3328
PROJ_PAD = 3456
PROJ_SPLITS = (NSA_WIDTH, NSA_WIDTH + 6 * NSA_GROUPS * HEAD_DIM, NSA_WIDTH + 6 * NSA_GROUPS * HEAD_DIM + 3 * NSA_HEADS)

PROJ_TM = 512
RET_UNROLL = 4
NSA_TQ = 256
SEL_KC = 512
POST_CHAINS = 2
MOE_TM = 512
MOE_RB = 256
MOE_SLOTS = 8
RUN_ALIGN = 16
MOE_NLOC = 2 * MOE_TM + N_EXPERTS * RUN_ALIGN
VMEM_LIMIT = 56 * 1024 * 1024


def _cparams(n_axes):
    return pltpu.CompilerParams(dimension_semantics=("arbitrary",) * n_axes,
                                vmem_limit_bytes=VMEM_LIMIT)


def _dot(a, b):
    return jnp.dot(a, b, preferred_element_type=F32)


def _dot_nt(a, b):
    return lax.dot_general(a, b, (((1,), (1,)), ((), ())), preferred_element_type=F32)


def _dot_tn(a, b):
    return lax.dot_general(a, b, (((0,), (0,)), ((), ())), preferred_element_type=F32)


def _rms_full(x, g):
    ms = jnp.mean(x * x, axis=-1, keepdims=True)
    return x * lax.rsqrt(ms + EPS) * g


def _seg_mean(x, avg):
    return _dot(x.astype(BF16), avg)


def _silu(x):
    return x * (1.0 / (1.0 + jnp.exp(-x)))


def _sigmoid(x):
    return 1.0 / (1.0 + jnp.exp(-x))


def _block_avg(width):
    i = np.arange(width)
    return ((i[:, None] // HEAD_DIM == i[None, :] // HEAD_DIM) / HEAD_DIM).astype(np.float32)


def _proj_body(x_ref, g_ref, wt_ref, o_ref, w_scr):
    @pl.when(pl.program_id(0) == 0)
    def _():
        kv0, gate0, ret0 = PROJ_SPLITS
        w_scr[COL_QA:COL_QR] = wt_ref[0:kv0].astype(BF16)
        w_scr[COL_QR:COL_KVC] = wt_ref[ret0:].astype(BF16)
        w_scr[COL_KVC:COL_GATE] = wt_ref[kv0:gate0].astype(BF16)
        pad = jnp.zeros((PROJ_PAD - COL_GATE - (ret0 - gate0), wt_ref.shape[1]), F32)
        w_scr[COL_GATE:] = jnp.concatenate([wt_ref[gate0:ret0], pad], axis=0).astype(BF16)

    h = _rms_full(x_ref[...], g_ref[...]).astype(BF16)
    step = PROJ_PAD // 3
    for j in range(3):
        o_ref[:, j * step:(j + 1) * step] = _dot_nt(h, w_scr[j * step:(j + 1) * step]).astype(BF16)


def _proj(x2d, g, w_t):
    T = x2d.shape[0]
    assert w_t.shape[0] == PROJ_SPLITS[2] + 4 * RET_WIDTH
    return pl.pallas_call(
        _proj_body,
        grid=(T // PROJ_TM,),
        in_specs=[pl.BlockSpec((PROJ_TM, D_MODEL), lambda i: (i, 0)),
                  pl.BlockSpec((1, D_MODEL), lambda i: (0, 0)),
                  pl.BlockSpec(w_t.shape, lambda i: (0, 0))],
        out_specs=pl.BlockSpec((PROJ_TM, PROJ_PAD), lambda i: (i, 0)),
        out_shape=jax.ShapeDtypeStruct((T, PROJ_PAD), BF16),
        scratch_shapes=[pltpu.VMEM((PROJ_PAD, D_MODEL), BF16)],
        compiler_params=_cparams(1),
        name="proj",
    )(x2d, g, w_t)


def _dup_groups(x):
    lane = lax.broadcasted_iota(jnp.int32, x.shape, 1)
    xs = pltpu.roll(x, HEAD_DIM, axis=1)
    lo = lane < HEAD_DIM
    return jnp.where(lo, x, xs), jnp.where(lo, xs, x)


def _ones_groups(x):
    lane = lax.broadcasted_iota(jnp.int32, x.shape, 1)
    lo = lane < HEAD_DIM
    xs = pltpu.roll(x, HEAD_DIM, axis=1)
    return jnp.where(lo, x, 1.0), jnp.where(lo, 1.0, xs), jnp.where(lo, xs, 1.0), jnp.where(lo, 1.0, x)


def _nsa_prep_body(kvc_ref, ksv_ref, kwv_ref, pos_ref, w1k_ref, w1v_ref, w2_ref, gain_ref, avg_ref,
                   kcmp_ref, vcmp_ref, ks_ref, vs_ref, kw_ref, vw_ref, scr_k, scr_v, w1_ref):
    @pl.when(pl.program_id(0) == 0)
    def _():
        zero = jnp.zeros((HEAD_DIM, CMP_HIDDEN), BF16)
        for j, src in ((0, w1k_ref), (1, w1v_ref)):
            for l in range(CMP_BLOCK):
                piece = src[l * HEAD_DIM:(l + 1) * HEAD_DIM, :].astype(BF16)
                r0 = (l % CMP_STRIDE) * LANES
                w1_ref[j, l // CMP_STRIDE, r0:r0 + HEAD_DIM, :] = jnp.concatenate([piece, zero], axis=1)
                w1_ref[j, l // CMP_STRIDE, r0 + HEAD_DIM:r0 + LANES, :] = jnp.concatenate([zero, piece], axis=1)

    avg = avg_ref[...]
    n_c = scr_k.shape[0] // CMP_STRIDE
    scr_k[...] = kvc_ref[0, :, 0:LANES].astype(F32)
    scr_v[...] = kvc_ref[0, :, LANES:2 * LANES].astype(F32)
    for j, out_ref, scr in ((0, kcmp_ref, scr_k), (1, vcmp_ref, scr_v)):
        ycat = jnp.concatenate(
            [scr[pl.ds(l, n_c, stride=CMP_STRIDE), :] for l in range(CMP_STRIDE)], axis=1)
        first = _dot((ycat + pos_ref[j, 0]).astype(BF16), w1_ref[j, 0])
        second = _dot((ycat + pos_ref[j, 1]).astype(BF16), w1_ref[j, 1])
        hidden = first + pltpu.roll(second, n_c - 1, axis=0)
        cmp_tok = _dot(_silu(hidden).astype(BF16), w2_ref[j])
        if j == 0:
            ms = _seg_mean(cmp_tok * cmp_tok, avg)
            cmp_tok = cmp_tok * lax.rsqrt(ms + EPS) * gain_ref[0]
        for i, piece in enumerate(_dup_groups(cmp_tok) if j == 0 else _ones_groups(cmp_tok)):
            out_ref[0, i] = piece.astype(BF16)

    for src_ref, k_out, v_out, gi in ((ksv_ref, ks_ref, vs_ref, 1), (kwv_ref, kw_ref, vw_ref, 2)):
        k = src_ref[0, :, 0:LANES].astype(F32)
        ms = _seg_mean(k * k, avg)
        k = k * lax.rsqrt(ms + EPS) * gain_ref[gi]
        d0, d1 = _dup_groups(k)
        k_out[0, 0] = d0.astype(BF16)
        k_out[0, 1] = d1.astype(BF16)
        for i, piece in enumerate(_ones_groups(src_ref[0, :, LANES:2 * LANES].astype(F32))):
            v_out[0, i] = piece.astype(BF16)


def _nsa_prep(proj3, pos, w1k, w1v, w2, gains):
    B, S, _ = proj3.shape
    n_c = S // CMP_STRIDE
    avg = jnp.asarray(_block_avg(LANES), BF16)
    col = lambda c: pl.BlockSpec((1, S, 2 * LANES), lambda b: (b, 0, c // (2 * LANES)))
    full = lambda a: pl.BlockSpec(a.shape, lambda b: (0,) * a.ndim)
    cmp_spec = pl.BlockSpec((1, NSA_GROUPS, n_c, LANES), lambda b: (b, 0, 0, 0))
    seq_spec = pl.BlockSpec((1, NSA_GROUPS, S, LANES), lambda b: (b, 0, 0, 0))
    cmp_shape = jax.ShapeDtypeStruct((B, NSA_GROUPS, n_c, LANES), BF16)
    seq_shape = jax.ShapeDtypeStruct((B, NSA_GROUPS, S, LANES), BF16)

    def val(a):
        if isinstance(a, jax.ShapeDtypeStruct):
            return jax.ShapeDtypeStruct((a.shape[0], 2 * a.shape[1]) + a.shape[2:], a.dtype)
        return pl.BlockSpec((1, 2 * NSA_GROUPS) + a.block_shape[2:], lambda b: (b, 0, 0, 0))

    return pl.pallas_call(
        _nsa_prep_body,
        grid=(B,),
        in_specs=[col(COL_KVC), col(COL_KSV), col(COL_KWV), full(pos), full(w1k), full(w1v), full(w2), full(gains),
                  full(avg)],
        out_specs=[cmp_spec, val(cmp_spec), seq_spec, val(seq_spec), seq_spec, val(seq_spec)],
        out_shape=[cmp_shape, val(cmp_shape), seq_shape, val(seq_shape), seq_shape, val(seq_shape)],
        scratch_shapes=[pltpu.VMEM((S, LANES), F32), pltpu.VMEM((S, LANES), F32),
                        pltpu.VMEM((2, 2, CMP_STRIDE * LANES, NSA_GROUPS * CMP_HIDDEN), BF16)],
        compiler_params=_cparams(1),
        name="nsa_prep",
    )(proj3, proj3, proj3, pos, w1k, w1v, w2, gains, avg)


def _nsa_attn_body(q_ref, gate_ref, kcmp_ref, vcmp_ref, ks_ref, vs_ref, kw_ref, vw_ref,
                   qgain_ref, kgain_ref, ogain_ref, avgq_ref, avgo_ref, msct_ref, esel_ref, egate_ref, wbias_ref,
                   dbias_ref, cbias_ref,
                   o_ref, m_scr, acc_scr):
    tq = q_ref.shape[1]
    n_cmp = kcmp_ref.shape[2]
    n_sel = msct_ref.shape[0]
    kc_len = esel_ref.shape[2]
    rows = 4 * tq
    qi = pl.program_id(1)
    q0 = qi * tq

    q = q_ref[0].astype(F32)
    ms = _seg_mean(q * q, avgq_ref[...])
    qn = q * lax.rsqrt(ms + EPS) * qgain_ref[...] * (HEAD_DIM ** -0.5)

    gate_sig = _sigmoid(gate_ref[0].astype(F32)).astype(BF16)
    gates = [_dot(gate_sig, egate_ref[j]) for j in range(3)]

    lane_q = lax.broadcasted_iota(jnp.int32, (tq, LANES), 1)
    lo_q = lane_q < HEAD_DIM

    blk = lax.broadcasted_iota(jnp.int32, (n_sel, tq), 0)
    cur = lax.shift_right_logical(q0 + lax.broadcasted_iota(jnp.int32, (n_sel, tq), 1), int(np.log2(SEL_BLOCK)))
    forced = (blk == 0) | (blk == cur) | (blk == cur - 1)
    future = blk > cur
    blk_f = blk.astype(F32)

    def heads4(x):
        return jnp.concatenate([x] * 4, axis=0)

    def weighted_values(p, v_low, v_high):
        return jnp.concatenate([_dot(p[:2 * tq], v_low), _dot(p[2 * tq:], v_high)], axis=0)

    def normalised_pairs(acc, guard):
        out = []
        for p in range(2):
            low = acc[p * tq:(p + 1) * tq]
            high = acc[(2 + p) * tq:(3 + p) * tq]
            den = pltpu.roll(jnp.where(lo_q, high, low), HEAD_DIM, axis=1)
            if guard:
                den = jnp.maximum(den, 1e-30)
            out.append(jnp.where(lo_q, low, high) / den)
        return out

    groups = range(NSA_GROUPS)
    qs = []
    for g in groups:
        slabs = [qn[:, (2 * g + p) * LANES:(2 * g + p + 1) * LANES] for p in range(2)]
        qs.append(jnp.concatenate(
            [jnp.where(lo_q, slabs[0], 0.0), jnp.where(lo_q, slabs[1], 0.0),
             jnp.where(lo_q, 0.0, slabs[0]), jnp.where(lo_q, 0.0, slabs[1])], axis=0).astype(BF16))

    def compressed_and_select(shift):
        s_c = [_dot_nt(qs[g], kcmp_ref[0, g]) for g in groups]
        if shift is None:
            r_c = lax.broadcasted_iota(jnp.int32, (rows, n_cmp), 0)
            c_c = lax.broadcasted_iota(jnp.int32, (rows, n_cmp), 1)
            cmask = (c_c * CMP_STRIDE + (CMP_BLOCK - 1)) <= q0 + (r_c & (tq - 1))
            s_c = [jnp.where(cmask, s_c[g], NEG_INF) for g in groups]
            e_c = [jnp.where(cmask, jnp.exp(s_c[g] - jnp.max(s_c[g], axis=-1, keepdims=True)), 0.0) for g in groups]
        else:
            bias = heads4(cbias_ref[qi] - shift)
            e_c = [jnp.exp(s_c[g] + bias) for g in groups]
        e_b = [e_c[g].astype(BF16) for g in groups]
        acc_c = [weighted_values(e_b[g], vcmp_ref[0, 2 * g], vcmp_ref[0, 2 * g + 1]) for g in groups]
        ones = jnp.ones((8, n_cmp), BF16)
        imp = []
        for g in groups:
            num = [_dot_nt(msct_ref[...], e_b[g][h * tq:(h + 1) * tq]) for h in range(4)]
            den = [_dot_nt(ones, e_b[g][h * tq:(h + 1) * tq])[0:1] for h in range(4)]
            parts = [num[h] / jnp.maximum(den[h], 1e-30) for h in range(4)]
            imp.append((parts[0] + parts[1]) + (parts[2] + parts[3]))
        v = [jnp.where(forced, FORCE_SCORE, jnp.where(future, NEG_INF, imp[g])) for g in groups]
        sel = [jnp.zeros((n_sel, tq), F32) for g in groups]
        for _ in range(SEL_TOPK):
            mx = [jnp.max(v[g], axis=0, keepdims=True) for g in groups]
            first = [jnp.min(jnp.where(v[g] == mx[g], blk_f, float(LANES)), axis=0, keepdims=True) for g in groups]
            pick = [blk_f == first[g] for g in groups]
            sel = [jnp.where(pick[g], 1.0, sel[g]) for g in groups]
            v = [jnp.where(pick[g], BELOW_ALL, v[g]) for g in groups]
        return [normalised_pairs(acc_c[g], True) for g in groups], [sel[g].astype(BF16) for g in groups]

    n_before = lax.shift_right_logical(q0, int(np.log2(kc_len)))
    causal = dbias_ref[qi & (kc_len // tq - 1)]
    w0 = pl.multiple_of(jnp.maximum(q0 - WINDOW, 0), tq)
    n_w = WINDOW + tq
    w_case = jnp.minimum(qi, WINDOW // tq)

    def sel_keys(ref, g, kc):
        return ref[0, g, pl.ds(pl.multiple_of(kc * kc_len, kc_len), kc_len), :]

    def sel_scores(sel_b, g, kc, causal_bias, shift):
        chosen = _dot_tn(sel_b[g], esel_ref[kc])
        bias = (chosen - 1.0) * (-NEG_INF)
        if causal_bias is not None:
            bias = bias + causal_bias
        if shift is not None:
            bias = bias - shift
        return _dot_nt(qs[g], sel_keys(ks_ref, g, kc)) + heads4(bias)

    def win_scores(g, shift):
        bias = wbias_ref[w_case] if shift is None else wbias_ref[w_case] - shift
        return _dot_nt(qs[g], kw_ref[0, g, pl.ds(w0, n_w), :]) + heads4(bias)

    def win_values(g):
        return vw_ref[0, 2 * g, pl.ds(w0, n_w), :], vw_ref[0, 2 * g + 1, pl.ds(w0, n_w), :]

    def sel_values(g, kc):
        return sel_keys(vs_ref, 2 * g, kc), sel_keys(vs_ref, 2 * g + 1, kc)

    def finish(cmp_s, acc_w):
        for g in groups:
            sel_s = normalised_pairs(acc_scr[g], False)
            win_s = normalised_pairs(acc_w[g], False)
            for p in range(2):
                cols = slice((2 * g + p) * LANES, (2 * g + p + 1) * LANES)
                mix = gates[0][:, cols] * cmp_s[g][p] + gates[1][:, cols] * sel_s[p] + gates[2][:, cols] * win_s[p]
                ms_o = _seg_mean(mix * mix, avgo_ref[...])
                o_ref[0, :, cols] = (mix * lax.rsqrt(ms_o + EPS) * ogain_ref[:, cols]).astype(BF16)

    def fixed_shift_path(shift):
        cmp_s, sel_b = compressed_and_select(shift)

        def probs(s):
            return jnp.exp(s).astype(BF16)

        for g in groups:
            acc_scr[g] = jnp.zeros(acc_scr.shape[1:], F32)

        def before(kc, carry):
            s = [sel_scores(sel_b, g, kc, None, shift) for g in groups]
            p = [probs(s[g]) for g in groups]
            for g in groups:
                acc_scr[g] = acc_scr[g] + weighted_values(p[g], *sel_values(g, kc))
            return carry

        lax.fori_loop(0, n_before, before, 0)
        s_d0 = sel_scores(sel_b, 0, n_before, causal, shift)
        s_d1 = sel_scores(sel_b, 1, n_before, causal, shift)
        p_d0 = probs(s_d0)
        s_w0 = win_scores(0, shift)
        acc_scr[0] = acc_scr[0] + weighted_values(p_d0, *sel_values(0, n_before))
        p_d1 = probs(s_d1)
        s_w1 = win_scores(1, shift)
        acc_scr[1] = acc_scr[1] + weighted_values(p_d1, *sel_values(1, n_before))
        p_w0 = probs(s_w0)
        acc_w0 = weighted_values(p_w0, *win_values(0))
        p_w1 = probs(s_w1)
        acc_w1 = weighted_values(p_w1, *win_values(1))
        finish(cmp_s, [acc_w0, acc_w1])

    def online_path():
        cmp_s, sel_b = compressed_and_select(None)
        for g in groups:
            m_scr[g] = jnp.full(m_scr.shape[1:], NEG_INF, F32)
            acc_scr[g] = jnp.zeros(acc_scr.shape[1:], F32)

        def sel_softmax(g, s):
            m_old = m_scr[g]
            m_new = jnp.maximum(m_old, jnp.max(s, axis=-1, keepdims=True))
            m_scr[g] = m_new
            return jnp.exp(s - m_new).astype(BF16), jnp.exp(m_old - m_new)

        def sel_accumulate(g, kc, p, alpha):
            acc_scr[g] = alpha * acc_scr[g] + weighted_values(p, *sel_values(g, kc))

        def win_softmax(s):
            return jnp.exp(s - jnp.max(s, axis=-1, keepdims=True)).astype(BF16)

        def before(kc, carry):
            s = [sel_scores(sel_b, g, kc, None, None) for g in groups]
            pa = [sel_softmax(g, s[g]) for g in groups]
            for g in groups:
                sel_accumulate(g, kc, *pa[g])
            return carry

        lax.fori_loop(0, n_before, before, 0)
        s_d0 = sel_scores(sel_b, 0, n_before, causal, None)
        s_d1 = sel_scores(sel_b, 1, n_before, causal, None)
        pa0 = sel_softmax(0, s_d0)
        s_w0 = win_scores(0, None)
        sel_accumulate(0, n_before, *pa0)
        pa1 = sel_softmax(1, s_d1)
        s_w1 = win_scores(1, None)
        sel_accumulate(1, n_before, *pa1)
        acc_w0 = weighted_values(win_softmax(s_w0), *win_values(0))
        acc_w1 = weighted_values(win_softmax(s_w1), *win_values(1))
        finish(cmp_s, [acc_w0, acc_w1])

    bound = 1.01 * (HEAD_DIM ** 0.5) * jnp.max(jnp.abs(qgain_ref[...])) * jnp.max(jnp.abs(kgain_ref[...]))
    safe = bound <= MAX_FIXED_SHIFT
    pl.when(safe)(lambda: fixed_shift_path(bound))
    pl.when(jnp.logical_not(safe))(online_path)


def _sel_from_cmp(n_cmp, n_sel):
    c0 = np.arange(n_cmp) * CMP_STRIDE
    s0 = np.arange(n_sel) * SEL_BLOCK
    ov = np.minimum(c0[None, :] + CMP_BLOCK, s0[:, None] + SEL_BLOCK) - np.maximum(c0[None, :], s0[:, None])
    m = (np.clip(ov, 0, None) / CMP_BLOCK).astype(np.float32)
    m[:, (np.arange(n_cmp) * CMP_STRIDE + CMP_BLOCK) > n_sel * SEL_BLOCK] = 0.0
    return m


def _nsa_attn(proj3, kcmp, vcmp, ks, vs, kw, vw, q_gain, k_gains, o_gain):
    B, S, _ = proj3.shape
    n_cmp = kcmp.shape[2]
    n_sel = S // SEL_BLOCK
    tq = NSA_TQ
    assert n_sel % 8 == 0 and S % SEL_KC == 0 and SEL_KC % tq == 0 and WINDOW % tq == 0 and S >= WINDOW + tq
    avgq = jnp.asarray(_block_avg(NSA_WIDTH), BF16)
    avgo = jnp.asarray(_block_avg(LANES), BF16)
    msct = jnp.asarray(_sel_from_cmp(n_cmp, n_sel), BF16)
    esel = (np.arange(n_sel)[:, None] == np.arange(S)[None, :] // SEL_BLOCK).astype(np.float32)
    esel = jnp.asarray(esel.reshape(n_sel, S // SEL_KC, SEL_KC).transpose(1, 0, 2), BF16)
    src = np.arange(LANES)[:, None]
    dst = np.arange(NSA_WIDTH)[None, :]
    egate = jnp.asarray(np.stack([(src == (dst // HEAD_DIM) * 3 + j) for j in range(3)]).astype(np.float32), BF16)
    r = np.arange(tq)[:, None]
    n_w = WINDOW + tq
    wcases = []
    for i in range(WINDOW // tq + 1):
        diff = (i * tq - max(i * tq - WINDOW, 0)) + r - np.arange(n_w)[None, :]
        wcases.append(np.where((diff >= 0) & (diff < WINDOW), 0.0, NEG_INF))
    wbias = jnp.asarray(np.stack(wcases), F32)
    dbias = jnp.asarray(np.stack([np.where(np.arange(SEL_KC)[None, :] <= i * tq + r, 0.0, NEG_INF)
                                  for i in range(SEL_KC // tq)]), F32)
    c_end = np.arange(n_cmp)[None, :] * CMP_STRIDE + (CMP_BLOCK - 1)
    cbias = jnp.asarray(np.stack([np.where(c_end <= i * tq + r, 0.0, NEG_INF) for i in range(S // tq)]), F32)

    full = lambda a: pl.BlockSpec(a.shape, lambda b, i: (0,) * a.ndim)
    per_b = lambda a: pl.BlockSpec((1,) + a.shape[1:], lambda b, i: (b,) + (0,) * (a.ndim - 1))
    return pl.pallas_call(
        _nsa_attn_body,
        grid=(B, S // tq),
        in_specs=[pl.BlockSpec((1, tq, NSA_WIDTH), lambda b, i: (b, i, COL_QA // NSA_WIDTH)),
                  pl.BlockSpec((1, tq, LANES), lambda b, i: (b, i, COL_GATE // LANES)),
                  per_b(kcmp), per_b(vcmp), per_b(ks), per_b(vs), per_b(kw), per_b(vw),
                  full(q_gain), full(k_gains), full(o_gain), full(avgq), full(avgo), full(msct), full(esel), full(egate),
                  full(wbias), full(dbias), full(cbias)],
        out_specs=pl.BlockSpec((1, tq, NSA_WIDTH), lambda b, i: (b, i, 0)),
        out_shape=jax.ShapeDtypeStruct((B, S, NSA_WIDTH), BF16),
        scratch_shapes=[pltpu.VMEM((NSA_GROUPS, 4 * tq, 1), F32), pltpu.VMEM((NSA_GROUPS, 4 * tq, LANES), F32)],
        compiler_params=_cparams(2),
        name="nsa_attn",
    )(proj3, proj3, kcmp, vcmp, ks, vs, kw, vw, q_gain, k_gains, o_gain, avgq, avgo, msct, esel, egate, wbias, dbias,
      cbias)


def _retention_body(q_ref, k_ref, v_ref, g_ref, cos_ref, sin_ref, decay_ref, xi_ref, zeta_ref, gammac_ref,
                    gain_ref, avg_ref, o_ref, state_scr):
    S = q_ref.shape[1]
    C = RET_CHUNK
    lane = lax.broadcasted_iota(jnp.int32, (C, LANES), 1)
    lo = lane < HEAD_DIM
    first_half = (lane & (HEAD_DIM - 1)) < HEAD_DIM // 2
    r = lax.broadcasted_iota(jnp.int32, (LANES, LANES), 0)
    c = lax.broadcasted_iota(jnp.int32, (LANES, LANES), 1)
    same_head = (r < HEAD_DIM) == (c < HEAD_DIM)
    avg = avg_ref[...]
    state_scr[...] = jnp.zeros(state_scr.shape, F32)

    def rope(x, cos, sin):
        swapped = jnp.where(first_half, pltpu.roll(x, LANES - HEAD_DIM // 2, axis=1),
                            pltpu.roll(x, HEAD_DIM // 2, axis=1))
        return x * cos + swapped * sin

    n_pairs = RET_HEADS // 2
    cols = [slice(p * LANES, (p + 1) * LANES) for p in range(n_pairs)]
    units = [(u, p) for u in range(RET_UNROLL) for p in range(n_pairs)]

    def chunks(n, carry):
        r0 = [pl.multiple_of((n * RET_UNROLL + u) * C, C) for u in range(RET_UNROLL)]
        cos = [cos_ref[pl.ds(r0[u], C), :] for u in range(RET_UNROLL)]
        sin = [sin_ref[pl.ds(r0[u], C), :] for u in range(RET_UNROLL)]
        q = [rope(q_ref[0, pl.ds(r0[u], C), cols[p]].astype(F32), cos[u], sin[u]) for u, p in units]
        k = [rope(k_ref[0, pl.ds(r0[u], C), cols[p]].astype(F32), cos[u], sin[u]) * (HEAD_DIM ** -0.5)
             for u, p in units]
        vb = [v_ref[0, pl.ds(r0[u], C), cols[p]] for u, p in units]
        kb = [k[i].astype(BF16) for i in range(len(units))]
        inner = [_dot_nt(jnp.where(lo if half == 0 else ~lo, q[i], 0.0).astype(BF16), kb[i])
                 * decay_ref[2 * units[i][1] + half] for i in range(len(units)) for half in range(2)]
        upd = [_dot_tn((k[i] * zeta_ref[units[i][1]]).astype(BF16), vb[i]) for i in range(len(units))]
        state = [state_scr[p] for p in range(n_pairs)]
        for u in range(RET_UNROLL):
            for p in range(n_pairs):
                prev = state[u * n_pairs + p]
                state.append(gammac_ref[p] * prev + jnp.where(same_head, upd[u * n_pairs + p], 0.0))
        cross = [_dot(q[i].astype(BF16), state[i].astype(BF16)) * xi_ref[units[i][1]] for i in range(len(units))]
        outs = [_dot(inner[j].astype(BF16), vb[j // 2]) for j in range(2 * len(units))]
        for p in range(n_pairs):
            state_scr[p] = state[RET_UNROLL * n_pairs + p]
        y = jnp.concatenate([jnp.where(lo, outs[2 * i], outs[2 * i + 1]) + cross[i] for i in range(len(units))],
                            axis=0)
        mu = _seg_mean(y, avg)
        d = y - mu
        var = _seg_mean(d * d, avg)
        yn = d * lax.rsqrt(var + EPS)
        for i, (u, p) in enumerate(units):
            gate = g_ref[0, pl.ds(r0[u], C), cols[p]].astype(F32)
            o_ref[0, pl.ds(r0[u], C), cols[p]] = (_silu(gate) * (yn[i * C:(i + 1) * C] * gain_ref[:, cols[p]])).astype(BF16)
        return carry

    lax.fori_loop(0, S // (C * RET_UNROLL), chunks, 0)


def _retention_tables(S):
    half = HEAD_DIM // 2
    inv_freq = ROPE_BASE ** (-jnp.arange(half, dtype=F32) / half)
    ang = jnp.arange(S, dtype=F32)[:, None] * inv_freq[None, :]
    cos, sin = jnp.cos(ang), jnp.sin(ang)
    cos_t = jnp.tile(cos, (1, 4))
    sin_t = jnp.tile(jnp.concatenate([-sin, sin], axis=1), (1, 2))
    C = RET_CHUNK
    H = RET_HEADS
    log_gamma = jnp.log1p(-jnp.power(2.0, -5.0 - jnp.arange(H, dtype=F32)))
    i = jnp.arange(C, dtype=F32)
    rel = i[:, None] - i[None, :]
    decay = jnp.where(rel >= 0, jnp.exp(jnp.maximum(rel, 0.0)[None] * log_gamma[:, None, None]), 0.0)
    xi = jnp.exp((i + 1.0)[:, None] * log_gamma[None, :])
    zeta = jnp.exp((C - 1.0 - i)[:, None] * log_gamma[None, :])
    gamma_c = jnp.exp(C * log_gamma)
    per_pair = lambda t: jnp.repeat(t.T.reshape(H // 2, 2, -1), HEAD_DIM, axis=1).transpose(0, 2, 1)
    gammac = jnp.repeat(gamma_c.reshape(H // 2, 2), HEAD_DIM, axis=1)[:, None, :]
    return cos_t, sin_t, decay, per_pair(xi), per_pair(zeta), gammac


def _retention(proj3, gain):
    B, S, _ = proj3.shape
    cos_t, sin_t, decay, xi, zeta, gammac = _retention_tables(S)
    avg = jnp.asarray(_block_avg(LANES), BF16)
    col = lambda c: pl.BlockSpec((1, S, RET_WIDTH), lambda b: (b, 0, c // RET_WIDTH))
    full = lambda a: pl.BlockSpec(a.shape, lambda b: (0,) * a.ndim)
    return pl.pallas_call(
        _retention_body,
        grid=(B,),
        in_specs=[col(COL_QR), col(COL_KR), col(COL_VR), col(COL_GR), full(cos_t), full(sin_t), full(decay),
                  full(xi), full(zeta), full(gammac), full(gain), full(avg)],
        out_specs=pl.BlockSpec((1, S, RET_WIDTH), lambda b: (b, 0, 0)),
        out_shape=jax.ShapeDtypeStruct((B, S, RET_WIDTH), BF16),
        scratch_shapes=[pltpu.VMEM((RET_HEADS // 2, LANES, LANES), F32)],
        compiler_params=_cparams(1),
        name="retention",
    )(proj3, proj3, proj3, proj3, cos_t, sin_t, decay, xi, zeta, gammac, gain, avg)


def _mem_prep_body(mem_ref, g_ref, wkv_ref, kgain_ref, avg_ref, k_ref, v_ref):
    hm = _rms_full(mem_ref[0], g_ref[...]).astype(BF16)
    kv = _dot(hm, wkv_ref[...])
    k = kv[:, :MEM_WIDTH]
    ms = _seg_mean(k * k, avg_ref[...])
    k_ref[0] = (k * lax.rsqrt(ms + EPS) * kgain_ref[...]).astype(BF16)
    v_ref[0] = kv[:, MEM_WIDTH:].astype(BF16)


def _mem_prep(mem, g, wkv, kgain):
    B, M, _ = mem.shape
    avg = jnp.asarray(_block_avg(MEM_WIDTH), BF16)
    full = lambda a: pl.BlockSpec(a.shape, lambda b: (0,) * a.ndim)
    out_spec = pl.BlockSpec((1, M, MEM_WIDTH), lambda b: (b, 0, 0))
    out_shape = jax.ShapeDtypeStruct((B, M, MEM_WIDTH), BF16)
    return pl.pallas_call(
        _mem_prep_body,
        grid=(B,),
        in_specs=[pl.BlockSpec((1, M, D_MODEL), lambda b: (b, 0, 0)), full(g), full(wkv), full(kgain), full(avg)],
        out_specs=[out_spec, out_spec],
        out_shape=[out_shape, out_shape],
        compiler_params=_cparams(1),
        name="mem_prep",
    )(mem, g, wkv, kgain, avg)


def _post_body(x_ref, oa_ref, ob_ref, wout_ref, mk_ref, mv_ref, gx_ref, wq_ref, qgain_ref, avg_ref, wo_ref,
               gf_ref, wr_ref, br_ref, tri_ref, x2_ref, h_ref, route_ref, count_ref):
    tm = x_ref.shape[1] // POST_CHAINS
    chains = range(POST_CHAINS)
    rows = [slice(c * tm, (c + 1) * tm) for c in chains]
    x1 = [x_ref[0, rows[c]] + _dot(oa_ref[0, rows[c]], wout_ref[0:NSA_WIDTH, :])
          + _dot(ob_ref[0, rows[c]], wout_ref[NSA_WIDTH:, :]) for c in chains]

    h = [_rms_full(x1[c], gx_ref[...]).astype(BF16) for c in chains]
    q = [_dot(h[c], wq_ref[...]) for c in chains]
    ms = [_seg_mean(q[c] * q[c], avg_ref[...]) for c in chains]
    q = [q[c] * lax.rsqrt(ms[c] + EPS) * qgain_ref[...] * (HEAD_DIM ** -0.5) for c in chains]
    lane = lax.broadcasted_iota(jnp.int32, (tm, LANES), 1)
    lo = lane < HEAD_DIM
    heads = [(c, p, half) for c in chains for p in range(MEM_HEADS // 2) for half in range(2)]
    s = [_dot_nt(jnp.where(lo if half == 0 else ~lo, q[c][:, p * LANES:(p + 1) * LANES], 0.0).astype(BF16),
                 mk_ref[0, :, p * LANES:(p + 1) * LANES]) for c, p, half in heads]
    e = [jnp.exp(s[i] - jnp.max(s[i], axis=-1, keepdims=True)) for i in range(len(heads))]
    pr = [(e[i] / jnp.sum(e[i], axis=-1, keepdims=True)).astype(BF16) for i in range(len(heads))]
    outs = [_dot(pr[i], mv_ref[0, :, heads[i][1] * LANES:(heads[i][1] + 1) * LANES]) for i in range(len(heads))]
    per_chain = MEM_HEADS
    o = [jnp.concatenate([jnp.where(lo, outs[c * per_chain + 2 * p], outs[c * per_chain + 2 * p + 1])
                          for p in range(MEM_HEADS // 2)], axis=1).astype(BF16) for c in chains]
    x2 = [x1[c] + _dot(o[c], wo_ref[...]) for c in chains]
    for c in chains:
        x2_ref[0, rows[c]] = x2[c]

    hf = [_rms_full(x2[c], gf_ref[...]).astype(BF16) for c in chains]
    for c in chains:
        h_ref[0, rows[c]] = hf[c]
    logits = [_dot(hf[c], wr_ref[...]) + br_ref[...] for c in chains]
    lane_f = lane.astype(F32)
    big = float(LANES)
    picks = []
    for c in chains:
        gl = jnp.where(lane < N_GROUPS, logits[c], BELOW_ALL)
        gmax = jnp.max(gl, axis=-1, keepdims=True)
        grp = jnp.min(jnp.where(gl == gmax, lane_f, big), axis=-1, keepdims=True)
        g_w = 1.0 / jnp.sum(jnp.where(lane < N_GROUPS, jnp.exp(gl - gmax), 0.0), axis=-1, keepdims=True)
        e_lo = N_GROUPS + grp * EXPERTS_PER_GROUP
        el = jnp.where((lane_f >= e_lo) & (lane_f < e_lo + EXPERTS_PER_GROUP), logits[c], BELOW_ALL)
        v0 = jnp.max(el, axis=-1, keepdims=True)
        i0 = jnp.min(jnp.where(el == v0, lane_f, big), axis=-1, keepdims=True)
        el = jnp.where(lane_f == i0, BELOW_ALL, el)
        v1 = jnp.max(el, axis=-1, keepdims=True)
        i1 = jnp.min(jnp.where(el == v1, lane_f, big), axis=-1, keepdims=True)
        e1 = jnp.exp(v1 - v0)
        picks.append((i0 - N_GROUPS, i1 - N_GROUPS, g_w / (1.0 + e1), g_w * e1 / (1.0 + e1)))

    hot = [[lane_f == picks[c][s] for s in range(2)] for c in chains]
    both = jnp.concatenate([jnp.where(hot[c][0], 1.0, 0.0) + jnp.where(hot[c][1], 1.0, 0.0) for c in chains], axis=0)
    before = _dot(tri_ref[...], both.astype(BF16))
    count_ref[0] = jnp.broadcast_to(jnp.sum(both, axis=0, keepdims=True), count_ref.shape[1:])
    for c in chains:
        e0, e1, w0, w1 = picks[c]
        r0 = jnp.sum(jnp.where(hot[c][0], before[rows[c]], 0.0), axis=-1, keepdims=True)
        r1 = jnp.sum(jnp.where(hot[c][1], before[rows[c]], 0.0), axis=-1, keepdims=True)
        cols = (e0, e1, w0, w1, r0, r1)
        route = jnp.zeros((tm, LANES), F32)
        for k in range(len(cols)):
            route = jnp.where(lane == k, cols[k], route)
        route_ref[0, rows[c]] = route


def _post(x, oa, ob, wout, mk, mv, gx, wq, qgain, wo, gf, wr, br):
    B, S, _ = x.shape
    tm = MOE_TM
    n_s = S // tm
    avg = jnp.asarray(_block_avg(MEM_WIDTH), BF16)
    tri = jnp.asarray(np.tril(np.ones((tm, tm), np.float32), -1), BF16)
    full = lambda a: pl.BlockSpec(a.shape, lambda b, i: (0,) * a.ndim)
    per_b = lambda a: pl.BlockSpec((1,) + a.shape[1:], lambda b, i: (b,) + (0,) * (a.ndim - 1))
    tile = lambda w: pl.BlockSpec((1, tm, w), lambda b, i: (b, i, 0))
    return pl.pallas_call(
        _post_body,
        grid=(B, n_s),
        in_specs=[tile(D_MODEL), tile(NSA_WIDTH), tile(RET_WIDTH), full(wout), per_b(mk), per_b(mv), full(gx),
                  full(wq), full(qgain), full(avg), full(wo), full(gf), full(wr), full(br), full(tri)],
        out_specs=[tile(D_MODEL), tile(D_MODEL), tile(LANES),
                   pl.BlockSpec((1, 8, LANES), lambda b, i: (b * n_s + i, 0, 0))],
        out_shape=[jax.ShapeDtypeStruct((B, S, D_MODEL), F32), jax.ShapeDtypeStruct((B, S, D_MODEL), BF16),
                   jax.ShapeDtypeStruct((B, S, LANES), F32), jax.ShapeDtypeStruct((B * n_s, 8, LANES), F32)],
        compiler_params=_cparams(2),
        name="post_mixer",
    )(x, oa, ob, wout, mk, mv, gx, wq, qgain, avg, wo, gf, wr, br, tri)


def _row_copy(src, dst, sem):
    return pltpu.make_async_copy(src, dst, sem)


def _run_pieces(n, max_piece, fn):
    b = RUN_ALIGN
    while b <= max_piece:
        pl.when((n & b) != 0)(functools.partial(fn, n & (-2 * b), b))
        b *= 2


def _move_groups(i, gmap_ref, n_loc, copy):
    n_groups = n_loc // RUN_ALIGN
    for j in range(n_groups):
        glob = pl.multiple_of(gmap_ref[i * n_groups + j], RUN_ALIGN)
        copy(pl.ds(j * RUN_ALIGN, RUN_ALIGN), pl.ds(glob, RUN_ALIGN)).start()


def _local_positions(route, loff_row):
    lane = lax.broadcasted_iota(jnp.int32, route.shape, 1).astype(F32)
    pos = []
    for s in range(2):
        base = jnp.sum(jnp.where(lane == route[:, s:s + 1], loff_row, 0.0), axis=-1, keepdims=True)
        pos.append(base + route[:, 4 + s:5 + s])
    return pos


def _scatter_body(gmap_ref, tstart_ref, tlen_ref, nact_ref,
                  h_ref, route_ref, lofff_ref, xs_ref, xloc, zbuf, sems):
    i = pl.program_id(0)
    tm = h_ref.shape[0]
    n_loc = xloc.shape[1]
    slot = i & 1
    sem = sems.at[0]

    def tails(start):
        def per_expert(e, carry):
            n = tlen_ref[e]
            st = tstart_ref[e]

            def piece(off, size):
                c = _row_copy(zbuf.at[pl.ds(0, size)], xs_ref.at[pl.ds(pl.multiple_of(st + off, RUN_ALIGN), size)], sem)
                c.start() if start else c.wait()

            _run_pieces(n, MOE_RB // 2, piece)
            return carry

        lax.fori_loop(0, N_EXPERTS, per_expert, 0)

    def unused(start):
        rows = zbuf.shape[0]

        def per_unit(u, carry):
            c = _row_copy(zbuf, xs_ref.at[pl.ds(pl.multiple_of(u * rows, rows), rows)], sem)
            c.start() if start else c.wait()
            return carry

        lax.fori_loop(nact_ref[0] * (MOE_RB // rows), xs_ref.shape[0] // rows, per_unit, 0)

    @pl.when(i == 0)
    def _():
        zbuf[...] = jnp.zeros(zbuf.shape, zbuf.dtype)
        tails(True)
        unused(True)
        tails(False)
        unused(False)

    pos = _local_positions(route_ref[...], lofff_ref[0, 0:1, :])
    col = lax.broadcasted_iota(jnp.int32, (tm, n_loc), 1).astype(F32)
    perm_t = jnp.where((col == pos[0]) | (col == pos[1]), 1.0, 0.0).astype(BF16)
    xloc[slot] = _dot_tn(perm_t, h_ref[...]).astype(BF16)

    def wait_slot(s):
        _row_copy(xloc.at[s], xs_ref.at[pl.ds(0, n_loc)], sems.at[s]).wait()

    pl.when(i > 0)(lambda: wait_slot(1 - slot))
    _move_groups(i, gmap_ref, n_loc, lambda loc, glob: _row_copy(xloc.at[slot, loc], xs_ref.at[glob], sems.at[slot]))
    pl.when(i == pl.num_programs(0) - 1)(lambda: wait_slot(slot))


def _scatter_rows(tables, h2d, route2d, loff_f, n_rows):
    T = h2d.shape[0]
    tm = MOE_TM
    n_loc = MOE_NLOC
    tile = lambda w: pl.BlockSpec((tm, w), lambda i, *_: (i, 0))
    grid_spec = pltpu.PrefetchScalarGridSpec(
        num_scalar_prefetch=4,
        grid=(T // tm,),
        in_specs=[tile(D_MODEL), tile(LANES), pl.BlockSpec((1, 8, LANES), lambda i, *_: (i, 0, 0))],
        out_specs=pl.BlockSpec(memory_space=pl.ANY),
        scratch_shapes=[pltpu.VMEM((2, n_loc, D_MODEL), BF16), pltpu.VMEM((MOE_RB // 2, D_MODEL), BF16),
                        pltpu.SemaphoreType.DMA((2,))],
    )
    return pl.pallas_call(
        _scatter_body,
        grid_spec=grid_spec,
        out_shape=jax.ShapeDtypeStruct((n_rows, D_MODEL), BF16),
        compiler_params=_cparams(1),
        name="moe_scatter",
    )(*tables, h2d, route2d, loff_f)


def _expert_body(blk0_ref, nblk_ref, n_act_ref, xs_ref, wg_ref, wu_ref, wd_ref, ys_ref,
                 wg_b, wu_b, wd_b, xbuf, ybuf, sem_in, sem_out):
    e = pl.program_id(0)
    n_slots, rb = xbuf.shape[0], xbuf.shape[1]
    ahead = n_slots - 2
    n_act = n_act_ref[0]
    b0 = blk0_ref[e]

    def rows(g):
        return pl.ds(pl.multiple_of(g * rb, rb), rb)

    def x_copy(g, slot):
        return _row_copy(xs_ref.at[rows(g)], xbuf.at[slot], sem_in.at[slot])

    def y_copy(g, slot):
        return _row_copy(ybuf.at[slot], ys_ref.at[rows(g)], sem_out.at[slot])

    @pl.when(e == 0)
    def _():
        for k in range(ahead):
            pl.when(k < n_act)(lambda k=k: x_copy(k, k).start())

    wg_b[...] = wg_ref[0].astype(BF16)
    wu_b[...] = wu_ref[0].astype(BF16)
    wd_b[...] = wd_ref[0].astype(BF16)

    def blocks(g0, count):
        gs = [g0 + c for c in range(count)]
        slots = [g & (n_slots - 1) for g in gs]
        for c in range(count):
            x_copy(gs[c], slots[c]).wait()
        for c in range(count):
            nxt = gs[c] + ahead
            pl.when(nxt < n_act)(lambda nxt=nxt: x_copy(nxt, nxt & (n_slots - 1)).start())
        x = [xbuf[slots[c]] for c in range(count)]
        a = [_dot(x[c], wg_b[...]) for c in range(count)]
        b = [_dot(x[c], wu_b[...]) for c in range(count)]
        h = [(_silu(a[c]) * b[c]).astype(BF16) for c in range(count)]
        y = [_dot(h[c], wd_b[...]).astype(BF16) for c in range(count)]
        for c in range(count):
            pl.when(gs[c] >= n_slots)(lambda c=c: y_copy(gs[c] - n_slots, slots[c]).wait())
        for c in range(count):
            ybuf[slots[c]] = y[c]
            y_copy(gs[c], slots[c]).start()

    nb = nblk_ref[e]

    def pair(j, carry):
        blocks(b0 + 2 * j, 2)
        return carry

    lax.fori_loop(0, lax.shift_right_logical(nb, 1), pair, 0)
    pl.when((nb & 1) == 1)(lambda: blocks(b0 + nb - 1, 1))

    @pl.when(e == pl.num_programs(0) - 1)
    def _():
        for k in range(1, n_slots + 1):
            pl.when(n_act >= k)(lambda k=k: y_copy(n_act - k, (n_act - k) & (n_slots - 1)).wait())
        ybuf[0] = jnp.zeros(ybuf.shape[1:], ybuf.dtype)
        n_blocks = ys_ref.shape[0] // rb

        def fill(start):
            def per_block(g, carry):
                c = y_copy(g, 0)
                c.start() if start else c.wait()
                return carry

            lax.fori_loop(n_act, n_blocks, per_block, 0)

        fill(True)
        fill(False)


def _experts(blk0, nblk, n_act, xs, n_rows, wg, wu, wd):
    rb = MOE_RB
    weight = lambda a: pl.BlockSpec((1,) + a.shape[1:], lambda e, *_: (e, 0, 0))
    grid_spec = pltpu.PrefetchScalarGridSpec(
        num_scalar_prefetch=3,
        grid=(N_EXPERTS,),
        in_specs=[pl.BlockSpec(memory_space=pl.ANY), weight(wg), weight(wu), weight(wd)],
        out_specs=pl.BlockSpec(memory_space=pl.ANY),
        scratch_shapes=[pltpu.VMEM((D_MODEL, EXPERT_FF), BF16), pltpu.VMEM((D_MODEL, EXPERT_FF), BF16),
                        pltpu.VMEM((EXPERT_FF, D_MODEL), BF16), pltpu.VMEM((MOE_SLOTS, rb, D_MODEL), BF16),
                        pltpu.VMEM((MOE_SLOTS, rb, D_MODEL), BF16), pltpu.SemaphoreType.DMA((MOE_SLOTS,)),
                        pltpu.SemaphoreType.DMA((MOE_SLOTS,))],
    )
    return pl.pallas_call(
        _expert_body,
        grid_spec=grid_spec,
        out_shape=jax.ShapeDtypeStruct((n_rows, D_MODEL), BF16),
        compiler_params=_cparams(1),
        name="moe_experts",
    )(blk0, nblk, n_act, xs, wg, wu, wd)


def _combine_body(gmap_ref, x_ref, route_ref, lofff_ref, ys_ref, o_ref, yloc, sems):
    i = pl.program_id(0)
    tm = x_ref.shape[0]
    n_loc = yloc.shape[1]
    slot = i & 1

    def fetch(tile, s):
        _move_groups(tile, gmap_ref, n_loc, lambda loc, glob: _row_copy(ys_ref.at[glob], yloc.at[s, loc], sems.at[s]))

    pl.when(i == 0)(lambda: fetch(i, slot))
    _row_copy(ys_ref.at[pl.ds(0, n_loc)], yloc.at[slot], sems.at[slot]).wait()
    pl.when(i + 1 < pl.num_programs(0))(lambda: fetch(i + 1, 1 - slot))

    route = route_ref[...]
    pos = _local_positions(route, lofff_ref[0, 0:1, :])
    col = lax.broadcasted_iota(jnp.int32, (tm, n_loc), 1).astype(F32)
    perm_w = jnp.where(col == pos[0], route[:, 2:3], jnp.where(col == pos[1], route[:, 3:4], 0.0)).astype(BF16)
    o_ref[...] = x_ref[...] + _dot(perm_w, yloc[slot])


def _combine(tables, x2d, route2d, loff_f, ys):
    T = x2d.shape[0]
    tm = MOE_TM
    n_loc = MOE_NLOC
    tile = lambda w: pl.BlockSpec((tm, w), lambda i, *_: (i, 0))
    grid_spec = pltpu.PrefetchScalarGridSpec(
        num_scalar_prefetch=1,
        grid=(T // tm,),
        in_specs=[tile(D_MODEL), tile(LANES), pl.BlockSpec((1, 8, LANES), lambda i, *_: (i, 0, 0)),
                  pl.BlockSpec(memory_space=pl.ANY)],
        out_specs=tile(D_MODEL),
        scratch_shapes=[pltpu.VMEM((2, n_loc, D_MODEL), BF16), pltpu.SemaphoreType.DMA((2,))],
    )
    return pl.pallas_call(
        _combine_body,
        grid_spec=grid_spec,
        out_shape=jax.ShapeDtypeStruct((T, D_MODEL), F32),
        compiler_params=_cparams(1),
        name="moe_combine",
    )(*tables, x2d, route2d, loff_f, ys)


def _compress_weights(pos, w2):
    eye = jnp.eye(NSA_GROUPS, dtype=F32)
    w2b = jnp.einsum('hd,gk->ghkd', w2, eye).reshape(NSA_GROUPS * CMP_HIDDEN, LANES)
    posb = jnp.tile(pos, (1, NSA_GROUPS)).reshape(2, 1, CMP_STRIDE * LANES)
    return posb, w2b.astype(BF16)


def _dup2(g):
    return jnp.tile(g.reshape(1, HEAD_DIM), (1, 2))


def kernel(x, mem, mix_norm, w_in, nsa_q_norm, nsa_kcmp_norm, nsa_ksel_norm, nsa_kwin_norm, cmp_pos_k, cmp_pos_v, cmp_k_w1, cmp_k_w2, cmp_v_w1, cmp_v_w2, nsa_out_norm, ret_out_norm, w_out, mem_x_norm, mem_kv_norm, mem_wq, mem_wkv, mem_q_norm, mem_k_norm, mem_wo, ffn_norm, router_group_w, router_group_b, router_expert_w, router_expert_b, exp_w_gate, exp_w_up, exp_w_down):
    B, S, D = x.shape
    T = B * S
    depth = mix_norm.shape[0]
    for l in range(depth):
        proj = _proj(x.reshape(T, D), mix_norm[l].reshape(1, D), w_in[l].T).reshape(B, S, PROJ_PAD)
        pk, w2k = _compress_weights(cmp_pos_k[l], cmp_k_w2[l])
        pv, w2v = _compress_weights(cmp_pos_v[l], cmp_v_w2[l])
        gains = jnp.stack([_dup2(nsa_kcmp_norm[l]), _dup2(nsa_ksel_norm[l]), _dup2(nsa_kwin_norm[l])])
        kcmp, vcmp, ks, vs, kw, vw = _nsa_prep(proj, jnp.stack([pk, pv]), cmp_k_w1[l], cmp_v_w1[l],
                                               jnp.stack([w2k, w2v]), gains)
        o_a = _nsa_attn(proj, kcmp, vcmp, ks, vs, kw, vw,
                        jnp.tile(nsa_q_norm[l].reshape(1, HEAD_DIM), (1, NSA_HEADS)), gains,
                        nsa_out_norm[l].reshape(1, NSA_WIDTH))
        o_b = _retention(proj, ret_out_norm[l].reshape(1, RET_WIDTH))
        mk, mv = _mem_prep(mem, mem_kv_norm[l].reshape(1, D), mem_wkv[l].astype(BF16),
                           jnp.tile(mem_k_norm[l].reshape(1, HEAD_DIM), (1, MEM_HEADS)))
        w_r = jnp.concatenate([router_group_w[l],
                               router_expert_w[l].transpose(1, 0, 2).reshape(D, N_EXPERTS),
                               jnp.zeros((D, LANES - N_GROUPS - N_EXPERTS), F32)], axis=1).astype(BF16)
        b_r = jnp.concatenate([router_group_b[l], router_expert_b[l].reshape(N_EXPERTS),
                               jnp.zeros((LANES - N_GROUPS - N_EXPERTS,), F32)]).reshape(1, LANES)
        x2, hf, route, counts = _post(
            x, o_a, o_b, w_out[l].astype(BF16), mk, mv, mem_x_norm[l].reshape(1, D), mem_wq[l].astype(BF16),
            jnp.tile(mem_q_norm[l].reshape(1, HEAD_DIM), (1, MEM_HEADS)), mem_wo[l].astype(BF16),
            ffn_norm[l].reshape(1, D), w_r, b_r)
        route2d = route.reshape(T, LANES)
        n_tiles = T // MOE_TM
        cnt = counts[:, 0, :N_EXPERTS].astype(jnp.int32)
        cnt = (cnt + RUN_ALIGN - 1) // RUN_ALIGN * RUN_ALIGN
        loff = jnp.cumsum(cnt, axis=1) - cnt
        total = jnp.sum(cnt, axis=0)
        padded = (total + MOE_RB - 1) // MOE_RB * MOE_RB
        pend = jnp.cumsum(padded)
        pstart = pend - padded
        goff = pstart[None, :] + jnp.cumsum(cnt, axis=0) - cnt
        n_rows = 2 * T + n_tiles * N_EXPERTS * RUN_ALIGN + N_EXPERTS * MOE_RB
        n_act = (pend[-1:] // MOE_RB).astype(jnp.int32)
        loff_f = jnp.broadcast_to(jnp.pad(loff.astype(F32), ((0, 0), (0, LANES - N_EXPERTS)))[:, None, :],
                                  (n_tiles, 8, LANES))
        grp_row = jnp.arange(MOE_NLOC // RUN_ALIGN, dtype=jnp.int32) * RUN_ALIGN
        inside = ((loff[:, None, :] <= grp_row[None, :, None])
                  & (grp_row[None, :, None] < (loff + cnt)[:, None, :])).astype(jnp.int32)
        shift = jnp.sum(inside * (goff - loff)[:, None, :], axis=2)
        used = jnp.sum(inside, axis=2) > 0
        gmap_scatter = jnp.where(used, shift + grp_row[None, :], n_rows + grp_row[None, :]).reshape(-1)
        gmap_gather = jnp.where(used, shift + grp_row[None, :], 0).reshape(-1)
        xs = _scatter_rows((gmap_scatter, pstart + total, padded - total, n_act), hf.reshape(T, D), route2d, loff_f,
                           n_rows + MOE_NLOC)
        ys = _experts(pstart // MOE_RB, padded // MOE_RB, n_act, xs, n_rows,
                      exp_w_gate[l], exp_w_up[l], exp_w_down[l])
        x = _combine((gmap_gather,), x2.reshape(T, D), route2d, loff_f, ys).reshape(B, S, D)
    return x
```

```python
import functools

import numpy as np
import jax
import jax.numpy as jnp
from jax import lax
from jax.experimental import pallas as pl
from jax.experimental.pallas import tpu as pltpu

F32 = jnp.float32
BF16 = jnp.bfloat16

D_MODEL = 1024
HEAD_DIM = 64
LANES = 128
NSA_HEADS = 8
NSA_GROUPS = 2
NSA_WIDTH = NSA_HEADS * HEAD_DIM
CMP_BLOCK = 32
CMP_STRIDE = 16
CMP_HIDDEN = 2 * HEAD_DIM
SEL_BLOCK = 64
SEL_TOPK = 8
WINDOW = 512
RET_HEADS = 8
RET_WIDTH = RET_HEADS * HEAD_DIM
RET_CHUNK = 128
ROPE_BASE = 10000.0
MEM_HEADS = 4
MEM_WIDTH = MEM_HEADS * HEAD_DIM
N_GROUPS = 4
EXPERTS_PER_GROUP = 8
N_EXPERTS = N_GROUPS * EXPERTS_PER_GROUP
EXPERT_FF = D_MODEL // 4
EPS = 1e-6
NEG_INF = -1e30
BELOW_ALL = -3e38
MAX_FIXED_SHIFT = 40.0

COL_QA = 0
COL_QR, COL_KR, COL_VR, COL_GR = 512, 1024, 1536, 2048
COL_KVC, COL_KSV, COL_KWV = 2560, 2816, 3072
COL_GATE = 3328
PROJ_PAD = 3456
PROJ_SPLITS = (NSA_WIDTH, NSA_WIDTH + 6 * NSA_GROUPS * HEAD_DIM, NSA_WIDTH + 6 * NSA_GROUPS * HEAD_DIM + 3 * NSA_HEADS)

PROJ_TM = 512
RET_UNROLL = 4
NSA_TQ = 256
SEL_KC = 512
POST_CHAINS = 2
MOE_TM = 512
MOE_RB = 256
MOE_SLOTS = 8
RUN_ALIGN = 16
MOE_NLOC = 2 * MOE_TM + N_EXPERTS * RUN_ALIGN
VMEM_LIMIT = 56 * 1024 * 1024


def _cparams(n_axes):
    return pltpu.CompilerParams(dimension_semantics=("arbitrary",) * n_axes,
                                vmem_limit_bytes=VMEM_LIMIT)


def _dot(a, b):
    return jnp.dot(a, b, preferred_element_type=F32)


def _dot_nt(a, b):
    return lax.dot_general(a, b, (((1,), (1,)), ((), ())), preferred_element_type=F32)


def _dot_tn(a, b):
    return lax.dot_general(a, b, (((0,), (0,)), ((), ())), preferred_element_type=F32)


def _rms_full(x, g):
    ms = jnp.mean(x * x, axis=-1, keepdims=True)
    return x * lax.rsqrt(ms + EPS) * g


def _seg_mean(x, avg):
    return _dot(x.astype(BF16), avg)


def _silu(x):
    return x * (1.0 / (1.0 + jnp.exp(-x)))


def _sigmoid(x):
    return 1.0 / (1.0 + jnp.exp(-x))


def _block_avg(width):
    i = np.arange(width)
    return ((i[:, None] // HEAD_DIM == i[None, :] // HEAD_DIM) / HEAD_DIM).astype(np.float32)


def _proj_body(x_ref, g_ref, wt_ref, o_ref, w_scr):
    @pl.when(pl.program_id(0) == 0)
    def _():
        kv0, gate0, ret0 = PROJ_SPLITS
        w_scr[COL_QA:COL_QR] = wt_ref[0:kv0].astype(BF16)
        w_scr[COL_QR:COL_KVC] = wt_ref[ret0:].astype(BF16)
        w_scr[COL_KVC:COL_GATE] = wt_ref[kv0:gate0].astype(BF16)
        pad = jnp.zeros((PROJ_PAD - COL_GATE - (ret0 - gate0), wt_ref.shape[1]), F32)
        w_scr[COL_GATE:] = jnp.concatenate([wt_ref[gate0:ret0], pad], axis=0).astype(BF16)

    h = _rms_full(x_ref[...], g_ref[...]).astype(BF16)
    step = PROJ_PAD // 3
    for j in range(3):
        o_ref[:, j * step:(j + 1) * step] = _dot_nt(h, w_scr[j * step:(j + 1) * step]).astype(BF16)


def _proj(x2d, g, w_t):
    T = x2d.shape[0]
    assert w_t.shape[0] == PROJ_SPLITS[2] + 4 * RET_WIDTH
    return pl.pallas_call(
        _proj_body,
        grid=(T // PROJ_TM,),
        in_specs=[pl.BlockSpec((PROJ_TM, D_MODEL), lambda i: (i, 0)),
                  pl.BlockSpec((1, D_MODEL), lambda i: (0, 0)),
                  pl.BlockSpec(w_t.shape, lambda i: (0, 0))],
        out_specs=pl.BlockSpec((PROJ_TM, PROJ_PAD), lambda i: (i, 0)),
        out_shape=jax.ShapeDtypeStruct((T, PROJ_PAD), BF16),
        scratch_shapes=[pltpu.VMEM((PROJ_PAD, D_MODEL), BF16)],
        compiler_params=_cparams(1),
        name="proj",
    )(x2d, g, w_t)


def _dup_groups(x):
    lane = lax.broadcasted_iota(jnp.int32, x.shape, 1)
    xs = pltpu.roll(x, HEAD_DIM, axis=1)
    lo = lane < HEAD_DIM
    return jnp.where(lo, x, xs), jnp.where(lo, xs, x)


def _ones_groups(x):
    lane = lax.broadcasted_iota(jnp.int32, x.shape, 1)
    lo = lane < HEAD_DIM
    xs = pltpu.roll(x, HEAD_DIM, axis=1)
    return jnp.where(lo, x, 1.0), jnp.where(lo, 1.0, xs), jnp.where(lo, xs, 1.0), jnp.where(lo, 1.0, x)


def _nsa_prep_body(kvc_ref, ksv_ref, kwv_ref, pos_ref, w1k_ref, w1v_ref, w2_ref, gain_ref, avg_ref,
                   kcmp_ref, vcmp_ref, ks_ref, vs_ref, kw_ref, vw_ref, scr_k, scr_v, w1_ref):
    @pl.when(pl.program_id(0) == 0)
    def _():
        zero = jnp.zeros((HEAD_DIM, CMP_HIDDEN), BF16)
        for j, src in ((0, w1k_ref), (1, w1v_ref)):
            for l in range(CMP_BLOCK):
                piece = src[l * HEAD_DIM:(l + 1) * HEAD_DIM, :].astype(BF16)
                r0 = (l % CMP_STRIDE) * LANES
                w1_ref[j, l // CMP_STRIDE, r0:r0 + HEAD_DIM, :] = jnp.concatenate([piece, zero], axis=1)
                w1_ref[j, l // CMP_STRIDE, r0 + HEAD_DIM:r0 + LANES, :] = jnp.concatenate([zero, piece], axis=1)

    avg = avg_ref[...]
    n_c = scr_k.shape[0] // CMP_STRIDE
    scr_k[...] = kvc_ref[0, :, 0:LANES].astype(F32)
    scr_v[...] = kvc_ref[0, :, LANES:2 * LANES].astype(F32)
    for j, out_ref, scr in ((0, kcmp_ref, scr_k), (1, vcmp_ref, scr_v)):
        ycat = jnp.concatenate(
            [scr[pl.ds(l, n_c, stride=CMP_STRIDE), :] for l in range(CMP_STRIDE)], axis=1)
        first = _dot((ycat + pos_ref[j, 0]).astype(BF16), w1_ref[j, 0])
        second = _dot((ycat + pos_ref[j, 1]).astype(BF16), w1_ref[j, 1])
        hidden = first + pltpu.roll(second, n_c - 1, axis=0)
        cmp_tok = _dot(_silu(hidden).astype(BF16), w2_ref[j])
        if j == 0:
            ms = _seg_mean(cmp_tok * cmp_tok, avg)
            cmp_tok = cmp_tok * lax.rsqrt(ms + EPS) * gain_ref[0]
        for i, piece in enumerate(_dup_groups(cmp_tok) if j == 0 else _ones_groups(cmp_tok)):
            out_ref[0, i] = piece.astype(BF16)

    for src_ref, k_out, v_out, gi in ((ksv_ref, ks_ref, vs_ref, 1), (kwv_ref, kw_ref, vw_ref, 2)):
        k = src_ref[0, :, 0:LANES].astype(F32)
        ms = _seg_mean(k * k, avg)
        k = k * lax.rsqrt(ms + EPS) * gain_ref[gi]
        d0, d1 = _dup_groups(k)
        k_out[0, 0] = d0.astype(BF16)
        k_out[0, 1] = d1.astype(BF16)
        for i, piece in enumerate(_ones_groups(src_ref[0, :, LANES:2 * LANES].astype(F32))):
            v_out[0, i] = piece.astype(BF16)


def _nsa_prep(proj3, pos, w1k, w1v, w2, gains):
    B, S, _ = proj3.shape
    n_c = S // CMP_STRIDE
    avg = jnp.asarray(_block_avg(LANES), BF16)
    col = lambda c: pl.BlockSpec((1, S, 2 * LANES), lambda b: (b, 0, c // (2 * LANES)))
    full = lambda a: pl.BlockSpec(a.shape, lambda b: (0,) * a.ndim)
    cmp_spec = pl.BlockSpec((1, NSA_GROUPS, n_c, LANES), lambda b: (b, 0, 0, 0))
    seq_spec = pl.BlockSpec((1, NSA_GROUPS, S, LANES), lambda b: (b, 0, 0, 0))
    cmp_shape = jax.ShapeDtypeStruct((B, NSA_GROUPS, n_c, LANES), BF16)
    seq_shape = jax.ShapeDtypeStruct((B, NSA_GROUPS, S, LANES), BF16)

    def val(a):
        if isinstance(a, jax.ShapeDtypeStruct):
            return jax.ShapeDtypeStruct((a.shape[0], 2 * a.shape[1]) + a.shape[2:], a.dtype)
        return pl.BlockSpec((1, 2 * NSA_GROUPS) + a.block_shape[2:], lambda b: (b, 0, 0, 0))

    return pl.pallas_call(
        _nsa_prep_body,
        grid=(B,),
        in_specs=[col(COL_KVC), col(COL_KSV), col(COL_KWV), full(pos), full(w1k), full(w1v), full(w2), full(gains),
                  full(avg)],
        out_specs=[cmp_spec, val(cmp_spec), seq_spec, val(seq_spec), seq_spec, val(seq_spec)],
        out_shape=[cmp_shape, val(cmp_shape), seq_shape, val(seq_shape), seq_shape, val(seq_shape)],
        scratch_shapes=[pltpu.VMEM((S, LANES), F32), pltpu.VMEM((S, LANES), F32),
                        pltpu.VMEM((2, 2, CMP_STRIDE * LANES, NSA_GROUPS * CMP_HIDDEN), BF16)],
        compiler_params=_cparams(1),
        name="nsa_prep",
    )(proj3, proj3, proj3, pos, w1k, w1v, w2, gains, avg)


def _nsa_attn_body(q_ref, gate_ref, kcmp_ref, vcmp_ref, ks_ref, vs_ref, kw_ref, vw_ref,
                   qgain_ref, kgain_ref, ogain_ref, avgq_ref, avgo_ref, msct_ref, esel_ref, egate_ref, wbias_ref,
                   dbias_ref, cbias_ref,
                   o_ref, m_scr, acc_scr):
    tq = q_ref.shape[1]
    n_cmp = kcmp_ref.shape[2]
    n_sel = msct_ref.shape[0]
    kc_len = esel_ref.shape[2]
    rows = 4 * tq
    qi = pl.program_id(1)
    q0 = qi * tq

    q = q_ref[0].astype(F32)
    ms = _seg_mean(q * q, avgq_ref[...])
    qn = q * lax.rsqrt(ms + EPS) * qgain_ref[...] * (HEAD_DIM ** -0.5)

    gate_sig = _sigmoid(gate_ref[0].astype(F32)).astype(BF16)
    gates = [_dot(gate_sig, egate_ref[j]) for j in range(3)]

    lane_q = lax.broadcasted_iota(jnp.int32, (tq, LANES), 1)
    lo_q = lane_q < HEAD_DIM

    blk = lax.broadcasted_iota(jnp.int32, (n_sel, tq), 0)
    cur = lax.shift_right_logical(q0 + lax.broadcasted_iota(jnp.int32, (n_sel, tq), 1), int(np.log2(SEL_BLOCK)))
    forced = (blk == 0) | (blk == cur) | (blk == cur - 1)
    future = blk > cur
    blk_f = blk.astype(F32)

    def heads4(x):
        return jnp.concatenate([x] * 4, axis=0)

    def weighted_values(p, v_low, v_high):
        return jnp.concatenate([_dot(p[:2 * tq], v_low), _dot(p[2 * tq:], v_high)], axis=0)

    def normalised_pairs(acc, guard):
        out = []
        for p in range(2):
            low = acc[p * tq:(p + 1) * tq]
            high = acc[(2 + p) * tq:(3 + p) * tq]
            den = pltpu.roll(jnp.where(lo_q, high, low), HEAD_DIM, axis=1)
            if guard:
                den = jnp.maximum(den, 1e-30)
            out.append(jnp.where(lo_q, low, high) / den)
        return out

    groups = range(NSA_GROUPS)
    qs = []
    for g in groups:
        slabs = [qn[:, (2 * g + p) * LANES:(2 * g + p + 1) * LANES] for p in range(2)]
        qs.append(jnp.concatenate(
            [jnp.where(lo_q, slabs[0], 0.0), jnp.where(lo_q, slabs[1], 0.0),
             jnp.where(lo_q, 0.0, slabs[0]), jnp.where(lo_q, 0.0, slabs[1])], axis=0).astype(BF16))

    def compressed_and_select(shift):
        s_c = [_dot_nt(qs[g], kcmp_ref[0, g]) for g in groups]
        if shift is None:
            r_c = lax.broadcasted_iota(jnp.int32, (rows, n_cmp), 0)
            c_c = lax.broadcasted_iota(jnp.int32, (rows, n_cmp), 1)
            cmask = (c_c * CMP_STRIDE + (CMP_BLOCK - 1)) <= q0 + (r_c & (tq - 1))
            s_c = [jnp.where(cmask, s_c[g], NEG_INF) for g in groups]
            e_c = [jnp.where(cmask, jnp.exp(s_c[g] - jnp.max(s_c[g], axis=-1, keepdims=True)), 0.0) for g in groups]
        else:
            bias = heads4(cbias_ref[qi] - shift)
            e_c = [jnp.exp(s_c[g] + bias) for g in groups]
        e_b = [e_c[g].astype(BF16) for g in groups]
        acc_c = [weighted_values(e_b[g], vcmp_ref[0, 2 * g], vcmp_ref[0, 2 * g + 1]) for g in groups]
        ones = jnp.ones((8, n_cmp), BF16)
        imp = []
        for g in groups:
            num = [_dot_nt(msct_ref[...], e_b[g][h * tq:(h + 1) * tq]) for h in range(4)]
            den = [_dot_nt(ones, e_b[g][h * tq:(h + 1) * tq])[0:1] for h in range(4)]
            parts = [num[h] / jnp.maximum(den[h], 1e-30) for h in range(4)]
            imp.append((parts[0] + parts[1]) + (parts[2] + parts[3]))
        v = [jnp.where(forced, BELOW_ALL, jnp.where(future, NEG_INF, imp[g])) for g in groups]
        sel = [jnp.where(forced, 1.0, 0.0) for g in groups]
        for _ in range(SEL_TOPK - 3):
            mx = [jnp.max(v[g], axis=0, keepdims=True) for g in groups]
            first = [jnp.min(jnp.where(v[g] == mx[g], blk_f, float(LANES)), axis=0, keepdims=True) for g in groups]
            pick = [blk_f == first[g] for g in groups]
            sel = [jnp.where(pick[g], 1.0, sel[g]) for g in groups]
            v = [jnp.where(pick[g], BELOW_ALL, v[g]) for g in groups]
        return [normalised_pairs(acc_c[g], True) for g in groups], [sel[g].astype(BF16) for g in groups]

    n_before = lax.shift_right_logical(q0, int(np.log2(kc_len)))
    causal = dbias_ref[qi & (kc_len // tq - 1)]
    w0 = pl.multiple_of(jnp.maximum(q0 - WINDOW, 0), tq)
    n_w = WINDOW + tq
    w_case = jnp.minimum(qi, WINDOW // tq)

    def sel_keys(ref, g, kc):
        return ref[0, g, pl.ds(pl.multiple_of(kc * kc_len, kc_len), kc_len), :]

    def sel_scores(sel_b, g, kc, causal_bias, shift):
        chosen = _dot_tn(sel_b[g], esel_ref[kc])
        bias = (chosen - 1.0) * (-NEG_INF)
        if causal_bias is not None:
            bias = bias + causal_bias
        if shift is not None:
            bias = bias - shift
        return _dot_nt(qs[g], sel_keys(ks_ref, g, kc)) + heads4(bias)

    def win_scores(g, shift):
        bias = wbias_ref[w_case] if shift is None else wbias_ref[w_case] - shift
        return _dot_nt(qs[g], kw_ref[0, g, pl.ds(w0, n_w), :]) + heads4(bias)

    def win_values(g):
        return vw_ref[0, 2 * g, pl.ds(w0, n_w), :], vw_ref[0, 2 * g + 1, pl.ds(w0, n_w), :]

    def sel_values(g, kc):
        return sel_keys(vs_ref, 2 * g, kc), sel_keys(vs_ref, 2 * g + 1, kc)

    def finish(cmp_s, acc_w):
        for g in groups:
            sel_s = normalised_pairs(acc_scr[g], False)
            win_s = normalised_pairs(acc_w[g], False)
            for p in range(2):
                cols = slice((2 * g + p) * LANES, (2 * g + p + 1) * LANES)
                mix = gates[0][:, cols] * cmp_s[g][p] + gates[1][:, cols] * sel_s[p] + gates[2][:, cols] * win_s[p]
                ms_o = _seg_mean(mix * mix, avgo_ref[...])
                o_ref[0, :, cols] = (mix * lax.rsqrt(ms_o + EPS) * ogain_ref[:, cols]).astype(BF16)

    def fixed_shift_path(shift):
        cmp_s, sel_b = compressed_and_select(shift)

        def probs(s):
            return jnp.exp(s).astype(BF16)

        for g in groups:
            acc_scr[g] = jnp.zeros(acc_scr.shape[1:], F32)

        def before(kc, carry):
            s = [sel_scores(sel_b, g, kc, None, shift) for g in groups]
            p = [probs(s[g]) for g in groups]
            for g in groups:
                acc_scr[g] = acc_scr[g] + weighted_values(p[g], *sel_values(g, kc))
            return carry

        lax.fori_loop(0, n_before, before, 0)
        s_d0 = sel_scores(sel_b, 0, n_before, causal, shift)
        s_d1 = sel_scores(sel_b, 1, n_before, causal, shift)
        p_d0 = probs(s_d0)
        s_w0 = win_scores(0, shift)
        acc_scr[0] = acc_scr[0] + weighted_values(p_d0, *sel_values(0, n_before))
        p_d1 = probs(s_d1)
        s_w1 = win_scores(1, shift)
        acc_scr[1] = acc_scr[1] + weighted_values(p_d1, *sel_values(1, n_before))
        p_w0 = probs(s_w0)
        acc_w0 = weighted_values(p_w0, *win_values(0))
        p_w1 = probs(s_w1)
        acc_w1 = weighted_values(p_w1, *win_values(1))
        finish(cmp_s, [acc_w0, acc_w1])

    def online_path():
        cmp_s, sel_b = compressed_and_select(None)
        for g in groups:
            m_scr[g] = jnp.full(m_scr.shape[1:], NEG_INF, F32)
            acc_scr[g] = jnp.zeros(acc_scr.shape[1:], F32)

        def sel_softmax(g, s):
            m_old = m_scr[g]
            m_new = jnp.maximum(m_old, jnp.max(s, axis=-1, keepdims=True))
            m_scr[g] = m_new
            return jnp.exp(s - m_new).astype(BF16), jnp.exp(m_old - m_new)

        def sel_accumulate(g, kc, p, alpha):
            acc_scr[g] = alpha * acc_scr[g] + weighted_values(p, *sel_values(g, kc))

        def win_softmax(s):
            return jnp.exp(s - jnp.max(s, axis=-1, keepdims=True)).astype(BF16)

        def before(kc, carry):
            s = [sel_scores(sel_b, g, kc, None, None) for g in groups]
            pa = [sel_softmax(g, s[g]) for g in groups]
            for g in groups:
                sel_accumulate(g, kc, *pa[g])
            return carry

        lax.fori_loop(0, n_before, before, 0)
        s_d0 = sel_scores(sel_b, 0, n_before, causal, None)
        s_d1 = sel_scores(sel_b, 1, n_before, causal, None)
        pa0 = sel_softmax(0, s_d0)
        s_w0 = win_scores(0, None)
        sel_accumulate(0, n_before, *pa0)
        pa1 = sel_softmax(1, s_d1)
        s_w1 = win_scores(1, None)
        sel_accumulate(1, n_before, *pa1)
        acc_w0 = weighted_values(win_softmax(s_w0), *win_values(0))
        acc_w1 = weighted_values(win_softmax(s_w1), *win_values(1))
        finish(cmp_s, [acc_w0, acc_w1])

    bound = 1.01 * (HEAD_DIM ** 0.5) * jnp.max(jnp.abs(qgain_ref[...])) * jnp.max(jnp.abs(kgain_ref[...]))
    safe = bound <= MAX_FIXED_SHIFT
    pl.when(safe)(lambda: fixed_shift_path(bound))
    pl.when(jnp.logical_not(safe))(online_path)


def _sel_from_cmp(n_cmp, n_sel):
    c0 = np.arange(n_cmp) * CMP_STRIDE
    s0 = np.arange(n_sel) * SEL_BLOCK
    ov = np.minimum(c0[None, :] + CMP_BLOCK, s0[:, None] + SEL_BLOCK) - np.maximum(c0[None, :], s0[:, None])
    m = (np.clip(ov, 0, None) / CMP_BLOCK).astype(np.float32)
    m[:, (np.arange(n_cmp) * CMP_STRIDE + CMP_BLOCK) > n_sel * SEL_BLOCK] = 0.0
    return m


def _nsa_attn(proj3, kcmp, vcmp, ks, vs, kw, vw, q_gain, k_gains, o_gain):
    B, S, _ = proj3.shape
    n_cmp = kcmp.shape[2]
    n_sel = S // SEL_BLOCK
    tq = NSA_TQ
    assert n_sel % 8 == 0 and S % SEL_KC == 0 and SEL_KC % tq == 0 and WINDOW % tq == 0 and S >= WINDOW + tq
    avgq = jnp.asarray(_block_avg(NSA_WIDTH), BF16)
    avgo = jnp.asarray(_block_avg(LANES), BF16)
    msct = jnp.asarray(_sel_from_cmp(n_cmp, n_sel), BF16)
    esel = (np.arange(n_sel)[:, None] == np.arange(S)[None, :] // SEL_BLOCK).astype(np.float32)
    esel = jnp.asarray(esel.reshape(n_sel, S // SEL_KC, SEL_KC).transpose(1, 0, 2), BF16)
    src = np.arange(LANES)[:, None]
    dst = np.arange(NSA_WIDTH)[None, :]
    egate = jnp.asarray(np.stack([(src == (dst // HEAD_DIM) * 3 + j) for j in range(3)]).astype(np.float32), BF16)
    r = np.arange(tq)[:, None]
    n_w = WINDOW + tq
    wcases = []
    for i in range(WINDOW // tq + 1):
        diff = (i * tq - max(i * tq - WINDOW, 0)) + r - np.arange(n_w)[None, :]
        wcases.append(np.where((diff >= 0) & (diff < WINDOW), 0.0, NEG_INF))
    wbias = jnp.asarray(np.stack(wcases), F32)
    dbias = jnp.asarray(np.stack([np.where(np.arange(SEL_KC)[None, :] <= i * tq + r, 0.0, NEG_INF)
                                  for i in range(SEL_KC // tq)]), F32)
    c_end = np.arange(n_cmp)[None, :] * CMP_STRIDE + (CMP_BLOCK - 1)
    cbias = jnp.asarray(np.stack([np.where(c_end <= i * tq + r, 0.0, NEG_INF) for i in range(S // tq)]), F32)

    full = lambda a: pl.BlockSpec(a.shape, lambda b, i: (0,) * a.ndim)
    per_b = lambda a: pl.BlockSpec((1,) + a.shape[1:], lambda b, i: (b,) + (0,) * (a.ndim - 1))
    return pl.pallas_call(
        _nsa_attn_body,
        grid=(B, S // tq),
        in_specs=[pl.BlockSpec((1, tq, NSA_WIDTH), lambda b, i: (b, i, COL_QA // NSA_WIDTH)),
                  pl.BlockSpec((1, tq, LANES), lambda b, i: (b, i, COL_GATE // LANES)),
                  per_b(kcmp), per_b(vcmp), per_b(ks), per_b(vs), per_b(kw), per_b(vw),
                  full(q_gain), full(k_gains), full(o_gain), full(avgq), full(avgo), full(msct), full(esel), full(egate),
                  full(wbias), full(dbias), full(cbias)],
        out_specs=pl.BlockSpec((1, tq, NSA_WIDTH), lambda b, i: (b, i, 0)),
        out_shape=jax.ShapeDtypeStruct((B, S, NSA_WIDTH), BF16),
        scratch_shapes=[pltpu.VMEM((NSA_GROUPS, 4 * tq, 1), F32), pltpu.VMEM((NSA_GROUPS, 4 * tq, LANES), F32)],
        compiler_params=_cparams(2),
        name="nsa_attn",
    )(proj3, proj3, kcmp, vcmp, ks, vs, kw, vw, q_gain, k_gains, o_gain, avgq, avgo, msct, esel, egate, wbias, dbias,
      cbias)


def _retention_body(q_ref, k_ref, v_ref, g_ref, cos_ref, sin_ref, decay_ref, xi_ref, zeta_ref, gammac_ref,
                    gain_ref, avg_ref, o_ref, state_scr):
    S = q_ref.shape[1]
    C = RET_CHUNK
    lane = lax.broadcasted_iota(jnp.int32, (C, LANES), 1)
    lo = lane < HEAD_DIM
    first_half = (lane & (HEAD_DIM - 1)) < HEAD_DIM // 2
    r = lax.broadcasted_iota(jnp.int32, (LANES, LANES), 0)
    c = lax.broadcasted_iota(jnp.int32, (LANES, LANES), 1)
    same_head = (r < HEAD_DIM) == (c < HEAD_DIM)
    avg = avg_ref[...]
    state_scr[...] = jnp.zeros(state_scr.shape, F32)

    def rope(x, cos, sin):
        swapped = jnp.where(first_half, pltpu.roll(x, LANES - HEAD_DIM // 2, axis=1),
                            pltpu.roll(x, HEAD_DIM // 2, axis=1))
        return x * cos + swapped * sin

    n_pairs = RET_HEADS // 2
    cols = [slice(p * LANES, (p + 1) * LANES) for p in range(n_pairs)]
    units = [(u, p) for u in range(RET_UNROLL) for p in range(n_pairs)]

    def chunks(n, carry):
        r0 = [pl.multiple_of((n * RET_UNROLL + u) * C, C) for u in range(RET_UNROLL)]
        cos = [cos_ref[pl.ds(r0[u], C), :] for u in range(RET_UNROLL)]
        sin = [sin_ref[pl.ds(r0[u], C), :] for u in range(RET_UNROLL)]
        q = [rope(q_ref[0, pl.ds(r0[u], C), cols[p]].astype(F32), cos[u], sin[u]) for u, p in units]
        k = [rope(k_ref[0, pl.ds(r0[u], C), cols[p]].astype(F32), cos[u], sin[u]) * (HEAD_DIM ** -0.5)
             for u, p in units]
        vb = [v_ref[0, pl.ds(r0[u], C), cols[p]] for u, p in units]
        kb = [k[i].astype(BF16) for i in range(len(units))]
        inner = [_dot_nt(jnp.where(lo if half == 0 else ~lo, q[i], 0.0).astype(BF16), kb[i])
                 * decay_ref[2 * units[i][1] + half] for i in range(len(units)) for half in range(2)]
        upd = [_dot_tn((k[i] * zeta_ref[units[i][1]]).astype(BF16), vb[i]) for i in range(len(units))]
        state = [state_scr[p] for p in range(n_pairs)]
        for u in range(RET_UNROLL):
            for p in range(n_pairs):
                prev = state[u * n_pairs + p]
                state.append(gammac_ref[p] * prev + jnp.where(same_head, upd[u * n_pairs + p], 0.0))
        cross = [_dot(q[i].astype(BF16), state[i].astype(BF16)) * xi_ref[units[i][1]] for i in range(len(units))]
        outs = [_dot(inner[j].astype(BF16), vb[j // 2]) for j in range(2 * len(units))]
        for p in range(n_pairs):
            state_scr[p] = state[RET_UNROLL * n_pairs + p]
        y = jnp.concatenate([jnp.where(lo, outs[2 * i], outs[2 * i + 1]) + cross[i] for i in range(len(units))],
                            axis=0)
        mu = _seg_mean(y, avg)
        d = y - mu
        var = _seg_mean(d * d, avg)
        yn = d * lax.rsqrt(var + EPS)
        for i, (u, p) in enumerate(units):
            gate = g_ref[0, pl.ds(r0[u], C), cols[p]].astype(F32)
            o_ref[0, pl.ds(r0[u], C), cols[p]] = (_silu(gate) * (yn[i * C:(i + 1) * C] * gain_ref[:, cols[p]])).astype(BF16)
        return carry

    lax.fori_loop(0, S // (C * RET_UNROLL), chunks, 0)


def _retention_tables(S):
    half = HEAD_DIM // 2
    inv_freq = ROPE_BASE ** (-jnp.arange(half, dtype=F32) / half)
    ang = jnp.arange(S, dtype=F32)[:, None] * inv_freq[None, :]
    cos, sin = jnp.cos(ang), jnp.sin(ang)
    cos_t = jnp.tile(cos, (1, 4))
    sin_t = jnp.tile(jnp.concatenate([-sin, sin], axis=1), (1, 2))
    C = RET_CHUNK
    H = RET_HEADS
    log_gamma = jnp.log1p(-jnp.power(2.0, -5.0 - jnp.arange(H, dtype=F32)))
    i = jnp.arange(C, dtype=F32)
    rel = i[:, None] - i[None, :]
    decay = jnp.where(rel >= 0, jnp.exp(jnp.maximum(rel, 0.0)[None] * log_gamma[:, None, None]), 0.0)
    xi = jnp.exp((i + 1.0)[:, None] * log_gamma[None, :])
    zeta = jnp.exp((C - 1.0 - i)[:, None] * log_gamma[None, :])
    gamma_c = jnp.exp(C * log_gamma)
    per_pair = lambda t: jnp.repeat(t.T.reshape(H // 2, 2, -1), HEAD_DIM, axis=1).transpose(0, 2, 1)
    gammac = jnp.repeat(gamma_c.reshape(H // 2, 2), HEAD_DIM, axis=1)[:, None, :]
    return cos_t, sin_t, decay, per_pair(xi), per_pair(zeta), gammac


def _retention(proj3, gain):
    B, S, _ = proj3.shape
    cos_t, sin_t, decay, xi, zeta, gammac = _retention_tables(S)
    avg = jnp.asarray(_block_avg(LANES), BF16)
    col = lambda c: pl.BlockSpec((1, S, RET_WIDTH), lambda b: (b, 0, c // RET_WIDTH))
    full = lambda a: pl.BlockSpec(a.shape, lambda b: (0,) * a.ndim)
    return pl.pallas_call(
        _retention_body,
        grid=(B,),
        in_specs=[col(COL_QR), col(COL_KR), col(COL_VR), col(COL_GR), full(cos_t), full(sin_t), full(decay),
                  full(xi), full(zeta), full(gammac), full(gain), full(avg)],
        out_specs=pl.BlockSpec((1, S, RET_WIDTH), lambda b: (b, 0, 0)),
        out_shape=jax.ShapeDtypeStruct((B, S, RET_WIDTH), BF16),
        scratch_shapes=[pltpu.VMEM((RET_HEADS // 2, LANES, LANES), F32)],
        compiler_params=_cparams(1),
        name="retention",
    )(proj3, proj3, proj3, proj3, cos_t, sin_t, decay, xi, zeta, gammac, gain, avg)


def _mem_prep_body(mem_ref, g_ref, wkv_ref, kgain_ref, avg_ref, k_ref, v_ref):
    hm = _rms_full(mem_ref[0], g_ref[...]).astype(BF16)
    kv = _dot(hm, wkv_ref[...])
    k = kv[:, :MEM_WIDTH]
    ms = _seg_mean(k * k, avg_ref[...])
    k_ref[0] = (k * lax.rsqrt(ms + EPS) * kgain_ref[...]).astype(BF16)
    v_ref[0] = kv[:, MEM_WIDTH:].astype(BF16)


def _mem_prep(mem, g, wkv, kgain):
    B, M, _ = mem.shape
    avg = jnp.asarray(_block_avg(MEM_WIDTH), BF16)
    full = lambda a: pl.BlockSpec(a.shape, lambda b: (0,) * a.ndim)
    out_spec = pl.BlockSpec((1, M, MEM_WIDTH), lambda b: (b, 0, 0))
    out_shape = jax.ShapeDtypeStruct((B, M, MEM_WIDTH), BF16)
    return pl.pallas_call(
        _mem_prep_body,
        grid=(B,),
        in_specs=[pl.BlockSpec((1, M, D_MODEL), lambda b: (b, 0, 0)), full(g), full(wkv), full(kgain), full(avg)],
        out_specs=[out_spec, out_spec],
        out_shape=[out_shape, out_shape],
        compiler_params=_cparams(1),
        name="mem_prep",
    )(mem, g, wkv, kgain, avg)


def _post_body(x_ref, oa_ref, ob_ref, wout_ref, mk_ref, mv_ref, gx_ref, wq_ref, qgain_ref, avg_ref, wo_ref,
               gf_ref, wr_ref, br_ref, tri_ref, x2_ref, h_ref, route_ref, count_ref):
    tm = x_ref.shape[1] // POST_CHAINS
    chains = range(POST_CHAINS)
    rows = [slice(c * tm, (c + 1) * tm) for c in chains]
    x1 = [x_ref[0, rows[c]] + _dot(oa_ref[0, rows[c]], wout_ref[0:NSA_WIDTH, :])
          + _dot(ob_ref[0, rows[c]], wout_ref[NSA_WIDTH:, :]) for c in chains]

    h = [_rms_full(x1[c], gx_ref[...]).astype(BF16) for c in chains]
    q = [_dot(h[c], wq_ref[...]) for c in chains]
    ms = [_seg_mean(q[c] * q[c], avg_ref[...]) for c in chains]
    q = [q[c] * lax.rsqrt(ms[c] + EPS) * qgain_ref[...] * (HEAD_DIM ** -0.5) for c in chains]
    lane = lax.broadcasted_iota(jnp.int32, (tm, LANES), 1)
    lo = lane < HEAD_DIM
    heads = [(c, p, half) for c in chains for p in range(MEM_HEADS // 2) for half in range(2)]
    s = [_dot_nt(jnp.where(lo if half == 0 else ~lo, q[c][:, p * LANES:(p + 1) * LANES], 0.0).astype(BF16),
                 mk_ref[0, :, p * LANES:(p + 1) * LANES]) for c, p, half in heads]
    e = [jnp.exp(s[i] - jnp.max(s[i], axis=-1, keepdims=True)) for i in range(len(heads))]
    pr = [(e[i] / jnp.sum(e[i], axis=-1, keepdims=True)).astype(BF16) for i in range(len(heads))]
    outs = [_dot(pr[i], mv_ref[0, :, heads[i][1] * LANES:(heads[i][1] + 1) * LANES]) for i in range(len(heads))]
    per_chain = MEM_HEADS
    o = [jnp.concatenate([jnp.where(lo, outs[c * per_chain + 2 * p], outs[c * per_chain + 2 * p + 1])
                          for p in range(MEM_HEADS // 2)], axis=1).astype(BF16) for c in chains]
    x2 = [x1[c] + _dot(o[c], wo_ref[...]) for c in chains]
    for c in chains:
        x2_ref[0, rows[c]] = x2[c]

    hf = [_rms_full(x2[c], gf_ref[...]).astype(BF16) for c in chains]
    for c in chains:
        h_ref[0, rows[c]] = hf[c]
    logits = [_dot(hf[c], wr_ref[...]) + br_ref[...] for c in chains]
    lane_f = lane.astype(F32)
    big = float(LANES)
    picks = []
    for c in chains:
        gl = jnp.where(lane < N_GROUPS, logits[c], BELOW_ALL)
        gmax = jnp.max(gl, axis=-1, keepdims=True)
        grp = jnp.min(jnp.where(gl == gmax, lane_f, big), axis=-1, keepdims=True)
        g_w = 1.0 / jnp.sum(jnp.where(lane < N_GROUPS, jnp.exp(gl - gmax), 0.0), axis=-1, keepdims=True)
        e_lo = N_GROUPS + grp * EXPERTS_PER_GROUP
        el = jnp.where((lane_f >= e_lo) & (lane_f < e_lo + EXPERTS_PER_GROUP), logits[c], BELOW_ALL)
        v0 = jnp.max(el, axis=-1, keepdims=True)
        i0 = jnp.min(jnp.where(el == v0, lane_f, big), axis=-1, keepdims=True)
        el = jnp.where(lane_f == i0, BELOW_ALL, el)
        v1 = jnp.max(el, axis=-1, keepdims=True)
        i1 = jnp.min(jnp.where(el == v1, lane_f, big), axis=-1, keepdims=True)
        e1 = jnp.exp(v1 - v0)
        picks.append((i0 - N_GROUPS, i1 - N_GROUPS, g_w / (1.0 + e1), g_w * e1 / (1.0 + e1)))

    hot = [[lane_f == picks[c][s] for s in range(2)] for c in chains]
    both = jnp.concatenate([jnp.where(hot[c][0], 1.0, 0.0) + jnp.where(hot[c][1], 1.0, 0.0) for c in chains], axis=0)
    before = _dot(tri_ref[...], both.astype(BF16))
    count_ref[0] = jnp.broadcast_to(jnp.sum(both, axis=0, keepdims=True), count_ref.shape[1:])
    for c in chains:
        e0, e1, w0, w1 = picks[c]
        r0 = jnp.sum(jnp.where(hot[c][0], before[rows[c]], 0.0), axis=-1, keepdims=True)
        r1 = jnp.sum(jnp.where(hot[c][1], before[rows[c]], 0.0), axis=-1, keepdims=True)
        cols = (e0, e1, w0, w1, r0, r1)
        route = jnp.zeros((tm, LANES), F32)
        for k in range(len(cols)):
            route = jnp.where(lane == k, cols[k], route)
        route_ref[0, rows[c]] = route


def _post(x, oa, ob, wout, mk, mv, gx, wq, qgain, wo, gf, wr, br):
    B, S, _ = x.shape
    tm = MOE_TM
    n_s = S // tm
    avg = jnp.asarray(_block_avg(MEM_WIDTH), BF16)
    tri = jnp.asarray(np.tril(np.ones((tm, tm), np.float32), -1), BF16)
    full = lambda a: pl.BlockSpec(a.shape, lambda b, i: (0,) * a.ndim)
    per_b = lambda a: pl.BlockSpec((1,) + a.shape[1:], lambda b, i: (b,) + (0,) * (a.ndim - 1))
    tile = lambda w: pl.BlockSpec((1, tm, w), lambda b, i: (b, i, 0))
    return pl.pallas_call(
        _post_body,
        grid=(B, n_s),
        in_specs=[tile(D_MODEL), tile(NSA_WIDTH), tile(RET_WIDTH), full(wout), per_b(mk), per_b(mv), full(gx),
                  full(wq), full(qgain), full(avg), full(wo), full(gf), full(wr), full(br), full(tri)],
        out_specs=[tile(D_MODEL), tile(D_MODEL), tile(LANES),
                   pl.BlockSpec((1, 8, LANES), lambda b, i: (b * n_s + i, 0, 0))],
        out_shape=[jax.ShapeDtypeStruct((B, S, D_MODEL), F32), jax.ShapeDtypeStruct((B, S, D_MODEL), BF16),
                   jax.ShapeDtypeStruct((B, S, LANES), F32), jax.ShapeDtypeStruct((B * n_s, 8, LANES), F32)],
        compiler_params=_cparams(2),
        name="post_mixer",
    )(x, oa, ob, wout, mk, mv, gx, wq, qgain, avg, wo, gf, wr, br, tri)


def _row_copy(src, dst, sem):
    return pltpu.make_async_copy(src, dst, sem)


def _run_pieces(n, max_piece, fn):
    b = RUN_ALIGN
    while b <= max_piece:
        pl.when((n & b) != 0)(functools.partial(fn, n & (-2 * b), b))
        b *= 2


def _move_groups(i, gmap_ref, n_loc, copy):
    n_groups = n_loc // RUN_ALIGN
    for j in range(n_groups):
        glob = pl.multiple_of(gmap_ref[i * n_groups + j], RUN_ALIGN)
        copy(pl.ds(j * RUN_ALIGN, RUN_ALIGN), pl.ds(glob, RUN_ALIGN)).start()


def _local_positions(route, loff_row):
    lane = lax.broadcasted_iota(jnp.int32, route.shape, 1).astype(F32)
    pos = []
    for s in range(2):
        base = jnp.sum(jnp.where(lane == route[:, s:s + 1], loff_row, 0.0), axis=-1, keepdims=True)
        pos.append(base + route[:, 4 + s:5 + s])
    return pos


def _scatter_body(gmap_ref, tstart_ref, tlen_ref, nact_ref,
                  h_ref, route_ref, lofff_ref, xs_ref, xloc, zbuf, sems):
    i = pl.program_id(0)
    tm = h_ref.shape[0]
    n_loc = xloc.shape[1]
    slot = i & 1
    sem = sems.at[0]

    def tails(start):
        def per_expert(e, carry):
            n = tlen_ref[e]
            st = tstart_ref[e]

            def piece(off, size):
                c = _row_copy(zbuf.at[pl.ds(0, size)], xs_ref.at[pl.ds(pl.multiple_of(st + off, RUN_ALIGN), size)], sem)
                c.start() if start else c.wait()

            _run_pieces(n, MOE_RB // 2, piece)
            return carry

        lax.fori_loop(0, N_EXPERTS, per_expert, 0)

    def unused(start):
        rows = zbuf.shape[0]

        def per_unit(u, carry):
            c = _row_copy(zbuf, xs_ref.at[pl.ds(pl.multiple_of(u * rows, rows), rows)], sem)
            c.start() if start else c.wait()
            return carry

        lax.fori_loop(nact_ref[0] * (MOE_RB // rows), xs_ref.shape[0] // rows, per_unit, 0)

    @pl.when(i == 0)
    def _():
        zbuf[...] = jnp.zeros(zbuf.shape, zbuf.dtype)
        tails(True)
        unused(True)
        tails(False)
        unused(False)

    pos = _local_positions(route_ref[...], lofff_ref[0, 0:1, :])
    col = lax.broadcasted_iota(jnp.int32, (tm, n_loc), 1).astype(F32)
    perm_t = jnp.where((col == pos[0]) | (col == pos[1]), 1.0, 0.0).astype(BF16)
    xloc[slot] = _dot_tn(perm_t, h_ref[...]).astype(BF16)

    def wait_slot(s):
        _row_copy(xloc.at[s], xs_ref.at[pl.ds(0, n_loc)], sems.at[s]).wait()

    pl.when(i > 0)(lambda: wait_slot(1 - slot))
    _move_groups(i, gmap_ref, n_loc, lambda loc, glob: _row_copy(xloc.at[slot, loc], xs_ref.at[glob], sems.at[slot]))
    pl.when(i == pl.num_programs(0) - 1)(lambda: wait_slot(slot))


def _scatter_rows(tables, h2d, route2d, loff_f, n_rows):
    T = h2d.shape[0]
    tm = MOE_TM
    n_loc = MOE_NLOC
    tile = lambda w: pl.BlockSpec((tm, w), lambda i, *_: (i, 0))
    grid_spec = pltpu.PrefetchScalarGridSpec(
        num_scalar_prefetch=4,
        grid=(T // tm,),
        in_specs=[tile(D_MODEL), tile(LANES), pl.BlockSpec((1, 8, LANES), lambda i, *_: (i, 0, 0))],
        out_specs=pl.BlockSpec(memory_space=pl.ANY),
        scratch_shapes=[pltpu.VMEM((2, n_loc, D_MODEL), BF16), pltpu.VMEM((MOE_RB // 2, D_MODEL), BF16),
                        pltpu.SemaphoreType.DMA((2,))],
    )
    return pl.pallas_call(
        _scatter_body,
        grid_spec=grid_spec,
        out_shape=jax.ShapeDtypeStruct((n_rows, D_MODEL), BF16),
        compiler_params=_cparams(1),
        name="moe_scatter",
    )(*tables, h2d, route2d, loff_f)


def _expert_body(blk0_ref, nblk_ref, n_act_ref, xs_ref, wg_ref, wu_ref, wd_ref, ys_ref,
                 wg_b, wu_b, wd_b, xbuf, ybuf, sem_in, sem_out):
    e = pl.program_id(0)
    n_slots, rb = xbuf.shape[0], xbuf.shape[1]
    ahead = n_slots - 2
    n_act = n_act_ref[0]
    b0 = blk0_ref[e]

    def rows(g):
        return pl.ds(pl.multiple_of(g * rb, rb), rb)

    def x_copy(g, slot):
        return _row_copy(xs_ref.at[rows(g)], xbuf.at[slot], sem_in.at[slot])

    def y_copy(g, slot):
        return _row_copy(ybuf.at[slot], ys_ref.at[rows(g)], sem_out.at[slot])

    @pl.when(e == 0)
    def _():
        for k in range(ahead):
            pl.when(k < n_act)(lambda k=k: x_copy(k, k).start())

    wg_b[...] = wg_ref[0].astype(BF16)
    wu_b[...] = wu_ref[0].astype(BF16)
    wd_b[...] = wd_ref[0].astype(BF16)

    def blocks(g0, count):
        gs = [g0 + c for c in range(count)]
        slots = [g & (n_slots - 1) for g in gs]
        for c in range(count):
            x_copy(gs[c], slots[c]).wait()
        for c in range(count):
            nxt = gs[c] + ahead
            pl.when(nxt < n_act)(lambda nxt=nxt: x_copy(nxt, nxt & (n_slots - 1)).start())
        x = [xbuf[slots[c]] for c in range(count)]
        a = [_dot(x[c], wg_b[...]) for c in range(count)]
        b = [_dot(x[c], wu_b[...]) for c in range(count)]
        h = [(_silu(a[c]) * b[c]).astype(BF16) for c in range(count)]
        y = [_dot(h[c], wd_b[...]).astype(BF16) for c in range(count)]
        for c in range(count):
            pl.when(gs[c] >= n_slots)(lambda c=c: y_copy(gs[c] - n_slots, slots[c]).wait())
        for c in range(count):
            ybuf[slots[c]] = y[c]
            y_copy(gs[c], slots[c]).start()

    nb = nblk_ref[e]

    def pair(j, carry):
        blocks(b0 + 2 * j, 2)
        return carry

    lax.fori_loop(0, lax.shift_right_logical(nb, 1), pair, 0)
    pl.when((nb & 1) == 1)(lambda: blocks(b0 + nb - 1, 1))

    @pl.when(e == pl.num_programs(0) - 1)
    def _():
        for k in range(1, n_slots + 1):
            pl.when(n_act >= k)(lambda k=k: y_copy(n_act - k, (n_act - k) & (n_slots - 1)).wait())
        ybuf[0] = jnp.zeros(ybuf.shape[1:], ybuf.dtype)
        n_blocks = ys_ref.shape[0] // rb

        def fill(start):
            def per_block(g, carry):
                c = y_copy(g, 0)
                c.start() if start else c.wait()
                return carry

            lax.fori_loop(n_act, n_blocks, per_block, 0)

        fill(True)
        fill(False)


def _experts(blk0, nblk, n_act, xs, n_rows, wg, wu, wd):
    rb = MOE_RB
    weight = lambda a: pl.BlockSpec((1,) + a.shape[1:], lambda e, *_: (e, 0, 0))
    grid_spec = pltpu.PrefetchScalarGridSpec(
        num_scalar_prefetch=3,
        grid=(N_EXPERTS,),
        in_specs=[pl.BlockSpec(memory_space=pl.ANY), weight(wg), weight(wu), weight(wd)],
        out_specs=pl.BlockSpec(memory_space=pl.ANY),
        scratch_shapes=[pltpu.VMEM((D_MODEL, EXPERT_FF), BF16), pltpu.VMEM((D_MODEL, EXPERT_FF), BF16),
                        pltpu.VMEM((EXPERT_FF, D_MODEL), BF16), pltpu.VMEM((MOE_SLOTS, rb, D_MODEL), BF16),
                        pltpu.VMEM((MOE_SLOTS, rb, D_MODEL), BF16), pltpu.SemaphoreType.DMA((MOE_SLOTS,)),
                        pltpu.SemaphoreType.DMA((MOE_SLOTS,))],
    )
    return pl.pallas_call(
        _expert_body,
        grid_spec=grid_spec,
        out_shape=jax.ShapeDtypeStruct((n_rows, D_MODEL), BF16),
        compiler_params=_cparams(1),
        name="moe_experts",
    )(blk0, nblk, n_act, xs, wg, wu, wd)


def _combine_body(gmap_ref, x_ref, route_ref, lofff_ref, ys_ref, o_ref, yloc, sems):
    i = pl.program_id(0)
    tm = x_ref.shape[0]
    n_loc = yloc.shape[1]
    slot = i & 1

    def fetch(tile, s):
        _move_groups(tile, gmap_ref, n_loc, lambda loc, glob: _row_copy(ys_ref.at[glob], yloc.at[s, loc], sems.at[s]))

    pl.when(i == 0)(lambda: fetch(i, slot))
    _row_copy(ys_ref.at[pl.ds(0, n_loc)], yloc.at[slot], sems.at[slot]).wait()
    pl.when(i + 1 < pl.num_programs(0))(lambda: fetch(i + 1, 1 - slot))

    route = route_ref[...]
    pos = _local_positions(route, lofff_ref[0, 0:1, :])
    col = lax.broadcasted_iota(jnp.int32, (tm, n_loc), 1).astype(F32)
    perm_w = jnp.where(col == pos[0], route[:, 2:3], jnp.where(col == pos[1], route[:, 3:4], 0.0)).astype(BF16)
    o_ref[...] = x_ref[...] + _dot(perm_w, yloc[slot])


def _combine(tables, x2d, route2d, loff_f, ys):
    T = x2d.shape[0]
    tm = MOE_TM
    n_loc = MOE_NLOC
    tile = lambda w: pl.BlockSpec((tm, w), lambda i, *_: (i, 0))
    grid_spec = pltpu.PrefetchScalarGridSpec(
        num_scalar_prefetch=1,
        grid=(T // tm,),
        in_specs=[tile(D_MODEL), tile(LANES), pl.BlockSpec((1, 8, LANES), lambda i, *_: (i, 0, 0)),
                  pl.BlockSpec(memory_space=pl.ANY)],
        out_specs=tile(D_MODEL),
        scratch_shapes=[pltpu.VMEM((2, n_loc, D_MODEL), BF16), pltpu.SemaphoreType.DMA((2,))],
    )
    return pl.pallas_call(
        _combine_body,
        grid_spec=grid_spec,
        out_shape=jax.ShapeDtypeStruct((T, D_MODEL), F32),
        compiler_params=_cparams(1),
        name="moe_combine",
    )(*tables, x2d, route2d, loff_f, ys)


def _compress_weights(pos, w2):
    eye = jnp.eye(NSA_GROUPS, dtype=F32)
    w2b = jnp.einsum('hd,gk->ghkd', w2, eye).reshape(NSA_GROUPS * CMP_HIDDEN, LANES)
    posb = jnp.tile(pos, (1, NSA_GROUPS)).reshape(2, 1, CMP_STRIDE * LANES)
    return posb, w2b.astype(BF16)


def _dup2(g):
    return jnp.tile(g.reshape(1, HEAD_DIM), (1, 2))


def kernel(x, mem, mix_norm, w_in, nsa_q_norm, nsa_kcmp_norm, nsa_ksel_norm, nsa_kwin_norm, cmp_pos_k, cmp_pos_v, cmp_k_w1, cmp_k_w2, cmp_v_w1, cmp_v_w2, nsa_out_norm, ret_out_norm, w_out, mem_x_norm, mem_kv_norm, mem_wq, mem_wkv, mem_q_norm, mem_k_norm, mem_wo, ffn_norm, router_group_w, router_group_b, router_expert_w, router_expert_b, exp_w_gate, exp_w_up, exp_w_down):
    B, S, D = x.shape
    T = B * S
    depth = mix_norm.shape[0]
    for l in range(depth):
        proj = _proj(x.reshape(T, D), mix_norm[l].reshape(1, D), w_in[l].T).reshape(B, S, PROJ_PAD)
        pk, w2k = _compress_weights(cmp_pos_k[l], cmp_k_w2[l])
        pv, w2v = _compress_weights(cmp_pos_v[l], cmp_v_w2[l])
        gains = jnp.stack([_dup2(nsa_kcmp_norm[l]), _dup2(nsa_ksel_norm[l]), _dup2(nsa_kwin_norm[l])])
        kcmp, vcmp, ks, vs, kw, vw = _nsa_prep(proj, jnp.stack([pk, pv]), cmp_k_w1[l], cmp_v_w1[l],
                                               jnp.stack([w2k, w2v]), gains)
        o_a = _nsa_attn(proj, kcmp, vcmp, ks, vs, kw, vw,
                        jnp.tile(nsa_q_norm[l].reshape(1, HEAD_DIM), (1, NSA_HEADS)), gains,
                        nsa_out_norm[l].reshape(1, NSA_WIDTH))
        o_b = _retention(proj, ret_out_norm[l].reshape(1, RET_WIDTH))
        mk, mv = _mem_prep(mem, mem_kv_norm[l].reshape(1, D), mem_wkv[l].astype(BF16),
                           jnp.tile(mem_k_norm[l].reshape(1, HEAD_DIM), (1, MEM_HEADS)))
        w_r = jnp.concatenate([router_group_w[l],
                               router_expert_w[l].transpose(1, 0, 2).reshape(D, N_EXPERTS),
                               jnp.zeros((D, LANES - N_GROUPS - N_EXPERTS), F32)], axis=1).astype(BF16)
        b_r = jnp.concatenate([router_group_b[l], router_expert_b[l].reshape(N_EXPERTS),
                               jnp.zeros((LANES - N_GROUPS - N_EXPERTS,), F32)]).reshape(1, LANES)
        x2, hf, route, counts = _post(
            x, o_a, o_b, w_out[l].astype(BF16), mk, mv, mem_x_norm[l].reshape(1, D), mem_wq[l].astype(BF16),
            jnp.tile(mem_q_norm[l].reshape(1, HEAD_DIM), (1, MEM_HEADS)), mem_wo[l].astype(BF16),
            ffn_norm[l].reshape(1, D), w_r, b_r)
        route2d = route.reshape(T, LANES)
        n_tiles = T // MOE_TM
        cnt = counts[:, 0, :N_EXPERTS].astype(jnp.int32)
        cnt = (cnt + RUN_ALIGN - 1) // RUN_ALIGN * RUN_ALIGN
        loff = jnp.cumsum(cnt, axis=1) - cnt
        total = jnp.sum(cnt, axis=0)
        padded = (total + MOE_RB - 1) // MOE_RB * MOE_RB
        pend = jnp.cumsum(padded)
        pstart = pend - padded
        goff = pstart[None, :] + jnp.cumsum(cnt, axis=0) - cnt
        n_rows = 2 * T + n_tiles * N_EXPERTS * RUN_ALIGN + N_EXPERTS * MOE_RB
        n_act = (pend[-1:] // MOE_RB).astype(jnp.int32)
        loff_f = jnp.broadcast_to(jnp.pad(loff.astype(F32), ((0, 0), (0, LANES - N_EXPERTS)))[:, None, :],
                                  (n_tiles, 8, LANES))
        grp_row = jnp.arange(MOE_NLOC // RUN_ALIGN, dtype=jnp.int32) * RUN_ALIGN
        inside = ((loff[:, None, :] <= grp_row[None, :, None])
                  & (grp_row[None, :, None] < (loff + cnt)[:, None, :])).astype(jnp.int32)
        shift = jnp.sum(inside * (goff - loff)[:, None, :], axis=2)
        used = jnp.sum(inside, axis=2) > 0
        gmap_scatter = jnp.where(used, shift + grp_row[None, :], n_rows + grp_row[None, :]).reshape(-1)
        gmap_gather = jnp.where(used, shift + grp_row[None, :], 0).reshape(-1)
        xs = _scatter_rows((gmap_scatter, pstart + total, padded - total, n_act), hf.reshape(T, D), route2d, loff_f,
                           n_rows + MOE_NLOC)
        ys = _experts(pstart // MOE_RB, padded // MOE_RB, n_act, xs, n_rows,
                      exp_w_gate[l], exp_w_up[l], exp_w_down[l])
        x = _combine((gmap_gather,), x2.reshape(T, D), route2d, loff_f, ys).reshape(B, S, D)
    return x
```

```python
import functools

import numpy as np
import jax
import jax.numpy as jnp
from jax import lax
from jax.experimental import pallas as pl
from jax.experimental.pallas import tpu as pltpu

F32 = jnp.float32
BF16 = jnp.bfloat16

D_MODEL = 1024
HEAD_DIM = 64
LANES = 128
NSA_HEADS = 8
NSA_GROUPS = 2
NSA_WIDTH = NSA_HEADS * HEAD_DIM
CMP_BLOCK = 32
CMP_STRIDE = 16
CMP_HIDDEN = 2 * HEAD_DIM
SEL_BLOCK = 64
SEL_TOPK = 8
WINDOW = 512
RET_HEADS = 8
RET_WIDTH = RET_HEADS * HEAD_DIM
RET_CHUNK = 128
ROPE_BASE = 10000.0
MEM_HEADS = 4
MEM_WIDTH = MEM_HEADS * HEAD_DIM
N_GROUPS = 4
EXPERTS_PER_GROUP = 8
N_EXPERTS = N_GROUPS * EXPERTS_PER_GROUP
EXPERT_FF = D_MODEL // 4
EPS = 1e-6
NEG_INF = -1e30
BELOW_ALL = -3e38
MAX_FIXED_SHIFT = 40.0

COL_QA = 0
COL_QR, COL_KR, COL_VR, COL_GR = 512, 1024, 1536, 2048
COL_KVC, COL_KSV, COL_KWV = 2560, 2816, 3072
COL_GATE = 3328
PROJ_PAD = 3456
PROJ_SPLITS = (NSA_WIDTH, NSA_WIDTH + 6 * NSA_GROUPS * HEAD_DIM, NSA_WIDTH + 6 * NSA_GROUPS * HEAD_DIM + 3 * NSA_HEADS)

PROJ_TM = 512
RET_UNROLL = 4
NSA_TQ = 256
SEL_KC = 512
POST_CHAINS = 2
MOE_TM = 512
MOE_RB = 256
MOE_SLOTS = 8
RUN_ALIGN = 16
MOE_NLOC = 2 * MOE_TM + N_EXPERTS * RUN_ALIGN
VMEM_LIMIT = 56 * 1024 * 1024


def _cparams(n_axes):
    return pltpu.CompilerParams(dimension_semantics=("arbitrary",) * n_axes,
                                vmem_limit_bytes=VMEM_LIMIT)


def _dot(a, b):
    return jnp.dot(a, b, preferred_element_type=F32)


def _dot_nt(a, b):
    return lax.dot_general(a, b, (((1,), (1,)), ((), ())), preferred_element_type=F32)


def _dot_tn(a, b):
    return lax.dot_general(a, b, (((0,), (0,)), ((), ())), preferred_element_type=F32)


def _rms_full(x, g):
    ms = jnp.mean(x * x, axis=-1, keepdims=True)
    return x * lax.rsqrt(ms + EPS) * g


def _seg_mean(x, avg):
    return _dot(x.astype(BF16), avg)


def _silu(x):
    return x * (1.0 / (1.0 + jnp.exp(-x)))


def _sigmoid(x):
    return 1.0 / (1.0 + jnp.exp(-x))


def _block_avg(width):
    i = np.arange(width)
    return ((i[:, None] // HEAD_DIM == i[None, :] // HEAD_DIM) / HEAD_DIM).astype(np.float32)


def _proj_body(x_ref, g_ref, wt_ref, o_ref, w_scr):
    @pl.when(pl.program_id(0) == 0)
    def _():
        kv0, gate0, ret0 = PROJ_SPLITS
        w_scr[COL_QA:COL_QR] = wt_ref[0:kv0].astype(BF16)
        w_scr[COL_QR:COL_KVC] = wt_ref[ret0:].astype(BF16)
        w_scr[COL_KVC:COL_GATE] = wt_ref[kv0:gate0].astype(BF16)
        pad = jnp.zeros((PROJ_PAD - COL_GATE - (ret0 - gate0), wt_ref.shape[1]), F32)
        w_scr[COL_GATE:] = jnp.concatenate([wt_ref[gate0:ret0], pad], axis=0).astype(BF16)

    h = _rms_full(x_ref[...], g_ref[...]).astype(BF16)
    step = PROJ_PAD // 3
    for j in range(3):
        o_ref[:, j * step:(j + 1) * step] = _dot_nt(h, w_scr[j * step:(j + 1) * step]).astype(BF16)


def _proj(x2d, g, w_t):
    T = x2d.shape[0]
    assert w_t.shape[0] == PROJ_SPLITS[2] + 4 * RET_WIDTH
    return pl.pallas_call(
        _proj_body,
        grid=(T // PROJ_TM,),
        in_specs=[pl.BlockSpec((PROJ_TM, D_MODEL), lambda i: (i, 0)),
                  pl.BlockSpec((1, D_MODEL), lambda i: (0, 0)),
                  pl.BlockSpec(w_t.shape, lambda i: (0, 0))],
        out_specs=pl.BlockSpec((PROJ_TM, PROJ_PAD), lambda i: (i, 0)),
        out_shape=jax.ShapeDtypeStruct((T, PROJ_PAD), BF16),
        scratch_shapes=[pltpu.VMEM((PROJ_PAD, D_MODEL), BF16)],
        compiler_params=_cparams(1),
        name="proj",
    )(x2d, g, w_t)


def _dup_groups(x):
    lane = lax.broadcasted_iota(jnp.int32, x.shape, 1)
    xs = pltpu.roll(x, HEAD_DIM, axis=1)
    lo = lane < HEAD_DIM
    return jnp.where(lo, x, xs), jnp.where(lo, xs, x)


def _ones_groups(x):
    lane = lax.broadcasted_iota(jnp.int32, x.shape, 1)
    lo = lane < HEAD_DIM
    xs = pltpu.roll(x, HEAD_DIM, axis=1)
    return jnp.where(lo, x, 1.0), jnp.where(lo, 1.0, xs), jnp.where(lo, xs, 1.0), jnp.where(lo, 1.0, x)


def _nsa_prep_body(kvc_ref, ksv_ref, kwv_ref, pos_ref, w1k_ref, w1v_ref, w2_ref, gain_ref, avg_ref,
                   kcmp_ref, vcmp_ref, ks_ref, vs_ref, kw_ref, vw_ref, scr_k, scr_v, w1_ref):
    @pl.when(pl.program_id(0) == 0)
    def _():
        zero = jnp.zeros((HEAD_DIM, CMP_HIDDEN), BF16)
        for j, src in ((0, w1k_ref), (1, w1v_ref)):
            for l in range(CMP_BLOCK):
                piece = src[l * HEAD_DIM:(l + 1) * HEAD_DIM, :].astype(BF16)
                r0 = (l % CMP_STRIDE) * LANES
                w1_ref[j, l // CMP_STRIDE, r0:r0 + HEAD_DIM, :] = jnp.concatenate([piece, zero], axis=1)
                w1_ref[j, l // CMP_STRIDE, r0 + HEAD_DIM:r0 + LANES, :] = jnp.concatenate([zero, piece], axis=1)

    avg = avg_ref[...]
    n_c = scr_k.shape[0] // CMP_STRIDE
    scr_k[...] = kvc_ref[0, :, 0:LANES].astype(F32)
    scr_v[...] = kvc_ref[0, :, LANES:2 * LANES].astype(F32)
    for j, out_ref, scr in ((0, kcmp_ref, scr_k), (1, vcmp_ref, scr_v)):
        ycat = jnp.concatenate(
            [scr[pl.ds(l, n_c, stride=CMP_STRIDE), :] for l in range(CMP_STRIDE)], axis=1)
        first = _dot((ycat + pos_ref[j, 0]).astype(BF16), w1_ref[j, 0])
        second = _dot((ycat + pos_ref[j, 1]).astype(BF16), w1_ref[j, 1])
        hidden = first + pltpu.roll(second, n_c - 1, axis=0)
        cmp_tok = _dot(_silu(hidden).astype(BF16), w2_ref[j])
        if j == 0:
            ms = _seg_mean(cmp_tok * cmp_tok, avg)
            cmp_tok = cmp_tok * lax.rsqrt(ms + EPS) * gain_ref[0]
        for i, piece in enumerate(_dup_groups(cmp_tok) if j == 0 else _ones_groups(cmp_tok)):
            out_ref[0, i] = piece.astype(BF16)

    for src_ref, k_out, v_out, gi in ((ksv_ref, ks_ref, vs_ref, 1), (kwv_ref, kw_ref, vw_ref, 2)):
        k = src_ref[0, :, 0:LANES].astype(F32)
        ms = _seg_mean(k * k, avg)
        k = k * lax.rsqrt(ms + EPS) * gain_ref[gi]
        d0, d1 = _dup_groups(k)
        k_out[0, 0] = d0.astype(BF16)
        k_out[0, 1] = d1.astype(BF16)
        for i, piece in enumerate(_ones_groups(src_ref[0, :, LANES:2 * LANES].astype(F32))):
            v_out[0, i] = piece.astype(BF16)


def _nsa_prep(proj3, pos, w1k, w1v, w2, gains):
    B, S, _ = proj3.shape
    n_c = S // CMP_STRIDE
    avg = jnp.asarray(_block_avg(LANES), BF16)
    col = lambda c: pl.BlockSpec((1, S, 2 * LANES), lambda b: (b, 0, c // (2 * LANES)))
    full = lambda a: pl.BlockSpec(a.shape, lambda b: (0,) * a.ndim)
    cmp_spec = pl.BlockSpec((1, NSA_GROUPS, n_c, LANES), lambda b: (b, 0, 0, 0))
    seq_spec = pl.BlockSpec((1, NSA_GROUPS, S, LANES), lambda b: (b, 0, 0, 0))
    cmp_shape = jax.ShapeDtypeStruct((B, NSA_GROUPS, n_c, LANES), BF16)
    seq_shape = jax.ShapeDtypeStruct((B, NSA_GROUPS, S, LANES), BF16)

    def val(a):
        if isinstance(a, jax.ShapeDtypeStruct):
            return jax.ShapeDtypeStruct((a.shape[0], 2 * a.shape[1]) + a.shape[2:], a.dtype)
        return pl.BlockSpec((1, 2 * NSA_GROUPS) + a.block_shape[2:], lambda b: (b, 0, 0, 0))

    return pl.pallas_call(
        _nsa_prep_body,
        grid=(B,),
        in_specs=[col(COL_KVC), col(COL_KSV), col(COL_KWV), full(pos), full(w1k), full(w1v), full(w2), full(gains),
                  full(avg)],
        out_specs=[cmp_spec, val(cmp_spec), seq_spec, val(seq_spec), seq_spec, val(seq_spec)],
        out_shape=[cmp_shape, val(cmp_shape), seq_shape, val(seq_shape), seq_shape, val(seq_shape)],
        scratch_shapes=[pltpu.VMEM((S, LANES), F32), pltpu.VMEM((S, LANES), F32),
                        pltpu.VMEM((2, 2, CMP_STRIDE * LANES, NSA_GROUPS * CMP_HIDDEN), BF16)],
        compiler_params=_cparams(1),
        name="nsa_prep",
    )(proj3, proj3, proj3, pos, w1k, w1v, w2, gains, avg)


def _nsa_attn_body(q_ref, gate_ref, kcmp_ref, vcmp_ref, ks_ref, vs_ref, kw_ref, vw_ref,
                   qgain_ref, kgain_ref, ogain_ref, avgq_ref, avgo_ref, msct_ref, esel_ref, egate_ref, wbias_ref,
                   dbias_ref, cbias_ref,
                   o_ref, m_scr, acc_scr):
    tq = q_ref.shape[1]
    n_cmp = kcmp_ref.shape[2]
    n_sel = msct_ref.shape[0]
    kc_len = esel_ref.shape[2]
    rows = 4 * tq
    qi = pl.program_id(1)
    q0 = qi * tq

    q = q_ref[0].astype(F32)
    ms = _seg_mean(q * q, avgq_ref[...])
    qn = q * lax.rsqrt(ms + EPS) * qgain_ref[...] * (HEAD_DIM ** -0.5)

    gate_sig = _sigmoid(gate_ref[0].astype(F32)).astype(BF16)
    gates = [_dot(gate_sig, egate_ref[j]) for j in range(3)]

    lane_q = lax.broadcasted_iota(jnp.int32, (tq, LANES), 1)
    lo_q = lane_q < HEAD_DIM

    blk = lax.broadcasted_iota(jnp.int32, (n_sel, tq), 0)
    cur = lax.shift_right_logical(q0 + lax.broadcasted_iota(jnp.int32, (n_sel, tq), 1), int(np.log2(SEL_BLOCK)))
    forced = (blk == 0) | (blk == cur) | (blk == cur - 1)
    future = blk > cur
    blk_f = blk.astype(F32)

    def heads4(x):
        return jnp.concatenate([x] * 4, axis=0)

    def weighted_values(p, v_low, v_high):
        return jnp.concatenate([_dot(p[:2 * tq], v_low), _dot(p[2 * tq:], v_high)], axis=0)

    def normalised_pairs(acc, guard):
        out = []
        for p in range(2):
            low = acc[p * tq:(p + 1) * tq]
            high = acc[(2 + p) * tq:(3 + p) * tq]
            den = pltpu.roll(jnp.where(lo_q, high, low), HEAD_DIM, axis=1)
            if guard:
                den = jnp.maximum(den, 1e-30)
            out.append(jnp.where(lo_q, low, high) / den)
        return out

    groups = range(NSA_GROUPS)
    qs = []
    for g in groups:
        slabs = [qn[:, (2 * g + p) * LANES:(2 * g + p + 1) * LANES] for p in range(2)]
        qs.append(jnp.concatenate(
            [jnp.where(lo_q, slabs[0], 0.0), jnp.where(lo_q, slabs[1], 0.0),
             jnp.where(lo_q, 0.0, slabs[0]), jnp.where(lo_q, 0.0, slabs[1])], axis=0).astype(BF16))

    def compressed_and_select(shift):
        s_c = [_dot_nt(qs[g], kcmp_ref[0, g]) for g in groups]
        if shift is None:
            r_c = lax.broadcasted_iota(jnp.int32, (rows, n_cmp), 0)
            c_c = lax.broadcasted_iota(jnp.int32, (rows, n_cmp), 1)
            cmask = (c_c * CMP_STRIDE + (CMP_BLOCK - 1)) <= q0 + (r_c & (tq - 1))
            s_c = [jnp.where(cmask, s_c[g], NEG_INF) for g in groups]
            e_c = [jnp.where(cmask, jnp.exp(s_c[g] - jnp.max(s_c[g], axis=-1, keepdims=True)), 0.0) for g in groups]
        else:
            bias = heads4(cbias_ref[qi] - shift)
            e_c = [jnp.exp(s_c[g] + bias) for g in groups]
        e_b = [e_c[g].astype(BF16) for g in groups]
        acc_c = [weighted_values(e_b[g], vcmp_ref[0, 2 * g], vcmp_ref[0, 2 * g + 1]) for g in groups]
        ones = jnp.ones((8, n_cmp), BF16)
        imp = []
        for g in groups:
            num = [_dot_nt(msct_ref[...], e_b[g][h * tq:(h + 1) * tq]) for h in range(4)]
            den = [_dot_nt(ones, e_b[g][h * tq:(h + 1) * tq])[0:1] for h in range(4)]
            parts = [num[h] / jnp.maximum(den[h], 1e-30) for h in range(4)]
            imp.append((parts[0] + parts[1]) + (parts[2] + parts[3]))
        v = [jnp.where(forced, BELOW_ALL, jnp.where(future, NEG_INF, imp[g])) for g in groups]
        sel = [jnp.where(forced, 1.0, 0.0) for g in groups]
        for _ in range(SEL_TOPK - 3):
            mx = [jnp.max(v[g], axis=0, keepdims=True) for g in groups]
            first = [jnp.min(jnp.where(v[g] == mx[g], blk_f, float(LANES)), axis=0, keepdims=True) for g in groups]
            pick = [blk_f == first[g] for g in groups]
            sel = [jnp.where(pick[g], 1.0, sel[g]) for g in groups]
            v = [jnp.where(pick[g], BELOW_ALL, v[g]) for g in groups]
        return [normalised_pairs(acc_c[g], True) for g in groups], [sel[g].astype(BF16) for g in groups]

    n_before = lax.shift_right_logical(q0, int(np.log2(kc_len)))
    causal = dbias_ref[qi & (kc_len // tq - 1)]
    w0 = pl.multiple_of(jnp.maximum(q0 - WINDOW, 0), tq)
    n_w = WINDOW + tq
    w_case = jnp.minimum(qi, WINDOW // tq)

    def sel_keys(ref, g, kc):
        return ref[0, g, pl.ds(pl.multiple_of(kc * kc_len, kc_len), kc_len), :]

    def sel_scores(sel_b, g, kc, causal_bias, shift):
        chosen = _dot_tn(sel_b[g], esel_ref[kc])
        bias = (chosen - 1.0) * (-NEG_INF)
        if causal_bias is not None:
            bias = bias + causal_bias
        if shift is not None:
            bias = bias - shift
        return _dot_nt(qs[g], sel_keys(ks_ref, g, kc)) + heads4(bias)

    def win_scores(g, shift):
        bias = wbias_ref[w_case] if shift is None else wbias_ref[w_case] - shift
        return _dot_nt(qs[g], kw_ref[0, g, pl.ds(w0, n_w), :]) + heads4(bias)

    def win_values(g):
        return vw_ref[0, 2 * g, pl.ds(w0, n_w), :], vw_ref[0, 2 * g + 1, pl.ds(w0, n_w), :]

    def sel_values(g, kc):
        return sel_keys(vs_ref, 2 * g, kc), sel_keys(vs_ref, 2 * g + 1, kc)

    def finish(cmp_s, acc_w):
        for g in groups:
            sel_s = normalised_pairs(acc_scr[g], False)
            win_s = normalised_pairs(acc_w[g], False)
            for p in range(2):
                cols = slice((2 * g + p) * LANES, (2 * g + p + 1) * LANES)
                mix = gates[0][:, cols] * cmp_s[g][p] + gates[1][:, cols] * sel_s[p] + gates[2][:, cols] * win_s[p]
                ms_o = _seg_mean(mix * mix, avgo_ref[...])
                o_ref[0, :, cols] = (mix * lax.rsqrt(ms_o + EPS) * ogain_ref[:, cols]).astype(BF16)

    def fixed_shift_path(shift):
        cmp_s, sel_b = compressed_and_select(shift)

        def probs(s):
            return jnp.exp(s).astype(BF16)

        for g in groups:
            acc_scr[g] = jnp.zeros(acc_scr.shape[1:], F32)

        def before(kc, carry):
            s = [sel_scores(sel_b, g, kc, None, shift) for g in groups]
            p = [probs(s[g]) for g in groups]
            for g in groups:
                acc_scr[g] = acc_scr[g] + weighted_values(p[g], *sel_values(g, kc))
            return carry

        lax.fori_loop(0, n_before, before, 0)
        s_d0 = sel_scores(sel_b, 0, n_before, causal, shift)
        s_d1 = sel_scores(sel_b, 1, n_before, causal, shift)
        p_d0 = probs(s_d0)
        s_w0 = win_scores(0, shift)
        acc_scr[0] = acc_scr[0] + weighted_values(p_d0, *sel_values(0, n_before))
        p_d1 = probs(s_d1)
        s_w1 = win_scores(1, shift)
        acc_scr[1] = acc_scr[1] + weighted_values(p_d1, *sel_values(1, n_before))
        p_w0 = probs(s_w0)
        acc_w0 = weighted_values(p_w0, *win_values(0))
        p_w1 = probs(s_w1)
        acc_w1 = weighted_values(p_w1, *win_values(1))
        finish(cmp_s, [acc_w0, acc_w1])

    def online_path():
        cmp_s, sel_b = compressed_and_select(None)
        for g in groups:
            m_scr[g] = jnp.full(m_scr.shape[1:], NEG_INF, F32)
            acc_scr[g] = jnp.zeros(acc_scr.shape[1:], F32)

        def sel_softmax(g, s):
            m_old = m_scr[g]
            m_new = jnp.maximum(m_old, jnp.max(s, axis=-1, keepdims=True))
            m_scr[g] = m_new
            return jnp.exp(s - m_new).astype(BF16), jnp.exp(m_old - m_new)

        def sel_accumulate(g, kc, p, alpha):
            acc_scr[g] = alpha * acc_scr[g] + weighted_values(p, *sel_values(g, kc))

        def win_softmax(s):
            return jnp.exp(s - jnp.max(s, axis=-1, keepdims=True)).astype(BF16)

        def before(kc, carry):
            s = [sel_scores(sel_b, g, kc, None, None) for g in groups]
            pa = [sel_softmax(g, s[g]) for g in groups]
            for g in groups:
                sel_accumulate(g, kc, *pa[g])
            return carry

        lax.fori_loop(0, n_before, before, 0)
        s_d0 = sel_scores(sel_b, 0, n_before, causal, None)
        s_d1 = sel_scores(sel_b, 1, n_before, causal, None)
        pa0 = sel_softmax(0, s_d0)
        s_w0 = win_scores(0, None)
        sel_accumulate(0, n_before, *pa0)
        pa1 = sel_softmax(1, s_d1)
        s_w1 = win_scores(1, None)
        sel_accumulate(1, n_before, *pa1)
        acc_w0 = weighted_values(win_softmax(s_w0), *win_values(0))
        acc_w1 = weighted_values(win_softmax(s_w1), *win_values(1))
        finish(cmp_s, [acc_w0, acc_w1])

    bound = 1.01 * (HEAD_DIM ** 0.5) * jnp.max(jnp.abs(qgain_ref[...])) * jnp.max(jnp.abs(kgain_ref[...]))
    safe = bound <= MAX_FIXED_SHIFT
    pl.when(safe)(lambda: fixed_shift_path(bound))
    pl.when(jnp.logical_not(safe))(online_path)


def _sel_from_cmp(n_cmp, n_sel):
    c0 = np.arange(n_cmp) * CMP_STRIDE
    s0 = np.arange(n_sel) * SEL_BLOCK
    ov = np.minimum(c0[None, :] + CMP_BLOCK, s0[:, None] + SEL_BLOCK) - np.maximum(c0[None, :], s0[:, None])
    m = (np.clip(ov, 0, None) / CMP_BLOCK).astype(np.float32)
    m[:, (np.arange(n_cmp) * CMP_STRIDE + CMP_BLOCK) > n_sel * SEL_BLOCK] = 0.0
    return m


def _nsa_attn(proj3, kcmp, vcmp, ks, vs, kw, vw, q_gain, k_gains, o_gain):
    B, S, _ = proj3.shape
    n_cmp = kcmp.shape[2]
    n_sel = S // SEL_BLOCK
    tq = NSA_TQ
    assert n_sel % 8 == 0 and S % SEL_KC == 0 and SEL_KC % tq == 0 and WINDOW % tq == 0 and S >= WINDOW + tq
    avgq = jnp.asarray(_block_avg(NSA_WIDTH), BF16)
    avgo = jnp.asarray(_block_avg(LANES), BF16)
    msct = jnp.asarray(_sel_from_cmp(n_cmp, n_sel), BF16)
    esel = (np.arange(n_sel)[:, None] == np.arange(S)[None, :] // SEL_BLOCK).astype(np.float32)
    esel = jnp.asarray(esel.reshape(n_sel, S // SEL_KC, SEL_KC).transpose(1, 0, 2), BF16)
    src = np.arange(LANES)[:, None]
    dst = np.arange(NSA_WIDTH)[None, :]
    egate = jnp.asarray(np.stack([(src == (dst // HEAD_DIM) * 3 + j) for j in range(3)]).astype(np.float32), BF16)
    r = np.arange(tq)[:, None]
    n_w = WINDOW + tq
    wcases = []
    for i in range(WINDOW // tq + 1):
        diff = (i * tq - max(i * tq - WINDOW, 0)) + r - np.arange(n_w)[None, :]
        wcases.append(np.where((diff >= 0) & (diff < WINDOW), 0.0, NEG_INF))
    wbias = jnp.asarray(np.stack(wcases), F32)
    dbias = jnp.asarray(np.stack([np.where(np.arange(SEL_KC)[None, :] <= i * tq + r, 0.0, NEG_INF)
                                  for i in range(SEL_KC // tq)]), F32)
    c_end = np.arange(n_cmp)[None, :] * CMP_STRIDE + (CMP_BLOCK - 1)
    cbias = jnp.asarray(np.stack([np.where(c_end <= i * tq + r, 0.0, NEG_INF) for i in range(S // tq)]), F32)

    full = lambda a: pl.BlockSpec(a.shape, lambda b, i: (0,) * a.ndim)
    per_b = lambda a: pl.BlockSpec((1,) + a.shape[1:], lambda b, i: (b,) + (0,) * (a.ndim - 1))
    return pl.pallas_call(
        _nsa_attn_body,
        grid=(B, S // tq),
        in_specs=[pl.BlockSpec((1, tq, NSA_WIDTH), lambda b, i: (b, i, COL_QA // NSA_WIDTH)),
                  pl.BlockSpec((1, tq, LANES), lambda b, i: (b, i, COL_GATE // LANES)),
                  per_b(kcmp), per_b(vcmp), per_b(ks), per_b(vs), per_b(kw), per_b(vw),
                  full(q_gain), full(k_gains), full(o_gain), full(avgq), full(avgo), full(msct), full(esel), full(egate),
                  full(wbias), full(dbias), full(cbias)],
        out_specs=pl.BlockSpec((1, tq, NSA_WIDTH), lambda b, i: (b, i, 0)),
        out_shape=jax.ShapeDtypeStruct((B, S, NSA_WIDTH), BF16),
        scratch_shapes=[pltpu.VMEM((NSA_GROUPS, 4 * tq, 1), F32), pltpu.VMEM((NSA_GROUPS, 4 * tq, LANES), F32)],
        compiler_params=_cparams(2),
        name="nsa_attn",
    )(proj3, proj3, kcmp, vcmp, ks, vs, kw, vw, q_gain, k_gains, o_gain, avgq, avgo, msct, esel, egate, wbias, dbias,
      cbias)


def _retention_body(q_ref, k_ref, v_ref, g_ref, cos_ref, sin_ref, decay_ref, xi_ref, zeta_ref, gammac_ref,
                    gain_ref, avg_ref, o_ref, state_scr):
    S = q_ref.shape[1]
    C = RET_CHUNK
    lane = lax.broadcasted_iota(jnp.int32, (C, LANES), 1)
    lo = lane < HEAD_DIM
    first_half = (lane & (HEAD_DIM - 1)) < HEAD_DIM // 2
    r = lax.broadcasted_iota(jnp.int32, (LANES, LANES), 0)
    c = lax.broadcasted_iota(jnp.int32, (LANES, LANES), 1)
    same_head = (r < HEAD_DIM) == (c < HEAD_DIM)
    avg = avg_ref[...]
    state_scr[...] = jnp.zeros(state_scr.shape, F32)

    def rope(x, cos, sin):
        swapped = jnp.where(first_half, pltpu.roll(x, LANES - HEAD_DIM // 2, axis=1),
                            pltpu.roll(x, HEAD_DIM // 2, axis=1))
        return x * cos + swapped * sin

    n_pairs = RET_HEADS // 2
    cols = [slice(p * LANES, (p + 1) * LANES) for p in range(n_pairs)]
    units = [(u, p) for u in range(RET_UNROLL) for p in range(n_pairs)]

    def chunks(n, carry):
        r0 = [pl.multiple_of((n * RET_UNROLL + u) * C, C) for u in range(RET_UNROLL)]
        cos = [cos_ref[pl.ds(r0[u], C), :] for u in range(RET_UNROLL)]
        sin = [sin_ref[pl.ds(r0[u], C), :] for u in range(RET_UNROLL)]
        q = [rope(q_ref[0, pl.ds(r0[u], C), cols[p]].astype(F32), cos[u], sin[u]) for u, p in units]
        k = [rope(k_ref[0, pl.ds(r0[u], C), cols[p]].astype(F32), cos[u], sin[u]) * (HEAD_DIM ** -0.5)
             for u, p in units]
        vb = [v_ref[0, pl.ds(r0[u], C), cols[p]] for u, p in units]
        kb = [k[i].astype(BF16) for i in range(len(units))]
        inner = [_dot_nt(jnp.where(lo if half == 0 else ~lo, q[i], 0.0).astype(BF16), kb[i])
                 * decay_ref[2 * units[i][1] + half] for i in range(len(units)) for half in range(2)]
        upd = [_dot_tn((k[i] * zeta_ref[units[i][1]]).astype(BF16), vb[i]) for i in range(len(units))]
        state = [state_scr[p] for p in range(n_pairs)]
        for u in range(RET_UNROLL):
            for p in range(n_pairs):
                prev = state[u * n_pairs + p]
                state.append(gammac_ref[p] * prev + jnp.where(same_head, upd[u * n_pairs + p], 0.0))
        cross = [_dot(q[i].astype(BF16), state[i].astype(BF16)) * xi_ref[units[i][1]] for i in range(len(units))]
        outs = [_dot(inner[j].astype(BF16), vb[j // 2]) for j in range(2 * len(units))]
        for p in range(n_pairs):
            state_scr[p] = state[RET_UNROLL * n_pairs + p]
        y = jnp.concatenate([jnp.where(lo, outs[2 * i], outs[2 * i + 1]) + cross[i] for i in range(len(units))],
                            axis=0)
        mu = _seg_mean(y, avg)
        d = y - mu
        var = _seg_mean(d * d, avg)
        yn = d * lax.rsqrt(var + EPS)
        for i, (u, p) in enumerate(units):
            gate = g_ref[0, pl.ds(r0[u], C), cols[p]].astype(F32)
            o_ref[0, pl.ds(r0[u], C), cols[p]] = (_silu(gate) * (yn[i * C:(i + 1) * C] * gain_ref[:, cols[p]])).astype(BF16)
        return carry

    lax.fori_loop(0, S // (C * RET_UNROLL), chunks, 0)


def _retention_tables(S):
    half = HEAD_DIM // 2
    inv_freq = ROPE_BASE ** (-np.arange(half, dtype=np.float64) / half)
    ang = np.arange(S, dtype=np.float64)[:, None] * inv_freq[None, :]
    cos, sin = np.cos(ang), np.sin(ang)
    cos_t = np.tile(cos, (1, 4))
    sin_t = np.tile(np.concatenate([-sin, sin], axis=1), (1, 2))
    C = RET_CHUNK
    H = RET_HEADS
    log_gamma = np.log1p(-np.power(2.0, -5.0 - np.arange(H, dtype=np.float64)))
    i = np.arange(C, dtype=np.float64)
    rel = i[:, None] - i[None, :]
    decay = np.where(rel >= 0, np.exp(np.maximum(rel, 0.0)[None] * log_gamma[:, None, None]), 0.0)
    xi = np.exp((i + 1.0)[:, None] * log_gamma[None, :])
    zeta = np.exp((C - 1.0 - i)[:, None] * log_gamma[None, :])
    gamma_c = np.exp(C * log_gamma)
    per_pair = lambda t: np.repeat(t.T.reshape(H // 2, 2, -1), HEAD_DIM, axis=1).transpose(0, 2, 1)
    gammac = np.repeat(gamma_c.reshape(H // 2, 2), HEAD_DIM, axis=1)[:, None, :]
    return tuple(jnp.asarray(t, F32) for t in (cos_t, sin_t, decay, per_pair(xi), per_pair(zeta), gammac))


def _retention(proj3, gain):
    B, S, _ = proj3.shape
    cos_t, sin_t, decay, xi, zeta, gammac = _retention_tables(S)
    avg = jnp.asarray(_block_avg(LANES), BF16)
    col = lambda c: pl.BlockSpec((1, S, RET_WIDTH), lambda b: (b, 0, c // RET_WIDTH))
    full = lambda a: pl.BlockSpec(a.shape, lambda b: (0,) * a.ndim)
    return pl.pallas_call(
        _retention_body,
        grid=(B,),
        in_specs=[col(COL_QR), col(COL_KR), col(COL_VR), col(COL_GR), full(cos_t), full(sin_t), full(decay),
                  full(xi), full(zeta), full(gammac), full(gain), full(avg)],
        out_specs=pl.BlockSpec((1, S, RET_WIDTH), lambda b: (b, 0, 0)),
        out_shape=jax.ShapeDtypeStruct((B, S, RET_WIDTH), BF16),
        scratch_shapes=[pltpu.VMEM((RET_HEADS // 2, LANES, LANES), F32)],
        compiler_params=_cparams(1),
        name="retention",
    )(proj3, proj3, proj3, proj3, cos_t, sin_t, decay, xi, zeta, gammac, gain, avg)


def _mem_prep_body(mem_ref, g_ref, wkv_ref, kgain_ref, avg_ref, k_ref, v_ref):
    hm = _rms_full(mem_ref[0], g_ref[...]).astype(BF16)
    kv = _dot(hm, wkv_ref[...])
    k = kv[:, :MEM_WIDTH]
    ms = _seg_mean(k * k, avg_ref[...])
    k_ref[0] = (k * lax.rsqrt(ms + EPS) * kgain_ref[...]).astype(BF16)
    v_ref[0] = kv[:, MEM_WIDTH:].astype(BF16)


def _mem_prep(mem, g, wkv, kgain):
    B, M, _ = mem.shape
    avg = jnp.asarray(_block_avg(MEM_WIDTH), BF16)
    full = lambda a: pl.BlockSpec(a.shape, lambda b: (0,) * a.ndim)
    out_spec = pl.BlockSpec((1, M, MEM_WIDTH), lambda b: (b, 0, 0))
    out_shape = jax.ShapeDtypeStruct((B, M, MEM_WIDTH), BF16)
    return pl.pallas_call(
        _mem_prep_body,
        grid=(B,),
        in_specs=[pl.BlockSpec((1, M, D_MODEL), lambda b: (b, 0, 0)), full(g), full(wkv), full(kgain), full(avg)],
        out_specs=[out_spec, out_spec],
        out_shape=[out_shape, out_shape],
        compiler_params=_cparams(1),
        name="mem_prep",
    )(mem, g, wkv, kgain, avg)


def _post_body(x_ref, oa_ref, ob_ref, wout_ref, mk_ref, mv_ref, gx_ref, wq_ref, qgain_ref, avg_ref, wo_ref,
               gf_ref, wr_ref, br_ref, tri_ref, x2_ref, h_ref, route_ref, count_ref):
    tm = x_ref.shape[1] // POST_CHAINS
    chains = range(POST_CHAINS)
    rows = [slice(c * tm, (c + 1) * tm) for c in chains]
    x1 = [x_ref[0, rows[c]] + _dot(oa_ref[0, rows[c]], wout_ref[0:NSA_WIDTH, :])
          + _dot(ob_ref[0, rows[c]], wout_ref[NSA_WIDTH:, :]) for c in chains]

    h = [_rms_full(x1[c], gx_ref[...]).astype(BF16) for c in chains]
    q = [_dot(h[c], wq_ref[...]) for c in chains]
    ms = [_seg_mean(q[c] * q[c], avg_ref[...]) for c in chains]
    q = [q[c] * lax.rsqrt(ms[c] + EPS) * qgain_ref[...] * (HEAD_DIM ** -0.5) for c in chains]
    lane = lax.broadcasted_iota(jnp.int32, (tm, LANES), 1)
    lo = lane < HEAD_DIM
    heads = [(c, p, half) for c in chains for p in range(MEM_HEADS // 2) for half in range(2)]
    s = [_dot_nt(jnp.where(lo if half == 0 else ~lo, q[c][:, p * LANES:(p + 1) * LANES], 0.0).astype(BF16),
                 mk_ref[0, :, p * LANES:(p + 1) * LANES]) for c, p, half in heads]
    e = [jnp.exp(s[i] - jnp.max(s[i], axis=-1, keepdims=True)) for i in range(len(heads))]
    pr = [(e[i] / jnp.sum(e[i], axis=-1, keepdims=True)).astype(BF16) for i in range(len(heads))]
    outs = [_dot(pr[i], mv_ref[0, :, heads[i][1] * LANES:(heads[i][1] + 1) * LANES]) for i in range(len(heads))]
    per_chain = MEM_HEADS
    o = [jnp.concatenate([jnp.where(lo, outs[c * per_chain + 2 * p], outs[c * per_chain + 2 * p + 1])
                          for p in range(MEM_HEADS // 2)], axis=1).astype(BF16) for c in chains]
    x2 = [x1[c] + _dot(o[c], wo_ref[...]) for c in chains]
    for c in chains:
        x2_ref[0, rows[c]] = x2[c]

    hf = [_rms_full(x2[c], gf_ref[...]).astype(BF16) for c in chains]
    for c in chains:
        h_ref[0, rows[c]] = hf[c]
    logits = [_dot(hf[c], wr_ref[...]) + br_ref[...] for c in chains]
    lane_f = lane.astype(F32)
    big = float(LANES)
    picks = []
    for c in chains:
        gl = jnp.where(lane < N_GROUPS, logits[c], BELOW_ALL)
        gmax = jnp.max(gl, axis=-1, keepdims=True)
        grp = jnp.min(jnp.where(gl == gmax, lane_f, big), axis=-1, keepdims=True)
        g_w = 1.0 / jnp.sum(jnp.where(lane < N_GROUPS, jnp.exp(gl - gmax), 0.0), axis=-1, keepdims=True)
        e_lo = N_GROUPS + grp * EXPERTS_PER_GROUP
        el = jnp.where((lane_f >= e_lo) & (lane_f < e_lo + EXPERTS_PER_GROUP), logits[c], BELOW_ALL)
        v0 = jnp.max(el, axis=-1, keepdims=True)
        i0 = jnp.min(jnp.where(el == v0, lane_f, big), axis=-1, keepdims=True)
        el = jnp.where(lane_f == i0, BELOW_ALL, el)
        v1 = jnp.max(el, axis=-1, keepdims=True)
        i1 = jnp.min(jnp.where(el == v1, lane_f, big), axis=-1, keepdims=True)
        e1 = jnp.exp(v1 - v0)
        picks.append((i0 - N_GROUPS, i1 - N_GROUPS, g_w / (1.0 + e1), g_w * e1 / (1.0 + e1)))

    hot = [[lane_f == picks[c][s] for s in range(2)] for c in chains]
    both = jnp.concatenate([jnp.where(hot[c][0], 1.0, 0.0) + jnp.where(hot[c][1], 1.0, 0.0) for c in chains], axis=0)
    before = _dot(tri_ref[...], both.astype(BF16))
    count_ref[0] = jnp.broadcast_to(jnp.sum(both, axis=0, keepdims=True), count_ref.shape[1:])
    for c in chains:
        e0, e1, w0, w1 = picks[c]
        r0 = jnp.sum(jnp.where(hot[c][0], before[rows[c]], 0.0), axis=-1, keepdims=True)
        r1 = jnp.sum(jnp.where(hot[c][1], before[rows[c]], 0.0), axis=-1, keepdims=True)
        cols = (e0, e1, w0, w1, r0, r1)
        route = jnp.zeros((tm, LANES), F32)
        for k in range(len(cols)):
            route = jnp.where(lane == k, cols[k], route)
        route_ref[0, rows[c]] = route


def _post(x, oa, ob, wout, mk, mv, gx, wq, qgain, wo, gf, wr, br):
    B, S, _ = x.shape
    tm = MOE_TM
    n_s = S // tm
    avg = jnp.asarray(_block_avg(MEM_WIDTH), BF16)
    tri = jnp.asarray(np.tril(np.ones((tm, tm), np.float32), -1), BF16)
    full = lambda a: pl.BlockSpec(a.shape, lambda b, i: (0,) * a.ndim)
    per_b = lambda a: pl.BlockSpec((1,) + a.shape[1:], lambda b, i: (b,) + (0,) * (a.ndim - 1))
    tile = lambda w: pl.BlockSpec((1, tm, w), lambda b, i: (b, i, 0))
    return pl.pallas_call(
        _post_body,
        grid=(B, n_s),
        in_specs=[tile(D_MODEL), tile(NSA_WIDTH), tile(RET_WIDTH), full(wout), per_b(mk), per_b(mv), full(gx),
                  full(wq), full(qgain), full(avg), full(wo), full(gf), full(wr), full(br), full(tri)],
        out_specs=[tile(D_MODEL), tile(D_MODEL), tile(LANES),
                   pl.BlockSpec((1, 8, LANES), lambda b, i: (b * n_s + i, 0, 0))],
        out_shape=[jax.ShapeDtypeStruct((B, S, D_MODEL), F32), jax.ShapeDtypeStruct((B, S, D_MODEL), BF16),
                   jax.ShapeDtypeStruct((B, S, LANES), F32), jax.ShapeDtypeStruct((B * n_s, 8, LANES), F32)],
        compiler_params=_cparams(2),
        name="post_mixer",
    )(x, oa, ob, wout, mk, mv, gx, wq, qgain, avg, wo, gf, wr, br, tri)


def _row_copy(src, dst, sem):
    return pltpu.make_async_copy(src, dst, sem)


def _run_pieces(n, max_piece, fn):
    b = RUN_ALIGN
    while b <= max_piece:
        pl.when((n & b) != 0)(functools.partial(fn, n & (-2 * b), b))
        b *= 2


def _move_groups(i, gmap_ref, n_loc, copy):
    n_groups = n_loc // RUN_ALIGN
    for j in range(n_groups):
        glob = pl.multiple_of(gmap_ref[i * n_groups + j], RUN_ALIGN)
        copy(pl.ds(j * RUN_ALIGN, RUN_ALIGN), pl.ds(glob, RUN_ALIGN)).start()


def _local_positions(route, loff_row):
    lane = lax.broadcasted_iota(jnp.int32, route.shape, 1).astype(F32)
    pos = []
    for s in range(2):
        base = jnp.sum(jnp.where(lane == route[:, s:s + 1], loff_row, 0.0), axis=-1, keepdims=True)
        pos.append(base + route[:, 4 + s:5 + s])
    return pos


def _scatter_body(gmap_ref, tstart_ref, tlen_ref, nact_ref,
                  h_ref, route_ref, lofff_ref, xs_ref, xloc, zbuf, sems):
    i = pl.program_id(0)
    tm = h_ref.shape[0]
    n_loc = xloc.shape[1]
    slot = i & 1
    sem = sems.at[0]

    def tails(start):
        def per_expert(e, carry):
            n = tlen_ref[e]
            st = tstart_ref[e]

            def piece(off, size):
                c = _row_copy(zbuf.at[pl.ds(0, size)], xs_ref.at[pl.ds(pl.multiple_of(st + off, RUN_ALIGN), size)], sem)
                c.start() if start else c.wait()

            _run_pieces(n, MOE_RB // 2, piece)
            return carry

        lax.fori_loop(0, N_EXPERTS, per_expert, 0)

    def unused(start):
        rows = zbuf.shape[0]

        def per_unit(u, carry):
            c = _row_copy(zbuf, xs_ref.at[pl.ds(pl.multiple_of(u * rows, rows), rows)], sem)
            c.start() if start else c.wait()
            return carry

        lax.fori_loop(nact_ref[0] * (MOE_RB // rows), xs_ref.shape[0] // rows, per_unit, 0)

    @pl.when(i == 0)
    def _():
        zbuf[...] = jnp.zeros(zbuf.shape, zbuf.dtype)
        tails(True)
        unused(True)
        tails(False)
        unused(False)

    pos = _local_positions(route_ref[...], lofff_ref[0, 0:1, :])
    col = lax.broadcasted_iota(jnp.int32, (tm, n_loc), 1).astype(F32)
    perm_t = jnp.where((col == pos[0]) | (col == pos[1]), 1.0, 0.0).astype(BF16)
    xloc[slot] = _dot_tn(perm_t, h_ref[...]).astype(BF16)

    def wait_slot(s):
        _row_copy(xloc.at[s], xs_ref.at[pl.ds(0, n_loc)], sems.at[s]).wait()

    pl.when(i > 0)(lambda: wait_slot(1 - slot))
    _move_groups(i, gmap_ref, n_loc, lambda loc, glob: _row_copy(xloc.at[slot, loc], xs_ref.at[glob], sems.at[slot]))
    pl.when(i == pl.num_programs(0) - 1)(lambda: wait_slot(slot))


def _scatter_rows(tables, h2d, route2d, loff_f, n_rows):
    T = h2d.shape[0]
    tm = MOE_TM
    n_loc = MOE_NLOC
    tile = lambda w: pl.BlockSpec((tm, w), lambda i, *_: (i, 0))
    grid_spec = pltpu.PrefetchScalarGridSpec(
        num_scalar_prefetch=4,
        grid=(T // tm,),
        in_specs=[tile(D_MODEL), tile(LANES), pl.BlockSpec((1, 8, LANES), lambda i, *_: (i, 0, 0))],
        out_specs=pl.BlockSpec(memory_space=pl.ANY),
        scratch_shapes=[pltpu.VMEM((2, n_loc, D_MODEL), BF16), pltpu.VMEM((MOE_RB // 2, D_MODEL), BF16),
                        pltpu.SemaphoreType.DMA((2,))],
    )
    return pl.pallas_call(
        _scatter_body,
        grid_spec=grid_spec,
        out_shape=jax.ShapeDtypeStruct((n_rows, D_MODEL), BF16),
        compiler_params=_cparams(1),
        name="moe_scatter",
    )(*tables, h2d, route2d, loff_f)


def _expert_body(blk0_ref, nblk_ref, n_act_ref, xs_ref, wg_ref, wu_ref, wd_ref, ys_ref,
                 wg_b, wu_b, wd_b, xbuf, ybuf, sem_in, sem_out):
    e = pl.program_id(0)
    n_slots, rb = xbuf.shape[0], xbuf.shape[1]
    ahead = n_slots - 2
    n_act = n_act_ref[0]
    b0 = blk0_ref[e]

    def rows(g):
        return pl.ds(pl.multiple_of(g * rb, rb), rb)

    def x_copy(g, slot):
        return _row_copy(xs_ref.at[rows(g)], xbuf.at[slot], sem_in.at[slot])

    def y_copy(g, slot):
        return _row_copy(ybuf.at[slot], ys_ref.at[rows(g)], sem_out.at[slot])

    @pl.when(e == 0)
    def _():
        for k in range(ahead):
            pl.when(k < n_act)(lambda k=k: x_copy(k, k).start())

    wg_b[...] = wg_ref[0].astype(BF16)
    wu_b[...] = wu_ref[0].astype(BF16)
    wd_b[...] = wd_ref[0].astype(BF16)

    def blocks(g0, count):
        gs = [g0 + c for c in range(count)]
        slots = [g & (n_slots - 1) for g in gs]
        for c in range(count):
            x_copy(gs[c], slots[c]).wait()
        for c in range(count):
            nxt = gs[c] + ahead
            pl.when(nxt < n_act)(lambda nxt=nxt: x_copy(nxt, nxt & (n_slots - 1)).start())
        x = [xbuf[slots[c]] for c in range(count)]
        a = [_dot(x[c], wg_b[...]) for c in range(count)]
        b = [_dot(x[c], wu_b[...]) for c in range(count)]
        h = [(_silu(a[c]) * b[c]).astype(BF16) for c in range(count)]
        y = [_dot(h[c], wd_b[...]).astype(BF16) for c in range(count)]
        for c in range(count):
            pl.when(gs[c] >= n_slots)(lambda c=c: y_copy(gs[c] - n_slots, slots[c]).wait())
        for c in range(count):
            ybuf[slots[c]] = y[c]
            y_copy(gs[c], slots[c]).start()

    nb = nblk_ref[e]

    def pair(j, carry):
        blocks(b0 + 2 * j, 2)
        return carry

    lax.fori_loop(0, lax.shift_right_logical(nb, 1), pair, 0)
    pl.when((nb & 1) == 1)(lambda: blocks(b0 + nb - 1, 1))

    @pl.when(e == pl.num_programs(0) - 1)
    def _():
        for k in range(1, n_slots + 1):
            pl.when(n_act >= k)(lambda k=k: y_copy(n_act - k, (n_act - k) & (n_slots - 1)).wait())
        ybuf[0] = jnp.zeros(ybuf.shape[1:], ybuf.dtype)
        n_blocks = ys_ref.shape[0] // rb

        def fill(start):
            def per_block(g, carry):
                c = y_copy(g, 0)
                c.start() if start else c.wait()
                return carry

            lax.fori_loop(n_act, n_blocks, per_block, 0)

        fill(True)
        fill(False)


def _experts(blk0, nblk, n_act, xs, n_rows, wg, wu, wd):
    rb = MOE_RB
    weight = lambda a: pl.BlockSpec((1,) + a.shape[1:], lambda e, *_: (e, 0, 0))
    grid_spec = pltpu.PrefetchScalarGridSpec(
        num_scalar_prefetch=3,
        grid=(N_EXPERTS,),
        in_specs=[pl.BlockSpec(memory_space=pl.ANY), weight(wg), weight(wu), weight(wd)],
        out_specs=pl.BlockSpec(memory_space=pl.ANY),
        scratch_shapes=[pltpu.VMEM((D_MODEL, EXPERT_FF), BF16), pltpu.VMEM((D_MODEL, EXPERT_FF), BF16),
                        pltpu.VMEM((EXPERT_FF, D_MODEL), BF16), pltpu.VMEM((MOE_SLOTS, rb, D_MODEL), BF16),
                        pltpu.VMEM((MOE_SLOTS, rb, D_MODEL), BF16), pltpu.SemaphoreType.DMA((MOE_SLOTS,)),
                        pltpu.SemaphoreType.DMA((MOE_SLOTS,))],
    )
    return pl.pallas_call(
        _expert_body,
        grid_spec=grid_spec,
        out_shape=jax.ShapeDtypeStruct((n_rows, D_MODEL), BF16),
        compiler_params=_cparams(1),
        name="moe_experts",
    )(blk0, nblk, n_act, xs, wg, wu, wd)


def _combine_body(gmap_ref, x_ref, route_ref, lofff_ref, ys_ref, o_ref, yloc, sems):
    i = pl.program_id(0)
    tm = x_ref.shape[0]
    n_loc = yloc.shape[1]
    slot = i & 1

    def fetch(tile, s):
        _move_groups(tile, gmap_ref, n_loc, lambda loc, glob: _row_copy(ys_ref.at[glob], yloc.at[s, loc], sems.at[s]))

    pl.when(i == 0)(lambda: fetch(i, slot))
    _row_copy(ys_ref.at[pl.ds(0, n_loc)], yloc.at[slot], sems.at[slot]).wait()
    pl.when(i + 1 < pl.num_programs(0))(lambda: fetch(i + 1, 1 - slot))

    route = route_ref[...]
    pos = _local_positions(route, lofff_ref[0, 0:1, :])
    col = lax.broadcasted_iota(jnp.int32, (tm, n_loc), 1).astype(F32)
    perm_w = jnp.where(col == pos[0], route[:, 2:3], jnp.where(col == pos[1], route[:, 3:4], 0.0)).astype(BF16)
    o_ref[...] = x_ref[...] + _dot(perm_w, yloc[slot])


def _combine(tables, x2d, route2d, loff_f, ys):
    T = x2d.shape[0]
    tm = MOE_TM
    n_loc = MOE_NLOC
    tile = lambda w: pl.BlockSpec((tm, w), lambda i, *_: (i, 0))
    grid_spec = pltpu.PrefetchScalarGridSpec(
        num_scalar_prefetch=1,
        grid=(T // tm,),
        in_specs=[tile(D_MODEL), tile(LANES), pl.BlockSpec((1, 8, LANES), lambda i, *_: (i, 0, 0)),
                  pl.BlockSpec(memory_space=pl.ANY)],
        out_specs=tile(D_MODEL),
        scratch_shapes=[pltpu.VMEM((2, n_loc, D_MODEL), BF16), pltpu.SemaphoreType.DMA((2,))],
    )
    return pl.pallas_call(
        _combine_body,
        grid_spec=grid_spec,
        out_shape=jax.ShapeDtypeStruct((T, D_MODEL), F32),
        compiler_params=_cparams(1),
        name="moe_combine",
    )(*tables, x2d, route2d, loff_f, ys)


def _compress_weights(pos, w2):
    eye = jnp.eye(NSA_GROUPS, dtype=F32)
    w2b = jnp.einsum('hd,gk->ghkd', w2, eye).reshape(NSA_GROUPS * CMP_HIDDEN, LANES)
    posb = jnp.tile(pos, (1, NSA_GROUPS)).reshape(2, 1, CMP_STRIDE * LANES)
    return posb, w2b.astype(BF16)


def _dup2(g):
    return jnp.tile(g.reshape(1, HEAD_DIM), (1, 2))


def kernel(x, mem, mix_norm, w_in, nsa_q_norm, nsa_kcmp_norm, nsa_ksel_norm, nsa_kwin_norm, cmp_pos_k, cmp_pos_v, cmp_k_w1, cmp_k_w2, cmp_v_w1, cmp_v_w2, nsa_out_norm, ret_out_norm, w_out, mem_x_norm, mem_kv_norm, mem_wq, mem_wkv, mem_q_norm, mem_k_norm, mem_wo, ffn_norm, router_group_w, router_group_b, router_expert_w, router_expert_b, exp_w_gate, exp_w_up, exp_w_down):
    B, S, D = x.shape
    T = B * S
    depth = mix_norm.shape[0]
    for l in range(depth):
        proj = _proj(x.reshape(T, D), mix_norm[l].reshape(1, D), w_in[l].T).reshape(B, S, PROJ_PAD)
        pk, w2k = _compress_weights(cmp_pos_k[l], cmp_k_w2[l])
        pv, w2v = _compress_weights(cmp_pos_v[l], cmp_v_w2[l])
        gains = jnp.stack([_dup2(nsa_kcmp_norm[l]), _dup2(nsa_ksel_norm[l]), _dup2(nsa_kwin_norm[l])])
        kcmp, vcmp, ks, vs, kw, vw = _nsa_prep(proj, jnp.stack([pk, pv]), cmp_k_w1[l], cmp_v_w1[l],
                                               jnp.stack([w2k, w2v]), gains)
        o_a = _nsa_attn(proj, kcmp, vcmp, ks, vs, kw, vw,
                        jnp.tile(nsa_q_norm[l].reshape(1, HEAD_DIM), (1, NSA_HEADS)), gains,
                        nsa_out_norm[l].reshape(1, NSA_WIDTH))
        o_b = _retention(proj, ret_out_norm[l].reshape(1, RET_WIDTH))
        mk, mv = _mem_prep(mem, mem_kv_norm[l].reshape(1, D), mem_wkv[l].astype(BF16),
                           jnp.tile(mem_k_norm[l].reshape(1, HEAD_DIM), (1, MEM_HEADS)))
        w_r = jnp.concatenate([router_group_w[l],
                               router_expert_w[l].transpose(1, 0, 2).reshape(D, N_EXPERTS),
                               jnp.zeros((D, LANES - N_GROUPS - N_EXPERTS), F32)], axis=1).astype(BF16)
        b_r = jnp.concatenate([router_group_b[l], router_expert_b[l].reshape(N_EXPERTS),
                               jnp.zeros((LANES - N_GROUPS - N_EXPERTS,), F32)]).reshape(1, LANES)
        x2, hf, route, counts = _post(
            x, o_a, o_b, w_out[l].astype(BF16), mk, mv, mem_x_norm[l].reshape(1, D), mem_wq[l].astype(BF16),
            jnp.tile(mem_q_norm[l].reshape(1, HEAD_DIM), (1, MEM_HEADS)), mem_wo[l].astype(BF16),
            ffn_norm[l].reshape(1, D), w_r, b_r)
        route2d = route.reshape(T, LANES)
        n_tiles = T // MOE_TM
        cnt = counts[:, 0, :N_EXPERTS].astype(jnp.int32)
        cnt = (cnt + RUN_ALIGN - 1) // RUN_ALIGN * RUN_ALIGN
        loff = jnp.cumsum(cnt, axis=1) - cnt
        total = jnp.sum(cnt, axis=0)
        padded = (total + MOE_RB - 1) // MOE_RB * MOE_RB
        pend = jnp.cumsum(padded)
        pstart = pend - padded
        goff = pstart[None, :] + jnp.cumsum(cnt, axis=0) - cnt
        n_rows = 2 * T + n_tiles * N_EXPERTS * RUN_ALIGN + N_EXPERTS * MOE_RB
        n_act = (pend[-1:] // MOE_RB).astype(jnp.int32)
        loff_f = jnp.broadcast_to(jnp.pad(loff.astype(F32), ((0, 0), (0, LANES - N_EXPERTS)))[:, None, :],
                                  (n_tiles, 8, LANES))
        grp_row = jnp.arange(MOE_NLOC // RUN_ALIGN, dtype=jnp.int32) * RUN_ALIGN
        inside = ((loff[:, None, :] <= grp_row[None, :, None])
                  & (grp_row[None, :, None] < (loff + cnt)[:, None, :])).astype(jnp.int32)
        shift = jnp.sum(inside * (goff - loff)[:, None, :], axis=2)
        used = jnp.sum(inside, axis=2) > 0
        gmap_scatter = jnp.where(used, shift + grp_row[None, :], n_rows + grp_row[None, :]).reshape(-1)
        gmap_gather = jnp.where(used, shift + grp_row[None, :], 0).reshape(-1)
        xs = _scatter_rows((gmap_scatter, pstart + total, padded - total, n_act), hf.reshape(T, D), route2d, loff_f,
                           n_rows + MOE_NLOC)
        ys = _experts(pstart // MOE_RB, padded // MOE_RB, n_act, xs, n_rows,
                      exp_w_gate[l], exp_w_up[l], exp_w_down[l])
        x = _combine((gmap_gather,), x2.reshape(T, D), route2d, loff_f, ys).reshape(B, S, D)
    return x
```

```python
import functools

import numpy as np
import jax
import jax.numpy as jnp
from jax import lax
from jax.experimental import pallas as pl
from jax.experimental.pallas import tpu as pltpu

F32 = jnp.float32
BF16 = jnp.bfloat16

D_MODEL = 1024
HEAD_DIM = 64
LANES = 128
NSA_HEADS = 8
NSA_GROUPS = 2
NSA_WIDTH = NSA_HEADS * HEAD_DIM
CMP_BLOCK = 32
CMP_STRIDE = 16
CMP_HIDDEN = 2 * HEAD_DIM
SEL_BLOCK = 64
SEL_TOPK = 8
WINDOW = 512
RET_HEADS = 8
RET_WIDTH = RET_HEADS * HEAD_DIM
RET_CHUNK = 128
ROPE_BASE = 10000.0
MEM_HEADS = 4
MEM_WIDTH = MEM_HEADS * HEAD_DIM
N_GROUPS = 4
EXPERTS_PER_GROUP = 8
N_EXPERTS = N_GROUPS * EXPERTS_PER_GROUP
EXPERT_FF = D_MODEL // 4
EPS = 1e-6
NEG_INF = -1e30
BELOW_ALL = -3e38
MAX_FIXED_SHIFT = 40.0

COL_QA = 0
COL_QR, COL_KR, COL_VR, COL_GR = 512, 1024, 1536, 2048
COL_KVC, COL_KSV, COL_KWV = 2560, 2816, 3072
COL_GATE = 3328
PROJ_PAD = 3456
PROJ_SPLITS = (NSA_WIDTH, NSA_WIDTH + 6 * NSA_GROUPS * HEAD_DIM, NSA_WIDTH + 6 * NSA_GROUPS * HEAD_DIM + 3 * NSA_HEADS)

PROJ_TM = 512
RET_UNROLL = 4
NSA_TQ = 256
SEL_KC = 512
POST_CHAINS = 2
MOE_TM = 512
MOE_RB = 256
MOE_SLOTS = 8
RUN_ALIGN = 16
MOE_NLOC = 2 * MOE_TM + N_EXPERTS * RUN_ALIGN
VMEM_LIMIT = 56 * 1024 * 1024


def _cparams(n_axes):
    return pltpu.CompilerParams(dimension_semantics=("arbitrary",) * n_axes,
                                vmem_limit_bytes=VMEM_LIMIT)


def _dot(a, b):
    return jnp.dot(a, b, preferred_element_type=F32)


def _dot_nt(a, b):
    return lax.dot_general(a, b, (((1,), (1,)), ((), ())), preferred_element_type=F32)


def _dot_tn(a, b):
    return lax.dot_general(a, b, (((0,), (0,)), ((), ())), preferred_element_type=F32)


def _rms_full(x, g):
    ms = jnp.mean(x * x, axis=-1, keepdims=True)
    return x * lax.rsqrt(ms + EPS) * g


def _seg_mean(x, avg):
    return _dot(x.astype(BF16), avg)


def _silu(x):
    return x * (1.0 / (1.0 + jnp.exp(-x)))


def _sigmoid(x):
    return 1.0 / (1.0 + jnp.exp(-x))


def _block_avg(width):
    i = np.arange(width)
    return ((i[:, None] // HEAD_DIM == i[None, :] // HEAD_DIM) / HEAD_DIM).astype(np.float32)


def _proj_body(x_ref, g_ref, wt_ref, o_ref, w_scr):
    @pl.when(pl.program_id(0) == 0)
    def _():
        kv0, gate0, ret0 = PROJ_SPLITS
        w_scr[COL_QA:COL_QR] = wt_ref[0:kv0].astype(BF16)
        w_scr[COL_QR:COL_KVC] = wt_ref[ret0:].astype(BF16)
        w_scr[COL_KVC:COL_GATE] = wt_ref[kv0:gate0].astype(BF16)
        pad = jnp.zeros((PROJ_PAD - COL_GATE - (ret0 - gate0), wt_ref.shape[1]), F32)
        w_scr[COL_GATE:] = jnp.concatenate([wt_ref[gate0:ret0], pad], axis=0).astype(BF16)

    h = _rms_full(x_ref[...], g_ref[...]).astype(BF16)
    step = PROJ_PAD // 3
    for j in range(3):
        o_ref[:, j * step:(j + 1) * step] = _dot_nt(h, w_scr[j * step:(j + 1) * step]).astype(BF16)


def _proj(x2d, g, w_t):
    T = x2d.shape[0]
    assert w_t.shape[0] == PROJ_SPLITS[2] + 4 * RET_WIDTH
    return pl.pallas_call(
        _proj_body,
        grid=(T // PROJ_TM,),
        in_specs=[pl.BlockSpec((PROJ_TM, D_MODEL), lambda i: (i, 0)),
                  pl.BlockSpec((1, D_MODEL), lambda i: (0, 0)),
                  pl.BlockSpec(w_t.shape, lambda i: (0, 0))],
        out_specs=pl.BlockSpec((PROJ_TM, PROJ_PAD), lambda i: (i, 0)),
        out_shape=jax.ShapeDtypeStruct((T, PROJ_PAD), BF16),
        scratch_shapes=[pltpu.VMEM((PROJ_PAD, D_MODEL), BF16)],
        compiler_params=_cparams(1),
        name="proj",
    )(x2d, g, w_t)


def _dup_groups(x):
    lane = lax.broadcasted_iota(jnp.int32, x.shape, 1)
    xs = pltpu.roll(x, HEAD_DIM, axis=1)
    lo = lane < HEAD_DIM
    return jnp.where(lo, x, xs), jnp.where(lo, xs, x)


def _ones_groups(x):
    lane = lax.broadcasted_iota(jnp.int32, x.shape, 1)
    lo = lane < HEAD_DIM
    xs = pltpu.roll(x, HEAD_DIM, axis=1)
    return jnp.where(lo, x, 1.0), jnp.where(lo, 1.0, xs), jnp.where(lo, xs, 1.0), jnp.where(lo, 1.0, x)


def _nsa_prep_body(kvc_ref, ksv_ref, kwv_ref, pos_ref, w1k_ref, w1v_ref, w2_ref, gain_ref, avg_ref,
                   kcmp_ref, vcmp_ref, ks_ref, vs_ref, kw_ref, vw_ref, scr_k, scr_v, w1_ref):
    @pl.when(pl.program_id(0) == 0)
    def _():
        zero = jnp.zeros((HEAD_DIM, CMP_HIDDEN), BF16)
        for j, src in ((0, w1k_ref), (1, w1v_ref)):
            for l in range(CMP_BLOCK):
                piece = src[l * HEAD_DIM:(l + 1) * HEAD_DIM, :].astype(BF16)
                r0 = (l % CMP_STRIDE) * LANES
                w1_ref[j, l // CMP_STRIDE, r0:r0 + HEAD_DIM, :] = jnp.concatenate([piece, zero], axis=1)
                w1_ref[j, l // CMP_STRIDE, r0 + HEAD_DIM:r0 + LANES, :] = jnp.concatenate([zero, piece], axis=1)

    avg = avg_ref[...]
    n_c = scr_k.shape[0] // CMP_STRIDE
    scr_k[...] = kvc_ref[0, :, 0:LANES].astype(F32)
    scr_v[...] = kvc_ref[0, :, LANES:2 * LANES].astype(F32)
    for j, out_ref, scr in ((0, kcmp_ref, scr_k), (1, vcmp_ref, scr_v)):
        ycat = jnp.concatenate(
            [scr[pl.ds(l, n_c, stride=CMP_STRIDE), :] for l in range(CMP_STRIDE)], axis=1)
        first = _dot((ycat + pos_ref[j, 0]).astype(BF16), w1_ref[j, 0])
        second = _dot((ycat + pos_ref[j, 1]).astype(BF16), w1_ref[j, 1])
        hidden = first + pltpu.roll(second, n_c - 1, axis=0)
        cmp_tok = _dot(_silu(hidden).astype(BF16), w2_ref[j])
        if j == 0:
            ms = _seg_mean(cmp_tok * cmp_tok, avg)
            cmp_tok = cmp_tok * lax.rsqrt(ms + EPS) * gain_ref[0]
        for i, piece in enumerate(_dup_groups(cmp_tok) if j == 0 else _ones_groups(cmp_tok)):
            out_ref[0, i] = piece.astype(BF16)

    for src_ref, k_out, v_out, gi in ((ksv_ref, ks_ref, vs_ref, 1), (kwv_ref, kw_ref, vw_ref, 2)):
        k = src_ref[0, :, 0:LANES].astype(F32)
        ms = _seg_mean(k * k, avg)
        k = k * lax.rsqrt(ms + EPS) * gain_ref[gi]
        d0, d1 = _dup_groups(k)
        k_out[0, 0] = d0.astype(BF16)
        k_out[0, 1] = d1.astype(BF16)
        for i, piece in enumerate(_ones_groups(src_ref[0, :, LANES:2 * LANES].astype(F32))):
            v_out[0, i] = piece.astype(BF16)


def _nsa_prep(proj3, pos, w1k, w1v, w2, gains):
    B, S, _ = proj3.shape
    n_c = S // CMP_STRIDE
    avg = jnp.asarray(_block_avg(LANES), BF16)
    col = lambda c: pl.BlockSpec((1, S, 2 * LANES), lambda b: (b, 0, c // (2 * LANES)))
    full = lambda a: pl.BlockSpec(a.shape, lambda b: (0,) * a.ndim)
    cmp_spec = pl.BlockSpec((1, NSA_GROUPS, n_c, LANES), lambda b: (b, 0, 0, 0))
    seq_spec = pl.BlockSpec((1, NSA_GROUPS, S, LANES), lambda b: (b, 0, 0, 0))
    cmp_shape = jax.ShapeDtypeStruct((B, NSA_GROUPS, n_c, LANES), BF16)
    seq_shape = jax.ShapeDtypeStruct((B, NSA_GROUPS, S, LANES), BF16)

    def val(a):
        if isinstance(a, jax.ShapeDtypeStruct):
            return jax.ShapeDtypeStruct((a.shape[0], 2 * a.shape[1]) + a.shape[2:], a.dtype)
        return pl.BlockSpec((1, 2 * NSA_GROUPS) + a.block_shape[2:], lambda b: (b, 0, 0, 0))

    return pl.pallas_call(
        _nsa_prep_body,
        grid=(B,),
        in_specs=[col(COL_KVC), col(COL_KSV), col(COL_KWV), full(pos), full(w1k), full(w1v), full(w2), full(gains),
                  full(avg)],
        out_specs=[cmp_spec, val(cmp_spec), seq_spec, val(seq_spec), seq_spec, val(seq_spec)],
        out_shape=[cmp_shape, val(cmp_shape), seq_shape, val(seq_shape), seq_shape, val(seq_shape)],
        scratch_shapes=[pltpu.VMEM((S, LANES), F32), pltpu.VMEM((S, LANES), F32),
                        pltpu.VMEM((2, 2, CMP_STRIDE * LANES, NSA_GROUPS * CMP_HIDDEN), BF16)],
        compiler_params=_cparams(1),
        name="nsa_prep",
    )(proj3, proj3, proj3, pos, w1k, w1v, w2, gains, avg)


def _nsa_attn_body(q_ref, gate_ref, kcmp_ref, vcmp_ref, ks_ref, vs_ref, kw_ref, vw_ref,
                   qgain_ref, kgain_ref, ogain_ref, avgq_ref, avgo_ref, msct_ref, esel_ref, egate_ref, wbias_ref,
                   dbias_ref, cbias_ref,
                   o_ref, m_scr, acc_scr):
    tq = q_ref.shape[1]
    n_cmp = kcmp_ref.shape[2]
    n_sel = msct_ref.shape[0]
    kc_len = esel_ref.shape[2]
    rows = 4 * tq
    qi = pl.program_id(1)
    q0 = qi * tq

    q = q_ref[0].astype(F32)
    ms = _seg_mean(q * q, avgq_ref[...])
    qn = q * lax.rsqrt(ms + EPS) * qgain_ref[...] * (HEAD_DIM ** -0.5)

    gate_sig = _sigmoid(gate_ref[0].astype(F32)).astype(BF16)
    gates = [_dot(gate_sig, egate_ref[j]) for j in range(3)]

    lane_q = lax.broadcasted_iota(jnp.int32, (tq, LANES), 1)
    lo_q = lane_q < HEAD_DIM

    blk = lax.broadcasted_iota(jnp.int32, (n_sel, tq), 0)
    cur = lax.shift_right_logical(q0 + lax.broadcasted_iota(jnp.int32, (n_sel, tq), 1), int(np.log2(SEL_BLOCK)))
    forced = (blk == 0) | (blk == cur) | (blk == cur - 1)
    future = blk > cur
    blk_f = blk.astype(F32)

    def heads4(x):
        return jnp.concatenate([x] * 4, axis=0)

    def weighted_values(p, v_low, v_high):
        return jnp.concatenate([_dot(p[:2 * tq], v_low), _dot(p[2 * tq:], v_high)], axis=0)

    def normalised_pairs(acc, guard):
        out = []
        for p in range(2):
            low = acc[p * tq:(p + 1) * tq]
            high = acc[(2 + p) * tq:(3 + p) * tq]
            den = pltpu.roll(jnp.where(lo_q, high, low), HEAD_DIM, axis=1)
            if guard:
                den = jnp.maximum(den, 1e-30)
            out.append(jnp.where(lo_q, low, high) / den)
        return out

    groups = range(NSA_GROUPS)
    qs = []
    for g in groups:
        slabs = [qn[:, (2 * g + p) * LANES:(2 * g + p + 1) * LANES] for p in range(2)]
        qs.append(jnp.concatenate(
            [jnp.where(lo_q, slabs[0], 0.0), jnp.where(lo_q, slabs[1], 0.0),
             jnp.where(lo_q, 0.0, slabs[0]), jnp.where(lo_q, 0.0, slabs[1])], axis=0).astype(BF16))

    def compressed_and_select(shift):
        s_c = [_dot_nt(qs[g], kcmp_ref[0, g]) for g in groups]
        if shift is None:
            r_c = lax.broadcasted_iota(jnp.int32, (rows, n_cmp), 0)
            c_c = lax.broadcasted_iota(jnp.int32, (rows, n_cmp), 1)
            cmask = (c_c * CMP_STRIDE + (CMP_BLOCK - 1)) <= q0 + (r_c & (tq - 1))
            s_c = [jnp.where(cmask, s_c[g], NEG_INF) for g in groups]
            e_c = [jnp.where(cmask, jnp.exp(s_c[g] - jnp.max(s_c[g], axis=-1, keepdims=True)), 0.0) for g in groups]
        else:
            bias = heads4(cbias_ref[qi] - shift)
            e_c = [jnp.exp(s_c[g] + bias) for g in groups]
        e_b = [e_c[g].astype(BF16) for g in groups]
        acc_c = [weighted_values(e_b[g], vcmp_ref[0, 2 * g], vcmp_ref[0, 2 * g + 1]) for g in groups]
        ones = jnp.ones((8, n_cmp), BF16)
        imp = []
        for g in groups:
            num = [_dot_nt(msct_ref[...], e_b[g][h * tq:(h + 1) * tq]) for h in range(4)]
            den = [_dot_nt(ones, e_b[g][h * tq:(h + 1) * tq])[0:1] for h in range(4)]
            parts = [num[h] / jnp.maximum(den[h], 1e-30) for h in range(4)]
            imp.append((parts[0] + parts[1]) + (parts[2] + parts[3]))
        v = [jnp.where(forced, BELOW_ALL, jnp.where(future, NEG_INF, imp[g])) for g in groups]
        sel = [jnp.where(forced, 1.0, 0.0) for g in groups]
        for _ in range(SEL_TOPK - 3):
            mx = [jnp.max(v[g], axis=0, keepdims=True) for g in groups]
            first = [jnp.min(jnp.where(v[g] == mx[g], blk_f, float(LANES)), axis=0, keepdims=True) for g in groups]
            pick = [blk_f == first[g] for g in groups]
            sel = [jnp.where(pick[g], 1.0, sel[g]) for g in groups]
            v = [jnp.where(pick[g], BELOW_ALL, v[g]) for g in groups]
        return [normalised_pairs(acc_c[g], True) for g in groups], [sel[g].astype(BF16) for g in groups]

    n_before = lax.shift_right_logical(q0, int(np.log2(kc_len)))
    causal = dbias_ref[qi & (kc_len // tq - 1)]
    w0 = pl.multiple_of(jnp.maximum(q0 - WINDOW, 0), tq)
    n_w = WINDOW + tq
    w_case = jnp.minimum(qi, WINDOW // tq)

    def sel_keys(ref, g, kc):
        return ref[0, g, pl.ds(pl.multiple_of(kc * kc_len, kc_len), kc_len), :]

    def sel_scores(sel_b, g, kc, causal_bias, shift):
        chosen = _dot_tn(sel_b[g], esel_ref[kc])
        bias = (chosen - 1.0) * (-NEG_INF)
        if causal_bias is not None:
            bias = bias + causal_bias
        if shift is not None:
            bias = bias - shift
        return _dot_nt(qs[g], sel_keys(ks_ref, g, kc)) + heads4(bias)

    def win_scores(g, shift):
        bias = wbias_ref[w_case] if shift is None else wbias_ref[w_case] - shift
        return _dot_nt(qs[g], kw_ref[0, g, pl.ds(w0, n_w), :]) + heads4(bias)

    def win_values(g):
        return vw_ref[0, 2 * g, pl.ds(w0, n_w), :], vw_ref[0, 2 * g + 1, pl.ds(w0, n_w), :]

    def sel_values(g, kc):
        return sel_keys(vs_ref, 2 * g, kc), sel_keys(vs_ref, 2 * g + 1, kc)

    def finish(cmp_s, acc_w):
        for g in groups:
            sel_s = normalised_pairs(acc_scr[g], False)
            win_s = normalised_pairs(acc_w[g], False)
            for p in range(2):
                cols = slice((2 * g + p) * LANES, (2 * g + p + 1) * LANES)
                mix = gates[0][:, cols] * cmp_s[g][p] + gates[1][:, cols] * sel_s[p] + gates[2][:, cols] * win_s[p]
                ms_o = _seg_mean(mix * mix, avgo_ref[...])
                o_ref[0, :, cols] = (mix * lax.rsqrt(ms_o + EPS) * ogain_ref[:, cols]).astype(BF16)

    def fixed_shift_path(shift):
        cmp_s, sel_b = compressed_and_select(shift)

        def probs(s):
            return jnp.exp(s).astype(BF16)

        for g in groups:
            acc_scr[g] = jnp.zeros(acc_scr.shape[1:], F32)

        def before(kc, carry):
            s = [sel_scores(sel_b, g, kc, None, shift) for g in groups]
            p = [probs(s[g]) for g in groups]
            for g in groups:
                acc_scr[g] = acc_scr[g] + weighted_values(p[g], *sel_values(g, kc))
            return carry

        lax.fori_loop(0, n_before, before, 0)
        s_d0 = sel_scores(sel_b, 0, n_before, causal, shift)
        s_d1 = sel_scores(sel_b, 1, n_before, causal, shift)
        p_d0 = probs(s_d0)
        s_w0 = win_scores(0, shift)
        acc_scr[0] = acc_scr[0] + weighted_values(p_d0, *sel_values(0, n_before))
        p_d1 = probs(s_d1)
        s_w1 = win_scores(1, shift)
        acc_scr[1] = acc_scr[1] + weighted_values(p_d1, *sel_values(1, n_before))
        p_w0 = probs(s_w0)
        acc_w0 = weighted_values(p_w0, *win_values(0))
        p_w1 = probs(s_w1)
        acc_w1 = weighted_values(p_w1, *win_values(1))
        finish(cmp_s, [acc_w0, acc_w1])

    def online_path():
        cmp_s, sel_b = compressed_and_select(None)
        for g in groups:
            m_scr[g] = jnp.full(m_scr.shape[1:], NEG_INF, F32)
            acc_scr[g] = jnp.zeros(acc_scr.shape[1:], F32)

        def sel_softmax(g, s):
            m_old = m_scr[g]
            m_new = jnp.maximum(m_old, jnp.max(s, axis=-1, keepdims=True))
            m_scr[g] = m_new
            return jnp.exp(s - m_new).astype(BF16), jnp.exp(m_old - m_new)

        def sel_accumulate(g, kc, p, alpha):
            acc_scr[g] = alpha * acc_scr[g] + weighted_values(p, *sel_values(g, kc))

        def win_softmax(s):
            return jnp.exp(s - jnp.max(s, axis=-1, keepdims=True)).astype(BF16)

        def before(kc, carry):
            s = [sel_scores(sel_b, g, kc, None, None) for g in groups]
            pa = [sel_softmax(g, s[g]) for g in groups]
            for g in groups:
                sel_accumulate(g, kc, *pa[g])
            return carry

        lax.fori_loop(0, n_before, before, 0)
        s_d0 = sel_scores(sel_b, 0, n_before, causal, None)
        s_d1 = sel_scores(sel_b, 1, n_before, causal, None)
        pa0 = sel_softmax(0, s_d0)
        s_w0 = win_scores(0, None)
        sel_accumulate(0, n_before, *pa0)
        pa1 = sel_softmax(1, s_d1)
        s_w1 = win_scores(1, None)
        sel_accumulate(1, n_before, *pa1)
        acc_w0 = weighted_values(win_softmax(s_w0), *win_values(0))
        acc_w1 = weighted_values(win_softmax(s_w1), *win_values(1))
        finish(cmp_s, [acc_w0, acc_w1])

    bound = 1.01 * (HEAD_DIM ** 0.5) * jnp.max(jnp.abs(qgain_ref[...])) * jnp.max(jnp.abs(kgain_ref[...]))
    safe = bound <= MAX_FIXED_SHIFT
    pl.when(safe)(lambda: fixed_shift_path(bound))
    pl.when(jnp.logical_not(safe))(online_path)


def _sel_from_cmp(n_cmp, n_sel):
    c0 = np.arange(n_cmp) * CMP_STRIDE
    s0 = np.arange(n_sel) * SEL_BLOCK
    ov = np.minimum(c0[None, :] + CMP_BLOCK, s0[:, None] + SEL_BLOCK) - np.maximum(c0[None, :], s0[:, None])
    m = (np.clip(ov, 0, None) / CMP_BLOCK).astype(np.float32)
    m[:, (np.arange(n_cmp) * CMP_STRIDE + CMP_BLOCK) > n_sel * SEL_BLOCK] = 0.0
    return m


def _nsa_attn(proj3, kcmp, vcmp, ks, vs, kw, vw, q_gain, k_gains, o_gain):
    B, S, _ = proj3.shape
    n_cmp = kcmp.shape[2]
    n_sel = S // SEL_BLOCK
    tq = NSA_TQ
    assert n_sel % 8 == 0 and S % SEL_KC == 0 and SEL_KC % tq == 0 and WINDOW % tq == 0 and S >= WINDOW + tq
    avgq = jnp.asarray(_block_avg(NSA_WIDTH), BF16)
    avgo = jnp.asarray(_block_avg(LANES), BF16)
    msct = jnp.asarray(_sel_from_cmp(n_cmp, n_sel), BF16)
    esel = (np.arange(n_sel)[:, None] == np.arange(S)[None, :] // SEL_BLOCK).astype(np.float32)
    esel = jnp.asarray(esel.reshape(n_sel, S // SEL_KC, SEL_KC).transpose(1, 0, 2), BF16)
    src = np.arange(LANES)[:, None]
    dst = np.arange(NSA_WIDTH)[None, :]
    egate = jnp.asarray(np.stack([(src == (dst // HEAD_DIM) * 3 + j) for j in range(3)]).astype(np.float32), BF16)
    r = np.arange(tq)[:, None]
    n_w = WINDOW + tq
    wcases = []
    for i in range(WINDOW // tq + 1):
        diff = (i * tq - max(i * tq - WINDOW, 0)) + r - np.arange(n_w)[None, :]
        wcases.append(np.where((diff >= 0) & (diff < WINDOW), 0.0, NEG_INF))
    wbias = jnp.asarray(np.stack(wcases), F32)
    dbias = jnp.asarray(np.stack([np.where(np.arange(SEL_KC)[None, :] <= i * tq + r, 0.0, NEG_INF)
                                  for i in range(SEL_KC // tq)]), F32)
    c_end = np.arange(n_cmp)[None, :] * CMP_STRIDE + (CMP_BLOCK - 1)
    cbias = jnp.asarray(np.stack([np.where(c_end <= i * tq + r, 0.0, NEG_INF) for i in range(S // tq)]), F32)

    full = lambda a: pl.BlockSpec(a.shape, lambda b, i: (0,) * a.ndim)
    per_b = lambda a: pl.BlockSpec((1,) + a.shape[1:], lambda b, i: (b,) + (0,) * (a.ndim - 1))
    return pl.pallas_call(
        _nsa_attn_body,
        grid=(B, S // tq),
        in_specs=[pl.BlockSpec((1, tq, NSA_WIDTH), lambda b, i: (b, i, COL_QA // NSA_WIDTH)),
                  pl.BlockSpec((1, tq, LANES), lambda b, i: (b, i, COL_GATE // LANES)),
                  per_b(kcmp), per_b(vcmp), per_b(ks), per_b(vs), per_b(kw), per_b(vw),
                  full(q_gain), full(k_gains), full(o_gain), full(avgq), full(avgo), full(msct), full(esel), full(egate),
                  full(wbias), full(dbias), full(cbias)],
        out_specs=pl.BlockSpec((1, tq, NSA_WIDTH), lambda b, i: (b, i, 0)),
        out_shape=jax.ShapeDtypeStruct((B, S, NSA_WIDTH), BF16),
        scratch_shapes=[pltpu.VMEM((NSA_GROUPS, 4 * tq, 1), F32), pltpu.VMEM((NSA_GROUPS, 4 * tq, LANES), F32)],
        compiler_params=_cparams(2),
        name="nsa_attn",
    )(proj3, proj3, kcmp, vcmp, ks, vs, kw, vw, q_gain, k_gains, o_gain, avgq, avgo, msct, esel, egate, wbias, dbias,
      cbias)


def _retention_body(q_ref, k_ref, v_ref, g_ref, cos_ref, sin_ref, decay_ref, xi_ref, zeta_ref, gammac_ref,
                    gain_ref, avg_ref, o_ref, state_scr):
    S = q_ref.shape[1]
    C = RET_CHUNK
    lane = lax.broadcasted_iota(jnp.int32, (C, LANES), 1)
    lo = lane < HEAD_DIM
    first_half = (lane & (HEAD_DIM - 1)) < HEAD_DIM // 2
    r = lax.broadcasted_iota(jnp.int32, (LANES, LANES), 0)
    c = lax.broadcasted_iota(jnp.int32, (LANES, LANES), 1)
    same_head = (r < HEAD_DIM) == (c < HEAD_DIM)
    avg = avg_ref[...]
    state_scr[...] = jnp.zeros(state_scr.shape, F32)

    def rope(x, cos, sin):
        swapped = jnp.where(first_half, pltpu.roll(x, LANES - HEAD_DIM // 2, axis=1),
                            pltpu.roll(x, HEAD_DIM // 2, axis=1))
        return x * cos + swapped * sin

    n_pairs = RET_HEADS // 2
    cols = [slice(p * LANES, (p + 1) * LANES) for p in range(n_pairs)]
    units = [(u, p) for u in range(RET_UNROLL) for p in range(n_pairs)]

    def chunks(n, carry):
        r0 = [pl.multiple_of((n * RET_UNROLL + u) * C, C) for u in range(RET_UNROLL)]
        cos = [cos_ref[pl.ds(r0[u], C), :] for u in range(RET_UNROLL)]
        sin = [sin_ref[pl.ds(r0[u], C), :] for u in range(RET_UNROLL)]
        q = [rope(q_ref[0, pl.ds(r0[u], C), cols[p]].astype(F32), cos[u], sin[u]) for u, p in units]
        k = [rope(k_ref[0, pl.ds(r0[u], C), cols[p]].astype(F32), cos[u], sin[u]) * (HEAD_DIM ** -0.5)
             for u, p in units]
        vb = [v_ref[0, pl.ds(r0[u], C), cols[p]] for u, p in units]
        kb = [k[i].astype(BF16) for i in range(len(units))]
        inner = [_dot_nt(jnp.where(lo if half == 0 else ~lo, q[i], 0.0).astype(BF16), kb[i])
                 * decay_ref[2 * units[i][1] + half] for i in range(len(units)) for half in range(2)]
        upd = [_dot_tn((k[i] * zeta_ref[units[i][1]]).astype(BF16), vb[i]) for i in range(len(units))]
        state = [state_scr[p] for p in range(n_pairs)]
        for u in range(RET_UNROLL):
            for p in range(n_pairs):
                prev = state[u * n_pairs + p]
                state.append(gammac_ref[p] * prev + jnp.where(same_head, upd[u * n_pairs + p], 0.0))
        cross = [_dot(q[i].astype(BF16), state[i].astype(BF16)) * xi_ref[units[i][1]] for i in range(len(units))]
        outs = [_dot(inner[j].astype(BF16), vb[j // 2]) for j in range(2 * len(units))]
        for p in range(n_pairs):
            state_scr[p] = state[RET_UNROLL * n_pairs + p]
        y = jnp.concatenate([jnp.where(lo, outs[2 * i], outs[2 * i + 1]) + cross[i] for i in range(len(units))],
                            axis=0)
        mu = _seg_mean(y, avg)
        d = y - mu
        var = _seg_mean(d * d, avg)
        yn = d * lax.rsqrt(var + EPS)
        for i, (u, p) in enumerate(units):
            gate = g_ref[0, pl.ds(r0[u], C), cols[p]].astype(F32)
            o_ref[0, pl.ds(r0[u], C), cols[p]] = (_silu(gate) * (yn[i * C:(i + 1) * C] * gain_ref[:, cols[p]])).astype(BF16)
        return carry

    lax.fori_loop(0, S // (C * RET_UNROLL), chunks, 0)


def _retention_tables(S):
    half = HEAD_DIM // 2
    inv_freq = ROPE_BASE ** (-np.arange(half, dtype=np.float64) / half)
    ang = np.arange(S, dtype=np.float64)[:, None] * inv_freq[None, :]
    cos, sin = np.cos(ang), np.sin(ang)
    cos_t = np.tile(cos, (1, 4))
    sin_t = np.tile(np.concatenate([-sin, sin], axis=1), (1, 2))
    C = RET_CHUNK
    H = RET_HEADS
    log_gamma = np.log1p(-np.power(2.0, -5.0 - np.arange(H, dtype=np.float64)))
    i = np.arange(C, dtype=np.float64)
    rel = i[:, None] - i[None, :]
    decay = np.where(rel >= 0, np.exp(np.maximum(rel, 0.0)[None] * log_gamma[:, None, None]), 0.0)
    xi = np.exp((i + 1.0)[:, None] * log_gamma[None, :])
    zeta = np.exp((C - 1.0 - i)[:, None] * log_gamma[None, :])
    gamma_c = np.exp(C * log_gamma)
    per_pair = lambda t: np.repeat(t.T.reshape(H // 2, 2, -1), HEAD_DIM, axis=1).transpose(0, 2, 1)
    gammac = np.repeat(gamma_c.reshape(H // 2, 2), HEAD_DIM, axis=1)[:, None, :]
    return tuple(jnp.asarray(t, F32) for t in (cos_t, sin_t, decay, per_pair(xi), per_pair(zeta), gammac))


def _retention(proj3, gain):
    B, S, _ = proj3.shape
    cos_t, sin_t, decay, xi, zeta, gammac = _retention_tables(S)
    avg = jnp.asarray(_block_avg(LANES), BF16)
    col = lambda c: pl.BlockSpec((1, S, RET_WIDTH), lambda b: (b, 0, c // RET_WIDTH))
    full = lambda a: pl.BlockSpec(a.shape, lambda b: (0,) * a.ndim)
    return pl.pallas_call(
        _retention_body,
        grid=(B,),
        in_specs=[col(COL_QR), col(COL_KR), col(COL_VR), col(COL_GR), full(cos_t), full(sin_t), full(decay),
                  full(xi), full(zeta), full(gammac), full(gain), full(avg)],
        out_specs=pl.BlockSpec((1, S, RET_WIDTH), lambda b: (b, 0, 0)),
        out_shape=jax.ShapeDtypeStruct((B, S, RET_WIDTH), BF16),
        scratch_shapes=[pltpu.VMEM((RET_HEADS // 2, LANES, LANES), F32)],
        compiler_params=_cparams(1),
        name="retention",
    )(proj3, proj3, proj3, proj3, cos_t, sin_t, decay, xi, zeta, gammac, gain, avg)


def _mem_prep_body(mem_ref, g_ref, wkv_ref, kgain_ref, avg_ref, k_ref, v_ref):
    hm = _rms_full(mem_ref[0], g_ref[...]).astype(BF16)
    kv = _dot(hm, wkv_ref[...])
    k = kv[:, :MEM_WIDTH]
    ms = _seg_mean(k * k, avg_ref[...])
    k_ref[0] = (k * lax.rsqrt(ms + EPS) * kgain_ref[...]).astype(BF16)
    v_ref[0] = kv[:, MEM_WIDTH:].astype(BF16)


def _mem_prep(mem, g, wkv, kgain):
    B, M, _ = mem.shape
    avg = jnp.asarray(_block_avg(MEM_WIDTH), BF16)
    full = lambda a: pl.BlockSpec(a.shape, lambda b: (0,) * a.ndim)
    out_spec = pl.BlockSpec((1, M, MEM_WIDTH), lambda b: (b, 0, 0))
    out_shape = jax.ShapeDtypeStruct((B, M, MEM_WIDTH), BF16)
    return pl.pallas_call(
        _mem_prep_body,
        grid=(B,),
        in_specs=[pl.BlockSpec((1, M, D_MODEL), lambda b: (b, 0, 0)), full(g), full(wkv), full(kgain), full(avg)],
        out_specs=[out_spec, out_spec],
        out_shape=[out_shape, out_shape],
        compiler_params=_cparams(1),
        name="mem_prep",
    )(mem, g, wkv, kgain, avg)


def _post_body(x_ref, oa_ref, ob_ref, wout_ref, mk_ref, mv_ref, gx_ref, wq_ref, qgain_ref, avg_ref, wo_ref,
               gf_ref, wr_ref, br_ref, tri_ref, x2_ref, h_ref, route_ref, count_ref, wout_b, wq_b, wo_b):
    @pl.when((pl.program_id(0) == 0) & (pl.program_id(1) == 0))
    def _():
        wout_b[...] = wout_ref[...].astype(BF16)
        wq_b[...] = wq_ref[...].astype(BF16)
        wo_b[...] = wo_ref[...].astype(BF16)

    tm = x_ref.shape[1] // POST_CHAINS
    chains = range(POST_CHAINS)
    rows = [slice(c * tm, (c + 1) * tm) for c in chains]
    x1 = [x_ref[0, rows[c]] + _dot(oa_ref[0, rows[c]], wout_b[0:NSA_WIDTH, :])
          + _dot(ob_ref[0, rows[c]], wout_b[NSA_WIDTH:, :]) for c in chains]

    h = [_rms_full(x1[c], gx_ref[...]).astype(BF16) for c in chains]
    q = [_dot(h[c], wq_b[...]) for c in chains]
    ms = [_seg_mean(q[c] * q[c], avg_ref[...]) for c in chains]
    q = [q[c] * lax.rsqrt(ms[c] + EPS) * qgain_ref[...] * (HEAD_DIM ** -0.5) for c in chains]
    lane = lax.broadcasted_iota(jnp.int32, (tm, LANES), 1)
    lo = lane < HEAD_DIM
    heads = [(c, p, half) for c in chains for p in range(MEM_HEADS // 2) for half in range(2)]
    s = [_dot_nt(jnp.where(lo if half == 0 else ~lo, q[c][:, p * LANES:(p + 1) * LANES], 0.0).astype(BF16),
                 mk_ref[0, :, p * LANES:(p + 1) * LANES]) for c, p, half in heads]
    e = [jnp.exp(s[i] - jnp.max(s[i], axis=-1, keepdims=True)) for i in range(len(heads))]
    pr = [(e[i] / jnp.sum(e[i], axis=-1, keepdims=True)).astype(BF16) for i in range(len(heads))]
    outs = [_dot(pr[i], mv_ref[0, :, heads[i][1] * LANES:(heads[i][1] + 1) * LANES]) for i in range(len(heads))]
    per_chain = MEM_HEADS
    o = [jnp.concatenate([jnp.where(lo, outs[c * per_chain + 2 * p], outs[c * per_chain + 2 * p + 1])
                          for p in range(MEM_HEADS // 2)], axis=1).astype(BF16) for c in chains]
    x2 = [x1[c] + _dot(o[c], wo_b[...]) for c in chains]
    for c in chains:
        x2_ref[0, rows[c]] = x2[c]

    hf = [_rms_full(x2[c], gf_ref[...]).astype(BF16) for c in chains]
    for c in chains:
        h_ref[0, rows[c]] = hf[c]
    logits = [_dot(hf[c], wr_ref[...]) + br_ref[...] for c in chains]
    lane_f = lane.astype(F32)
    big = float(LANES)
    picks = []
    for c in chains:
        gl = jnp.where(lane < N_GROUPS, logits[c], BELOW_ALL)
        gmax = jnp.max(gl, axis=-1, keepdims=True)
        grp = jnp.min(jnp.where(gl == gmax, lane_f, big), axis=-1, keepdims=True)
        g_w = 1.0 / jnp.sum(jnp.where(lane < N_GROUPS, jnp.exp(gl - gmax), 0.0), axis=-1, keepdims=True)
        e_lo = N_GROUPS + grp * EXPERTS_PER_GROUP
        el = jnp.where((lane_f >= e_lo) & (lane_f < e_lo + EXPERTS_PER_GROUP), logits[c], BELOW_ALL)
        v0 = jnp.max(el, axis=-1, keepdims=True)
        i0 = jnp.min(jnp.where(el == v0, lane_f, big), axis=-1, keepdims=True)
        el = jnp.where(lane_f == i0, BELOW_ALL, el)
        v1 = jnp.max(el, axis=-1, keepdims=True)
        i1 = jnp.min(jnp.where(el == v1, lane_f, big), axis=-1, keepdims=True)
        e1 = jnp.exp(v1 - v0)
        picks.append((i0 - N_GROUPS, i1 - N_GROUPS, g_w / (1.0 + e1), g_w * e1 / (1.0 + e1)))

    hot = [[lane_f == picks[c][s] for s in range(2)] for c in chains]
    both = jnp.concatenate([jnp.where(hot[c][0], 1.0, 0.0) + jnp.where(hot[c][1], 1.0, 0.0) for c in chains], axis=0)
    before = _dot(tri_ref[...], both.astype(BF16))
    count_ref[0] = jnp.broadcast_to(jnp.sum(both, axis=0, keepdims=True), count_ref.shape[1:])
    for c in chains:
        e0, e1, w0, w1 = picks[c]
        r0 = jnp.sum(jnp.where(hot[c][0], before[rows[c]], 0.0), axis=-1, keepdims=True)
        r1 = jnp.sum(jnp.where(hot[c][1], before[rows[c]], 0.0), axis=-1, keepdims=True)
        cols = (e0, e1, w0, w1, r0, r1)
        route = jnp.zeros((tm, LANES), F32)
        for k in range(len(cols)):
            route = jnp.where(lane == k, cols[k], route)
        route_ref[0, rows[c]] = route


def _post(x, oa, ob, wout, mk, mv, gx, wq, qgain, wo, gf, wr, br):
    B, S, _ = x.shape
    tm = MOE_TM
    n_s = S // tm
    avg = jnp.asarray(_block_avg(MEM_WIDTH), BF16)
    tri = jnp.asarray(np.tril(np.ones((tm, tm), np.float32), -1), BF16)
    full = lambda a: pl.BlockSpec(a.shape, lambda b, i: (0,) * a.ndim)
    per_b = lambda a: pl.BlockSpec((1,) + a.shape[1:], lambda b, i: (b,) + (0,) * (a.ndim - 1))
    tile = lambda w: pl.BlockSpec((1, tm, w), lambda b, i: (b, i, 0))
    return pl.pallas_call(
        _post_body,
        grid=(B, n_s),
        in_specs=[tile(D_MODEL), tile(NSA_WIDTH), tile(RET_WIDTH), full(wout), per_b(mk), per_b(mv), full(gx),
                  full(wq), full(qgain), full(avg), full(wo), full(gf), full(wr), full(br), full(tri)],
        out_specs=[tile(D_MODEL), tile(D_MODEL), tile(LANES),
                   pl.BlockSpec((1, 8, LANES), lambda b, i: (b * n_s + i, 0, 0))],
        out_shape=[jax.ShapeDtypeStruct((B, S, D_MODEL), F32), jax.ShapeDtypeStruct((B, S, D_MODEL), BF16),
                   jax.ShapeDtypeStruct((B, S, LANES), F32), jax.ShapeDtypeStruct((B * n_s, 8, LANES), F32)],
        scratch_shapes=[pltpu.VMEM(wout.shape, BF16), pltpu.VMEM(wq.shape, BF16), pltpu.VMEM(wo.shape, BF16)],
        compiler_params=_cparams(2),
        name="post_mixer",
    )(x, oa, ob, wout, mk, mv, gx, wq, qgain, avg, wo, gf, wr, br, tri)


def _row_copy(src, dst, sem):
    return pltpu.make_async_copy(src, dst, sem)


def _run_pieces(n, max_piece, fn):
    b = RUN_ALIGN
    while b <= max_piece:
        pl.when((n & b) != 0)(functools.partial(fn, n & (-2 * b), b))
        b *= 2


def _move_groups(i, gmap_ref, n_loc, copy):
    n_groups = n_loc // RUN_ALIGN
    for j in range(n_groups):
        glob = pl.multiple_of(gmap_ref[i * n_groups + j], RUN_ALIGN)
        copy(pl.ds(j * RUN_ALIGN, RUN_ALIGN), pl.ds(glob, RUN_ALIGN)).start()


def _local_positions(route, loff_row):
    lane = lax.broadcasted_iota(jnp.int32, route.shape, 1).astype(F32)
    pos = []
    for s in range(2):
        base = jnp.sum(jnp.where(lane == route[:, s:s + 1], loff_row, 0.0), axis=-1, keepdims=True)
        pos.append(base + route[:, 4 + s:5 + s])
    return pos


def _scatter_body(gmap_ref, tstart_ref, tlen_ref, nact_ref,
                  h_ref, route_ref, lofff_ref, xs_ref, xloc, zbuf, sems):
    i = pl.program_id(0)
    tm = h_ref.shape[0]
    n_loc = xloc.shape[1]
    slot = i & 1
    sem = sems.at[0]

    def tails(start):
        def per_expert(e, carry):
            n = tlen_ref[e]
            st = tstart_ref[e]

            def piece(off, size):
                c = _row_copy(zbuf.at[pl.ds(0, size)], xs_ref.at[pl.ds(pl.multiple_of(st + off, RUN_ALIGN), size)], sem)
                c.start() if start else c.wait()

            _run_pieces(n, MOE_RB // 2, piece)
            return carry

        lax.fori_loop(0, N_EXPERTS, per_expert, 0)

    def unused(start):
        rows = zbuf.shape[0]

        def per_unit(u, carry):
            c = _row_copy(zbuf, xs_ref.at[pl.ds(pl.multiple_of(u * rows, rows), rows)], sem)
            c.start() if start else c.wait()
            return carry

        lax.fori_loop(nact_ref[0] * (MOE_RB // rows), xs_ref.shape[0] // rows, per_unit, 0)

    @pl.when(i == 0)
    def _():
        zbuf[...] = jnp.zeros(zbuf.shape, zbuf.dtype)
        tails(True)
        unused(True)
        tails(False)
        unused(False)

    pos = _local_positions(route_ref[...], lofff_ref[0, 0:1, :])
    col = lax.broadcasted_iota(jnp.int32, (tm, n_loc), 1).astype(F32)
    perm_t = jnp.where((col == pos[0]) | (col == pos[1]), 1.0, 0.0).astype(BF16)
    xloc[slot] = _dot_tn(perm_t, h_ref[...]).astype(BF16)

    def wait_slot(s):
        _row_copy(xloc.at[s], xs_ref.at[pl.ds(0, n_loc)], sems.at[s]).wait()

    pl.when(i > 0)(lambda: wait_slot(1 - slot))
    _move_groups(i, gmap_ref, n_loc, lambda loc, glob: _row_copy(xloc.at[slot, loc], xs_ref.at[glob], sems.at[slot]))
    pl.when(i == pl.num_programs(0) - 1)(lambda: wait_slot(slot))


def _scatter_rows(tables, h2d, route2d, loff_f, n_rows):
    T = h2d.shape[0]
    tm = MOE_TM
    n_loc = MOE_NLOC
    tile = lambda w: pl.BlockSpec((tm, w), lambda i, *_: (i, 0))
    grid_spec = pltpu.PrefetchScalarGridSpec(
        num_scalar_prefetch=4,
        grid=(T // tm,),
        in_specs=[tile(D_MODEL), tile(LANES), pl.BlockSpec((1, 8, LANES), lambda i, *_: (i, 0, 0))],
        out_specs=pl.BlockSpec(memory_space=pl.ANY),
        scratch_shapes=[pltpu.VMEM((2, n_loc, D_MODEL), BF16), pltpu.VMEM((MOE_RB // 2, D_MODEL), BF16),
                        pltpu.SemaphoreType.DMA((2,))],
    )
    return pl.pallas_call(
        _scatter_body,
        grid_spec=grid_spec,
        out_shape=jax.ShapeDtypeStruct((n_rows, D_MODEL), BF16),
        compiler_params=_cparams(1),
        name="moe_scatter",
    )(*tables, h2d, route2d, loff_f)


def _expert_body(blk0_ref, nblk_ref, n_act_ref, xs_ref, wg_ref, wu_ref, wd_ref, ys_ref,
                 wg_b, wu_b, wd_b, xbuf, ybuf, sem_in, sem_out):
    e = pl.program_id(0)
    n_slots, rb = xbuf.shape[0], xbuf.shape[1]
    ahead = n_slots - 2
    n_act = n_act_ref[0]
    b0 = blk0_ref[e]

    def rows(g):
        return pl.ds(pl.multiple_of(g * rb, rb), rb)

    def x_copy(g, slot):
        return _row_copy(xs_ref.at[rows(g)], xbuf.at[slot], sem_in.at[slot])

    def y_copy(g, slot):
        return _row_copy(ybuf.at[slot], ys_ref.at[rows(g)], sem_out.at[slot])

    @pl.when(e == 0)
    def _():
        for k in range(ahead):
            pl.when(k < n_act)(lambda k=k: x_copy(k, k).start())

    wg_b[...] = wg_ref[0].astype(BF16)
    wu_b[...] = wu_ref[0].astype(BF16)
    wd_b[...] = wd_ref[0].astype(BF16)

    def blocks(g0, count):
        gs = [g0 + c for c in range(count)]
        slots = [g & (n_slots - 1) for g in gs]
        for c in range(count):
            x_copy(gs[c], slots[c]).wait()
        for c in range(count):
            nxt = gs[c] + ahead
            pl.when(nxt < n_act)(lambda nxt=nxt: x_copy(nxt, nxt & (n_slots - 1)).start())
        x = [xbuf[slots[c]] for c in range(count)]
        a = [_dot(x[c], wg_b[...]) for c in range(count)]
        b = [_dot(x[c], wu_b[...]) for c in range(count)]
        h = [(_silu(a[c]) * b[c]).astype(BF16) for c in range(count)]
        y = [_dot(h[c], wd_b[...]).astype(BF16) for c in range(count)]
        for c in range(count):
            pl.when(gs[c] >= n_slots)(lambda c=c: y_copy(gs[c] - n_slots, slots[c]).wait())
        for c in range(count):
            ybuf[slots[c]] = y[c]
            y_copy(gs[c], slots[c]).start()

    nb = nblk_ref[e]

    def pair(j, carry):
        blocks(b0 + 2 * j, 2)
        return carry

    lax.fori_loop(0, lax.shift_right_logical(nb, 1), pair, 0)
    pl.when((nb & 1) == 1)(lambda: blocks(b0 + nb - 1, 1))

    @pl.when(e == pl.num_programs(0) - 1)
    def _():
        for k in range(1, n_slots + 1):
            pl.when(n_act >= k)(lambda k=k: y_copy(n_act - k, (n_act - k) & (n_slots - 1)).wait())
        ybuf[0] = jnp.zeros(ybuf.shape[1:], ybuf.dtype)
        n_blocks = ys_ref.shape[0] // rb

        def fill(start):
            def per_block(g, carry):
                c = y_copy(g, 0)
                c.start() if start else c.wait()
                return carry

            lax.fori_loop(n_act, n_blocks, per_block, 0)

        fill(True)
        fill(False)


def _experts(blk0, nblk, n_act, xs, n_rows, wg, wu, wd):
    rb = MOE_RB
    weight = lambda a: pl.BlockSpec((1,) + a.shape[1:], lambda e, *_: (e, 0, 0))
    grid_spec = pltpu.PrefetchScalarGridSpec(
        num_scalar_prefetch=3,
        grid=(N_EXPERTS,),
        in_specs=[pl.BlockSpec(memory_space=pl.ANY), weight(wg), weight(wu), weight(wd)],
        out_specs=pl.BlockSpec(memory_space=pl.ANY),
        scratch_shapes=[pltpu.VMEM((D_MODEL, EXPERT_FF), BF16), pltpu.VMEM((D_MODEL, EXPERT_FF), BF16),
                        pltpu.VMEM((EXPERT_FF, D_MODEL), BF16), pltpu.VMEM((MOE_SLOTS, rb, D_MODEL), BF16),
                        pltpu.VMEM((MOE_SLOTS, rb, D_MODEL), BF16), pltpu.SemaphoreType.DMA((MOE_SLOTS,)),
                        pltpu.SemaphoreType.DMA((MOE_SLOTS,))],
    )
    return pl.pallas_call(
        _expert_body,
        grid_spec=grid_spec,
        out_shape=jax.ShapeDtypeStruct((n_rows, D_MODEL), BF16),
        compiler_params=_cparams(1),
        name="moe_experts",
    )(blk0, nblk, n_act, xs, wg, wu, wd)


def _combine_body(gmap_ref, x_ref, route_ref, lofff_ref, ys_ref, o_ref, yloc, sems):
    i = pl.program_id(0)
    tm = x_ref.shape[0]
    n_loc = yloc.shape[1]
    slot = i & 1

    def fetch(tile, s):
        _move_groups(tile, gmap_ref, n_loc, lambda loc, glob: _row_copy(ys_ref.at[glob], yloc.at[s, loc], sems.at[s]))

    pl.when(i == 0)(lambda: fetch(i, slot))
    _row_copy(ys_ref.at[pl.ds(0, n_loc)], yloc.at[slot], sems.at[slot]).wait()
    pl.when(i + 1 < pl.num_programs(0))(lambda: fetch(i + 1, 1 - slot))

    route = route_ref[...]
    pos = _local_positions(route, lofff_ref[0, 0:1, :])
    col = lax.broadcasted_iota(jnp.int32, (tm, n_loc), 1).astype(F32)
    perm_w = jnp.where(col == pos[0], route[:, 2:3], jnp.where(col == pos[1], route[:, 3:4], 0.0)).astype(BF16)
    o_ref[...] = x_ref[...] + _dot(perm_w, yloc[slot])


def _combine(tables, x2d, route2d, loff_f, ys):
    T = x2d.shape[0]
    tm = MOE_TM
    n_loc = MOE_NLOC
    tile = lambda w: pl.BlockSpec((tm, w), lambda i, *_: (i, 0))
    grid_spec = pltpu.PrefetchScalarGridSpec(
        num_scalar_prefetch=1,
        grid=(T // tm,),
        in_specs=[tile(D_MODEL), tile(LANES), pl.BlockSpec((1, 8, LANES), lambda i, *_: (i, 0, 0)),
                  pl.BlockSpec(memory_space=pl.ANY)],
        out_specs=tile(D_MODEL),
        scratch_shapes=[pltpu.VMEM((2, n_loc, D_MODEL), BF16), pltpu.SemaphoreType.DMA((2,))],
    )
    return pl.pallas_call(
        _combine_body,
        grid_spec=grid_spec,
        out_shape=jax.ShapeDtypeStruct((T, D_MODEL), F32),
        compiler_params=_cparams(1),
        name="moe_combine",
    )(*tables, x2d, route2d, loff_f, ys)


def _compress_weights(pos, w2):
    eye = jnp.eye(NSA_GROUPS, dtype=F32)
    w2b = jnp.einsum('hd,gk->ghkd', w2, eye).reshape(NSA_GROUPS * CMP_HIDDEN, LANES)
    posb = jnp.tile(pos, (1, NSA_GROUPS)).reshape(2, 1, CMP_STRIDE * LANES)
    return posb, w2b.astype(BF16)


def _dup2(g):
    return jnp.tile(g.reshape(1, HEAD_DIM), (1, 2))


def kernel(x, mem, mix_norm, w_in, nsa_q_norm, nsa_kcmp_norm, nsa_ksel_norm, nsa_kwin_norm, cmp_pos_k, cmp_pos_v, cmp_k_w1, cmp_k_w2, cmp_v_w1, cmp_v_w2, nsa_out_norm, ret_out_norm, w_out, mem_x_norm, mem_kv_norm, mem_wq, mem_wkv, mem_q_norm, mem_k_norm, mem_wo, ffn_norm, router_group_w, router_group_b, router_expert_w, router_expert_b, exp_w_gate, exp_w_up, exp_w_down):
    B, S, D = x.shape
    T = B * S
    depth = mix_norm.shape[0]
    for l in range(depth):
        proj = _proj(x.reshape(T, D), mix_norm[l].reshape(1, D), w_in[l].T).reshape(B, S, PROJ_PAD)
        pk, w2k = _compress_weights(cmp_pos_k[l], cmp_k_w2[l])
        pv, w2v = _compress_weights(cmp_pos_v[l], cmp_v_w2[l])
        gains = jnp.stack([_dup2(nsa_kcmp_norm[l]), _dup2(nsa_ksel_norm[l]), _dup2(nsa_kwin_norm[l])])
        kcmp, vcmp, ks, vs, kw, vw = _nsa_prep(proj, jnp.stack([pk, pv]), cmp_k_w1[l], cmp_v_w1[l],
                                               jnp.stack([w2k, w2v]), gains)
        o_a = _nsa_attn(proj, kcmp, vcmp, ks, vs, kw, vw,
                        jnp.tile(nsa_q_norm[l].reshape(1, HEAD_DIM), (1, NSA_HEADS)), gains,
                        nsa_out_norm[l].reshape(1, NSA_WIDTH))
        o_b = _retention(proj, ret_out_norm[l].reshape(1, RET_WIDTH))
        mk, mv = _mem_prep(mem, mem_kv_norm[l].reshape(1, D), mem_wkv[l].astype(BF16),
                           jnp.tile(mem_k_norm[l].reshape(1, HEAD_DIM), (1, MEM_HEADS)))
        w_r = jnp.concatenate([router_group_w[l],
                               router_expert_w[l].transpose(1, 0, 2).reshape(D, N_EXPERTS),
                               jnp.zeros((D, LANES - N_GROUPS - N_EXPERTS), F32)], axis=1).astype(BF16)
        b_r = jnp.concatenate([router_group_b[l], router_expert_b[l].reshape(N_EXPERTS),
                               jnp.zeros((LANES - N_GROUPS - N_EXPERTS,), F32)]).reshape(1, LANES)
        x2, hf, route, counts = _post(
            x, o_a, o_b, w_out[l], mk, mv, mem_x_norm[l].reshape(1, D), mem_wq[l],
            jnp.tile(mem_q_norm[l].reshape(1, HEAD_DIM), (1, MEM_HEADS)), mem_wo[l],
            ffn_norm[l].reshape(1, D), w_r, b_r)
        route2d = route.reshape(T, LANES)
        n_tiles = T // MOE_TM
        cnt = counts[:, 0, :N_EXPERTS].astype(jnp.int32)
        cnt = (cnt + RUN_ALIGN - 1) // RUN_ALIGN * RUN_ALIGN
        loff = jnp.cumsum(cnt, axis=1) - cnt
        total = jnp.sum(cnt, axis=0)
        padded = (total + MOE_RB - 1) // MOE_RB * MOE_RB
        pend = jnp.cumsum(padded)
        pstart = pend - padded
        goff = pstart[None, :] + jnp.cumsum(cnt, axis=0) - cnt
        n_rows = 2 * T + n_tiles * N_EXPERTS * RUN_ALIGN + N_EXPERTS * MOE_RB
        n_act = (pend[-1:] // MOE_RB).astype(jnp.int32)
        loff_f = jnp.broadcast_to(jnp.pad(loff.astype(F32), ((0, 0), (0, LANES - N_EXPERTS)))[:, None, :],
                                  (n_tiles, 8, LANES))
        grp_row = jnp.arange(MOE_NLOC // RUN_ALIGN, dtype=jnp.int32) * RUN_ALIGN
        inside = ((loff[:, None, :] <= grp_row[None, :, None])
                  & (grp_row[None, :, None] < (loff + cnt)[:, None, :])).astype(jnp.int32)
        shift = jnp.sum(inside * (goff - loff)[:, None, :], axis=2)
        used = jnp.sum(inside, axis=2) > 0
        gmap_scatter = jnp.where(used, shift + grp_row[None, :], n_rows + grp_row[None, :]).reshape(-1)
        gmap_gather = jnp.where(used, shift + grp_row[None, :], 0).reshape(-1)
        xs = _scatter_rows((gmap_scatter, pstart + total, padded - total, n_act), hf.reshape(T, D), route2d, loff_f,
                           n_rows + MOE_NLOC)
        ys = _experts(pstart // MOE_RB, padded // MOE_RB, n_act, xs, n_rows,
                      exp_w_gate[l], exp_w_up[l], exp_w_down[l])
        x = _combine((gmap_gather,), x2.reshape(T, D), route2d, loff_f, ys).reshape(B, S, D)
    return x
```

```python
import functools

import numpy as np
import jax
import jax.numpy as jnp
from jax import lax
from jax.experimental import pallas as pl
from jax.experimental.pallas import tpu as pltpu

F32 = jnp.float32
BF16 = jnp.bfloat16

D_MODEL = 1024
HEAD_DIM = 64
LANES = 128
NSA_HEADS = 8
NSA_GROUPS = 2
NSA_WIDTH = NSA_HEADS * HEAD_DIM
CMP_BLOCK = 32
CMP_STRIDE = 16
CMP_HIDDEN = 2 * HEAD_DIM
SEL_BLOCK = 64
SEL_TOPK = 8
WINDOW = 512
RET_HEADS = 8
RET_WIDTH = RET_HEADS * HEAD_DIM
RET_CHUNK = 128
ROPE_BASE = 10000.0
MEM_HEADS = 4
MEM_WIDTH = MEM_HEADS * HEAD_DIM
N_GROUPS = 4
EXPERTS_PER_GROUP = 8
N_EXPERTS = N_GROUPS * EXPERTS_PER_GROUP
EXPERT_FF = D_MODEL // 4
EPS = 1e-6
NEG_INF = -1e30
BELOW_ALL = -3e38
MAX_FIXED_SHIFT = 40.0

COL_QA = 0
COL_QR, COL_KR, COL_VR, COL_GR = 512, 1024, 1536, 2048
COL_KVC, COL_KSV, COL_KWV = 2560, 2816, 3072
COL_GATE = 3328
PROJ_PAD = 3456
PROJ_SPLITS = (NSA_WIDTH, NSA_WIDTH + 6 * NSA_GROUPS * HEAD_DIM, NSA_WIDTH + 6 * NSA_GROUPS * HEAD_DIM + 3 * NSA_HEADS)

PROJ_TM = 512
RET_UNROLL = 4
NSA_TQ = 256
SEL_KC = 512
POST_CHAINS = 2
MOE_TM = 512
MOE_RB = 256
MOE_SLOTS = 8
RUN_ALIGN = 16
MOE_NLOC = 2 * MOE_TM + N_EXPERTS * RUN_ALIGN
VMEM_LIMIT = 56 * 1024 * 1024


def _cparams(n_axes):
    return pltpu.CompilerParams(dimension_semantics=("arbitrary",) * n_axes,
                                vmem_limit_bytes=VMEM_LIMIT)


def _dot(a, b):
    return jnp.dot(a, b, preferred_element_type=F32)


def _dot_nt(a, b):
    return lax.dot_general(a, b, (((1,), (1,)), ((), ())), preferred_element_type=F32)


def _dot_tn(a, b):
    return lax.dot_general(a, b, (((0,), (0,)), ((), ())), preferred_element_type=F32)


def _rms_full(x, g):
    ms = jnp.mean(x * x, axis=-1, keepdims=True)
    return x * lax.rsqrt(ms + EPS) * g


def _seg_mean(x, avg):
    return _dot(x.astype(BF16), avg)


def _silu(x):
    return x * (1.0 / (1.0 + jnp.exp(-x)))


def _sigmoid(x):
    return 1.0 / (1.0 + jnp.exp(-x))


def _block_avg(width):
    i = np.arange(width)
    return ((i[:, None] // HEAD_DIM == i[None, :] // HEAD_DIM) / HEAD_DIM).astype(np.float32)


def _proj_body(x_ref, g_ref, wt_ref, o_ref, w_scr):
    @pl.when(pl.program_id(0) == 0)
    def _():
        kv0, gate0, ret0 = PROJ_SPLITS
        w_scr[COL_QA:COL_QR] = wt_ref[0:kv0].astype(BF16)
        w_scr[COL_QR:COL_KVC] = wt_ref[ret0:].astype(BF16)
        w_scr[COL_KVC:COL_GATE] = wt_ref[kv0:gate0].astype(BF16)
        pad = jnp.zeros((PROJ_PAD - COL_GATE - (ret0 - gate0), wt_ref.shape[1]), F32)
        w_scr[COL_GATE:] = jnp.concatenate([wt_ref[gate0:ret0], pad], axis=0).astype(BF16)

    h = _rms_full(x_ref[...], g_ref[...]).astype(BF16)
    step = PROJ_PAD // 3
    for j in range(3):
        o_ref[:, j * step:(j + 1) * step] = _dot_nt(h, w_scr[j * step:(j + 1) * step]).astype(BF16)


def _proj(x2d, g, w_t):
    T = x2d.shape[0]
    assert w_t.shape[0] == PROJ_SPLITS[2] + 4 * RET_WIDTH
    return pl.pallas_call(
        _proj_body,
        grid=(T // PROJ_TM,),
        in_specs=[pl.BlockSpec((PROJ_TM, D_MODEL), lambda i: (i, 0)),
                  pl.BlockSpec((1, D_MODEL), lambda i: (0, 0)),
                  pl.BlockSpec(w_t.shape, lambda i: (0, 0))],
        out_specs=pl.BlockSpec((PROJ_TM, PROJ_PAD), lambda i: (i, 0)),
        out_shape=jax.ShapeDtypeStruct((T, PROJ_PAD), BF16),
        scratch_shapes=[pltpu.VMEM((PROJ_PAD, D_MODEL), BF16)],
        compiler_params=_cparams(1),
        name="proj",
    )(x2d, g, w_t)


def _dup_groups(x):
    lane = lax.broadcasted_iota(jnp.int32, x.shape, 1)
    xs = pltpu.roll(x, HEAD_DIM, axis=1)
    lo = lane < HEAD_DIM
    return jnp.where(lo, x, xs), jnp.where(lo, xs, x)


def _ones_groups(x):
    lane = lax.broadcasted_iota(jnp.int32, x.shape, 1)
    lo = lane < HEAD_DIM
    xs = pltpu.roll(x, HEAD_DIM, axis=1)
    return jnp.where(lo, x, 1.0), jnp.where(lo, 1.0, xs), jnp.where(lo, xs, 1.0), jnp.where(lo, 1.0, x)


def _nsa_prep_body(kvc_ref, ksv_ref, kwv_ref, pos_ref, w1k_ref, w1v_ref, w2_ref, gain_ref, avg_ref,
                   kcmp_ref, vcmp_ref, ks_ref, vs_ref, kw_ref, vw_ref, scr_k, scr_v, w1_ref):
    @pl.when(pl.program_id(0) == 0)
    def _():
        zero = jnp.zeros((HEAD_DIM, CMP_HIDDEN), BF16)
        for j, src in ((0, w1k_ref), (1, w1v_ref)):
            for l in range(CMP_BLOCK):
                piece = src[l * HEAD_DIM:(l + 1) * HEAD_DIM, :].astype(BF16)
                r0 = (l % CMP_STRIDE) * LANES
                w1_ref[j, l // CMP_STRIDE, r0:r0 + HEAD_DIM, :] = jnp.concatenate([piece, zero], axis=1)
                w1_ref[j, l // CMP_STRIDE, r0 + HEAD_DIM:r0 + LANES, :] = jnp.concatenate([zero, piece], axis=1)

    avg = avg_ref[...]
    n_c = scr_k.shape[0] // CMP_STRIDE
    scr_k[...] = kvc_ref[0, :, 0:LANES].astype(F32)
    scr_v[...] = kvc_ref[0, :, LANES:2 * LANES].astype(F32)
    for j, out_ref, scr in ((0, kcmp_ref, scr_k), (1, vcmp_ref, scr_v)):
        ycat = jnp.concatenate(
            [scr[pl.ds(l, n_c, stride=CMP_STRIDE), :] for l in range(CMP_STRIDE)], axis=1)
        first = _dot((ycat + pos_ref[j, 0]).astype(BF16), w1_ref[j, 0])
        second = _dot((ycat + pos_ref[j, 1]).astype(BF16), w1_ref[j, 1])
        hidden = first + pltpu.roll(second, n_c - 1, axis=0)
        cmp_tok = _dot(_silu(hidden).astype(BF16), w2_ref[j])
        if j == 0:
            ms = _seg_mean(cmp_tok * cmp_tok, avg)
            cmp_tok = cmp_tok * lax.rsqrt(ms + EPS) * gain_ref[0]
        for i, piece in enumerate(_dup_groups(cmp_tok) if j == 0 else _ones_groups(cmp_tok)):
            out_ref[0, i] = piece.astype(BF16)

    for src_ref, k_out, v_out, gi in ((ksv_ref, ks_ref, vs_ref, 1), (kwv_ref, kw_ref, vw_ref, 2)):
        k = src_ref[0, :, 0:LANES].astype(F32)
        ms = _seg_mean(k * k, avg)
        k = k * lax.rsqrt(ms + EPS) * gain_ref[gi]
        d0, d1 = _dup_groups(k)
        k_out[0, 0] = d0.astype(BF16)
        k_out[0, 1] = d1.astype(BF16)
        for i, piece in enumerate(_ones_groups(src_ref[0, :, LANES:2 * LANES].astype(F32))):
            v_out[0, i] = piece.astype(BF16)


def _nsa_prep(proj3, pos, w1k, w1v, w2, gains):
    B, S, _ = proj3.shape
    n_c = S // CMP_STRIDE
    avg = jnp.asarray(_block_avg(LANES), BF16)
    col = lambda c: pl.BlockSpec((1, S, 2 * LANES), lambda b: (b, 0, c // (2 * LANES)))
    full = lambda a: pl.BlockSpec(a.shape, lambda b: (0,) * a.ndim)
    cmp_spec = pl.BlockSpec((1, NSA_GROUPS, n_c, LANES), lambda b: (b, 0, 0, 0))
    seq_spec = pl.BlockSpec((1, NSA_GROUPS, S, LANES), lambda b: (b, 0, 0, 0))
    cmp_shape = jax.ShapeDtypeStruct((B, NSA_GROUPS, n_c, LANES), BF16)
    seq_shape = jax.ShapeDtypeStruct((B, NSA_GROUPS, S, LANES), BF16)

    def val(a):
        if isinstance(a, jax.ShapeDtypeStruct):
            return jax.ShapeDtypeStruct((a.shape[0], 2 * a.shape[1]) + a.shape[2:], a.dtype)
        return pl.BlockSpec((1, 2 * NSA_GROUPS) + a.block_shape[2:], lambda b: (b, 0, 0, 0))

    return pl.pallas_call(
        _nsa_prep_body,
        grid=(B,),
        in_specs=[col(COL_KVC), col(COL_KSV), col(COL_KWV), full(pos), full(w1k), full(w1v), full(w2), full(gains),
                  full(avg)],
        out_specs=[cmp_spec, val(cmp_spec), seq_spec, val(seq_spec), seq_spec, val(seq_spec)],
        out_shape=[cmp_shape, val(cmp_shape), seq_shape, val(seq_shape), seq_shape, val(seq_shape)],
        scratch_shapes=[pltpu.VMEM((S, LANES), F32), pltpu.VMEM((S, LANES), F32),
                        pltpu.VMEM((2, 2, CMP_STRIDE * LANES, NSA_GROUPS * CMP_HIDDEN), BF16)],
        compiler_params=_cparams(1),
        name="nsa_prep",
    )(proj3, proj3, proj3, pos, w1k, w1v, w2, gains, avg)


def _nsa_attn_body(q_ref, gate_ref, kcmp_ref, vcmp_ref, ks_ref, vs_ref, kw_ref, vw_ref,
                   qgain_ref, kgain_ref, ogain_ref, avgq_ref, avgo_ref, msct_ref, esel_ref, egate_ref, wbias_ref,
                   dbias_ref, cbias_ref,
                   o_ref, m_scr, acc_scr):
    tq = q_ref.shape[1]
    n_cmp = kcmp_ref.shape[2]
    n_sel = msct_ref.shape[0]
    kc_len = esel_ref.shape[2]
    rows = 4 * tq
    qi = pl.program_id(1)
    q0 = qi * tq

    q = q_ref[0].astype(F32)
    ms = _seg_mean(q * q, avgq_ref[...])
    qn = q * lax.rsqrt(ms + EPS) * qgain_ref[...] * (HEAD_DIM ** -0.5)

    gate_sig = _sigmoid(gate_ref[0].astype(F32)).astype(BF16)
    gates = [_dot(gate_sig, egate_ref[j]) for j in range(3)]

    lane_q = lax.broadcasted_iota(jnp.int32, (tq, LANES), 1)
    lo_q = lane_q < HEAD_DIM

    blk = lax.broadcasted_iota(jnp.int32, (n_sel, tq), 0)
    cur = lax.shift_right_logical(q0 + lax.broadcasted_iota(jnp.int32, (n_sel, tq), 1), int(np.log2(SEL_BLOCK)))
    forced = (blk == 0) | (blk == cur) | (blk == cur - 1)
    future = blk > cur
    blk_f = blk.astype(F32)

    def heads4(x):
        return jnp.concatenate([x] * 4, axis=0)

    def weighted_values(p, v_low, v_high):
        return jnp.concatenate([_dot(p[:2 * tq], v_low), _dot(p[2 * tq:], v_high)], axis=0)

    def normalised_pairs(acc, guard):
        out = []
        for p in range(2):
            low = acc[p * tq:(p + 1) * tq]
            high = acc[(2 + p) * tq:(3 + p) * tq]
            den = pltpu.roll(jnp.where(lo_q, high, low), HEAD_DIM, axis=1)
            if guard:
                den = jnp.maximum(den, 1e-30)
            out.append(jnp.where(lo_q, low, high) / den)
        return out

    groups = range(NSA_GROUPS)
    qs = []
    for g in groups:
        slabs = [qn[:, (2 * g + p) * LANES:(2 * g + p + 1) * LANES] for p in range(2)]
        qs.append(jnp.concatenate(
            [jnp.where(lo_q, slabs[0], 0.0), jnp.where(lo_q, slabs[1], 0.0),
             jnp.where(lo_q, 0.0, slabs[0]), jnp.where(lo_q, 0.0, slabs[1])], axis=0).astype(BF16))

    def compressed_and_select(shift):
        s_c = [_dot_nt(qs[g], kcmp_ref[0, g]) for g in groups]
        if shift is None:
            r_c = lax.broadcasted_iota(jnp.int32, (rows, n_cmp), 0)
            c_c = lax.broadcasted_iota(jnp.int32, (rows, n_cmp), 1)
            cmask = (c_c * CMP_STRIDE + (CMP_BLOCK - 1)) <= q0 + (r_c & (tq - 1))
            s_c = [jnp.where(cmask, s_c[g], NEG_INF) for g in groups]
            e_c = [jnp.where(cmask, jnp.exp(s_c[g] - jnp.max(s_c[g], axis=-1, keepdims=True)), 0.0) for g in groups]
        else:
            bias = heads4(cbias_ref[qi] - shift)
            e_c = [jnp.exp(s_c[g] + bias) for g in groups]
        e_b = [e_c[g].astype(BF16) for g in groups]
        acc_c = [weighted_values(e_b[g], vcmp_ref[0, 2 * g], vcmp_ref[0, 2 * g + 1]) for g in groups]
        ones = jnp.ones((8, n_cmp), BF16)
        imp = []
        for g in groups:
            num = [_dot_nt(msct_ref[...], e_b[g][h * tq:(h + 1) * tq]) for h in range(4)]
            den = [_dot_nt(ones, e_b[g][h * tq:(h + 1) * tq])[0:1] for h in range(4)]
            parts = [num[h] / jnp.maximum(den[h], 1e-30) for h in range(4)]
            imp.append((parts[0] + parts[1]) + (parts[2] + parts[3]))
        v = [jnp.where(forced, BELOW_ALL, jnp.where(future, NEG_INF, imp[g])) for g in groups]
        sel = [jnp.where(forced, 1.0, 0.0) for g in groups]
        for _ in range(SEL_TOPK - 3):
            mx = [jnp.max(v[g], axis=0, keepdims=True) for g in groups]
            first = [jnp.min(jnp.where(v[g] == mx[g], blk_f, float(LANES)), axis=0, keepdims=True) for g in groups]
            pick = [blk_f == first[g] for g in groups]
            sel = [jnp.where(pick[g], 1.0, sel[g]) for g in groups]
            v = [jnp.where(pick[g], BELOW_ALL, v[g]) for g in groups]
        return [normalised_pairs(acc_c[g], True) for g in groups], [sel[g].astype(BF16) for g in groups]

    n_before = lax.shift_right_logical(q0, int(np.log2(kc_len)))
    causal = dbias_ref[qi & (kc_len // tq - 1)]
    w0 = pl.multiple_of(jnp.maximum(q0 - WINDOW, 0), tq)
    n_w = WINDOW + tq
    w_case = jnp.minimum(qi, WINDOW // tq)

    def sel_keys(ref, g, kc):
        return ref[0, g, pl.ds(pl.multiple_of(kc * kc_len, kc_len), kc_len), :]

    def sel_scores(sel_b, g, kc, causal_bias, shift):
        chosen = _dot_tn(sel_b[g], esel_ref[kc])
        bias = (chosen - 1.0) * (-NEG_INF)
        if causal_bias is not None:
            bias = bias + causal_bias
        if shift is not None:
            bias = bias - shift
        return _dot_nt(qs[g], sel_keys(ks_ref, g, kc)) + heads4(bias)

    def win_scores(g, shift):
        bias = wbias_ref[w_case] if shift is None else wbias_ref[w_case] - shift
        return _dot_nt(qs[g], kw_ref[0, g, pl.ds(w0, n_w), :]) + heads4(bias)

    def win_values(g):
        return vw_ref[0, 2 * g, pl.ds(w0, n_w), :], vw_ref[0, 2 * g + 1, pl.ds(w0, n_w), :]

    def sel_values(g, kc):
        return sel_keys(vs_ref, 2 * g, kc), sel_keys(vs_ref, 2 * g + 1, kc)

    def finish(cmp_s, acc_w):
        for g in groups:
            sel_s = normalised_pairs(acc_scr[g], False)
            win_s = normalised_pairs(acc_w[g], False)
            for p in range(2):
                cols = slice((2 * g + p) * LANES, (2 * g + p + 1) * LANES)
                mix = gates[0][:, cols] * cmp_s[g][p] + gates[1][:, cols] * sel_s[p] + gates[2][:, cols] * win_s[p]
                ms_o = _seg_mean(mix * mix, avgo_ref[...])
                o_ref[0, :, cols] = (mix * lax.rsqrt(ms_o + EPS) * ogain_ref[:, cols]).astype(BF16)

    def fixed_shift_path(shift):
        cmp_s, sel_b = compressed_and_select(shift)

        def probs(s):
            return jnp.exp(s).astype(BF16)

        for g in groups:
            acc_scr[g] = jnp.zeros(acc_scr.shape[1:], F32)

        def before(kc, carry):
            s = [sel_scores(sel_b, g, kc, None, shift) for g in groups]
            p = [probs(s[g]) for g in groups]
            for g in groups:
                acc_scr[g] = acc_scr[g] + weighted_values(p[g], *sel_values(g, kc))
            return carry

        lax.fori_loop(0, n_before, before, 0)
        s_d0 = sel_scores(sel_b, 0, n_before, causal, shift)
        s_d1 = sel_scores(sel_b, 1, n_before, causal, shift)
        p_d0 = probs(s_d0)
        s_w0 = win_scores(0, shift)
        acc_scr[0] = acc_scr[0] + weighted_values(p_d0, *sel_values(0, n_before))
        p_d1 = probs(s_d1)
        s_w1 = win_scores(1, shift)
        acc_scr[1] = acc_scr[1] + weighted_values(p_d1, *sel_values(1, n_before))
        p_w0 = probs(s_w0)
        acc_w0 = weighted_values(p_w0, *win_values(0))
        p_w1 = probs(s_w1)
        acc_w1 = weighted_values(p_w1, *win_values(1))
        finish(cmp_s, [acc_w0, acc_w1])

    def online_path():
        cmp_s, sel_b = compressed_and_select(None)
        for g in groups:
            m_scr[g] = jnp.full(m_scr.shape[1:], NEG_INF, F32)
            acc_scr[g] = jnp.zeros(acc_scr.shape[1:], F32)

        def sel_softmax(g, s):
            m_old = m_scr[g]
            m_new = jnp.maximum(m_old, jnp.max(s, axis=-1, keepdims=True))
            m_scr[g] = m_new
            return jnp.exp(s - m_new).astype(BF16), jnp.exp(m_old - m_new)

        def sel_accumulate(g, kc, p, alpha):
            acc_scr[g] = alpha * acc_scr[g] + weighted_values(p, *sel_values(g, kc))

        def win_softmax(s):
            return jnp.exp(s - jnp.max(s, axis=-1, keepdims=True)).astype(BF16)

        def before(kc, carry):
            s = [sel_scores(sel_b, g, kc, None, None) for g in groups]
            pa = [sel_softmax(g, s[g]) for g in groups]
            for g in groups:
                sel_accumulate(g, kc, *pa[g])
            return carry

        lax.fori_loop(0, n_before, before, 0)
        s_d0 = sel_scores(sel_b, 0, n_before, causal, None)
        s_d1 = sel_scores(sel_b, 1, n_before, causal, None)
        pa0 = sel_softmax(0, s_d0)
        s_w0 = win_scores(0, None)
        sel_accumulate(0, n_before, *pa0)
        pa1 = sel_softmax(1, s_d1)
        s_w1 = win_scores(1, None)
        sel_accumulate(1, n_before, *pa1)
        acc_w0 = weighted_values(win_softmax(s_w0), *win_values(0))
        acc_w1 = weighted_values(win_softmax(s_w1), *win_values(1))
        finish(cmp_s, [acc_w0, acc_w1])

    bound = 1.01 * (HEAD_DIM ** 0.5) * jnp.max(jnp.abs(qgain_ref[...])) * jnp.max(jnp.abs(kgain_ref[...]))
    safe = bound <= MAX_FIXED_SHIFT
    pl.when(safe)(lambda: fixed_shift_path(bound))
    pl.when(jnp.logical_not(safe))(online_path)


def _sel_from_cmp(n_cmp, n_sel):
    c0 = np.arange(n_cmp) * CMP_STRIDE
    s0 = np.arange(n_sel) * SEL_BLOCK
    ov = np.minimum(c0[None, :] + CMP_BLOCK, s0[:, None] + SEL_BLOCK) - np.maximum(c0[None, :], s0[:, None])
    m = (np.clip(ov, 0, None) / CMP_BLOCK).astype(np.float32)
    m[:, (np.arange(n_cmp) * CMP_STRIDE + CMP_BLOCK) > n_sel * SEL_BLOCK] = 0.0
    return m


def _nsa_attn(proj3, kcmp, vcmp, ks, vs, kw, vw, q_gain, k_gains, o_gain):
    B, S, _ = proj3.shape
    n_cmp = kcmp.shape[2]
    n_sel = S // SEL_BLOCK
    tq = NSA_TQ
    assert n_sel % 8 == 0 and S % SEL_KC == 0 and SEL_KC % tq == 0 and WINDOW % tq == 0 and S >= WINDOW + tq
    avgq = jnp.asarray(_block_avg(NSA_WIDTH), BF16)
    avgo = jnp.asarray(_block_avg(LANES), BF16)
    msct = jnp.asarray(_sel_from_cmp(n_cmp, n_sel), BF16)
    esel = (np.arange(n_sel)[:, None] == np.arange(S)[None, :] // SEL_BLOCK).astype(np.float32)
    esel = jnp.asarray(esel.reshape(n_sel, S // SEL_KC, SEL_KC).transpose(1, 0, 2), BF16)
    src = np.arange(LANES)[:, None]
    dst = np.arange(NSA_WIDTH)[None, :]
    egate = jnp.asarray(np.stack([(src == (dst // HEAD_DIM) * 3 + j) for j in range(3)]).astype(np.float32), BF16)
    r = np.arange(tq)[:, None]
    n_w = WINDOW + tq
    wcases = []
    for i in range(WINDOW // tq + 1):
        diff = (i * tq - max(i * tq - WINDOW, 0)) + r - np.arange(n_w)[None, :]
        wcases.append(np.where((diff >= 0) & (diff < WINDOW), 0.0, NEG_INF))
    wbias = jnp.asarray(np.stack(wcases), F32)
    dbias = jnp.asarray(np.stack([np.where(np.arange(SEL_KC)[None, :] <= i * tq + r, 0.0, NEG_INF)
                                  for i in range(SEL_KC // tq)]), F32)
    c_end = np.arange(n_cmp)[None, :] * CMP_STRIDE + (CMP_BLOCK - 1)
    cbias = jnp.asarray(np.stack([np.where(c_end <= i * tq + r, 0.0, NEG_INF) for i in range(S // tq)]), F32)

    full = lambda a: pl.BlockSpec(a.shape, lambda b, i: (0,) * a.ndim)
    per_b = lambda a: pl.BlockSpec((1,) + a.shape[1:], lambda b, i: (b,) + (0,) * (a.ndim - 1))
    return pl.pallas_call(
        _nsa_attn_body,
        grid=(B, S // tq),
        in_specs=[pl.BlockSpec((1, tq, NSA_WIDTH), lambda b, i: (b, i, COL_QA // NSA_WIDTH)),
                  pl.BlockSpec((1, tq, LANES), lambda b, i: (b, i, COL_GATE // LANES)),
                  per_b(kcmp), per_b(vcmp), per_b(ks), per_b(vs), per_b(kw), per_b(vw),
                  full(q_gain), full(k_gains), full(o_gain), full(avgq), full(avgo), full(msct), full(esel), full(egate),
                  full(wbias), full(dbias), full(cbias)],
        out_specs=pl.BlockSpec((1, tq, NSA_WIDTH), lambda b, i: (b, i, 0)),
        out_shape=jax.ShapeDtypeStruct((B, S, NSA_WIDTH), BF16),
        scratch_shapes=[pltpu.VMEM((NSA_GROUPS, 4 * tq, 1), F32), pltpu.VMEM((NSA_GROUPS, 4 * tq, LANES), F32)],
        compiler_params=_cparams(2),
        name="nsa_attn",
    )(proj3, proj3, kcmp, vcmp, ks, vs, kw, vw, q_gain, k_gains, o_gain, avgq, avgo, msct, esel, egate, wbias, dbias,
      cbias)


def _retention_body(q_ref, k_ref, v_ref, g_ref, cos_ref, sin_ref, decay_ref, xi_ref, zeta_ref, gammac_ref,
                    gain_ref, avg_ref, o_ref, state_scr):
    S = q_ref.shape[1]
    C = RET_CHUNK
    lane = lax.broadcasted_iota(jnp.int32, (C, LANES), 1)
    lo = lane < HEAD_DIM
    first_half = (lane & (HEAD_DIM - 1)) < HEAD_DIM // 2
    r = lax.broadcasted_iota(jnp.int32, (LANES, LANES), 0)
    c = lax.broadcasted_iota(jnp.int32, (LANES, LANES), 1)
    same_head = (r < HEAD_DIM) == (c < HEAD_DIM)
    avg = avg_ref[...]
    state_scr[...] = jnp.zeros(state_scr.shape, F32)

    def rope(x, cos, sin):
        swapped = jnp.where(first_half, pltpu.roll(x, LANES - HEAD_DIM // 2, axis=1),
                            pltpu.roll(x, HEAD_DIM // 2, axis=1))
        return x * cos + swapped * sin

    n_pairs = RET_HEADS // 2
    cols = [slice(p * LANES, (p + 1) * LANES) for p in range(n_pairs)]
    units = [(u, p) for u in range(RET_UNROLL) for p in range(n_pairs)]

    def chunks(n, carry):
        r0 = [pl.multiple_of((n * RET_UNROLL + u) * C, C) for u in range(RET_UNROLL)]
        cos = [cos_ref[pl.ds(r0[u], C), :] for u in range(RET_UNROLL)]
        sin = [sin_ref[pl.ds(r0[u], C), :] for u in range(RET_UNROLL)]
        q = [rope(q_ref[0, pl.ds(r0[u], C), cols[p]].astype(F32), cos[u], sin[u]) for u, p in units]
        k = [rope(k_ref[0, pl.ds(r0[u], C), cols[p]].astype(F32), cos[u], sin[u]) * (HEAD_DIM ** -0.5)
             for u, p in units]
        vb = [v_ref[0, pl.ds(r0[u], C), cols[p]] for u, p in units]
        kb = [k[i].astype(BF16) for i in range(len(units))]
        inner = [_dot_nt(jnp.where(lo if half == 0 else ~lo, q[i], 0.0).astype(BF16), kb[i])
                 * decay_ref[2 * units[i][1] + half] for i in range(len(units)) for half in range(2)]
        upd = [_dot_tn((k[i] * zeta_ref[units[i][1]]).astype(BF16), vb[i]) for i in range(len(units))]
        state = [state_scr[p] for p in range(n_pairs)]
        for u in range(RET_UNROLL):
            for p in range(n_pairs):
                prev = state[u * n_pairs + p]
                state.append(gammac_ref[p] * prev + jnp.where(same_head, upd[u * n_pairs + p], 0.0))
        cross = [_dot(q[i].astype(BF16), state[i].astype(BF16)) * xi_ref[units[i][1]] for i in range(len(units))]
        outs = [_dot(inner[j].astype(BF16), vb[j // 2]) for j in range(2 * len(units))]
        for p in range(n_pairs):
            state_scr[p] = state[RET_UNROLL * n_pairs + p]
        y = jnp.concatenate([jnp.where(lo, outs[2 * i], outs[2 * i + 1]) + cross[i] for i in range(len(units))],
                            axis=0)
        mu = _seg_mean(y, avg)
        d = y - mu
        var = _seg_mean(d * d, avg)
        yn = d * lax.rsqrt(var + EPS)
        for i, (u, p) in enumerate(units):
            gate = g_ref[0, pl.ds(r0[u], C), cols[p]].astype(F32)
            o_ref[0, pl.ds(r0[u], C), cols[p]] = (_silu(gate) * (yn[i * C:(i + 1) * C] * gain_ref[:, cols[p]])).astype(BF16)
        return carry

    lax.fori_loop(0, S // (C * RET_UNROLL), chunks, 0)


def _retention_tables(S):
    half = HEAD_DIM // 2
    inv_freq = ROPE_BASE ** (-np.arange(half, dtype=np.float64) / half)
    ang = np.arange(S, dtype=np.float64)[:, None] * inv_freq[None, :]
    cos, sin = np.cos(ang), np.sin(ang)
    cos_t = np.tile(cos, (1, 4))
    sin_t = np.tile(np.concatenate([-sin, sin], axis=1), (1, 2))
    C = RET_CHUNK
    H = RET_HEADS
    log_gamma = np.log1p(-np.power(2.0, -5.0 - np.arange(H, dtype=np.float64)))
    i = np.arange(C, dtype=np.float64)
    rel = i[:, None] - i[None, :]
    decay = np.where(rel >= 0, np.exp(np.maximum(rel, 0.0)[None] * log_gamma[:, None, None]), 0.0)
    xi = np.exp((i + 1.0)[:, None] * log_gamma[None, :])
    zeta = np.exp((C - 1.0 - i)[:, None] * log_gamma[None, :])
    gamma_c = np.exp(C * log_gamma)
    per_pair = lambda t: np.repeat(t.T.reshape(H // 2, 2, -1), HEAD_DIM, axis=1).transpose(0, 2, 1)
    gammac = np.repeat(gamma_c.reshape(H // 2, 2), HEAD_DIM, axis=1)[:, None, :]
    return tuple(jnp.asarray(t, F32) for t in (cos_t, sin_t, decay, per_pair(xi), per_pair(zeta), gammac))


def _retention(proj3, gain):
    B, S, _ = proj3.shape
    cos_t, sin_t, decay, xi, zeta, gammac = _retention_tables(S)
    avg = jnp.asarray(_block_avg(LANES), BF16)
    col = lambda c: pl.BlockSpec((1, S, RET_WIDTH), lambda b: (b, 0, c // RET_WIDTH))
    full = lambda a: pl.BlockSpec(a.shape, lambda b: (0,) * a.ndim)
    return pl.pallas_call(
        _retention_body,
        grid=(B,),
        in_specs=[col(COL_QR), col(COL_KR), col(COL_VR), col(COL_GR), full(cos_t), full(sin_t), full(decay),
                  full(xi), full(zeta), full(gammac), full(gain), full(avg)],
        out_specs=pl.BlockSpec((1, S, RET_WIDTH), lambda b: (b, 0, 0)),
        out_shape=jax.ShapeDtypeStruct((B, S, RET_WIDTH), BF16),
        scratch_shapes=[pltpu.VMEM((RET_HEADS // 2, LANES, LANES), F32)],
        compiler_params=_cparams(1),
        name="retention",
    )(proj3, proj3, proj3, proj3, cos_t, sin_t, decay, xi, zeta, gammac, gain, avg)


def _mem_prep_body(mem_ref, g_ref, wkv_ref, kgain_ref, avg_ref, k_ref, v_ref):
    hm = _rms_full(mem_ref[0], g_ref[...]).astype(BF16)
    kv = _dot(hm, wkv_ref[...].astype(BF16))
    k = kv[:, :MEM_WIDTH]
    ms = _seg_mean(k * k, avg_ref[...])
    k_ref[0] = (k * lax.rsqrt(ms + EPS) * kgain_ref[...]).astype(BF16)
    v_ref[0] = kv[:, MEM_WIDTH:].astype(BF16)


def _mem_prep(mem, g, wkv, kgain):
    B, M, _ = mem.shape
    avg = jnp.asarray(_block_avg(MEM_WIDTH), BF16)
    full = lambda a: pl.BlockSpec(a.shape, lambda b: (0,) * a.ndim)
    out_spec = pl.BlockSpec((1, M, MEM_WIDTH), lambda b: (b, 0, 0))
    out_shape = jax.ShapeDtypeStruct((B, M, MEM_WIDTH), BF16)
    return pl.pallas_call(
        _mem_prep_body,
        grid=(B,),
        in_specs=[pl.BlockSpec((1, M, D_MODEL), lambda b: (b, 0, 0)), full(g), full(wkv), full(kgain), full(avg)],
        out_specs=[out_spec, out_spec],
        out_shape=[out_shape, out_shape],
        compiler_params=_cparams(1),
        name="mem_prep",
    )(mem, g, wkv, kgain, avg)


def _post_body(x_ref, oa_ref, ob_ref, wout_ref, mk_ref, mv_ref, gx_ref, wq_ref, qgain_ref, avg_ref, wo_ref,
               gf_ref, wr_ref, br_ref, tri_ref, x2_ref, h_ref, route_ref, count_ref, wout_b, wq_b, wo_b, wr_b):
    @pl.when((pl.program_id(0) == 0) & (pl.program_id(1) == 0))
    def _():
        wout_b[...] = wout_ref[...].astype(BF16)
        wq_b[...] = wq_ref[...].astype(BF16)
        wo_b[...] = wo_ref[...].astype(BF16)
        wr_b[...] = wr_ref[...].astype(BF16)

    tm = x_ref.shape[1] // POST_CHAINS
    chains = range(POST_CHAINS)
    rows = [slice(c * tm, (c + 1) * tm) for c in chains]
    x1 = [x_ref[0, rows[c]] + _dot(oa_ref[0, rows[c]], wout_b[0:NSA_WIDTH, :])
          + _dot(ob_ref[0, rows[c]], wout_b[NSA_WIDTH:, :]) for c in chains]

    h = [_rms_full(x1[c], gx_ref[...]).astype(BF16) for c in chains]
    q = [_dot(h[c], wq_b[...]) for c in chains]
    ms = [_seg_mean(q[c] * q[c], avg_ref[...]) for c in chains]
    q = [q[c] * lax.rsqrt(ms[c] + EPS) * qgain_ref[...] * (HEAD_DIM ** -0.5) for c in chains]
    lane = lax.broadcasted_iota(jnp.int32, (tm, LANES), 1)
    lo = lane < HEAD_DIM
    heads = [(c, p, half) for c in chains for p in range(MEM_HEADS // 2) for half in range(2)]
    s = [_dot_nt(jnp.where(lo if half == 0 else ~lo, q[c][:, p * LANES:(p + 1) * LANES], 0.0).astype(BF16),
                 mk_ref[0, :, p * LANES:(p + 1) * LANES]) for c, p, half in heads]
    e = [jnp.exp(s[i] - jnp.max(s[i], axis=-1, keepdims=True)) for i in range(len(heads))]
    pr = [(e[i] / jnp.sum(e[i], axis=-1, keepdims=True)).astype(BF16) for i in range(len(heads))]
    outs = [_dot(pr[i], mv_ref[0, :, heads[i][1] * LANES:(heads[i][1] + 1) * LANES]) for i in range(len(heads))]
    per_chain = MEM_HEADS
    o = [jnp.concatenate([jnp.where(lo, outs[c * per_chain + 2 * p], outs[c * per_chain + 2 * p + 1])
                          for p in range(MEM_HEADS // 2)], axis=1).astype(BF16) for c in chains]
    x2 = [x1[c] + _dot(o[c], wo_b[...]) for c in chains]
    for c in chains:
        x2_ref[0, rows[c]] = x2[c]

    hf = [_rms_full(x2[c], gf_ref[...]).astype(BF16) for c in chains]
    for c in chains:
        h_ref[0, rows[c]] = hf[c]
    logits = [_dot(hf[c], wr_b[...]) + br_ref[...] for c in chains]
    lane_f = lane.astype(F32)
    big = float(LANES)
    picks = []
    for c in chains:
        gl = jnp.where(lane < N_GROUPS, logits[c], BELOW_ALL)
        gmax = jnp.max(gl, axis=-1, keepdims=True)
        grp = jnp.min(jnp.where(gl == gmax, lane_f, big), axis=-1, keepdims=True)
        g_w = 1.0 / jnp.sum(jnp.where(lane < N_GROUPS, jnp.exp(gl - gmax), 0.0), axis=-1, keepdims=True)
        e_lo = N_GROUPS + grp * EXPERTS_PER_GROUP
        el = jnp.where((lane_f >= e_lo) & (lane_f < e_lo + EXPERTS_PER_GROUP), logits[c], BELOW_ALL)
        v0 = jnp.max(el, axis=-1, keepdims=True)
        i0 = jnp.min(jnp.where(el == v0, lane_f, big), axis=-1, keepdims=True)
        el = jnp.where(lane_f == i0, BELOW_ALL, el)
        v1 = jnp.max(el, axis=-1, keepdims=True)
        i1 = jnp.min(jnp.where(el == v1, lane_f, big), axis=-1, keepdims=True)
        e1 = jnp.exp(v1 - v0)
        picks.append((i0 - N_GROUPS, i1 - N_GROUPS, g_w / (1.0 + e1), g_w * e1 / (1.0 + e1)))

    hot = [[lane_f == picks[c][s] for s in range(2)] for c in chains]
    both = jnp.concatenate([jnp.where(hot[c][0], 1.0, 0.0) + jnp.where(hot[c][1], 1.0, 0.0) for c in chains], axis=0)
    before = _dot(tri_ref[...], both.astype(BF16))
    count_ref[0] = jnp.broadcast_to(jnp.sum(both, axis=0, keepdims=True), count_ref.shape[1:])
    for c in chains:
        e0, e1, w0, w1 = picks[c]
        r0 = jnp.sum(jnp.where(hot[c][0], before[rows[c]], 0.0), axis=-1, keepdims=True)
        r1 = jnp.sum(jnp.where(hot[c][1], before[rows[c]], 0.0), axis=-1, keepdims=True)
        cols = (e0, e1, w0, w1, r0, r1)
        route = jnp.zeros((tm, LANES), F32)
        for k in range(len(cols)):
            route = jnp.where(lane == k, cols[k], route)
        route_ref[0, rows[c]] = route


def _post(x, oa, ob, wout, mk, mv, gx, wq, qgain, wo, gf, wr, br):
    B, S, _ = x.shape
    tm = MOE_TM
    n_s = S // tm
    avg = jnp.asarray(_block_avg(MEM_WIDTH), BF16)
    tri = jnp.asarray(np.tril(np.ones((tm, tm), np.float32), -1), BF16)
    full = lambda a: pl.BlockSpec(a.shape, lambda b, i: (0,) * a.ndim)
    per_b = lambda a: pl.BlockSpec((1,) + a.shape[1:], lambda b, i: (b,) + (0,) * (a.ndim - 1))
    tile = lambda w: pl.BlockSpec((1, tm, w), lambda b, i: (b, i, 0))
    return pl.pallas_call(
        _post_body,
        grid=(B, n_s),
        in_specs=[tile(D_MODEL), tile(NSA_WIDTH), tile(RET_WIDTH), full(wout), per_b(mk), per_b(mv), full(gx),
                  full(wq), full(qgain), full(avg), full(wo), full(gf), full(wr), full(br), full(tri)],
        out_specs=[tile(D_MODEL), tile(D_MODEL), tile(LANES),
                   pl.BlockSpec((1, 8, LANES), lambda b, i: (b * n_s + i, 0, 0))],
        out_shape=[jax.ShapeDtypeStruct((B, S, D_MODEL), F32), jax.ShapeDtypeStruct((B, S, D_MODEL), BF16),
                   jax.ShapeDtypeStruct((B, S, LANES), F32), jax.ShapeDtypeStruct((B * n_s, 8, LANES), F32)],
        scratch_shapes=[pltpu.VMEM(wout.shape, BF16), pltpu.VMEM(wq.shape, BF16), pltpu.VMEM(wo.shape, BF16),
                        pltpu.VMEM(wr.shape, BF16)],
        compiler_params=_cparams(2),
        name="post_mixer",
    )(x, oa, ob, wout, mk, mv, gx, wq, qgain, avg, wo, gf, wr, br, tri)


def _row_copy(src, dst, sem):
    return pltpu.make_async_copy(src, dst, sem)


def _run_pieces(n, max_piece, fn):
    b = RUN_ALIGN
    while b <= max_piece:
        pl.when((n & b) != 0)(functools.partial(fn, n & (-2 * b), b))
        b *= 2


def _move_groups(i, gmap_ref, n_loc, copy):
    n_groups = n_loc // RUN_ALIGN
    for j in range(n_groups):
        glob = pl.multiple_of(gmap_ref[i * n_groups + j], RUN_ALIGN)
        copy(pl.ds(j * RUN_ALIGN, RUN_ALIGN), pl.ds(glob, RUN_ALIGN)).start()


def _local_positions(route, loff_row):
    lane = lax.broadcasted_iota(jnp.int32, route.shape, 1).astype(F32)
    pos = []
    for s in range(2):
        base = jnp.sum(jnp.where(lane == route[:, s:s + 1], loff_row, 0.0), axis=-1, keepdims=True)
        pos.append(base + route[:, 4 + s:5 + s])
    return pos


def _scatter_body(gmap_ref, tstart_ref, tlen_ref, nact_ref,
                  h_ref, route_ref, lofff_ref, xs_ref, xloc, zbuf, sems):
    i = pl.program_id(0)
    tm = h_ref.shape[0]
    n_loc = xloc.shape[1]
    slot = i & 1
    sem = sems.at[0]

    def tails(start):
        def per_expert(e, carry):
            n = tlen_ref[e]
            st = tstart_ref[e]

            def piece(off, size):
                c = _row_copy(zbuf.at[pl.ds(0, size)], xs_ref.at[pl.ds(pl.multiple_of(st + off, RUN_ALIGN), size)], sem)
                c.start() if start else c.wait()

            _run_pieces(n, MOE_RB // 2, piece)
            return carry

        lax.fori_loop(0, N_EXPERTS, per_expert, 0)

    def unused(start):
        rows = zbuf.shape[0]

        def per_unit(u, carry):
            c = _row_copy(zbuf, xs_ref.at[pl.ds(pl.multiple_of(u * rows, rows), rows)], sem)
            c.start() if start else c.wait()
            return carry

        lax.fori_loop(nact_ref[0] * (MOE_RB // rows), xs_ref.shape[0] // rows, per_unit, 0)

    @pl.when(i == 0)
    def _():
        zbuf[...] = jnp.zeros(zbuf.shape, zbuf.dtype)
        tails(True)
        unused(True)
        tails(False)
        unused(False)

    pos = _local_positions(route_ref[...], lofff_ref[0, 0:1, :])
    col = lax.broadcasted_iota(jnp.int32, (tm, n_loc), 1).astype(F32)
    perm_t = jnp.where((col == pos[0]) | (col == pos[1]), 1.0, 0.0).astype(BF16)
    xloc[slot] = _dot_tn(perm_t, h_ref[...]).astype(BF16)

    def wait_slot(s):
        _row_copy(xloc.at[s], xs_ref.at[pl.ds(0, n_loc)], sems.at[s]).wait()

    pl.when(i > 0)(lambda: wait_slot(1 - slot))
    _move_groups(i, gmap_ref, n_loc, lambda loc, glob: _row_copy(xloc.at[slot, loc], xs_ref.at[glob], sems.at[slot]))
    pl.when(i == pl.num_programs(0) - 1)(lambda: wait_slot(slot))


def _scatter_rows(tables, h2d, route2d, loff_f, n_rows):
    T = h2d.shape[0]
    tm = MOE_TM
    n_loc = MOE_NLOC
    tile = lambda w: pl.BlockSpec((tm, w), lambda i, *_: (i, 0))
    grid_spec = pltpu.PrefetchScalarGridSpec(
        num_scalar_prefetch=4,
        grid=(T // tm,),
        in_specs=[tile(D_MODEL), tile(LANES), pl.BlockSpec((1, 8, LANES), lambda i, *_: (i, 0, 0))],
        out_specs=pl.BlockSpec(memory_space=pl.ANY),
        scratch_shapes=[pltpu.VMEM((2, n_loc, D_MODEL), BF16), pltpu.VMEM((MOE_RB // 2, D_MODEL), BF16),
                        pltpu.SemaphoreType.DMA((2,))],
    )
    return pl.pallas_call(
        _scatter_body,
        grid_spec=grid_spec,
        out_shape=jax.ShapeDtypeStruct((n_rows, D_MODEL), BF16),
        compiler_params=_cparams(1),
        name="moe_scatter",
    )(*tables, h2d, route2d, loff_f)


def _expert_body(blk0_ref, nblk_ref, n_act_ref, xs_ref, wg_ref, wu_ref, wd_ref, ys_ref,
                 wg_b, wu_b, wd_b, xbuf, ybuf, sem_in, sem_out):
    e = pl.program_id(0)
    n_slots, rb = xbuf.shape[0], xbuf.shape[1]
    ahead = n_slots - 2
    n_act = n_act_ref[0]
    b0 = blk0_ref[e]

    def rows(g):
        return pl.ds(pl.multiple_of(g * rb, rb), rb)

    def x_copy(g, slot):
        return _row_copy(xs_ref.at[rows(g)], xbuf.at[slot], sem_in.at[slot])

    def y_copy(g, slot):
        return _row_copy(ybuf.at[slot], ys_ref.at[rows(g)], sem_out.at[slot])

    @pl.when(e == 0)
    def _():
        for k in range(ahead):
            pl.when(k < n_act)(lambda k=k: x_copy(k, k).start())

    wg_b[...] = wg_ref[0].astype(BF16)
    wu_b[...] = wu_ref[0].astype(BF16)
    wd_b[...] = wd_ref[0].astype(BF16)

    def blocks(g0, count):
        gs = [g0 + c for c in range(count)]
        slots = [g & (n_slots - 1) for g in gs]
        for c in range(count):
            x_copy(gs[c], slots[c]).wait()
        for c in range(count):
            nxt = gs[c] + ahead
            pl.when(nxt < n_act)(lambda nxt=nxt: x_copy(nxt, nxt & (n_slots - 1)).start())
        x = [xbuf[slots[c]] for c in range(count)]
        a = [_dot(x[c], wg_b[...]) for c in range(count)]
        b = [_dot(x[c], wu_b[...]) for c in range(count)]
        h = [(_silu(a[c]) * b[c]).astype(BF16) for c in range(count)]
        y = [_dot(h[c], wd_b[...]).astype(BF16) for c in range(count)]
        for c in range(count):
            pl.when(gs[c] >= n_slots)(lambda c=c: y_copy(gs[c] - n_slots, slots[c]).wait())
        for c in range(count):
            ybuf[slots[c]] = y[c]
            y_copy(gs[c], slots[c]).start()

    nb = nblk_ref[e]

    def pair(j, carry):
        blocks(b0 + 2 * j, 2)
        return carry

    lax.fori_loop(0, lax.shift_right_logical(nb, 1), pair, 0)
    pl.when((nb & 1) == 1)(lambda: blocks(b0 + nb - 1, 1))

    @pl.when(e == pl.num_programs(0) - 1)
    def _():
        for k in range(1, n_slots + 1):
            pl.when(n_act >= k)(lambda k=k: y_copy(n_act - k, (n_act - k) & (n_slots - 1)).wait())
        ybuf[0] = jnp.zeros(ybuf.shape[1:], ybuf.dtype)
        n_blocks = ys_ref.shape[0] // rb

        def fill(start):
            def per_block(g, carry):
                c = y_copy(g, 0)
                c.start() if start else c.wait()
                return carry

            lax.fori_loop(n_act, n_blocks, per_block, 0)

        fill(True)
        fill(False)


def _experts(blk0, nblk, n_act, xs, n_rows, wg, wu, wd):
    rb = MOE_RB
    weight = lambda a: pl.BlockSpec((1,) + a.shape[1:], lambda e, *_: (e, 0, 0))
    grid_spec = pltpu.PrefetchScalarGridSpec(
        num_scalar_prefetch=3,
        grid=(N_EXPERTS,),
        in_specs=[pl.BlockSpec(memory_space=pl.ANY), weight(wg), weight(wu), weight(wd)],
        out_specs=pl.BlockSpec(memory_space=pl.ANY),
        scratch_shapes=[pltpu.VMEM((D_MODEL, EXPERT_FF), BF16), pltpu.VMEM((D_MODEL, EXPERT_FF), BF16),
                        pltpu.VMEM((EXPERT_FF, D_MODEL), BF16), pltpu.VMEM((MOE_SLOTS, rb, D_MODEL), BF16),
                        pltpu.VMEM((MOE_SLOTS, rb, D_MODEL), BF16), pltpu.SemaphoreType.DMA((MOE_SLOTS,)),
                        pltpu.SemaphoreType.DMA((MOE_SLOTS,))],
    )
    return pl.pallas_call(
        _expert_body,
        grid_spec=grid_spec,
        out_shape=jax.ShapeDtypeStruct((n_rows, D_MODEL), BF16),
        compiler_params=_cparams(1),
        name="moe_experts",
    )(blk0, nblk, n_act, xs, wg, wu, wd)


def _combine_body(gmap_ref, x_ref, route_ref, lofff_ref, ys_ref, o_ref, yloc, sems):
    i = pl.program_id(0)
    tm = x_ref.shape[0]
    n_loc = yloc.shape[1]
    slot = i & 1

    def fetch(tile, s):
        _move_groups(tile, gmap_ref, n_loc, lambda loc, glob: _row_copy(ys_ref.at[glob], yloc.at[s, loc], sems.at[s]))

    pl.when(i == 0)(lambda: fetch(i, slot))
    _row_copy(ys_ref.at[pl.ds(0, n_loc)], yloc.at[slot], sems.at[slot]).wait()
    pl.when(i + 1 < pl.num_programs(0))(lambda: fetch(i + 1, 1 - slot))

    route = route_ref[...]
    pos = _local_positions(route, lofff_ref[0, 0:1, :])
    col = lax.broadcasted_iota(jnp.int32, (tm, n_loc), 1).astype(F32)
    perm_w = jnp.where(col == pos[0], route[:, 2:3], jnp.where(col == pos[1], route[:, 3:4], 0.0)).astype(BF16)
    o_ref[...] = x_ref[...] + _dot(perm_w, yloc[slot])


def _combine(tables, x2d, route2d, loff_f, ys):
    T = x2d.shape[0]
    tm = MOE_TM
    n_loc = MOE_NLOC
    tile = lambda w: pl.BlockSpec((tm, w), lambda i, *_: (i, 0))
    grid_spec = pltpu.PrefetchScalarGridSpec(
        num_scalar_prefetch=1,
        grid=(T // tm,),
        in_specs=[tile(D_MODEL), tile(LANES), pl.BlockSpec((1, 8, LANES), lambda i, *_: (i, 0, 0)),
                  pl.BlockSpec(memory_space=pl.ANY)],
        out_specs=tile(D_MODEL),
        scratch_shapes=[pltpu.VMEM((2, n_loc, D_MODEL), BF16), pltpu.SemaphoreType.DMA((2,))],
    )
    return pl.pallas_call(
        _combine_body,
        grid_spec=grid_spec,
        out_shape=jax.ShapeDtypeStruct((T, D_MODEL), F32),
        compiler_params=_cparams(1),
        name="moe_combine",
    )(*tables, x2d, route2d, loff_f, ys)


def _compress_weights(pos, w2):
    eye = jnp.eye(NSA_GROUPS, dtype=F32)
    w2b = jnp.einsum('hd,gk->ghkd', w2, eye).reshape(NSA_GROUPS * CMP_HIDDEN, LANES)
    posb = jnp.tile(pos, (1, NSA_GROUPS)).reshape(2, 1, CMP_STRIDE * LANES)
    return posb, w2b.astype(BF16)


def _dup2(g):
    return jnp.tile(g.reshape(1, HEAD_DIM), (1, 2))


def kernel(x, mem, mix_norm, w_in, nsa_q_norm, nsa_kcmp_norm, nsa_ksel_norm, nsa_kwin_norm, cmp_pos_k, cmp_pos_v, cmp_k_w1, cmp_k_w2, cmp_v_w1, cmp_v_w2, nsa_out_norm, ret_out_norm, w_out, mem_x_norm, mem_kv_norm, mem_wq, mem_wkv, mem_q_norm, mem_k_norm, mem_wo, ffn_norm, router_group_w, router_group_b, router_expert_w, router_expert_b, exp_w_gate, exp_w_up, exp_w_down):
    B, S, D = x.shape
    T = B * S
    depth = mix_norm.shape[0]
    for l in range(depth):
        proj = _proj(x.reshape(T, D), mix_norm[l].reshape(1, D), w_in[l].T).reshape(B, S, PROJ_PAD)
        pk, w2k = _compress_weights(cmp_pos_k[l], cmp_k_w2[l])
        pv, w2v = _compress_weights(cmp_pos_v[l], cmp_v_w2[l])
        gains = jnp.stack([_dup2(nsa_kcmp_norm[l]), _dup2(nsa_ksel_norm[l]), _dup2(nsa_kwin_norm[l])])
        kcmp, vcmp, ks, vs, kw, vw = _nsa_prep(proj, jnp.stack([pk, pv]), cmp_k_w1[l], cmp_v_w1[l],
                                               jnp.stack([w2k, w2v]), gains)
        o_a = _nsa_attn(proj, kcmp, vcmp, ks, vs, kw, vw,
                        jnp.tile(nsa_q_norm[l].reshape(1, HEAD_DIM), (1, NSA_HEADS)), gains,
                        nsa_out_norm[l].reshape(1, NSA_WIDTH))
        o_b = _retention(proj, ret_out_norm[l].reshape(1, RET_WIDTH))
        mk, mv = _mem_prep(mem, mem_kv_norm[l].reshape(1, D), mem_wkv[l],
                           jnp.tile(mem_k_norm[l].reshape(1, HEAD_DIM), (1, MEM_HEADS)))
        w_r = jnp.concatenate([router_group_w[l],
                               router_expert_w[l].transpose(1, 0, 2).reshape(D, N_EXPERTS),
                               jnp.zeros((D, LANES - N_GROUPS - N_EXPERTS), F32)], axis=1)
        b_r = jnp.concatenate([router_group_b[l], router_expert_b[l].reshape(N_EXPERTS),
                               jnp.zeros((LANES - N_GROUPS - N_EXPERTS,), F32)]).reshape(1, LANES)
        x2, hf, route, counts = _post(
            x, o_a, o_b, w_out[l], mk, mv, mem_x_norm[l].reshape(1, D), mem_wq[l],
            jnp.tile(mem_q_norm[l].reshape(1, HEAD_DIM), (1, MEM_HEADS)), mem_wo[l],
            ffn_norm[l].reshape(1, D), w_r, b_r)
        route2d = route.reshape(T, LANES)
        n_tiles = T // MOE_TM
        cnt = counts[:, 0, :N_EXPERTS].astype(jnp.int32)
        cnt = (cnt + RUN_ALIGN - 1) // RUN_ALIGN * RUN_ALIGN
        loff = jnp.cumsum(cnt, axis=1) - cnt
        total = jnp.sum(cnt, axis=0)
        padded = (total + MOE_RB - 1) // MOE_RB * MOE_RB
        pend = jnp.cumsum(padded)
        pstart = pend - padded
        goff = pstart[None, :] + jnp.cumsum(cnt, axis=0) - cnt
        n_rows = 2 * T + n_tiles * N_EXPERTS * RUN_ALIGN + N_EXPERTS * MOE_RB
        n_act = (pend[-1:] // MOE_RB).astype(jnp.int32)
        loff_f = jnp.broadcast_to(jnp.pad(loff.astype(F32), ((0, 0), (0, LANES - N_EXPERTS)))[:, None, :],
                                  (n_tiles, 8, LANES))
        grp_row = jnp.arange(MOE_NLOC // RUN_ALIGN, dtype=jnp.int32) * RUN_ALIGN
        inside = ((loff[:, None, :] <= grp_row[None, :, None])
                  & (grp_row[None, :, None] < (loff + cnt)[:, None, :])).astype(jnp.int32)
        shift = jnp.sum(inside * (goff - loff)[:, None, :], axis=2)
        used = jnp.sum(inside, axis=2) > 0
        gmap_scatter = jnp.where(used, shift + grp_row[None, :], n_rows + grp_row[None, :]).reshape(-1)
        gmap_gather = jnp.where(used, shift + grp_row[None, :], 0).reshape(-1)
        xs = _scatter_rows((gmap_scatter, pstart + total, padded - total, n_act), hf.reshape(T, D), route2d, loff_f,
                           n_rows + MOE_NLOC)
        ys = _experts(pstart // MOE_RB, padded // MOE_RB, n_act, xs, n_rows,
                      exp_w_gate[l], exp_w_up[l], exp_w_down[l])
        x = _combine((gmap_gather,), x2.reshape(T, D), route2d, loff_f, ys).reshape(B, S, D)
    return x
```

```python
import functools

import numpy as np
import jax
import jax.numpy as jnp
from jax import lax
from jax.experimental import pallas as pl
from jax.experimental.pallas import tpu as pltpu

F32 = jnp.float32
BF16 = jnp.bfloat16

D_MODEL = 1024
HEAD_DIM = 64
LANES = 128
NSA_HEADS = 8
NSA_GROUPS = 2
NSA_WIDTH = NSA_HEADS * HEAD_DIM
CMP_BLOCK = 32
CMP_STRIDE = 16
CMP_HIDDEN = 2 * HEAD_DIM
SEL_BLOCK = 64
SEL_TOPK = 8
WINDOW = 512
RET_HEADS = 8
RET_WIDTH = RET_HEADS * HEAD_DIM
RET_CHUNK = 128
ROPE_BASE = 10000.0
MEM_HEADS = 4
MEM_WIDTH = MEM_HEADS * HEAD_DIM
N_GROUPS = 4
EXPERTS_PER_GROUP = 8
N_EXPERTS = N_GROUPS * EXPERTS_PER_GROUP
EXPERT_FF = D_MODEL // 4
EPS = 1e-6
NEG_INF = -1e30
BELOW_ALL = -3e38
MAX_FIXED_SHIFT = 40.0

COL_QA = 0
COL_QR, COL_KR, COL_VR, COL_GR = 512, 1024, 1536, 2048
COL_KVC, COL_KSV, COL_KWV = 2560, 2816, 3072
COL_GATE = 3328
PROJ_PAD = 3456
PROJ_SPLITS = (NSA_WIDTH, NSA_WIDTH + 6 * NSA_GROUPS * HEAD_DIM, NSA_WIDTH + 6 * NSA_GROUPS * HEAD_DIM + 3 * NSA_HEADS)

PROJ_TM = 512
RET_UNROLL = 4
NSA_TQ = 256
SEL_KC = 512
POST_CHAINS = 2
MOE_TM = 512
MOE_RB = 256
MOE_SLOTS = 8
RUN_ALIGN = 16
MOE_NLOC = 2 * MOE_TM + N_EXPERTS * RUN_ALIGN
VMEM_LIMIT = 56 * 1024 * 1024


def _cparams(n_axes):
    return pltpu.CompilerParams(dimension_semantics=("arbitrary",) * n_axes,
                                vmem_limit_bytes=VMEM_LIMIT)


def _dot(a, b):
    return jnp.dot(a, b, preferred_element_type=F32)


def _dot_nt(a, b):
    return lax.dot_general(a, b, (((1,), (1,)), ((), ())), preferred_element_type=F32)


def _dot_tn(a, b):
    return lax.dot_general(a, b, (((0,), (0,)), ((), ())), preferred_element_type=F32)


def _rms_full(x, g):
    ms = jnp.mean(x * x, axis=-1, keepdims=True)
    return x * lax.rsqrt(ms + EPS) * g


def _seg_mean(x, avg):
    return _dot(x.astype(BF16), avg)


def _silu(x):
    return x * (1.0 / (1.0 + jnp.exp(-x)))


def _sigmoid(x):
    return 1.0 / (1.0 + jnp.exp(-x))


def _block_avg(width):
    i = np.arange(width)
    return ((i[:, None] // HEAD_DIM == i[None, :] // HEAD_DIM) / HEAD_DIM).astype(np.float32)


def _proj_body(x_ref, g_ref, wt_ref, o_ref, w_scr):
    @pl.when(pl.program_id(0) == 0)
    def _():
        kv0, gate0, ret0 = PROJ_SPLITS
        w_scr[COL_QA:COL_QR] = wt_ref[0:kv0].astype(BF16)
        w_scr[COL_QR:COL_KVC] = wt_ref[ret0:].astype(BF16)
        w_scr[COL_KVC:COL_GATE] = wt_ref[kv0:gate0].astype(BF16)
        pad = jnp.zeros((PROJ_PAD - COL_GATE - (ret0 - gate0), wt_ref.shape[1]), F32)
        w_scr[COL_GATE:] = jnp.concatenate([wt_ref[gate0:ret0], pad], axis=0).astype(BF16)

    h = _rms_full(x_ref[...], g_ref[...]).astype(BF16)
    step = PROJ_PAD // 3
    for j in range(3):
        o_ref[:, j * step:(j + 1) * step] = _dot_nt(h, w_scr[j * step:(j + 1) * step]).astype(BF16)


def _proj(x2d, g, w_t):
    T = x2d.shape[0]
    assert w_t.shape[0] == PROJ_SPLITS[2] + 4 * RET_WIDTH
    return pl.pallas_call(
        _proj_body,
        grid=(T // PROJ_TM,),
        in_specs=[pl.BlockSpec((PROJ_TM, D_MODEL), lambda i: (i, 0)),
                  pl.BlockSpec((1, D_MODEL), lambda i: (0, 0)),
                  pl.BlockSpec(w_t.shape, lambda i: (0, 0))],
        out_specs=pl.BlockSpec((PROJ_TM, PROJ_PAD), lambda i: (i, 0)),
        out_shape=jax.ShapeDtypeStruct((T, PROJ_PAD), BF16),
        scratch_shapes=[pltpu.VMEM((PROJ_PAD, D_MODEL), BF16)],
        compiler_params=_cparams(1),
        name="proj",
    )(x2d, g, w_t)


def _dup_groups(x):
    lane = lax.broadcasted_iota(jnp.int32, x.shape, 1)
    xs = pltpu.roll(x, HEAD_DIM, axis=1)
    lo = lane < HEAD_DIM
    return jnp.where(lo, x, xs), jnp.where(lo, xs, x)


def _ones_groups(x):
    lane = lax.broadcasted_iota(jnp.int32, x.shape, 1)
    lo = lane < HEAD_DIM
    xs = pltpu.roll(x, HEAD_DIM, axis=1)
    return jnp.where(lo, x, 1.0), jnp.where(lo, 1.0, xs), jnp.where(lo, xs, 1.0), jnp.where(lo, 1.0, x)


def _nsa_prep_body(kvc_ref, ksv_ref, kwv_ref, pos_ref, w1k_ref, w1v_ref, w2_ref, gain_ref, avg_ref,
                   mem_ref, memg_ref, wkv_ref, memk_gain_ref, mem_avg_ref,
                   kcmp_ref, vcmp_ref, ks_ref, vs_ref, kw_ref, vw_ref, mk_ref, mv_ref, scr_k, scr_v, w1_ref):
    _mem_prep_body(mem_ref, memg_ref, wkv_ref, memk_gain_ref, mem_avg_ref, mk_ref, mv_ref)

    @pl.when(pl.program_id(0) == 0)
    def _():
        zero = jnp.zeros((HEAD_DIM, CMP_HIDDEN), BF16)
        for j, src in ((0, w1k_ref), (1, w1v_ref)):
            for l in range(CMP_BLOCK):
                piece = src[l * HEAD_DIM:(l + 1) * HEAD_DIM, :].astype(BF16)
                r0 = (l % CMP_STRIDE) * LANES
                w1_ref[j, l // CMP_STRIDE, r0:r0 + HEAD_DIM, :] = jnp.concatenate([piece, zero], axis=1)
                w1_ref[j, l // CMP_STRIDE, r0 + HEAD_DIM:r0 + LANES, :] = jnp.concatenate([zero, piece], axis=1)

    avg = avg_ref[...]
    n_c = scr_k.shape[0] // CMP_STRIDE
    scr_k[...] = kvc_ref[0, :, 0:LANES].astype(F32)
    scr_v[...] = kvc_ref[0, :, LANES:2 * LANES].astype(F32)
    for j, out_ref, scr in ((0, kcmp_ref, scr_k), (1, vcmp_ref, scr_v)):
        ycat = jnp.concatenate(
            [scr[pl.ds(l, n_c, stride=CMP_STRIDE), :] for l in range(CMP_STRIDE)], axis=1)
        first = _dot((ycat + pos_ref[j, 0]).astype(BF16), w1_ref[j, 0])
        second = _dot((ycat + pos_ref[j, 1]).astype(BF16), w1_ref[j, 1])
        hidden = first + pltpu.roll(second, n_c - 1, axis=0)
        cmp_tok = _dot(_silu(hidden).astype(BF16), w2_ref[j])
        if j == 0:
            ms = _seg_mean(cmp_tok * cmp_tok, avg)
            cmp_tok = cmp_tok * lax.rsqrt(ms + EPS) * gain_ref[0]
        for i, piece in enumerate(_dup_groups(cmp_tok) if j == 0 else _ones_groups(cmp_tok)):
            out_ref[0, i] = piece.astype(BF16)

    for src_ref, k_out, v_out, gi in ((ksv_ref, ks_ref, vs_ref, 1), (kwv_ref, kw_ref, vw_ref, 2)):
        k = src_ref[0, :, 0:LANES].astype(F32)
        ms = _seg_mean(k * k, avg)
        k = k * lax.rsqrt(ms + EPS) * gain_ref[gi]
        d0, d1 = _dup_groups(k)
        k_out[0, 0] = d0.astype(BF16)
        k_out[0, 1] = d1.astype(BF16)
        for i, piece in enumerate(_ones_groups(src_ref[0, :, LANES:2 * LANES].astype(F32))):
            v_out[0, i] = piece.astype(BF16)


def _nsa_prep(proj3, pos, w1k, w1v, w2, gains, mem_args):
    B, S, _ = proj3.shape
    n_c = S // CMP_STRIDE
    avg = jnp.asarray(_block_avg(LANES), BF16)
    col = lambda c: pl.BlockSpec((1, S, 2 * LANES), lambda b: (b, 0, c // (2 * LANES)))
    full = lambda a: pl.BlockSpec(a.shape, lambda b: (0,) * a.ndim)
    cmp_spec = pl.BlockSpec((1, NSA_GROUPS, n_c, LANES), lambda b: (b, 0, 0, 0))
    seq_spec = pl.BlockSpec((1, NSA_GROUPS, S, LANES), lambda b: (b, 0, 0, 0))
    cmp_shape = jax.ShapeDtypeStruct((B, NSA_GROUPS, n_c, LANES), BF16)
    seq_shape = jax.ShapeDtypeStruct((B, NSA_GROUPS, S, LANES), BF16)

    def val(a):
        if isinstance(a, jax.ShapeDtypeStruct):
            return jax.ShapeDtypeStruct((a.shape[0], 2 * a.shape[1]) + a.shape[2:], a.dtype)
        return pl.BlockSpec((1, 2 * NSA_GROUPS) + a.block_shape[2:], lambda b: (b, 0, 0, 0))

    mem, mem_g, wkv, mem_kgain = mem_args
    M = mem.shape[1]
    mem_avg = jnp.asarray(_block_avg(MEM_WIDTH), BF16)
    mem_spec = pl.BlockSpec((1, M, MEM_WIDTH), lambda b: (b, 0, 0))
    mem_shape = jax.ShapeDtypeStruct((B, M, MEM_WIDTH), BF16)
    return pl.pallas_call(
        _nsa_prep_body,
        grid=(B,),
        in_specs=[col(COL_KVC), col(COL_KSV), col(COL_KWV), full(pos), full(w1k), full(w1v), full(w2), full(gains),
                  full(avg), pl.BlockSpec((1, M, D_MODEL), lambda b: (b, 0, 0)), full(mem_g), full(wkv),
                  full(mem_kgain), full(mem_avg)],
        out_specs=[cmp_spec, val(cmp_spec), seq_spec, val(seq_spec), seq_spec, val(seq_spec), mem_spec, mem_spec],
        out_shape=[cmp_shape, val(cmp_shape), seq_shape, val(seq_shape), seq_shape, val(seq_shape), mem_shape,
                   mem_shape],
        scratch_shapes=[pltpu.VMEM((S, LANES), F32), pltpu.VMEM((S, LANES), F32),
                        pltpu.VMEM((2, 2, CMP_STRIDE * LANES, NSA_GROUPS * CMP_HIDDEN), BF16)],
        compiler_params=_cparams(1),
        name="nsa_prep",
    )(proj3, proj3, proj3, pos, w1k, w1v, w2, gains, avg, mem, mem_g, wkv, mem_kgain, mem_avg)


def _nsa_attn_body(q_ref, gate_ref, kcmp_ref, vcmp_ref, ks_ref, vs_ref, kw_ref, vw_ref,
                   qgain_ref, kgain_ref, ogain_ref, avgq_ref, avgo_ref, msct_ref, esel_ref, egate_ref, wbias_ref,
                   dbias_ref, cbias_ref,
                   o_ref, m_scr, acc_scr):
    tq = q_ref.shape[1]
    n_cmp = kcmp_ref.shape[2]
    n_sel = msct_ref.shape[0]
    kc_len = esel_ref.shape[2]
    rows = 4 * tq
    qi = pl.program_id(1)
    q0 = qi * tq

    q = q_ref[0].astype(F32)
    ms = _seg_mean(q * q, avgq_ref[...])
    qn = q * lax.rsqrt(ms + EPS) * qgain_ref[...] * (HEAD_DIM ** -0.5)

    gate_sig = _sigmoid(gate_ref[0].astype(F32)).astype(BF16)
    gates = [_dot(gate_sig, egate_ref[j]) for j in range(3)]

    lane_q = lax.broadcasted_iota(jnp.int32, (tq, LANES), 1)
    lo_q = lane_q < HEAD_DIM

    blk = lax.broadcasted_iota(jnp.int32, (n_sel, tq), 0)
    cur = lax.shift_right_logical(q0 + lax.broadcasted_iota(jnp.int32, (n_sel, tq), 1), int(np.log2(SEL_BLOCK)))
    forced = (blk == 0) | (blk == cur) | (blk == cur - 1)
    future = blk > cur
    blk_f = blk.astype(F32)

    def heads4(x):
        return jnp.concatenate([x] * 4, axis=0)

    def weighted_values(p, v_low, v_high):
        return jnp.concatenate([_dot(p[:2 * tq], v_low), _dot(p[2 * tq:], v_high)], axis=0)

    def normalised_pairs(acc, guard):
        out = []
        for p in range(2):
            low = acc[p * tq:(p + 1) * tq]
            high = acc[(2 + p) * tq:(3 + p) * tq]
            den = pltpu.roll(jnp.where(lo_q, high, low), HEAD_DIM, axis=1)
            if guard:
                den = jnp.maximum(den, 1e-30)
            out.append(jnp.where(lo_q, low, high) / den)
        return out

    groups = range(NSA_GROUPS)
    qs = []
    for g in groups:
        slabs = [qn[:, (2 * g + p) * LANES:(2 * g + p + 1) * LANES] for p in range(2)]
        qs.append(jnp.concatenate(
            [jnp.where(lo_q, slabs[0], 0.0), jnp.where(lo_q, slabs[1], 0.0),
             jnp.where(lo_q, 0.0, slabs[0]), jnp.where(lo_q, 0.0, slabs[1])], axis=0).astype(BF16))

    def compressed_and_select(shift):
        s_c = [_dot_nt(qs[g], kcmp_ref[0, g]) for g in groups]
        if shift is None:
            r_c = lax.broadcasted_iota(jnp.int32, (rows, n_cmp), 0)
            c_c = lax.broadcasted_iota(jnp.int32, (rows, n_cmp), 1)
            cmask = (c_c * CMP_STRIDE + (CMP_BLOCK - 1)) <= q0 + (r_c & (tq - 1))
            s_c = [jnp.where(cmask, s_c[g], NEG_INF) for g in groups]
            e_c = [jnp.where(cmask, jnp.exp(s_c[g] - jnp.max(s_c[g], axis=-1, keepdims=True)), 0.0) for g in groups]
        else:
            bias = heads4(cbias_ref[qi] - shift)
            e_c = [jnp.exp(s_c[g] + bias) for g in groups]
        e_b = [e_c[g].astype(BF16) for g in groups]
        acc_c = [weighted_values(e_b[g], vcmp_ref[0, 2 * g], vcmp_ref[0, 2 * g + 1]) for g in groups]
        ones = jnp.ones((8, n_cmp), BF16)
        imp = []
        for g in groups:
            num = [_dot_nt(msct_ref[...], e_b[g][h * tq:(h + 1) * tq]) for h in range(4)]
            den = [_dot_nt(ones, e_b[g][h * tq:(h + 1) * tq])[0:1] for h in range(4)]
            parts = [num[h] / jnp.maximum(den[h], 1e-30) for h in range(4)]
            imp.append((parts[0] + parts[1]) + (parts[2] + parts[3]))
        v = [jnp.where(forced, BELOW_ALL, jnp.where(future, NEG_INF, imp[g])) for g in groups]
        sel = [jnp.where(forced, 1.0, 0.0) for g in groups]
        for _ in range(SEL_TOPK - 3):
            mx = [jnp.max(v[g], axis=0, keepdims=True) for g in groups]
            first = [jnp.min(jnp.where(v[g] == mx[g], blk_f, float(LANES)), axis=0, keepdims=True) for g in groups]
            pick = [blk_f == first[g] for g in groups]
            sel = [jnp.where(pick[g], 1.0, sel[g]) for g in groups]
            v = [jnp.where(pick[g], BELOW_ALL, v[g]) for g in groups]
        return [normalised_pairs(acc_c[g], True) for g in groups], [sel[g].astype(BF16) for g in groups]

    n_before = lax.shift_right_logical(q0, int(np.log2(kc_len)))
    causal = dbias_ref[qi & (kc_len // tq - 1)]
    w0 = pl.multiple_of(jnp.maximum(q0 - WINDOW, 0), tq)
    n_w = WINDOW + tq
    w_case = jnp.minimum(qi, WINDOW // tq)

    def sel_keys(ref, g, kc):
        return ref[0, g, pl.ds(pl.multiple_of(kc * kc_len, kc_len), kc_len), :]

    def sel_scores(sel_b, g, kc, causal_bias, shift):
        chosen = _dot_tn(sel_b[g], esel_ref[kc])
        bias = (chosen - 1.0) * (-NEG_INF)
        if causal_bias is not None:
            bias = bias + causal_bias
        if shift is not None:
            bias = bias - shift
        return _dot_nt(qs[g], sel_keys(ks_ref, g, kc)) + heads4(bias)

    def win_scores(g, shift):
        bias = wbias_ref[w_case] if shift is None else wbias_ref[w_case] - shift
        return _dot_nt(qs[g], kw_ref[0, g, pl.ds(w0, n_w), :]) + heads4(bias)

    def win_values(g):
        return vw_ref[0, 2 * g, pl.ds(w0, n_w), :], vw_ref[0, 2 * g + 1, pl.ds(w0, n_w), :]

    def sel_values(g, kc):
        return sel_keys(vs_ref, 2 * g, kc), sel_keys(vs_ref, 2 * g + 1, kc)

    def finish(cmp_s, acc_w):
        for g in groups:
            sel_s = normalised_pairs(acc_scr[g], False)
            win_s = normalised_pairs(acc_w[g], False)
            for p in range(2):
                cols = slice((2 * g + p) * LANES, (2 * g + p + 1) * LANES)
                mix = gates[0][:, cols] * cmp_s[g][p] + gates[1][:, cols] * sel_s[p] + gates[2][:, cols] * win_s[p]
                ms_o = _seg_mean(mix * mix, avgo_ref[...])
                o_ref[0, :, cols] = (mix * lax.rsqrt(ms_o + EPS) * ogain_ref[:, cols]).astype(BF16)

    def fixed_shift_path(shift):
        cmp_s, sel_b = compressed_and_select(shift)

        def probs(s):
            return jnp.exp(s).astype(BF16)

        for g in groups:
            acc_scr[g] = jnp.zeros(acc_scr.shape[1:], F32)

        def before(kc, carry):
            s = [sel_scores(sel_b, g, kc, None, shift) for g in groups]
            p = [probs(s[g]) for g in groups]
            for g in groups:
                acc_scr[g] = acc_scr[g] + weighted_values(p[g], *sel_values(g, kc))
            return carry

        lax.fori_loop(0, n_before, before, 0)
        s_d0 = sel_scores(sel_b, 0, n_before, causal, shift)
        s_d1 = sel_scores(sel_b, 1, n_before, causal, shift)
        p_d0 = probs(s_d0)
        s_w0 = win_scores(0, shift)
        acc_scr[0] = acc_scr[0] + weighted_values(p_d0, *sel_values(0, n_before))
        p_d1 = probs(s_d1)
        s_w1 = win_scores(1, shift)
        acc_scr[1] = acc_scr[1] + weighted_values(p_d1, *sel_values(1, n_before))
        p_w0 = probs(s_w0)
        acc_w0 = weighted_values(p_w0, *win_values(0))
        p_w1 = probs(s_w1)
        acc_w1 = weighted_values(p_w1, *win_values(1))
        finish(cmp_s, [acc_w0, acc_w1])

    def online_path():
        cmp_s, sel_b = compressed_and_select(None)
        for g in groups:
            m_scr[g] = jnp.full(m_scr.shape[1:], NEG_INF, F32)
            acc_scr[g] = jnp.zeros(acc_scr.shape[1:], F32)

        def sel_softmax(g, s):
            m_old = m_scr[g]
            m_new = jnp.maximum(m_old, jnp.max(s, axis=-1, keepdims=True))
            m_scr[g] = m_new
            return jnp.exp(s - m_new).astype(BF16), jnp.exp(m_old - m_new)

        def sel_accumulate(g, kc, p, alpha):
            acc_scr[g] = alpha * acc_scr[g] + weighted_values(p, *sel_values(g, kc))

        def win_softmax(s):
            return jnp.exp(s - jnp.max(s, axis=-1, keepdims=True)).astype(BF16)

        def before(kc, carry):
            s = [sel_scores(sel_b, g, kc, None, None) for g in groups]
            pa = [sel_softmax(g, s[g]) for g in groups]
            for g in groups:
                sel_accumulate(g, kc, *pa[g])
            return carry

        lax.fori_loop(0, n_before, before, 0)
        s_d0 = sel_scores(sel_b, 0, n_before, causal, None)
        s_d1 = sel_scores(sel_b, 1, n_before, causal, None)
        pa0 = sel_softmax(0, s_d0)
        s_w0 = win_scores(0, None)
        sel_accumulate(0, n_before, *pa0)
        pa1 = sel_softmax(1, s_d1)
        s_w1 = win_scores(1, None)
        sel_accumulate(1, n_before, *pa1)
        acc_w0 = weighted_values(win_softmax(s_w0), *win_values(0))
        acc_w1 = weighted_values(win_softmax(s_w1), *win_values(1))
        finish(cmp_s, [acc_w0, acc_w1])

    bound = 1.01 * (HEAD_DIM ** 0.5) * jnp.max(jnp.abs(qgain_ref[...])) * jnp.max(jnp.abs(kgain_ref[...]))
    safe = bound <= MAX_FIXED_SHIFT
    pl.when(safe)(lambda: fixed_shift_path(bound))
    pl.when(jnp.logical_not(safe))(online_path)


def _sel_from_cmp(n_cmp, n_sel):
    c0 = np.arange(n_cmp) * CMP_STRIDE
    s0 = np.arange(n_sel) * SEL_BLOCK
    ov = np.minimum(c0[None, :] + CMP_BLOCK, s0[:, None] + SEL_BLOCK) - np.maximum(c0[None, :], s0[:, None])
    m = (np.clip(ov, 0, None) / CMP_BLOCK).astype(np.float32)
    m[:, (np.arange(n_cmp) * CMP_STRIDE + CMP_BLOCK) > n_sel * SEL_BLOCK] = 0.0
    return m


def _nsa_attn(proj3, kcmp, vcmp, ks, vs, kw, vw, q_gain, k_gains, o_gain):
    B, S, _ = proj3.shape
    n_cmp = kcmp.shape[2]
    n_sel = S // SEL_BLOCK
    tq = NSA_TQ
    assert n_sel % 8 == 0 and S % SEL_KC == 0 and SEL_KC % tq == 0 and WINDOW % tq == 0 and S >= WINDOW + tq
    avgq = jnp.asarray(_block_avg(NSA_WIDTH), BF16)
    avgo = jnp.asarray(_block_avg(LANES), BF16)
    msct = jnp.asarray(_sel_from_cmp(n_cmp, n_sel), BF16)
    esel = (np.arange(n_sel)[:, None] == np.arange(S)[None, :] // SEL_BLOCK).astype(np.float32)
    esel = jnp.asarray(esel.reshape(n_sel, S // SEL_KC, SEL_KC).transpose(1, 0, 2), BF16)
    src = np.arange(LANES)[:, None]
    dst = np.arange(NSA_WIDTH)[None, :]
    egate = jnp.asarray(np.stack([(src == (dst // HEAD_DIM) * 3 + j) for j in range(3)]).astype(np.float32), BF16)
    r = np.arange(tq)[:, None]
    n_w = WINDOW + tq
    wcases = []
    for i in range(WINDOW // tq + 1):
        diff = (i * tq - max(i * tq - WINDOW, 0)) + r - np.arange(n_w)[None, :]
        wcases.append(np.where((diff >= 0) & (diff < WINDOW), 0.0, NEG_INF))
    wbias = jnp.asarray(np.stack(wcases), F32)
    dbias = jnp.asarray(np.stack([np.where(np.arange(SEL_KC)[None, :] <= i * tq + r, 0.0, NEG_INF)
                                  for i in range(SEL_KC // tq)]), F32)
    c_end = np.arange(n_cmp)[None, :] * CMP_STRIDE + (CMP_BLOCK - 1)
    cbias = jnp.asarray(np.stack([np.where(c_end <= i * tq + r, 0.0, NEG_INF) for i in range(S // tq)]), F32)

    full = lambda a: pl.BlockSpec(a.shape, lambda b, i: (0,) * a.ndim)
    per_b = lambda a: pl.BlockSpec((1,) + a.shape[1:], lambda b, i: (b,) + (0,) * (a.ndim - 1))
    return pl.pallas_call(
        _nsa_attn_body,
        grid=(B, S // tq),
        in_specs=[pl.BlockSpec((1, tq, NSA_WIDTH), lambda b, i: (b, i, COL_QA // NSA_WIDTH)),
                  pl.BlockSpec((1, tq, LANES), lambda b, i: (b, i, COL_GATE // LANES)),
                  per_b(kcmp), per_b(vcmp), per_b(ks), per_b(vs), per_b(kw), per_b(vw),
                  full(q_gain), full(k_gains), full(o_gain), full(avgq), full(avgo), full(msct), full(esel), full(egate),
                  full(wbias), full(dbias), full(cbias)],
        out_specs=pl.BlockSpec((1, tq, NSA_WIDTH), lambda b, i: (b, i, 0)),
        out_shape=jax.ShapeDtypeStruct((B, S, NSA_WIDTH), BF16),
        scratch_shapes=[pltpu.VMEM((NSA_GROUPS, 4 * tq, 1), F32), pltpu.VMEM((NSA_GROUPS, 4 * tq, LANES), F32)],
        compiler_params=_cparams(2),
        name="nsa_attn",
    )(proj3, proj3, kcmp, vcmp, ks, vs, kw, vw, q_gain, k_gains, o_gain, avgq, avgo, msct, esel, egate, wbias, dbias,
      cbias)


def _retention_body(q_ref, k_ref, v_ref, g_ref, cos_ref, sin_ref, decay_ref, xi_ref, zeta_ref, gammac_ref,
                    gain_ref, avg_ref, o_ref, state_scr):
    S = q_ref.shape[1]
    C = RET_CHUNK
    lane = lax.broadcasted_iota(jnp.int32, (C, LANES), 1)
    lo = lane < HEAD_DIM
    first_half = (lane & (HEAD_DIM - 1)) < HEAD_DIM // 2
    r = lax.broadcasted_iota(jnp.int32, (LANES, LANES), 0)
    c = lax.broadcasted_iota(jnp.int32, (LANES, LANES), 1)
    same_head = (r < HEAD_DIM) == (c < HEAD_DIM)
    avg = avg_ref[...]
    state_scr[...] = jnp.zeros(state_scr.shape, F32)

    def rope(x, cos, sin):
        swapped = jnp.where(first_half, pltpu.roll(x, LANES - HEAD_DIM // 2, axis=1),
                            pltpu.roll(x, HEAD_DIM // 2, axis=1))
        return x * cos + swapped * sin

    n_pairs = RET_HEADS // 2
    cols = [slice(p * LANES, (p + 1) * LANES) for p in range(n_pairs)]
    units = [(u, p) for u in range(RET_UNROLL) for p in range(n_pairs)]

    def chunks(n, carry):
        r0 = [pl.multiple_of((n * RET_UNROLL + u) * C, C) for u in range(RET_UNROLL)]
        cos = [cos_ref[pl.ds(r0[u], C), :] for u in range(RET_UNROLL)]
        sin = [sin_ref[pl.ds(r0[u], C), :] for u in range(RET_UNROLL)]
        q = [rope(q_ref[0, pl.ds(r0[u], C), cols[p]].astype(F32), cos[u], sin[u]) for u, p in units]
        k = [rope(k_ref[0, pl.ds(r0[u], C), cols[p]].astype(F32), cos[u], sin[u]) * (HEAD_DIM ** -0.5)
             for u, p in units]
        vb = [v_ref[0, pl.ds(r0[u], C), cols[p]] for u, p in units]
        kb = [k[i].astype(BF16) for i in range(len(units))]
        inner = [_dot_nt(jnp.where(lo if half == 0 else ~lo, q[i], 0.0).astype(BF16), kb[i])
                 * decay_ref[2 * units[i][1] + half] for i in range(len(units)) for half in range(2)]
        upd = [_dot_tn((k[i] * zeta_ref[units[i][1]]).astype(BF16), vb[i]) for i in range(len(units))]
        state = [state_scr[p] for p in range(n_pairs)]
        for u in range(RET_UNROLL):
            for p in range(n_pairs):
                prev = state[u * n_pairs + p]
                state.append(gammac_ref[p] * prev + jnp.where(same_head, upd[u * n_pairs + p], 0.0))
        cross = [_dot(q[i].astype(BF16), state[i].astype(BF16)) * xi_ref[units[i][1]] for i in range(len(units))]
        outs = [_dot(inner[j].astype(BF16), vb[j // 2]) for j in range(2 * len(units))]
        for p in range(n_pairs):
            state_scr[p] = state[RET_UNROLL * n_pairs + p]
        y = jnp.concatenate([jnp.where(lo, outs[2 * i], outs[2 * i + 1]) + cross[i] for i in range(len(units))],
                            axis=0)
        mu = _seg_mean(y, avg)
        d = y - mu
        var = _seg_mean(d * d, avg)
        yn = d * lax.rsqrt(var + EPS)
        for i, (u, p) in enumerate(units):
            gate = g_ref[0, pl.ds(r0[u], C), cols[p]].astype(F32)
            o_ref[0, pl.ds(r0[u], C), cols[p]] = (_silu(gate) * (yn[i * C:(i + 1) * C] * gain_ref[:, cols[p]])).astype(BF16)
        return carry

    lax.fori_loop(0, S // (C * RET_UNROLL), chunks, 0)


def _retention_tables(S):
    half = HEAD_DIM // 2
    inv_freq = ROPE_BASE ** (-np.arange(half, dtype=np.float64) / half)
    ang = np.arange(S, dtype=np.float64)[:, None] * inv_freq[None, :]
    cos, sin = np.cos(ang), np.sin(ang)
    cos_t = np.tile(cos, (1, 4))
    sin_t = np.tile(np.concatenate([-sin, sin], axis=1), (1, 2))
    C = RET_CHUNK
    H = RET_HEADS
    log_gamma = np.log1p(-np.power(2.0, -5.0 - np.arange(H, dtype=np.float64)))
    i = np.arange(C, dtype=np.float64)
    rel = i[:, None] - i[None, :]
    decay = np.where(rel >= 0, np.exp(np.maximum(rel, 0.0)[None] * log_gamma[:, None, None]), 0.0)
    xi = np.exp((i + 1.0)[:, None] * log_gamma[None, :])
    zeta = np.exp((C - 1.0 - i)[:, None] * log_gamma[None, :])
    gamma_c = np.exp(C * log_gamma)
    per_pair = lambda t: np.repeat(t.T.reshape(H // 2, 2, -1), HEAD_DIM, axis=1).transpose(0, 2, 1)
    gammac = np.repeat(gamma_c.reshape(H // 2, 2), HEAD_DIM, axis=1)[:, None, :]
    return tuple(jnp.asarray(t, F32) for t in (cos_t, sin_t, decay, per_pair(xi), per_pair(zeta), gammac))


def _retention(proj3, gain):
    B, S, _ = proj3.shape
    cos_t, sin_t, decay, xi, zeta, gammac = _retention_tables(S)
    avg = jnp.asarray(_block_avg(LANES), BF16)
    col = lambda c: pl.BlockSpec((1, S, RET_WIDTH), lambda b: (b, 0, c // RET_WIDTH))
    full = lambda a: pl.BlockSpec(a.shape, lambda b: (0,) * a.ndim)
    return pl.pallas_call(
        _retention_body,
        grid=(B,),
        in_specs=[col(COL_QR), col(COL_KR), col(COL_VR), col(COL_GR), full(cos_t), full(sin_t), full(decay),
                  full(xi), full(zeta), full(gammac), full(gain), full(avg)],
        out_specs=pl.BlockSpec((1, S, RET_WIDTH), lambda b: (b, 0, 0)),
        out_shape=jax.ShapeDtypeStruct((B, S, RET_WIDTH), BF16),
        scratch_shapes=[pltpu.VMEM((RET_HEADS // 2, LANES, LANES), F32)],
        compiler_params=_cparams(1),
        name="retention",
    )(proj3, proj3, proj3, proj3, cos_t, sin_t, decay, xi, zeta, gammac, gain, avg)


def _mem_prep_body(mem_ref, g_ref, wkv_ref, kgain_ref, avg_ref, k_ref, v_ref):
    hm = _rms_full(mem_ref[0], g_ref[...]).astype(BF16)
    kv = _dot(hm, wkv_ref[...])
    k = kv[:, :MEM_WIDTH]
    ms = _seg_mean(k * k, avg_ref[...])
    k_ref[0] = (k * lax.rsqrt(ms + EPS) * kgain_ref[...]).astype(BF16)
    v_ref[0] = kv[:, MEM_WIDTH:].astype(BF16)


def _post_body(x_ref, oa_ref, ob_ref, wout_ref, mk_ref, mv_ref, gx_ref, wq_ref, qgain_ref, avg_ref, wo_ref,
               gf_ref, wr_ref, br_ref, tri_ref, x2_ref, h_ref, route_ref, count_ref, wout_b, wq_b, wo_b):
    @pl.when((pl.program_id(0) == 0) & (pl.program_id(1) == 0))
    def _():
        wout_b[...] = wout_ref[...].astype(BF16)
        wq_b[...] = wq_ref[...].astype(BF16)
        wo_b[...] = wo_ref[...].astype(BF16)

    tm = x_ref.shape[1] // POST_CHAINS
    chains = range(POST_CHAINS)
    rows = [slice(c * tm, (c + 1) * tm) for c in chains]
    x1 = [x_ref[0, rows[c]] + _dot(oa_ref[0, rows[c]], wout_b[0:NSA_WIDTH, :])
          + _dot(ob_ref[0, rows[c]], wout_b[NSA_WIDTH:, :]) for c in chains]

    h = [_rms_full(x1[c], gx_ref[...]).astype(BF16) for c in chains]
    q = [_dot(h[c], wq_b[...]) for c in chains]
    ms = [_seg_mean(q[c] * q[c], avg_ref[...]) for c in chains]
    q = [q[c] * lax.rsqrt(ms[c] + EPS) * qgain_ref[...] * (HEAD_DIM ** -0.5) for c in chains]
    lane = lax.broadcasted_iota(jnp.int32, (tm, LANES), 1)
    lo = lane < HEAD_DIM
    heads = [(c, p, half) for c in chains for p in range(MEM_HEADS // 2) for half in range(2)]
    s = [_dot_nt(jnp.where(lo if half == 0 else ~lo, q[c][:, p * LANES:(p + 1) * LANES], 0.0).astype(BF16),
                 mk_ref[0, :, p * LANES:(p + 1) * LANES]) for c, p, half in heads]
    e = [jnp.exp(s[i] - jnp.max(s[i], axis=-1, keepdims=True)) for i in range(len(heads))]
    pr = [(e[i] / jnp.sum(e[i], axis=-1, keepdims=True)).astype(BF16) for i in range(len(heads))]
    outs = [_dot(pr[i], mv_ref[0, :, heads[i][1] * LANES:(heads[i][1] + 1) * LANES]) for i in range(len(heads))]
    per_chain = MEM_HEADS
    o = [jnp.concatenate([jnp.where(lo, outs[c * per_chain + 2 * p], outs[c * per_chain + 2 * p + 1])
                          for p in range(MEM_HEADS // 2)], axis=1).astype(BF16) for c in chains]
    x2 = [x1[c] + _dot(o[c], wo_b[...]) for c in chains]
    for c in chains:
        x2_ref[0, rows[c]] = x2[c]

    hf = [_rms_full(x2[c], gf_ref[...]).astype(BF16) for c in chains]
    for c in chains:
        h_ref[0, rows[c]] = hf[c]
    logits = [_dot(hf[c], wr_ref[...]) + br_ref[...] for c in chains]
    lane_f = lane.astype(F32)
    big = float(LANES)
    picks = []
    for c in chains:
        gl = jnp.where(lane < N_GROUPS, logits[c], BELOW_ALL)
        gmax = jnp.max(gl, axis=-1, keepdims=True)
        grp = jnp.min(jnp.where(gl == gmax, lane_f, big), axis=-1, keepdims=True)
        g_w = 1.0 / jnp.sum(jnp.where(lane < N_GROUPS, jnp.exp(gl - gmax), 0.0), axis=-1, keepdims=True)
        e_lo = N_GROUPS + grp * EXPERTS_PER_GROUP
        el = jnp.where((lane_f >= e_lo) & (lane_f < e_lo + EXPERTS_PER_GROUP), logits[c], BELOW_ALL)
        v0 = jnp.max(el, axis=-1, keepdims=True)
        i0 = jnp.min(jnp.where(el == v0, lane_f, big), axis=-1, keepdims=True)
        el = jnp.where(lane_f == i0, BELOW_ALL, el)
        v1 = jnp.max(el, axis=-1, keepdims=True)
        i1 = jnp.min(jnp.where(el == v1, lane_f, big), axis=-1, keepdims=True)
        e1 = jnp.exp(v1 - v0)
        picks.append((i0 - N_GROUPS, i1 - N_GROUPS, g_w / (1.0 + e1), g_w * e1 / (1.0 + e1)))

    hot = [[lane_f == picks[c][s] for s in range(2)] for c in chains]
    both = jnp.concatenate([jnp.where(hot[c][0], 1.0, 0.0) + jnp.where(hot[c][1], 1.0, 0.0) for c in chains], axis=0)
    before = _dot(tri_ref[...], both.astype(BF16))
    count_ref[0] = jnp.broadcast_to(jnp.sum(both, axis=0, keepdims=True), count_ref.shape[1:])
    for c in chains:
        e0, e1, w0, w1 = picks[c]
        r0 = jnp.sum(jnp.where(hot[c][0], before[rows[c]], 0.0), axis=-1, keepdims=True)
        r1 = jnp.sum(jnp.where(hot[c][1], before[rows[c]], 0.0), axis=-1, keepdims=True)
        cols = (e0, e1, w0, w1, r0, r1)
        route = jnp.zeros((tm, LANES), F32)
        for k in range(len(cols)):
            route = jnp.where(lane == k, cols[k], route)
        route_ref[0, rows[c]] = route


def _post(x, oa, ob, wout, mk, mv, gx, wq, qgain, wo, gf, wr, br):
    B, S, _ = x.shape
    tm = MOE_TM
    n_s = S // tm
    avg = jnp.asarray(_block_avg(MEM_WIDTH), BF16)
    tri = jnp.asarray(np.tril(np.ones((tm, tm), np.float32), -1), BF16)
    full = lambda a: pl.BlockSpec(a.shape, lambda b, i: (0,) * a.ndim)
    per_b = lambda a: pl.BlockSpec((1,) + a.shape[1:], lambda b, i: (b,) + (0,) * (a.ndim - 1))
    tile = lambda w: pl.BlockSpec((1, tm, w), lambda b, i: (b, i, 0))
    return pl.pallas_call(
        _post_body,
        grid=(B, n_s),
        in_specs=[tile(D_MODEL), tile(NSA_WIDTH), tile(RET_WIDTH), full(wout), per_b(mk), per_b(mv), full(gx),
                  full(wq), full(qgain), full(avg), full(wo), full(gf), full(wr), full(br), full(tri)],
        out_specs=[tile(D_MODEL), tile(D_MODEL), tile(LANES),
                   pl.BlockSpec((1, 8, LANES), lambda b, i: (b * n_s + i, 0, 0))],
        out_shape=[jax.ShapeDtypeStruct((B, S, D_MODEL), F32), jax.ShapeDtypeStruct((B, S, D_MODEL), BF16),
                   jax.ShapeDtypeStruct((B, S, LANES), F32), jax.ShapeDtypeStruct((B * n_s, 8, LANES), F32)],
        scratch_shapes=[pltpu.VMEM(wout.shape, BF16), pltpu.VMEM(wq.shape, BF16), pltpu.VMEM(wo.shape, BF16)],
        compiler_params=_cparams(2),
        name="post_mixer",
    )(x, oa, ob, wout, mk, mv, gx, wq, qgain, avg, wo, gf, wr, br, tri)


def _row_copy(src, dst, sem):
    return pltpu.make_async_copy(src, dst, sem)


def _run_pieces(n, max_piece, fn):
    b = RUN_ALIGN
    while b <= max_piece:
        pl.when((n & b) != 0)(functools.partial(fn, n & (-2 * b), b))
        b *= 2


def _move_groups(i, gmap_ref, n_loc, copy):
    n_groups = n_loc // RUN_ALIGN
    for j in range(n_groups):
        glob = pl.multiple_of(gmap_ref[i * n_groups + j], RUN_ALIGN)
        copy(pl.ds(j * RUN_ALIGN, RUN_ALIGN), pl.ds(glob, RUN_ALIGN)).start()


def _local_positions(route, loff_row):
    lane = lax.broadcasted_iota(jnp.int32, route.shape, 1).astype(F32)
    pos = []
    for s in range(2):
        base = jnp.sum(jnp.where(lane == route[:, s:s + 1], loff_row, 0.0), axis=-1, keepdims=True)
        pos.append(base + route[:, 4 + s:5 + s])
    return pos


def _scatter_body(gmap_ref, tstart_ref, tlen_ref, nact_ref,
                  h_ref, route_ref, lofff_ref, xs_ref, xloc, zbuf, sems):
    i = pl.program_id(0)
    tm = h_ref.shape[0]
    n_loc = xloc.shape[1]
    slot = i & 1
    sem = sems.at[0]

    def tails(start):
        def per_expert(e, carry):
            n = tlen_ref[e]
            st = tstart_ref[e]

            def piece(off, size):
                c = _row_copy(zbuf.at[pl.ds(0, size)], xs_ref.at[pl.ds(pl.multiple_of(st + off, RUN_ALIGN), size)], sem)
                c.start() if start else c.wait()

            _run_pieces(n, MOE_RB // 2, piece)
            return carry

        lax.fori_loop(0, N_EXPERTS, per_expert, 0)

    def unused(start):
        rows = zbuf.shape[0]

        def per_unit(u, carry):
            c = _row_copy(zbuf, xs_ref.at[pl.ds(pl.multiple_of(u * rows, rows), rows)], sem)
            c.start() if start else c.wait()
            return carry

        lax.fori_loop(nact_ref[0] * (MOE_RB // rows), xs_ref.shape[0] // rows, per_unit, 0)

    @pl.when(i == 0)
    def _():
        zbuf[...] = jnp.zeros(zbuf.shape, zbuf.dtype)
        tails(True)
        unused(True)
        tails(False)
        unused(False)

    pos = _local_positions(route_ref[...], lofff_ref[0, 0:1, :])
    col = lax.broadcasted_iota(jnp.int32, (tm, n_loc), 1).astype(F32)
    perm_t = jnp.where((col == pos[0]) | (col == pos[1]), 1.0, 0.0).astype(BF16)
    xloc[slot] = _dot_tn(perm_t, h_ref[...]).astype(BF16)

    def wait_slot(s):
        _row_copy(xloc.at[s], xs_ref.at[pl.ds(0, n_loc)], sems.at[s]).wait()

    pl.when(i > 0)(lambda: wait_slot(1 - slot))
    _move_groups(i, gmap_ref, n_loc, lambda loc, glob: _row_copy(xloc.at[slot, loc], xs_ref.at[glob], sems.at[slot]))
    pl.when(i == pl.num_programs(0) - 1)(lambda: wait_slot(slot))


def _scatter_rows(tables, h2d, route2d, loff_f, n_rows):
    T = h2d.shape[0]
    tm = MOE_TM
    n_loc = MOE_NLOC
    tile = lambda w: pl.BlockSpec((tm, w), lambda i, *_: (i, 0))
    grid_spec = pltpu.PrefetchScalarGridSpec(
        num_scalar_prefetch=4,
        grid=(T // tm,),
        in_specs=[tile(D_MODEL), tile(LANES), pl.BlockSpec((1, 8, LANES), lambda i, *_: (i, 0, 0))],
        out_specs=pl.BlockSpec(memory_space=pl.ANY),
        scratch_shapes=[pltpu.VMEM((2, n_loc, D_MODEL), BF16), pltpu.VMEM((MOE_RB // 2, D_MODEL), BF16),
                        pltpu.SemaphoreType.DMA((2,))],
    )
    return pl.pallas_call(
        _scatter_body,
        grid_spec=grid_spec,
        out_shape=jax.ShapeDtypeStruct((n_rows, D_MODEL), BF16),
        compiler_params=_cparams(1),
        name="moe_scatter",
    )(*tables, h2d, route2d, loff_f)


def _expert_body(blk0_ref, nblk_ref, n_act_ref, xs_ref, wg_ref, wu_ref, wd_ref, ys_ref,
                 wg_b, wu_b, wd_b, xbuf, ybuf, sem_in, sem_out):
    e = pl.program_id(0)
    n_slots, rb = xbuf.shape[0], xbuf.shape[1]
    ahead = n_slots - 2
    n_act = n_act_ref[0]
    b0 = blk0_ref[e]

    def rows(g):
        return pl.ds(pl.multiple_of(g * rb, rb), rb)

    def x_copy(g, slot):
        return _row_copy(xs_ref.at[rows(g)], xbuf.at[slot], sem_in.at[slot])

    def y_copy(g, slot):
        return _row_copy(ybuf.at[slot], ys_ref.at[rows(g)], sem_out.at[slot])

    @pl.when(e == 0)
    def _():
        for k in range(ahead):
            pl.when(k < n_act)(lambda k=k: x_copy(k, k).start())

    wg_b[...] = wg_ref[0].astype(BF16)
    wu_b[...] = wu_ref[0].astype(BF16)
    wd_b[...] = wd_ref[0].astype(BF16)

    def blocks(g0, count):
        gs = [g0 + c for c in range(count)]
        slots = [g & (n_slots - 1) for g in gs]
        for c in range(count):
            x_copy(gs[c], slots[c]).wait()
        for c in range(count):
            nxt = gs[c] + ahead
            pl.when(nxt < n_act)(lambda nxt=nxt: x_copy(nxt, nxt & (n_slots - 1)).start())
        x = [xbuf[slots[c]] for c in range(count)]
        a = [_dot(x[c], wg_b[...]) for c in range(count)]
        b = [_dot(x[c], wu_b[...]) for c in range(count)]
        h = [(_silu(a[c]) * b[c]).astype(BF16) for c in range(count)]
        y = [_dot(h[c], wd_b[...]).astype(BF16) for c in range(count)]
        for c in range(count):
            pl.when(gs[c] >= n_slots)(lambda c=c: y_copy(gs[c] - n_slots, slots[c]).wait())
        for c in range(count):
            ybuf[slots[c]] = y[c]
            y_copy(gs[c], slots[c]).start()

    nb = nblk_ref[e]

    def pair(j, carry):
        blocks(b0 + 2 * j, 2)
        return carry

    lax.fori_loop(0, lax.shift_right_logical(nb, 1), pair, 0)
    pl.when((nb & 1) == 1)(lambda: blocks(b0 + nb - 1, 1))

    @pl.when(e == pl.num_programs(0) - 1)
    def _():
        for k in range(1, n_slots + 1):
            pl.when(n_act >= k)(lambda k=k: y_copy(n_act - k, (n_act - k) & (n_slots - 1)).wait())
        ybuf[0] = jnp.zeros(ybuf.shape[1:], ybuf.dtype)
        n_blocks = ys_ref.shape[0] // rb

        def fill(start):
            def per_block(g, carry):
                c = y_copy(g, 0)
                c.start() if start else c.wait()
                return carry

            lax.fori_loop(n_act, n_blocks, per_block, 0)

        fill(True)
        fill(False)


def _experts(blk0, nblk, n_act, xs, n_rows, wg, wu, wd):
    rb = MOE_RB
    weight = lambda a: pl.BlockSpec((1,) + a.shape[1:], lambda e, *_: (e, 0, 0))
    grid_spec = pltpu.PrefetchScalarGridSpec(
        num_scalar_prefetch=3,
        grid=(N_EXPERTS,),
        in_specs=[pl.BlockSpec(memory_space=pl.ANY), weight(wg), weight(wu), weight(wd)],
        out_specs=pl.BlockSpec(memory_space=pl.ANY),
        scratch_shapes=[pltpu.VMEM((D_MODEL, EXPERT_FF), BF16), pltpu.VMEM((D_MODEL, EXPERT_FF), BF16),
                        pltpu.VMEM((EXPERT_FF, D_MODEL), BF16), pltpu.VMEM((MOE_SLOTS, rb, D_MODEL), BF16),
                        pltpu.VMEM((MOE_SLOTS, rb, D_MODEL), BF16), pltpu.SemaphoreType.DMA((MOE_SLOTS,)),
                        pltpu.SemaphoreType.DMA((MOE_SLOTS,))],
    )
    return pl.pallas_call(
        _expert_body,
        grid_spec=grid_spec,
        out_shape=jax.ShapeDtypeStruct((n_rows, D_MODEL), BF16),
        compiler_params=_cparams(1),
        name="moe_experts",
    )(blk0, nblk, n_act, xs, wg, wu, wd)


def _combine_body(gmap_ref, x_ref, route_ref, lofff_ref, ys_ref, o_ref, yloc, sems):
    i = pl.program_id(0)
    tm = x_ref.shape[0]
    n_loc = yloc.shape[1]
    slot = i & 1

    def fetch(tile, s):
        _move_groups(tile, gmap_ref, n_loc, lambda loc, glob: _row_copy(ys_ref.at[glob], yloc.at[s, loc], sems.at[s]))

    pl.when(i == 0)(lambda: fetch(i, slot))
    _row_copy(ys_ref.at[pl.ds(0, n_loc)], yloc.at[slot], sems.at[slot]).wait()
    pl.when(i + 1 < pl.num_programs(0))(lambda: fetch(i + 1, 1 - slot))

    route = route_ref[...]
    pos = _local_positions(route, lofff_ref[0, 0:1, :])
    col = lax.broadcasted_iota(jnp.int32, (tm, n_loc), 1).astype(F32)
    perm_w = jnp.where(col == pos[0], route[:, 2:3], jnp.where(col == pos[1], route[:, 3:4], 0.0)).astype(BF16)
    o_ref[...] = x_ref[...] + _dot(perm_w, yloc[slot])


def _combine(tables, x2d, route2d, loff_f, ys):
    T = x2d.shape[0]
    tm = MOE_TM
    n_loc = MOE_NLOC
    tile = lambda w: pl.BlockSpec((tm, w), lambda i, *_: (i, 0))
    grid_spec = pltpu.PrefetchScalarGridSpec(
        num_scalar_prefetch=1,
        grid=(T // tm,),
        in_specs=[tile(D_MODEL), tile(LANES), pl.BlockSpec((1, 8, LANES), lambda i, *_: (i, 0, 0)),
                  pl.BlockSpec(memory_space=pl.ANY)],
        out_specs=tile(D_MODEL),
        scratch_shapes=[pltpu.VMEM((2, n_loc, D_MODEL), BF16), pltpu.SemaphoreType.DMA((2,))],
    )
    return pl.pallas_call(
        _combine_body,
        grid_spec=grid_spec,
        out_shape=jax.ShapeDtypeStruct((T, D_MODEL), F32),
        compiler_params=_cparams(1),
        name="moe_combine",
    )(*tables, x2d, route2d, loff_f, ys)


def _compress_weights(pos, w2):
    eye = jnp.eye(NSA_GROUPS, dtype=F32)
    w2b = jnp.einsum('hd,gk->ghkd', w2, eye).reshape(NSA_GROUPS * CMP_HIDDEN, LANES)
    posb = jnp.tile(pos, (1, NSA_GROUPS)).reshape(2, 1, CMP_STRIDE * LANES)
    return posb, w2b.astype(BF16)


def _dup2(g):
    return jnp.tile(g.reshape(1, HEAD_DIM), (1, 2))


def kernel(x, mem, mix_norm, w_in, nsa_q_norm, nsa_kcmp_norm, nsa_ksel_norm, nsa_kwin_norm, cmp_pos_k, cmp_pos_v, cmp_k_w1, cmp_k_w2, cmp_v_w1, cmp_v_w2, nsa_out_norm, ret_out_norm, w_out, mem_x_norm, mem_kv_norm, mem_wq, mem_wkv, mem_q_norm, mem_k_norm, mem_wo, ffn_norm, router_group_w, router_group_b, router_expert_w, router_expert_b, exp_w_gate, exp_w_up, exp_w_down):
    B, S, D = x.shape
    T = B * S
    depth = mix_norm.shape[0]
    for l in range(depth):
        proj = _proj(x.reshape(T, D), mix_norm[l].reshape(1, D), w_in[l].T).reshape(B, S, PROJ_PAD)
        pk, w2k = _compress_weights(cmp_pos_k[l], cmp_k_w2[l])
        pv, w2v = _compress_weights(cmp_pos_v[l], cmp_v_w2[l])
        gains = jnp.stack([_dup2(nsa_kcmp_norm[l]), _dup2(nsa_ksel_norm[l]), _dup2(nsa_kwin_norm[l])])
        mem_args = (mem, mem_kv_norm[l].reshape(1, D), mem_wkv[l].astype(BF16),
                    jnp.tile(mem_k_norm[l].reshape(1, HEAD_DIM), (1, MEM_HEADS)))
        kcmp, vcmp, ks, vs, kw, vw, mk, mv = _nsa_prep(proj, jnp.stack([pk, pv]), cmp_k_w1[l], cmp_v_w1[l],
                                                       jnp.stack([w2k, w2v]), gains, mem_args)
        o_a = _nsa_attn(proj, kcmp, vcmp, ks, vs, kw, vw,
                        jnp.tile(nsa_q_norm[l].reshape(1, HEAD_DIM), (1, NSA_HEADS)), gains,
                        nsa_out_norm[l].reshape(1, NSA_WIDTH))
        o_b = _retention(proj, ret_out_norm[l].reshape(1, RET_WIDTH))
        w_r = jnp.concatenate([router_group_w[l],
                               router_expert_w[l].transpose(1, 0, 2).reshape(D, N_EXPERTS),
                               jnp.zeros((D, LANES - N_GROUPS - N_EXPERTS), F32)], axis=1).astype(BF16)
        b_r = jnp.concatenate([router_group_b[l], router_expert_b[l].reshape(N_EXPERTS),
                               jnp.zeros((LANES - N_GROUPS - N_EXPERTS,), F32)]).reshape(1, LANES)
        x2, hf, route, counts = _post(
            x, o_a, o_b, w_out[l], mk, mv, mem_x_norm[l].reshape(1, D), mem_wq[l],
            jnp.tile(mem_q_norm[l].reshape(1, HEAD_DIM), (1, MEM_HEADS)), mem_wo[l],
            ffn_norm[l].reshape(1, D), w_r, b_r)
        route2d = route.reshape(T, LANES)
        n_tiles = T // MOE_TM
        cnt = counts[:, 0, :N_EXPERTS].astype(jnp.int32)
        cnt = (cnt + RUN_ALIGN - 1) // RUN_ALIGN * RUN_ALIGN
        loff = jnp.cumsum(cnt, axis=1) - cnt
        total = jnp.sum(cnt, axis=0)
        padded = (total + MOE_RB - 1) // MOE_RB * MOE_RB
        pend = jnp.cumsum(padded)
        pstart = pend - padded
        goff = pstart[None, :] + jnp.cumsum(cnt, axis=0) - cnt
        n_rows = 2 * T + n_tiles * N_EXPERTS * RUN_ALIGN + N_EXPERTS * MOE_RB
        n_act = (pend[-1:] // MOE_RB).astype(jnp.int32)
        loff_f = jnp.broadcast_to(jnp.pad(loff.astype(F32), ((0, 0), (0, LANES - N_EXPERTS)))[:, None, :],
                                  (n_tiles, 8, LANES))
        grp_row = jnp.arange(MOE_NLOC // RUN_ALIGN, dtype=jnp.int32) * RUN_ALIGN
        inside = ((loff[:, None, :] <= grp_row[None, :, None])
                  & (grp_row[None, :, None] < (loff + cnt)[:, None, :])).astype(jnp.int32)
        shift = jnp.sum(inside * (goff - loff)[:, None, :], axis=2)
        used = jnp.sum(inside, axis=2) > 0
        gmap_scatter = jnp.where(used, shift + grp_row[None, :], n_rows + grp_row[None, :]).reshape(-1)
        gmap_gather = jnp.where(used, shift + grp_row[None, :], 0).reshape(-1)
        xs = _scatter_rows((gmap_scatter, pstart + total, padded - total, n_act), hf.reshape(T, D), route2d, loff_f,
                           n_rows + MOE_NLOC)
        ys = _experts(pstart // MOE_RB, padded // MOE_RB, n_act, xs, n_rows,
                      exp_w_gate[l], exp_w_up[l], exp_w_down[l])
        x = _combine((gmap_gather,), x2.reshape(T, D), route2d, loff_f, ys).reshape(B, S, D)
    return x
```
